```python
import jax, jax.numpy as jnp
from jax import lax
import numpy as np

D_MODEL = 2048
BATCH = 8
SEQ = 2048
DEPTH = 2

N_MIXERS = 2
N_META = 16
D_FF = 5632
EPS = 1e-6
GLA_HEADS = 4
GLA_DK = D_MODEL // 2
GLA_DV = D_MODEL
GLA_HEAD_K = GLA_DK // GLA_HEADS
GLA_HEAD_V = GLA_DV // GLA_HEADS
GLA_GATE_RANK = 16
GLA_GATE_NORM = 16.0
GLA_CHUNK = 64
GLA_IN_W = GLA_DK + GLA_DK + GLA_DV + GLA_GATE_RANK + GLA_DV
POOL_WINDOWS = (2, 4, 8, 16)
POOL_GROUPS = 4
POOL_GROUP_W = D_MODEL // POOL_GROUPS
N_GLA_LAYERS = (DEPTH + 1) // 2
N_POOL_LAYERS = DEPTH // 2

kernel_name = 'hybrid_gla_pool_macaron'


def rms_norm(x, g):
    xf = x.astype(jnp.float32)
    y = xf * lax.rsqrt(jnp.mean(xf * xf, axis=-1, keepdims=True) + EPS)
    return (y * g.astype(jnp.float32)).astype(x.dtype)


def ffn_half(x, g, w_gate, w_up, w_down):
    h = rms_norm(x, g)
    return x + 0.5 * ((jax.nn.silu(h @ w_gate) * (h @ w_up)) @ w_down)


def gla_chunked(q, k, v, lg):
    B, H, T, dk = q.shape
    dv = v.shape[-1]
    C = GLA_CHUNK
    n = T // C

    def to_chunks(a):
        return jnp.moveaxis(a.reshape(B, H, n, C, a.shape[-1]), 2, 0)

    causal = jnp.tril(jnp.ones((C, C), dtype=bool))

    def step(S, inp):
        qc, kc, vc, gc = inp
        b = jnp.cumsum(gc, axis=2)
        b_last = b[:, :, -1:, :]
        o_inter = jnp.einsum('bhik,bhkv->bhiv', qc * jnp.exp(b), S)
        diff = b[:, :, :, None, :] - b[:, :, None, :, :]
        decay = jnp.exp(jnp.where(causal[:, :, None], diff, -jnp.inf))
        A = jnp.einsum('bhijk,bhjk->bhij', qc[:, :, :, None, :] * decay, kc)
        o = o_inter + jnp.einsum('bhij,bhjv->bhiv', A, vc)
        S = jnp.exp(b_last[:, :, 0, :])[..., None] * S + jnp.einsum('bhjk,bhjv->bhkv', kc * jnp.exp(b_last - b), vc)
        return S, o

    S0 = jnp.zeros((B, H, dk, dv), jnp.float32)
    _, o = lax.scan(step, S0, (to_chunks(q), to_chunks(k), to_chunks(v), to_chunks(lg)))
    return jnp.moveaxis(o, 0, 2).reshape(B, H, T, dv)


def gla_mixer(h, w_in, w_lr, b_lr, head_norm, w_out):
    B, L, _ = h.shape
    proj = h @ w_in
    q, k, v, lr, r = jnp.split(proj, [GLA_DK, 2 * GLA_DK, 2 * GLA_DK + GLA_DV, 2 * GLA_DK + GLA_DV + GLA_GATE_RANK], axis=-1)
    lg = jax.nn.log_sigmoid((lr @ w_lr + b_lr).astype(jnp.float32)) / GLA_GATE_NORM

    def heads(a, d):
        return a.reshape(B, L, GLA_HEADS, d).transpose(0, 2, 1, 3).astype(jnp.float32)

    q = heads(q, GLA_HEAD_K) * (GLA_HEAD_K ** -0.5)
    k = heads(k, GLA_HEAD_K)
    v = heads(v, GLA_HEAD_V)
    lg = heads(lg, GLA_HEAD_K)
    pad = (-N_META) % GLA_CHUNK
    padf = lambda a: jnp.pad(a, ((0, 0), (0, 0), (pad, 0), (0, 0)))
    o = gla_chunked(padf(q), padf(k), padf(v), padf(lg))[:, :, pad:, :]
    o = o * lax.rsqrt(jnp.mean(o * o, axis=-1, keepdims=True) + EPS) * head_norm.astype(jnp.float32)
    o = o.transpose(0, 2, 1, 3).reshape(B, L, GLA_DV).astype(h.dtype)
    return (o * jax.nn.silu(r)) @ w_out


def pool_mixer(h, w, b, scale):
    B, L, D = h.shape
    hf = h.astype(jnp.float32).reshape(B, L, POOL_GROUPS, POOL_GROUP_W)
    cs = jnp.cumsum(hf, axis=1)
    t = jnp.arange(L)
    outs = []
    for g, win in enumerate(POOL_WINDOWS):
        csg = cs[:, :, g]
        prev = jnp.pad(csg, ((0, 0), (win, 0), (0, 0)))[:, :L]
        cnt = jnp.minimum(t + 1, win).astype(jnp.float32)[:, None]
        outs.append((csg - prev) / cnt - hf[:, :, g])
    pooled = jnp.stack(outs, axis=2).astype(h.dtype)
    y = jnp.einsum('blgc,gcd->blgd', pooled, w) + b
    return y.reshape(B, L, D) * scale


def _fwd_setup_inputs(seed: int = 0) -> dict:
    key = jax.random.key(seed)
    ks = jax.random.split(key, 20)
    f32 = jnp.float32
    nrm = lambda k, s, sc: jax.random.normal(k, s, f32) * sc
    return {
        'x': nrm(ks[0], (BATCH, SEQ, D_MODEL), 1.0),
        'meta': nrm(ks[1], (N_META, D_MODEL), 1.0),
        'ffn_norm': 1.0 + nrm(ks[2], (DEPTH, 2, D_MODEL), 0.02),
        'ffn_w_gate': nrm(ks[3], (DEPTH, 2, D_MODEL, D_FF), D_MODEL ** -0.5),
        'ffn_w_up': nrm(ks[4], (DEPTH, 2, D_MODEL, D_FF), D_MODEL ** -0.5),
        'ffn_w_down': nrm(ks[5], (DEPTH, 2, D_FF, D_MODEL), D_FF ** -0.5),
        'gla_norm': 1.0 + nrm(ks[6], (N_GLA_LAYERS, D_MODEL), 0.02),
        'gla_w_in': nrm(ks[7], (N_GLA_LAYERS, D_MODEL, GLA_IN_W), D_MODEL ** -0.5),
        'gla_w_lr': nrm(ks[8], (N_GLA_LAYERS, GLA_GATE_RANK, GLA_DK), GLA_GATE_RANK ** -0.5),
        'gla_b_lr': nrm(ks[9], (N_GLA_LAYERS, GLA_DK), 0.01),
        'gla_head_norm': 1.0 + nrm(ks[10], (N_GLA_LAYERS, GLA_HEAD_V), 0.02),
        'gla_w_out': nrm(ks[11], (N_GLA_LAYERS, GLA_DV, D_MODEL), GLA_DV ** -0.5),
        'pool_norm': 1.0 + nrm(ks[12], (N_POOL_LAYERS, D_MODEL), 0.02),
        'pool_w': nrm(ks[13], (N_POOL_LAYERS, POOL_GROUPS, POOL_GROUP_W, POOL_GROUP_W), POOL_GROUP_W ** -0.5),
        'pool_b': nrm(ks[14], (N_POOL_LAYERS, POOL_GROUPS, POOL_GROUP_W), 0.01),
        'pool_scale': 1.0 + nrm(ks[15], (N_POOL_LAYERS, D_MODEL), 0.02),
        'final_norm': 1.0 + nrm(ks[16], (D_MODEL,), 0.02),
    }


def _fwd_reference(x, meta, ffn_norm, ffn_w_gate, ffn_w_up, ffn_w_down, gla_norm, gla_w_in, gla_w_lr, gla_b_lr,
              gla_head_norm, gla_w_out, pool_norm, pool_w, pool_b, pool_scale, final_norm):
    B = x.shape[0]
    m = jnp.broadcast_to(meta.astype(x.dtype)[None], (B, N_META, D_MODEL))
    x = jnp.concatenate([m, x], axis=1)
    for i in range(DEPTH):
        x = ffn_half(x, ffn_norm[i, 0], ffn_w_gate[i, 0], ffn_w_up[i, 0], ffn_w_down[i, 0])
        j = i // N_MIXERS
        if i % N_MIXERS == 0:
            x = x + gla_mixer(rms_norm(x, gla_norm[j]), gla_w_in[j], gla_w_lr[j], gla_b_lr[j], gla_head_norm[j], gla_w_out[j])
        else:
            x = x + pool_mixer(rms_norm(x, pool_norm[j]), pool_w[j], pool_b[j], pool_scale[j])
        x = ffn_half(x, ffn_norm[i, 1], ffn_w_gate[i, 1], ffn_w_up[i, 1], ffn_w_down[i, 1])
    return rms_norm(x, final_norm)[:, N_META:]


import jax as _jax
import jax.numpy as _jnp

TWIN_FORMAT = 'train_step'
FWD_PARAMS = ['x', 'meta', 'ffn_norm', 'ffn_w_gate', 'ffn_w_up', 'ffn_w_down', 'gla_norm', 'gla_w_in', 'gla_w_lr', 'gla_b_lr', 'gla_head_norm', 'gla_w_out', 'pool_norm', 'pool_w', 'pool_b', 'pool_scale', 'final_norm']
TWIN_WEIGHTS = ['meta', 'ffn_norm', 'ffn_w_gate', 'ffn_w_up', 'ffn_w_down', 'gla_norm', 'gla_w_in', 'gla_w_lr', 'gla_b_lr', 'gla_head_norm', 'gla_w_out', 'pool_norm', 'pool_w', 'pool_b', 'pool_scale', 'final_norm']
TWIN_DIFF_INPUT = 'x'
TWIN_INPUTS = ['x', 'meta', 'ffn_norm', 'ffn_w_gate', 'ffn_w_up', 'ffn_w_down', 'gla_norm', 'gla_w_in', 'gla_w_lr', 'gla_b_lr', 'gla_head_norm', 'gla_w_out', 'pool_norm', 'pool_w', 'pool_b', 'pool_scale', 'final_norm', 'loss_target', 'm_meta', 'm_ffn_norm', 'm_ffn_w_gate', 'm_ffn_w_up', 'm_ffn_w_down', 'm_gla_norm', 'm_gla_w_in', 'm_gla_w_lr', 'm_gla_b_lr', 'm_gla_head_norm', 'm_gla_w_out', 'm_pool_norm', 'm_pool_w', 'm_pool_b', 'm_pool_scale', 'm_final_norm', 'v_meta', 'v_ffn_norm', 'v_ffn_w_gate', 'v_ffn_w_up', 'v_ffn_w_down', 'v_gla_norm', 'v_gla_w_in', 'v_gla_w_lr', 'v_gla_b_lr', 'v_gla_head_norm', 'v_gla_w_out', 'v_pool_norm', 'v_pool_w', 'v_pool_b', 'v_pool_scale', 'v_final_norm']
TWIN_OUTPUTS = ['loss', 'grad_x', 'grad_meta', 'grad_ffn_norm', 'grad_ffn_w_gate', 'grad_ffn_w_up', 'grad_ffn_w_down', 'grad_gla_norm', 'grad_gla_w_in', 'grad_gla_w_lr', 'grad_gla_b_lr', 'grad_gla_head_norm', 'grad_gla_w_out', 'grad_pool_norm', 'grad_pool_w', 'grad_pool_b', 'grad_pool_scale', 'grad_final_norm', 'delta_meta', 'delta_ffn_norm', 'delta_ffn_w_gate', 'delta_ffn_w_up', 'delta_ffn_w_down', 'delta_gla_norm', 'delta_gla_w_in', 'delta_gla_w_lr', 'delta_gla_b_lr', 'delta_gla_head_norm', 'delta_gla_w_out', 'delta_pool_norm', 'delta_pool_w', 'delta_pool_b', 'delta_pool_scale', 'delta_final_norm', 'new_m_meta', 'new_m_ffn_norm', 'new_m_ffn_w_gate', 'new_m_ffn_w_up', 'new_m_ffn_w_down', 'new_m_gla_norm', 'new_m_gla_w_in', 'new_m_gla_w_lr', 'new_m_gla_b_lr', 'new_m_gla_head_norm', 'new_m_gla_w_out', 'new_m_pool_norm', 'new_m_pool_w', 'new_m_pool_b', 'new_m_pool_scale', 'new_m_final_norm', 'new_v_meta', 'new_v_ffn_norm', 'new_v_ffn_w_gate', 'new_v_ffn_w_up', 'new_v_ffn_w_down', 'new_v_gla_norm', 'new_v_gla_w_in', 'new_v_gla_w_lr', 'new_v_gla_b_lr', 'new_v_gla_head_norm', 'new_v_gla_w_out', 'new_v_pool_norm', 'new_v_pool_w', 'new_v_pool_b', 'new_v_pool_scale', 'new_v_final_norm']
TWIN_LEAF_KINDS = {'loss': 'loss', 'grad_x': 'grad_x', 'grad_meta': 'grad_w', 'grad_ffn_norm': 'grad_w', 'grad_ffn_w_gate': 'grad_w', 'grad_ffn_w_up': 'grad_w', 'grad_ffn_w_down': 'grad_w', 'grad_gla_norm': 'grad_w', 'grad_gla_w_in': 'grad_w', 'grad_gla_w_lr': 'grad_w', 'grad_gla_b_lr': 'grad_w', 'grad_gla_head_norm': 'grad_w', 'grad_gla_w_out': 'grad_w', 'grad_pool_norm': 'grad_w', 'grad_pool_w': 'grad_w', 'grad_pool_b': 'grad_w', 'grad_pool_scale': 'grad_w', 'grad_final_norm': 'grad_w', 'delta_meta': 'delta_w', 'delta_ffn_norm': 'delta_w', 'delta_ffn_w_gate': 'delta_w', 'delta_ffn_w_up': 'delta_w', 'delta_ffn_w_down': 'delta_w', 'delta_gla_norm': 'delta_w', 'delta_gla_w_in': 'delta_w', 'delta_gla_w_lr': 'delta_w', 'delta_gla_b_lr': 'delta_w', 'delta_gla_head_norm': 'delta_w', 'delta_gla_w_out': 'delta_w', 'delta_pool_norm': 'delta_w', 'delta_pool_w': 'delta_w', 'delta_pool_b': 'delta_w', 'delta_pool_scale': 'delta_w', 'delta_final_norm': 'delta_w', 'new_m_meta': 'new_m', 'new_m_ffn_norm': 'new_m', 'new_m_ffn_w_gate': 'new_m', 'new_m_ffn_w_up': 'new_m', 'new_m_ffn_w_down': 'new_m', 'new_m_gla_norm': 'new_m', 'new_m_gla_w_in': 'new_m', 'new_m_gla_w_lr': 'new_m', 'new_m_gla_b_lr': 'new_m', 'new_m_gla_head_norm': 'new_m', 'new_m_gla_w_out': 'new_m', 'new_m_pool_norm': 'new_m', 'new_m_pool_w': 'new_m', 'new_m_pool_b': 'new_m', 'new_m_pool_scale': 'new_m', 'new_m_final_norm': 'new_m', 'new_v_meta': 'new_v', 'new_v_ffn_norm': 'new_v', 'new_v_ffn_w_gate': 'new_v', 'new_v_ffn_w_up': 'new_v', 'new_v_ffn_w_down': 'new_v', 'new_v_gla_norm': 'new_v', 'new_v_gla_w_in': 'new_v', 'new_v_gla_w_lr': 'new_v', 'new_v_gla_b_lr': 'new_v', 'new_v_gla_head_norm': 'new_v', 'new_v_gla_w_out': 'new_v', 'new_v_pool_norm': 'new_v', 'new_v_pool_w': 'new_v', 'new_v_pool_b': 'new_v', 'new_v_pool_scale': 'new_v', 'new_v_final_norm': 'new_v'}


def _forward(args):
    return _fwd_reference(*[args[k] for k in FWD_PARAMS])


def _output_shape():
    out = _jax.eval_shape(lambda: _forward(_fwd_setup_inputs(0)))
    return out.shape, out.dtype

N_MICROBATCH = 1
ADAM_LR = 0.001
ADAM_B1 = 0.9
ADAM_B2 = 0.999
ADAM_EPS = 1e-08
ADAM_WD = 0.01
ADAM_STEP = 10
PER_EXAMPLE_BATCH_AXIS = {'x': 0, 'loss_target': 0}
SHARED_INPUTS = []
_WEIGHT_DTYPES = {'meta': _jnp.float32, 'ffn_norm': _jnp.float32, 'ffn_w_gate': _jnp.float32, 'ffn_w_up': _jnp.float32, 'ffn_w_down': _jnp.float32, 'gla_norm': _jnp.float32, 'gla_w_in': _jnp.float32, 'gla_w_lr': _jnp.float32, 'gla_b_lr': _jnp.float32, 'gla_head_norm': _jnp.float32, 'gla_w_out': _jnp.float32, 'pool_norm': _jnp.float32, 'pool_w': _jnp.float32, 'pool_b': _jnp.float32, 'pool_scale': _jnp.float32, 'final_norm': _jnp.float32}
MOMENT_SCALE = {'meta': 2.396920e-03, 'ffn_norm': 2.605465e-02, 'ffn_w_gate': 1.112145e-02, 'ffn_w_up': 1.077077e-02, 'ffn_w_down': 1.787306e-02, 'gla_norm': 7.063244e-02, 'gla_w_in': 3.950644e-02, 'gla_w_lr': 5.442068e-03, 'gla_b_lr': 2.173030e-02, 'gla_head_norm': 6.808603e-02, 'gla_w_out': 3.357267e-02, 'pool_norm': 3.653983e-02, 'pool_w': 3.655167e-02, 'pool_b': 4.186935e-02, 'pool_scale': 8.447334e-02, 'final_norm': 8.022499e+00}


def _to_microbatches(a, axis):
    t = _jnp.moveaxis(a, axis, 0)
    t = t.reshape((N_MICROBATCH, t.shape[0] // N_MICROBATCH) + t.shape[1:])
    return _jnp.moveaxis(t, 1, axis + 1)


def setup_inputs(seed: int = 0) -> dict:
    inp = _fwd_setup_inputs(seed)
    key = _jax.random.fold_in(_jax.random.key(seed), 7919)
    shape, _ = _output_shape()
    out = dict(inp)
    out["loss_target"] = _jax.random.normal(_jax.random.fold_in(key, 0), shape, _jnp.float32)
    for i, name in enumerate(TWIN_WEIGHTS):
        w = inp[name].astype(_jnp.float32)
        if MOMENT_SCALE is None:
            s = _jnp.sqrt(_jnp.mean(_jnp.square(w)) + 1e-30)
        else:
            s = MOMENT_SCALE[name]
        km, kv = _jax.random.split(_jax.random.fold_in(key, i + 1))
        out[name] = w
        out["m_" + name] = s * _jax.random.normal(km, w.shape, _jnp.float32)
        out["v_" + name] = (s * s) * _jax.random.uniform(kv, w.shape, _jnp.float32, 0.5, 1.5)
    if N_MICROBATCH > 1:
        for name, axis in PER_EXAMPLE_BATCH_AXIS.items():
            out[name] = _to_microbatches(out[name], axis)
    return {'x': out['x'], 'meta': out['meta'], 'ffn_norm': out['ffn_norm'], 'ffn_w_gate': out['ffn_w_gate'], 'ffn_w_up': out['ffn_w_up'], 'ffn_w_down': out['ffn_w_down'], 'gla_norm': out['gla_norm'], 'gla_w_in': out['gla_w_in'], 'gla_w_lr': out['gla_w_lr'], 'gla_b_lr': out['gla_b_lr'], 'gla_head_norm': out['gla_head_norm'], 'gla_w_out': out['gla_w_out'], 'pool_norm': out['pool_norm'], 'pool_w': out['pool_w'], 'pool_b': out['pool_b'], 'pool_scale': out['pool_scale'], 'final_norm': out['final_norm'], 'loss_target': out['loss_target'], 'm_meta': out['m_meta'], 'm_ffn_norm': out['m_ffn_norm'], 'm_ffn_w_gate': out['m_ffn_w_gate'], 'm_ffn_w_up': out['m_ffn_w_up'], 'm_ffn_w_down': out['m_ffn_w_down'], 'm_gla_norm': out['m_gla_norm'], 'm_gla_w_in': out['m_gla_w_in'], 'm_gla_w_lr': out['m_gla_w_lr'], 'm_gla_b_lr': out['m_gla_b_lr'], 'm_gla_head_norm': out['m_gla_head_norm'], 'm_gla_w_out': out['m_gla_w_out'], 'm_pool_norm': out['m_pool_norm'], 'm_pool_w': out['m_pool_w'], 'm_pool_b': out['m_pool_b'], 'm_pool_scale': out['m_pool_scale'], 'm_final_norm': out['m_final_norm'], 'v_meta': out['v_meta'], 'v_ffn_norm': out['v_ffn_norm'], 'v_ffn_w_gate': out['v_ffn_w_gate'], 'v_ffn_w_up': out['v_ffn_w_up'], 'v_ffn_w_down': out['v_ffn_w_down'], 'v_gla_norm': out['v_gla_norm'], 'v_gla_w_in': out['v_gla_w_in'], 'v_gla_w_lr': out['v_gla_w_lr'], 'v_gla_b_lr': out['v_gla_b_lr'], 'v_gla_head_norm': out['v_gla_head_norm'], 'v_gla_w_out': out['v_gla_w_out'], 'v_pool_norm': out['v_pool_norm'], 'v_pool_w': out['v_pool_w'], 'v_pool_b': out['v_pool_b'], 'v_pool_scale': out['v_pool_scale'], 'v_final_norm': out['v_final_norm']}


def _loss(weights, diff, rest, loss_target):
    with _jax.named_scope("forward"):
        args = {**rest, TWIN_DIFF_INPUT: diff, **{k: w.astype(_WEIGHT_DTYPES[k]) for k, w in weights.items()}}
        y = _forward(args)
    with _jax.named_scope("loss_head"):
        err = _jnp.square(y.astype(_jnp.float32) - loss_target)
        return 0.5 * _jnp.sum(_jnp.mean(err, axis=-1)) if err.ndim else 0.5 * err


def _adamw(w, g, m, v):
    m = ADAM_B1 * m + (1.0 - ADAM_B1) * g
    v = ADAM_B2 * v + (1.0 - ADAM_B2) * _jnp.square(g)
    m_hat = m / (1.0 - ADAM_B1 ** ADAM_STEP)
    v_hat = v / (1.0 - ADAM_B2 ** ADAM_STEP)
    delta = -ADAM_LR * (m_hat / (_jnp.sqrt(v_hat) + ADAM_EPS) + ADAM_WD * w)
    return delta, m, v


def reference(x, meta, ffn_norm, ffn_w_gate, ffn_w_up, ffn_w_down, gla_norm, gla_w_in, gla_w_lr, gla_b_lr, gla_head_norm, gla_w_out, pool_norm, pool_w, pool_b, pool_scale, final_norm, loss_target, m_meta, m_ffn_norm, m_ffn_w_gate, m_ffn_w_up, m_ffn_w_down, m_gla_norm, m_gla_w_in, m_gla_w_lr, m_gla_b_lr, m_gla_head_norm, m_gla_w_out, m_pool_norm, m_pool_w, m_pool_b, m_pool_scale, m_final_norm, v_meta, v_ffn_norm, v_ffn_w_gate, v_ffn_w_up, v_ffn_w_down, v_gla_norm, v_gla_w_in, v_gla_w_lr, v_gla_b_lr, v_gla_head_norm, v_gla_w_out, v_pool_norm, v_pool_w, v_pool_b, v_pool_scale, v_final_norm):
    given = dict(x=x, meta=meta, ffn_norm=ffn_norm, ffn_w_gate=ffn_w_gate, ffn_w_up=ffn_w_up, ffn_w_down=ffn_w_down, gla_norm=gla_norm, gla_w_in=gla_w_in, gla_w_lr=gla_w_lr, gla_b_lr=gla_b_lr, gla_head_norm=gla_head_norm, gla_w_out=gla_w_out, pool_norm=pool_norm, pool_w=pool_w, pool_b=pool_b, pool_scale=pool_scale, final_norm=final_norm, loss_target=loss_target, m_meta=m_meta, m_ffn_norm=m_ffn_norm, m_ffn_w_gate=m_ffn_w_gate, m_ffn_w_up=m_ffn_w_up, m_ffn_w_down=m_ffn_w_down, m_gla_norm=m_gla_norm, m_gla_w_in=m_gla_w_in, m_gla_w_lr=m_gla_w_lr, m_gla_b_lr=m_gla_b_lr, m_gla_head_norm=m_gla_head_norm, m_gla_w_out=m_gla_w_out, m_pool_norm=m_pool_norm, m_pool_w=m_pool_w, m_pool_b=m_pool_b, m_pool_scale=m_pool_scale, m_final_norm=m_final_norm, v_meta=v_meta, v_ffn_norm=v_ffn_norm, v_ffn_w_gate=v_ffn_w_gate, v_ffn_w_up=v_ffn_w_up, v_ffn_w_down=v_ffn_w_down, v_gla_norm=v_gla_norm, v_gla_w_in=v_gla_w_in, v_gla_w_lr=v_gla_w_lr, v_gla_b_lr=v_gla_b_lr, v_gla_head_norm=v_gla_head_norm, v_gla_w_out=v_gla_w_out, v_pool_norm=v_pool_norm, v_pool_w=v_pool_w, v_pool_b=v_pool_b, v_pool_scale=v_pool_scale, v_final_norm=v_final_norm)
    weights = {n: given[n] for n in TWIN_WEIGHTS}
    shared = {n: given[n] for n in SHARED_INPUTS}
    per_example = {n: given[n] for n in ['x']}
    grad_fn = _jax.value_and_grad(_loss, argnums=(0, 1))

    def one_microbatch(ex, loss_target):
        ex = dict(ex)
        diff = ex.pop(TWIN_DIFF_INPUT)
        return grad_fn(weights, diff, {**shared, **ex}, loss_target)

    if N_MICROBATCH == 1:
        loss, (grad_w, grad_x) = one_microbatch(per_example, given["loss_target"])
    else:
        def body(carry, xs):
            loss_sum, grad_sum = carry
            l_k, (gw_k, gx_k) = one_microbatch(xs[0], xs[1])
            with _jax.named_scope("update"):
                return (loss_sum + l_k, _jax.tree.map(_jnp.add, grad_sum, gw_k)), gx_k

        init = (_jnp.zeros((), _jnp.float32), _jax.tree.map(_jnp.zeros_like, weights))
        (loss, grad_w), grad_x = _jax.lax.scan(body, init, (per_example, given["loss_target"]))
    with _jax.named_scope("update"):
        delta_w, new_m, new_v = {}, {}, {}
        for n in TWIN_WEIGHTS:
            delta_w[n], new_m[n], new_v[n] = _adamw(weights[n], grad_w[n], given["m_" + n], given["v_" + n])
    return (loss, grad_x, *[grad_w[n] for n in TWIN_WEIGHTS], *[delta_w[n] for n in TWIN_WEIGHTS],
            *[new_m[n] for n in TWIN_WEIGHTS], *[new_v[n] for n in TWIN_WEIGHTS])
```

```python
import functools

import jax
import jax.numpy as jnp
from jax import lax
from jax.experimental import pallas as pl
from jax.experimental.pallas import tpu as pltpu

F32 = jnp.float32
BF16 = jnp.bfloat16
MESH = pl.DeviceIdType.MESH
ANY = pl.BlockSpec(memory_space=pl.ANY)

N_META = 16
CHUNK = 64
PAD = CHUNK - N_META
OFF = PAD + N_META
EPS = 1e-6
HEADS = 4
GATE_RANK = 16
GATE_NORM = 16.0
LR_W = 128
N_CHIPS = 4
N_DEV = 8
ADAM_LR, ADAM_B1, ADAM_B2, ADAM_EPS, ADAM_WD, ADAM_STEP = 0.001, 0.9, 0.999, 1e-08, 0.01, 10
VMEM_LIMIT = 56 * 1024 * 1024
ROW_TILE = 176


def _tile(n, target, mult=16):
    best = None
    for d in range(mult, min(n, target) + 1, mult):
        if n % d == 0:
            best = d
    return best if best is not None else n


def _params(*sem):
    return pltpu.CompilerParams(dimension_semantics=sem, vmem_limit_bytes=VMEM_LIMIT)


def _dot(a, b):
    return jnp.dot(a, b, preferred_element_type=F32)


def _dot_nt(a, b):
    return lax.dot_general(a, b, (((1,), (1,)), ((), ())), preferred_element_type=F32)


def _dot_tn(a, b):
    return lax.dot_general(a, b, (((0,), (0,)), ((), ())), preferred_element_type=F32)


def _sigmoid(x):
    return 1.0 / (1.0 + jnp.exp(-x))


def _rows(tile, width=1):
    return lax.broadcasted_iota(jnp.int32, (tile, width), 0)


def _pad_rows(t):
    return jnp.pad(t, ((0, -t.shape[0] % 8), (0, 0)))


def rmsnorm_fwd(x, g, out_dtype, name):
    M, D = x.shape
    tr = _tile(M, ROW_TILE)

    def body(x_ref, g_ref, h_ref, r_ref):
        xv = x_ref[...]
        r = lax.rsqrt(jnp.mean(xv * xv, axis=-1, keepdims=True) + EPS)
        h_ref[...] = (xv * r * g_ref[...]).astype(out_dtype)
        r_ref[...] = r

    return pl.pallas_call(
        body, name=name, grid=(M // tr,),
        in_specs=[pl.BlockSpec((tr, D), lambda i: (i, 0)), pl.BlockSpec((1, D), lambda i: (0, 0))],
        out_specs=[pl.BlockSpec((tr, D), lambda i: (i, 0)), pl.BlockSpec((tr, 1), lambda i: (i, 0))],
        out_shape=[jax.ShapeDtypeStruct((M, D), out_dtype), jax.ShapeDtypeStruct((M, 1), F32)],
        compiler_params=_params("parallel"),
    )(x, g)


def rmsnorm_bwd(dh, x, g, rstd, dres, name):
    M, D = x.shape
    tr = _tile(M, ROW_TILE)

    def body(dh_ref, x_ref, g_ref, r_ref, dres_ref, dx_ref, dg_ref):
        @pl.when(pl.program_id(0) == 0)
        def _():
            dg_ref[...] = jnp.zeros_like(dg_ref)

        r = r_ref[...]
        xhat = x_ref[...] * r
        dhv = dh_ref[...]
        gd = dhv * g_ref[...]
        dx_ref[...] = dres_ref[...] + r * (gd - xhat * jnp.mean(gd * xhat, axis=-1, keepdims=True))
        dg_ref[...] += jnp.sum(dhv * xhat, axis=0, keepdims=True)

    row = pl.BlockSpec((tr, D), lambda i: (i, 0))
    vec = pl.BlockSpec((1, D), lambda i: (0, 0))
    return pl.pallas_call(
        body, name=name, grid=(M // tr,),
        in_specs=[row, row, vec, pl.BlockSpec((tr, 1), lambda i: (i, 0)), row],
        out_specs=[row, vec],
        out_shape=[jax.ShapeDtypeStruct((M, D), F32), jax.ShapeDtypeStruct((1, D), F32)],
        compiler_params=_params("arbitrary"),
    )(dh, x, g, rstd, dres)


def final_loss(x, g, target):
    M, D = x.shape
    tr = _tile(M, ROW_TILE)

    def body(x_ref, g_ref, t_ref, loss_ref, dx_ref, dg_ref):
        i = pl.program_id(0)

        @pl.when(i == 0)
        def _():
            loss_ref[...] = jnp.zeros_like(loss_ref)
            dg_ref[...] = jnp.zeros_like(dg_ref)

        live = (_rows(tr) + i * tr) >= OFF
        xv = x_ref[...]
        gv = g_ref[...]
        r = lax.rsqrt(jnp.mean(xv * xv, axis=-1, keepdims=True) + EPS)
        xhat = xv * r
        err = jnp.where(live, xhat * gv - t_ref[...], 0.0)
        loss_ref[...] += 0.5 * jnp.sum(jnp.mean(err * err, axis=-1, keepdims=True), axis=0, keepdims=True)
        dy = err * (1.0 / D)
        gd = dy * gv
        dx_ref[...] = r * (gd - xhat * jnp.mean(gd * xhat, axis=-1, keepdims=True))
        dg_ref[...] += jnp.sum(dy * xhat, axis=0, keepdims=True)

    row = pl.BlockSpec((tr, D), lambda i: (i, 0))
    vec = pl.BlockSpec((1, D), lambda i: (0, 0))
    return pl.pallas_call(
        body, name="final_loss", grid=(M // tr,),
        in_specs=[row, vec, row],
        out_specs=[pl.BlockSpec((1, 1), lambda i: (0, 0)), row, vec],
        out_shape=[jax.ShapeDtypeStruct((1, 1), F32), jax.ShapeDtypeStruct((M, D), F32),
                   jax.ShapeDtypeStruct((1, D), F32)],
        compiler_params=_params("arbitrary"),
    )(x, g, target)


def mm_nn(a, w, out_dtype, name, tm_target=704, tn_target=896):
    M, K = a.shape
    N = w.shape[1]
    tm, tn = _tile(M, tm_target), _tile(N, tn_target, 128)

    def body(a_ref, w_ref, o_ref):
        o_ref[...] = _dot(a_ref[...], w_ref[...]).astype(out_dtype)

    return pl.pallas_call(
        body, name=name, grid=(N // tn, M // tm),
        in_specs=[pl.BlockSpec((tm, K), lambda n, i: (i, 0)), pl.BlockSpec((K, tn), lambda n, i: (0, n))],
        out_specs=pl.BlockSpec((tm, tn), lambda n, i: (i, n)),
        out_shape=jax.ShapeDtypeStruct((M, N), out_dtype),
        compiler_params=_params("parallel", "parallel"),
    )(a, w)


def ffn_gateup(h, wg, wu, name):
    M, D = h.shape
    Fs = wg.shape[2]
    tm = _tile(M, 352)

    def body(h_ref, wg_ref, wu_ref, g_ref, u_ref):
        hv = h_ref[...]
        g_ref[...] = _dot(hv, wg_ref[...]).astype(BF16)
        u_ref[...] = _dot(hv, wu_ref[...]).astype(BF16)

    wspec = pl.BlockSpec((None, D, Fs), lambda j, i: (j, 0, 0))
    ospec = pl.BlockSpec((tm, Fs), lambda j, i: (i, j))
    return pl.pallas_call(
        body, name=name, grid=(N_CHIPS, M // tm),
        in_specs=[pl.BlockSpec((tm, D), lambda j, i: (i, 0)), wspec, wspec],
        out_specs=[ospec, ospec],
        out_shape=[jax.ShapeDtypeStruct((M, N_CHIPS * Fs), BF16)] * 2,
        compiler_params=_params("parallel", "parallel"),
    )(h, wg, wu)


def mm_residual(acts, w, x, scale, tk, name):
    M, N = x.shape
    K = w.shape[0]
    tm = _tile(M, 352)
    swiglu = len(acts) == 2

    def body(*refs):
        a_refs, (w_ref, x_ref, o_ref, acc) = refs[:len(acts)], refs[len(acts):]
        k = pl.program_id(1)

        @pl.when(k == 0)
        def _():
            acc[...] = jnp.zeros_like(acc)

        if swiglu:
            gv = a_refs[0][...].astype(F32)
            av = (gv * _sigmoid(gv) * a_refs[1][...].astype(F32)).astype(BF16)
        else:
            av = a_refs[0][...]
        acc[...] += _dot(av, w_ref[...])

        @pl.when(k == pl.num_programs(1) - 1)
        def _():
            o_ref[...] = x_ref[...] + scale * acc[...]

    aspec = pl.BlockSpec((tm, tk), lambda i, k: (i, k))
    return pl.pallas_call(
        body, name=name, grid=(M // tm, K // tk),
        in_specs=[aspec] * len(acts) + [pl.BlockSpec((tk, N), lambda i, k: (k, 0)),
                                        pl.BlockSpec((tm, N), lambda i, k: (i, 0))],
        out_specs=pl.BlockSpec((tm, N), lambda i, k: (i, 0)),
        out_shape=jax.ShapeDtypeStruct((M, N), F32),
        scratch_shapes=[pltpu.VMEM((tm, N), F32)],
        compiler_params=_params("parallel", "arbitrary"),
    )(*acts, w, x)


def ffn_bwd_act(dout, wd, gate, up, name):
    M, D = dout.shape
    F = wd.shape[0]
    Fs = F // N_CHIPS
    tm = _tile(M, 352)

    def body(dy_ref, wd_ref, g_ref, u_ref, dg_ref, du_ref, a_ref):
        dy = (0.5 * dy_ref[...]).astype(BF16)
        da = _dot_nt(dy, wd_ref[...])
        gv = g_ref[...].astype(F32)
        uv = u_ref[...].astype(F32)
        s = _sigmoid(gv)
        silu = gv * s
        a_ref[...] = (silu * uv).astype(BF16)
        dg_ref[...] = (da * uv * (s * (1.0 + gv * (1.0 - s)))).astype(BF16)
        du_ref[...] = (da * silu).astype(BF16)

    fspec = pl.BlockSpec((tm, Fs), lambda j, i: (i, j))
    return pl.pallas_call(
        body, name=name, grid=(N_CHIPS, M // tm),
        in_specs=[pl.BlockSpec((tm, D), lambda j, i: (i, 0)), pl.BlockSpec((Fs, D), lambda j, i: (j, 0)),
                  fspec, fspec],
        out_specs=[fspec, fspec, fspec],
        out_shape=[jax.ShapeDtypeStruct((M, F), BF16)] * 3,
        compiler_params=_params("parallel", "parallel"),
    )(dout, wd, gate, up)


def mm_tn(a, b, ta, tb, name, b_scale=1.0, stacked_out=False, out_dtype=BF16):
    T, Ma = a.shape
    Nb = b.shape[1]

    def body(a_ref, b_ref, o_ref):
        bv = b_ref[...]
        if b_scale != 1.0:
            bv = b_scale * bv
        o_ref[...] = _dot_tn(a_ref[...], bv.astype(BF16)).astype(out_dtype)

    if stacked_out:
        out_spec = pl.BlockSpec((None, ta, tb), lambda jb, ja: (jb, ja, 0))
        out_shape = jax.ShapeDtypeStruct((Nb // tb, Ma, tb), out_dtype)
    else:
        out_spec = pl.BlockSpec((ta, tb), lambda jb, ja: (ja, jb))
        out_shape = jax.ShapeDtypeStruct((Ma, Nb), out_dtype)
    return pl.pallas_call(
        body, name=name, grid=(Nb // tb, Ma // ta),
        in_specs=[pl.BlockSpec((T, ta), lambda jb, ja: (0, ja)), pl.BlockSpec((T, tb), lambda jb, ja: (0, jb))],
        out_specs=out_spec, out_shape=out_shape,
        compiler_params=_params("parallel", "parallel"),
    )(a, b)


def mm_nt(pairs, tm, tn, tk, name, a_scale=1.0, stacked_w=False):
    M, K = pairs[0][0].shape
    N = pairs[0][1].shape[1] if stacked_w else pairs[0][1].shape[0]
    n_pairs = len(pairs)

    def body(*refs):
        o_ref, acc = refs[2 * n_pairs:]
        k = pl.program_id(2)

        @pl.when(k == 0)
        def _():
            acc[...] = jnp.zeros_like(acc)

        for p in range(n_pairs):
            av = refs[2 * p][...]
            if a_scale != 1.0:
                av = a_scale * av
            acc[...] += _dot_nt(av.astype(BF16), refs[2 * p + 1][...])

        @pl.when(k == pl.num_programs(2) - 1)
        def _():
            o_ref[...] = acc[...]

    aspec = pl.BlockSpec((tm, tk), lambda i, n, k: (i, k))
    if stacked_w:
        wspec = pl.BlockSpec((None, tn, tk), lambda i, n, k: (k, n, 0))
    else:
        wspec = pl.BlockSpec((tn, tk), lambda i, n, k: (n, k))
    return pl.pallas_call(
        body, name=name, grid=(M // tm, N // tn, K // tk),
        in_specs=[aspec, wspec] * n_pairs,
        out_specs=pl.BlockSpec((tm, tn), lambda i, n, k: (i, n)),
        out_shape=jax.ShapeDtypeStruct((M, N), F32),
        scratch_shapes=[pltpu.VMEM((tm, tn), F32)],
        compiler_params=_params("parallel", "parallel", "arbitrary"),
    )(*[t for pair in pairs for t in pair])


def _tri(lower):
    r = lax.broadcasted_iota(jnp.int32, (CHUNK, CHUNK), 0)
    c = lax.broadcasted_iota(jnp.int32, (CHUNK, CHUNK), 1)
    return (r >= c) if lower else (r <= c)


def _tri_sum(mask, x, pieces):
    ones = mask.astype(BF16)
    acc = jnp.zeros_like(x)
    rest = x
    for _ in range(pieces):
        piece = rest.astype(BF16)
        acc = acc + _dot(ones, piece)
        rest = rest - piece.astype(F32)
    return acc


def _gla_gates(lr_ref, wlr_ref, blr_ref, chunk):
    z = _dot(lr_ref[...].astype(BF16), wlr_ref[...]) + blr_ref[...]
    live = (_rows(CHUNK) + chunk * CHUNK) >= PAD
    lg = jnp.where(live, (jnp.minimum(z, 0.0) - jnp.log(1.0 + jnp.exp(-jnp.abs(z)))) * (1.0 / GATE_NORM), 0.0)
    b = _tri_sum(_tri(True), lg, 3)
    b_last = jnp.sum(lg, axis=0, keepdims=True)
    b_mid = jnp.sum(jnp.where(_rows(CHUNK) < CHUNK // 2, lg, 0.0), axis=0, keepdims=True)
    return z, live, b, b_last, b_mid


def _gla_specs(dkh, dvh, D, chunk_of):
    lr_blk = (3 * D) // LR_W
    return [
        pl.BlockSpec((CHUNK, dkh), lambda c, h: (chunk_of(c), h)),
        pl.BlockSpec((CHUNK, dkh), lambda c, h: (chunk_of(c), HEADS + h)),
        pl.BlockSpec((CHUNK, dvh), lambda c, h: (chunk_of(c), HEADS + h)),
        pl.BlockSpec((CHUNK, LR_W), lambda c, h: (chunk_of(c), lr_blk)),
        pl.BlockSpec((LR_W, dkh), lambda c, h: (0, h)),
        pl.BlockSpec((1, dkh), lambda c, h: (0, h)),
    ]


def gla_fwd(proj, wlr, blr, D):
    M = proj.shape[0]
    n = M // CHUNK
    dkh, dvh = D // 2 // HEADS, D // HEADS
    qscale = float(dkh) ** -0.5

    def body(q_ref, k_ref, v_ref, lr_ref, wlr_ref, blr_ref, o_ref, st_ref, S):
        c, h = pl.program_id(0), pl.program_id(1)

        @pl.when(c == 0)
        def _():
            S[h] = jnp.zeros((dvh, dkh), F32)

        _, _, b, b_last, b_mid = _gla_gates(lr_ref, wlr_ref, blr_ref, c)
        q = q_ref[...] * qscale
        k = k_ref[...]
        v = v_ref[...].astype(BF16)
        s0 = S[h]
        st_ref[...] = s0
        qb = (q * jnp.exp(b)).astype(BF16)
        kb = (k * jnp.exp(b_last - b)).astype(BF16)
        qt = (q * jnp.exp(b - b_mid)).astype(BF16)
        kt = (k * jnp.exp(b_mid - b)).astype(BF16)
        a = jnp.where(_tri(True), _dot_nt(qt, kt), 0.0).astype(BF16)
        o_ref[...] = _dot_nt(qb, s0.astype(BF16)) + _dot(a, v)
        S[h] = jnp.exp(b_last) * s0 + _dot_tn(v, kb)

    return pl.pallas_call(
        body, name="gla_fwd", grid=(n, HEADS),
        in_specs=_gla_specs(dkh, dvh, D, lambda c: c),
        out_specs=[pl.BlockSpec((CHUNK, dvh), lambda c, h: (c, h)),
                   pl.BlockSpec((None, None, dvh, dkh), lambda c, h: (c, h, 0, 0))],
        out_shape=[jax.ShapeDtypeStruct((M, D), F32), jax.ShapeDtypeStruct((n, HEADS, dvh, dkh), F32)],
        scratch_shapes=[pltpu.VMEM((HEADS, dvh, dkh), F32)],
        compiler_params=_params("arbitrary", "arbitrary"),
    )(proj, proj, proj, proj, wlr, blr)


def gla_bwd(proj, wlr, blr, st, do, D):
    M = proj.shape[0]
    n = M // CHUNK
    dkh, dvh = D // 2 // HEADS, D // HEADS
    qscale = float(dkh) ** -0.5
    rev = lambda c: n - 1 - c

    def body(q_ref, k_ref, v_ref, lr_ref, wlr_ref, blr_ref, st_ref, do_ref,
             dq_ref, dk_ref, dv_ref, dlr_ref, dwlr_ref, dblr_ref, dS, acc_w, acc_b):
        step, h = pl.program_id(0), pl.program_id(1)
        c = n - 1 - step

        @pl.when(step == 0)
        def _():
            dS[h] = jnp.zeros((dvh, dkh), F32)
            acc_w[h] = jnp.zeros((LR_W, dkh), F32)
            acc_b[h] = jnp.zeros((1, dkh), F32)

        z, live, b, b_last, b_mid = _gla_gates(lr_ref, wlr_ref, blr_ref, c)
        q = q_ref[...] * qscale
        k = k_ref[...]
        v = v_ref[...].astype(BF16)
        dov = do_ref[...].astype(BF16)
        s0 = st_ref[...]
        ds1 = dS[h]
        ds1b = ds1.astype(BF16)
        e_b, e_lb = jnp.exp(b), jnp.exp(b_last - b)
        e_bm, e_mb = jnp.exp(b - b_mid), jnp.exp(b_mid - b)
        e_last = jnp.exp(b_last)
        qb, kb, qt, kt = q * e_b, k * e_lb, q * e_bm, k * e_mb
        qbb, kbb, qtb, ktb = qb.astype(BF16), kb.astype(BF16), qt.astype(BF16), kt.astype(BF16)
        lower = _tri(True)
        a = jnp.where(lower, _dot_nt(qtb, ktb), 0.0).astype(BF16)
        da = jnp.where(lower, _dot_nt(dov, v), 0.0).astype(BF16)

        dqb = _dot(dov, s0.astype(BF16))
        dqt = _dot(da, ktb)
        dkt = _dot_tn(da, qtb)
        dkb = _dot(v, ds1b)
        keep = live.astype(F32)
        dv_ref[...] = (keep * (_dot_tn(a, dov) + _dot_nt(kbb, ds1b))).astype(BF16)
        dq_ref[...] = (keep * qscale * (dqb * e_b + dqt * e_bm)).astype(BF16)
        dk_ref[...] = (keep * (dkb * e_lb + dkt * e_mb)).astype(BF16)

        db = dqb * qb - dkb * kb + dqt * qt - dkt * kt
        db_last = (jnp.sum(dkb * kb, axis=0, keepdims=True)
                   + jnp.sum(ds1 * s0, axis=0, keepdims=True) * e_last)
        db = db + jnp.where(_rows(CHUNK) == CHUNK - 1, db_last, 0.0)
        dlg = jnp.where(live, _tri_sum(_tri(False), db, 2), 0.0)
        dz = dlg * (1.0 / GATE_NORM) / (1.0 + jnp.exp(z))
        dzb = dz.astype(BF16)

        dlr_h = _dot_nt(dzb, wlr_ref[...])

        @pl.when(h == 0)
        def _():
            dlr_ref[...] = dlr_h

        @pl.when(h > 0)
        def _():
            dlr_ref[...] += dlr_h

        acc_w[h] += _dot_tn(lr_ref[...].astype(BF16), dzb)
        acc_b[h] += jnp.sum(dz, axis=0, keepdims=True)
        dS[h] = e_last * ds1 + _dot_tn(dov, qbb)

        @pl.when(step == n - 1)
        def _():
            dwlr_ref[h] = acc_w[h]
            dblr_ref[h] = acc_b[h]

    return pl.pallas_call(
        body, name="gla_bwd", grid=(n, HEADS),
        in_specs=_gla_specs(dkh, dvh, D, rev) + [
            pl.BlockSpec((None, None, dvh, dkh), lambda c, h: (rev(c), h, 0, 0)),
            pl.BlockSpec((CHUNK, dvh), lambda c, h: (rev(c), h))],
        out_specs=[pl.BlockSpec((CHUNK, dkh), lambda c, h: (rev(c), h)),
                   pl.BlockSpec((CHUNK, dkh), lambda c, h: (rev(c), h)),
                   pl.BlockSpec((CHUNK, dvh), lambda c, h: (rev(c), h)),
                   pl.BlockSpec((CHUNK, LR_W), lambda c, h: (rev(c), 0)),
                   pl.BlockSpec((HEADS, LR_W, dkh), lambda c, h: (0, 0, 0)),
                   pl.BlockSpec((HEADS, 1, dkh), lambda c, h: (0, 0, 0))],
        out_shape=[jax.ShapeDtypeStruct((M, D // 2), BF16), jax.ShapeDtypeStruct((M, D // 2), BF16),
                   jax.ShapeDtypeStruct((M, D), BF16), jax.ShapeDtypeStruct((M, LR_W), F32),
                   jax.ShapeDtypeStruct((HEADS, LR_W, dkh), F32), jax.ShapeDtypeStruct((HEADS, 1, dkh), F32)],
        scratch_shapes=[pltpu.VMEM((HEADS, dvh, dkh), F32), pltpu.VMEM((HEADS, LR_W, dkh), F32),
                        pltpu.VMEM((HEADS, 1, dkh), F32)],
        compiler_params=_params("arbitrary", "arbitrary"),
    )(proj, proj, proj, proj, wlr, blr, st, do)


def gla_post_fwd(o, proj, head_norm, D):
    M = o.shape[0]
    dvh = D // HEADS
    tr = _tile(M, ROW_TILE)

    def body(o_ref, r_ref, hn_ref, out_ref):
        for hd in range(HEADS):
            cols = slice(hd * dvh, (hd + 1) * dvh)
            ov = o_ref[:, cols]
            rs = lax.rsqrt(jnp.mean(ov * ov, axis=-1, keepdims=True) + EPS)
            rv = r_ref[:, cols]
            out_ref[:, cols] = (ov * rs * hn_ref[...] * (rv * _sigmoid(rv))).astype(BF16)

    row = pl.BlockSpec((tr, D), lambda i: (i, 0))
    return pl.pallas_call(
        body, name="gla_post_fwd", grid=(M // tr,),
        in_specs=[row, pl.BlockSpec((tr, D), lambda i: (i, 2)), pl.BlockSpec((1, dvh), lambda i: (0, 0))],
        out_specs=row, out_shape=jax.ShapeDtypeStruct((M, D), BF16),
        compiler_params=_params("parallel"),
    )(o, proj, head_norm)


def gla_post_bwd(dgated, o, proj, head_norm, D):
    M = o.shape[0]
    dvh = D // HEADS
    tr = _tile(M, ROW_TILE)

    def body(dg_ref, o_ref, r_ref, hn_ref, do_ref, dr_ref, dhn_ref):
        @pl.when(pl.program_id(0) == 0)
        def _():
            dhn_ref[...] = jnp.zeros_like(dhn_ref)

        hn = hn_ref[...]
        dhn = jnp.zeros((1, dvh), F32)
        for hd in range(HEADS):
            cols = slice(hd * dvh, (hd + 1) * dvh)
            ov = o_ref[:, cols]
            rs = lax.rsqrt(jnp.mean(ov * ov, axis=-1, keepdims=True) + EPS)
            ohat = ov * rs
            rv = r_ref[:, cols]
            s = _sigmoid(rv)
            dgv = dg_ref[:, cols]
            don = dgv * (rv * s)
            dr_ref[:, cols] = (dgv * ohat * hn * (s * (1.0 + rv * (1.0 - s)))).astype(BF16)
            gd = don * hn
            do_ref[:, cols] = rs * (gd - ohat * jnp.mean(gd * ohat, axis=-1, keepdims=True))
            dhn = dhn + jnp.sum(don * ohat, axis=0, keepdims=True)
        dhn_ref[...] += dhn

    row = pl.BlockSpec((tr, D), lambda i: (i, 0))
    vec = pl.BlockSpec((1, dvh), lambda i: (0, 0))
    return pl.pallas_call(
        body, name="gla_post_bwd", grid=(M // tr,),
        in_specs=[row, row, pl.BlockSpec((tr, D), lambda i: (i, 2)), vec],
        out_specs=[row, row, vec],
        out_shape=[jax.ShapeDtypeStruct((M, D), F32), jax.ShapeDtypeStruct((M, D), BF16),
                   jax.ShapeDtypeStruct((1, dvh), F32)],
        compiler_params=_params("arbitrary"),
    )(dgated, o, proj, head_norm)


def _pool_counts(M, g):
    t = _rows(M) - PAD
    win = jnp.left_shift(2, g)
    return t >= 0, jnp.maximum(jnp.minimum(t + 1, win), 1).astype(F32)


def _window_sum(x, g, M, back):
    sums = []
    s = x
    for lvl in range(4):
        sh = 1 << lvl
        s = s + pltpu.roll(s, (M - sh) if back else sh, 0)
        sums.append(s)
    return jnp.where(g == 0, sums[0], jnp.where(g == 1, sums[1], jnp.where(g == 2, sums[2], sums[3])))


POOL_COLS = 128


def pool_window(hp):
    M, D = hp.shape
    cw = min(POOL_COLS, D // 4)
    per_group = (D // 4) // cw

    def body(h_ref, p_ref):
        g = pl.program_id(0) // per_group
        live, cnt = _pool_counts(M, g)
        hv = h_ref[...]
        p_ref[...] = jnp.where(live, _window_sum(hv, g, M, False) / cnt - hv, 0.0).astype(BF16)

    col = pl.BlockSpec((M, cw), lambda j: (0, j))
    return pl.pallas_call(
        body, name="pool_window", grid=(D // cw,), in_specs=[col], out_specs=col,
        out_shape=jax.ShapeDtypeStruct((M, D), BF16), compiler_params=_params("parallel"),
    )(hp)


def pool_window_bwd(dpooled):
    M, D = dpooled.shape
    cw = min(POOL_COLS, D // 4)
    per_group = (D // 4) // cw

    def body(d_ref, o_ref):
        g = pl.program_id(0) // per_group
        live, cnt = _pool_counts(M, g)
        dv = jnp.where(live, d_ref[...], 0.0)
        o_ref[...] = jnp.where(live, _window_sum(dv / cnt, g, M, True) - dv, 0.0)

    col = pl.BlockSpec((M, cw), lambda j: (0, j))
    return pl.pallas_call(
        body, name="pool_window_bwd", grid=(D // cw,), in_specs=[col], out_specs=col,
        out_shape=jax.ShapeDtypeStruct((M, D), F32), compiler_params=_params("parallel"),
    )(dpooled)


def pool_mix(pooled, x, w, bias, scale):
    M, D = x.shape
    W = D // 4
    tm = _tile(M, 352)

    def body(p_ref, x_ref, w_ref, b_ref, s_ref, out_ref):
        live = (_rows(tm) + pl.program_id(1) * tm) >= PAD
        y = (_dot(p_ref[...], w_ref[...]) + b_ref[...]) * s_ref[...]
        out_ref[...] = x_ref[...] + jnp.where(live, y, 0.0)

    blk = pl.BlockSpec((tm, W), lambda g, i: (i, g))
    vec = pl.BlockSpec((1, W), lambda g, i: (0, g))
    return pl.pallas_call(
        body, name="pool_mix", grid=(4, M // tm),
        in_specs=[blk, blk, pl.BlockSpec((None, W, W), lambda g, i: (g, 0, 0)), vec, vec],
        out_specs=blk, out_shape=jax.ShapeDtypeStruct((M, D), F32),
        compiler_params=_params("parallel", "parallel"),
    )(pooled, x, w, bias, scale)


def pool_mix_bwd(dy, pooled, w, bias, scale):
    M, D = dy.shape
    W = D // 4
    tm = _tile(M, 352)

    def body(dy_ref, p_ref, w_ref, b_ref, s_ref, dp_ref, dw_ref, db_ref, ds_ref, acc_w):
        i = pl.program_id(1)

        @pl.when(i == 0)
        def _():
            acc_w[...] = jnp.zeros_like(acc_w)
            db_ref[...] = jnp.zeros_like(db_ref)
            ds_ref[...] = jnp.zeros_like(ds_ref)

        live = (_rows(tm) + i * tm) >= PAD
        dyv = jnp.where(live, dy_ref[...], 0.0)
        pooled = p_ref[...]
        wv = w_ref[...]
        ds_ref[...] += jnp.sum(dyv * (_dot(pooled, wv) + b_ref[...]), axis=0, keepdims=True)
        dys = dyv * s_ref[...]
        db_ref[...] += jnp.sum(dys, axis=0, keepdims=True)
        dysb = dys.astype(BF16)
        acc_w[...] += _dot_tn(pooled, dysb)
        dp_ref[...] = _dot_nt(dysb, wv)

        @pl.when(i == pl.num_programs(1) - 1)
        def _():
            dw_ref[...] = acc_w[...].astype(BF16)

    blk = pl.BlockSpec((tm, W), lambda g, i: (i, g))
    vec = pl.BlockSpec((1, W), lambda g, i: (0, g))
    wspec = pl.BlockSpec((None, W, W), lambda g, i: (g, 0, 0))
    return pl.pallas_call(
        body, name="pool_mix_bwd", grid=(4, M // tm),
        in_specs=[blk, blk, wspec, vec, vec],
        out_specs=[blk, wspec, vec, vec],
        out_shape=[jax.ShapeDtypeStruct((M, D), F32), jax.ShapeDtypeStruct((4, W, W), BF16),
                   jax.ShapeDtypeStruct((1, D), F32), jax.ShapeDtypeStruct((1, D), F32)],
        scratch_shapes=[pltpu.VMEM((W, W), F32)],
        compiler_params=_params("parallel", "arbitrary"),
    )(dy, pooled, w, bias, scale)


def adamw(w, g, m, v, name):
    shape = w.shape
    C = shape[-1]
    R = w.size // C
    tr = _tile(R, 256, 8)

    def body(w_ref, g_ref, m_ref, v_ref, d_ref, nm_ref, nv_ref):
        gv = g_ref[...]
        nm = ADAM_B1 * m_ref[...] + (1.0 - ADAM_B1) * gv
        nv = ADAM_B2 * v_ref[...] + (1.0 - ADAM_B2) * (gv * gv)
        m_hat = nm / (1.0 - ADAM_B1 ** ADAM_STEP)
        v_hat = nv / (1.0 - ADAM_B2 ** ADAM_STEP)
        d_ref[...] = -ADAM_LR * (m_hat / (jnp.sqrt(v_hat) + ADAM_EPS) + ADAM_WD * w_ref[...])
        nm_ref[...] = nm
        nv_ref[...] = nv

    spec = pl.BlockSpec((tr, C), lambda i: (i, 0))
    outs = pl.pallas_call(
        body, name=name, grid=(R // tr,),
        in_specs=[spec] * 4, out_specs=[spec] * 3,
        out_shape=[jax.ShapeDtypeStruct((R, C), F32)] * 3,
        compiler_params=_params("parallel"),
    )(*[t.reshape(R, C) for t in (w, g, m, v)])
    return [t.reshape(shape) for t in outs]


def add_sibling(grad, recv, core, name):
    _, _, Rh, C = grad.shape
    tr = _tile(Rh, 512)

    def body(core_ref, g_ref, r_ref, o_ref):
        o_ref[...] = (g_ref[...].astype(F32) + r_ref[...].astype(F32)).astype(BF16)

    return pl.pallas_call(
        body, name=name,
        grid_spec=pltpu.PrefetchScalarGridSpec(
            num_scalar_prefetch=1, grid=(N_CHIPS, Rh // tr),
            in_specs=[pl.BlockSpec((None, None, tr, C), lambda j, i, core_ref: (j, core_ref[0], i, 0)),
                      pl.BlockSpec((None, tr, C), lambda j, i, core_ref: (j, i, 0))],
            out_specs=pl.BlockSpec((None, tr, C), lambda j, i, core_ref: (j, i, 0))),
        out_shape=jax.ShapeDtypeStruct((N_CHIPS, Rh, C), BF16),
        compiler_params=_params("parallel", "parallel"),
    )(core, grad, recv)


def add_chips(part, recv, chip, name):
    _, Rh, C = part.shape
    tr = _tile(Rh, 512)

    def body(chip_ref, p_ref, r_ref, o_ref):
        acc = p_ref[...].astype(F32)
        for k in range(N_CHIPS - 1):
            acc = acc + r_ref[k].astype(F32)
        o_ref[...] = acc

    return pl.pallas_call(
        body, name=name,
        grid_spec=pltpu.PrefetchScalarGridSpec(
            num_scalar_prefetch=1, grid=(Rh // tr,),
            in_specs=[pl.BlockSpec((None, tr, C), lambda i, chip_ref: (chip_ref[0], i, 0)),
                      pl.BlockSpec((N_CHIPS - 1, tr, C), lambda i, chip_ref: (0, i, 0))],
            out_specs=pl.BlockSpec((tr, C), lambda i, chip_ref: (i, 0))),
        out_shape=jax.ShapeDtypeStruct((Rh, C), F32),
        compiler_params=_params("parallel"),
    )(chip, part, recv)


def sum_devices(gathered):
    _, R, C = gathered.shape

    def body(g_ref, o_ref):
        acc = g_ref[0]
        for d in range(1, N_DEV):
            acc = acc + g_ref[d]
        o_ref[...] = acc

    return pl.pallas_call(
        body, name="sum_devices", grid=(1,),
        in_specs=[pl.BlockSpec((N_DEV, R, C), lambda i: (0, 0, 0))],
        out_specs=pl.BlockSpec((R, C), lambda i: (0, 0)),
        out_shape=jax.ShapeDtypeStruct((R, C), F32),
        compiler_params=_params("arbitrary"),
    )(gathered)


def _place():
    x, y, c = lax.axis_index("x"), lax.axis_index("y"), lax.axis_index("c")
    others = [(1 - x, y), (x, 1 - y), (1 - x, 1 - y)]
    return x, y, c, others


def _remote(src, dst, send_sems, recv_sems, idx, device):
    return pltpu.make_async_remote_copy(src_ref=src, dst_ref=dst, send_sem=send_sems.at[idx],
                                        recv_sem=recv_sems.at[idx], device_id=device, device_id_type=MESH)


def gather_weights(groups, small, name):
    members = [(gi, mi) for gi, grp in enumerate(groups) for mi in range(grp.shape[0])]
    n_mem = len(members)
    n_in = len(groups) + (small is not None)
    n_copies = 6 * n_mem + (3 if small is not None else 0)

    def body(*refs):
        in_refs, out_refs = refs[:n_in], refs[n_in:n_in + n_mem + (small is not None)]
        send_sems, recv_sems, local_sems = refs[n_in + n_mem + (small is not None):]
        x, y, c, others = _place()
        me = 2 * x + y
        sibling = (x, y, 1 - c)
        local, sends = [], []

        for t, (gi, mi) in enumerate(members):
            src, out = in_refs[gi].at[mi], out_refs[t]
            for half in range(2):
                cp = pltpu.make_async_copy(src.at[half], out.at[me, half], local_sems.at[2 * t + half])
                cp.start()
                local.append(cp)
            for k, chip in enumerate(others):
                cp = _remote(src.at[c], out.at[me, c], send_sems, recv_sems, 6 * t + k, (*chip, c))
                cp.start()
                sends.append(cp)
        if small is not None:
            cp = pltpu.make_async_copy(in_refs[-1], out_refs[-1].at[me], local_sems.at[2 * n_mem])
            cp.start()
            local.append(cp)
            for k, chip in enumerate(others):
                cp = _remote(in_refs[-1], out_refs[-1].at[me], send_sems, recv_sems, 6 * n_mem + k, (*chip, c))
                cp.start()
                sends.append(cp)

        for t in range(n_mem):
            out = out_refs[t]
            for k, (ox, oy) in enumerate(others):
                theirs = out.at[2 * ox + oy, c]
                _remote(theirs, theirs, send_sems, recv_sems, 6 * t + k, sibling).wait_recv()
                cp = _remote(theirs, theirs, send_sems, recv_sems, 6 * t + 3 + k, sibling)
                cp.start()
                sends.append(cp)
        for t in range(n_mem):
            out = out_refs[t]
            for k, (ox, oy) in enumerate(others):
                theirs = out.at[2 * ox + oy, 1 - c]
                _remote(theirs, theirs, send_sems, recv_sems, 6 * t + 3 + k, sibling).wait_recv()
        if small is not None:
            for k, (ox, oy) in enumerate(others):
                theirs = out_refs[-1].at[2 * ox + oy]
                _remote(theirs, theirs, send_sems, recv_sems, 6 * n_mem + k, sibling).wait_recv()
        for cp in sends:
            cp.wait_send()
        for cp in local:
            cp.wait()

    out_shape = [jax.ShapeDtypeStruct((N_CHIPS,) + groups[gi].shape[1:], groups[gi].dtype) for gi, _ in members]
    if small is not None:
        out_shape.append(jax.ShapeDtypeStruct((N_CHIPS,) + small.shape, small.dtype))
    return pl.pallas_call(
        body, name=name, in_specs=[ANY] * n_in, out_specs=[ANY] * len(out_shape), out_shape=out_shape,
        scratch_shapes=[pltpu.SemaphoreType.DMA((n_copies,)), pltpu.SemaphoreType.DMA((n_copies,)),
                        pltpu.SemaphoreType.DMA((2 * n_mem + 1,))],
    )(*groups, *([small] if small is not None else []))


def send_halves_to_sibling(grads, name):
    n = len(grads)

    def body(*refs):
        in_refs, out_refs, (send_sems, recv_sems) = refs[:n], refs[n:2 * n], refs[2 * n:]
        x, y, c, _ = _place()
        sibling = (x, y, 1 - c)
        copies = []
        for t in range(n):
            for j in range(N_CHIPS):
                cp = _remote(in_refs[t].at[j, 1 - c], out_refs[t].at[j], send_sems, recv_sems,
                             N_CHIPS * t + j, sibling)
                cp.start()
                copies.append(cp)
        for cp in copies:
            cp.wait()

    return pl.pallas_call(
        body, name=name, in_specs=[ANY] * n, out_specs=[ANY] * n,
        out_shape=[jax.ShapeDtypeStruct((N_CHIPS,) + g.shape[2:], g.dtype) for g in grads],
        scratch_shapes=[pltpu.SemaphoreType.DMA((N_CHIPS * n,)), pltpu.SemaphoreType.DMA((N_CHIPS * n,))],
    )(*grads)


def send_parts_to_chips(parts, name):
    n = len(parts)

    def body(*refs):
        in_refs, out_refs, (send_sems, recv_sems) = refs[:n], refs[n:2 * n], refs[2 * n:]
        x, y, c, others = _place()
        copies = []
        for t in range(n):
            for k, (ox, oy) in enumerate(others):
                cp = _remote(in_refs[t].at[2 * ox + oy], out_refs[t].at[k], send_sems, recv_sems,
                             3 * t + k, (ox, oy, c))
                cp.start()
                copies.append(cp)
        for cp in copies:
            cp.wait()

    return pl.pallas_call(
        body, name=name, in_specs=[ANY] * n, out_specs=[ANY] * n,
        out_shape=[jax.ShapeDtypeStruct((N_CHIPS - 1,) + p.shape[1:], p.dtype) for p in parts],
        scratch_shapes=[pltpu.SemaphoreType.DMA((3 * n,)), pltpu.SemaphoreType.DMA((3 * n,))],
    )(*parts)


def exchange_halves(halves, group_sizes, name):
    n = len(halves)
    n_groups = len(group_sizes)
    slots = [(gi, mi) for gi, size in enumerate(group_sizes) for mi in range(size)]

    def body(*refs):
        in_refs, out_refs = refs[:n], refs[n:n + n_groups]
        send_sems, recv_sems, local_sems = refs[n + n_groups:]
        x, y, c, _ = _place()
        sibling = (x, y, 1 - c)
        copies, local = [], []
        for t, (gi, mi) in enumerate(slots):
            mine = out_refs[gi].at[mi, c]
            cp = pltpu.make_async_copy(in_refs[t], mine, local_sems.at[t])
            cp.start()
            local.append(cp)
            cp = _remote(in_refs[t], mine, send_sems, recv_sems, t, sibling)
            cp.start()
            copies.append(cp)
        for t, (gi, mi) in enumerate(slots):
            theirs = out_refs[gi].at[mi, 1 - c]
            _remote(theirs, theirs, send_sems, recv_sems, t, sibling).wait_recv()
        for cp in copies:
            cp.wait_send()
        for cp in local:
            cp.wait()

    out_shape, t = [], 0
    for size in group_sizes:
        out_shape.append(jax.ShapeDtypeStruct((size, 2) + halves[t].shape, halves[t].dtype))
        t += size
    return pl.pallas_call(
        body, name=name, in_specs=[ANY] * n, out_specs=[ANY] * n_groups, out_shape=out_shape,
        scratch_shapes=[pltpu.SemaphoreType.DMA((n,)), pltpu.SemaphoreType.DMA((n,)),
                        pltpu.SemaphoreType.DMA((n,))],
    )(*halves)


def gather_devices(buf):
    def body(in_ref, out_ref, send_sems, recv_sems, local_sem):
        x, y, c, _ = _place()
        me = 4 * x + 2 * y + c
        local = pltpu.make_async_copy(in_ref, out_ref.at[me], local_sem)
        local.start()
        copies = []
        for k in range(1, N_DEV):
            fx, fy, fc = (k >> 2) & 1, (k >> 1) & 1, k & 1
            peer = (x ^ fx, y ^ fy, c ^ fc)
            cp = _remote(in_ref, out_ref.at[me], send_sems, recv_sems, k - 1, peer)
            cp.start()
            copies.append(cp)
        for k in range(1, N_DEV):
            fx, fy, fc = (k >> 2) & 1, (k >> 1) & 1, k & 1
            theirs = out_ref.at[4 * (x ^ fx) + 2 * (y ^ fy) + (c ^ fc)]
            _remote(theirs, theirs, send_sems, recv_sems, k - 1, (x, y, c)).wait_recv()
        for cp in copies:
            cp.wait_send()
        local.wait()

    return pl.pallas_call(
        body, name="gather_devices", in_specs=[ANY], out_specs=ANY,
        out_shape=jax.ShapeDtypeStruct((N_DEV,) + buf.shape, buf.dtype),
        scratch_shapes=[pltpu.SemaphoreType.DMA((N_DEV - 1,)), pltpu.SemaphoreType.DMA((N_DEV - 1,)),
                        pltpu.SemaphoreType.DMA],
    )(buf)


def reduce_scatter(grads, group_sizes, core, chip, name):
    recv = send_halves_to_sibling(grads, name + "_sibling")
    parts = [add_sibling(g, r, core, f"{name}_add_sibling{t}") for t, (g, r) in enumerate(zip(grads, recv))]
    recv = send_parts_to_chips(parts, name + "_chips")
    halves = [add_chips(p, r, chip, f"{name}_add_chips{t}") for t, (p, r) in enumerate(zip(parts, recv))]
    return exchange_halves(halves, group_sizes, name + "_swap")


def _ffn_fwd(x, gain, wg, wu, wd, tag):
    h, rstd = rmsnorm_fwd(x, gain, BF16, f"ffn_norm_{tag}")
    gate, up = ffn_gateup(h, wg, wu, f"ffn_gateup_{tag}")
    out = mm_residual([gate, up], wd, x, 0.5, wd.shape[0] // N_CHIPS, f"ffn_down_{tag}")
    return out, (x, gain, h, rstd, gate, up)


def _ffn_bwd(dout, saved, wg, wu, wd, tag):
    x, gain, h, rstd, gate, up = saved
    D = x.shape[1]
    Fs = wg.shape[2]
    dgate, dup, act = ffn_bwd_act(dout, wd, gate, up, f"ffn_bwd_act_{tag}")
    d_wd = mm_tn(act, dout, Fs, _tile(D, 512, 128), f"ffn_bwd_wd_{tag}", b_scale=0.5)
    d_wg = mm_tn(h, dgate, _tile(D, 512, 128), Fs, f"ffn_bwd_wg_{tag}", stacked_out=True)
    d_wu = mm_tn(h, dup, _tile(D, 512, 128), Fs, f"ffn_bwd_wu_{tag}", stacked_out=True)
    dh = mm_nt([(dgate, wg), (dup, wu)], _tile(x.shape[0], 352), D, Fs, f"ffn_bwd_dh_{tag}", stacked_w=True)
    dx, dgain = rmsnorm_bwd(dh, x, gain, rstd, dout, f"ffn_norm_bwd_{tag}")
    return dx, dgain, d_wg, d_wu, d_wd


def kernel(x, meta, ffn_norm, ffn_w_gate, ffn_w_up, ffn_w_down, gla_norm, gla_w_in, gla_w_lr, gla_b_lr, gla_head_norm, gla_w_out, pool_norm, pool_w, pool_b, pool_scale, final_norm, loss_target, m_meta, m_ffn_norm, m_ffn_w_gate, m_ffn_w_up, m_ffn_w_down, m_gla_norm, m_gla_w_in, m_gla_w_lr, m_gla_b_lr, m_gla_head_norm, m_gla_w_out, m_pool_norm, m_pool_w, m_pool_b, m_pool_scale, m_final_norm, v_meta, v_ffn_norm, v_ffn_w_gate, v_ffn_w_up, v_ffn_w_down, v_gla_norm, v_gla_w_in, v_gla_w_lr, v_gla_b_lr, v_gla_head_norm, v_gla_w_out, v_pool_norm, v_pool_w, v_pool_b, v_pool_scale, v_final_norm):
    S, D = x.shape[1], x.shape[2]
    M = OFF + S
    Dq = D // N_CHIPS
    Fs = ffn_w_gate.shape[3]
    F = N_CHIPS * Fs
    dk = D // 2
    n_in = gla_w_in.shape[2]
    W = D // 4
    core = lax.axis_index("c").astype(jnp.int32).reshape(1)
    chip_id = 2 * lax.axis_index("x") + lax.axis_index("y")
    chip = chip_id.astype(jnp.int32).reshape(1)

    small = jnp.concatenate([_pad_rows(t) for t in (
        meta, ffn_norm.reshape(4, Dq), gla_w_lr.reshape(8, Dq), pool_norm, pool_b.reshape(1, Dq), pool_scale)],
        axis=0)
    halves = lambda w, n: w.astype(BF16).reshape(n, 2, -1, w.shape[-1])
    gathered = gather_weights(
        [halves(ffn_w_gate, 4), halves(ffn_w_up, 4), halves(ffn_w_down, 4), halves(gla_w_in, 1),
         halves(gla_w_out, 1), halves(pool_w, 1)], small, "gather_weights")
    wg = [t.reshape(N_CHIPS, D, Fs) for t in gathered[0:4]]
    wu = [t.reshape(N_CHIPS, D, Fs) for t in gathered[4:8]]
    wd = [t.reshape(F, D) for t in gathered[8:12]]
    w_in = gathered[12].reshape(N_CHIPS, D, n_in).transpose(1, 0, 2).reshape(D, N_CHIPS * n_in)
    w_out = gathered[13].reshape(D, D)
    w_pool = gathered[14].reshape(N_CHIPS, 4, W // N_CHIPS, W).transpose(1, 0, 2, 3).reshape(4, W, W)
    sm = gathered[15]
    unshard = lambda t: t.transpose(1, 0, 2).reshape(t.shape[1], D)
    meta_f = unshard(sm[:, 0:16])
    ffn_norm_f = unshard(sm[:, 16:20])
    w_lr_f = sm[:, 24:32].reshape(N_CHIPS, GATE_RANK, dk // N_CHIPS).transpose(1, 0, 2).reshape(GATE_RANK, dk)
    pool_norm_f = sm[:, 32].reshape(1, D)
    pool_b_f = sm[:, 40].reshape(N_CHIPS, 4, W // N_CHIPS).transpose(1, 0, 2).reshape(1, D)
    pool_scale_f = sm[:, 48].reshape(1, D)
    qkv = 2 * dk + D
    w_all = jnp.concatenate([w_in[:, :qkv], w_in[:, qkv + GATE_RANK:], w_in[:, qkv:qkv + GATE_RANK],
                             jnp.zeros((D, LR_W - GATE_RANK), BF16)], axis=1)
    wlr_pad = jnp.pad(w_lr_f.astype(BF16), ((0, LR_W - GATE_RANK), (0, 0)))
    final_g = final_norm.reshape(1, D)

    x0 = jnp.concatenate([jnp.zeros((PAD, D), F32), meta_f, x[0]], axis=0)
    target = jnp.pad(loss_target[0], ((OFF, 0), (0, 0)))
    x1, ffn0 = _ffn_fwd(x0, ffn_norm_f[0:1], wg[0], wu[0], wd[0], "0")
    hg, rstd_g = rmsnorm_fwd(x1, gla_norm, BF16, "gla_norm")
    proj = mm_nn(hg, w_all, F32, "gla_proj")
    o, st = gla_fwd(proj, wlr_pad, gla_b_lr, D)
    gated = gla_post_fwd(o, proj, gla_head_norm, D)
    x2 = mm_residual([gated], w_out, x1, 1.0, D, "gla_out")
    x3, ffn1 = _ffn_fwd(x2, ffn_norm_f[1:2], wg[1], wu[1], wd[1], "1")
    x4, ffn2 = _ffn_fwd(x3, ffn_norm_f[2:3], wg[2], wu[2], wd[2], "2")
    hp, rstd_p = rmsnorm_fwd(x4, pool_norm_f, F32, "pool_norm")
    pooled = pool_window(hp)
    x5 = pool_mix(pooled, x4, w_pool, pool_b_f, pool_scale_f)
    x6, ffn3 = _ffn_fwd(x5, ffn_norm_f[3:4], wg[3], wu[3], wd[3], "3")
    loss, dx6, d_final = final_loss(x6, final_g, target)

    dx5, dn3, dwg3, dwu3, dwd3 = _ffn_bwd(dx6, ffn3, wg[3], wu[3], wd[3], "3")
    dpooled, d_wpool, d_pool_b, d_pool_scale = pool_mix_bwd(dx5, pooled, w_pool, pool_b_f, pool_scale_f)
    dhp = pool_window_bwd(dpooled)
    dx4, d_pool_norm = rmsnorm_bwd(dhp, x4, pool_norm_f, rstd_p, dx5, "pool_norm_bwd")
    dx3, dn2, dwg2, dwu2, dwd2 = _ffn_bwd(dx4, ffn2, wg[2], wu[2], wd[2], "2")
    dx2, dn1, dwg1, dwu1, dwd1 = _ffn_bwd(dx3, ffn1, wg[1], wu[1], wd[1], "1")
    tm = _tile(M, 352)
    td = _tile(D, 512, 128)
    dgated = mm_nt([(dx2, w_out)], tm, td, D, "gla_out_bwd_act")
    d_wout = mm_tn(gated, dx2, td, td, "gla_out_bwd_w")
    do, dr, d_head_norm = gla_post_bwd(dgated, o, proj, gla_head_norm, D)
    dq, dkk, dv, dlr, dwlr, dblr = gla_bwd(proj, wlr_pad, gla_b_lr, st, do, D)
    dproj = jnp.concatenate([dq, dkk, dv, dr, dlr.astype(BF16)], axis=1)
    tp = _tile(proj.shape[1], 896, 128)
    dhg = mm_nt([(dproj, w_all)], tm, D, tp, "gla_proj_bwd_act")
    d_wall = mm_tn(hg, dproj, td, tp, "gla_proj_bwd_w")
    dx1, d_gla_norm = rmsnorm_bwd(dhg, x1, gla_norm, rstd_g, dx2, "gla_norm_bwd")
    dx0, dn0, dwg0, dwu0, dwd0 = _ffn_bwd(dx1, ffn0, wg[0], wu[0], wd[0], "0")

    d_win = jnp.concatenate([d_wall[:, :qkv], d_wall[:, qkv + D:qkv + D + GATE_RANK], d_wall[:, qkv:qkv + D]], axis=1)
    d_win = d_win.reshape(D, N_CHIPS, n_in).transpose(1, 0, 2)
    d_wpool = d_wpool.reshape(4, N_CHIPS, W // N_CHIPS, W).transpose(1, 0, 2, 3)
    cut = lambda t: t.reshape(N_CHIPS, 2, -1, t.shape[-1])
    grads = [cut(t) for t in (dwg0, dwg1, dwg2, dwg3, dwu0, dwu1, dwu2, dwu3, dwd0, dwd1, dwd2, dwd3,
                              d_win, d_wout, d_wpool)]
    g_gate, g_up, g_down, g_win, g_wout, g_wpool = reduce_scatter(grads, [4, 4, 4, 1, 1, 1], core, chip, "reduce")
    g_gate = g_gate.reshape(ffn_w_gate.shape)
    g_up = g_up.reshape(ffn_w_up.shape)
    g_down = g_down.reshape(ffn_w_down.shape)
    g_win = g_win.reshape(gla_w_in.shape)
    g_wout = g_wout.reshape(gla_w_out.shape)
    g_wpool = g_wpool.reshape(pool_w.shape)

    d_wlr = dwlr[:, :GATE_RANK].transpose(1, 0, 2).reshape(GATE_RANK, dk)
    pieces = [dx0[PAD:OFF], dn0, dn1, dn2, dn3, d_gla_norm, d_wlr,
              dblr.reshape(1, dk), d_head_norm, d_pool_norm, d_pool_b, d_pool_scale, d_final]
    packed = jnp.concatenate([_pad_rows(p.reshape(-1, Dq)) for p in pieces], axis=0)
    total = sum_devices(gather_devices(packed))
    sums, at = [], 0
    for p in pieces:
        r = p.size // Dq
        sums.append(total[at:at + r].reshape(p.shape))
        at += r + (-r % 8)
    (s_meta, s_n0, s_n1, s_n2, s_n3, s_gla_norm, s_wlr, s_blr, s_head_norm, s_pool_norm, s_pool_b, s_pool_scale,
     s_final) = sums
    s_ffn_norm = jnp.stack([s_n0, s_n1, s_n2, s_n3], axis=0)[:, 0]
    mine = lambda t, width: lax.dynamic_slice_in_dim(t, chip_id * width, width, axis=t.ndim - 1)
    g_meta = mine(s_meta, Dq)
    g_ffn_norm = mine(s_ffn_norm, Dq).reshape(ffn_norm.shape)
    g_gla_norm = s_gla_norm
    g_wlr = mine(s_wlr, dk // N_CHIPS).reshape(gla_w_lr.shape)
    g_blr = s_blr
    g_head_norm = s_head_norm
    g_pool_norm = mine(s_pool_norm, Dq)
    g_pool_b = mine(s_pool_b.reshape(4, W), W // N_CHIPS).reshape(pool_b.shape)
    g_pool_scale = mine(s_pool_scale, Dq)
    g_final = s_final.reshape(final_norm.shape)

    weights = [meta, ffn_norm, ffn_w_gate, ffn_w_up, ffn_w_down, gla_norm, gla_w_in, gla_w_lr, gla_b_lr,
               gla_head_norm, gla_w_out, pool_norm, pool_w, pool_b, pool_scale, final_norm]
    moments_m = [m_meta, m_ffn_norm, m_ffn_w_gate, m_ffn_w_up, m_ffn_w_down, m_gla_norm, m_gla_w_in, m_gla_w_lr,
                 m_gla_b_lr, m_gla_head_norm, m_gla_w_out, m_pool_norm, m_pool_w, m_pool_b, m_pool_scale,
                 m_final_norm]
    moments_v = [v_meta, v_ffn_norm, v_ffn_w_gate, v_ffn_w_up, v_ffn_w_down, v_gla_norm, v_gla_w_in, v_gla_w_lr,
                 v_gla_b_lr, v_gla_head_norm, v_gla_w_out, v_pool_norm, v_pool_w, v_pool_b, v_pool_scale,
                 v_final_norm]
    grads_w = [g_meta, g_ffn_norm, g_gate, g_up, g_down, g_gla_norm, g_win, g_wlr, g_blr, g_head_norm, g_wout,
               g_pool_norm, g_wpool, g_pool_b, g_pool_scale, g_final]
    deltas, new_m, new_v = [], [], []
    for i, (w, g, m, v) in enumerate(zip(weights, grads_w, moments_m, moments_v)):
        d, nm, nv = adamw(w, g, m, v, f"adamw_{i}")
        deltas.append(d)
        new_m.append(nm)
        new_v.append(nv)

    loss = lax.psum(loss[0, 0], ("x", "y", "c"))
    grad_x = dx0[OFF:][None]
    return (loss, grad_x, *grads_w, *deltas, *new_m, *new_v)
```

```python
import functools

import jax
import jax.numpy as jnp
from jax import lax
from jax.experimental import pallas as pl
from jax.experimental.pallas import tpu as pltpu

F32 = jnp.float32
BF16 = jnp.bfloat16
MESH = pl.DeviceIdType.MESH
ANY = pl.BlockSpec(memory_space=pl.ANY)

N_META = 16
CHUNK = 64
PAD = CHUNK - N_META
OFF = PAD + N_META
EPS = 1e-6
HEADS = 4
GATE_RANK = 16
GATE_NORM = 16.0
LR_W = 128
N_CHIPS = 4
N_DEV = 8
ADAM_LR, ADAM_B1, ADAM_B2, ADAM_EPS, ADAM_WD, ADAM_STEP = 0.001, 0.9, 0.999, 1e-08, 0.01, 10
VMEM_LIMIT = 56 * 1024 * 1024
ROW_TILE = 176


def _tile(n, target, mult=16):
    best = None
    for d in range(mult, min(n, target) + 1, mult):
        if n % d == 0:
            best = d
    return best if best is not None else n


def _params(*sem):
    return pltpu.CompilerParams(dimension_semantics=sem, vmem_limit_bytes=VMEM_LIMIT)


def _dot(a, b):
    return jnp.dot(a, b, preferred_element_type=F32)


def _dot_nt(a, b):
    return lax.dot_general(a, b, (((1,), (1,)), ((), ())), preferred_element_type=F32)


def _dot_tn(a, b):
    return lax.dot_general(a, b, (((0,), (0,)), ((), ())), preferred_element_type=F32)


def _sigmoid(x):
    return 1.0 / (1.0 + jnp.exp(-x))


def _rows(tile, width=1):
    return lax.broadcasted_iota(jnp.int32, (tile, width), 0)


def _pad_rows(t):
    return jnp.pad(t, ((0, -t.shape[0] % 8), (0, 0)))


def rmsnorm_fwd(x, g, out_dtype, name):
    M, D = x.shape
    tr = _tile(M, ROW_TILE)

    def body(x_ref, g_ref, h_ref, r_ref):
        xv = x_ref[...]
        r = lax.rsqrt(jnp.mean(xv * xv, axis=-1, keepdims=True) + EPS)
        h_ref[...] = (xv * r * g_ref[...]).astype(out_dtype)
        r_ref[...] = r

    return pl.pallas_call(
        body, name=name, grid=(M // tr,),
        in_specs=[pl.BlockSpec((tr, D), lambda i: (i, 0)), pl.BlockSpec((1, D), lambda i: (0, 0))],
        out_specs=[pl.BlockSpec((tr, D), lambda i: (i, 0)), pl.BlockSpec((tr, 1), lambda i: (i, 0))],
        out_shape=[jax.ShapeDtypeStruct((M, D), out_dtype), jax.ShapeDtypeStruct((M, 1), F32)],
        compiler_params=_params("parallel"),
    )(x, g)


def rmsnorm_bwd(dh, x, g, rstd, dres, name):
    M, D = x.shape
    tr = _tile(M, ROW_TILE)

    def body(dh_ref, x_ref, g_ref, r_ref, dres_ref, dx_ref, dg_ref):
        @pl.when(pl.program_id(0) == 0)
        def _():
            dg_ref[...] = jnp.zeros_like(dg_ref)

        r = r_ref[...]
        xhat = x_ref[...] * r
        dhv = dh_ref[...]
        gd = dhv * g_ref[...]
        dx_ref[...] = dres_ref[...] + r * (gd - xhat * jnp.mean(gd * xhat, axis=-1, keepdims=True))
        dg_ref[...] += jnp.sum(dhv * xhat, axis=0, keepdims=True)

    row = pl.BlockSpec((tr, D), lambda i: (i, 0))
    vec = pl.BlockSpec((1, D), lambda i: (0, 0))
    return pl.pallas_call(
        body, name=name, grid=(M // tr,),
        in_specs=[row, row, vec, pl.BlockSpec((tr, 1), lambda i: (i, 0)), row],
        out_specs=[row, vec],
        out_shape=[jax.ShapeDtypeStruct((M, D), F32), jax.ShapeDtypeStruct((1, D), F32)],
        compiler_params=_params("arbitrary"),
    )(dh, x, g, rstd, dres)


def final_loss(x, g, target):
    M, D = x.shape
    tr = _tile(M, ROW_TILE)

    def body(x_ref, g_ref, t_ref, loss_ref, dx_ref, dg_ref):
        i = pl.program_id(0)

        @pl.when(i == 0)
        def _():
            loss_ref[...] = jnp.zeros_like(loss_ref)
            dg_ref[...] = jnp.zeros_like(dg_ref)

        live = (_rows(tr) + i * tr) >= OFF
        xv = x_ref[...]
        gv = g_ref[...]
        r = lax.rsqrt(jnp.mean(xv * xv, axis=-1, keepdims=True) + EPS)
        xhat = xv * r
        err = jnp.where(live, xhat * gv - t_ref[...], 0.0)
        loss_ref[...] += 0.5 * jnp.sum(jnp.mean(err * err, axis=-1, keepdims=True), axis=0, keepdims=True)
        dy = err * (1.0 / D)
        gd = dy * gv
        dx_ref[...] = r * (gd - xhat * jnp.mean(gd * xhat, axis=-1, keepdims=True))
        dg_ref[...] += jnp.sum(dy * xhat, axis=0, keepdims=True)

    row = pl.BlockSpec((tr, D), lambda i: (i, 0))
    vec = pl.BlockSpec((1, D), lambda i: (0, 0))
    return pl.pallas_call(
        body, name="final_loss", grid=(M // tr,),
        in_specs=[row, vec, row],
        out_specs=[pl.BlockSpec((1, 1), lambda i: (0, 0)), row, vec],
        out_shape=[jax.ShapeDtypeStruct((1, 1), F32), jax.ShapeDtypeStruct((M, D), F32),
                   jax.ShapeDtypeStruct((1, D), F32)],
        compiler_params=_params("arbitrary"),
    )(x, g, target)


def mm_nn(a, w, out_dtype, name, tm_target=704, tn_target=896):
    M, K = a.shape
    N = w.shape[1]
    tm, tn = _tile(M, tm_target), _tile(N, tn_target, 128)

    def body(a_ref, w_ref, o_ref):
        o_ref[...] = _dot(a_ref[...], w_ref[...]).astype(out_dtype)

    return pl.pallas_call(
        body, name=name, grid=(N // tn, M // tm),
        in_specs=[pl.BlockSpec((tm, K), lambda n, i: (i, 0)), pl.BlockSpec((K, tn), lambda n, i: (0, n))],
        out_specs=pl.BlockSpec((tm, tn), lambda n, i: (i, n)),
        out_shape=jax.ShapeDtypeStruct((M, N), out_dtype),
        compiler_params=_params("parallel", "parallel"),
    )(a, w)


def ffn_gateup(h, wg, wu, name):
    M, D = h.shape
    Fs = wg.shape[2]
    tm = _tile(M, 352)

    def body(h_ref, wg_ref, wu_ref, g_ref, u_ref):
        hv = h_ref[...]
        g_ref[...] = _dot(hv, wg_ref[...]).astype(BF16)
        u_ref[...] = _dot(hv, wu_ref[...]).astype(BF16)

    wspec = pl.BlockSpec((None, D, Fs), lambda j, i: (j, 0, 0))
    ospec = pl.BlockSpec((tm, Fs), lambda j, i: (i, j))
    return pl.pallas_call(
        body, name=name, grid=(N_CHIPS, M // tm),
        in_specs=[pl.BlockSpec((tm, D), lambda j, i: (i, 0)), wspec, wspec],
        out_specs=[ospec, ospec],
        out_shape=[jax.ShapeDtypeStruct((M, N_CHIPS * Fs), BF16)] * 2,
        compiler_params=_params("parallel", "parallel"),
    )(h, wg, wu)


def mm_residual(acts, w, x, scale, tk, name):
    M, N = x.shape
    K = w.shape[0]
    tm = _tile(M, 352)
    swiglu = len(acts) == 2

    def body(*refs):
        a_refs, (w_ref, x_ref, o_ref, acc) = refs[:len(acts)], refs[len(acts):]
        k = pl.program_id(1)

        @pl.when(k == 0)
        def _():
            acc[...] = jnp.zeros_like(acc)

        if swiglu:
            gv = a_refs[0][...].astype(F32)
            av = (gv * _sigmoid(gv) * a_refs[1][...].astype(F32)).astype(BF16)
        else:
            av = a_refs[0][...]
        acc[...] += _dot(av, w_ref[...])

        @pl.when(k == pl.num_programs(1) - 1)
        def _():
            o_ref[...] = x_ref[...] + scale * acc[...]

    aspec = pl.BlockSpec((tm, tk), lambda i, k: (i, k))
    return pl.pallas_call(
        body, name=name, grid=(M // tm, K // tk),
        in_specs=[aspec] * len(acts) + [pl.BlockSpec((tk, N), lambda i, k: (k, 0)),
                                        pl.BlockSpec((tm, N), lambda i, k: (i, 0))],
        out_specs=pl.BlockSpec((tm, N), lambda i, k: (i, 0)),
        out_shape=jax.ShapeDtypeStruct((M, N), F32),
        scratch_shapes=[pltpu.VMEM((tm, N), F32)],
        compiler_params=_params("parallel", "arbitrary"),
    )(*acts, w, x)


def ffn_bwd_act(dout, wd, gate, up, name):
    M, D = dout.shape
    F = wd.shape[0]
    Fs = F // N_CHIPS
    tm = _tile(M, 352)

    def body(dy_ref, wd_ref, g_ref, u_ref, dg_ref, du_ref, a_ref):
        dy = (0.5 * dy_ref[...]).astype(BF16)
        da = _dot_nt(dy, wd_ref[...])
        gv = g_ref[...].astype(F32)
        uv = u_ref[...].astype(F32)
        s = _sigmoid(gv)
        silu = gv * s
        a_ref[...] = (silu * uv).astype(BF16)
        dg_ref[...] = (da * uv * (s * (1.0 + gv * (1.0 - s)))).astype(BF16)
        du_ref[...] = (da * silu).astype(BF16)

    fspec = pl.BlockSpec((tm, Fs), lambda j, i: (i, j))
    return pl.pallas_call(
        body, name=name, grid=(N_CHIPS, M // tm),
        in_specs=[pl.BlockSpec((tm, D), lambda j, i: (i, 0)), pl.BlockSpec((Fs, D), lambda j, i: (j, 0)),
                  fspec, fspec],
        out_specs=[fspec, fspec, fspec],
        out_shape=[jax.ShapeDtypeStruct((M, F), BF16)] * 3,
        compiler_params=_params("parallel", "parallel"),
    )(dout, wd, gate, up)


def mm_tn(a, b, ta, tb, name, b_scale=1.0, stacked_out=False, out_dtype=BF16):
    T, Ma = a.shape
    Nb = b.shape[1]

    def body(a_ref, b_ref, o_ref):
        bv = b_ref[...]
        if b_scale != 1.0:
            bv = b_scale * bv
        o_ref[...] = _dot_tn(a_ref[...], bv.astype(BF16)).astype(out_dtype)

    if stacked_out:
        out_spec = pl.BlockSpec((None, ta, tb), lambda jb, ja: (jb, ja, 0))
        out_shape = jax.ShapeDtypeStruct((Nb // tb, Ma, tb), out_dtype)
    else:
        out_spec = pl.BlockSpec((ta, tb), lambda jb, ja: (ja, jb))
        out_shape = jax.ShapeDtypeStruct((Ma, Nb), out_dtype)
    return pl.pallas_call(
        body, name=name, grid=(Nb // tb, Ma // ta),
        in_specs=[pl.BlockSpec((T, ta), lambda jb, ja: (0, ja)), pl.BlockSpec((T, tb), lambda jb, ja: (0, jb))],
        out_specs=out_spec, out_shape=out_shape,
        compiler_params=_params("parallel", "parallel"),
    )(a, b)


def mm_nt(pairs, tm, tn, tk, name, a_scale=1.0, stacked_w=False):
    M, K = pairs[0][0].shape
    N = pairs[0][1].shape[1] if stacked_w else pairs[0][1].shape[0]
    n_pairs = len(pairs)

    def body(*refs):
        o_ref, acc = refs[2 * n_pairs:]
        k = pl.program_id(2)

        @pl.when(k == 0)
        def _():
            acc[...] = jnp.zeros_like(acc)

        for p in range(n_pairs):
            av = refs[2 * p][...]
            if a_scale != 1.0:
                av = a_scale * av
            acc[...] += _dot_nt(av.astype(BF16), refs[2 * p + 1][...])

        @pl.when(k == pl.num_programs(2) - 1)
        def _():
            o_ref[...] = acc[...]

    aspec = pl.BlockSpec((tm, tk), lambda i, n, k: (i, k))
    if stacked_w:
        wspec = pl.BlockSpec((None, tn, tk), lambda i, n, k: (k, n, 0))
    else:
        wspec = pl.BlockSpec((tn, tk), lambda i, n, k: (n, k))
    return pl.pallas_call(
        body, name=name, grid=(M // tm, N // tn, K // tk),
        in_specs=[aspec, wspec] * n_pairs,
        out_specs=pl.BlockSpec((tm, tn), lambda i, n, k: (i, n)),
        out_shape=jax.ShapeDtypeStruct((M, N), F32),
        scratch_shapes=[pltpu.VMEM((tm, tn), F32)],
        compiler_params=_params("parallel", "parallel", "arbitrary"),
    )(*[t for pair in pairs for t in pair])


def _tri(lower):
    r = lax.broadcasted_iota(jnp.int32, (CHUNK, CHUNK), 0)
    c = lax.broadcasted_iota(jnp.int32, (CHUNK, CHUNK), 1)
    return (r >= c) if lower else (r <= c)


def _tri_sum(mask, x, pieces):
    ones = mask.astype(BF16)
    acc = jnp.zeros_like(x)
    rest = x
    for _ in range(pieces):
        piece = rest.astype(BF16)
        acc = acc + _dot(ones, piece)
        rest = rest - piece.astype(F32)
    return acc


def _gla_gates(lr_ref, wlr_ref, blr_ref, chunk):
    z = _dot(lr_ref[...].astype(BF16), wlr_ref[...]) + blr_ref[...]
    live = (_rows(CHUNK) + chunk * CHUNK) >= PAD
    lg = jnp.where(live, (jnp.minimum(z, 0.0) - jnp.log(1.0 + jnp.exp(-jnp.abs(z)))) * (1.0 / GATE_NORM), 0.0)
    b = _tri_sum(_tri(True), lg, 3)
    b_last = jnp.sum(lg, axis=0, keepdims=True)
    b_mid = jnp.sum(jnp.where(_rows(CHUNK) < CHUNK // 2, lg, 0.0), axis=0, keepdims=True)
    return z, live, b, b_last, b_mid


def _gla_specs(dkh, dvh, D, chunk_of):
    lr_blk = (3 * D) // LR_W
    return [
        pl.BlockSpec((CHUNK, dkh), lambda c, h: (chunk_of(c), h)),
        pl.BlockSpec((CHUNK, dkh), lambda c, h: (chunk_of(c), HEADS + h)),
        pl.BlockSpec((CHUNK, dvh), lambda c, h: (chunk_of(c), HEADS + h)),
        pl.BlockSpec((CHUNK, LR_W), lambda c, h: (chunk_of(c), lr_blk)),
        pl.BlockSpec((LR_W, dkh), lambda c, h: (0, h)),
        pl.BlockSpec((1, dkh), lambda c, h: (0, h)),
    ]


def gla_fwd(proj, wlr, blr, D):
    M = proj.shape[0]
    n = M // CHUNK
    dkh, dvh = D // 2 // HEADS, D // HEADS
    qscale = float(dkh) ** -0.5

    def body(q_ref, k_ref, v_ref, lr_ref, wlr_ref, blr_ref, o_ref, st_ref, S):
        c, h = pl.program_id(0), pl.program_id(1)

        @pl.when(c == 0)
        def _():
            S[h] = jnp.zeros((dvh, dkh), F32)

        _, _, b, b_last, b_mid = _gla_gates(lr_ref, wlr_ref, blr_ref, c)
        q = q_ref[...] * qscale
        k = k_ref[...]
        v = v_ref[...].astype(BF16)
        s0 = S[h]
        st_ref[...] = s0
        qb = (q * jnp.exp(b)).astype(BF16)
        kb = (k * jnp.exp(b_last - b)).astype(BF16)
        qt = (q * jnp.exp(b - b_mid)).astype(BF16)
        kt = (k * jnp.exp(b_mid - b)).astype(BF16)
        a = jnp.where(_tri(True), _dot_nt(qt, kt), 0.0).astype(BF16)
        o_ref[...] = _dot_nt(qb, s0.astype(BF16)) + _dot(a, v)
        S[h] = jnp.exp(b_last) * s0 + _dot_tn(v, kb)

    return pl.pallas_call(
        body, name="gla_fwd", grid=(n, HEADS),
        in_specs=_gla_specs(dkh, dvh, D, lambda c: c),
        out_specs=[pl.BlockSpec((CHUNK, dvh), lambda c, h: (c, h)),
                   pl.BlockSpec((None, None, dvh, dkh), lambda c, h: (c, h, 0, 0))],
        out_shape=[jax.ShapeDtypeStruct((M, D), F32), jax.ShapeDtypeStruct((n, HEADS, dvh, dkh), F32)],
        scratch_shapes=[pltpu.VMEM((HEADS, dvh, dkh), F32)],
        compiler_params=_params("arbitrary", "arbitrary"),
    )(proj, proj, proj, proj, wlr, blr)


def gla_bwd(proj, wlr, blr, st, do, D):
    M = proj.shape[0]
    n = M // CHUNK
    dkh, dvh = D // 2 // HEADS, D // HEADS
    qscale = float(dkh) ** -0.5
    rev = lambda c: n - 1 - c

    def body(q_ref, k_ref, v_ref, lr_ref, wlr_ref, blr_ref, st_ref, do_ref,
             dq_ref, dk_ref, dv_ref, dlr_ref, dwlr_ref, dblr_ref, dS, acc_w, acc_b):
        step, h = pl.program_id(0), pl.program_id(1)
        c = n - 1 - step

        @pl.when(step == 0)
        def _():
            dS[h] = jnp.zeros((dvh, dkh), F32)
            acc_w[h] = jnp.zeros((LR_W, dkh), F32)
            acc_b[h] = jnp.zeros((1, dkh), F32)

        z, live, b, b_last, b_mid = _gla_gates(lr_ref, wlr_ref, blr_ref, c)
        q = q_ref[...] * qscale
        k = k_ref[...]
        v = v_ref[...].astype(BF16)
        dov = do_ref[...].astype(BF16)
        s0 = st_ref[...]
        ds1 = dS[h]
        ds1b = ds1.astype(BF16)
        e_b, e_lb = jnp.exp(b), jnp.exp(b_last - b)
        e_bm, e_mb = jnp.exp(b - b_mid), jnp.exp(b_mid - b)
        e_last = jnp.exp(b_last)
        qb, kb, qt, kt = q * e_b, k * e_lb, q * e_bm, k * e_mb
        qbb, kbb, qtb, ktb = qb.astype(BF16), kb.astype(BF16), qt.astype(BF16), kt.astype(BF16)
        lower = _tri(True)
        a = jnp.where(lower, _dot_nt(qtb, ktb), 0.0).astype(BF16)
        da = jnp.where(lower, _dot_nt(dov, v), 0.0).astype(BF16)

        dqb = _dot(dov, s0.astype(BF16))
        dqt = _dot(da, ktb)
        dkt = _dot_tn(da, qtb)
        dkb = _dot(v, ds1b)
        keep = live.astype(F32)
        dv_ref[...] = (keep * (_dot_tn(a, dov) + _dot_nt(kbb, ds1b))).astype(BF16)
        dq_ref[...] = (keep * qscale * (dqb * e_b + dqt * e_bm)).astype(BF16)
        dk_ref[...] = (keep * (dkb * e_lb + dkt * e_mb)).astype(BF16)

        db = dqb * qb - dkb * kb + dqt * qt - dkt * kt
        db_last = (jnp.sum(dkb * kb, axis=0, keepdims=True)
                   + jnp.sum(ds1 * s0, axis=0, keepdims=True) * e_last)
        db = db + jnp.where(_rows(CHUNK) == CHUNK - 1, db_last, 0.0)
        dlg = jnp.where(live, _tri_sum(_tri(False), db, 2), 0.0)
        dz = dlg * (1.0 / GATE_NORM) / (1.0 + jnp.exp(z))
        dzb = dz.astype(BF16)

        dlr_h = _dot_nt(dzb, wlr_ref[...])

        @pl.when(h == 0)
        def _():
            dlr_ref[...] = dlr_h

        @pl.when(h > 0)
        def _():
            dlr_ref[...] += dlr_h

        acc_w[h] += _dot_tn(lr_ref[...].astype(BF16), dzb)
        acc_b[h] += jnp.sum(dz, axis=0, keepdims=True)
        dS[h] = e_last * ds1 + _dot_tn(dov, qbb)

        @pl.when(step == n - 1)
        def _():
            dwlr_ref[h] = acc_w[h]
            dblr_ref[h] = acc_b[h]

    return pl.pallas_call(
        body, name="gla_bwd", grid=(n, HEADS),
        in_specs=_gla_specs(dkh, dvh, D, rev) + [
            pl.BlockSpec((None, None, dvh, dkh), lambda c, h: (rev(c), h, 0, 0)),
            pl.BlockSpec((CHUNK, dvh), lambda c, h: (rev(c), h))],
        out_specs=[pl.BlockSpec((CHUNK, dkh), lambda c, h: (rev(c), h)),
                   pl.BlockSpec((CHUNK, dkh), lambda c, h: (rev(c), h)),
                   pl.BlockSpec((CHUNK, dvh), lambda c, h: (rev(c), h)),
                   pl.BlockSpec((CHUNK, LR_W), lambda c, h: (rev(c), 0)),
                   pl.BlockSpec((HEADS, LR_W, dkh), lambda c, h: (0, 0, 0)),
                   pl.BlockSpec((HEADS, 1, dkh), lambda c, h: (0, 0, 0))],
        out_shape=[jax.ShapeDtypeStruct((M, D // 2), BF16), jax.ShapeDtypeStruct((M, D // 2), BF16),
                   jax.ShapeDtypeStruct((M, D), BF16), jax.ShapeDtypeStruct((M, LR_W), F32),
                   jax.ShapeDtypeStruct((HEADS, LR_W, dkh), F32), jax.ShapeDtypeStruct((HEADS, 1, dkh), F32)],
        scratch_shapes=[pltpu.VMEM((HEADS, dvh, dkh), F32), pltpu.VMEM((HEADS, LR_W, dkh), F32),
                        pltpu.VMEM((HEADS, 1, dkh), F32)],
        compiler_params=_params("arbitrary", "arbitrary"),
    )(proj, proj, proj, proj, wlr, blr, st, do)


def gla_post_fwd(o, proj, head_norm, D):
    M = o.shape[0]
    dvh = D // HEADS
    tr = _tile(M, ROW_TILE)

    def body(o_ref, r_ref, hn_ref, out_ref):
        for hd in range(HEADS):
            cols = slice(hd * dvh, (hd + 1) * dvh)
            ov = o_ref[:, cols]
            rs = lax.rsqrt(jnp.mean(ov * ov, axis=-1, keepdims=True) + EPS)
            rv = r_ref[:, cols]
            out_ref[:, cols] = (ov * rs * hn_ref[...] * (rv * _sigmoid(rv))).astype(BF16)

    row = pl.BlockSpec((tr, D), lambda i: (i, 0))
    return pl.pallas_call(
        body, name="gla_post_fwd", grid=(M // tr,),
        in_specs=[row, pl.BlockSpec((tr, D), lambda i: (i, 2)), pl.BlockSpec((1, dvh), lambda i: (0, 0))],
        out_specs=row, out_shape=jax.ShapeDtypeStruct((M, D), BF16),
        compiler_params=_params("parallel"),
    )(o, proj, head_norm)


def gla_post_bwd(dgated, o, proj, head_norm, D):
    M = o.shape[0]
    dvh = D // HEADS
    tr = _tile(M, ROW_TILE)

    def body(dg_ref, o_ref, r_ref, hn_ref, do_ref, dr_ref, dhn_ref):
        @pl.when(pl.program_id(0) == 0)
        def _():
            dhn_ref[...] = jnp.zeros_like(dhn_ref)

        hn = hn_ref[...]
        dhn = jnp.zeros((1, dvh), F32)
        for hd in range(HEADS):
            cols = slice(hd * dvh, (hd + 1) * dvh)
            ov = o_ref[:, cols]
            rs = lax.rsqrt(jnp.mean(ov * ov, axis=-1, keepdims=True) + EPS)
            ohat = ov * rs
            rv = r_ref[:, cols]
            s = _sigmoid(rv)
            dgv = dg_ref[:, cols]
            don = dgv * (rv * s)
            dr_ref[:, cols] = (dgv * ohat * hn * (s * (1.0 + rv * (1.0 - s)))).astype(BF16)
            gd = don * hn
            do_ref[:, cols] = rs * (gd - ohat * jnp.mean(gd * ohat, axis=-1, keepdims=True))
            dhn = dhn + jnp.sum(don * ohat, axis=0, keepdims=True)
        dhn_ref[...] += dhn

    row = pl.BlockSpec((tr, D), lambda i: (i, 0))
    vec = pl.BlockSpec((1, dvh), lambda i: (0, 0))
    return pl.pallas_call(
        body, name="gla_post_bwd", grid=(M // tr,),
        in_specs=[row, row, pl.BlockSpec((tr, D), lambda i: (i, 2)), vec],
        out_specs=[row, row, vec],
        out_shape=[jax.ShapeDtypeStruct((M, D), F32), jax.ShapeDtypeStruct((M, D), BF16),
                   jax.ShapeDtypeStruct((1, dvh), F32)],
        compiler_params=_params("arbitrary"),
    )(dgated, o, proj, head_norm)


def _pool_counts(M, g):
    t = _rows(M) - PAD
    win = jnp.left_shift(2, g)
    return t >= 0, jnp.maximum(jnp.minimum(t + 1, win), 1).astype(F32)


def _window_sum(x, g, M, back):
    sums = []
    s = x
    for lvl in range(4):
        sh = 1 << lvl
        s = s + pltpu.roll(s, (M - sh) if back else sh, 0)
        sums.append(s)
    return jnp.where(g == 0, sums[0], jnp.where(g == 1, sums[1], jnp.where(g == 2, sums[2], sums[3])))


POOL_COLS = 128


def pool_window(hp):
    M, D = hp.shape
    cw = min(POOL_COLS, D // 4)
    per_group = (D // 4) // cw

    def body(h_ref, p_ref):
        g = pl.program_id(0) // per_group
        live, cnt = _pool_counts(M, g)
        hv = h_ref[...]
        p_ref[...] = jnp.where(live, _window_sum(hv, g, M, False) / cnt - hv, 0.0).astype(BF16)

    col = pl.BlockSpec((M, cw), lambda j: (0, j))
    return pl.pallas_call(
        body, name="pool_window", grid=(D // cw,), in_specs=[col], out_specs=col,
        out_shape=jax.ShapeDtypeStruct((M, D), BF16), compiler_params=_params("parallel"),
    )(hp)


def pool_window_bwd(dpooled):
    M, D = dpooled.shape
    cw = min(POOL_COLS, D // 4)
    per_group = (D // 4) // cw

    def body(d_ref, o_ref):
        g = pl.program_id(0) // per_group
        live, cnt = _pool_counts(M, g)
        dv = jnp.where(live, d_ref[...], 0.0)
        o_ref[...] = jnp.where(live, _window_sum(dv / cnt, g, M, True) - dv, 0.0)

    col = pl.BlockSpec((M, cw), lambda j: (0, j))
    return pl.pallas_call(
        body, name="pool_window_bwd", grid=(D // cw,), in_specs=[col], out_specs=col,
        out_shape=jax.ShapeDtypeStruct((M, D), F32), compiler_params=_params("parallel"),
    )(dpooled)


def pool_mix(pooled, x, w, bias, scale):
    M, D = x.shape
    W = D // 4
    tm = _tile(M, 352)

    def body(p_ref, x_ref, w_ref, b_ref, s_ref, out_ref):
        live = (_rows(tm) + pl.program_id(1) * tm) >= PAD
        y = (_dot(p_ref[...], w_ref[...]) + b_ref[...]) * s_ref[...]
        out_ref[...] = x_ref[...] + jnp.where(live, y, 0.0)

    blk = pl.BlockSpec((tm, W), lambda g, i: (i, g))
    vec = pl.BlockSpec((1, W), lambda g, i: (0, g))
    return pl.pallas_call(
        body, name="pool_mix", grid=(4, M // tm),
        in_specs=[blk, blk, pl.BlockSpec((None, W, W), lambda g, i: (g, 0, 0)), vec, vec],
        out_specs=blk, out_shape=jax.ShapeDtypeStruct((M, D), F32),
        compiler_params=_params("parallel", "parallel"),
    )(pooled, x, w, bias, scale)


def pool_mix_bwd(dy, pooled, w, bias, scale):
    M, D = dy.shape
    W = D // 4
    tm = _tile(M, 352)

    def body(dy_ref, p_ref, w_ref, b_ref, s_ref, dp_ref, dw_ref, db_ref, ds_ref, acc_w):
        i = pl.program_id(1)

        @pl.when(i == 0)
        def _():
            acc_w[...] = jnp.zeros_like(acc_w)
            db_ref[...] = jnp.zeros_like(db_ref)
            ds_ref[...] = jnp.zeros_like(ds_ref)

        live = (_rows(tm) + i * tm) >= PAD
        dyv = jnp.where(live, dy_ref[...], 0.0)
        pooled = p_ref[...]
        wv = w_ref[...]
        ds_ref[...] += jnp.sum(dyv * (_dot(pooled, wv) + b_ref[...]), axis=0, keepdims=True)
        dys = dyv * s_ref[...]
        db_ref[...] += jnp.sum(dys, axis=0, keepdims=True)
        dysb = dys.astype(BF16)
        acc_w[...] += _dot_tn(pooled, dysb)
        dp_ref[...] = _dot_nt(dysb, wv)

        @pl.when(i == pl.num_programs(1) - 1)
        def _():
            dw_ref[...] = acc_w[...].astype(BF16)

    blk = pl.BlockSpec((tm, W), lambda g, i: (i, g))
    vec = pl.BlockSpec((1, W), lambda g, i: (0, g))
    wspec = pl.BlockSpec((None, W, W), lambda g, i: (g, 0, 0))
    return pl.pallas_call(
        body, name="pool_mix_bwd", grid=(4, M // tm),
        in_specs=[blk, blk, wspec, vec, vec],
        out_specs=[blk, wspec, vec, vec],
        out_shape=[jax.ShapeDtypeStruct((M, D), F32), jax.ShapeDtypeStruct((4, W, W), BF16),
                   jax.ShapeDtypeStruct((1, D), F32), jax.ShapeDtypeStruct((1, D), F32)],
        scratch_shapes=[pltpu.VMEM((W, W), F32)],
        compiler_params=_params("parallel", "arbitrary"),
    )(dy, pooled, w, bias, scale)


def adamw(w, g, m, v, name):
    shape = w.shape
    C = shape[-1]
    R = w.size // C
    tr = _tile(R, 256, 8)

    def body(w_ref, g_ref, m_ref, v_ref, d_ref, nm_ref, nv_ref):
        gv = g_ref[...]
        nm = ADAM_B1 * m_ref[...] + (1.0 - ADAM_B1) * gv
        nv = ADAM_B2 * v_ref[...] + (1.0 - ADAM_B2) * (gv * gv)
        m_hat = nm / (1.0 - ADAM_B1 ** ADAM_STEP)
        v_hat = nv / (1.0 - ADAM_B2 ** ADAM_STEP)
        d_ref[...] = -ADAM_LR * (m_hat / (jnp.sqrt(v_hat) + ADAM_EPS) + ADAM_WD * w_ref[...])
        nm_ref[...] = nm
        nv_ref[...] = nv

    spec = pl.BlockSpec((tr, C), lambda i: (i, 0))
    outs = pl.pallas_call(
        body, name=name, grid=(R // tr,),
        in_specs=[spec] * 4, out_specs=[spec] * 3,
        out_shape=[jax.ShapeDtypeStruct((R, C), F32)] * 3,
        compiler_params=_params("parallel"),
    )(*[t.reshape(R, C) for t in (w, g, m, v)])
    return [t.reshape(shape) for t in outs]


def add_sibling(grad, recv, core, name):
    _, _, Rh, C = grad.shape
    tr = _tile(Rh, 512)

    def body(core_ref, g_ref, r_ref, o_ref):
        o_ref[...] = (g_ref[...].astype(F32) + r_ref[...].astype(F32)).astype(BF16)

    return pl.pallas_call(
        body, name=name,
        grid_spec=pltpu.PrefetchScalarGridSpec(
            num_scalar_prefetch=1, grid=(N_CHIPS, Rh // tr),
            in_specs=[pl.BlockSpec((None, None, tr, C), lambda j, i, core_ref: (j, core_ref[0], i, 0)),
                      pl.BlockSpec((None, tr, C), lambda j, i, core_ref: (j, i, 0))],
            out_specs=pl.BlockSpec((None, tr, C), lambda j, i, core_ref: (j, i, 0))),
        out_shape=jax.ShapeDtypeStruct((N_CHIPS, Rh, C), BF16),
        compiler_params=_params("parallel", "parallel"),
    )(core, grad, recv)


def add_chips(part, recv, chip, core, group, n, mi, name):
    _, Rh, C = part.shape
    tr = _tile(Rh, 512)

    def body(chip_ref, core_ref, p_ref, r_ref, *rest):
        o_ref = rest[-1]
        acc = p_ref[...].astype(F32)
        for k in range(N_CHIPS - 1):
            acc = acc + r_ref[k].astype(F32)
        o_ref[...] = acc

    carried = [] if group is None else [group]
    return pl.pallas_call(
        body, name=name,
        grid_spec=pltpu.PrefetchScalarGridSpec(
            num_scalar_prefetch=2, grid=(Rh // tr,),
            in_specs=[pl.BlockSpec((None, tr, C), lambda i, chip_ref, core_ref: (chip_ref[0], i, 0)),
                      pl.BlockSpec((N_CHIPS - 1, tr, C), lambda i, chip_ref, core_ref: (0, i, 0))]
            + [ANY] * len(carried),
            out_specs=pl.BlockSpec((None, None, tr, C), lambda i, chip_ref, core_ref: (mi, core_ref[0], i, 0))),
        out_shape=jax.ShapeDtypeStruct((n, 2, Rh, C), F32),
        input_output_aliases={4: 0} if carried else {},
        compiler_params=_params("parallel"),
    )(chip, core, part, recv, *carried)


def stage_shard(shard, mi, chip, name):
    _, _, Rh, C = shard.shape
    tr = _tile(Rh, 512)

    def body(chip_ref, s_ref, o_ref):
        o_ref[...] = s_ref[...].astype(BF16)

    return pl.pallas_call(
        body, name=name,
        grid_spec=pltpu.PrefetchScalarGridSpec(
            num_scalar_prefetch=1, grid=(2, Rh // tr),
            in_specs=[pl.BlockSpec((None, None, tr, C), lambda h, i, chip_ref: (mi, h, i, 0))],
            out_specs=pl.BlockSpec((None, None, tr, C), lambda h, i, chip_ref: (chip_ref[0], h, i, 0))),
        out_shape=jax.ShapeDtypeStruct((N_CHIPS, 2, Rh, C), BF16),
        compiler_params=_params("parallel", "parallel"),
    )(chip, shard)


def sum_devices(gathered):
    _, R, C = gathered.shape

    def body(g_ref, o_ref):
        acc = g_ref[0]
        for d in range(1, N_DEV):
            acc = acc + g_ref[d]
        o_ref[...] = acc

    return pl.pallas_call(
        body, name="sum_devices", grid=(1,),
        in_specs=[pl.BlockSpec((N_DEV, R, C), lambda i: (0, 0, 0))],
        out_specs=pl.BlockSpec((R, C), lambda i: (0, 0)),
        out_shape=jax.ShapeDtypeStruct((R, C), F32),
        compiler_params=_params("arbitrary"),
    )(gathered)


def _place():
    x, y, c = lax.axis_index("x"), lax.axis_index("y"), lax.axis_index("c")
    others = [(1 - x, y), (x, 1 - y), (1 - x, 1 - y)]
    return x, y, c, others


def _remote(src, dst, send_sems, recv_sems, idx, device):
    return pltpu.make_async_remote_copy(src_ref=src, dst_ref=dst, send_sem=send_sems.at[idx],
                                        recv_sem=recv_sems.at[idx], device_id=device, device_id_type=MESH)


def gather_weights(staged, small, name):
    n_mem = len(staged)
    has_small = small is not None
    n_copies = 6 * n_mem + (3 if has_small else 0)

    def body(*refs):
        small_ref = refs[n_mem] if has_small else None
        out_refs = refs[n_mem + has_small:2 * n_mem + 2 * has_small]
        send_sems, recv_sems, local_sem = refs[2 * n_mem + 2 * has_small:]
        x, y, c, others = _place()
        me = 2 * x + y
        sibling = (x, y, 1 - c)
        sends = []

        for t in range(n_mem):
            mine = out_refs[t].at[me, c]
            for k, chip in enumerate(others):
                cp = _remote(mine, mine, send_sems, recv_sems, 6 * t + k, (*chip, c))
                cp.start()
                sends.append(cp)
        if has_small:
            local = pltpu.make_async_copy(small_ref, out_refs[-1].at[me], local_sem)
            local.start()
            for k, chip in enumerate(others):
                cp = _remote(small_ref, out_refs[-1].at[me], send_sems, recv_sems, 6 * n_mem + k, (*chip, c))
                cp.start()
                sends.append(cp)

        for t in range(n_mem):
            for k, (ox, oy) in enumerate(others):
                theirs = out_refs[t].at[2 * ox + oy, c]
                _remote(theirs, theirs, send_sems, recv_sems, 6 * t + k, sibling).wait_recv()
                cp = _remote(theirs, theirs, send_sems, recv_sems, 6 * t + 3 + k, sibling)
                cp.start()
                sends.append(cp)
        for t in range(n_mem):
            for k, (ox, oy) in enumerate(others):
                theirs = out_refs[t].at[2 * ox + oy, 1 - c]
                _remote(theirs, theirs, send_sems, recv_sems, 6 * t + 3 + k, sibling).wait_recv()
        if has_small:
            for k, (ox, oy) in enumerate(others):
                theirs = out_refs[-1].at[2 * ox + oy]
                _remote(theirs, theirs, send_sems, recv_sems, 6 * n_mem + k, sibling).wait_recv()
            local.wait()
        for cp in sends:
            cp.wait_send()

    out_shape = [jax.ShapeDtypeStruct(s.shape, s.dtype) for s in staged]
    if has_small:
        out_shape.append(jax.ShapeDtypeStruct((N_CHIPS,) + small.shape, small.dtype))
    return pl.pallas_call(
        body, name=name, in_specs=[ANY] * (n_mem + has_small), out_specs=[ANY] * len(out_shape),
        out_shape=out_shape, input_output_aliases={t: t for t in range(n_mem)},
        scratch_shapes=[pltpu.SemaphoreType.DMA((n_copies,)), pltpu.SemaphoreType.DMA((n_copies,)),
                        pltpu.SemaphoreType.DMA],
    )(*staged, *([small] if has_small else []))


def send_halves_to_sibling(grads, name):
    n = len(grads)

    def body(*refs):
        in_refs, out_refs, (send_sems, recv_sems) = refs[:n], refs[n:2 * n], refs[2 * n:]
        x, y, c, _ = _place()
        sibling = (x, y, 1 - c)
        copies = []
        for t in range(n):
            for j in range(N_CHIPS):
                cp = _remote(in_refs[t].at[j, 1 - c], out_refs[t].at[j], send_sems, recv_sems,
                             N_CHIPS * t + j, sibling)
                cp.start()
                copies.append(cp)
        for cp in copies:
            cp.wait()

    return pl.pallas_call(
        body, name=name, in_specs=[ANY] * n, out_specs=[ANY] * n,
        out_shape=[jax.ShapeDtypeStruct((N_CHIPS,) + g.shape[2:], g.dtype) for g in grads],
        scratch_shapes=[pltpu.SemaphoreType.DMA((N_CHIPS * n,)), pltpu.SemaphoreType.DMA((N_CHIPS * n,))],
    )(*grads)


def send_parts_to_chips(parts, name):
    n = len(parts)

    def body(*refs):
        in_refs, out_refs, (send_sems, recv_sems) = refs[:n], refs[n:2 * n], refs[2 * n:]
        x, y, c, others = _place()
        copies = []
        for t in range(n):
            for k, (ox, oy) in enumerate(others):
                cp = _remote(in_refs[t].at[2 * ox + oy], out_refs[t].at[k], send_sems, recv_sems,
                             3 * t + k, (ox, oy, c))
                cp.start()
                copies.append(cp)
        for cp in copies:
            cp.wait()

    return pl.pallas_call(
        body, name=name, in_specs=[ANY] * n, out_specs=[ANY] * n,
        out_shape=[jax.ShapeDtypeStruct((N_CHIPS - 1,) + p.shape[1:], p.dtype) for p in parts],
        scratch_shapes=[pltpu.SemaphoreType.DMA((3 * n,)), pltpu.SemaphoreType.DMA((3 * n,))],
    )(*parts)


def exchange_halves(groups, name):
    n_groups = len(groups)
    slots = [(gi, mi) for gi, grp in enumerate(groups) for mi in range(grp.shape[0])]

    def body(*refs):
        out_refs = refs[n_groups:2 * n_groups]
        send_sems, recv_sems = refs[2 * n_groups:]
        x, y, c, _ = _place()
        sibling = (x, y, 1 - c)
        copies = []
        for t, (gi, mi) in enumerate(slots):
            mine = out_refs[gi].at[mi, c]
            cp = _remote(mine, mine, send_sems, recv_sems, t, sibling)
            cp.start()
            copies.append(cp)
        for t, (gi, mi) in enumerate(slots):
            theirs = out_refs[gi].at[mi, 1 - c]
            _remote(theirs, theirs, send_sems, recv_sems, t, sibling).wait_recv()
        for cp in copies:
            cp.wait_send()

    return pl.pallas_call(
        body, name=name, in_specs=[ANY] * n_groups, out_specs=[ANY] * n_groups,
        out_shape=[jax.ShapeDtypeStruct(g.shape, g.dtype) for g in groups],
        input_output_aliases={gi: gi for gi in range(n_groups)},
        scratch_shapes=[pltpu.SemaphoreType.DMA((len(slots),)), pltpu.SemaphoreType.DMA((len(slots),))],
    )(*groups)


def gather_devices(buf):
    def body(in_ref, out_ref, send_sems, recv_sems, local_sem):
        x, y, c, _ = _place()
        me = 4 * x + 2 * y + c
        local = pltpu.make_async_copy(in_ref, out_ref.at[me], local_sem)
        local.start()
        copies = []
        for k in range(1, N_DEV):
            fx, fy, fc = (k >> 2) & 1, (k >> 1) & 1, k & 1
            peer = (x ^ fx, y ^ fy, c ^ fc)
            cp = _remote(in_ref, out_ref.at[me], send_sems, recv_sems, k - 1, peer)
            cp.start()
            copies.append(cp)
        for k in range(1, N_DEV):
            fx, fy, fc = (k >> 2) & 1, (k >> 1) & 1, k & 1
            theirs = out_ref.at[4 * (x ^ fx) + 2 * (y ^ fy) + (c ^ fc)]
            _remote(theirs, theirs, send_sems, recv_sems, k - 1, (x, y, c)).wait_recv()
        for cp in copies:
            cp.wait_send()
        local.wait()

    return pl.pallas_call(
        body, name="gather_devices", in_specs=[ANY], out_specs=ANY,
        out_shape=jax.ShapeDtypeStruct((N_DEV,) + buf.shape, buf.dtype),
        scratch_shapes=[pltpu.SemaphoreType.DMA((N_DEV - 1,)), pltpu.SemaphoreType.DMA((N_DEV - 1,)),
                        pltpu.SemaphoreType.DMA],
    )(buf)


def reduce_scatter(grads, group_sizes, core, chip, name):
    recv = send_halves_to_sibling(grads, name + "_sibling")
    parts = [add_sibling(g, r, core, f"{name}_add_sibling{t}") for t, (g, r) in enumerate(zip(grads, recv))]
    recv = send_parts_to_chips(parts, name + "_chips")
    groups, t = [], 0
    for size in group_sizes:
        group = None
        for mi in range(size):
            group = add_chips(parts[t], recv[t], chip, core, group, size, mi, f"{name}_add_chips{t}")
            t += 1
        groups.append(group)
    return exchange_halves(groups, name + "_swap")


def _ffn_fwd(x, gain, wg, wu, wd, tag):
    h, rstd = rmsnorm_fwd(x, gain, BF16, f"ffn_norm_{tag}")
    gate, up = ffn_gateup(h, wg, wu, f"ffn_gateup_{tag}")
    out = mm_residual([gate, up], wd, x, 0.5, wd.shape[0] // N_CHIPS, f"ffn_down_{tag}")
    return out, (x, gain, h, rstd, gate, up)


def _ffn_bwd(dout, saved, wg, wu, wd, tag):
    x, gain, h, rstd, gate, up = saved
    D = x.shape[1]
    Fs = wg.shape[2]
    dgate, dup, act = ffn_bwd_act(dout, wd, gate, up, f"ffn_bwd_act_{tag}")
    d_wd = mm_tn(act, dout, Fs, _tile(D, 512, 128), f"ffn_bwd_wd_{tag}", b_scale=0.5)
    d_wg = mm_tn(h, dgate, _tile(D, 512, 128), Fs, f"ffn_bwd_wg_{tag}", stacked_out=True)
    d_wu = mm_tn(h, dup, _tile(D, 512, 128), Fs, f"ffn_bwd_wu_{tag}", stacked_out=True)
    dh = mm_nt([(dgate, wg), (dup, wu)], _tile(x.shape[0], 352), D, Fs, f"ffn_bwd_dh_{tag}", stacked_w=True)
    dx, dgain = rmsnorm_bwd(dh, x, gain, rstd, dout, f"ffn_norm_bwd_{tag}")
    return dx, dgain, d_wg, d_wu, d_wd


def kernel(x, meta, ffn_norm, ffn_w_gate, ffn_w_up, ffn_w_down, gla_norm, gla_w_in, gla_w_lr, gla_b_lr, gla_head_norm, gla_w_out, pool_norm, pool_w, pool_b, pool_scale, final_norm, loss_target, m_meta, m_ffn_norm, m_ffn_w_gate, m_ffn_w_up, m_ffn_w_down, m_gla_norm, m_gla_w_in, m_gla_w_lr, m_gla_b_lr, m_gla_head_norm, m_gla_w_out, m_pool_norm, m_pool_w, m_pool_b, m_pool_scale, m_final_norm, v_meta, v_ffn_norm, v_ffn_w_gate, v_ffn_w_up, v_ffn_w_down, v_gla_norm, v_gla_w_in, v_gla_w_lr, v_gla_b_lr, v_gla_head_norm, v_gla_w_out, v_pool_norm, v_pool_w, v_pool_b, v_pool_scale, v_final_norm):
    S, D = x.shape[1], x.shape[2]
    M = OFF + S
    Dq = D // N_CHIPS
    Fs = ffn_w_gate.shape[3]
    F = N_CHIPS * Fs
    dk = D // 2
    n_in = gla_w_in.shape[2]
    W = D // 4
    core = lax.axis_index("c").astype(jnp.int32).reshape(1)
    chip_id = 2 * lax.axis_index("x") + lax.axis_index("y")
    chip = chip_id.astype(jnp.int32).reshape(1)

    small = jnp.concatenate([_pad_rows(t) for t in (
        meta, ffn_norm.reshape(4, Dq), gla_w_lr.reshape(8, Dq), pool_norm, pool_b.reshape(1, Dq), pool_scale)],
        axis=0)
    staged = []
    for kind, (w, n) in enumerate([(ffn_w_gate, 4), (ffn_w_up, 4), (ffn_w_down, 4), (gla_w_in, 1),
                                   (gla_w_out, 1), (pool_w, 1)]):
        halves = w.reshape(n, 2, -1, w.shape[-1])
        staged += [stage_shard(halves, mi, chip,f"stage_{kind}_{mi}") for mi in range(n)]
    gathered = gather_weights(staged, small, "gather_weights")
    wg = [t.reshape(N_CHIPS, D, Fs) for t in gathered[0:4]]
    wu = [t.reshape(N_CHIPS, D, Fs) for t in gathered[4:8]]
    wd = [t.reshape(F, D) for t in gathered[8:12]]
    w_in = gathered[12].reshape(N_CHIPS, D, n_in).transpose(1, 0, 2).reshape(D, N_CHIPS * n_in)
    w_out = gathered[13].reshape(D, D)
    w_pool = gathered[14].reshape(N_CHIPS, 4, W // N_CHIPS, W).transpose(1, 0, 2, 3).reshape(4, W, W)
    sm = gathered[15]
    unshard = lambda t: t.transpose(1, 0, 2).reshape(t.shape[1], D)
    meta_f = unshard(sm[:, 0:16])
    ffn_norm_f = unshard(sm[:, 16:20])
    w_lr_f = sm[:, 24:32].reshape(N_CHIPS, GATE_RANK, dk // N_CHIPS).transpose(1, 0, 2).reshape(GATE_RANK, dk)
    pool_norm_f = sm[:, 32].reshape(1, D)
    pool_b_f = sm[:, 40].reshape(N_CHIPS, 4, W // N_CHIPS).transpose(1, 0, 2).reshape(1, D)
    pool_scale_f = sm[:, 48].reshape(1, D)
    qkv = 2 * dk + D
    w_all = jnp.concatenate([w_in[:, :qkv], w_in[:, qkv + GATE_RANK:], w_in[:, qkv:qkv + GATE_RANK],
                             jnp.zeros((D, LR_W - GATE_RANK), BF16)], axis=1)
    wlr_pad = jnp.pad(w_lr_f.astype(BF16), ((0, LR_W - GATE_RANK), (0, 0)))
    final_g = final_norm.reshape(1, D)

    x0 = jnp.concatenate([jnp.zeros((PAD, D), F32), meta_f, x[0]], axis=0)
    target = jnp.pad(loss_target[0], ((OFF, 0), (0, 0)))
    x1, ffn0 = _ffn_fwd(x0, ffn_norm_f[0:1], wg[0], wu[0], wd[0], "0")
    hg, rstd_g = rmsnorm_fwd(x1, gla_norm, BF16, "gla_norm")
    proj = mm_nn(hg, w_all, F32, "gla_proj")
    o, st = gla_fwd(proj, wlr_pad, gla_b_lr, D)
    gated = gla_post_fwd(o, proj, gla_head_norm, D)
    x2 = mm_residual([gated], w_out, x1, 1.0, D, "gla_out")
    x3, ffn1 = _ffn_fwd(x2, ffn_norm_f[1:2], wg[1], wu[1], wd[1], "1")
    x4, ffn2 = _ffn_fwd(x3, ffn_norm_f[2:3], wg[2], wu[2], wd[2], "2")
    hp, rstd_p = rmsnorm_fwd(x4, pool_norm_f, F32, "pool_norm")
    pooled = pool_window(hp)
    x5 = pool_mix(pooled, x4, w_pool, pool_b_f, pool_scale_f)
    x6, ffn3 = _ffn_fwd(x5, ffn_norm_f[3:4], wg[3], wu[3], wd[3], "3")
    loss, dx6, d_final = final_loss(x6, final_g, target)

    dx5, dn3, dwg3, dwu3, dwd3 = _ffn_bwd(dx6, ffn3, wg[3], wu[3], wd[3], "3")
    dpooled, d_wpool, d_pool_b, d_pool_scale = pool_mix_bwd(dx5, pooled, w_pool, pool_b_f, pool_scale_f)
    dhp = pool_window_bwd(dpooled)
    dx4, d_pool_norm = rmsnorm_bwd(dhp, x4, pool_norm_f, rstd_p, dx5, "pool_norm_bwd")
    dx3, dn2, dwg2, dwu2, dwd2 = _ffn_bwd(dx4, ffn2, wg[2], wu[2], wd[2], "2")
    dx2, dn1, dwg1, dwu1, dwd1 = _ffn_bwd(dx3, ffn1, wg[1], wu[1], wd[1], "1")
    tm = _tile(M, 352)
    td = _tile(D, 512, 128)
    dgated = mm_nt([(dx2, w_out)], tm, td, D, "gla_out_bwd_act")
    d_wout = mm_tn(gated, dx2, td, td, "gla_out_bwd_w")
    do, dr, d_head_norm = gla_post_bwd(dgated, o, proj, gla_head_norm, D)
    dq, dkk, dv, dlr, dwlr, dblr = gla_bwd(proj, wlr_pad, gla_b_lr, st, do, D)
    dproj = jnp.concatenate([dq, dkk, dv, dr, dlr.astype(BF16)], axis=1)
    tp = _tile(proj.shape[1], 896, 128)
    dhg = mm_nt([(dproj, w_all)], tm, D, tp, "gla_proj_bwd_act")
    d_wall = mm_tn(hg, dproj, td, tp, "gla_proj_bwd_w")
    dx1, d_gla_norm = rmsnorm_bwd(dhg, x1, gla_norm, rstd_g, dx2, "gla_norm_bwd")
    dx0, dn0, dwg0, dwu0, dwd0 = _ffn_bwd(dx1, ffn0, wg[0], wu[0], wd[0], "0")

    d_win = jnp.concatenate([d_wall[:, :qkv], d_wall[:, qkv + D:qkv + D + GATE_RANK], d_wall[:, qkv:qkv + D]], axis=1)
    d_win = d_win.reshape(D, N_CHIPS, n_in).transpose(1, 0, 2)
    d_wpool = d_wpool.reshape(4, N_CHIPS, W // N_CHIPS, W).transpose(1, 0, 2, 3)
    cut = lambda t: t.reshape(N_CHIPS, 2, -1, t.shape[-1])
    grads = [cut(t) for t in (dwg0, dwg1, dwg2, dwg3, dwu0, dwu1, dwu2, dwu3, dwd0, dwd1, dwd2, dwd3,
                              d_win, d_wout, d_wpool)]
    g_gate, g_up, g_down, g_win, g_wout, g_wpool = reduce_scatter(grads, [4, 4, 4, 1, 1, 1], core, chip, "reduce")
    g_gate = g_gate.reshape(ffn_w_gate.shape)
    g_up = g_up.reshape(ffn_w_up.shape)
    g_down = g_down.reshape(ffn_w_down.shape)
    g_win = g_win.reshape(gla_w_in.shape)
    g_wout = g_wout.reshape(gla_w_out.shape)
    g_wpool = g_wpool.reshape(pool_w.shape)

    d_wlr = dwlr[:, :GATE_RANK].transpose(1, 0, 2).reshape(GATE_RANK, dk)
    pieces = [dx0[PAD:OFF], dn0, dn1, dn2, dn3, d_gla_norm, d_wlr,
              dblr.reshape(1, dk), d_head_norm, d_pool_norm, d_pool_b, d_pool_scale, d_final]
    packed = jnp.concatenate([_pad_rows(p.reshape(-1, Dq)) for p in pieces], axis=0)
    total = sum_devices(gather_devices(packed))
    sums, at = [], 0
    for p in pieces:
        r = p.size // Dq
        sums.append(total[at:at + r].reshape(p.shape))
        at += r + (-r % 8)
    (s_meta, s_n0, s_n1, s_n2, s_n3, s_gla_norm, s_wlr, s_blr, s_head_norm, s_pool_norm, s_pool_b, s_pool_scale,
     s_final) = sums
    s_ffn_norm = jnp.stack([s_n0, s_n1, s_n2, s_n3], axis=0)[:, 0]
    mine = lambda t, width: lax.dynamic_slice_in_dim(t, chip_id * width, width, axis=t.ndim - 1)
    g_meta = mine(s_meta, Dq)
    g_ffn_norm = mine(s_ffn_norm, Dq).reshape(ffn_norm.shape)
    g_gla_norm = s_gla_norm
    g_wlr = mine(s_wlr, dk // N_CHIPS).reshape(gla_w_lr.shape)
    g_blr = s_blr
    g_head_norm = s_head_norm
    g_pool_norm = mine(s_pool_norm, Dq)
    g_pool_b = mine(s_pool_b.reshape(4, W), W // N_CHIPS).reshape(pool_b.shape)
    g_pool_scale = mine(s_pool_scale, Dq)
    g_final = s_final.reshape(final_norm.shape)

    weights = [meta, ffn_norm, ffn_w_gate, ffn_w_up, ffn_w_down, gla_norm, gla_w_in, gla_w_lr, gla_b_lr,
               gla_head_norm, gla_w_out, pool_norm, pool_w, pool_b, pool_scale, final_norm]
    moments_m = [m_meta, m_ffn_norm, m_ffn_w_gate, m_ffn_w_up, m_ffn_w_down, m_gla_norm, m_gla_w_in, m_gla_w_lr,
                 m_gla_b_lr, m_gla_head_norm, m_gla_w_out, m_pool_norm, m_pool_w, m_pool_b, m_pool_scale,
                 m_final_norm]
    moments_v = [v_meta, v_ffn_norm, v_ffn_w_gate, v_ffn_w_up, v_ffn_w_down, v_gla_norm, v_gla_w_in, v_gla_w_lr,
                 v_gla_b_lr, v_gla_head_norm, v_gla_w_out, v_pool_norm, v_pool_w, v_pool_b, v_pool_scale,
                 v_final_norm]
    grads_w = [g_meta, g_ffn_norm, g_gate, g_up, g_down, g_gla_norm, g_win, g_wlr, g_blr, g_head_norm, g_wout,
               g_pool_norm, g_wpool, g_pool_b, g_pool_scale, g_final]
    deltas, new_m, new_v = [], [], []
    for i, (w, g, m, v) in enumerate(zip(weights, grads_w, moments_m, moments_v)):
        d, nm, nv = adamw(w, g, m, v, f"adamw_{i}")
        deltas.append(d)
        new_m.append(nm)
        new_v.append(nv)

    loss = lax.psum(loss[0, 0], ("x", "y", "c"))
    grad_x = dx0[OFF:][None]
    return (loss, grad_x, *grads_w, *deltas, *new_m, *new_v)
```

```python
import functools

import jax
import jax.numpy as jnp
from jax import lax
from jax.experimental import pallas as pl
from jax.experimental.pallas import tpu as pltpu

F32 = jnp.float32
BF16 = jnp.bfloat16
MESH = pl.DeviceIdType.MESH
ANY = pl.BlockSpec(memory_space=pl.ANY)

N_META = 16
CHUNK = 64
PAD = CHUNK - N_META
OFF = PAD + N_META
EPS = 1e-6
HEADS = 4
GATE_RANK = 16
GATE_NORM = 16.0
LR_W = 128
N_CHIPS = 4
N_DEV = 8
ADAM_LR, ADAM_B1, ADAM_B2, ADAM_EPS, ADAM_WD, ADAM_STEP = 0.001, 0.9, 0.999, 1e-08, 0.01, 10
VMEM_LIMIT = 56 * 1024 * 1024
ROW_TILE = 176


def _tile(n, target, mult=16):
    best = None
    for d in range(mult, min(n, target) + 1, mult):
        if n % d == 0:
            best = d
    return best if best is not None else n


def _params(*sem):
    return pltpu.CompilerParams(dimension_semantics=sem, vmem_limit_bytes=VMEM_LIMIT)


def _dot(a, b):
    return jnp.dot(a, b, preferred_element_type=F32)


def _dot_nt(a, b):
    return lax.dot_general(a, b, (((1,), (1,)), ((), ())), preferred_element_type=F32)


def _dot_tn(a, b):
    return lax.dot_general(a, b, (((0,), (0,)), ((), ())), preferred_element_type=F32)


def _sigmoid(x):
    return 1.0 / (1.0 + jnp.exp(-x))


def _rows(tile, width=1):
    return lax.broadcasted_iota(jnp.int32, (tile, width), 0)


def _dep_specs(dep):
    return [] if dep is None else [ANY]


def _dep_args(dep):
    return [] if dep is None else [dep]


def _pad_rows(t):
    return jnp.pad(t, ((0, -t.shape[0] % 8), (0, 0)))


def rmsnorm_fwd(x, g, out_dtype, name):
    M, D = x.shape
    tr = _tile(M, ROW_TILE)

    def body(x_ref, g_ref, h_ref, r_ref):
        xv = x_ref[...]
        r = lax.rsqrt(jnp.mean(xv * xv, axis=-1, keepdims=True) + EPS)
        h_ref[...] = (xv * r * g_ref[...]).astype(out_dtype)
        r_ref[...] = r

    return pl.pallas_call(
        body, name=name, grid=(M // tr,),
        in_specs=[pl.BlockSpec((tr, D), lambda i: (i, 0)), pl.BlockSpec((1, D), lambda i: (0, 0))],
        out_specs=[pl.BlockSpec((tr, D), lambda i: (i, 0)), pl.BlockSpec((tr, 1), lambda i: (i, 0))],
        out_shape=[jax.ShapeDtypeStruct((M, D), out_dtype), jax.ShapeDtypeStruct((M, 1), F32)],
        compiler_params=_params("parallel"),
    )(x, g)


def rmsnorm_bwd(dh, x, g, rstd, dres, name):
    M, D = x.shape
    tr = _tile(M, ROW_TILE)

    def body(dh_ref, x_ref, g_ref, r_ref, dres_ref, dx_ref, dg_ref):
        @pl.when(pl.program_id(0) == 0)
        def _():
            dg_ref[...] = jnp.zeros_like(dg_ref)

        r = r_ref[...]
        xhat = x_ref[...] * r
        dhv = dh_ref[...]
        gd = dhv * g_ref[...]
        dx_ref[...] = dres_ref[...] + r * (gd - xhat * jnp.mean(gd * xhat, axis=-1, keepdims=True))
        dg_ref[...] += jnp.sum(dhv * xhat, axis=0, keepdims=True)

    row = pl.BlockSpec((tr, D), lambda i: (i, 0))
    vec = pl.BlockSpec((1, D), lambda i: (0, 0))
    return pl.pallas_call(
        body, name=name, grid=(M // tr,),
        in_specs=[row, row, vec, pl.BlockSpec((tr, 1), lambda i: (i, 0)), row],
        out_specs=[row, vec],
        out_shape=[jax.ShapeDtypeStruct((M, D), F32), jax.ShapeDtypeStruct((1, D), F32)],
        compiler_params=_params("arbitrary"),
    )(dh, x, g, rstd, dres)


def final_loss(x, g, target):
    M, D = x.shape
    tr = _tile(M, ROW_TILE)

    def body(x_ref, g_ref, t_ref, loss_ref, dx_ref, dg_ref):
        i = pl.program_id(0)

        @pl.when(i == 0)
        def _():
            loss_ref[...] = jnp.zeros_like(loss_ref)
            dg_ref[...] = jnp.zeros_like(dg_ref)

        live = (_rows(tr) + i * tr) >= OFF
        xv = x_ref[...]
        gv = g_ref[...]
        r = lax.rsqrt(jnp.mean(xv * xv, axis=-1, keepdims=True) + EPS)
        xhat = xv * r
        err = jnp.where(live, xhat * gv - t_ref[...], 0.0)
        loss_ref[...] += 0.5 * jnp.sum(jnp.mean(err * err, axis=-1, keepdims=True), axis=0, keepdims=True)
        dy = err * (1.0 / D)
        gd = dy * gv
        dx_ref[...] = r * (gd - xhat * jnp.mean(gd * xhat, axis=-1, keepdims=True))
        dg_ref[...] += jnp.sum(dy * xhat, axis=0, keepdims=True)

    row = pl.BlockSpec((tr, D), lambda i: (i, 0))
    vec = pl.BlockSpec((1, D), lambda i: (0, 0))
    return pl.pallas_call(
        body, name="final_loss", grid=(M // tr,),
        in_specs=[row, vec, row],
        out_specs=[pl.BlockSpec((1, 1), lambda i: (0, 0)), row, vec],
        out_shape=[jax.ShapeDtypeStruct((1, 1), F32), jax.ShapeDtypeStruct((M, D), F32),
                   jax.ShapeDtypeStruct((1, D), F32)],
        compiler_params=_params("arbitrary"),
    )(x, g, target)


def mm_nn(a, w, out_dtype, name, tm_target=704, tn_target=896):
    M, K = a.shape
    N = w.shape[1]
    tm, tn = _tile(M, tm_target), _tile(N, tn_target, 128)

    def body(a_ref, w_ref, o_ref):
        o_ref[...] = _dot(a_ref[...], w_ref[...]).astype(out_dtype)

    return pl.pallas_call(
        body, name=name, grid=(N // tn, M // tm),
        in_specs=[pl.BlockSpec((tm, K), lambda n, i: (i, 0)), pl.BlockSpec((K, tn), lambda n, i: (0, n))],
        out_specs=pl.BlockSpec((tm, tn), lambda n, i: (i, n)),
        out_shape=jax.ShapeDtypeStruct((M, N), out_dtype),
        compiler_params=_params("parallel", "parallel"),
    )(a, w)


def ffn_gateup(h, wg, wu, name):
    M, D = h.shape
    Fs = wg.shape[2]
    tm = _tile(M, 352)

    def body(h_ref, wg_ref, wu_ref, g_ref, u_ref):
        hv = h_ref[...]
        g_ref[...] = _dot(hv, wg_ref[...]).astype(BF16)
        u_ref[...] = _dot(hv, wu_ref[...]).astype(BF16)

    wspec = pl.BlockSpec((None, D, Fs), lambda j, i: (j, 0, 0))
    ospec = pl.BlockSpec((tm, Fs), lambda j, i: (i, j))
    return pl.pallas_call(
        body, name=name, grid=(N_CHIPS, M // tm),
        in_specs=[pl.BlockSpec((tm, D), lambda j, i: (i, 0)), wspec, wspec],
        out_specs=[ospec, ospec],
        out_shape=[jax.ShapeDtypeStruct((M, N_CHIPS * Fs), BF16)] * 2,
        compiler_params=_params("parallel", "parallel"),
    )(h, wg, wu)


def mm_residual(acts, w, x, scale, tk, name):
    M, N = x.shape
    K = w.shape[0]
    tm = _tile(M, 352)
    swiglu = len(acts) == 2

    def body(*refs):
        a_refs, (w_ref, x_ref, o_ref, acc) = refs[:len(acts)], refs[len(acts):]
        k = pl.program_id(1)

        @pl.when(k == 0)
        def _():
            acc[...] = jnp.zeros_like(acc)

        if swiglu:
            gv = a_refs[0][...].astype(F32)
            av = (gv * _sigmoid(gv) * a_refs[1][...].astype(F32)).astype(BF16)
        else:
            av = a_refs[0][...]
        acc[...] += _dot(av, w_ref[...])

        @pl.when(k == pl.num_programs(1) - 1)
        def _():
            o_ref[...] = x_ref[...] + scale * acc[...]

    aspec = pl.BlockSpec((tm, tk), lambda i, k: (i, k))
    return pl.pallas_call(
        body, name=name, grid=(M // tm, K // tk),
        in_specs=[aspec] * len(acts) + [pl.BlockSpec((tk, N), lambda i, k: (k, 0)),
                                        pl.BlockSpec((tm, N), lambda i, k: (i, 0))],
        out_specs=pl.BlockSpec((tm, N), lambda i, k: (i, 0)),
        out_shape=jax.ShapeDtypeStruct((M, N), F32),
        scratch_shapes=[pltpu.VMEM((tm, N), F32)],
        compiler_params=_params("parallel", "arbitrary"),
    )(*acts, w, x)


def ffn_bwd_act(dout, wd, gate, up, name, dep=None):
    M, D = dout.shape
    F = wd.shape[0]
    Fs = F // N_CHIPS
    tm = _tile(M, 352)

    def body(dy_ref, wd_ref, g_ref, u_ref, *rest):
        dg_ref, du_ref, a_ref = rest[-3:]
        dy = (0.5 * dy_ref[...]).astype(BF16)
        da = _dot_nt(dy, wd_ref[...])
        gv = g_ref[...].astype(F32)
        uv = u_ref[...].astype(F32)
        s = _sigmoid(gv)
        silu = gv * s
        a_ref[...] = (silu * uv).astype(BF16)
        dg_ref[...] = (da * uv * (s * (1.0 + gv * (1.0 - s)))).astype(BF16)
        du_ref[...] = (da * silu).astype(BF16)

    fspec = pl.BlockSpec((tm, Fs), lambda j, i: (i, j))
    return pl.pallas_call(
        body, name=name, grid=(N_CHIPS, M // tm),
        in_specs=[pl.BlockSpec((tm, D), lambda j, i: (i, 0)), pl.BlockSpec((Fs, D), lambda j, i: (j, 0)),
                  fspec, fspec] + _dep_specs(dep),
        out_specs=[fspec, fspec, fspec],
        out_shape=[jax.ShapeDtypeStruct((M, F), BF16)] * 3,
        compiler_params=_params("parallel", "parallel"),
    )(dout, wd, gate, up, *_dep_args(dep))


def mm_tn(a, b, ta, tb, name, b_scale=1.0, stacked_out=False, out_dtype=BF16):
    T, Ma = a.shape
    Nb = b.shape[1]

    def body(a_ref, b_ref, o_ref):
        bv = b_ref[...]
        if b_scale != 1.0:
            bv = b_scale * bv
        o_ref[...] = _dot_tn(a_ref[...], bv.astype(BF16)).astype(out_dtype)

    if stacked_out:
        out_spec = pl.BlockSpec((None, ta, tb), lambda jb, ja: (jb, ja, 0))
        out_shape = jax.ShapeDtypeStruct((Nb // tb, Ma, tb), out_dtype)
    else:
        out_spec = pl.BlockSpec((ta, tb), lambda jb, ja: (ja, jb))
        out_shape = jax.ShapeDtypeStruct((Ma, Nb), out_dtype)
    return pl.pallas_call(
        body, name=name, grid=(Nb // tb, Ma // ta),
        in_specs=[pl.BlockSpec((T, ta), lambda jb, ja: (0, ja)), pl.BlockSpec((T, tb), lambda jb, ja: (0, jb))],
        out_specs=out_spec, out_shape=out_shape,
        compiler_params=_params("parallel", "parallel"),
    )(a, b)


def mm_nt(pairs, tm, tn, tk, name, a_scale=1.0, stacked_w=False, dep=None):
    M, K = pairs[0][0].shape
    N = pairs[0][1].shape[1] if stacked_w else pairs[0][1].shape[0]
    n_pairs = len(pairs)

    def body(*refs):
        o_ref, acc = refs[-2:]
        k = pl.program_id(2)

        @pl.when(k == 0)
        def _():
            acc[...] = jnp.zeros_like(acc)

        for p in range(n_pairs):
            av = refs[2 * p][...]
            if a_scale != 1.0:
                av = a_scale * av
            acc[...] += _dot_nt(av.astype(BF16), refs[2 * p + 1][...])

        @pl.when(k == pl.num_programs(2) - 1)
        def _():
            o_ref[...] = acc[...]

    aspec = pl.BlockSpec((tm, tk), lambda i, n, k: (i, k))
    if stacked_w:
        wspec = pl.BlockSpec((None, tn, tk), lambda i, n, k: (k, n, 0))
    else:
        wspec = pl.BlockSpec((tn, tk), lambda i, n, k: (n, k))
    return pl.pallas_call(
        body, name=name, grid=(M // tm, N // tn, K // tk),
        in_specs=[aspec, wspec] * n_pairs + _dep_specs(dep),
        out_specs=pl.BlockSpec((tm, tn), lambda i, n, k: (i, n)),
        out_shape=jax.ShapeDtypeStruct((M, N), F32),
        scratch_shapes=[pltpu.VMEM((tm, tn), F32)],
        compiler_params=_params("parallel", "parallel", "arbitrary"),
    )(*[t for pair in pairs for t in pair], *_dep_args(dep))


def _tri(lower):
    r = lax.broadcasted_iota(jnp.int32, (CHUNK, CHUNK), 0)
    c = lax.broadcasted_iota(jnp.int32, (CHUNK, CHUNK), 1)
    return (r >= c) if lower else (r <= c)


def _tri_sum(mask, x, pieces):
    ones = mask.astype(BF16)
    acc = jnp.zeros_like(x)
    rest = x
    for _ in range(pieces):
        piece = rest.astype(BF16)
        acc = acc + _dot(ones, piece)
        rest = rest - piece.astype(F32)
    return acc


def _gla_gates(lr_ref, wlr_ref, blr_ref, chunk):
    z = _dot(lr_ref[...].astype(BF16), wlr_ref[...]) + blr_ref[...]
    live = (_rows(CHUNK) + chunk * CHUNK) >= PAD
    lg = jnp.where(live, (jnp.minimum(z, 0.0) - jnp.log(1.0 + jnp.exp(-jnp.abs(z)))) * (1.0 / GATE_NORM), 0.0)
    b = _tri_sum(_tri(True), lg, 3)
    b_last = jnp.sum(lg, axis=0, keepdims=True)
    b_mid = jnp.sum(jnp.where(_rows(CHUNK) < CHUNK // 2, lg, 0.0), axis=0, keepdims=True)
    return z, live, b, b_last, b_mid


def _gla_specs(dkh, dvh, D, chunk_of):
    lr_blk = (3 * D) // LR_W
    return [
        pl.BlockSpec((CHUNK, dkh), lambda c, h: (chunk_of(c), h)),
        pl.BlockSpec((CHUNK, dkh), lambda c, h: (chunk_of(c), HEADS + h)),
        pl.BlockSpec((CHUNK, dvh), lambda c, h: (chunk_of(c), HEADS + h)),
        pl.BlockSpec((CHUNK, LR_W), lambda c, h: (chunk_of(c), lr_blk)),
        pl.BlockSpec((LR_W, dkh), lambda c, h: (0, h)),
        pl.BlockSpec((1, dkh), lambda c, h: (0, h)),
    ]


def gla_fwd(proj, wlr, blr, D):
    M = proj.shape[0]
    n = M // CHUNK
    dkh, dvh = D // 2 // HEADS, D // HEADS
    qscale = float(dkh) ** -0.5

    def body(q_ref, k_ref, v_ref, lr_ref, wlr_ref, blr_ref, o_ref, st_ref, S):
        c, h = pl.program_id(0), pl.program_id(1)

        @pl.when(c == 0)
        def _():
            S[h] = jnp.zeros((dvh, dkh), F32)

        _, _, b, b_last, b_mid = _gla_gates(lr_ref, wlr_ref, blr_ref, c)
        q = q_ref[...] * qscale
        k = k_ref[...]
        v = v_ref[...].astype(BF16)
        s0 = S[h]
        st_ref[...] = s0
        qb = (q * jnp.exp(b)).astype(BF16)
        kb = (k * jnp.exp(b_last - b)).astype(BF16)
        qt = (q * jnp.exp(b - b_mid)).astype(BF16)
        kt = (k * jnp.exp(b_mid - b)).astype(BF16)
        a = jnp.where(_tri(True), _dot_nt(qt, kt), 0.0).astype(BF16)
        o_ref[...] = _dot_nt(qb, s0.astype(BF16)) + _dot(a, v)
        S[h] = jnp.exp(b_last) * s0 + _dot_tn(v, kb)

    return pl.pallas_call(
        body, name="gla_fwd", grid=(n, HEADS),
        in_specs=_gla_specs(dkh, dvh, D, lambda c: c),
        out_specs=[pl.BlockSpec((CHUNK, dvh), lambda c, h: (c, h)),
                   pl.BlockSpec((None, None, dvh, dkh), lambda c, h: (c, h, 0, 0))],
        out_shape=[jax.ShapeDtypeStruct((M, D), F32), jax.ShapeDtypeStruct((n, HEADS, dvh, dkh), F32)],
        scratch_shapes=[pltpu.VMEM((HEADS, dvh, dkh), F32)],
        compiler_params=_params("arbitrary", "arbitrary"),
    )(proj, proj, proj, proj, wlr, blr)


def gla_bwd(proj, wlr, blr, st, do, D):
    M = proj.shape[0]
    n = M // CHUNK
    dkh, dvh = D // 2 // HEADS, D // HEADS
    qscale = float(dkh) ** -0.5
    rev = lambda c: n - 1 - c

    def body(q_ref, k_ref, v_ref, lr_ref, wlr_ref, blr_ref, st_ref, do_ref,
             dq_ref, dk_ref, dv_ref, dlr_ref, dwlr_ref, dblr_ref, dS, acc_w, acc_b):
        step, h = pl.program_id(0), pl.program_id(1)
        c = n - 1 - step

        @pl.when(step == 0)
        def _():
            dS[h] = jnp.zeros((dvh, dkh), F32)
            acc_w[h] = jnp.zeros((LR_W, dkh), F32)
            acc_b[h] = jnp.zeros((1, dkh), F32)

        z, live, b, b_last, b_mid = _gla_gates(lr_ref, wlr_ref, blr_ref, c)
        q = q_ref[...] * qscale
        k = k_ref[...]
        v = v_ref[...].astype(BF16)
        dov = do_ref[...].astype(BF16)
        s0 = st_ref[...]
        ds1 = dS[h]
        ds1b = ds1.astype(BF16)
        e_b, e_lb = jnp.exp(b), jnp.exp(b_last - b)
        e_bm, e_mb = jnp.exp(b - b_mid), jnp.exp(b_mid - b)
        e_last = jnp.exp(b_last)
        qb, kb, qt, kt = q * e_b, k * e_lb, q * e_bm, k * e_mb
        qbb, kbb, qtb, ktb = qb.astype(BF16), kb.astype(BF16), qt.astype(BF16), kt.astype(BF16)
        lower = _tri(True)
        a = jnp.where(lower, _dot_nt(qtb, ktb), 0.0).astype(BF16)
        da = jnp.where(lower, _dot_nt(dov, v), 0.0).astype(BF16)

        dqb = _dot(dov, s0.astype(BF16))
        dqt = _dot(da, ktb)
        dkt = _dot_tn(da, qtb)
        dkb = _dot(v, ds1b)
        keep = live.astype(F32)
        dv_ref[...] = (keep * (_dot_tn(a, dov) + _dot_nt(kbb, ds1b))).astype(BF16)
        dq_ref[...] = (keep * qscale * (dqb * e_b + dqt * e_bm)).astype(BF16)
        dk_ref[...] = (keep * (dkb * e_lb + dkt * e_mb)).astype(BF16)

        db = dqb * qb - dkb * kb + dqt * qt - dkt * kt
        db_last = (jnp.sum(dkb * kb, axis=0, keepdims=True)
                   + jnp.sum(ds1 * s0, axis=0, keepdims=True) * e_last)
        db = db + jnp.where(_rows(CHUNK) == CHUNK - 1, db_last, 0.0)
        dlg = jnp.where(live, _tri_sum(_tri(False), db, 2), 0.0)
        dz = dlg * (1.0 / GATE_NORM) / (1.0 + jnp.exp(z))
        dzb = dz.astype(BF16)

        dlr_h = _dot_nt(dzb, wlr_ref[...])

        @pl.when(h == 0)
        def _():
            dlr_ref[...] = dlr_h

        @pl.when(h > 0)
        def _():
            dlr_ref[...] += dlr_h

        acc_w[h] += _dot_tn(lr_ref[...].astype(BF16), dzb)
        acc_b[h] += jnp.sum(dz, axis=0, keepdims=True)
        dS[h] = e_last * ds1 + _dot_tn(dov, qbb)

        @pl.when(step == n - 1)
        def _():
            dwlr_ref[h] = acc_w[h]
            dblr_ref[h] = acc_b[h]

    return pl.pallas_call(
        body, name="gla_bwd", grid=(n, HEADS),
        in_specs=_gla_specs(dkh, dvh, D, rev) + [
            pl.BlockSpec((None, None, dvh, dkh), lambda c, h: (rev(c), h, 0, 0)),
            pl.BlockSpec((CHUNK, dvh), lambda c, h: (rev(c), h))],
        out_specs=[pl.BlockSpec((CHUNK, dkh), lambda c, h: (rev(c), h)),
                   pl.BlockSpec((CHUNK, dkh), lambda c, h: (rev(c), h)),
                   pl.BlockSpec((CHUNK, dvh), lambda c, h: (rev(c), h)),
                   pl.BlockSpec((CHUNK, LR_W), lambda c, h: (rev(c), 0)),
                   pl.BlockSpec((HEADS, LR_W, dkh), lambda c, h: (0, 0, 0)),
                   pl.BlockSpec((HEADS, 1, dkh), lambda c, h: (0, 0, 0))],
        out_shape=[jax.ShapeDtypeStruct((M, D // 2), BF16), jax.ShapeDtypeStruct((M, D // 2), BF16),
                   jax.ShapeDtypeStruct((M, D), BF16), jax.ShapeDtypeStruct((M, LR_W), F32),
                   jax.ShapeDtypeStruct((HEADS, LR_W, dkh), F32), jax.ShapeDtypeStruct((HEADS, 1, dkh), F32)],
        scratch_shapes=[pltpu.VMEM((HEADS, dvh, dkh), F32), pltpu.VMEM((HEADS, LR_W, dkh), F32),
                        pltpu.VMEM((HEADS, 1, dkh), F32)],
        compiler_params=_params("arbitrary", "arbitrary"),
    )(proj, proj, proj, proj, wlr, blr, st, do)


def gla_post_fwd(o, proj, head_norm, D):
    M = o.shape[0]
    dvh = D // HEADS
    tr = _tile(M, ROW_TILE)

    def body(o_ref, r_ref, hn_ref, out_ref):
        for hd in range(HEADS):
            cols = slice(hd * dvh, (hd + 1) * dvh)
            ov = o_ref[:, cols]
            rs = lax.rsqrt(jnp.mean(ov * ov, axis=-1, keepdims=True) + EPS)
            rv = r_ref[:, cols]
            out_ref[:, cols] = (ov * rs * hn_ref[...] * (rv * _sigmoid(rv))).astype(BF16)

    row = pl.BlockSpec((tr, D), lambda i: (i, 0))
    return pl.pallas_call(
        body, name="gla_post_fwd", grid=(M // tr,),
        in_specs=[row, pl.BlockSpec((tr, D), lambda i: (i, 2)), pl.BlockSpec((1, dvh), lambda i: (0, 0))],
        out_specs=row, out_shape=jax.ShapeDtypeStruct((M, D), BF16),
        compiler_params=_params("parallel"),
    )(o, proj, head_norm)


def gla_post_bwd(dgated, o, proj, head_norm, D):
    M = o.shape[0]
    dvh = D // HEADS
    tr = _tile(M, ROW_TILE)

    def body(dg_ref, o_ref, r_ref, hn_ref, do_ref, dr_ref, dhn_ref):
        @pl.when(pl.program_id(0) == 0)
        def _():
            dhn_ref[...] = jnp.zeros_like(dhn_ref)

        hn = hn_ref[...]
        dhn = jnp.zeros((1, dvh), F32)
        for hd in range(HEADS):
            cols = slice(hd * dvh, (hd + 1) * dvh)
            ov = o_ref[:, cols]
            rs = lax.rsqrt(jnp.mean(ov * ov, axis=-1, keepdims=True) + EPS)
            ohat = ov * rs
            rv = r_ref[:, cols]
            s = _sigmoid(rv)
            dgv = dg_ref[:, cols]
            don = dgv * (rv * s)
            dr_ref[:, cols] = (dgv * ohat * hn * (s * (1.0 + rv * (1.0 - s)))).astype(BF16)
            gd = don * hn
            do_ref[:, cols] = rs * (gd - ohat * jnp.mean(gd * ohat, axis=-1, keepdims=True))
            dhn = dhn + jnp.sum(don * ohat, axis=0, keepdims=True)
        dhn_ref[...] += dhn

    row = pl.BlockSpec((tr, D), lambda i: (i, 0))
    vec = pl.BlockSpec((1, dvh), lambda i: (0, 0))
    return pl.pallas_call(
        body, name="gla_post_bwd", grid=(M // tr,),
        in_specs=[row, row, pl.BlockSpec((tr, D), lambda i: (i, 2)), vec],
        out_specs=[row, row, vec],
        out_shape=[jax.ShapeDtypeStruct((M, D), F32), jax.ShapeDtypeStruct((M, D), BF16),
                   jax.ShapeDtypeStruct((1, dvh), F32)],
        compiler_params=_params("arbitrary"),
    )(dgated, o, proj, head_norm)


def _pool_counts(M, g):
    t = _rows(M) - PAD
    win = jnp.left_shift(2, g)
    return t >= 0, jnp.maximum(jnp.minimum(t + 1, win), 1).astype(F32)


def _window_sum(x, g, M, back):
    sums = []
    s = x
    for lvl in range(4):
        sh = 1 << lvl
        s = s + pltpu.roll(s, (M - sh) if back else sh, 0)
        sums.append(s)
    return jnp.where(g == 0, sums[0], jnp.where(g == 1, sums[1], jnp.where(g == 2, sums[2], sums[3])))


POOL_COLS = 128


def pool_window(hp):
    M, D = hp.shape
    cw = min(POOL_COLS, D // 4)
    per_group = (D // 4) // cw

    def body(h_ref, p_ref):
        g = pl.program_id(0) // per_group
        live, cnt = _pool_counts(M, g)
        hv = h_ref[...]
        p_ref[...] = jnp.where(live, _window_sum(hv, g, M, False) / cnt - hv, 0.0).astype(BF16)

    col = pl.BlockSpec((M, cw), lambda j: (0, j))
    return pl.pallas_call(
        body, name="pool_window", grid=(D // cw,), in_specs=[col], out_specs=col,
        out_shape=jax.ShapeDtypeStruct((M, D), BF16), compiler_params=_params("parallel"),
    )(hp)


def pool_window_bwd(dpooled):
    M, D = dpooled.shape
    cw = min(POOL_COLS, D // 4)
    per_group = (D // 4) // cw

    def body(d_ref, o_ref):
        g = pl.program_id(0) // per_group
        live, cnt = _pool_counts(M, g)
        dv = jnp.where(live, d_ref[...], 0.0)
        o_ref[...] = jnp.where(live, _window_sum(dv / cnt, g, M, True) - dv, 0.0)

    col = pl.BlockSpec((M, cw), lambda j: (0, j))
    return pl.pallas_call(
        body, name="pool_window_bwd", grid=(D // cw,), in_specs=[col], out_specs=col,
        out_shape=jax.ShapeDtypeStruct((M, D), F32), compiler_params=_params("parallel"),
    )(dpooled)


def pool_mix(pooled, x, w, bias, scale):
    M, D = x.shape
    W = D // 4
    tm = _tile(M, 352)

    def body(p_ref, x_ref, w_ref, b_ref, s_ref, out_ref):
        live = (_rows(tm) + pl.program_id(1) * tm) >= PAD
        y = (_dot(p_ref[...], w_ref[...]) + b_ref[...]) * s_ref[...]
        out_ref[...] = x_ref[...] + jnp.where(live, y, 0.0)

    blk = pl.BlockSpec((tm, W), lambda g, i: (i, g))
    vec = pl.BlockSpec((1, W), lambda g, i: (0, g))
    return pl.pallas_call(
        body, name="pool_mix", grid=(4, M // tm),
        in_specs=[blk, blk, pl.BlockSpec((None, W, W), lambda g, i: (g, 0, 0)), vec, vec],
        out_specs=blk, out_shape=jax.ShapeDtypeStruct((M, D), F32),
        compiler_params=_params("parallel", "parallel"),
    )(pooled, x, w, bias, scale)


def pool_mix_bwd(dy, pooled, w, bias, scale, dep=None):
    M, D = dy.shape
    W = D // 4
    tm = _tile(M, 352)

    def body(dy_ref, p_ref, w_ref, b_ref, s_ref, *rest):
        dp_ref, dw_ref, db_ref, ds_ref, acc_w = rest[-5:]
        i = pl.program_id(1)

        @pl.when(i == 0)
        def _():
            acc_w[...] = jnp.zeros_like(acc_w)
            db_ref[...] = jnp.zeros_like(db_ref)
            ds_ref[...] = jnp.zeros_like(ds_ref)

        live = (_rows(tm) + i * tm) >= PAD
        dyv = jnp.where(live, dy_ref[...], 0.0)
        pooled = p_ref[...]
        wv = w_ref[...]
        ds_ref[...] += jnp.sum(dyv * (_dot(pooled, wv) + b_ref[...]), axis=0, keepdims=True)
        dys = dyv * s_ref[...]
        db_ref[...] += jnp.sum(dys, axis=0, keepdims=True)
        dysb = dys.astype(BF16)
        acc_w[...] += _dot_tn(pooled, dysb)
        dp_ref[...] = _dot_nt(dysb, wv)

        @pl.when(i == pl.num_programs(1) - 1)
        def _():
            dw_ref[...] = acc_w[...].astype(BF16)

    blk = pl.BlockSpec((tm, W), lambda g, i: (i, g))
    vec = pl.BlockSpec((1, W), lambda g, i: (0, g))
    wspec = pl.BlockSpec((None, W, W), lambda g, i: (g, 0, 0))
    return pl.pallas_call(
        body, name="pool_mix_bwd", grid=(4, M // tm),
        in_specs=[blk, blk, wspec, vec, vec] + _dep_specs(dep),
        out_specs=[blk, wspec, vec, vec],
        out_shape=[jax.ShapeDtypeStruct((M, D), F32), jax.ShapeDtypeStruct((4, W, W), BF16),
                   jax.ShapeDtypeStruct((1, D), F32), jax.ShapeDtypeStruct((1, D), F32)],
        scratch_shapes=[pltpu.VMEM((W, W), F32)],
        compiler_params=_params("parallel", "arbitrary"),
    )(dy, pooled, w, bias, scale, *_dep_args(dep))


def adamw(w, g, m, v, name):
    shape = w.shape
    C = shape[-1]
    R = w.size // C
    tr = _tile(R, 256, 8)

    def body(w_ref, g_ref, m_ref, v_ref, d_ref, nm_ref, nv_ref):
        gv = g_ref[...]
        nm = ADAM_B1 * m_ref[...] + (1.0 - ADAM_B1) * gv
        nv = ADAM_B2 * v_ref[...] + (1.0 - ADAM_B2) * (gv * gv)
        m_hat = nm / (1.0 - ADAM_B1 ** ADAM_STEP)
        v_hat = nv / (1.0 - ADAM_B2 ** ADAM_STEP)
        d_ref[...] = -ADAM_LR * (m_hat / (jnp.sqrt(v_hat) + ADAM_EPS) + ADAM_WD * w_ref[...])
        nm_ref[...] = nm
        nv_ref[...] = nv

    spec = pl.BlockSpec((tr, C), lambda i: (i, 0))
    outs = pl.pallas_call(
        body, name=name, grid=(R // tr,),
        in_specs=[spec] * 4, out_specs=[spec] * 3,
        out_shape=[jax.ShapeDtypeStruct((R, C), F32)] * 3,
        compiler_params=_params("parallel"),
    )(*[t.reshape(R, C) for t in (w, g, m, v)])
    return [t.reshape(shape) for t in outs]


def add_sibling(grad, recv, core, name):
    _, _, Rh, C = grad.shape
    tr = _tile(Rh, 512)

    def body(core_ref, g_ref, r_ref, o_ref):
        o_ref[...] = (g_ref[...].astype(F32) + r_ref[...].astype(F32)).astype(BF16)

    return pl.pallas_call(
        body, name=name,
        grid_spec=pltpu.PrefetchScalarGridSpec(
            num_scalar_prefetch=1, grid=(N_CHIPS, Rh // tr),
            in_specs=[pl.BlockSpec((None, None, tr, C), lambda j, i, core_ref: (j, core_ref[0], i, 0)),
                      pl.BlockSpec((None, tr, C), lambda j, i, core_ref: (j, i, 0))],
            out_specs=pl.BlockSpec((None, tr, C), lambda j, i, core_ref: (j, i, 0))),
        out_shape=jax.ShapeDtypeStruct((N_CHIPS, Rh, C), BF16),
        compiler_params=_params("parallel", "parallel"),
    )(core, grad, recv)


def add_chips(part, recv, chip, core, group, n, mi, name):
    _, Rh, C = part.shape
    tr = _tile(Rh, 512)

    def body(chip_ref, core_ref, p_ref, r_ref, *rest):
        o_ref = rest[-1]
        acc = p_ref[...].astype(F32)
        for k in range(N_CHIPS - 1):
            acc = acc + r_ref[k].astype(F32)
        o_ref[...] = acc

    carried = [] if group is None else [group]
    return pl.pallas_call(
        body, name=name,
        grid_spec=pltpu.PrefetchScalarGridSpec(
            num_scalar_prefetch=2, grid=(Rh // tr,),
            in_specs=[pl.BlockSpec((None, tr, C), lambda i, chip_ref, core_ref: (chip_ref[0], i, 0)),
                      pl.BlockSpec((N_CHIPS - 1, tr, C), lambda i, chip_ref, core_ref: (0, i, 0))]
            + [ANY] * len(carried),
            out_specs=pl.BlockSpec((None, None, tr, C), lambda i, chip_ref, core_ref: (mi, core_ref[0], i, 0))),
        out_shape=jax.ShapeDtypeStruct((n, 2, Rh, C), F32),
        input_output_aliases={4: 0} if carried else {},
        compiler_params=_params("parallel"),
    )(chip, core, part, recv, *carried)


def stage_shard(shard, mi, chip, name):
    _, _, Rh, C = shard.shape
    tr = _tile(Rh, 512)

    def body(chip_ref, s_ref, o_ref):
        o_ref[...] = s_ref[...].astype(BF16)

    return pl.pallas_call(
        body, name=name,
        grid_spec=pltpu.PrefetchScalarGridSpec(
            num_scalar_prefetch=1, grid=(2, Rh // tr),
            in_specs=[pl.BlockSpec((None, None, tr, C), lambda h, i, chip_ref: (mi, h, i, 0))],
            out_specs=pl.BlockSpec((None, None, tr, C), lambda h, i, chip_ref: (chip_ref[0], h, i, 0))),
        out_shape=jax.ShapeDtypeStruct((N_CHIPS, 2, Rh, C), BF16),
        compiler_params=_params("parallel", "parallel"),
    )(chip, shard)


def sum_devices(gathered):
    _, R, C = gathered.shape

    def body(g_ref, o_ref):
        acc = g_ref[0]
        for d in range(1, N_DEV):
            acc = acc + g_ref[d]
        o_ref[...] = acc

    return pl.pallas_call(
        body, name="sum_devices", grid=(1,),
        in_specs=[pl.BlockSpec((N_DEV, R, C), lambda i: (0, 0, 0))],
        out_specs=pl.BlockSpec((R, C), lambda i: (0, 0)),
        out_shape=jax.ShapeDtypeStruct((R, C), F32),
        compiler_params=_params("arbitrary"),
    )(gathered)


def _place():
    x, y, c = lax.axis_index("x"), lax.axis_index("y"), lax.axis_index("c")
    others = [(1 - x, y), (x, 1 - y), (1 - x, 1 - y)]
    return x, y, c, others


def _remote(src, dst, send_sems, recv_sems, idx, device):
    return pltpu.make_async_remote_copy(src_ref=src, dst_ref=dst, send_sem=send_sems.at[idx],
                                        recv_sem=recv_sems.at[idx], device_id=device, device_id_type=MESH)


HBM = pl.BlockSpec(memory_space=pltpu.HBM)
SEM = pl.BlockSpec(memory_space=pltpu.SEMAPHORE)
EFFECT = pltpu.SideEffectType.DATAFLOW_SIDE_EFFECTING


def _in_hbm(t):
    return pltpu.with_memory_space_constraint(t, pltpu.HBM)


def _own_slice(buf, me, c):
    return buf.at[me, c] if len(buf.shape) == 4 else buf.at[me]


def gather_start(staged, bucket_sizes, name):
    n, nb = len(staged), len(bucket_sizes)

    def body(*refs):
        in_refs, sems, token = refs[:n], refs[n:n + 2 * nb], refs[-1]
        x, y, c, others = _place()
        me = 2 * x + y
        t = 0
        for b, size in enumerate(bucket_sizes):
            for i in range(size):
                mine = _own_slice(in_refs[t], me, c)
                for k, chip in enumerate(others):
                    _remote(mine, mine, sems[2 * b], sems[2 * b + 1], 3 * i + k, (*chip, c)).start()
                t += 1
        token[...] = jnp.zeros_like(token)

    sem_shapes = [pltpu.SemaphoreType.DMA((3 * size,)) for size in bucket_sizes for _ in range(2)]
    outs = pl.pallas_call(
        body, name=name,
        out_shape=sem_shapes + [pltpu.HBM(s.shape, s.dtype) for s in staged] + [jax.ShapeDtypeStruct((8, 128), F32)],
        in_specs=[HBM] * n, out_specs=[SEM] * (2 * nb) + [HBM] * n + [pl.BlockSpec(memory_space=pltpu.VMEM)],
        input_output_aliases={t: 2 * nb + t for t in range(n)},
        compiler_params=pltpu.CompilerParams(has_side_effects=EFFECT),
    )(*[_in_hbm(s) for s in staged])
    sems = [(outs[2 * b], outs[2 * b + 1]) for b in range(nb)]
    return sems, list(outs[2 * nb:2 * nb + n]), outs[-1]


def gather_wait(bufs, sems, after, name):
    n = len(bufs)

    def body(*refs):
        in_refs, send_sems, recv_sems = refs[:n], refs[n], refs[n + 1]
        x, y, c, others = _place()
        me = 2 * x + y
        for i in range(n):
            mine = _own_slice(in_refs[i], me, c)
            for k, (ox, oy) in enumerate(others):
                cp = _remote(mine, _own_slice(in_refs[i], 2 * ox + oy, c), send_sems, recv_sems, 3 * i + k,
                             (ox, oy, c))
                cp.wait_send()
                cp.wait_recv()

    return pl.pallas_call(
        body, name=name, out_shape=[pltpu.HBM(b.shape, b.dtype) for b in bufs],
        in_specs=[HBM] * n + [SEM, SEM, ANY], out_specs=[HBM] * n,
        input_output_aliases={t: t for t in range(n)},
        compiler_params=pltpu.CompilerParams(has_side_effects=EFFECT),
    )(*bufs, *sems, after)


def forward_to_sibling(bufs, name):
    n = len(bufs)

    def body(*refs):
        out_refs, (send_sems, recv_sems) = refs[n:2 * n], refs[2 * n:]
        x, y, c, others = _place()
        sibling = (x, y, 1 - c)
        copies = []
        for t in range(n):
            for k, (ox, oy) in enumerate(others):
                mine = out_refs[t].at[2 * ox + oy, c]
                cp = _remote(mine, mine, send_sems, recv_sems, 3 * t + k, sibling)
                cp.start()
                copies.append(cp)
        for t in range(n):
            for k, (ox, oy) in enumerate(others):
                theirs = out_refs[t].at[2 * ox + oy, 1 - c]
                _remote(theirs, theirs, send_sems, recv_sems, 3 * t + k, sibling).wait_recv()
        for cp in copies:
            cp.wait_send()

    return pl.pallas_call(
        body, name=name, in_specs=[ANY] * n, out_specs=[ANY] * n,
        out_shape=[jax.ShapeDtypeStruct(b.shape, b.dtype) for b in bufs],
        input_output_aliases={t: t for t in range(n)},
        scratch_shapes=[pltpu.SemaphoreType.DMA((3 * n,)), pltpu.SemaphoreType.DMA((3 * n,))],
    )(*bufs)


def send_halves_to_sibling(grads, name):
    n = len(grads)

    def body(*refs):
        in_refs, out_refs, (send_sems, recv_sems) = refs[:n], refs[n:2 * n], refs[2 * n:]
        x, y, c, _ = _place()
        sibling = (x, y, 1 - c)
        copies = []
        for t in range(n):
            for j in range(N_CHIPS):
                cp = _remote(in_refs[t].at[j, 1 - c], out_refs[t].at[j], send_sems, recv_sems,
                             N_CHIPS * t + j, sibling)
                cp.start()
                copies.append(cp)
        for cp in copies:
            cp.wait()

    return pl.pallas_call(
        body, name=name, in_specs=[ANY] * n, out_specs=[ANY] * n,
        out_shape=[jax.ShapeDtypeStruct((N_CHIPS,) + g.shape[2:], g.dtype) for g in grads],
        scratch_shapes=[pltpu.SemaphoreType.DMA((N_CHIPS * n,)), pltpu.SemaphoreType.DMA((N_CHIPS * n,))],
    )(*grads)


def reduce_start(parts, name):
    n = len(parts)
    lands = [lax.empty((N_CHIPS - 1,) + p.shape[1:], p.dtype) for p in parts]

    def body(*refs):
        in_refs, land_refs, send_sems, recv_sems, token = refs[:n], refs[n:2 * n], refs[2 * n], refs[2 * n + 1], refs[-1]
        x, y, c, others = _place()
        for t in range(n):
            for k, (ox, oy) in enumerate(others):
                _remote(in_refs[t].at[2 * ox + oy], land_refs[t].at[k], send_sems, recv_sems, 3 * t + k,
                        (ox, oy, c)).start()
        token[...] = jnp.zeros_like(token)

    outs = pl.pallas_call(
        body, name=name,
        out_shape=[pltpu.SemaphoreType.DMA((3 * n,))] * 2 + [pltpu.HBM(t.shape, t.dtype) for t in parts + lands]
        + [jax.ShapeDtypeStruct((8, 128), F32)],
        in_specs=[HBM] * (2 * n), out_specs=[SEM, SEM] + [HBM] * (2 * n) + [pl.BlockSpec(memory_space=pltpu.VMEM)],
        input_output_aliases={t: 2 + t for t in range(2 * n)},
        compiler_params=pltpu.CompilerParams(has_side_effects=EFFECT),
    )(*[_in_hbm(t) for t in parts + lands])
    return (outs[0], outs[1]), list(outs[2:2 + n]), list(outs[2 + n:2 + 2 * n]), outs[-1]


def reduce_wait(parts, lands, sems, after, name):
    n = len(parts)

    def body(*refs):
        in_refs, land_refs, send_sems, recv_sems = refs[:n], refs[n:2 * n], refs[2 * n], refs[2 * n + 1]
        x, y, c, others = _place()
        for t in range(n):
            for k, (ox, oy) in enumerate(others):
                cp = _remote(in_refs[t].at[2 * ox + oy], land_refs[t].at[k], send_sems, recv_sems, 3 * t + k,
                             (ox, oy, c))
                cp.wait_send()
                cp.wait_recv()

    outs = pl.pallas_call(
        body, name=name, out_shape=[pltpu.HBM(t.shape, t.dtype) for t in parts + lands],
        in_specs=[HBM] * (2 * n) + [SEM, SEM, ANY], out_specs=[HBM] * (2 * n),
        input_output_aliases={t: t for t in range(2 * n)},
        compiler_params=pltpu.CompilerParams(has_side_effects=EFFECT),
    )(*parts, *lands, *sems, after)
    return list(outs[:n]), list(outs[n:])


def exchange_halves(groups, name):
    n_groups = len(groups)
    slots = [(gi, mi) for gi, grp in enumerate(groups) for mi in range(grp.shape[0])]

    def body(*refs):
        out_refs = refs[n_groups:2 * n_groups]
        send_sems, recv_sems = refs[2 * n_groups:]
        x, y, c, _ = _place()
        sibling = (x, y, 1 - c)
        copies = []
        for t, (gi, mi) in enumerate(slots):
            mine = out_refs[gi].at[mi, c]
            cp = _remote(mine, mine, send_sems, recv_sems, t, sibling)
            cp.start()
            copies.append(cp)
        for t, (gi, mi) in enumerate(slots):
            theirs = out_refs[gi].at[mi, 1 - c]
            _remote(theirs, theirs, send_sems, recv_sems, t, sibling).wait_recv()
        for cp in copies:
            cp.wait_send()

    return pl.pallas_call(
        body, name=name, in_specs=[ANY] * n_groups, out_specs=[ANY] * n_groups,
        out_shape=[jax.ShapeDtypeStruct(g.shape, g.dtype) for g in groups],
        input_output_aliases={gi: gi for gi in range(n_groups)},
        scratch_shapes=[pltpu.SemaphoreType.DMA((len(slots),)), pltpu.SemaphoreType.DMA((len(slots),))],
    )(*groups)


def gather_devices(buf):
    def body(in_ref, out_ref, send_sems, recv_sems, local_sem):
        x, y, c, _ = _place()
        me = 4 * x + 2 * y + c
        local = pltpu.make_async_copy(in_ref, out_ref.at[me], local_sem)
        local.start()
        copies = []
        for k in range(1, N_DEV):
            fx, fy, fc = (k >> 2) & 1, (k >> 1) & 1, k & 1
            peer = (x ^ fx, y ^ fy, c ^ fc)
            cp = _remote(in_ref, out_ref.at[me], send_sems, recv_sems, k - 1, peer)
            cp.start()
            copies.append(cp)
        for k in range(1, N_DEV):
            fx, fy, fc = (k >> 2) & 1, (k >> 1) & 1, k & 1
            theirs = out_ref.at[4 * (x ^ fx) + 2 * (y ^ fy) + (c ^ fc)]
            _remote(theirs, theirs, send_sems, recv_sems, k - 1, (x, y, c)).wait_recv()
        for cp in copies:
            cp.wait_send()
        local.wait()

    return pl.pallas_call(
        body, name="gather_devices", in_specs=[ANY], out_specs=ANY,
        out_shape=jax.ShapeDtypeStruct((N_DEV,) + buf.shape, buf.dtype),
        scratch_shapes=[pltpu.SemaphoreType.DMA((N_DEV - 1,)), pltpu.SemaphoreType.DMA((N_DEV - 1,)),
                        pltpu.SemaphoreType.DMA],
    )(buf)


class GradReducer:
    def __init__(self, core, chip, kinds):
        self.core, self.chip = core, chip
        self.sizes = dict(kinds)
        self.groups = {kind: None for kind, _ in kinds}

    def begin(self, grads, tag):
        arrays = [g.reshape(N_CHIPS, 2, -1, g.shape[-1]) for g, _, _ in grads]
        recv = send_halves_to_sibling(arrays, f"reduce_sibling_{tag}")
        parts = [add_sibling(g, r, self.core, f"reduce_add_sibling_{tag}_{t}")
                 for t, (g, r) in enumerate(zip(arrays, recv))]
        sems, parts, lands, token = reduce_start(parts, f"reduce_start_{tag}")
        return (sems, parts, lands, [(kind, mi) for _, kind, mi in grads], tag), token

    def end(self, state, after):
        sems, parts, lands, slots, tag = state
        parts, lands = reduce_wait(parts, lands, sems, after, f"reduce_wait_{tag}")
        for t, (kind, mi) in enumerate(slots):
            self.groups[kind] = add_chips(parts[t], lands[t], self.chip, self.core, self.groups[kind],
                                          self.sizes[kind], mi, f"reduce_add_chips_{tag}_{t}")

    def finish(self):
        kinds = list(self.groups)
        return dict(zip(kinds, exchange_halves([self.groups[k] for k in kinds], "reduce_swap")))


def _ffn_fwd(x, gain, wg, wu, wd, tag):
    h, rstd = rmsnorm_fwd(x, gain, BF16, f"ffn_norm_{tag}")
    gate, up = ffn_gateup(h, wg, wu, f"ffn_gateup_{tag}")
    out = mm_residual([gate, up], wd, x, 0.5, wd.shape[0] // N_CHIPS, f"ffn_down_{tag}")
    return out, (x, gain, h, rstd, gate, up)


def _ffn_bwd(dout, saved, wg, wu, wd, tag, dep=None):
    x, gain, h, rstd, gate, up = saved
    D = x.shape[1]
    Fs = wg.shape[2]
    dgate, dup, act = ffn_bwd_act(dout, wd, gate, up, f"ffn_bwd_act_{tag}", dep=dep)
    d_wd = mm_tn(act, dout, Fs, _tile(D, 512, 128), f"ffn_bwd_wd_{tag}", b_scale=0.5)
    d_wg = mm_tn(h, dgate, _tile(D, 512, 128), Fs, f"ffn_bwd_wg_{tag}", stacked_out=True)
    d_wu = mm_tn(h, dup, _tile(D, 512, 128), Fs, f"ffn_bwd_wu_{tag}", stacked_out=True)
    dh = mm_nt([(dgate, wg), (dup, wu)], _tile(x.shape[0], 352), D, Fs, f"ffn_bwd_dh_{tag}", stacked_w=True)
    dx, dgain = rmsnorm_bwd(dh, x, gain, rstd, dout, f"ffn_norm_bwd_{tag}")
    return dx, dgain, d_wg, d_wu, d_wd


def kernel(x, meta, ffn_norm, ffn_w_gate, ffn_w_up, ffn_w_down, gla_norm, gla_w_in, gla_w_lr, gla_b_lr, gla_head_norm, gla_w_out, pool_norm, pool_w, pool_b, pool_scale, final_norm, loss_target, m_meta, m_ffn_norm, m_ffn_w_gate, m_ffn_w_up, m_ffn_w_down, m_gla_norm, m_gla_w_in, m_gla_w_lr, m_gla_b_lr, m_gla_head_norm, m_gla_w_out, m_pool_norm, m_pool_w, m_pool_b, m_pool_scale, m_final_norm, v_meta, v_ffn_norm, v_ffn_w_gate, v_ffn_w_up, v_ffn_w_down, v_gla_norm, v_gla_w_in, v_gla_w_lr, v_gla_b_lr, v_gla_head_norm, v_gla_w_out, v_pool_norm, v_pool_w, v_pool_b, v_pool_scale, v_final_norm):
    S, D = x.shape[1], x.shape[2]
    M = OFF + S
    Dq = D // N_CHIPS
    Fs = ffn_w_gate.shape[3]
    F = N_CHIPS * Fs
    dk = D // 2
    n_in = gla_w_in.shape[2]
    W = D // 4
    core = lax.axis_index("c").astype(jnp.int32).reshape(1)
    chip_id = 2 * lax.axis_index("x") + lax.axis_index("y")
    chip = chip_id.astype(jnp.int32).reshape(1)

    small = jnp.concatenate([_pad_rows(t) for t in (
        meta, ffn_norm.reshape(4, Dq), gla_w_lr.reshape(8, Dq), pool_norm, pool_b.reshape(1, Dq), pool_scale)],
        axis=0)
    def stage(w, kind, n, mi):
        return stage_shard(w.reshape(n, 2, -1, w.shape[-1]), mi, chip, f"stage_{kind}_{mi}")

    ffn_stage = lambda mi: [stage(ffn_w_gate, "gate", 4, mi), stage(ffn_w_up, "up", 4, mi),
                            stage(ffn_w_down, "down", 4, mi)]
    small_stage = lax.dynamic_update_slice(jnp.zeros((N_CHIPS,) + small.shape, F32), small[None], (chip_id, 0, 0))
    buckets = [ffn_stage(0) + [small_stage],
               [stage(gla_w_in, "win", 1, 0), stage(gla_w_out, "wout", 1, 0)],
               ffn_stage(1), ffn_stage(2), [stage(pool_w, "pool", 1, 0)] + ffn_stage(3)]
    sizes = [len(b) for b in buckets]
    starts = [sum(sizes[:b]) for b in range(len(buckets))]
    gather_sems, in_flight, gather_token = gather_start([t for b in buckets for t in b], sizes, "gather_start")

    def arrive(b, after, n_big):
        bufs = gather_wait(in_flight[starts[b]:starts[b] + sizes[b]], gather_sems[b], after, f"gather_wait_{b}")
        return forward_to_sibling(bufs[:n_big], f"gather_forward_{b}") + bufs[n_big:]

    ffn_w = lambda t: (t[0].reshape(N_CHIPS, D, Fs), t[1].reshape(N_CHIPS, D, Fs), t[2].reshape(F, D))
    got = arrive(0, gather_token, 3)
    wg, wu, wd = [None] * 4, [None] * 4, [None] * 4
    wg[0], wu[0], wd[0] = ffn_w(got)
    sm = got[3]
    unshard = lambda t: t.transpose(1, 0, 2).reshape(t.shape[1], D)
    meta_f = unshard(sm[:, 0:16])
    ffn_norm_f = unshard(sm[:, 16:20])
    w_lr_f = sm[:, 24:32].reshape(N_CHIPS, GATE_RANK, dk // N_CHIPS).transpose(1, 0, 2).reshape(GATE_RANK, dk)
    pool_norm_f = sm[:, 32].reshape(1, D)
    pool_b_f = sm[:, 40].reshape(N_CHIPS, 4, W // N_CHIPS).transpose(1, 0, 2).reshape(1, D)
    pool_scale_f = sm[:, 48].reshape(1, D)
    wlr_pad = jnp.pad(w_lr_f.astype(BF16), ((0, LR_W - GATE_RANK), (0, 0)))
    final_g = final_norm.reshape(1, D)
    qkv = 2 * dk + D

    x0 = jnp.concatenate([jnp.zeros((PAD, D), F32), meta_f, x[0]], axis=0)
    target = jnp.pad(loss_target[0], ((OFF, 0), (0, 0)))
    x1, ffn0 = _ffn_fwd(x0, ffn_norm_f[0:1], wg[0], wu[0], wd[0], "0")
    got = arrive(1, x1, 2)
    w_in = got[0].reshape(N_CHIPS, D, n_in).transpose(1, 0, 2).reshape(D, N_CHIPS * n_in)
    w_out = got[1].reshape(D, D)
    w_all = jnp.concatenate([w_in[:, :qkv], w_in[:, qkv + GATE_RANK:], w_in[:, qkv:qkv + GATE_RANK],
                             jnp.zeros((D, LR_W - GATE_RANK), BF16)], axis=1)
    hg, rstd_g = rmsnorm_fwd(x1, gla_norm, BF16, "gla_norm")
    proj = mm_nn(hg, w_all, F32, "gla_proj")
    o, st = gla_fwd(proj, wlr_pad, gla_b_lr, D)
    gated = gla_post_fwd(o, proj, gla_head_norm, D)
    x2 = mm_residual([gated], w_out, x1, 1.0, D, "gla_out")
    wg[1], wu[1], wd[1] = ffn_w(arrive(2, x2, 3))
    x3, ffn1 = _ffn_fwd(x2, ffn_norm_f[1:2], wg[1], wu[1], wd[1], "1")
    wg[2], wu[2], wd[2] = ffn_w(arrive(3, x3, 3))
    x4, ffn2 = _ffn_fwd(x3, ffn_norm_f[2:3], wg[2], wu[2], wd[2], "2")
    got = arrive(4, x4, 4)
    w_pool = got[0].reshape(N_CHIPS, 4, W // N_CHIPS, W).transpose(1, 0, 2, 3).reshape(4, W, W)
    wg[3], wu[3], wd[3] = ffn_w(got[1:])
    hp, rstd_p = rmsnorm_fwd(x4, pool_norm_f, F32, "pool_norm")
    pooled = pool_window(hp)
    x5 = pool_mix(pooled, x4, w_pool, pool_b_f, pool_scale_f)
    x6, ffn3 = _ffn_fwd(x5, ffn_norm_f[3:4], wg[3], wu[3], wd[3], "3")
    loss, dx6, d_final = final_loss(x6, final_g, target)

    reducer = GradReducer(core, chip, [("gate", 4), ("up", 4), ("down", 4), ("win", 1), ("wout", 1), ("pool", 1)])
    ffn_grads = lambda i, g, u, d: [(g, "gate", i), (u, "up", i), (d, "down", i)]
    dx5, dn3, dwg3, dwu3, dwd3 = _ffn_bwd(dx6, ffn3, wg[3], wu[3], wd[3], "3")
    red3, tok = reducer.begin(ffn_grads(3, dwg3, dwu3, dwd3), "ffn3")
    dpooled, d_wpool, d_pool_b, d_pool_scale = pool_mix_bwd(dx5, pooled, w_pool, pool_b_f, pool_scale_f, dep=tok)
    dhp = pool_window_bwd(dpooled)
    dx4, d_pool_norm = rmsnorm_bwd(dhp, x4, pool_norm_f, rstd_p, dx5, "pool_norm_bwd")
    d_wpool = d_wpool.reshape(4, N_CHIPS, W // N_CHIPS, W).transpose(1, 0, 2, 3)
    redp, tok = reducer.begin([(d_wpool, "pool", 0)], "pool")
    dx3, dn2, dwg2, dwu2, dwd2 = _ffn_bwd(dx4, ffn2, wg[2], wu[2], wd[2], "2", dep=tok)
    reducer.end(red3, dx3)
    reducer.end(redp, dx3)
    red2, tok = reducer.begin(ffn_grads(2, dwg2, dwu2, dwd2), "ffn2")
    dx2, dn1, dwg1, dwu1, dwd1 = _ffn_bwd(dx3, ffn1, wg[1], wu[1], wd[1], "1", dep=tok)
    reducer.end(red2, dx2)
    red1, tok = reducer.begin(ffn_grads(1, dwg1, dwu1, dwd1), "ffn1")
    tm = _tile(M, 352)
    td = _tile(D, 512, 128)
    dgated = mm_nt([(dx2, w_out)], tm, td, D, "gla_out_bwd_act", dep=tok)
    d_wout = mm_tn(gated, dx2, td, td, "gla_out_bwd_w")
    do, dr, d_head_norm = gla_post_bwd(dgated, o, proj, gla_head_norm, D)
    dq, dkk, dv, dlr, dwlr, dblr = gla_bwd(proj, wlr_pad, gla_b_lr, st, do, D)
    dproj = jnp.concatenate([dq, dkk, dv, dr, dlr.astype(BF16)], axis=1)
    tp = _tile(proj.shape[1], 896, 128)
    dhg = mm_nt([(dproj, w_all)], tm, D, tp, "gla_proj_bwd_act")
    d_wall = mm_tn(hg, dproj, td, tp, "gla_proj_bwd_w")
    dx1, d_gla_norm = rmsnorm_bwd(dhg, x1, gla_norm, rstd_g, dx2, "gla_norm_bwd")
    reducer.end(red1, dx1)
    d_win = jnp.concatenate([d_wall[:, :qkv], d_wall[:, qkv + D:qkv + D + GATE_RANK], d_wall[:, qkv:qkv + D]], axis=1)
    d_win = d_win.reshape(D, N_CHIPS, n_in).transpose(1, 0, 2)
    redg, tok = reducer.begin([(d_wout, "wout", 0), (d_win, "win", 0)], "gla")
    dx0, dn0, dwg0, dwu0, dwd0 = _ffn_bwd(dx1, ffn0, wg[0], wu[0], wd[0], "0", dep=tok)
    reducer.end(redg, dx0)
    red0, tok = reducer.begin(ffn_grads(0, dwg0, dwu0, dwd0), "ffn0")

    d_wlr = dwlr[:, :GATE_RANK].transpose(1, 0, 2).reshape(GATE_RANK, dk)
    pieces = [dx0[PAD:OFF], dn0, dn1, dn2, dn3, d_gla_norm, d_wlr,
              dblr.reshape(1, dk), d_head_norm, d_pool_norm, d_pool_b, d_pool_scale, d_final]
    packed = jnp.concatenate([_pad_rows(p.reshape(-1, Dq)) for p in pieces], axis=0) + tok[0, 0]
    total = sum_devices(gather_devices(packed))

    reducer.end(red0, total)
    reduced = reducer.finish()
    g_gate = reduced["gate"].reshape(ffn_w_gate.shape)
    g_up = reduced["up"].reshape(ffn_w_up.shape)
    g_down = reduced["down"].reshape(ffn_w_down.shape)
    g_win = reduced["win"].reshape(gla_w_in.shape)
    g_wout = reduced["wout"].reshape(gla_w_out.shape)
    g_wpool = reduced["pool"].reshape(pool_w.shape)
    sums, at = [], 0
    for p in pieces:
        r = p.size // Dq
        sums.append(total[at:at + r].reshape(p.shape))
        at += r + (-r % 8)
    (s_meta, s_n0, s_n1, s_n2, s_n3, s_gla_norm, s_wlr, s_blr, s_head_norm, s_pool_norm, s_pool_b, s_pool_scale,
     s_final) = sums
    s_ffn_norm = jnp.stack([s_n0, s_n1, s_n2, s_n3], axis=0)[:, 0]
    mine = lambda t, width: lax.dynamic_slice_in_dim(t, chip_id * width, width, axis=t.ndim - 1)
    g_meta = mine(s_meta, Dq)
    g_ffn_norm = mine(s_ffn_norm, Dq).reshape(ffn_norm.shape)
    g_gla_norm = s_gla_norm
    g_wlr = mine(s_wlr, dk // N_CHIPS).reshape(gla_w_lr.shape)
    g_blr = s_blr
    g_head_norm = s_head_norm
    g_pool_norm = mine(s_pool_norm, Dq)
    g_pool_b = mine(s_pool_b.reshape(4, W), W // N_CHIPS).reshape(pool_b.shape)
    g_pool_scale = mine(s_pool_scale, Dq)
    g_final = s_final.reshape(final_norm.shape)

    weights = [meta, ffn_norm, ffn_w_gate, ffn_w_up, ffn_w_down, gla_norm, gla_w_in, gla_w_lr, gla_b_lr,
               gla_head_norm, gla_w_out, pool_norm, pool_w, pool_b, pool_scale, final_norm]
    moments_m = [m_meta, m_ffn_norm, m_ffn_w_gate, m_ffn_w_up, m_ffn_w_down, m_gla_norm, m_gla_w_in, m_gla_w_lr,
                 m_gla_b_lr, m_gla_head_norm, m_gla_w_out, m_pool_norm, m_pool_w, m_pool_b, m_pool_scale,
                 m_final_norm]
    moments_v = [v_meta, v_ffn_norm, v_ffn_w_gate, v_ffn_w_up, v_ffn_w_down, v_gla_norm, v_gla_w_in, v_gla_w_lr,
                 v_gla_b_lr, v_gla_head_norm, v_gla_w_out, v_pool_norm, v_pool_w, v_pool_b, v_pool_scale,
                 v_final_norm]
    grads_w = [g_meta, g_ffn_norm, g_gate, g_up, g_down, g_gla_norm, g_win, g_wlr, g_blr, g_head_norm, g_wout,
               g_pool_norm, g_wpool, g_pool_b, g_pool_scale, g_final]
    deltas, new_m, new_v = [], [], []
    for i, (w, g, m, v) in enumerate(zip(weights, grads_w, moments_m, moments_v)):
        d, nm, nv = adamw(w, g, m, v, f"adamw_{i}")
        deltas.append(d)
        new_m.append(nm)
        new_v.append(nv)

    loss = lax.psum(loss[0, 0], ("x", "y", "c"))
    grad_x = dx0[OFF:][None]
    return (loss, grad_x, *grads_w, *deltas, *new_m, *new_v)
```

```python
import functools

import jax
import jax.numpy as jnp
from jax import lax
from jax.experimental import pallas as pl
from jax.experimental.pallas import tpu as pltpu

F32 = jnp.float32
BF16 = jnp.bfloat16
MESH = pl.DeviceIdType.MESH
ANY = pl.BlockSpec(memory_space=pl.ANY)

N_META = 16
CHUNK = 64
PAD = CHUNK - N_META
OFF = PAD + N_META
EPS = 1e-6
HEADS = 4
GATE_RANK = 16
GATE_NORM = 16.0
LR_W = 128
N_CHIPS = 4
N_DEV = 8
ADAM_LR, ADAM_B1, ADAM_B2, ADAM_EPS, ADAM_WD, ADAM_STEP = 0.001, 0.9, 0.999, 1e-08, 0.01, 10
VMEM_LIMIT = 56 * 1024 * 1024
ROW_TILE = 176


def _tile(n, target, mult=16):
    best = None
    for d in range(mult, min(n, target) + 1, mult):
        if n % d == 0:
            best = d
    return best if best is not None else n


def _params(*sem):
    return pltpu.CompilerParams(dimension_semantics=sem, vmem_limit_bytes=VMEM_LIMIT)


def _dot(a, b):
    return jnp.dot(a, b, preferred_element_type=F32)


def _dot_nt(a, b):
    return lax.dot_general(a, b, (((1,), (1,)), ((), ())), preferred_element_type=F32)


def _dot_tn(a, b):
    return lax.dot_general(a, b, (((0,), (0,)), ((), ())), preferred_element_type=F32)


def _sigmoid(x):
    return 1.0 / (1.0 + jnp.exp(-x))


def _rows(tile, width=1):
    return lax.broadcasted_iota(jnp.int32, (tile, width), 0)


def _dep_specs(dep):
    return [] if dep is None else [ANY]


def _dep_args(dep):
    return [] if dep is None else [dep]


def _pad_rows(t):
    return jnp.pad(t, ((0, -t.shape[0] % 8), (0, 0)))


def rmsnorm_fwd(x, g, out_dtype, name):
    M, D = x.shape
    tr = _tile(M, ROW_TILE)

    def body(x_ref, g_ref, h_ref, r_ref):
        xv = x_ref[...]
        r = lax.rsqrt(jnp.mean(xv * xv, axis=-1, keepdims=True) + EPS)
        h_ref[...] = (xv * r * g_ref[...]).astype(out_dtype)
        r_ref[...] = r

    return pl.pallas_call(
        body, name=name, grid=(M // tr,),
        in_specs=[pl.BlockSpec((tr, D), lambda i: (i, 0)), pl.BlockSpec((1, D), lambda i: (0, 0))],
        out_specs=[pl.BlockSpec((tr, D), lambda i: (i, 0)), pl.BlockSpec((tr, 1), lambda i: (i, 0))],
        out_shape=[jax.ShapeDtypeStruct((M, D), out_dtype), jax.ShapeDtypeStruct((M, 1), F32)],
        compiler_params=_params("parallel"),
    )(x, g)


def rmsnorm_bwd(dh, x, g, rstd, dres, name):
    M, D = x.shape
    tr = _tile(M, ROW_TILE)

    def body(dh_ref, x_ref, g_ref, r_ref, dres_ref, dx_ref, dg_ref):
        @pl.when(pl.program_id(0) == 0)
        def _():
            dg_ref[...] = jnp.zeros_like(dg_ref)

        r = r_ref[...]
        xhat = x_ref[...] * r
        dhv = dh_ref[...]
        gd = dhv * g_ref[...]
        dx_ref[...] = dres_ref[...] + r * (gd - xhat * jnp.mean(gd * xhat, axis=-1, keepdims=True))
        dg_ref[...] += jnp.sum(dhv * xhat, axis=0, keepdims=True)

    row = pl.BlockSpec((tr, D), lambda i: (i, 0))
    vec = pl.BlockSpec((1, D), lambda i: (0, 0))
    return pl.pallas_call(
        body, name=name, grid=(M // tr,),
        in_specs=[row, row, vec, pl.BlockSpec((tr, 1), lambda i: (i, 0)), row],
        out_specs=[row, vec],
        out_shape=[jax.ShapeDtypeStruct((M, D), F32), jax.ShapeDtypeStruct((1, D), F32)],
        compiler_params=_params("arbitrary"),
    )(dh, x, g, rstd, dres)


def final_loss(x, g, target):
    M, D = x.shape
    tr = _tile(M, ROW_TILE)

    def body(x_ref, g_ref, t_ref, loss_ref, dx_ref, dg_ref):
        i = pl.program_id(0)

        @pl.when(i == 0)
        def _():
            loss_ref[...] = jnp.zeros_like(loss_ref)
            dg_ref[...] = jnp.zeros_like(dg_ref)

        live = (_rows(tr) + i * tr) >= OFF
        xv = x_ref[...]
        gv = g_ref[...]
        r = lax.rsqrt(jnp.mean(xv * xv, axis=-1, keepdims=True) + EPS)
        xhat = xv * r
        err = jnp.where(live, xhat * gv - t_ref[...], 0.0)
        loss_ref[...] += 0.5 * jnp.sum(jnp.mean(err * err, axis=-1, keepdims=True), axis=0, keepdims=True)
        dy = err * (1.0 / D)
        gd = dy * gv
        dx_ref[...] = r * (gd - xhat * jnp.mean(gd * xhat, axis=-1, keepdims=True))
        dg_ref[...] += jnp.sum(dy * xhat, axis=0, keepdims=True)

    row = pl.BlockSpec((tr, D), lambda i: (i, 0))
    vec = pl.BlockSpec((1, D), lambda i: (0, 0))
    return pl.pallas_call(
        body, name="final_loss", grid=(M // tr,),
        in_specs=[row, vec, row],
        out_specs=[pl.BlockSpec((1, 1), lambda i: (0, 0)), row, vec],
        out_shape=[jax.ShapeDtypeStruct((1, 1), F32), jax.ShapeDtypeStruct((M, D), F32),
                   jax.ShapeDtypeStruct((1, D), F32)],
        compiler_params=_params("arbitrary"),
    )(x, g, target)


def mm_nn(a, w, out_dtype, name, tm_target=704, tn_target=896):
    M, K = a.shape
    N = w.shape[1]
    tm, tn = _tile(M, tm_target), _tile(N, tn_target, 128)

    def body(a_ref, w_ref, o_ref):
        o_ref[...] = _dot(a_ref[...], w_ref[...]).astype(out_dtype)

    return pl.pallas_call(
        body, name=name, grid=(N // tn, M // tm),
        in_specs=[pl.BlockSpec((tm, K), lambda n, i: (i, 0)), pl.BlockSpec((K, tn), lambda n, i: (0, n))],
        out_specs=pl.BlockSpec((tm, tn), lambda n, i: (i, n)),
        out_shape=jax.ShapeDtypeStruct((M, N), out_dtype),
        compiler_params=_params("parallel", "parallel"),
    )(a, w)


def ffn_gateup(h, wg, wu, name):
    M, D = h.shape
    Fs = wg.shape[2]
    tm = _tile(M, 352)

    def body(h_ref, wg_ref, wu_ref, g_ref, u_ref):
        hv = h_ref[...]
        g_ref[...] = _dot(hv, wg_ref[...]).astype(BF16)
        u_ref[...] = _dot(hv, wu_ref[...]).astype(BF16)

    wspec = pl.BlockSpec((None, D, Fs), lambda j, i: (j, 0, 0))
    ospec = pl.BlockSpec((tm, Fs), lambda j, i: (i, j))
    return pl.pallas_call(
        body, name=name, grid=(N_CHIPS, M // tm),
        in_specs=[pl.BlockSpec((tm, D), lambda j, i: (i, 0)), wspec, wspec],
        out_specs=[ospec, ospec],
        out_shape=[jax.ShapeDtypeStruct((M, N_CHIPS * Fs), BF16)] * 2,
        compiler_params=_params("parallel", "parallel"),
    )(h, wg, wu)


def mm_residual(acts, w, x, scale, tk, name):
    M, N = x.shape
    K = w.shape[0]
    tm = _tile(M, 352)
    swiglu = len(acts) == 2

    def body(*refs):
        a_refs, (w_ref, x_ref, o_ref, acc) = refs[:len(acts)], refs[len(acts):]
        k = pl.program_id(1)

        @pl.when(k == 0)
        def _():
            acc[...] = jnp.zeros_like(acc)

        if swiglu:
            gv = a_refs[0][...].astype(F32)
            av = (gv * _sigmoid(gv) * a_refs[1][...].astype(F32)).astype(BF16)
        else:
            av = a_refs[0][...]
        acc[...] += _dot(av, w_ref[...])

        @pl.when(k == pl.num_programs(1) - 1)
        def _():
            o_ref[...] = x_ref[...] + scale * acc[...]

    aspec = pl.BlockSpec((tm, tk), lambda i, k: (i, k))
    return pl.pallas_call(
        body, name=name, grid=(M // tm, K // tk),
        in_specs=[aspec] * len(acts) + [pl.BlockSpec((tk, N), lambda i, k: (k, 0)),
                                        pl.BlockSpec((tm, N), lambda i, k: (i, 0))],
        out_specs=pl.BlockSpec((tm, N), lambda i, k: (i, 0)),
        out_shape=jax.ShapeDtypeStruct((M, N), F32),
        scratch_shapes=[pltpu.VMEM((tm, N), F32)],
        compiler_params=_params("parallel", "arbitrary"),
    )(*acts, w, x)


def ffn_bwd_act(dout, wd, gate, up, name, dep=None):
    M, D = dout.shape
    F = wd.shape[0]
    Fs = F // N_CHIPS
    tm = _tile(M, 352)

    def body(dy_ref, wd_ref, g_ref, u_ref, *rest):
        dg_ref, du_ref, a_ref = rest[-3:]
        dy = (0.5 * dy_ref[...]).astype(BF16)
        da = _dot_nt(dy, wd_ref[...])
        gv = g_ref[...].astype(F32)
        uv = u_ref[...].astype(F32)
        s = _sigmoid(gv)
        silu = gv * s
        a_ref[...] = (silu * uv).astype(BF16)
        dg_ref[...] = (da * uv * (s * (1.0 + gv * (1.0 - s)))).astype(BF16)
        du_ref[...] = (da * silu).astype(BF16)

    fspec = pl.BlockSpec((tm, Fs), lambda j, i: (i, j))
    return pl.pallas_call(
        body, name=name, grid=(N_CHIPS, M // tm),
        in_specs=[pl.BlockSpec((tm, D), lambda j, i: (i, 0)), pl.BlockSpec((Fs, D), lambda j, i: (j, 0)),
                  fspec, fspec] + _dep_specs(dep),
        out_specs=[fspec, fspec, fspec],
        out_shape=[jax.ShapeDtypeStruct((M, F), BF16)] * 3,
        compiler_params=_params("parallel", "parallel"),
    )(dout, wd, gate, up, *_dep_args(dep))


def mm_tn(a, b, ta, tb, name, b_scale=1.0, stacked_out=False, out_dtype=BF16, dep=None):
    T, Ma = a.shape
    Nb = b.shape[1]

    def body(a_ref, b_ref, *rest):
        o_ref = rest[-1]
        bv = b_ref[...]
        if b_scale != 1.0:
            bv = b_scale * bv
        o_ref[...] = _dot_tn(a_ref[...], bv.astype(BF16)).astype(out_dtype)

    if stacked_out:
        out_spec = pl.BlockSpec((None, ta, tb), lambda jb, ja: (jb, ja, 0))
        out_shape = jax.ShapeDtypeStruct((Nb // tb, Ma, tb), out_dtype)
    else:
        out_spec = pl.BlockSpec((ta, tb), lambda jb, ja: (ja, jb))
        out_shape = jax.ShapeDtypeStruct((Ma, Nb), out_dtype)
    return pl.pallas_call(
        body, name=name, grid=(Nb // tb, Ma // ta),
        in_specs=[pl.BlockSpec((T, ta), lambda jb, ja: (0, ja)), pl.BlockSpec((T, tb), lambda jb, ja: (0, jb))]
        + _dep_specs(dep),
        out_specs=out_spec, out_shape=out_shape,
        compiler_params=_params("parallel", "parallel"),
    )(a, b, *_dep_args(dep))


def mm_nt(pairs, tm, tn, tk, name, a_scale=1.0, stacked_w=False, dep=None):
    M, K = pairs[0][0].shape
    N = pairs[0][1].shape[1] if stacked_w else pairs[0][1].shape[0]
    n_pairs = len(pairs)

    def body(*refs):
        o_ref, acc = refs[-2:]
        k = pl.program_id(2)

        @pl.when(k == 0)
        def _():
            acc[...] = jnp.zeros_like(acc)

        for p in range(n_pairs):
            av = refs[2 * p][...]
            if a_scale != 1.0:
                av = a_scale * av
            acc[...] += _dot_nt(av.astype(BF16), refs[2 * p + 1][...])

        @pl.when(k == pl.num_programs(2) - 1)
        def _():
            o_ref[...] = acc[...]

    aspec = pl.BlockSpec((tm, tk), lambda i, n, k: (i, k))
    if stacked_w:
        wspec = pl.BlockSpec((None, tn, tk), lambda i, n, k: (k, n, 0))
    else:
        wspec = pl.BlockSpec((tn, tk), lambda i, n, k: (n, k))
    return pl.pallas_call(
        body, name=name, grid=(M // tm, N // tn, K // tk),
        in_specs=[aspec, wspec] * n_pairs + _dep_specs(dep),
        out_specs=pl.BlockSpec((tm, tn), lambda i, n, k: (i, n)),
        out_shape=jax.ShapeDtypeStruct((M, N), F32),
        scratch_shapes=[pltpu.VMEM((tm, tn), F32)],
        compiler_params=_params("parallel", "parallel", "arbitrary"),
    )(*[t for pair in pairs for t in pair], *_dep_args(dep))


def _tri(lower):
    r = lax.broadcasted_iota(jnp.int32, (CHUNK, CHUNK), 0)
    c = lax.broadcasted_iota(jnp.int32, (CHUNK, CHUNK), 1)
    return (r >= c) if lower else (r <= c)


def _tri_sum(mask, x, pieces):
    ones = mask.astype(BF16)
    acc = jnp.zeros_like(x)
    rest = x
    for _ in range(pieces):
        piece = rest.astype(BF16)
        acc = acc + _dot(ones, piece)
        rest = rest - piece.astype(F32)
    return acc


def _gla_gates(lr_ref, wlr_ref, blr_ref, chunk):
    z = _dot(lr_ref[...].astype(BF16), wlr_ref[...]) + blr_ref[...]
    live = (_rows(CHUNK) + chunk * CHUNK) >= PAD
    lg = jnp.where(live, (jnp.minimum(z, 0.0) - jnp.log(1.0 + jnp.exp(-jnp.abs(z)))) * (1.0 / GATE_NORM), 0.0)
    b = _tri_sum(_tri(True), lg, 3)
    b_last = jnp.sum(lg, axis=0, keepdims=True)
    b_mid = jnp.sum(jnp.where(_rows(CHUNK) < CHUNK // 2, lg, 0.0), axis=0, keepdims=True)
    return z, live, b, b_last, b_mid


def _gla_specs(dkh, dvh, D, chunk_of):
    lr_blk = (3 * D) // LR_W
    return [
        pl.BlockSpec((CHUNK, dkh), lambda c, h: (chunk_of(c), h)),
        pl.BlockSpec((CHUNK, dkh), lambda c, h: (chunk_of(c), HEADS + h)),
        pl.BlockSpec((CHUNK, dvh), lambda c, h: (chunk_of(c), HEADS + h)),
        pl.BlockSpec((CHUNK, LR_W), lambda c, h: (chunk_of(c), lr_blk)),
        pl.BlockSpec((LR_W, dkh), lambda c, h: (0, h)),
        pl.BlockSpec((1, dkh), lambda c, h: (0, h)),
    ]


def gla_fwd(proj, wlr, blr, D):
    M = proj.shape[0]
    n = M // CHUNK
    dkh, dvh = D // 2 // HEADS, D // HEADS
    qscale = float(dkh) ** -0.5

    def body(q_ref, k_ref, v_ref, lr_ref, wlr_ref, blr_ref, o_ref, st_ref, S):
        c, h = pl.program_id(0), pl.program_id(1)

        @pl.when(c == 0)
        def _():
            S[h] = jnp.zeros((dvh, dkh), F32)

        _, _, b, b_last, b_mid = _gla_gates(lr_ref, wlr_ref, blr_ref, c)
        q = q_ref[...] * qscale
        k = k_ref[...]
        v = v_ref[...].astype(BF16)
        s0 = S[h]
        st_ref[...] = s0
        qb = (q * jnp.exp(b)).astype(BF16)
        kb = (k * jnp.exp(b_last - b)).astype(BF16)
        qt = (q * jnp.exp(b - b_mid)).astype(BF16)
        kt = (k * jnp.exp(b_mid - b)).astype(BF16)
        a = jnp.where(_tri(True), _dot_nt(qt, kt), 0.0).astype(BF16)
        o_ref[...] = _dot_nt(qb, s0.astype(BF16)) + _dot(a, v)
        S[h] = jnp.exp(b_last) * s0 + _dot_tn(v, kb)

    return pl.pallas_call(
        body, name="gla_fwd", grid=(n, HEADS),
        in_specs=_gla_specs(dkh, dvh, D, lambda c: c),
        out_specs=[pl.BlockSpec((CHUNK, dvh), lambda c, h: (c, h)),
                   pl.BlockSpec((None, None, dvh, dkh), lambda c, h: (c, h, 0, 0))],
        out_shape=[jax.ShapeDtypeStruct((M, D), F32), jax.ShapeDtypeStruct((n, HEADS, dvh, dkh), F32)],
        scratch_shapes=[pltpu.VMEM((HEADS, dvh, dkh), F32)],
        compiler_params=_params("arbitrary", "arbitrary"),
    )(proj, proj, proj, proj, wlr, blr)


def gla_bwd(proj, wlr, blr, st, do, D):
    M = proj.shape[0]
    n = M // CHUNK
    dkh, dvh = D // 2 // HEADS, D // HEADS
    qscale = float(dkh) ** -0.5
    rev = lambda c: n - 1 - c

    def body(q_ref, k_ref, v_ref, lr_ref, wlr_ref, blr_ref, st_ref, do_ref,
             dq_ref, dk_ref, dv_ref, dlr_ref, dwlr_ref, dblr_ref, dS, acc_w, acc_b):
        step, h = pl.program_id(0), pl.program_id(1)
        c = n - 1 - step

        @pl.when(step == 0)
        def _():
            dS[h] = jnp.zeros((dvh, dkh), F32)
            acc_w[h] = jnp.zeros((LR_W, dkh), F32)
            acc_b[h] = jnp.zeros((1, dkh), F32)

        z, live, b, b_last, b_mid = _gla_gates(lr_ref, wlr_ref, blr_ref, c)
        q = q_ref[...] * qscale
        k = k_ref[...]
        v = v_ref[...].astype(BF16)
        dov = do_ref[...].astype(BF16)
        s0 = st_ref[...]
        ds1 = dS[h]
        ds1b = ds1.astype(BF16)
        e_b, e_lb = jnp.exp(b), jnp.exp(b_last - b)
        e_bm, e_mb = jnp.exp(b - b_mid), jnp.exp(b_mid - b)
        e_last = jnp.exp(b_last)
        qb, kb, qt, kt = q * e_b, k * e_lb, q * e_bm, k * e_mb
        qbb, kbb, qtb, ktb = qb.astype(BF16), kb.astype(BF16), qt.astype(BF16), kt.astype(BF16)
        lower = _tri(True)
        a = jnp.where(lower, _dot_nt(qtb, ktb), 0.0).astype(BF16)
        da = jnp.where(lower, _dot_nt(dov, v), 0.0).astype(BF16)

        dqb = _dot(dov, s0.astype(BF16))
        dqt = _dot(da, ktb)
        dkt = _dot_tn(da, qtb)
        dkb = _dot(v, ds1b)
        keep = live.astype(F32)
        dv_ref[...] = (keep * (_dot_tn(a, dov) + _dot_nt(kbb, ds1b))).astype(BF16)
        dq_ref[...] = (keep * qscale * (dqb * e_b + dqt * e_bm)).astype(BF16)
        dk_ref[...] = (keep * (dkb * e_lb + dkt * e_mb)).astype(BF16)

        db = dqb * qb - dkb * kb + dqt * qt - dkt * kt
        db_last = (jnp.sum(dkb * kb, axis=0, keepdims=True)
                   + jnp.sum(ds1 * s0, axis=0, keepdims=True) * e_last)
        db = db + jnp.where(_rows(CHUNK) == CHUNK - 1, db_last, 0.0)
        dlg = jnp.where(live, _tri_sum(_tri(False), db, 2), 0.0)
        dz = dlg * (1.0 / GATE_NORM) / (1.0 + jnp.exp(z))
        dzb = dz.astype(BF16)

        dlr_h = _dot_nt(dzb, wlr_ref[...])

        @pl.when(h == 0)
        def _():
            dlr_ref[...] = dlr_h

        @pl.when(h > 0)
        def _():
            dlr_ref[...] += dlr_h

        acc_w[h] += _dot_tn(lr_ref[...].astype(BF16), dzb)
        acc_b[h] += jnp.sum(dz, axis=0, keepdims=True)
        dS[h] = e_last * ds1 + _dot_tn(dov, qbb)

        @pl.when(step == n - 1)
        def _():
            dwlr_ref[h] = acc_w[h]
            dblr_ref[h] = acc_b[h]

    return pl.pallas_call(
        body, name="gla_bwd", grid=(n, HEADS),
        in_specs=_gla_specs(dkh, dvh, D, rev) + [
            pl.BlockSpec((None, None, dvh, dkh), lambda c, h: (rev(c), h, 0, 0)),
            pl.BlockSpec((CHUNK, dvh), lambda c, h: (rev(c), h))],
        out_specs=[pl.BlockSpec((CHUNK, dkh), lambda c, h: (rev(c), h)),
                   pl.BlockSpec((CHUNK, dkh), lambda c, h: (rev(c), h)),
                   pl.BlockSpec((CHUNK, dvh), lambda c, h: (rev(c), h)),
                   pl.BlockSpec((CHUNK, LR_W), lambda c, h: (rev(c), 0)),
                   pl.BlockSpec((HEADS, LR_W, dkh), lambda c, h: (0, 0, 0)),
                   pl.BlockSpec((HEADS, 1, dkh), lambda c, h: (0, 0, 0))],
        out_shape=[jax.ShapeDtypeStruct((M, D // 2), BF16), jax.ShapeDtypeStruct((M, D // 2), BF16),
                   jax.ShapeDtypeStruct((M, D), BF16), jax.ShapeDtypeStruct((M, LR_W), F32),
                   jax.ShapeDtypeStruct((HEADS, LR_W, dkh), F32), jax.ShapeDtypeStruct((HEADS, 1, dkh), F32)],
        scratch_shapes=[pltpu.VMEM((HEADS, dvh, dkh), F32), pltpu.VMEM((HEADS, LR_W, dkh), F32),
                        pltpu.VMEM((HEADS, 1, dkh), F32)],
        compiler_params=_params("arbitrary", "arbitrary"),
    )(proj, proj, proj, proj, wlr, blr, st, do)


def gla_post_fwd(o, proj, head_norm, D):
    M = o.shape[0]
    dvh = D // HEADS
    tr = _tile(M, ROW_TILE)

    def body(o_ref, r_ref, hn_ref, out_ref):
        for hd in range(HEADS):
            cols = slice(hd * dvh, (hd + 1) * dvh)
            ov = o_ref[:, cols]
            rs = lax.rsqrt(jnp.mean(ov * ov, axis=-1, keepdims=True) + EPS)
            rv = r_ref[:, cols]
            out_ref[:, cols] = (ov * rs * hn_ref[...] * (rv * _sigmoid(rv))).astype(BF16)

    row = pl.BlockSpec((tr, D), lambda i: (i, 0))
    return pl.pallas_call(
        body, name="gla_post_fwd", grid=(M // tr,),
        in_specs=[row, pl.BlockSpec((tr, D), lambda i: (i, 2)), pl.BlockSpec((1, dvh), lambda i: (0, 0))],
        out_specs=row, out_shape=jax.ShapeDtypeStruct((M, D), BF16),
        compiler_params=_params("parallel"),
    )(o, proj, head_norm)


def gla_post_bwd(dgated, o, proj, head_norm, D):
    M = o.shape[0]
    dvh = D // HEADS
    tr = _tile(M, ROW_TILE)

    def body(dg_ref, o_ref, r_ref, hn_ref, do_ref, dr_ref, dhn_ref):
        @pl.when(pl.program_id(0) == 0)
        def _():
            dhn_ref[...] = jnp.zeros_like(dhn_ref)

        hn = hn_ref[...]
        dhn = jnp.zeros((1, dvh), F32)
        for hd in range(HEADS):
            cols = slice(hd * dvh, (hd + 1) * dvh)
            ov = o_ref[:, cols]
            rs = lax.rsqrt(jnp.mean(ov * ov, axis=-1, keepdims=True) + EPS)
            ohat = ov * rs
            rv = r_ref[:, cols]
            s = _sigmoid(rv)
            dgv = dg_ref[:, cols]
            don = dgv * (rv * s)
            dr_ref[:, cols] = (dgv * ohat * hn * (s * (1.0 + rv * (1.0 - s)))).astype(BF16)
            gd = don * hn
            do_ref[:, cols] = rs * (gd - ohat * jnp.mean(gd * ohat, axis=-1, keepdims=True))
            dhn = dhn + jnp.sum(don * ohat, axis=0, keepdims=True)
        dhn_ref[...] += dhn

    row = pl.BlockSpec((tr, D), lambda i: (i, 0))
    vec = pl.BlockSpec((1, dvh), lambda i: (0, 0))
    return pl.pallas_call(
        body, name="gla_post_bwd", grid=(M // tr,),
        in_specs=[row, row, pl.BlockSpec((tr, D), lambda i: (i, 2)), vec],
        out_specs=[row, row, vec],
        out_shape=[jax.ShapeDtypeStruct((M, D), F32), jax.ShapeDtypeStruct((M, D), BF16),
                   jax.ShapeDtypeStruct((1, dvh), F32)],
        compiler_params=_params("arbitrary"),
    )(dgated, o, proj, head_norm)


def _pool_counts(M, g):
    t = _rows(M) - PAD
    win = jnp.left_shift(2, g)
    return t >= 0, jnp.maximum(jnp.minimum(t + 1, win), 1).astype(F32)


def _window_sum(x, g, M, back):
    sums = []
    s = x
    for lvl in range(4):
        sh = 1 << lvl
        s = s + pltpu.roll(s, (M - sh) if back else sh, 0)
        sums.append(s)
    return jnp.where(g == 0, sums[0], jnp.where(g == 1, sums[1], jnp.where(g == 2, sums[2], sums[3])))


POOL_COLS = 128


def pool_window(hp):
    M, D = hp.shape
    cw = min(POOL_COLS, D // 4)
    per_group = (D // 4) // cw

    def body(h_ref, p_ref):
        g = pl.program_id(0) // per_group
        live, cnt = _pool_counts(M, g)
        hv = h_ref[...]
        p_ref[...] = jnp.where(live, _window_sum(hv, g, M, False) / cnt - hv, 0.0).astype(BF16)

    col = pl.BlockSpec((M, cw), lambda j: (0, j))
    return pl.pallas_call(
        body, name="pool_window", grid=(D // cw,), in_specs=[col], out_specs=col,
        out_shape=jax.ShapeDtypeStruct((M, D), BF16), compiler_params=_params("parallel"),
    )(hp)


def pool_window_bwd(dpooled):
    M, D = dpooled.shape
    cw = min(POOL_COLS, D // 4)
    per_group = (D // 4) // cw

    def body(d_ref, o_ref):
        g = pl.program_id(0) // per_group
        live, cnt = _pool_counts(M, g)
        dv = jnp.where(live, d_ref[...], 0.0)
        o_ref[...] = jnp.where(live, _window_sum(dv / cnt, g, M, True) - dv, 0.0)

    col = pl.BlockSpec((M, cw), lambda j: (0, j))
    return pl.pallas_call(
        body, name="pool_window_bwd", grid=(D // cw,), in_specs=[col], out_specs=col,
        out_shape=jax.ShapeDtypeStruct((M, D), F32), compiler_params=_params("parallel"),
    )(dpooled)


def pool_mix(pooled, x, w, bias, scale):
    M, D = x.shape
    W = D // 4
    tm = _tile(M, 352)

    def body(p_ref, x_ref, w_ref, b_ref, s_ref, out_ref):
        live = (_rows(tm) + pl.program_id(1) * tm) >= PAD
        y = (_dot(p_ref[...], w_ref[...]) + b_ref[...]) * s_ref[...]
        out_ref[...] = x_ref[...] + jnp.where(live, y, 0.0)

    blk = pl.BlockSpec((tm, W), lambda g, i: (i, g))
    vec = pl.BlockSpec((1, W), lambda g, i: (0, g))
    return pl.pallas_call(
        body, name="pool_mix", grid=(4, M // tm),
        in_specs=[blk, blk, pl.BlockSpec((None, W, W), lambda g, i: (g, 0, 0)), vec, vec],
        out_specs=blk, out_shape=jax.ShapeDtypeStruct((M, D), F32),
        compiler_params=_params("parallel", "parallel"),
    )(pooled, x, w, bias, scale)


def pool_mix_bwd(dy, pooled, w, bias, scale, dep=None):
    M, D = dy.shape
    W = D // 4
    tm = _tile(M, 352)

    def body(dy_ref, p_ref, w_ref, b_ref, s_ref, *rest):
        dp_ref, dw_ref, db_ref, ds_ref, acc_w = rest[-5:]
        i = pl.program_id(1)

        @pl.when(i == 0)
        def _():
            acc_w[...] = jnp.zeros_like(acc_w)
            db_ref[...] = jnp.zeros_like(db_ref)
            ds_ref[...] = jnp.zeros_like(ds_ref)

        live = (_rows(tm) + i * tm) >= PAD
        dyv = jnp.where(live, dy_ref[...], 0.0)
        pooled = p_ref[...]
        wv = w_ref[...]
        ds_ref[...] += jnp.sum(dyv * (_dot(pooled, wv) + b_ref[...]), axis=0, keepdims=True)
        dys = dyv * s_ref[...]
        db_ref[...] += jnp.sum(dys, axis=0, keepdims=True)
        dysb = dys.astype(BF16)
        acc_w[...] += _dot_tn(pooled, dysb)
        dp_ref[...] = _dot_nt(dysb, wv)

        @pl.when(i == pl.num_programs(1) - 1)
        def _():
            dw_ref[...] = acc_w[...].astype(BF16)

    blk = pl.BlockSpec((tm, W), lambda g, i: (i, g))
    vec = pl.BlockSpec((1, W), lambda g, i: (0, g))
    wspec = pl.BlockSpec((None, W, W), lambda g, i: (g, 0, 0))
    return pl.pallas_call(
        body, name="pool_mix_bwd", grid=(4, M // tm),
        in_specs=[blk, blk, wspec, vec, vec] + _dep_specs(dep),
        out_specs=[blk, wspec, vec, vec],
        out_shape=[jax.ShapeDtypeStruct((M, D), F32), jax.ShapeDtypeStruct((4, W, W), BF16),
                   jax.ShapeDtypeStruct((1, D), F32), jax.ShapeDtypeStruct((1, D), F32)],
        scratch_shapes=[pltpu.VMEM((W, W), F32)],
        compiler_params=_params("parallel", "arbitrary"),
    )(dy, pooled, w, bias, scale, *_dep_args(dep))


def adamw(w, g, m, v, name):
    shape = w.shape
    C = shape[-1]
    R = w.size // C
    tr = _tile(R, 256, 8)

    def body(w_ref, g_ref, m_ref, v_ref, d_ref, nm_ref, nv_ref):
        gv = g_ref[...]
        nm = ADAM_B1 * m_ref[...] + (1.0 - ADAM_B1) * gv
        nv = ADAM_B2 * v_ref[...] + (1.0 - ADAM_B2) * (gv * gv)
        m_hat = nm / (1.0 - ADAM_B1 ** ADAM_STEP)
        v_hat = nv / (1.0 - ADAM_B2 ** ADAM_STEP)
        d_ref[...] = -ADAM_LR * (m_hat / (jnp.sqrt(v_hat) + ADAM_EPS) + ADAM_WD * w_ref[...])
        nm_ref[...] = nm
        nv_ref[...] = nv

    spec = pl.BlockSpec((tr, C), lambda i: (i, 0))
    outs = pl.pallas_call(
        body, name=name, grid=(R // tr,),
        in_specs=[spec] * 4, out_specs=[spec] * 3,
        out_shape=[jax.ShapeDtypeStruct((R, C), F32)] * 3,
        compiler_params=_params("parallel"),
    )(*[t.reshape(R, C) for t in (w, g, m, v)])
    return [t.reshape(shape) for t in outs]


def add_sibling(grad, recv, core, name):
    _, _, Rh, C = grad.shape
    tr = _tile(Rh, 512)

    def body(core_ref, g_ref, r_ref, o_ref):
        o_ref[...] = (g_ref[...].astype(F32) + r_ref[...].astype(F32)).astype(BF16)

    return pl.pallas_call(
        body, name=name,
        grid_spec=pltpu.PrefetchScalarGridSpec(
            num_scalar_prefetch=1, grid=(N_CHIPS, Rh // tr),
            in_specs=[pl.BlockSpec((None, None, tr, C), lambda j, i, core_ref: (j, core_ref[0], i, 0)),
                      pl.BlockSpec((None, tr, C), lambda j, i, core_ref: (j, i, 0))],
            out_specs=pl.BlockSpec((None, tr, C), lambda j, i, core_ref: (j, i, 0))),
        out_shape=jax.ShapeDtypeStruct((N_CHIPS, Rh, C), BF16),
        compiler_params=_params("parallel", "parallel"),
    )(core, grad, recv)


def add_chips(part, recv, chip, core, group, n, mi, name):
    _, Rh, C = part.shape
    tr = _tile(Rh, 512)

    def body(chip_ref, core_ref, p_ref, r_ref, *rest):
        o_ref = rest[-1]
        acc = p_ref[...].astype(F32)
        for k in range(N_CHIPS - 1):
            acc = acc + r_ref[k].astype(F32)
        o_ref[...] = acc

    carried = [] if group is None else [group]
    return pl.pallas_call(
        body, name=name,
        grid_spec=pltpu.PrefetchScalarGridSpec(
            num_scalar_prefetch=2, grid=(Rh // tr,),
            in_specs=[pl.BlockSpec((None, tr, C), lambda i, chip_ref, core_ref: (chip_ref[0], i, 0)),
                      pl.BlockSpec((N_CHIPS - 1, tr, C), lambda i, chip_ref, core_ref: (0, i, 0))]
            + [ANY] * len(carried),
            out_specs=pl.BlockSpec((None, None, tr, C), lambda i, chip_ref, core_ref: (mi, core_ref[0], i, 0))),
        out_shape=jax.ShapeDtypeStruct((n, 2, Rh, C), F32),
        input_output_aliases={4: 0} if carried else {},
        compiler_params=_params("parallel"),
    )(chip, core, part, recv, *carried)


def stage_shard(shard, mi, chip, name):
    _, _, Rh, C = shard.shape
    tr = _tile(Rh, 512)

    def body(chip_ref, s_ref, o_ref):
        o_ref[...] = s_ref[...].astype(BF16)

    return pl.pallas_call(
        body, name=name,
        grid_spec=pltpu.PrefetchScalarGridSpec(
            num_scalar_prefetch=1, grid=(2, Rh // tr),
            in_specs=[pl.BlockSpec((None, None, tr, C), lambda h, i, chip_ref: (mi, h, i, 0))],
            out_specs=pl.BlockSpec((None, None, tr, C), lambda h, i, chip_ref: (chip_ref[0], h, i, 0))),
        out_shape=jax.ShapeDtypeStruct((N_CHIPS, 2, Rh, C), BF16),
        compiler_params=_params("parallel", "parallel"),
    )(chip, shard)


def sum_devices(gathered):
    _, R, C = gathered.shape

    def body(g_ref, o_ref):
        acc = g_ref[0]
        for d in range(1, N_DEV):
            acc = acc + g_ref[d]
        o_ref[...] = acc

    return pl.pallas_call(
        body, name="sum_devices", grid=(1,),
        in_specs=[pl.BlockSpec((N_DEV, R, C), lambda i: (0, 0, 0))],
        out_specs=pl.BlockSpec((R, C), lambda i: (0, 0)),
        out_shape=jax.ShapeDtypeStruct((R, C), F32),
        compiler_params=_params("arbitrary"),
    )(gathered)


def _place():
    x, y, c = lax.axis_index("x"), lax.axis_index("y"), lax.axis_index("c")
    others = [(1 - x, y), (x, 1 - y), (1 - x, 1 - y)]
    return x, y, c, others


def _remote(src, dst, send_sems, recv_sems, idx, device):
    return pltpu.make_async_remote_copy(src_ref=src, dst_ref=dst, send_sem=send_sems.at[idx],
                                        recv_sem=recv_sems.at[idx], device_id=device, device_id_type=MESH)


HBM = pl.BlockSpec(memory_space=pltpu.HBM)
SEM = pl.BlockSpec(memory_space=pltpu.SEMAPHORE)
EFFECT = pltpu.SideEffectType.DATAFLOW_SIDE_EFFECTING


def _in_hbm(t):
    return pltpu.with_memory_space_constraint(t, pltpu.HBM)


def _own_slice(buf, me, c):
    return buf.at[me, c] if len(buf.shape) == 4 else buf.at[me]


def gather_start(staged, bucket_sizes, name):
    n, nb = len(staged), len(bucket_sizes)

    def body(*refs):
        in_refs, sems, token = refs[:n], refs[n:n + 2 * nb], refs[-1]
        x, y, c, others = _place()
        me = 2 * x + y
        t = 0
        for b, size in enumerate(bucket_sizes):
            for i in range(size):
                mine = _own_slice(in_refs[t], me, c)
                for k, chip in enumerate(others):
                    _remote(mine, mine, sems[2 * b], sems[2 * b + 1], 3 * i + k, (*chip, c)).start()
                t += 1
        token[...] = jnp.zeros_like(token)

    sem_shapes = [pltpu.SemaphoreType.DMA((3 * size,)) for size in bucket_sizes for _ in range(2)]
    outs = pl.pallas_call(
        body, name=name,
        out_shape=sem_shapes + [pltpu.HBM(s.shape, s.dtype) for s in staged] + [jax.ShapeDtypeStruct((8, 128), F32)],
        in_specs=[HBM] * n, out_specs=[SEM] * (2 * nb) + [HBM] * n + [pl.BlockSpec(memory_space=pltpu.VMEM)],
        input_output_aliases={t: 2 * nb + t for t in range(n)},
        compiler_params=pltpu.CompilerParams(has_side_effects=EFFECT),
    )(*[_in_hbm(s) for s in staged])
    sems = [(outs[2 * b], outs[2 * b + 1]) for b in range(nb)]
    return sems, list(outs[2 * nb:2 * nb + n]), outs[-1]


def gather_wait(bufs, sems, after, name):
    n = len(bufs)

    def body(*refs):
        in_refs, send_sems, recv_sems = refs[:n], refs[n], refs[n + 1]
        x, y, c, others = _place()
        me = 2 * x + y
        for i in range(n):
            mine = _own_slice(in_refs[i], me, c)
            for k, (ox, oy) in enumerate(others):
                cp = _remote(mine, _own_slice(in_refs[i], 2 * ox + oy, c), send_sems, recv_sems, 3 * i + k,
                             (ox, oy, c))
                cp.wait_send()
                cp.wait_recv()

    return pl.pallas_call(
        body, name=name, out_shape=[pltpu.HBM(b.shape, b.dtype) for b in bufs],
        in_specs=[HBM] * n + [SEM, SEM, ANY], out_specs=[HBM] * n,
        input_output_aliases={t: t for t in range(n)},
        compiler_params=pltpu.CompilerParams(has_side_effects=EFFECT),
    )(*bufs, *sems, after)


def forward_to_sibling(bufs, name):
    n = len(bufs)

    def body(*refs):
        out_refs, (send_sems, recv_sems) = refs[n:2 * n], refs[2 * n:]
        x, y, c, others = _place()
        sibling = (x, y, 1 - c)
        copies = []
        for t in range(n):
            for k, (ox, oy) in enumerate(others):
                mine = out_refs[t].at[2 * ox + oy, c]
                cp = _remote(mine, mine, send_sems, recv_sems, 3 * t + k, sibling)
                cp.start()
                copies.append(cp)
        for t in range(n):
            for k, (ox, oy) in enumerate(others):
                theirs = out_refs[t].at[2 * ox + oy, 1 - c]
                _remote(theirs, theirs, send_sems, recv_sems, 3 * t + k, sibling).wait_recv()
        for cp in copies:
            cp.wait_send()

    return pl.pallas_call(
        body, name=name, in_specs=[ANY] * n, out_specs=[ANY] * n,
        out_shape=[jax.ShapeDtypeStruct(b.shape, b.dtype) for b in bufs],
        input_output_aliases={t: t for t in range(n)},
        scratch_shapes=[pltpu.SemaphoreType.DMA((3 * n,)), pltpu.SemaphoreType.DMA((3 * n,))],
    )(*bufs)


def send_halves_to_sibling(grads, name):
    n = len(grads)

    def body(*refs):
        in_refs, out_refs, (send_sems, recv_sems) = refs[:n], refs[n:2 * n], refs[2 * n:]
        x, y, c, _ = _place()
        sibling = (x, y, 1 - c)
        copies = []
        for t in range(n):
            for j in range(N_CHIPS):
                cp = _remote(in_refs[t].at[j, 1 - c], out_refs[t].at[j], send_sems, recv_sems,
                             N_CHIPS * t + j, sibling)
                cp.start()
                copies.append(cp)
        for cp in copies:
            cp.wait()

    return pl.pallas_call(
        body, name=name, in_specs=[ANY] * n, out_specs=[ANY] * n,
        out_shape=[jax.ShapeDtypeStruct((N_CHIPS,) + g.shape[2:], g.dtype) for g in grads],
        scratch_shapes=[pltpu.SemaphoreType.DMA((N_CHIPS * n,)), pltpu.SemaphoreType.DMA((N_CHIPS * n,))],
    )(*grads)


def reduce_start(parts, name):
    n = len(parts)
    lands = [lax.empty((N_CHIPS - 1,) + p.shape[1:], p.dtype) for p in parts]

    def body(*refs):
        in_refs, land_refs, send_sems, recv_sems, token = refs[:n], refs[n:2 * n], refs[2 * n], refs[2 * n + 1], refs[-1]
        x, y, c, others = _place()
        for t in range(n):
            for k, (ox, oy) in enumerate(others):
                _remote(in_refs[t].at[2 * ox + oy], land_refs[t].at[k], send_sems, recv_sems, 3 * t + k,
                        (ox, oy, c)).start()
        token[...] = jnp.zeros_like(token)

    outs = pl.pallas_call(
        body, name=name,
        out_shape=[pltpu.SemaphoreType.DMA((3 * n,))] * 2 + [pltpu.HBM(t.shape, t.dtype) for t in parts + lands]
        + [jax.ShapeDtypeStruct((8, 128), F32)],
        in_specs=[HBM] * (2 * n), out_specs=[SEM, SEM] + [HBM] * (2 * n) + [pl.BlockSpec(memory_space=pltpu.VMEM)],
        input_output_aliases={t: 2 + t for t in range(2 * n)},
        compiler_params=pltpu.CompilerParams(has_side_effects=EFFECT),
    )(*[_in_hbm(t) for t in parts + lands])
    return (outs[0], outs[1]), list(outs[2:2 + n]), list(outs[2 + n:2 + 2 * n]), outs[-1]


def reduce_wait(parts, lands, sems, after, name):
    n = len(parts)

    def body(*refs):
        in_refs, land_refs, send_sems, recv_sems = refs[:n], refs[n:2 * n], refs[2 * n], refs[2 * n + 1]
        x, y, c, others = _place()
        for t in range(n):
            for k, (ox, oy) in enumerate(others):
                cp = _remote(in_refs[t].at[2 * ox + oy], land_refs[t].at[k], send_sems, recv_sems, 3 * t + k,
                             (ox, oy, c))
                cp.wait_send()
                cp.wait_recv()

    outs = pl.pallas_call(
        body, name=name, out_shape=[pltpu.HBM(t.shape, t.dtype) for t in parts + lands],
        in_specs=[HBM] * (2 * n) + [SEM, SEM, ANY], out_specs=[HBM] * (2 * n),
        input_output_aliases={t: t for t in range(2 * n)},
        compiler_params=pltpu.CompilerParams(has_side_effects=EFFECT),
    )(*parts, *lands, *sems, after)
    return list(outs[:n]), list(outs[n:])


def exchange_halves(groups, name):
    n_groups = len(groups)
    slots = [(gi, mi) for gi, grp in enumerate(groups) for mi in range(grp.shape[0])]

    def body(*refs):
        out_refs = refs[n_groups:2 * n_groups]
        send_sems, recv_sems = refs[2 * n_groups:]
        x, y, c, _ = _place()
        sibling = (x, y, 1 - c)
        copies = []
        for t, (gi, mi) in enumerate(slots):
            mine = out_refs[gi].at[mi, c]
            cp = _remote(mine, mine, send_sems, recv_sems, t, sibling)
            cp.start()
            copies.append(cp)
        for t, (gi, mi) in enumerate(slots):
            theirs = out_refs[gi].at[mi, 1 - c]
            _remote(theirs, theirs, send_sems, recv_sems, t, sibling).wait_recv()
        for cp in copies:
            cp.wait_send()

    return pl.pallas_call(
        body, name=name, in_specs=[ANY] * n_groups, out_specs=[ANY] * n_groups,
        out_shape=[jax.ShapeDtypeStruct(g.shape, g.dtype) for g in groups],
        input_output_aliases={gi: gi for gi in range(n_groups)},
        scratch_shapes=[pltpu.SemaphoreType.DMA((len(slots),)), pltpu.SemaphoreType.DMA((len(slots),))],
    )(*groups)


def gather_devices(buf):
    def body(in_ref, out_ref, send_sems, recv_sems, local_sem):
        x, y, c, _ = _place()
        me = 4 * x + 2 * y + c
        local = pltpu.make_async_copy(in_ref, out_ref.at[me], local_sem)
        local.start()
        copies = []
        for k in range(1, N_DEV):
            fx, fy, fc = (k >> 2) & 1, (k >> 1) & 1, k & 1
            peer = (x ^ fx, y ^ fy, c ^ fc)
            cp = _remote(in_ref, out_ref.at[me], send_sems, recv_sems, k - 1, peer)
            cp.start()
            copies.append(cp)
        for k in range(1, N_DEV):
            fx, fy, fc = (k >> 2) & 1, (k >> 1) & 1, k & 1
            theirs = out_ref.at[4 * (x ^ fx) + 2 * (y ^ fy) + (c ^ fc)]
            _remote(theirs, theirs, send_sems, recv_sems, k - 1, (x, y, c)).wait_recv()
        for cp in copies:
            cp.wait_send()
        local.wait()

    return pl.pallas_call(
        body, name="gather_devices", in_specs=[ANY], out_specs=ANY,
        out_shape=jax.ShapeDtypeStruct((N_DEV,) + buf.shape, buf.dtype),
        scratch_shapes=[pltpu.SemaphoreType.DMA((N_DEV - 1,)), pltpu.SemaphoreType.DMA((N_DEV - 1,)),
                        pltpu.SemaphoreType.DMA],
    )(buf)


class GradReducer:
    def __init__(self, core, chip, kinds):
        self.core, self.chip = core, chip
        self.sizes = dict(kinds)
        self.groups = {kind: None for kind, _ in kinds}

    def begin(self, grads, tag):
        arrays = [g.reshape(N_CHIPS, 2, -1, g.shape[-1]) for g, _, _ in grads]
        recv = send_halves_to_sibling(arrays, f"reduce_sibling_{tag}")
        parts = [add_sibling(g, r, self.core, f"reduce_add_sibling_{tag}_{t}")
                 for t, (g, r) in enumerate(zip(arrays, recv))]
        sems, parts, lands, token = reduce_start(parts, f"reduce_start_{tag}")
        return (sems, parts, lands, [(kind, mi) for _, kind, mi in grads], tag), token

    def end(self, state, after):
        sems, parts, lands, slots, tag = state
        parts, lands = reduce_wait(parts, lands, sems, after, f"reduce_wait_{tag}")
        for t, (kind, mi) in enumerate(slots):
            self.groups[kind] = add_chips(parts[t], lands[t], self.chip, self.core, self.groups[kind],
                                          self.sizes[kind], mi, f"reduce_add_chips_{tag}_{t}")

    def finish(self):
        kinds = list(self.groups)
        return dict(zip(kinds, exchange_halves([self.groups[k] for k in kinds], "reduce_swap")))


def _ffn_fwd(x, gain, wg, wu, wd, tag):
    h, rstd = rmsnorm_fwd(x, gain, BF16, f"ffn_norm_{tag}")
    gate, up = ffn_gateup(h, wg, wu, f"ffn_gateup_{tag}")
    out = mm_residual([gate, up], wd, x, 0.5, wd.shape[0] // N_CHIPS, f"ffn_down_{tag}")
    return out, (x, gain, h, rstd, gate, up)


def _ffn_bwd(dout, saved, wg, wu, wd, index, reducer, dep=None, per_tensor=False):
    x, gain, h, rstd, gate, up = saved
    D = x.shape[1]
    Fs = wg.shape[2]
    td = _tile(D, 512, 128)
    begun = []

    def begin(grads, tag):
        state, token = reducer.begin(grads, f"ffn{index}{tag}")
        begun.append(state)
        return token

    dgate, dup, act = ffn_bwd_act(dout, wd, gate, up, f"ffn_bwd_act_{index}", dep=dep)
    d_wd = mm_tn(act, dout, Fs, td, f"ffn_bwd_wd_{index}", b_scale=0.5)
    tok = begin([(d_wd, "down", index)], "d") if per_tensor else None
    d_wg = mm_tn(h, dgate, td, Fs, f"ffn_bwd_wg_{index}", stacked_out=True, dep=tok)
    tok = begin([(d_wg, "gate", index)], "g") if per_tensor else None
    d_wu = mm_tn(h, dup, td, Fs, f"ffn_bwd_wu_{index}", stacked_out=True, dep=tok)
    tok = begin([(d_wu, "up", index)] if per_tensor else
                [(d_wd, "down", index), (d_wg, "gate", index), (d_wu, "up", index)], "u")
    dh = mm_nt([(dgate, wg), (dup, wu)], _tile(x.shape[0], 352), D, Fs, f"ffn_bwd_dh_{index}", stacked_w=True,
               dep=tok)
    dx, dgain = rmsnorm_bwd(dh, x, gain, rstd, dout, f"ffn_norm_bwd_{index}")
    return dx, dgain, begun


def kernel(x, meta, ffn_norm, ffn_w_gate, ffn_w_up, ffn_w_down, gla_norm, gla_w_in, gla_w_lr, gla_b_lr, gla_head_norm, gla_w_out, pool_norm, pool_w, pool_b, pool_scale, final_norm, loss_target, m_meta, m_ffn_norm, m_ffn_w_gate, m_ffn_w_up, m_ffn_w_down, m_gla_norm, m_gla_w_in, m_gla_w_lr, m_gla_b_lr, m_gla_head_norm, m_gla_w_out, m_pool_norm, m_pool_w, m_pool_b, m_pool_scale, m_final_norm, v_meta, v_ffn_norm, v_ffn_w_gate, v_ffn_w_up, v_ffn_w_down, v_gla_norm, v_gla_w_in, v_gla_w_lr, v_gla_b_lr, v_gla_head_norm, v_gla_w_out, v_pool_norm, v_pool_w, v_pool_b, v_pool_scale, v_final_norm):
    S, D = x.shape[1], x.shape[2]
    M = OFF + S
    Dq = D // N_CHIPS
    Fs = ffn_w_gate.shape[3]
    F = N_CHIPS * Fs
    dk = D // 2
    n_in = gla_w_in.shape[2]
    W = D // 4
    core = lax.axis_index("c").astype(jnp.int32).reshape(1)
    chip_id = 2 * lax.axis_index("x") + lax.axis_index("y")
    chip = chip_id.astype(jnp.int32).reshape(1)

    small = jnp.concatenate([_pad_rows(t) for t in (
        meta, ffn_norm.reshape(4, Dq), gla_w_lr.reshape(8, Dq), pool_norm, pool_b.reshape(1, Dq), pool_scale)],
        axis=0)
    def stage(w, kind, n, mi):
        return stage_shard(w.reshape(n, 2, -1, w.shape[-1]), mi, chip, f"stage_{kind}_{mi}")

    ffn_stage = lambda mi: [stage(ffn_w_gate, "gate", 4, mi), stage(ffn_w_up, "up", 4, mi),
                            stage(ffn_w_down, "down", 4, mi)]
    small_stage = lax.dynamic_update_slice(jnp.zeros((N_CHIPS,) + small.shape, F32), small[None], (chip_id, 0, 0))
    first = ffn_stage(0)
    buckets = [first[:2] + [small_stage], first[2:],
               [stage(gla_w_in, "win", 1, 0), stage(gla_w_out, "wout", 1, 0)],
               ffn_stage(1), ffn_stage(2), [stage(pool_w, "pool", 1, 0)] + ffn_stage(3)]
    sizes = [len(b) for b in buckets]
    starts = [sum(sizes[:b]) for b in range(len(buckets))]
    gather_sems, in_flight, gather_token = gather_start([t for b in buckets for t in b], sizes, "gather_start")

    def arrive(b, after, n_big):
        bufs = gather_wait(in_flight[starts[b]:starts[b] + sizes[b]], gather_sems[b], after, f"gather_wait_{b}")
        return forward_to_sibling(bufs[:n_big], f"gather_forward_{b}") + bufs[n_big:]

    ffn_w = lambda t: (t[0].reshape(N_CHIPS, D, Fs), t[1].reshape(N_CHIPS, D, Fs), t[2].reshape(F, D))
    got = arrive(0, gather_token, 2)
    wg, wu, wd = [None] * 4, [None] * 4, [None] * 4
    wg[0], wu[0] = got[0].reshape(N_CHIPS, D, Fs), got[1].reshape(N_CHIPS, D, Fs)
    sm = got[2]
    unshard = lambda t: t.transpose(1, 0, 2).reshape(t.shape[1], D)
    meta_f = unshard(sm[:, 0:16])
    ffn_norm_f = unshard(sm[:, 16:20])
    w_lr_f = sm[:, 24:32].reshape(N_CHIPS, GATE_RANK, dk // N_CHIPS).transpose(1, 0, 2).reshape(GATE_RANK, dk)
    pool_norm_f = sm[:, 32].reshape(1, D)
    pool_b_f = sm[:, 40].reshape(N_CHIPS, 4, W // N_CHIPS).transpose(1, 0, 2).reshape(1, D)
    pool_scale_f = sm[:, 48].reshape(1, D)
    wlr_pad = jnp.pad(w_lr_f.astype(BF16), ((0, LR_W - GATE_RANK), (0, 0)))
    final_g = final_norm.reshape(1, D)
    qkv = 2 * dk + D

    x0 = jnp.concatenate([jnp.zeros((PAD, D), F32), meta_f, x[0]], axis=0)
    target = jnp.pad(loss_target[0], ((OFF, 0), (0, 0)))
    h0, rstd0 = rmsnorm_fwd(x0, ffn_norm_f[0:1], BF16, "ffn_norm_0")
    gate0, up0 = ffn_gateup(h0, wg[0], wu[0], "ffn_gateup_0")
    wd[0] = arrive(1, gate0, 1)[0].reshape(F, D)
    x1 = mm_residual([gate0, up0], wd[0], x0, 0.5, Fs, "ffn_down_0")
    ffn0 = (x0, ffn_norm_f[0:1], h0, rstd0, gate0, up0)
    got = arrive(2, x1, 2)
    w_in = got[0].reshape(N_CHIPS, D, n_in).transpose(1, 0, 2).reshape(D, N_CHIPS * n_in)
    w_out = got[1].reshape(D, D)
    w_all = jnp.concatenate([w_in[:, :qkv], w_in[:, qkv + GATE_RANK:], w_in[:, qkv:qkv + GATE_RANK],
                             jnp.zeros((D, LR_W - GATE_RANK), BF16)], axis=1)
    hg, rstd_g = rmsnorm_fwd(x1, gla_norm, BF16, "gla_norm")
    proj = mm_nn(hg, w_all, F32, "gla_proj")
    o, st = gla_fwd(proj, wlr_pad, gla_b_lr, D)
    gated = gla_post_fwd(o, proj, gla_head_norm, D)
    x2 = mm_residual([gated], w_out, x1, 1.0, D, "gla_out")
    wg[1], wu[1], wd[1] = ffn_w(arrive(3, x2, 3))
    x3, ffn1 = _ffn_fwd(x2, ffn_norm_f[1:2], wg[1], wu[1], wd[1], "1")
    wg[2], wu[2], wd[2] = ffn_w(arrive(4, x3, 3))
    x4, ffn2 = _ffn_fwd(x3, ffn_norm_f[2:3], wg[2], wu[2], wd[2], "2")
    got = arrive(5, x4, 4)
    w_pool = got[0].reshape(N_CHIPS, 4, W // N_CHIPS, W).transpose(1, 0, 2, 3).reshape(4, W, W)
    wg[3], wu[3], wd[3] = ffn_w(got[1:])
    hp, rstd_p = rmsnorm_fwd(x4, pool_norm_f, F32, "pool_norm")
    pooled = pool_window(hp)
    x5 = pool_mix(pooled, x4, w_pool, pool_b_f, pool_scale_f)
    x6, ffn3 = _ffn_fwd(x5, ffn_norm_f[3:4], wg[3], wu[3], wd[3], "3")
    loss, dx6, d_final = final_loss(x6, final_g, target)

    reducer = GradReducer(core, chip, [("gate", 4), ("up", 4), ("down", 4), ("win", 1), ("wout", 1), ("pool", 1)])

    def settle(begun, after):
        for state in begun:
            reducer.end(state, after)

    dx5, dn3, red3 = _ffn_bwd(dx6, ffn3, wg[3], wu[3], wd[3], 3, reducer)
    dpooled, d_wpool, d_pool_b, d_pool_scale = pool_mix_bwd(dx5, pooled, w_pool, pool_b_f, pool_scale_f)
    dhp = pool_window_bwd(dpooled)
    dx4, d_pool_norm = rmsnorm_bwd(dhp, x4, pool_norm_f, rstd_p, dx5, "pool_norm_bwd")
    d_wpool = d_wpool.reshape(4, N_CHIPS, W // N_CHIPS, W).transpose(1, 0, 2, 3)
    redp, tok = reducer.begin([(d_wpool, "pool", 0)], "pool")
    dx3, dn2, red2 = _ffn_bwd(dx4, ffn2, wg[2], wu[2], wd[2], 2, reducer, dep=tok)
    settle(red3 + [redp], dx3)
    dx2, dn1, red1 = _ffn_bwd(dx3, ffn1, wg[1], wu[1], wd[1], 1, reducer)
    settle(red2, dx2)
    tm = _tile(M, 352)
    td = _tile(D, 512, 128)
    d_wout = mm_tn(gated, dx2, td, td, "gla_out_bwd_w")
    redo, tok = reducer.begin([(d_wout, "wout", 0)], "wout")
    dgated = mm_nt([(dx2, w_out)], tm, td, D, "gla_out_bwd_act", dep=tok)
    do, dr, d_head_norm = gla_post_bwd(dgated, o, proj, gla_head_norm, D)
    dq, dkk, dv, dlr, dwlr, dblr = gla_bwd(proj, wlr_pad, gla_b_lr, st, do, D)
    dproj = jnp.concatenate([dq, dkk, dv, dr, dlr.astype(BF16)], axis=1)
    tp = _tile(proj.shape[1], 896, 128)
    d_wall = mm_tn(hg, dproj, td, tp, "gla_proj_bwd_w")
    d_win = jnp.concatenate([d_wall[:, :qkv], d_wall[:, qkv + D:qkv + D + GATE_RANK], d_wall[:, qkv:qkv + D]], axis=1)
    d_win = d_win.reshape(D, N_CHIPS, n_in).transpose(1, 0, 2)
    redi, tok = reducer.begin([(d_win, "win", 0)], "win")
    dhg = mm_nt([(dproj, w_all)], tm, D, tp, "gla_proj_bwd_act", dep=tok)
    dx1, d_gla_norm = rmsnorm_bwd(dhg, x1, gla_norm, rstd_g, dx2, "gla_norm_bwd")
    settle(red1 + [redo], dx1)
    dx0, dn0, red0 = _ffn_bwd(dx1, ffn0, wg[0], wu[0], wd[0], 0, reducer, per_tensor=True)
    settle([redi], dx0)

    d_wlr = dwlr[:, :GATE_RANK].transpose(1, 0, 2).reshape(GATE_RANK, dk)
    pieces = [dx0[PAD:OFF], dn0, dn1, dn2, dn3, d_gla_norm, d_wlr,
              dblr.reshape(1, dk), d_head_norm, d_pool_norm, d_pool_b, d_pool_scale, d_final]
    packed = jnp.concatenate([_pad_rows(p.reshape(-1, Dq)) for p in pieces], axis=0)
    total = sum_devices(gather_devices(packed))

    settle(red0, total)
    reduced = reducer.finish()
    g_gate = reduced["gate"].reshape(ffn_w_gate.shape)
    g_up = reduced["up"].reshape(ffn_w_up.shape)
    g_down = reduced["down"].reshape(ffn_w_down.shape)
    g_win = reduced["win"].reshape(gla_w_in.shape)
    g_wout = reduced["wout"].reshape(gla_w_out.shape)
    g_wpool = reduced["pool"].reshape(pool_w.shape)
    sums, at = [], 0
    for p in pieces:
        r = p.size // Dq
        sums.append(total[at:at + r].reshape(p.shape))
        at += r + (-r % 8)
    (s_meta, s_n0, s_n1, s_n2, s_n3, s_gla_norm, s_wlr, s_blr, s_head_norm, s_pool_norm, s_pool_b, s_pool_scale,
     s_final) = sums
    s_ffn_norm = jnp.stack([s_n0, s_n1, s_n2, s_n3], axis=0)[:, 0]
    mine = lambda t, width: lax.dynamic_slice_in_dim(t, chip_id * width, width, axis=t.ndim - 1)
    g_meta = mine(s_meta, Dq)
    g_ffn_norm = mine(s_ffn_norm, Dq).reshape(ffn_norm.shape)
    g_gla_norm = s_gla_norm
    g_wlr = mine(s_wlr, dk // N_CHIPS).reshape(gla_w_lr.shape)
    g_blr = s_blr
    g_head_norm = s_head_norm
    g_pool_norm = mine(s_pool_norm, Dq)
    g_pool_b = mine(s_pool_b.reshape(4, W), W // N_CHIPS).reshape(pool_b.shape)
    g_pool_scale = mine(s_pool_scale, Dq)
    g_final = s_final.reshape(final_norm.shape)

    weights = [meta, ffn_norm, ffn_w_gate, ffn_w_up, ffn_w_down, gla_norm, gla_w_in, gla_w_lr, gla_b_lr,
               gla_head_norm, gla_w_out, pool_norm, pool_w, pool_b, pool_scale, final_norm]
    moments_m = [m_meta, m_ffn_norm, m_ffn_w_gate, m_ffn_w_up, m_ffn_w_down, m_gla_norm, m_gla_w_in, m_gla_w_lr,
                 m_gla_b_lr, m_gla_head_norm, m_gla_w_out, m_pool_norm, m_pool_w, m_pool_b, m_pool_scale,
                 m_final_norm]
    moments_v = [v_meta, v_ffn_norm, v_ffn_w_gate, v_ffn_w_up, v_ffn_w_down, v_gla_norm, v_gla_w_in, v_gla_w_lr,
                 v_gla_b_lr, v_gla_head_norm, v_gla_w_out, v_pool_norm, v_pool_w, v_pool_b, v_pool_scale,
                 v_final_norm]
    grads_w = [g_meta, g_ffn_norm, g_gate, g_up, g_down, g_gla_norm, g_win, g_wlr, g_blr, g_head_norm, g_wout,
               g_pool_norm, g_wpool, g_pool_b, g_pool_scale, g_final]
    deltas, new_m, new_v = [], [], []
    for i, (w, g, m, v) in enumerate(zip(weights, grads_w, moments_m, moments_v)):
        d, nm, nv = adamw(w, g, m, v, f"adamw_{i}")
        deltas.append(d)
        new_m.append(nm)
        new_v.append(nv)

    loss = lax.psum(loss[0, 0], ("x", "y", "c"))
    grad_x = dx0[OFF:][None]
    return (loss, grad_x, *grads_w, *deltas, *new_m, *new_v)
```

```python
import functools

import jax
import jax.numpy as jnp
from jax import lax
from jax.experimental import pallas as pl
from jax.experimental.pallas import tpu as pltpu

F32 = jnp.float32
BF16 = jnp.bfloat16
MESH = pl.DeviceIdType.MESH
ANY = pl.BlockSpec(memory_space=pl.ANY)

N_META = 16
CHUNK = 64
PAD = CHUNK - N_META
OFF = PAD + N_META
EPS = 1e-6
HEADS = 4
GATE_RANK = 16
GATE_NORM = 16.0
LR_W = 128
N_CHIPS = 4
N_DEV = 8
ADAM_LR, ADAM_B1, ADAM_B2, ADAM_EPS, ADAM_WD, ADAM_STEP = 0.001, 0.9, 0.999, 1e-08, 0.01, 10
VMEM_LIMIT = 56 * 1024 * 1024
ROW_TILE = 176


def _tile(n, target, mult=16):
    best = None
    for d in range(mult, min(n, target) + 1, mult):
        if n % d == 0:
            best = d
    return best if best is not None else n


def _params(*sem):
    return pltpu.CompilerParams(dimension_semantics=sem, vmem_limit_bytes=VMEM_LIMIT)


def _dot(a, b):
    return jnp.dot(a, b, preferred_element_type=F32)


def _dot_nt(a, b):
    return lax.dot_general(a, b, (((1,), (1,)), ((), ())), preferred_element_type=F32)


def _dot_tn(a, b):
    return lax.dot_general(a, b, (((0,), (0,)), ((), ())), preferred_element_type=F32)


MXU_WIDTH = 256


def _chunks(n):
    return [slice(lo, min(lo + MXU_WIDTH, n)) for lo in range(0, n, MXU_WIDTH)]


def _sigmoid(x):
    return 1.0 / (1.0 + jnp.exp(-x))


def _rows(tile, width=1):
    return lax.broadcasted_iota(jnp.int32, (tile, width), 0)


def _dep_args(dep):
    if dep is None:
        return []
    return list(dep) if isinstance(dep, (list, tuple)) else [dep]


def _dep_specs(dep):
    return [ANY] * len(_dep_args(dep))


def _pad_rows(t):
    return jnp.pad(t, ((0, -t.shape[0] % 8), (0, 0)))


def rmsnorm_fwd(x, g, out_dtype, name):
    M, D = x.shape
    tr = _tile(M, ROW_TILE)

    def body(x_ref, g_ref, h_ref, r_ref):
        xv = x_ref[...]
        r = lax.rsqrt(jnp.mean(xv * xv, axis=-1, keepdims=True) + EPS)
        h_ref[...] = (xv * r * g_ref[...]).astype(out_dtype)
        r_ref[...] = r

    return pl.pallas_call(
        body, name=name, grid=(M // tr,),
        in_specs=[pl.BlockSpec((tr, D), lambda i: (i, 0)), pl.BlockSpec((1, D), lambda i: (0, 0))],
        out_specs=[pl.BlockSpec((tr, D), lambda i: (i, 0)), pl.BlockSpec((tr, 1), lambda i: (i, 0))],
        out_shape=[jax.ShapeDtypeStruct((M, D), out_dtype), jax.ShapeDtypeStruct((M, 1), F32)],
        compiler_params=_params("parallel"),
    )(x, g)


def rmsnorm_bwd(dh, x, g, rstd, dres, name):
    M, D = x.shape
    tr = _tile(M, ROW_TILE)

    def body(dh_ref, x_ref, g_ref, r_ref, dres_ref, dx_ref, dg_ref):
        @pl.when(pl.program_id(0) == 0)
        def _():
            dg_ref[...] = jnp.zeros_like(dg_ref)

        r = r_ref[...]
        xhat = x_ref[...] * r
        dhv = dh_ref[...]
        gd = dhv * g_ref[...]
        dx_ref[...] = dres_ref[...] + r * (gd - xhat * jnp.mean(gd * xhat, axis=-1, keepdims=True))
        dg_ref[...] += jnp.sum(dhv * xhat, axis=0, keepdims=True)

    row = pl.BlockSpec((tr, D), lambda i: (i, 0))
    vec = pl.BlockSpec((1, D), lambda i: (0, 0))
    return pl.pallas_call(
        body, name=name, grid=(M // tr,),
        in_specs=[row, row, vec, pl.BlockSpec((tr, 1), lambda i: (i, 0)), row],
        out_specs=[row, vec],
        out_shape=[jax.ShapeDtypeStruct((M, D), F32), jax.ShapeDtypeStruct((1, D), F32)],
        compiler_params=_params("arbitrary"),
    )(dh, x, g, rstd, dres)


def final_loss(x, g, target):
    M, D = x.shape
    tr = _tile(M, ROW_TILE)

    def body(x_ref, g_ref, t_ref, loss_ref, dx_ref, dg_ref):
        i = pl.program_id(0)

        @pl.when(i == 0)
        def _():
            loss_ref[...] = jnp.zeros_like(loss_ref)
            dg_ref[...] = jnp.zeros_like(dg_ref)

        live = (_rows(tr) + i * tr) >= OFF
        xv = x_ref[...]
        gv = g_ref[...]
        r = lax.rsqrt(jnp.mean(xv * xv, axis=-1, keepdims=True) + EPS)
        xhat = xv * r
        err = jnp.where(live, xhat * gv - t_ref[...], 0.0)
        loss_ref[...] += 0.5 * jnp.sum(jnp.mean(err * err, axis=-1, keepdims=True), axis=0, keepdims=True)
        dy = err * (1.0 / D)
        gd = dy * gv
        dx_ref[...] = r * (gd - xhat * jnp.mean(gd * xhat, axis=-1, keepdims=True))
        dg_ref[...] += jnp.sum(dy * xhat, axis=0, keepdims=True)

    row = pl.BlockSpec((tr, D), lambda i: (i, 0))
    vec = pl.BlockSpec((1, D), lambda i: (0, 0))
    return pl.pallas_call(
        body, name="final_loss", grid=(M // tr,),
        in_specs=[row, vec, row],
        out_specs=[pl.BlockSpec((1, 1), lambda i: (0, 0)), row, vec],
        out_shape=[jax.ShapeDtypeStruct((1, 1), F32), jax.ShapeDtypeStruct((M, D), F32),
                   jax.ShapeDtypeStruct((1, D), F32)],
        compiler_params=_params("arbitrary"),
    )(x, g, target)


def mm_nn(a, w, out_dtype, name, tm_target=704, tn_target=896):
    M, K = a.shape
    N = w.shape[1]
    tm, tn = _tile(M, tm_target), _tile(N, tn_target, 128)

    def body(a_ref, w_ref, o_ref):
        o_ref[...] = _dot(a_ref[...], w_ref[...]).astype(out_dtype)

    return pl.pallas_call(
        body, name=name, grid=(N // tn, M // tm),
        in_specs=[pl.BlockSpec((tm, K), lambda n, i: (i, 0)), pl.BlockSpec((K, tn), lambda n, i: (0, n))],
        out_specs=pl.BlockSpec((tm, tn), lambda n, i: (i, n)),
        out_shape=jax.ShapeDtypeStruct((M, N), out_dtype),
        compiler_params=_params("parallel", "parallel"),
    )(a, w)


def ffn_gateup(h, wg, wu, name):
    M, D = h.shape
    Fs = wg.shape[2]
    tm = _tile(M, 352)

    def body(h_ref, wg_ref, wu_ref, g_ref, u_ref):
        hv = h_ref[...]
        g_ref[...] = _dot(hv, wg_ref[...]).astype(BF16)
        u_ref[...] = _dot(hv, wu_ref[...]).astype(BF16)

    wspec = pl.BlockSpec((None, D, Fs), lambda j, i: (j, 0, 0))
    ospec = pl.BlockSpec((tm, Fs), lambda j, i: (i, j))
    return pl.pallas_call(
        body, name=name, grid=(N_CHIPS, M // tm),
        in_specs=[pl.BlockSpec((tm, D), lambda j, i: (i, 0)), wspec, wspec],
        out_specs=[ospec, ospec],
        out_shape=[jax.ShapeDtypeStruct((M, N_CHIPS * Fs), BF16)] * 2,
        compiler_params=_params("parallel", "parallel"),
    )(h, wg, wu)


def mm_residual(acts, w, x, scale, tk, name):
    M, N = x.shape
    K = w.shape[0]
    tm = _tile(M, 352)
    swiglu = len(acts) == 2

    def body(*refs):
        a_refs, (w_ref, x_ref, o_ref, acc) = refs[:len(acts)], refs[len(acts):]
        k = pl.program_id(1)

        @pl.when(k == 0)
        def _():
            acc[...] = jnp.zeros_like(acc)

        if swiglu:
            part = None
            for cols in _chunks(tk):
                gv = a_refs[0][:, cols].astype(F32)
                av = (gv * _sigmoid(gv) * a_refs[1][:, cols].astype(F32)).astype(BF16)
                d = _dot(av, w_ref[cols, :])
                part = d if part is None else part + d
            acc[...] += part
        else:
            acc[...] += _dot(a_refs[0][...], w_ref[...])

        @pl.when(k == pl.num_programs(1) - 1)
        def _():
            o_ref[...] = x_ref[...] + scale * acc[...]

    aspec = pl.BlockSpec((tm, tk), lambda i, k: (i, k))
    return pl.pallas_call(
        body, name=name, grid=(M // tm, K // tk),
        in_specs=[aspec] * len(acts) + [pl.BlockSpec((tk, N), lambda i, k: (k, 0)),
                                        pl.BlockSpec((tm, N), lambda i, k: (i, 0))],
        out_specs=pl.BlockSpec((tm, N), lambda i, k: (i, 0)),
        out_shape=jax.ShapeDtypeStruct((M, N), F32),
        scratch_shapes=[pltpu.VMEM((tm, N), F32)],
        compiler_params=_params("parallel", "arbitrary"),
    )(*acts, w, x)


def ffn_bwd_act(dout, wd, gate, up, name, dep=None):
    M, D = dout.shape
    F = wd.shape[0]
    Fs = F // N_CHIPS
    tm = _tile(M, 352)

    def body(dy_ref, wd_ref, g_ref, u_ref, *rest):
        dg_ref, du_ref, a_ref = rest[-3:]
        dy = (0.5 * dy_ref[...]).astype(BF16)
        for cols in _chunks(Fs):
            da = _dot_nt(dy, wd_ref[cols, :])
            gv = g_ref[:, cols].astype(F32)
            uv = u_ref[:, cols].astype(F32)
            s = _sigmoid(gv)
            silu = gv * s
            a_ref[:, cols] = (silu * uv).astype(BF16)
            dg_ref[:, cols] = (da * uv * (s * (1.0 + gv * (1.0 - s)))).astype(BF16)
            du_ref[:, cols] = (da * silu).astype(BF16)

    fspec = pl.BlockSpec((tm, Fs), lambda j, i: (i, j))
    return pl.pallas_call(
        body, name=name, grid=(N_CHIPS, M // tm),
        in_specs=[pl.BlockSpec((tm, D), lambda j, i: (i, 0)), pl.BlockSpec((Fs, D), lambda j, i: (j, 0)),
                  fspec, fspec] + _dep_specs(dep),
        out_specs=[fspec, fspec, fspec],
        out_shape=[jax.ShapeDtypeStruct((M, F), BF16)] * 3,
        compiler_params=_params("parallel", "parallel"),
    )(dout, wd, gate, up, *_dep_args(dep))


def mm_tn(a, b, ta, tb, name, b_scale=1.0, stacked_out=False, out_dtype=BF16, dep=None):
    T, Ma = a.shape
    Nb = b.shape[1]

    def body(a_ref, b_ref, *rest):
        o_ref = rest[-1]
        bv = b_ref[...]
        if b_scale != 1.0:
            bv = b_scale * bv
        o_ref[...] = _dot_tn(a_ref[...], bv.astype(BF16)).astype(out_dtype)

    if stacked_out:
        out_spec = pl.BlockSpec((None, ta, tb), lambda jb, ja: (jb, ja, 0))
        out_shape = jax.ShapeDtypeStruct((Nb // tb, Ma, tb), out_dtype)
    else:
        out_spec = pl.BlockSpec((ta, tb), lambda jb, ja: (ja, jb))
        out_shape = jax.ShapeDtypeStruct((Ma, Nb), out_dtype)
    return pl.pallas_call(
        body, name=name, grid=(Nb // tb, Ma // ta),
        in_specs=[pl.BlockSpec((T, ta), lambda jb, ja: (0, ja)), pl.BlockSpec((T, tb), lambda jb, ja: (0, jb))]
        + _dep_specs(dep),
        out_specs=out_spec, out_shape=out_shape,
        compiler_params=_params("parallel", "parallel"),
    )(a, b, *_dep_args(dep))


def mm_nt(pairs, tm, tn, tk, name, a_scale=1.0, stacked_w=False, dep=None):
    M, K = pairs[0][0].shape
    N = pairs[0][1].shape[1] if stacked_w else pairs[0][1].shape[0]
    n_pairs = len(pairs)

    def body(*refs):
        o_ref, acc = refs[-2:]
        k = pl.program_id(2)

        @pl.when(k == 0)
        def _():
            acc[...] = jnp.zeros_like(acc)

        for p in range(n_pairs):
            av = refs[2 * p][...]
            if a_scale != 1.0:
                av = a_scale * av
            acc[...] += _dot_nt(av.astype(BF16), refs[2 * p + 1][...])

        @pl.when(k == pl.num_programs(2) - 1)
        def _():
            o_ref[...] = acc[...]

    aspec = pl.BlockSpec((tm, tk), lambda i, n, k: (i, k))
    if stacked_w:
        wspec = pl.BlockSpec((None, tn, tk), lambda i, n, k: (k, n, 0))
    else:
        wspec = pl.BlockSpec((tn, tk), lambda i, n, k: (n, k))
    return pl.pallas_call(
        body, name=name, grid=(M // tm, N // tn, K // tk),
        in_specs=[aspec, wspec] * n_pairs + _dep_specs(dep),
        out_specs=pl.BlockSpec((tm, tn), lambda i, n, k: (i, n)),
        out_shape=jax.ShapeDtypeStruct((M, N), F32),
        scratch_shapes=[pltpu.VMEM((tm, tn), F32)],
        compiler_params=_params("parallel", "parallel", "arbitrary"),
    )(*[t for pair in pairs for t in pair], *_dep_args(dep))


def _tri(lower):
    r = lax.broadcasted_iota(jnp.int32, (CHUNK, CHUNK), 0)
    c = lax.broadcasted_iota(jnp.int32, (CHUNK, CHUNK), 1)
    return (r >= c) if lower else (r <= c)


def _tri_sum(mask, x, pieces):
    ones = mask.astype(BF16)
    acc = jnp.zeros_like(x)
    rest = x
    for _ in range(pieces):
        piece = rest.astype(BF16)
        acc = acc + _dot(ones, piece)
        rest = rest - piece.astype(F32)
    return acc


def _gla_gates(lr_ref, wlr_ref, blr_ref, chunk):
    z = _dot(lr_ref[...].astype(BF16), wlr_ref[...]) + blr_ref[...]
    live = (_rows(CHUNK) + chunk * CHUNK) >= PAD
    lg = jnp.where(live, (jnp.minimum(z, 0.0) - jnp.log(1.0 + jnp.exp(-jnp.abs(z)))) * (1.0 / GATE_NORM), 0.0)
    b = _tri_sum(_tri(True), lg, 3)
    b_last = jnp.sum(lg, axis=0, keepdims=True)
    b_mid = jnp.sum(jnp.where(_rows(CHUNK) < CHUNK // 2, lg, 0.0), axis=0, keepdims=True)
    return z, live, b, b_last, b_mid


def _gla_specs(dkh, dvh, D, chunk_of):
    lr_blk = (3 * D) // LR_W
    return [
        pl.BlockSpec((CHUNK, dkh), lambda c, h: (chunk_of(c), h)),
        pl.BlockSpec((CHUNK, dkh), lambda c, h: (chunk_of(c), HEADS + h)),
        pl.BlockSpec((CHUNK, dvh), lambda c, h: (chunk_of(c), HEADS + h)),
        pl.BlockSpec((CHUNK, LR_W), lambda c, h: (chunk_of(c), lr_blk)),
        pl.BlockSpec((LR_W, dkh), lambda c, h: (0, h)),
        pl.BlockSpec((1, dkh), lambda c, h: (0, h)),
    ]


def gla_fwd(proj, wlr, blr, D):
    M = proj.shape[0]
    n = M // CHUNK
    dkh, dvh = D // 2 // HEADS, D // HEADS
    qscale = float(dkh) ** -0.5

    def body(q_ref, k_ref, v_ref, lr_ref, wlr_ref, blr_ref, o_ref, st_ref, S):
        c, h = pl.program_id(0), pl.program_id(1)

        @pl.when(c == 0)
        def _():
            S[h] = jnp.zeros((dvh, dkh), F32)

        _, _, b, b_last, b_mid = _gla_gates(lr_ref, wlr_ref, blr_ref, c)
        q = q_ref[...] * qscale
        k = k_ref[...]
        v = v_ref[...].astype(BF16)
        s0 = S[h]
        st_ref[...] = s0
        qb = (q * jnp.exp(b)).astype(BF16)
        kb = (k * jnp.exp(b_last - b)).astype(BF16)
        qt = (q * jnp.exp(b - b_mid)).astype(BF16)
        kt = (k * jnp.exp(b_mid - b)).astype(BF16)
        a = jnp.where(_tri(True), _dot_nt(qt, kt), 0.0).astype(BF16)
        o_ref[...] = _dot_nt(qb, s0.astype(BF16)) + _dot(a, v)
        S[h] = jnp.exp(b_last) * s0 + _dot_tn(v, kb)

    return pl.pallas_call(
        body, name="gla_fwd", grid=(n, HEADS),
        in_specs=_gla_specs(dkh, dvh, D, lambda c: c),
        out_specs=[pl.BlockSpec((CHUNK, dvh), lambda c, h: (c, h)),
                   pl.BlockSpec((None, None, dvh, dkh), lambda c, h: (c, h, 0, 0))],
        out_shape=[jax.ShapeDtypeStruct((M, D), F32), jax.ShapeDtypeStruct((n, HEADS, dvh, dkh), F32)],
        scratch_shapes=[pltpu.VMEM((HEADS, dvh, dkh), F32)],
        compiler_params=_params("arbitrary", "arbitrary"),
    )(proj, proj, proj, proj, wlr, blr)


def gla_bwd(proj, wlr, blr, st, do, D):
    M = proj.shape[0]
    n = M // CHUNK
    dkh, dvh = D // 2 // HEADS, D // HEADS
    qscale = float(dkh) ** -0.5
    rev = lambda c: n - 1 - c

    def body(q_ref, k_ref, v_ref, lr_ref, wlr_ref, blr_ref, st_ref, do_ref,
             dq_ref, dk_ref, dv_ref, dlr_ref, dwlr_ref, dblr_ref, dS, acc_w, acc_b):
        step, h = pl.program_id(0), pl.program_id(1)
        c = n - 1 - step

        @pl.when(step == 0)
        def _():
            dS[h] = jnp.zeros((dvh, dkh), F32)
            acc_w[h] = jnp.zeros((LR_W, dkh), F32)
            acc_b[h] = jnp.zeros((1, dkh), F32)

        z, live, b, b_last, b_mid = _gla_gates(lr_ref, wlr_ref, blr_ref, c)
        q = q_ref[...] * qscale
        k = k_ref[...]
        v = v_ref[...].astype(BF16)
        dov = do_ref[...].astype(BF16)
        s0 = st_ref[...]
        ds1 = dS[h]
        ds1b = ds1.astype(BF16)
        e_b, e_lb = jnp.exp(b), jnp.exp(b_last - b)
        e_bm, e_mb = jnp.exp(b - b_mid), jnp.exp(b_mid - b)
        e_last = jnp.exp(b_last)
        qb, kb, qt, kt = q * e_b, k * e_lb, q * e_bm, k * e_mb
        qbb, kbb, qtb, ktb = qb.astype(BF16), kb.astype(BF16), qt.astype(BF16), kt.astype(BF16)
        lower = _tri(True)
        a = jnp.where(lower, _dot_nt(qtb, ktb), 0.0).astype(BF16)
        da = jnp.where(lower, _dot_nt(dov, v), 0.0).astype(BF16)

        dqb = _dot(dov, s0.astype(BF16))
        dqt = _dot(da, ktb)
        dkt = _dot_tn(da, qtb)
        dkb = _dot(v, ds1b)
        keep = live.astype(F32)
        dv_ref[...] = (keep * (_dot_tn(a, dov) + _dot_nt(kbb, ds1b))).astype(BF16)
        dq_ref[...] = (keep * qscale * (dqb * e_b + dqt * e_bm)).astype(BF16)
        dk_ref[...] = (keep * (dkb * e_lb + dkt * e_mb)).astype(BF16)

        db = dqb * qb - dkb * kb + dqt * qt - dkt * kt
        db_last = (jnp.sum(dkb * kb, axis=0, keepdims=True)
                   + jnp.sum(ds1 * s0, axis=0, keepdims=True) * e_last)
        db = db + jnp.where(_rows(CHUNK) == CHUNK - 1, db_last, 0.0)
        dlg = jnp.where(live, _tri_sum(_tri(False), db, 2), 0.0)
        dz = dlg * (1.0 / GATE_NORM) / (1.0 + jnp.exp(z))
        dzb = dz.astype(BF16)

        dlr_h = _dot_nt(dzb, wlr_ref[...])

        @pl.when(h == 0)
        def _():
            dlr_ref[...] = dlr_h

        @pl.when(h > 0)
        def _():
            dlr_ref[...] += dlr_h

        acc_w[h] += _dot_tn(lr_ref[...].astype(BF16), dzb)
        acc_b[h] += jnp.sum(dz, axis=0, keepdims=True)
        dS[h] = e_last * ds1 + _dot_tn(dov, qbb)

        @pl.when(step == n - 1)
        def _():
            dwlr_ref[h] = acc_w[h]
            dblr_ref[h] = acc_b[h]

    return pl.pallas_call(
        body, name="gla_bwd", grid=(n, HEADS),
        in_specs=_gla_specs(dkh, dvh, D, rev) + [
            pl.BlockSpec((None, None, dvh, dkh), lambda c, h: (rev(c), h, 0, 0)),
            pl.BlockSpec((CHUNK, dvh), lambda c, h: (rev(c), h))],
        out_specs=[pl.BlockSpec((CHUNK, dkh), lambda c, h: (rev(c), h)),
                   pl.BlockSpec((CHUNK, dkh), lambda c, h: (rev(c), h)),
                   pl.BlockSpec((CHUNK, dvh), lambda c, h: (rev(c), h)),
                   pl.BlockSpec((CHUNK, LR_W), lambda c, h: (rev(c), 0)),
                   pl.BlockSpec((HEADS, LR_W, dkh), lambda c, h: (0, 0, 0)),
                   pl.BlockSpec((HEADS, 1, dkh), lambda c, h: (0, 0, 0))],
        out_shape=[jax.ShapeDtypeStruct((M, D // 2), BF16), jax.ShapeDtypeStruct((M, D // 2), BF16),
                   jax.ShapeDtypeStruct((M, D), BF16), jax.ShapeDtypeStruct((M, LR_W), F32),
                   jax.ShapeDtypeStruct((HEADS, LR_W, dkh), F32), jax.ShapeDtypeStruct((HEADS, 1, dkh), F32)],
        scratch_shapes=[pltpu.VMEM((HEADS, dvh, dkh), F32), pltpu.VMEM((HEADS, LR_W, dkh), F32),
                        pltpu.VMEM((HEADS, 1, dkh), F32)],
        compiler_params=_params("arbitrary", "arbitrary"),
    )(proj, proj, proj, proj, wlr, blr, st, do)


def gla_post_fwd(o, proj, head_norm, D):
    M = o.shape[0]
    dvh = D // HEADS
    tr = _tile(M, ROW_TILE)

    def body(o_ref, r_ref, hn_ref, out_ref):
        for hd in range(HEADS):
            cols = slice(hd * dvh, (hd + 1) * dvh)
            ov = o_ref[:, cols]
            rs = lax.rsqrt(jnp.mean(ov * ov, axis=-1, keepdims=True) + EPS)
            rv = r_ref[:, cols]
            out_ref[:, cols] = (ov * rs * hn_ref[...] * (rv * _sigmoid(rv))).astype(BF16)

    row = pl.BlockSpec((tr, D), lambda i: (i, 0))
    return pl.pallas_call(
        body, name="gla_post_fwd", grid=(M // tr,),
        in_specs=[row, pl.BlockSpec((tr, D), lambda i: (i, 2)), pl.BlockSpec((1, dvh), lambda i: (0, 0))],
        out_specs=row, out_shape=jax.ShapeDtypeStruct((M, D), BF16),
        compiler_params=_params("parallel"),
    )(o, proj, head_norm)


def gla_post_bwd(dgated, o, proj, head_norm, D):
    M = o.shape[0]
    dvh = D // HEADS
    tr = _tile(M, ROW_TILE)

    def body(dg_ref, o_ref, r_ref, hn_ref, do_ref, dr_ref, dhn_ref):
        @pl.when(pl.program_id(0) == 0)
        def _():
            dhn_ref[...] = jnp.zeros_like(dhn_ref)

        hn = hn_ref[...]
        dhn = jnp.zeros((1, dvh), F32)
        for hd in range(HEADS):
            cols = slice(hd * dvh, (hd + 1) * dvh)
            ov = o_ref[:, cols]
            rs = lax.rsqrt(jnp.mean(ov * ov, axis=-1, keepdims=True) + EPS)
            ohat = ov * rs
            rv = r_ref[:, cols]
            s = _sigmoid(rv)
            dgv = dg_ref[:, cols]
            don = dgv * (rv * s)
            dr_ref[:, cols] = (dgv * ohat * hn * (s * (1.0 + rv * (1.0 - s)))).astype(BF16)
            gd = don * hn
            do_ref[:, cols] = rs * (gd - ohat * jnp.mean(gd * ohat, axis=-1, keepdims=True))
            dhn = dhn + jnp.sum(don * ohat, axis=0, keepdims=True)
        dhn_ref[...] += dhn

    row = pl.BlockSpec((tr, D), lambda i: (i, 0))
    vec = pl.BlockSpec((1, dvh), lambda i: (0, 0))
    return pl.pallas_call(
        body, name="gla_post_bwd", grid=(M // tr,),
        in_specs=[row, row, pl.BlockSpec((tr, D), lambda i: (i, 2)), vec],
        out_specs=[row, row, vec],
        out_shape=[jax.ShapeDtypeStruct((M, D), F32), jax.ShapeDtypeStruct((M, D), BF16),
                   jax.ShapeDtypeStruct((1, dvh), F32)],
        compiler_params=_params("arbitrary"),
    )(dgated, o, proj, head_norm)


def _pool_counts(M, g):
    t = _rows(M) - PAD
    win = jnp.left_shift(2, g)
    return t >= 0, jnp.maximum(jnp.minimum(t + 1, win), 1).astype(F32)


def _window_sum(x, g, M, back):
    sums = []
    s = x
    for lvl in range(4):
        sh = 1 << lvl
        s = s + pltpu.roll(s, (M - sh) if back else sh, 0)
        sums.append(s)
    return jnp.where(g == 0, sums[0], jnp.where(g == 1, sums[1], jnp.where(g == 2, sums[2], sums[3])))


POOL_COLS = 128


def pool_window(hp):
    M, D = hp.shape
    cw = min(POOL_COLS, D // 4)
    per_group = (D // 4) // cw

    def body(h_ref, p_ref):
        g = pl.program_id(0) // per_group
        live, cnt = _pool_counts(M, g)
        hv = h_ref[...]
        p_ref[...] = jnp.where(live, _window_sum(hv, g, M, False) / cnt - hv, 0.0).astype(BF16)

    col = pl.BlockSpec((M, cw), lambda j: (0, j))
    return pl.pallas_call(
        body, name="pool_window", grid=(D // cw,), in_specs=[col], out_specs=col,
        out_shape=jax.ShapeDtypeStruct((M, D), BF16), compiler_params=_params("parallel"),
    )(hp)


def pool_window_bwd(dpooled):
    M, D = dpooled.shape
    cw = min(POOL_COLS, D // 4)
    per_group = (D // 4) // cw

    def body(d_ref, o_ref):
        g = pl.program_id(0) // per_group
        live, cnt = _pool_counts(M, g)
        dv = jnp.where(live, d_ref[...], 0.0)
        o_ref[...] = jnp.where(live, _window_sum(dv / cnt, g, M, True) - dv, 0.0)

    col = pl.BlockSpec((M, cw), lambda j: (0, j))
    return pl.pallas_call(
        body, name="pool_window_bwd", grid=(D // cw,), in_specs=[col], out_specs=col,
        out_shape=jax.ShapeDtypeStruct((M, D), F32), compiler_params=_params("parallel"),
    )(dpooled)


def pool_mix(pooled, x, w, bias, scale):
    M, D = x.shape
    W = D // 4
    tm = _tile(M, 352)

    def body(p_ref, x_ref, w_ref, b_ref, s_ref, out_ref):
        live = (_rows(tm) + pl.program_id(1) * tm) >= PAD
        y = (_dot(p_ref[...], w_ref[...]) + b_ref[...]) * s_ref[...]
        out_ref[...] = x_ref[...] + jnp.where(live, y, 0.0)

    blk = pl.BlockSpec((tm, W), lambda g, i: (i, g))
    vec = pl.BlockSpec((1, W), lambda g, i: (0, g))
    return pl.pallas_call(
        body, name="pool_mix", grid=(4, M // tm),
        in_specs=[blk, blk, pl.BlockSpec((None, W, W), lambda g, i: (g, 0, 0)), vec, vec],
        out_specs=blk, out_shape=jax.ShapeDtypeStruct((M, D), F32),
        compiler_params=_params("parallel", "parallel"),
    )(pooled, x, w, bias, scale)


def pool_mix_bwd(dy, pooled, w, bias, scale, dep=None):
    M, D = dy.shape
    W = D // 4
    tm = _tile(M, 352)

    def body(dy_ref, p_ref, w_ref, b_ref, s_ref, *rest):
        dp_ref, dw_ref, db_ref, ds_ref, acc_w = rest[-5:]
        i = pl.program_id(1)

        @pl.when(i == 0)
        def _():
            acc_w[...] = jnp.zeros_like(acc_w)
            db_ref[...] = jnp.zeros_like(db_ref)
            ds_ref[...] = jnp.zeros_like(ds_ref)

        live = (_rows(tm) + i * tm) >= PAD
        dyv = jnp.where(live, dy_ref[...], 0.0)
        pooled = p_ref[...]
        wv = w_ref[...]
        ds_ref[...] += jnp.sum(dyv * (_dot(pooled, wv) + b_ref[...]), axis=0, keepdims=True)
        dys = dyv * s_ref[...]
        db_ref[...] += jnp.sum(dys, axis=0, keepdims=True)
        dysb = dys.astype(BF16)
        acc_w[...] += _dot_tn(pooled, dysb)
        dp_ref[...] = _dot_nt(dysb, wv)

        @pl.when(i == pl.num_programs(1) - 1)
        def _():
            dw_ref[...] = acc_w[...].astype(BF16)

    blk = pl.BlockSpec((tm, W), lambda g, i: (i, g))
    vec = pl.BlockSpec((1, W), lambda g, i: (0, g))
    wspec = pl.BlockSpec((None, W, W), lambda g, i: (g, 0, 0))
    return pl.pallas_call(
        body, name="pool_mix_bwd", grid=(4, M // tm),
        in_specs=[blk, blk, wspec, vec, vec] + _dep_specs(dep),
        out_specs=[blk, wspec, vec, vec],
        out_shape=[jax.ShapeDtypeStruct((M, D), F32), jax.ShapeDtypeStruct((4, W, W), BF16),
                   jax.ShapeDtypeStruct((1, D), F32), jax.ShapeDtypeStruct((1, D), F32)],
        scratch_shapes=[pltpu.VMEM((W, W), F32)],
        compiler_params=_params("parallel", "arbitrary"),
    )(dy, pooled, w, bias, scale, *_dep_args(dep))


def adamw(w, g, m, v, name):
    shape = w.shape
    C = shape[-1]
    R = w.size // C
    tr = _tile(R, 256, 8)

    def body(w_ref, g_ref, m_ref, v_ref, d_ref, nm_ref, nv_ref):
        gv = g_ref[...]
        nm = ADAM_B1 * m_ref[...] + (1.0 - ADAM_B1) * gv
        nv = ADAM_B2 * v_ref[...] + (1.0 - ADAM_B2) * (gv * gv)
        m_hat = nm / (1.0 - ADAM_B1 ** ADAM_STEP)
        v_hat = nv / (1.0 - ADAM_B2 ** ADAM_STEP)
        d_ref[...] = -ADAM_LR * (m_hat / (jnp.sqrt(v_hat) + ADAM_EPS) + ADAM_WD * w_ref[...])
        nm_ref[...] = nm
        nv_ref[...] = nv

    spec = pl.BlockSpec((tr, C), lambda i: (i, 0))
    outs = pl.pallas_call(
        body, name=name, grid=(R // tr,),
        in_specs=[spec] * 4, out_specs=[spec] * 3,
        out_shape=[jax.ShapeDtypeStruct((R, C), F32)] * 3,
        compiler_params=_params("parallel"),
    )(*[t.reshape(R, C) for t in (w, g, m, v)])
    return [t.reshape(shape) for t in outs]


def add_sibling(grad, recv, core, name):
    _, _, Rh, C = grad.shape
    tr = _tile(Rh, 512)

    def body(core_ref, g_ref, r_ref, o_ref):
        o_ref[...] = (g_ref[...].astype(F32) + r_ref[...].astype(F32)).astype(BF16)

    return pl.pallas_call(
        body, name=name,
        grid_spec=pltpu.PrefetchScalarGridSpec(
            num_scalar_prefetch=1, grid=(N_CHIPS, Rh // tr),
            in_specs=[pl.BlockSpec((None, None, tr, C), lambda j, i, core_ref: (j, core_ref[0], i, 0)),
                      pl.BlockSpec((None, tr, C), lambda j, i, core_ref: (j, i, 0))],
            out_specs=pl.BlockSpec((None, tr, C), lambda j, i, core_ref: (j, i, 0))),
        out_shape=jax.ShapeDtypeStruct((N_CHIPS, Rh, C), BF16),
        compiler_params=_params("parallel", "parallel"),
    )(core, grad, recv)


def add_chips(part, recv, chip, core, group, n, mi, name):
    _, Rh, C = part.shape
    tr = _tile(Rh, 512)

    def body(chip_ref, core_ref, p_ref, r_ref, *rest):
        o_ref = rest[-1]
        acc = p_ref[...].astype(F32)
        for k in range(N_CHIPS - 1):
            acc = acc + r_ref[k].astype(F32)
        o_ref[...] = acc

    carried = [] if group is None else [group]
    return pl.pallas_call(
        body, name=name,
        grid_spec=pltpu.PrefetchScalarGridSpec(
            num_scalar_prefetch=2, grid=(Rh // tr,),
            in_specs=[pl.BlockSpec((None, tr, C), lambda i, chip_ref, core_ref: (chip_ref[0], i, 0)),
                      pl.BlockSpec((N_CHIPS - 1, tr, C), lambda i, chip_ref, core_ref: (0, i, 0))]
            + [ANY] * len(carried),
            out_specs=pl.BlockSpec((None, None, tr, C), lambda i, chip_ref, core_ref: (mi, core_ref[0], i, 0))),
        out_shape=jax.ShapeDtypeStruct((n, 2, Rh, C), F32),
        input_output_aliases={4: 0} if carried else {},
        compiler_params=_params("parallel"),
    )(chip, core, part, recv, *carried)


def stage_shard(shard, mi, chip, name):
    _, _, Rh, C = shard.shape
    tr = _tile(Rh, 512)

    def body(chip_ref, s_ref, o_ref):
        o_ref[...] = s_ref[...].astype(BF16)

    return pl.pallas_call(
        body, name=name,
        grid_spec=pltpu.PrefetchScalarGridSpec(
            num_scalar_prefetch=1, grid=(2, Rh // tr),
            in_specs=[pl.BlockSpec((None, None, tr, C), lambda h, i, chip_ref: (mi, h, i, 0))],
            out_specs=pl.BlockSpec((None, None, tr, C), lambda h, i, chip_ref: (chip_ref[0], h, i, 0))),
        out_shape=jax.ShapeDtypeStruct((N_CHIPS, 2, Rh, C), BF16),
        compiler_params=_params("parallel", "parallel"),
    )(chip, shard)


def sum_devices(gathered):
    _, R, C = gathered.shape

    def body(g_ref, o_ref):
        acc = g_ref[0]
        for d in range(1, N_DEV):
            acc = acc + g_ref[d]
        o_ref[...] = acc

    return pl.pallas_call(
        body, name="sum_devices", grid=(1,),
        in_specs=[pl.BlockSpec((N_DEV, R, C), lambda i: (0, 0, 0))],
        out_specs=pl.BlockSpec((R, C), lambda i: (0, 0)),
        out_shape=jax.ShapeDtypeStruct((R, C), F32),
        compiler_params=_params("arbitrary"),
    )(gathered)


def _place():
    x, y, c = lax.axis_index("x"), lax.axis_index("y"), lax.axis_index("c")
    others = [(1 - x, y), (x, 1 - y), (1 - x, 1 - y)]
    return x, y, c, others


def _remote(src, dst, send_sems, recv_sems, idx, device):
    return pltpu.make_async_remote_copy(src_ref=src, dst_ref=dst, send_sem=send_sems.at[idx],
                                        recv_sem=recv_sems.at[idx], device_id=device, device_id_type=MESH)


HBM = pl.BlockSpec(memory_space=pltpu.HBM)
SEM = pl.BlockSpec(memory_space=pltpu.SEMAPHORE)
EFFECT = pltpu.SideEffectType.DATAFLOW_SIDE_EFFECTING


def _in_hbm(t):
    return pltpu.with_memory_space_constraint(t, pltpu.HBM)


def _own_slice(buf, me, c):
    return buf.at[me, c] if len(buf.shape) == 4 else buf.at[me]


def gather_start(staged, bucket_sizes, name):
    n, nb = len(staged), len(bucket_sizes)

    def body(*refs):
        in_refs, sems, token = refs[:n], refs[n:n + 2 * nb], refs[-1]
        x, y, c, others = _place()
        me = 2 * x + y
        t = 0
        for b, size in enumerate(bucket_sizes):
            for i in range(size):
                mine = _own_slice(in_refs[t], me, c)
                for k, chip in enumerate(others):
                    _remote(mine, mine, sems[2 * b], sems[2 * b + 1], 3 * i + k, (*chip, c)).start()
                t += 1
        token[...] = jnp.zeros_like(token)

    sem_shapes = [pltpu.SemaphoreType.DMA((3 * size,)) for size in bucket_sizes for _ in range(2)]
    outs = pl.pallas_call(
        body, name=name,
        out_shape=sem_shapes + [pltpu.HBM(s.shape, s.dtype) for s in staged] + [jax.ShapeDtypeStruct((8, 128), F32)],
        in_specs=[HBM] * n, out_specs=[SEM] * (2 * nb) + [HBM] * n + [pl.BlockSpec(memory_space=pltpu.VMEM)],
        input_output_aliases={t: 2 * nb + t for t in range(n)},
        compiler_params=pltpu.CompilerParams(has_side_effects=EFFECT),
    )(*[_in_hbm(s) for s in staged])
    sems = [(outs[2 * b], outs[2 * b + 1]) for b in range(nb)]
    return sems, list(outs[2 * nb:2 * nb + n]), outs[-1]


def gather_wait(bufs, sems, after, name):
    n = len(bufs)

    def body(*refs):
        in_refs, send_sems, recv_sems = refs[:n], refs[n], refs[n + 1]
        x, y, c, others = _place()
        me = 2 * x + y
        for i in range(n):
            mine = _own_slice(in_refs[i], me, c)
            for k, (ox, oy) in enumerate(others):
                cp = _remote(mine, _own_slice(in_refs[i], 2 * ox + oy, c), send_sems, recv_sems, 3 * i + k,
                             (ox, oy, c))
                cp.wait_send()
                cp.wait_recv()

    return pl.pallas_call(
        body, name=name, out_shape=[pltpu.HBM(b.shape, b.dtype) for b in bufs],
        in_specs=[HBM] * n + [SEM, SEM, ANY], out_specs=[HBM] * n,
        input_output_aliases={t: t for t in range(n)},
        compiler_params=pltpu.CompilerParams(has_side_effects=EFFECT),
    )(*bufs, *sems, after)


def forward_to_sibling(bufs, name):
    n = len(bufs)

    def body(*refs):
        out_refs, (send_sems, recv_sems) = refs[n:2 * n], refs[2 * n:]
        x, y, c, others = _place()
        sibling = (x, y, 1 - c)
        copies = []
        for t in range(n):
            for k, (ox, oy) in enumerate(others):
                mine = out_refs[t].at[2 * ox + oy, c]
                cp = _remote(mine, mine, send_sems, recv_sems, 3 * t + k, sibling)
                cp.start()
                copies.append(cp)
        for t in range(n):
            for k, (ox, oy) in enumerate(others):
                theirs = out_refs[t].at[2 * ox + oy, 1 - c]
                _remote(theirs, theirs, send_sems, recv_sems, 3 * t + k, sibling).wait_recv()
        for cp in copies:
            cp.wait_send()

    return pl.pallas_call(
        body, name=name, in_specs=[ANY] * n, out_specs=[ANY] * n,
        out_shape=[jax.ShapeDtypeStruct(b.shape, b.dtype) for b in bufs],
        input_output_aliases={t: t for t in range(n)},
        scratch_shapes=[pltpu.SemaphoreType.DMA((3 * n,)), pltpu.SemaphoreType.DMA((3 * n,))],
    )(*bufs)


def sibling_start(grads, name):
    n = len(grads)
    lands = [lax.empty((N_CHIPS,) + g.shape[2:], g.dtype) for g in grads]

    def body(*refs):
        in_refs, land_refs, send_sems, recv_sems, token = refs[:n], refs[n:2 * n], refs[2 * n], refs[2 * n + 1], refs[-1]
        x, y, c, _ = _place()
        for t in range(n):
            for j in range(N_CHIPS):
                _remote(in_refs[t].at[j, 1 - c], land_refs[t].at[j], send_sems, recv_sems, N_CHIPS * t + j,
                        (x, y, 1 - c)).start()
        token[...] = jnp.zeros_like(token)

    outs = pl.pallas_call(
        body, name=name,
        out_shape=[pltpu.SemaphoreType.DMA((N_CHIPS * n,))] * 2 + [pltpu.HBM(t.shape, t.dtype) for t in grads + lands]
        + [jax.ShapeDtypeStruct((8, 128), F32)],
        in_specs=[HBM] * (2 * n), out_specs=[SEM, SEM] + [HBM] * (2 * n) + [pl.BlockSpec(memory_space=pltpu.VMEM)],
        input_output_aliases={t: 2 + t for t in range(2 * n)},
        compiler_params=pltpu.CompilerParams(has_side_effects=EFFECT),
    )(*[_in_hbm(t) for t in grads + lands])
    return (outs[0], outs[1]), list(outs[2:2 + n]), list(outs[2 + n:2 + 2 * n]), outs[-1]


def sibling_wait(grads, lands, sems, after, name):
    n = len(grads)

    def body(*refs):
        in_refs, land_refs, send_sems, recv_sems = refs[:n], refs[n:2 * n], refs[2 * n], refs[2 * n + 1]
        x, y, c, _ = _place()
        for t in range(n):
            for j in range(N_CHIPS):
                cp = _remote(in_refs[t].at[j, 1 - c], land_refs[t].at[j], send_sems, recv_sems, N_CHIPS * t + j,
                             (x, y, 1 - c))
                cp.wait_send()
                cp.wait_recv()

    outs = pl.pallas_call(
        body, name=name, out_shape=[pltpu.HBM(t.shape, t.dtype) for t in grads + lands],
        in_specs=[HBM] * (2 * n) + [SEM, SEM, ANY], out_specs=[HBM] * (2 * n),
        input_output_aliases={t: t for t in range(2 * n)},
        compiler_params=pltpu.CompilerParams(has_side_effects=EFFECT),
    )(*grads, *lands, *sems, after)
    return list(outs[:n]), list(outs[n:])


def reduce_start(parts, name):
    n = len(parts)
    lands = [lax.empty((N_CHIPS - 1,) + p.shape[1:], p.dtype) for p in parts]

    def body(*refs):
        in_refs, land_refs, send_sems, recv_sems, token = refs[:n], refs[n:2 * n], refs[2 * n], refs[2 * n + 1], refs[-1]
        x, y, c, others = _place()
        for t in range(n):
            for k, (ox, oy) in enumerate(others):
                _remote(in_refs[t].at[2 * ox + oy], land_refs[t].at[k], send_sems, recv_sems, 3 * t + k,
                        (ox, oy, c)).start()
        token[...] = jnp.zeros_like(token)

    outs = pl.pallas_call(
        body, name=name,
        out_shape=[pltpu.SemaphoreType.DMA((3 * n,))] * 2 + [pltpu.HBM(t.shape, t.dtype) for t in parts + lands]
        + [jax.ShapeDtypeStruct((8, 128), F32)],
        in_specs=[HBM] * (2 * n), out_specs=[SEM, SEM] + [HBM] * (2 * n) + [pl.BlockSpec(memory_space=pltpu.VMEM)],
        input_output_aliases={t: 2 + t for t in range(2 * n)},
        compiler_params=pltpu.CompilerParams(has_side_effects=EFFECT),
    )(*[_in_hbm(t) for t in parts + lands])
    return (outs[0], outs[1]), list(outs[2:2 + n]), list(outs[2 + n:2 + 2 * n]), outs[-1]


def reduce_wait(parts, lands, sems, after, name):
    n = len(parts)

    def body(*refs):
        in_refs, land_refs, send_sems, recv_sems = refs[:n], refs[n:2 * n], refs[2 * n], refs[2 * n + 1]
        x, y, c, others = _place()
        for t in range(n):
            for k, (ox, oy) in enumerate(others):
                cp = _remote(in_refs[t].at[2 * ox + oy], land_refs[t].at[k], send_sems, recv_sems, 3 * t + k,
                             (ox, oy, c))
                cp.wait_send()
                cp.wait_recv()

    outs = pl.pallas_call(
        body, name=name, out_shape=[pltpu.HBM(t.shape, t.dtype) for t in parts + lands],
        in_specs=[HBM] * (2 * n) + [SEM, SEM, ANY], out_specs=[HBM] * (2 * n),
        input_output_aliases={t: t for t in range(2 * n)},
        compiler_params=pltpu.CompilerParams(has_side_effects=EFFECT),
    )(*parts, *lands, *sems, after)
    return list(outs[:n]), list(outs[n:])


def exchange_halves(groups, name):
    n_groups = len(groups)
    slots = [(gi, mi) for gi, grp in enumerate(groups) for mi in range(grp.shape[0])]

    def body(*refs):
        out_refs = refs[n_groups:2 * n_groups]
        send_sems, recv_sems = refs[2 * n_groups:]
        x, y, c, _ = _place()
        sibling = (x, y, 1 - c)
        copies = []
        for t, (gi, mi) in enumerate(slots):
            mine = out_refs[gi].at[mi, c]
            cp = _remote(mine, mine, send_sems, recv_sems, t, sibling)
            cp.start()
            copies.append(cp)
        for t, (gi, mi) in enumerate(slots):
            theirs = out_refs[gi].at[mi, 1 - c]
            _remote(theirs, theirs, send_sems, recv_sems, t, sibling).wait_recv()
        for cp in copies:
            cp.wait_send()

    return pl.pallas_call(
        body, name=name, in_specs=[ANY] * n_groups, out_specs=[ANY] * n_groups,
        out_shape=[jax.ShapeDtypeStruct(g.shape, g.dtype) for g in groups],
        input_output_aliases={gi: gi for gi in range(n_groups)},
        scratch_shapes=[pltpu.SemaphoreType.DMA((len(slots),)), pltpu.SemaphoreType.DMA((len(slots),))],
    )(*groups)


def gather_devices(buf):
    def body(in_ref, out_ref, send_sems, recv_sems, local_sem):
        x, y, c, _ = _place()
        me = 4 * x + 2 * y + c
        local = pltpu.make_async_copy(in_ref, out_ref.at[me], local_sem)
        local.start()
        copies = []
        for k in range(1, N_DEV):
            fx, fy, fc = (k >> 2) & 1, (k >> 1) & 1, k & 1
            peer = (x ^ fx, y ^ fy, c ^ fc)
            cp = _remote(in_ref, out_ref.at[me], send_sems, recv_sems, k - 1, peer)
            cp.start()
            copies.append(cp)
        for k in range(1, N_DEV):
            fx, fy, fc = (k >> 2) & 1, (k >> 1) & 1, k & 1
            theirs = out_ref.at[4 * (x ^ fx) + 2 * (y ^ fy) + (c ^ fc)]
            _remote(theirs, theirs, send_sems, recv_sems, k - 1, (x, y, c)).wait_recv()
        for cp in copies:
            cp.wait_send()
        local.wait()

    return pl.pallas_call(
        body, name="gather_devices", in_specs=[ANY], out_specs=ANY,
        out_shape=jax.ShapeDtypeStruct((N_DEV,) + buf.shape, buf.dtype),
        scratch_shapes=[pltpu.SemaphoreType.DMA((N_DEV - 1,)), pltpu.SemaphoreType.DMA((N_DEV - 1,)),
                        pltpu.SemaphoreType.DMA],
    )(buf)


class GradReducer:
    def __init__(self, core, chip, kinds):
        self.core, self.chip = core, chip
        self.sizes = dict(kinds)
        self.groups = {kind: None for kind, _ in kinds}

    def send(self, grad, kind, mi, tag):
        array = grad.reshape(N_CHIPS, 2, -1, grad.shape[-1])
        sems, arrays, lands, token = sibling_start([array], f"reduce_sibling_start_{tag}")
        return (sems, arrays, lands, kind, mi, tag), token

    def begin(self, sent, after, tag):
        parts, slots = [], []
        for sems, arrays, lands, kind, mi, sent_tag in sent:
            arrays, lands = sibling_wait(arrays, lands, sems, after, f"reduce_sibling_wait_{sent_tag}")
            parts.append(add_sibling(arrays[0], lands[0], self.core, f"reduce_add_sibling_{sent_tag}"))
            slots.append((kind, mi))
        sems, parts, lands, token = reduce_start(parts, f"reduce_start_{tag}")
        return (sems, parts, lands, slots, tag), token

    def end(self, state, after):
        sems, parts, lands, slots, tag = state
        parts, lands = reduce_wait(parts, lands, sems, after, f"reduce_wait_{tag}")
        for t, (kind, mi) in enumerate(slots):
            self.groups[kind] = add_chips(parts[t], lands[t], self.chip, self.core, self.groups[kind],
                                          self.sizes[kind], mi, f"reduce_add_chips_{tag}_{t}")

    def finish(self):
        kinds = list(self.groups)
        return dict(zip(kinds, exchange_halves([self.groups[k] for k in kinds], "reduce_swap")))


def _ffn_fwd(x, gain, wg, wu, wd, tag):
    h, rstd = rmsnorm_fwd(x, gain, BF16, f"ffn_norm_{tag}")
    gate, up = ffn_gateup(h, wg, wu, f"ffn_gateup_{tag}")
    out = mm_residual([gate, up], wd, x, 0.5, wd.shape[0] // N_CHIPS, f"ffn_down_{tag}")
    return out, (x, gain, h, rstd, gate, up)


def _ffn_bwd(dout, saved, wg, wu, wd, index, reducer, dep=None, per_tensor=False):
    x, gain, h, rstd, gate, up = saved
    D = x.shape[1]
    Fs = wg.shape[2]
    td = _tile(D, 512, 128)
    tag = f"ffn{index}"
    begun = []

    def begin(sent, after, suffix):
        state, token = reducer.begin(sent, after, tag + suffix)
        begun.append(state)
        return token

    dgate, dup, act = ffn_bwd_act(dout, wd, gate, up, f"ffn_bwd_act_{index}", dep=dep)
    d_wd = mm_tn(act, dout, Fs, td, f"ffn_bwd_wd_{index}", b_scale=0.5)
    sent_d, tok = reducer.send(d_wd, "down", index, tag + "d")
    d_wg = mm_tn(h, dgate, td, Fs, f"ffn_bwd_wg_{index}", stacked_out=True, dep=tok)
    toks = [begin([sent_d], d_wg, "d")] if per_tensor else []
    sent_g, tok = reducer.send(d_wg, "gate", index, tag + "g")
    d_wu = mm_tn(h, dup, td, Fs, f"ffn_bwd_wu_{index}", stacked_out=True, dep=toks + [tok])
    toks = [begin([sent_g], d_wu, "g")] if per_tensor else []
    sent_u, tok = reducer.send(d_wu, "up", index, tag + "u")
    dh = mm_nt([(dgate, wg), (dup, wu)], _tile(x.shape[0], 352), D, Fs, f"ffn_bwd_dh_{index}", stacked_w=True,
               dep=toks + [tok])
    tok = begin([sent_u] if per_tensor else [sent_d, sent_g, sent_u], dh, "u")
    dx, dgain = rmsnorm_bwd(dh, x, gain, rstd, dout, f"ffn_norm_bwd_{index}")
    return dx, dgain, begun, tok


def kernel(x, meta, ffn_norm, ffn_w_gate, ffn_w_up, ffn_w_down, gla_norm, gla_w_in, gla_w_lr, gla_b_lr, gla_head_norm, gla_w_out, pool_norm, pool_w, pool_b, pool_scale, final_norm, loss_target, m_meta, m_ffn_norm, m_ffn_w_gate, m_ffn_w_up, m_ffn_w_down, m_gla_norm, m_gla_w_in, m_gla_w_lr, m_gla_b_lr, m_gla_head_norm, m_gla_w_out, m_pool_norm, m_pool_w, m_pool_b, m_pool_scale, m_final_norm, v_meta, v_ffn_norm, v_ffn_w_gate, v_ffn_w_up, v_ffn_w_down, v_gla_norm, v_gla_w_in, v_gla_w_lr, v_gla_b_lr, v_gla_head_norm, v_gla_w_out, v_pool_norm, v_pool_w, v_pool_b, v_pool_scale, v_final_norm):
    S, D = x.shape[1], x.shape[2]
    M = OFF + S
    Dq = D // N_CHIPS
    Fs = ffn_w_gate.shape[3]
    F = N_CHIPS * Fs
    dk = D // 2
    n_in = gla_w_in.shape[2]
    W = D // 4
    core = lax.axis_index("c").astype(jnp.int32).reshape(1)
    chip_id = 2 * lax.axis_index("x") + lax.axis_index("y")
    chip = chip_id.astype(jnp.int32).reshape(1)

    small = jnp.concatenate([_pad_rows(t) for t in (
        meta, ffn_norm.reshape(4, Dq), gla_w_lr.reshape(8, Dq), pool_norm, pool_b.reshape(1, Dq), pool_scale)],
        axis=0)
    def stage(w, kind, n, mi):
        return stage_shard(w.reshape(n, 2, -1, w.shape[-1]), mi, chip, f"stage_{kind}_{mi}")

    ffn_stage = lambda mi: [stage(ffn_w_gate, "gate", 4, mi), stage(ffn_w_up, "up", 4, mi),
                            stage(ffn_w_down, "down", 4, mi)]
    small_stage = lax.dynamic_update_slice(jnp.zeros((N_CHIPS,) + small.shape, F32), small[None], (chip_id, 0, 0))
    first = ffn_stage(0)
    buckets = [first[:2] + [small_stage], first[2:],
               [stage(gla_w_in, "win", 1, 0), stage(gla_w_out, "wout", 1, 0)],
               ffn_stage(1), ffn_stage(2), [stage(pool_w, "pool", 1, 0)] + ffn_stage(3)]
    sizes = [len(b) for b in buckets]
    starts = [sum(sizes[:b]) for b in range(len(buckets))]
    gather_sems, in_flight, gather_token = gather_start([t for b in buckets for t in b], sizes, "gather_start")

    def arrive(b, after, n_big):
        bufs = gather_wait(in_flight[starts[b]:starts[b] + sizes[b]], gather_sems[b], after, f"gather_wait_{b}")
        return forward_to_sibling(bufs[:n_big], f"gather_forward_{b}") + bufs[n_big:]

    ffn_w = lambda t: (t[0].reshape(N_CHIPS, D, Fs), t[1].reshape(N_CHIPS, D, Fs), t[2].reshape(F, D))
    got = arrive(0, gather_token, 2)
    wg, wu, wd = [None] * 4, [None] * 4, [None] * 4
    wg[0], wu[0] = got[0].reshape(N_CHIPS, D, Fs), got[1].reshape(N_CHIPS, D, Fs)
    sm = got[2]
    unshard = lambda t: t.transpose(1, 0, 2).reshape(t.shape[1], D)
    meta_f = unshard(sm[:, 0:16])
    ffn_norm_f = unshard(sm[:, 16:20])
    w_lr_f = sm[:, 24:32].reshape(N_CHIPS, GATE_RANK, dk // N_CHIPS).transpose(1, 0, 2).reshape(GATE_RANK, dk)
    pool_norm_f = sm[:, 32].reshape(1, D)
    pool_b_f = sm[:, 40].reshape(N_CHIPS, 4, W // N_CHIPS).transpose(1, 0, 2).reshape(1, D)
    pool_scale_f = sm[:, 48].reshape(1, D)
    wlr_pad = jnp.pad(w_lr_f.astype(BF16), ((0, LR_W - GATE_RANK), (0, 0)))
    final_g = final_norm.reshape(1, D)
    qkv = 2 * dk + D

    x0 = jnp.concatenate([jnp.zeros((PAD, D), F32), meta_f, x[0]], axis=0)
    target = jnp.pad(loss_target[0], ((OFF, 0), (0, 0)))
    h0, rstd0 = rmsnorm_fwd(x0, ffn_norm_f[0:1], BF16, "ffn_norm_0")
    gate0, up0 = ffn_gateup(h0, wg[0], wu[0], "ffn_gateup_0")
    wd[0] = arrive(1, gate0, 1)[0].reshape(F, D)
    x1 = mm_residual([gate0, up0], wd[0], x0, 0.5, Fs, "ffn_down_0")
    ffn0 = (x0, ffn_norm_f[0:1], h0, rstd0, gate0, up0)
    got = arrive(2, x1, 2)
    w_in = got[0].reshape(N_CHIPS, D, n_in).transpose(1, 0, 2).reshape(D, N_CHIPS * n_in)
    w_out = got[1].reshape(D, D)
    w_all = jnp.concatenate([w_in[:, :qkv], w_in[:, qkv + GATE_RANK:], w_in[:, qkv:qkv + GATE_RANK],
                             jnp.zeros((D, LR_W - GATE_RANK), BF16)], axis=1)
    hg, rstd_g = rmsnorm_fwd(x1, gla_norm, BF16, "gla_norm")
    proj = mm_nn(hg, w_all, F32, "gla_proj")
    o, st = gla_fwd(proj, wlr_pad, gla_b_lr, D)
    gated = gla_post_fwd(o, proj, gla_head_norm, D)
    x2 = mm_residual([gated], w_out, x1, 1.0, D, "gla_out")
    wg[1], wu[1], wd[1] = ffn_w(arrive(3, x2, 3))
    x3, ffn1 = _ffn_fwd(x2, ffn_norm_f[1:2], wg[1], wu[1], wd[1], "1")
    wg[2], wu[2], wd[2] = ffn_w(arrive(4, x3, 3))
    x4, ffn2 = _ffn_fwd(x3, ffn_norm_f[2:3], wg[2], wu[2], wd[2], "2")
    got = arrive(5, x4, 4)
    w_pool = got[0].reshape(N_CHIPS, 4, W // N_CHIPS, W).transpose(1, 0, 2, 3).reshape(4, W, W)
    wg[3], wu[3], wd[3] = ffn_w(got[1:])
    hp, rstd_p = rmsnorm_fwd(x4, pool_norm_f, F32, "pool_norm")
    pooled = pool_window(hp)
    x5 = pool_mix(pooled, x4, w_pool, pool_b_f, pool_scale_f)
    x6, ffn3 = _ffn_fwd(x5, ffn_norm_f[3:4], wg[3], wu[3], wd[3], "3")
    loss, dx6, d_final = final_loss(x6, final_g, target)

    reducer = GradReducer(core, chip, [("gate", 4), ("up", 4), ("down", 4), ("win", 1), ("wout", 1), ("pool", 1)])

    def settle(begun, after):
        for state in begun:
            reducer.end(state, after)

    dx5, dn3, red3, tok = _ffn_bwd(dx6, ffn3, wg[3], wu[3], wd[3], 3, reducer)
    dpooled, d_wpool, d_pool_b, d_pool_scale = pool_mix_bwd(dx5, pooled, w_pool, pool_b_f, pool_scale_f, dep=tok)
    dhp = pool_window_bwd(dpooled)
    dx4, d_pool_norm = rmsnorm_bwd(dhp, x4, pool_norm_f, rstd_p, dx5, "pool_norm_bwd")
    d_wpool = d_wpool.reshape(4, N_CHIPS, W // N_CHIPS, W).transpose(1, 0, 2, 3)
    sent_p, tok = reducer.send(d_wpool, "pool", 0, "pool")
    dx3, dn2, red2, tok = _ffn_bwd(dx4, ffn2, wg[2], wu[2], wd[2], 2, reducer, dep=tok)
    redp, tok_p = reducer.begin([sent_p], dx3, "pool")
    settle(red3, dx3)
    dx2, dn1, red1, tok = _ffn_bwd(dx3, ffn1, wg[1], wu[1], wd[1], 1, reducer, dep=[tok, tok_p])
    settle(red2 + [redp], dx2)
    tm = _tile(M, 352)
    td = _tile(D, 512, 128)
    d_wout = mm_tn(gated, dx2, td, td, "gla_out_bwd_w", dep=tok)
    sent_o, tok = reducer.send(d_wout, "wout", 0, "wout")
    dgated = mm_nt([(dx2, w_out)], tm, td, D, "gla_out_bwd_act", dep=tok)
    redo, tok_o = reducer.begin([sent_o], dgated, "wout")
    do, dr, d_head_norm = gla_post_bwd(dgated, o, proj, gla_head_norm, D)
    dq, dkk, dv, dlr, dwlr, dblr = gla_bwd(proj, wlr_pad, gla_b_lr, st, do, D)
    dproj = jnp.concatenate([dq, dkk, dv, dr, dlr.astype(BF16)], axis=1)
    tp = _tile(proj.shape[1], 896, 128)
    d_wall = mm_tn(hg, dproj, td, tp, "gla_proj_bwd_w", dep=tok_o)
    d_win = jnp.concatenate([d_wall[:, :qkv], d_wall[:, qkv + D:qkv + D + GATE_RANK], d_wall[:, qkv:qkv + D]], axis=1)
    d_win = d_win.reshape(D, N_CHIPS, n_in).transpose(1, 0, 2)
    sent_i, tok = reducer.send(d_win, "win", 0, "win")
    dhg = mm_nt([(dproj, w_all)], tm, D, tp, "gla_proj_bwd_act", dep=tok)
    redi, tok = reducer.begin([sent_i], dhg, "win")
    dx1, d_gla_norm = rmsnorm_bwd(dhg, x1, gla_norm, rstd_g, dx2, "gla_norm_bwd")
    settle(red1 + [redo], dx1)
    dx0, dn0, red0, _ = _ffn_bwd(dx1, ffn0, wg[0], wu[0], wd[0], 0, reducer, dep=tok, per_tensor=True)
    settle([redi], dx0)

    d_wlr = dwlr[:, :GATE_RANK].transpose(1, 0, 2).reshape(GATE_RANK, dk)
    pieces = [dx0[PAD:OFF], dn0, dn1, dn2, dn3, d_gla_norm, d_wlr,
              dblr.reshape(1, dk), d_head_norm, d_pool_norm, d_pool_b, d_pool_scale, d_final]
    packed = jnp.concatenate([_pad_rows(p.reshape(-1, Dq)) for p in pieces], axis=0)
    total = sum_devices(gather_devices(packed))

    settle(red0, total)
    reduced = reducer.finish()
    g_gate = reduced["gate"].reshape(ffn_w_gate.shape)
    g_up = reduced["up"].reshape(ffn_w_up.shape)
    g_down = reduced["down"].reshape(ffn_w_down.shape)
    g_win = reduced["win"].reshape(gla_w_in.shape)
    g_wout = reduced["wout"].reshape(gla_w_out.shape)
    g_wpool = reduced["pool"].reshape(pool_w.shape)
    sums, at = [], 0
    for p in pieces:
        r = p.size // Dq
        sums.append(total[at:at + r].reshape(p.shape))
        at += r + (-r % 8)
    (s_meta, s_n0, s_n1, s_n2, s_n3, s_gla_norm, s_wlr, s_blr, s_head_norm, s_pool_norm, s_pool_b, s_pool_scale,
     s_final) = sums
    s_ffn_norm = jnp.stack([s_n0, s_n1, s_n2, s_n3], axis=0)[:, 0]
    mine = lambda t, width: lax.dynamic_slice_in_dim(t, chip_id * width, width, axis=t.ndim - 1)
    g_meta = mine(s_meta, Dq)
    g_ffn_norm = mine(s_ffn_norm, Dq).reshape(ffn_norm.shape)
    g_gla_norm = s_gla_norm
    g_wlr = mine(s_wlr, dk // N_CHIPS).reshape(gla_w_lr.shape)
    g_blr = s_blr
    g_head_norm = s_head_norm
    g_pool_norm = mine(s_pool_norm, Dq)
    g_pool_b = mine(s_pool_b.reshape(4, W), W // N_CHIPS).reshape(pool_b.shape)
    g_pool_scale = mine(s_pool_scale, Dq)
    g_final = s_final.reshape(final_norm.shape)

    weights = [meta, ffn_norm, ffn_w_gate, ffn_w_up, ffn_w_down, gla_norm, gla_w_in, gla_w_lr, gla_b_lr,
               gla_head_norm, gla_w_out, pool_norm, pool_w, pool_b, pool_scale, final_norm]
    moments_m = [m_meta, m_ffn_norm, m_ffn_w_gate, m_ffn_w_up, m_ffn_w_down, m_gla_norm, m_gla_w_in, m_gla_w_lr,
                 m_gla_b_lr, m_gla_head_norm, m_gla_w_out, m_pool_norm, m_pool_w, m_pool_b, m_pool_scale,
                 m_final_norm]
    moments_v = [v_meta, v_ffn_norm, v_ffn_w_gate, v_ffn_w_up, v_ffn_w_down, v_gla_norm, v_gla_w_in, v_gla_w_lr,
                 v_gla_b_lr, v_gla_head_norm, v_gla_w_out, v_pool_norm, v_pool_w, v_pool_b, v_pool_scale,
                 v_final_norm]
    grads_w = [g_meta, g_ffn_norm, g_gate, g_up, g_down, g_gla_norm, g_win, g_wlr, g_blr, g_head_norm, g_wout,
               g_pool_norm, g_wpool, g_pool_b, g_pool_scale, g_final]
    deltas, new_m, new_v = [], [], []
    for i, (w, g, m, v) in enumerate(zip(weights, grads_w, moments_m, moments_v)):
        d, nm, nv = adamw(w, g, m, v, f"adamw_{i}")
        deltas.append(d)
        new_m.append(nm)
        new_v.append(nv)

    loss = lax.psum(loss[0, 0], ("x", "y", "c"))
    grad_x = dx0[OFF:][None]
    return (loss, grad_x, *grads_w, *deltas, *new_m, *new_v)
```

```python
import functools

import jax
import jax.numpy as jnp
from jax import lax
from jax.experimental import pallas as pl
from jax.experimental.pallas import tpu as pltpu

F32 = jnp.float32
BF16 = jnp.bfloat16
MESH = pl.DeviceIdType.MESH
ANY = pl.BlockSpec(memory_space=pl.ANY)

N_META = 16
CHUNK = 64
PAD = CHUNK - N_META
OFF = PAD + N_META
EPS = 1e-6
HEADS = 4
GATE_RANK = 16
GATE_NORM = 16.0
LR_W = 128
N_CHIPS = 4
N_DEV = 8
ADAM_LR, ADAM_B1, ADAM_B2, ADAM_EPS, ADAM_WD, ADAM_STEP = 0.001, 0.9, 0.999, 1e-08, 0.01, 10
VMEM_LIMIT = 56 * 1024 * 1024
ROW_TILE = 176
ONE_BUFFER = pl.Buffered(1)


def _tile(n, target, mult=16):
    best = None
    for d in range(mult, min(n, target) + 1, mult):
        if n % d == 0:
            best = d
    return best if best is not None else n


def _params(*sem):
    return pltpu.CompilerParams(dimension_semantics=sem, vmem_limit_bytes=VMEM_LIMIT)


def _dot(a, b):
    return jnp.dot(a, b, preferred_element_type=F32)


def _dot_nt(a, b):
    return lax.dot_general(a, b, (((1,), (1,)), ((), ())), preferred_element_type=F32)


def _dot_tn(a, b):
    return lax.dot_general(a, b, (((0,), (0,)), ((), ())), preferred_element_type=F32)


MXU_WIDTH = 256


def _chunks(n):
    return [slice(lo, min(lo + MXU_WIDTH, n)) for lo in range(0, n, MXU_WIDTH)]


def _sigmoid(x):
    return 1.0 / (1.0 + jnp.exp(-x))


def _rows(tile, width=1):
    return lax.broadcasted_iota(jnp.int32, (tile, width), 0)


def _dep_args(dep):
    if dep is None:
        return []
    return list(dep) if isinstance(dep, (list, tuple)) else [dep]


def _dep_specs(dep):
    return [ANY] * len(_dep_args(dep))


def _pad_rows(t):
    return jnp.pad(t, ((0, -t.shape[0] % 8), (0, 0)))


def rmsnorm_fwd(x, g, out_dtype, name):
    M, D = x.shape
    tr = _tile(M, ROW_TILE)

    def body(x_ref, g_ref, h_ref, r_ref):
        xv = x_ref[...]
        r = lax.rsqrt(jnp.mean(xv * xv, axis=-1, keepdims=True) + EPS)
        h_ref[...] = (xv * r * g_ref[...]).astype(out_dtype)
        r_ref[...] = r

    return pl.pallas_call(
        body, name=name, grid=(M // tr,),
        in_specs=[pl.BlockSpec((tr, D), lambda i: (i, 0)), pl.BlockSpec((1, D), lambda i: (0, 0))],
        out_specs=[pl.BlockSpec((tr, D), lambda i: (i, 0)), pl.BlockSpec((tr, 1), lambda i: (i, 0))],
        out_shape=[jax.ShapeDtypeStruct((M, D), out_dtype), jax.ShapeDtypeStruct((M, 1), F32)],
        compiler_params=_params("parallel"),
    )(x, g)


def rmsnorm_bwd(dh, x, g, rstd, dres, name):
    M, D = x.shape
    tr = _tile(M, ROW_TILE)

    def body(dh_ref, x_ref, g_ref, r_ref, dres_ref, dx_ref, dg_ref):
        @pl.when(pl.program_id(0) == 0)
        def _():
            dg_ref[...] = jnp.zeros_like(dg_ref)

        r = r_ref[...]
        xhat = x_ref[...] * r
        dhv = dh_ref[...]
        gd = dhv * g_ref[...]
        dx_ref[...] = dres_ref[...] + r * (gd - xhat * jnp.mean(gd * xhat, axis=-1, keepdims=True))
        dg_ref[...] += jnp.sum(dhv * xhat, axis=0, keepdims=True)

    row = pl.BlockSpec((tr, D), lambda i: (i, 0))
    vec = pl.BlockSpec((1, D), lambda i: (0, 0))
    return pl.pallas_call(
        body, name=name, grid=(M // tr,),
        in_specs=[row, row, vec, pl.BlockSpec((tr, 1), lambda i: (i, 0)), row],
        out_specs=[row, vec],
        out_shape=[jax.ShapeDtypeStruct((M, D), F32), jax.ShapeDtypeStruct((1, D), F32)],
        compiler_params=_params("arbitrary"),
    )(dh, x, g, rstd, dres)


def final_loss(x, g, target):
    M, D = x.shape
    tr = _tile(M, ROW_TILE)

    def body(x_ref, g_ref, t_ref, loss_ref, dx_ref, dg_ref):
        i = pl.program_id(0)

        @pl.when(i == 0)
        def _():
            loss_ref[...] = jnp.zeros_like(loss_ref)
            dg_ref[...] = jnp.zeros_like(dg_ref)

        live = (_rows(tr) + i * tr) >= OFF
        xv = x_ref[...]
        gv = g_ref[...]
        r = lax.rsqrt(jnp.mean(xv * xv, axis=-1, keepdims=True) + EPS)
        xhat = xv * r
        err = jnp.where(live, xhat * gv - t_ref[...], 0.0)
        loss_ref[...] += 0.5 * jnp.sum(jnp.mean(err * err, axis=-1, keepdims=True), axis=0, keepdims=True)
        dy = err * (1.0 / D)
        gd = dy * gv
        dx_ref[...] = r * (gd - xhat * jnp.mean(gd * xhat, axis=-1, keepdims=True))
        dg_ref[...] += jnp.sum(dy * xhat, axis=0, keepdims=True)

    row = pl.BlockSpec((tr, D), lambda i: (i, 0))
    vec = pl.BlockSpec((1, D), lambda i: (0, 0))
    return pl.pallas_call(
        body, name="final_loss", grid=(M // tr,),
        in_specs=[row, vec, row],
        out_specs=[pl.BlockSpec((1, 1), lambda i: (0, 0)), row, vec],
        out_shape=[jax.ShapeDtypeStruct((1, 1), F32), jax.ShapeDtypeStruct((M, D), F32),
                   jax.ShapeDtypeStruct((1, D), F32)],
        compiler_params=_params("arbitrary"),
    )(x, g, target)


def mm_nn(a, w, out_dtype, name, tm_target=704, tn_target=896):
    M, K = a.shape
    N = w.shape[1]
    tm, tn = _tile(M, tm_target), _tile(N, tn_target, 128)

    def body(a_ref, w_ref, o_ref):
        o_ref[...] = _dot(a_ref[...], w_ref[...]).astype(out_dtype)

    return pl.pallas_call(
        body, name=name, grid=(N // tn, M // tm),
        in_specs=[pl.BlockSpec((tm, K), lambda n, i: (i, 0)), pl.BlockSpec((K, tn), lambda n, i: (0, n))],
        out_specs=pl.BlockSpec((tm, tn), lambda n, i: (i, n)),
        out_shape=jax.ShapeDtypeStruct((M, N), out_dtype),
        compiler_params=_params("parallel", "parallel"),
    )(a, w)


def ffn_gateup(h, wg, wu, name):
    M, D = h.shape
    Fs = wg.shape[2]
    tm = _tile(M, 352)

    def body(h_ref, wg_ref, wu_ref, g_ref, u_ref):
        hv = h_ref[...]
        g_ref[...] = _dot(hv, wg_ref[...]).astype(BF16)
        u_ref[...] = _dot(hv, wu_ref[...]).astype(BF16)

    wspec = pl.BlockSpec((None, D, Fs), lambda j, i: (j, 0, 0))
    ospec = pl.BlockSpec((tm, Fs), lambda j, i: (i, j))
    return pl.pallas_call(
        body, name=name, grid=(N_CHIPS, M // tm),
        in_specs=[pl.BlockSpec((tm, D), lambda j, i: (i, 0)), wspec, wspec],
        out_specs=[ospec, ospec],
        out_shape=[jax.ShapeDtypeStruct((M, N_CHIPS * Fs), BF16)] * 2,
        compiler_params=_params("parallel", "parallel"),
    )(h, wg, wu)


def mm_residual(acts, w, x, scale, tk, name, tm_target=352):
    M, N = x.shape
    K = w.shape[0]
    tm = _tile(M, tm_target)
    swiglu = len(acts) == 2

    def body(*refs):
        a_refs, (w_ref, x_ref, o_ref, acc) = refs[:len(acts)], refs[len(acts):]
        k = pl.program_id(1)

        @pl.when(k == 0)
        def _():
            acc[...] = jnp.zeros_like(acc)

        if swiglu:
            for cols in _chunks(tk):
                gv = a_refs[0][:, cols].astype(F32)
                av = (gv * _sigmoid(gv) * a_refs[1][:, cols].astype(F32)).astype(BF16)
                acc[...] += _dot(av, w_ref[cols, :])
        else:
            acc[...] += _dot(a_refs[0][...], w_ref[...])

        @pl.when(k == pl.num_programs(1) - 1)
        def _():
            o_ref[...] = x_ref[...] + scale * acc[...]

    aspec = pl.BlockSpec((tm, tk), lambda i, k: (i, k))
    return pl.pallas_call(
        body, name=name, grid=(M // tm, K // tk),
        in_specs=[aspec] * len(acts) + [pl.BlockSpec((tk, N), lambda i, k: (k, 0)),
                                        pl.BlockSpec((tm, N), lambda i, k: (i, 0), pipeline_mode=ONE_BUFFER)],
        out_specs=pl.BlockSpec((tm, N), lambda i, k: (i, 0), pipeline_mode=ONE_BUFFER),
        out_shape=jax.ShapeDtypeStruct((M, N), F32),
        scratch_shapes=[pltpu.VMEM((tm, N), F32)],
        compiler_params=_params("parallel", "arbitrary"),
    )(*acts, w, x)


def ffn_bwd_act(dout, wd, gate, up, name, dep=None):
    M, D = dout.shape
    F = wd.shape[0]
    Fs = F // N_CHIPS
    tm = _tile(M, 352)

    def body(dy_ref, wd_ref, g_ref, u_ref, *rest):
        dg_ref, du_ref, a_ref = rest[-3:]
        dy = (0.5 * dy_ref[...]).astype(BF16)
        for cols in _chunks(Fs):
            da = _dot_nt(dy, wd_ref[cols, :])
            gv = g_ref[:, cols].astype(F32)
            uv = u_ref[:, cols].astype(F32)
            s = _sigmoid(gv)
            silu = gv * s
            a_ref[:, cols] = (silu * uv).astype(BF16)
            dg_ref[:, cols] = (da * uv * (s * (1.0 + gv * (1.0 - s)))).astype(BF16)
            du_ref[:, cols] = (da * silu).astype(BF16)

    fspec = pl.BlockSpec((tm, Fs), lambda j, i: (i, j))
    return pl.pallas_call(
        body, name=name, grid=(N_CHIPS, M // tm),
        in_specs=[pl.BlockSpec((tm, D), lambda j, i: (i, 0)), pl.BlockSpec((Fs, D), lambda j, i: (j, 0)),
                  fspec, fspec] + _dep_specs(dep),
        out_specs=[fspec, fspec, fspec],
        out_shape=[jax.ShapeDtypeStruct((M, F), BF16)] * 3,
        compiler_params=_params("parallel", "parallel"),
    )(dout, wd, gate, up, *_dep_args(dep))


def mm_tn(a, b, ta, tb, name, b_scale=1.0, stacked_out=False, out_dtype=BF16, dep=None):
    T, Ma = a.shape
    Nb = b.shape[1]

    def body(a_ref, b_ref, *rest):
        o_ref = rest[-1]
        bv = b_ref[...]
        if b_scale != 1.0:
            bv = b_scale * bv
        o_ref[...] = _dot_tn(a_ref[...], bv.astype(BF16)).astype(out_dtype)

    if stacked_out:
        out_spec = pl.BlockSpec((None, ta, tb), lambda jb, ja: (jb, ja, 0))
        out_shape = jax.ShapeDtypeStruct((Nb // tb, Ma, tb), out_dtype)
    else:
        out_spec = pl.BlockSpec((ta, tb), lambda jb, ja: (ja, jb))
        out_shape = jax.ShapeDtypeStruct((Ma, Nb), out_dtype)
    return pl.pallas_call(
        body, name=name, grid=(Nb // tb, Ma // ta),
        in_specs=[pl.BlockSpec((T, ta), lambda jb, ja: (0, ja)), pl.BlockSpec((T, tb), lambda jb, ja: (0, jb))]
        + _dep_specs(dep),
        out_specs=out_spec, out_shape=out_shape,
        compiler_params=_params("parallel", "parallel"),
    )(a, b, *_dep_args(dep))


def mm_nt(pairs, tm, tn, tk, name, a_scale=1.0, stacked_w=False, dep=None):
    M, K = pairs[0][0].shape
    N = pairs[0][1].shape[1] if stacked_w else pairs[0][1].shape[0]
    n_pairs = len(pairs)

    def body(*refs):
        o_ref, acc = refs[-2:]
        k = pl.program_id(2)

        @pl.when(k == 0)
        def _():
            acc[...] = jnp.zeros_like(acc)

        for p in range(n_pairs):
            av = refs[2 * p][...]
            if a_scale != 1.0:
                av = a_scale * av
            acc[...] += _dot_nt(av.astype(BF16), refs[2 * p + 1][...])

        @pl.when(k == pl.num_programs(2) - 1)
        def _():
            o_ref[...] = acc[...]

    aspec = pl.BlockSpec((tm, tk), lambda i, n, k: (i, k))
    if stacked_w:
        wspec = pl.BlockSpec((None, tn, tk), lambda i, n, k: (k, n, 0))
    else:
        wspec = pl.BlockSpec((tn, tk), lambda i, n, k: (n, k))
    return pl.pallas_call(
        body, name=name, grid=(M // tm, N // tn, K // tk),
        in_specs=[aspec, wspec] * n_pairs + _dep_specs(dep),
        out_specs=pl.BlockSpec((tm, tn), lambda i, n, k: (i, n), pipeline_mode=ONE_BUFFER),
        out_shape=jax.ShapeDtypeStruct((M, N), F32),
        scratch_shapes=[pltpu.VMEM((tm, tn), F32)],
        compiler_params=_params("parallel", "parallel", "arbitrary"),
    )(*[t for pair in pairs for t in pair], *_dep_args(dep))


def _tri(lower):
    r = lax.broadcasted_iota(jnp.int32, (CHUNK, CHUNK), 0)
    c = lax.broadcasted_iota(jnp.int32, (CHUNK, CHUNK), 1)
    return (r >= c) if lower else (r <= c)


def _tri_sum(mask, x, pieces):
    ones = mask.astype(BF16)
    acc = jnp.zeros_like(x)
    rest = x
    for _ in range(pieces):
        piece = rest.astype(BF16)
        acc = acc + _dot(ones, piece)
        rest = rest - piece.astype(F32)
    return acc


def _gla_gates(lr_ref, wlr_ref, blr_ref, chunk):
    z = _dot(lr_ref[...].astype(BF16), wlr_ref[...]) + blr_ref[...]
    live = (_rows(CHUNK) + chunk * CHUNK) >= PAD
    lg = jnp.where(live, (jnp.minimum(z, 0.0) - jnp.log(1.0 + jnp.exp(-jnp.abs(z)))) * (1.0 / GATE_NORM), 0.0)
    b = _tri_sum(_tri(True), lg, 3)
    b_last = jnp.sum(lg, axis=0, keepdims=True)
    b_mid = jnp.sum(jnp.where(_rows(CHUNK) < CHUNK // 2, lg, 0.0), axis=0, keepdims=True)
    return z, live, b, b_last, b_mid


def _gla_specs(dkh, dvh, D, chunk_of):
    lr_blk = (3 * D) // LR_W
    return [
        pl.BlockSpec((CHUNK, dkh), lambda c, h: (chunk_of(c), h)),
        pl.BlockSpec((CHUNK, dkh), lambda c, h: (chunk_of(c), HEADS + h)),
        pl.BlockSpec((CHUNK, dvh), lambda c, h: (chunk_of(c), HEADS + h)),
        pl.BlockSpec((CHUNK, LR_W), lambda c, h: (chunk_of(c), lr_blk)),
        pl.BlockSpec((LR_W, dkh), lambda c, h: (0, h)),
        pl.BlockSpec((1, dkh), lambda c, h: (0, h)),
    ]


def gla_fwd(proj, wlr, blr, D):
    M = proj.shape[0]
    n = M // CHUNK
    dkh, dvh = D // 2 // HEADS, D // HEADS
    qscale = float(dkh) ** -0.5

    def body(q_ref, k_ref, v_ref, lr_ref, wlr_ref, blr_ref, o_ref, st_ref, S):
        c, h = pl.program_id(0), pl.program_id(1)

        @pl.when(c == 0)
        def _():
            S[h] = jnp.zeros((dvh, dkh), F32)

        _, _, b, b_last, b_mid = _gla_gates(lr_ref, wlr_ref, blr_ref, c)
        q = q_ref[...] * qscale
        k = k_ref[...]
        v = v_ref[...].astype(BF16)
        s0 = S[h]
        st_ref[...] = s0
        qb = (q * jnp.exp(b)).astype(BF16)
        kb = (k * jnp.exp(b_last - b)).astype(BF16)
        qt = (q * jnp.exp(b - b_mid)).astype(BF16)
        kt = (k * jnp.exp(b_mid - b)).astype(BF16)
        a = jnp.where(_tri(True), _dot_nt(qt, kt), 0.0).astype(BF16)
        o_ref[...] = _dot_nt(qb, s0.astype(BF16)) + _dot(a, v)
        S[h] = jnp.exp(b_last) * s0 + _dot_tn(v, kb)

    return pl.pallas_call(
        body, name="gla_fwd", grid=(n, HEADS),
        in_specs=_gla_specs(dkh, dvh, D, lambda c: c),
        out_specs=[pl.BlockSpec((CHUNK, dvh), lambda c, h: (c, h)),
                   pl.BlockSpec((None, None, dvh, dkh), lambda c, h: (c, h, 0, 0))],
        out_shape=[jax.ShapeDtypeStruct((M, D), F32), jax.ShapeDtypeStruct((n, HEADS, dvh, dkh), F32)],
        scratch_shapes=[pltpu.VMEM((HEADS, dvh, dkh), F32)],
        compiler_params=_params("arbitrary", "arbitrary"),
    )(proj, proj, proj, proj, wlr, blr)


def gla_bwd(proj, wlr, blr, st, do, D):
    M = proj.shape[0]
    n = M // CHUNK
    dkh, dvh = D // 2 // HEADS, D // HEADS
    qscale = float(dkh) ** -0.5
    rev = lambda c: n - 1 - c

    def body(q_ref, k_ref, v_ref, lr_ref, wlr_ref, blr_ref, st_ref, do_ref,
             dq_ref, dk_ref, dv_ref, dlr_ref, dwlr_ref, dblr_ref, dS, acc_w, acc_b):
        step, h = pl.program_id(0), pl.program_id(1)
        c = n - 1 - step

        @pl.when(step == 0)
        def _():
            dS[h] = jnp.zeros((dvh, dkh), F32)
            acc_w[h] = jnp.zeros((LR_W, dkh), F32)
            acc_b[h] = jnp.zeros((1, dkh), F32)

        z, live, b, b_last, b_mid = _gla_gates(lr_ref, wlr_ref, blr_ref, c)
        q = q_ref[...] * qscale
        k = k_ref[...]
        v = v_ref[...].astype(BF16)
        dov = do_ref[...].astype(BF16)
        s0 = st_ref[...]
        ds1 = dS[h]
        ds1b = ds1.astype(BF16)
        e_b, e_lb = jnp.exp(b), jnp.exp(b_last - b)
        e_bm, e_mb = jnp.exp(b - b_mid), jnp.exp(b_mid - b)
        e_last = jnp.exp(b_last)
        qb, kb, qt, kt = q * e_b, k * e_lb, q * e_bm, k * e_mb
        qbb, kbb, qtb, ktb = qb.astype(BF16), kb.astype(BF16), qt.astype(BF16), kt.astype(BF16)
        lower = _tri(True)
        a = jnp.where(lower, _dot_nt(qtb, ktb), 0.0).astype(BF16)
        da = jnp.where(lower, _dot_nt(dov, v), 0.0).astype(BF16)

        dqb = _dot(dov, s0.astype(BF16))
        dqt = _dot(da, ktb)
        dkt = _dot_tn(da, qtb)
        dkb = _dot(v, ds1b)
        keep = live.astype(F32)
        dv_ref[...] = (keep * (_dot_tn(a, dov) + _dot_nt(kbb, ds1b))).astype(BF16)
        dq_ref[...] = (keep * qscale * (dqb * e_b + dqt * e_bm)).astype(BF16)
        dk_ref[...] = (keep * (dkb * e_lb + dkt * e_mb)).astype(BF16)

        db = dqb * qb - dkb * kb + dqt * qt - dkt * kt
        db_last = (jnp.sum(dkb * kb, axis=0, keepdims=True)
                   + jnp.sum(ds1 * s0, axis=0, keepdims=True) * e_last)
        db = db + jnp.where(_rows(CHUNK) == CHUNK - 1, db_last, 0.0)
        dlg = jnp.where(live, _tri_sum(_tri(False), db, 2), 0.0)
        dz = dlg * (1.0 / GATE_NORM) / (1.0 + jnp.exp(z))
        dzb = dz.astype(BF16)

        dlr_h = _dot_nt(dzb, wlr_ref[...])

        @pl.when(h == 0)
        def _():
            dlr_ref[...] = dlr_h

        @pl.when(h > 0)
        def _():
            dlr_ref[...] += dlr_h

        acc_w[h] += _dot_tn(lr_ref[...].astype(BF16), dzb)
        acc_b[h] += jnp.sum(dz, axis=0, keepdims=True)
        dS[h] = e_last * ds1 + _dot_tn(dov, qbb)

        @pl.when(step == n - 1)
        def _():
            dwlr_ref[h] = acc_w[h]
            dblr_ref[h] = acc_b[h]

    return pl.pallas_call(
        body, name="gla_bwd", grid=(n, HEADS),
        in_specs=_gla_specs(dkh, dvh, D, rev) + [
            pl.BlockSpec((None, None, dvh, dkh), lambda c, h: (rev(c), h, 0, 0)),
            pl.BlockSpec((CHUNK, dvh), lambda c, h: (rev(c), h))],
        out_specs=[pl.BlockSpec((CHUNK, dkh), lambda c, h: (rev(c), h)),
                   pl.BlockSpec((CHUNK, dkh), lambda c, h: (rev(c), h)),
                   pl.BlockSpec((CHUNK, dvh), lambda c, h: (rev(c), h)),
                   pl.BlockSpec((CHUNK, LR_W), lambda c, h: (rev(c), 0)),
                   pl.BlockSpec((HEADS, LR_W, dkh), lambda c, h: (0, 0, 0)),
                   pl.BlockSpec((HEADS, 1, dkh), lambda c, h: (0, 0, 0))],
        out_shape=[jax.ShapeDtypeStruct((M, D // 2), BF16), jax.ShapeDtypeStruct((M, D // 2), BF16),
                   jax.ShapeDtypeStruct((M, D), BF16), jax.ShapeDtypeStruct((M, LR_W), F32),
                   jax.ShapeDtypeStruct((HEADS, LR_W, dkh), F32), jax.ShapeDtypeStruct((HEADS, 1, dkh), F32)],
        scratch_shapes=[pltpu.VMEM((HEADS, dvh, dkh), F32), pltpu.VMEM((HEADS, LR_W, dkh), F32),
                        pltpu.VMEM((HEADS, 1, dkh), F32)],
        compiler_params=_params("arbitrary", "arbitrary"),
    )(proj, proj, proj, proj, wlr, blr, st, do)


def gla_post_fwd(o, proj, head_norm, D):
    M = o.shape[0]
    dvh = D // HEADS
    tr = _tile(M, ROW_TILE)

    def body(o_ref, r_ref, hn_ref, out_ref):
        for hd in range(HEADS):
            cols = slice(hd * dvh, (hd + 1) * dvh)
            ov = o_ref[:, cols]
            rs = lax.rsqrt(jnp.mean(ov * ov, axis=-1, keepdims=True) + EPS)
            rv = r_ref[:, cols]
            out_ref[:, cols] = (ov * rs * hn_ref[...] * (rv * _sigmoid(rv))).astype(BF16)

    row = pl.BlockSpec((tr, D), lambda i: (i, 0))
    return pl.pallas_call(
        body, name="gla_post_fwd", grid=(M // tr,),
        in_specs=[row, pl.BlockSpec((tr, D), lambda i: (i, 2)), pl.BlockSpec((1, dvh), lambda i: (0, 0))],
        out_specs=row, out_shape=jax.ShapeDtypeStruct((M, D), BF16),
        compiler_params=_params("parallel"),
    )(o, proj, head_norm)


def gla_post_bwd(dgated, o, proj, head_norm, D):
    M = o.shape[0]
    dvh = D // HEADS
    tr = _tile(M, ROW_TILE)

    def body(dg_ref, o_ref, r_ref, hn_ref, do_ref, dr_ref, dhn_ref):
        @pl.when(pl.program_id(0) == 0)
        def _():
            dhn_ref[...] = jnp.zeros_like(dhn_ref)

        hn = hn_ref[...]
        dhn = jnp.zeros((1, dvh), F32)
        for hd in range(HEADS):
            cols = slice(hd * dvh, (hd + 1) * dvh)
            ov = o_ref[:, cols]
            rs = lax.rsqrt(jnp.mean(ov * ov, axis=-1, keepdims=True) + EPS)
            ohat = ov * rs
            rv = r_ref[:, cols]
            s = _sigmoid(rv)
            dgv = dg_ref[:, cols]
            don = dgv * (rv * s)
            dr_ref[:, cols] = (dgv * ohat * hn * (s * (1.0 + rv * (1.0 - s)))).astype(BF16)
            gd = don * hn
            do_ref[:, cols] = rs * (gd - ohat * jnp.mean(gd * ohat, axis=-1, keepdims=True))
            dhn = dhn + jnp.sum(don * ohat, axis=0, keepdims=True)
        dhn_ref[...] += dhn

    row = pl.BlockSpec((tr, D), lambda i: (i, 0))
    vec = pl.BlockSpec((1, dvh), lambda i: (0, 0))
    return pl.pallas_call(
        body, name="gla_post_bwd", grid=(M // tr,),
        in_specs=[row, row, pl.BlockSpec((tr, D), lambda i: (i, 2)), vec],
        out_specs=[row, row, vec],
        out_shape=[jax.ShapeDtypeStruct((M, D), F32), jax.ShapeDtypeStruct((M, D), BF16),
                   jax.ShapeDtypeStruct((1, dvh), F32)],
        compiler_params=_params("arbitrary"),
    )(dgated, o, proj, head_norm)


def _pool_counts(M, g):
    t = _rows(M) - PAD
    win = jnp.left_shift(2, g)
    return t >= 0, jnp.maximum(jnp.minimum(t + 1, win), 1).astype(F32)


def _window_sum(x, g, M, back):
    sums = []
    s = x
    for lvl in range(4):
        sh = 1 << lvl
        s = s + pltpu.roll(s, (M - sh) if back else sh, 0)
        sums.append(s)
    return jnp.where(g == 0, sums[0], jnp.where(g == 1, sums[1], jnp.where(g == 2, sums[2], sums[3])))


POOL_COLS = 128


def pool_window(hp):
    M, D = hp.shape
    cw = min(POOL_COLS, D // 4)
    per_group = (D // 4) // cw

    def body(h_ref, p_ref):
        g = pl.program_id(0) // per_group
        live, cnt = _pool_counts(M, g)
        hv = h_ref[...]
        p_ref[...] = jnp.where(live, _window_sum(hv, g, M, False) / cnt - hv, 0.0).astype(BF16)

    col = pl.BlockSpec((M, cw), lambda j: (0, j))
    return pl.pallas_call(
        body, name="pool_window", grid=(D // cw,), in_specs=[col], out_specs=col,
        out_shape=jax.ShapeDtypeStruct((M, D), BF16), compiler_params=_params("parallel"),
    )(hp)


def pool_window_bwd(dpooled):
    M, D = dpooled.shape
    cw = min(POOL_COLS, D // 4)
    per_group = (D // 4) // cw

    def body(d_ref, o_ref):
        g = pl.program_id(0) // per_group
        live, cnt = _pool_counts(M, g)
        dv = jnp.where(live, d_ref[...], 0.0)
        o_ref[...] = jnp.where(live, _window_sum(dv / cnt, g, M, True) - dv, 0.0)

    col = pl.BlockSpec((M, cw), lambda j: (0, j))
    return pl.pallas_call(
        body, name="pool_window_bwd", grid=(D // cw,), in_specs=[col], out_specs=col,
        out_shape=jax.ShapeDtypeStruct((M, D), F32), compiler_params=_params("parallel"),
    )(dpooled)


def pool_mix(pooled, x, w, bias, scale):
    M, D = x.shape
    W = D // 4
    tm = _tile(M, 352)

    def body(p_ref, x_ref, w_ref, b_ref, s_ref, out_ref):
        live = (_rows(tm) + pl.program_id(1) * tm) >= PAD
        y = (_dot(p_ref[...], w_ref[...]) + b_ref[...]) * s_ref[...]
        out_ref[...] = x_ref[...] + jnp.where(live, y, 0.0)

    blk = pl.BlockSpec((tm, W), lambda g, i: (i, g))
    vec = pl.BlockSpec((1, W), lambda g, i: (0, g))
    return pl.pallas_call(
        body, name="pool_mix", grid=(4, M // tm),
        in_specs=[blk, blk, pl.BlockSpec((None, W, W), lambda g, i: (g, 0, 0)), vec, vec],
        out_specs=blk, out_shape=jax.ShapeDtypeStruct((M, D), F32),
        compiler_params=_params("parallel", "parallel"),
    )(pooled, x, w, bias, scale)


def pool_mix_bwd(dy, pooled, w, bias, scale, dep=None):
    M, D = dy.shape
    W = D // 4
    tm = _tile(M, 352)

    def body(dy_ref, p_ref, w_ref, b_ref, s_ref, *rest):
        dp_ref, dw_ref, db_ref, ds_ref, acc_w = rest[-5:]
        i = pl.program_id(1)

        @pl.when(i == 0)
        def _():
            acc_w[...] = jnp.zeros_like(acc_w)
            db_ref[...] = jnp.zeros_like(db_ref)
            ds_ref[...] = jnp.zeros_like(ds_ref)

        live = (_rows(tm) + i * tm) >= PAD
        dyv = jnp.where(live, dy_ref[...], 0.0)
        pooled = p_ref[...]
        wv = w_ref[...]
        ds_ref[...] += jnp.sum(dyv * (_dot(pooled, wv) + b_ref[...]), axis=0, keepdims=True)
        dys = dyv * s_ref[...]
        db_ref[...] += jnp.sum(dys, axis=0, keepdims=True)
        dysb = dys.astype(BF16)
        acc_w[...] += _dot_tn(pooled, dysb)
        dp_ref[...] = _dot_nt(dysb, wv)

        @pl.when(i == pl.num_programs(1) - 1)
        def _():
            dw_ref[...] = acc_w[...].astype(BF16)

    blk = pl.BlockSpec((tm, W), lambda g, i: (i, g))
    vec = pl.BlockSpec((1, W), lambda g, i: (0, g))
    wspec = pl.BlockSpec((None, W, W), lambda g, i: (g, 0, 0))
    return pl.pallas_call(
        body, name="pool_mix_bwd", grid=(4, M // tm),
        in_specs=[blk, blk, wspec, vec, vec] + _dep_specs(dep),
        out_specs=[blk, wspec, vec, vec],
        out_shape=[jax.ShapeDtypeStruct((M, D), F32), jax.ShapeDtypeStruct((4, W, W), BF16),
                   jax.ShapeDtypeStruct((1, D), F32), jax.ShapeDtypeStruct((1, D), F32)],
        scratch_shapes=[pltpu.VMEM((W, W), F32)],
        compiler_params=_params("parallel", "arbitrary"),
    )(dy, pooled, w, bias, scale, *_dep_args(dep))


def adamw(w, g, m, v, name):
    shape = w.shape
    C = shape[-1]
    R = w.size // C
    tr = _tile(R, 256, 8)

    def body(w_ref, g_ref, m_ref, v_ref, d_ref, nm_ref, nv_ref):
        gv = g_ref[...]
        nm = ADAM_B1 * m_ref[...] + (1.0 - ADAM_B1) * gv
        nv = ADAM_B2 * v_ref[...] + (1.0 - ADAM_B2) * (gv * gv)
        m_hat = nm / (1.0 - ADAM_B1 ** ADAM_STEP)
        v_hat = nv / (1.0 - ADAM_B2 ** ADAM_STEP)
        d_ref[...] = -ADAM_LR * (m_hat / (jnp.sqrt(v_hat) + ADAM_EPS) + ADAM_WD * w_ref[...])
        nm_ref[...] = nm
        nv_ref[...] = nv

    spec = pl.BlockSpec((tr, C), lambda i: (i, 0))
    outs = pl.pallas_call(
        body, name=name, grid=(R // tr,),
        in_specs=[spec] * 4, out_specs=[spec] * 3,
        out_shape=[jax.ShapeDtypeStruct((R, C), F32)] * 3,
        compiler_params=_params("parallel"),
    )(*[t.reshape(R, C) for t in (w, g, m, v)])
    return [t.reshape(shape) for t in outs]


def add_sibling(grad, recv, core, name):
    _, _, Rh, C = grad.shape
    tr = _tile(Rh, 512)

    def body(core_ref, g_ref, r_ref, o_ref):
        o_ref[...] = (g_ref[...].astype(F32) + r_ref[...].astype(F32)).astype(BF16)

    return pl.pallas_call(
        body, name=name,
        grid_spec=pltpu.PrefetchScalarGridSpec(
            num_scalar_prefetch=1, grid=(N_CHIPS, Rh // tr),
            in_specs=[pl.BlockSpec((None, None, tr, C), lambda j, i, core_ref: (j, core_ref[0], i, 0)),
                      pl.BlockSpec((None, tr, C), lambda j, i, core_ref: (j, i, 0))],
            out_specs=pl.BlockSpec((None, tr, C), lambda j, i, core_ref: (j, i, 0))),
        out_shape=jax.ShapeDtypeStruct((N_CHIPS, Rh, C), BF16),
        compiler_params=_params("parallel", "parallel"),
    )(core, grad, recv)


def add_chips(part, recv, chip, core, group, n, mi, name):
    _, Rh, C = part.shape
    tr = _tile(Rh, 512)

    def body(chip_ref, core_ref, p_ref, r_ref, *rest):
        o_ref = rest[-1]
        acc = p_ref[...].astype(F32)
        for k in range(N_CHIPS - 1):
            acc = acc + r_ref[k].astype(F32)
        o_ref[...] = acc

    carried = [] if group is None else [group]
    return pl.pallas_call(
        body, name=name,
        grid_spec=pltpu.PrefetchScalarGridSpec(
            num_scalar_prefetch=2, grid=(Rh // tr,),
            in_specs=[pl.BlockSpec((None, tr, C), lambda i, chip_ref, core_ref: (chip_ref[0], i, 0)),
                      pl.BlockSpec((N_CHIPS - 1, tr, C), lambda i, chip_ref, core_ref: (0, i, 0))]
            + [ANY] * len(carried),
            out_specs=pl.BlockSpec((None, None, tr, C), lambda i, chip_ref, core_ref: (mi, core_ref[0], i, 0))),
        out_shape=jax.ShapeDtypeStruct((n, 2, Rh, C), F32),
        input_output_aliases={4: 0} if carried else {},
        compiler_params=_params("parallel"),
    )(chip, core, part, recv, *carried)


def stage_shard(shard, mi, chip, name, dep=None):
    _, _, Rh, C = shard.shape
    tr = _tile(Rh, 512)

    def body(chip_ref, s_ref, *rest):
        rest[-1][...] = s_ref[...].astype(BF16)

    return pl.pallas_call(
        body, name=name,
        grid_spec=pltpu.PrefetchScalarGridSpec(
            num_scalar_prefetch=1, grid=(2, Rh // tr),
            in_specs=[pl.BlockSpec((None, None, tr, C), lambda h, i, chip_ref: (mi, h, i, 0))] + _dep_specs(dep),
            out_specs=pl.BlockSpec((None, None, tr, C), lambda h, i, chip_ref: (chip_ref[0], h, i, 0))),
        out_shape=jax.ShapeDtypeStruct((N_CHIPS, 2, Rh, C), BF16),
        compiler_params=_params("parallel", "parallel"),
    )(chip, shard, *_dep_args(dep))


def sum_devices(gathered):
    _, R, C = gathered.shape

    def body(g_ref, o_ref):
        acc = g_ref[0]
        for d in range(1, N_DEV):
            acc = acc + g_ref[d]
        o_ref[...] = acc

    return pl.pallas_call(
        body, name="sum_devices", grid=(1,),
        in_specs=[pl.BlockSpec((N_DEV, R, C), lambda i: (0, 0, 0))],
        out_specs=pl.BlockSpec((R, C), lambda i: (0, 0)),
        out_shape=jax.ShapeDtypeStruct((R, C), F32),
        compiler_params=_params("arbitrary"),
    )(gathered)


def _place():
    x, y, c = lax.axis_index("x"), lax.axis_index("y"), lax.axis_index("c")
    others = [(1 - x, y), (x, 1 - y), (1 - x, 1 - y)]
    return x, y, c, others


def _remote(src, dst, send_sems, recv_sems, idx, device):
    return pltpu.make_async_remote_copy(src_ref=src, dst_ref=dst, send_sem=send_sems.at[idx],
                                        recv_sem=recv_sems.at[idx], device_id=device, device_id_type=MESH)


HBM = pl.BlockSpec(memory_space=pltpu.HBM)
SEM = pl.BlockSpec(memory_space=pltpu.SEMAPHORE)
EFFECT = pltpu.SideEffectType.DATAFLOW_SIDE_EFFECTING


def _in_hbm(t):
    return pltpu.with_memory_space_constraint(t, pltpu.HBM)


def _own_slice(buf, me, c):
    return buf.at[me, c] if len(buf.shape) == 4 else buf.at[me]


def gather_start(staged, bucket_sizes, name):
    n, nb = len(staged), len(bucket_sizes)

    def body(*refs):
        in_refs, sems, token = refs[:n], refs[n:n + 2 * nb], refs[-1]
        x, y, c, others = _place()
        me = 2 * x + y
        t = 0
        for b, size in enumerate(bucket_sizes):
            for i in range(size):
                mine = _own_slice(in_refs[t], me, c)
                for k, chip in enumerate(others):
                    _remote(mine, mine, sems[2 * b], sems[2 * b + 1], 3 * i + k, (*chip, c)).start()
                t += 1
        token[...] = jnp.zeros_like(token)

    sem_shapes = [pltpu.SemaphoreType.DMA((3 * size,)) for size in bucket_sizes for _ in range(2)]
    outs = pl.pallas_call(
        body, name=name,
        out_shape=sem_shapes + [pltpu.HBM(s.shape, s.dtype) for s in staged] + [jax.ShapeDtypeStruct((8, 128), F32)],
        in_specs=[HBM] * n, out_specs=[SEM] * (2 * nb) + [HBM] * n + [pl.BlockSpec(memory_space=pltpu.VMEM)],
        input_output_aliases={t: 2 * nb + t for t in range(n)},
        compiler_params=pltpu.CompilerParams(has_side_effects=EFFECT),
    )(*[_in_hbm(s) for s in staged])
    sems = [(outs[2 * b], outs[2 * b + 1]) for b in range(nb)]
    return sems, list(outs[2 * nb:2 * nb + n]), outs[-1]


def gather_wait(bufs, sems, after, name):
    n = len(bufs)

    def body(*refs):
        in_refs, send_sems, recv_sems = refs[:n], refs[n], refs[n + 1]
        x, y, c, others = _place()
        me = 2 * x + y
        for i in range(n):
            mine = _own_slice(in_refs[i], me, c)
            for k, (ox, oy) in enumerate(others):
                cp = _remote(mine, _own_slice(in_refs[i], 2 * ox + oy, c), send_sems, recv_sems, 3 * i + k,
                             (ox, oy, c))
                cp.wait_send()
                cp.wait_recv()

    return pl.pallas_call(
        body, name=name, out_shape=[pltpu.HBM(b.shape, b.dtype) for b in bufs],
        in_specs=[HBM] * n + [SEM, SEM, ANY], out_specs=[HBM] * n,
        input_output_aliases={t: t for t in range(n)},
        compiler_params=pltpu.CompilerParams(has_side_effects=EFFECT),
    )(*bufs, *sems, after)


def forward_to_sibling(bufs, name):
    n = len(bufs)

    def body(*refs):
        out_refs, (send_sems, recv_sems) = refs[n:2 * n], refs[2 * n:]
        x, y, c, others = _place()
        sibling = (x, y, 1 - c)
        copies = []
        for t in range(n):
            for k, (ox, oy) in enumerate(others):
                mine = out_refs[t].at[2 * ox + oy, c]
                cp = _remote(mine, mine, send_sems, recv_sems, 3 * t + k, sibling)
                cp.start()
                copies.append(cp)
        for t in range(n):
            for k, (ox, oy) in enumerate(others):
                theirs = out_refs[t].at[2 * ox + oy, 1 - c]
                _remote(theirs, theirs, send_sems, recv_sems, 3 * t + k, sibling).wait_recv()
        for cp in copies:
            cp.wait_send()

    return pl.pallas_call(
        body, name=name, in_specs=[ANY] * n, out_specs=[ANY] * n,
        out_shape=[jax.ShapeDtypeStruct(b.shape, b.dtype) for b in bufs],
        input_output_aliases={t: t for t in range(n)},
        scratch_shapes=[pltpu.SemaphoreType.DMA((3 * n,)), pltpu.SemaphoreType.DMA((3 * n,))],
    )(*bufs)


def sibling_start(grads, name):
    n = len(grads)
    lands = [lax.empty((N_CHIPS,) + g.shape[2:], g.dtype) for g in grads]

    def body(*refs):
        in_refs, land_refs, send_sems, recv_sems, token = refs[:n], refs[n:2 * n], refs[2 * n], refs[2 * n + 1], refs[-1]
        x, y, c, _ = _place()
        for t in range(n):
            for j in range(N_CHIPS):
                _remote(in_refs[t].at[j, 1 - c], land_refs[t].at[j], send_sems, recv_sems, N_CHIPS * t + j,
                        (x, y, 1 - c)).start()
        token[...] = jnp.zeros_like(token)

    outs = pl.pallas_call(
        body, name=name,
        out_shape=[pltpu.SemaphoreType.DMA((N_CHIPS * n,))] * 2 + [pltpu.HBM(t.shape, t.dtype) for t in grads + lands]
        + [jax.ShapeDtypeStruct((8, 128), F32)],
        in_specs=[HBM] * (2 * n), out_specs=[SEM, SEM] + [HBM] * (2 * n) + [pl.BlockSpec(memory_space=pltpu.VMEM)],
        input_output_aliases={t: 2 + t for t in range(2 * n)},
        compiler_params=pltpu.CompilerParams(has_side_effects=EFFECT),
    )(*[_in_hbm(t) for t in grads + lands])
    return (outs[0], outs[1]), list(outs[2:2 + n]), list(outs[2 + n:2 + 2 * n]), outs[-1]


def sibling_wait(grads, lands, sems, after, name):
    n = len(grads)

    def body(*refs):
        in_refs, land_refs, send_sems, recv_sems = refs[:n], refs[n:2 * n], refs[2 * n], refs[2 * n + 1]
        x, y, c, _ = _place()
        for t in range(n):
            for j in range(N_CHIPS):
                cp = _remote(in_refs[t].at[j, 1 - c], land_refs[t].at[j], send_sems, recv_sems, N_CHIPS * t + j,
                             (x, y, 1 - c))
                cp.wait_send()
                cp.wait_recv()

    outs = pl.pallas_call(
        body, name=name, out_shape=[pltpu.HBM(t.shape, t.dtype) for t in grads + lands],
        in_specs=[HBM] * (2 * n) + [SEM, SEM, ANY], out_specs=[HBM] * (2 * n),
        input_output_aliases={t: t for t in range(2 * n)},
        compiler_params=pltpu.CompilerParams(has_side_effects=EFFECT),
    )(*grads, *lands, *sems, after)
    return list(outs[:n]), list(outs[n:])


def reduce_start(parts, name):
    n = len(parts)
    lands = [lax.empty((N_CHIPS - 1,) + p.shape[1:], p.dtype) for p in parts]

    def body(*refs):
        in_refs, land_refs, send_sems, recv_sems, token = refs[:n], refs[n:2 * n], refs[2 * n], refs[2 * n + 1], refs[-1]
        x, y, c, others = _place()
        for t in range(n):
            for k, (ox, oy) in enumerate(others):
                _remote(in_refs[t].at[2 * ox + oy], land_refs[t].at[k], send_sems, recv_sems, 3 * t + k,
                        (ox, oy, c)).start()
        token[...] = jnp.zeros_like(token)

    outs = pl.pallas_call(
        body, name=name,
        out_shape=[pltpu.SemaphoreType.DMA((3 * n,))] * 2 + [pltpu.HBM(t.shape, t.dtype) for t in parts + lands]
        + [jax.ShapeDtypeStruct((8, 128), F32)],
        in_specs=[HBM] * (2 * n), out_specs=[SEM, SEM] + [HBM] * (2 * n) + [pl.BlockSpec(memory_space=pltpu.VMEM)],
        input_output_aliases={t: 2 + t for t in range(2 * n)},
        compiler_params=pltpu.CompilerParams(has_side_effects=EFFECT),
    )(*[_in_hbm(t) for t in parts + lands])
    return (outs[0], outs[1]), list(outs[2:2 + n]), list(outs[2 + n:2 + 2 * n]), outs[-1]


def reduce_wait(parts, lands, sems, after, name):
    n = len(parts)

    def body(*refs):
        in_refs, land_refs, send_sems, recv_sems = refs[:n], refs[n:2 * n], refs[2 * n], refs[2 * n + 1]
        x, y, c, others = _place()
        for t in range(n):
            for k, (ox, oy) in enumerate(others):
                cp = _remote(in_refs[t].at[2 * ox + oy], land_refs[t].at[k], send_sems, recv_sems, 3 * t + k,
                             (ox, oy, c))
                cp.wait_send()
                cp.wait_recv()

    outs = pl.pallas_call(
        body, name=name, out_shape=[pltpu.HBM(t.shape, t.dtype) for t in parts + lands],
        in_specs=[HBM] * (2 * n) + [SEM, SEM, ANY], out_specs=[HBM] * (2 * n),
        input_output_aliases={t: t for t in range(2 * n)},
        compiler_params=pltpu.CompilerParams(has_side_effects=EFFECT),
    )(*parts, *lands, *sems, after)
    return list(outs[:n]), list(outs[n:])


def exchange_halves(groups, name):
    n_groups = len(groups)
    slots = [(gi, mi) for gi, grp in enumerate(groups) for mi in range(grp.shape[0])]

    def body(*refs):
        out_refs = refs[n_groups:2 * n_groups]
        send_sems, recv_sems = refs[2 * n_groups:]
        x, y, c, _ = _place()
        sibling = (x, y, 1 - c)
        copies = []
        for t, (gi, mi) in enumerate(slots):
            mine = out_refs[gi].at[mi, c]
            cp = _remote(mine, mine, send_sems, recv_sems, t, sibling)
            cp.start()
            copies.append(cp)
        for t, (gi, mi) in enumerate(slots):
            theirs = out_refs[gi].at[mi, 1 - c]
            _remote(theirs, theirs, send_sems, recv_sems, t, sibling).wait_recv()
        for cp in copies:
            cp.wait_send()

    return pl.pallas_call(
        body, name=name, in_specs=[ANY] * n_groups, out_specs=[ANY] * n_groups,
        out_shape=[jax.ShapeDtypeStruct(g.shape, g.dtype) for g in groups],
        input_output_aliases={gi: gi for gi in range(n_groups)},
        scratch_shapes=[pltpu.SemaphoreType.DMA((len(slots),)), pltpu.SemaphoreType.DMA((len(slots),))],
    )(*groups)


def gather_devices(buf):
    def body(in_ref, out_ref, send_sems, recv_sems, local_sem):
        x, y, c, _ = _place()
        me = 4 * x + 2 * y + c
        local = pltpu.make_async_copy(in_ref, out_ref.at[me], local_sem)
        local.start()
        copies = []
        for k in range(1, N_DEV):
            fx, fy, fc = (k >> 2) & 1, (k >> 1) & 1, k & 1
            peer = (x ^ fx, y ^ fy, c ^ fc)
            cp = _remote(in_ref, out_ref.at[me], send_sems, recv_sems, k - 1, peer)
            cp.start()
            copies.append(cp)
        for k in range(1, N_DEV):
            fx, fy, fc = (k >> 2) & 1, (k >> 1) & 1, k & 1
            theirs = out_ref.at[4 * (x ^ fx) + 2 * (y ^ fy) + (c ^ fc)]
            _remote(theirs, theirs, send_sems, recv_sems, k - 1, (x, y, c)).wait_recv()
        for cp in copies:
            cp.wait_send()
        local.wait()

    return pl.pallas_call(
        body, name="gather_devices", in_specs=[ANY], out_specs=ANY,
        out_shape=jax.ShapeDtypeStruct((N_DEV,) + buf.shape, buf.dtype),
        scratch_shapes=[pltpu.SemaphoreType.DMA((N_DEV - 1,)), pltpu.SemaphoreType.DMA((N_DEV - 1,)),
                        pltpu.SemaphoreType.DMA],
    )(buf)


class GradReducer:
    def __init__(self, core, chip, kinds):
        self.core, self.chip = core, chip
        self.sizes = dict(kinds)
        self.groups = {kind: None for kind, _ in kinds}

    def send(self, grad, kind, mi, tag):
        array = grad.reshape(N_CHIPS, 2, -1, grad.shape[-1])
        sems, arrays, lands, token = sibling_start([array], f"reduce_sibling_start_{tag}")
        return (sems, arrays, lands, kind, mi, tag), token

    def begin(self, sent, after, tag):
        parts, slots = [], []
        for sems, arrays, lands, kind, mi, sent_tag in sent:
            arrays, lands = sibling_wait(arrays, lands, sems, after, f"reduce_sibling_wait_{sent_tag}")
            parts.append(add_sibling(arrays[0], lands[0], self.core, f"reduce_add_sibling_{sent_tag}"))
            slots.append((kind, mi))
        sems, parts, lands, token = reduce_start(parts, f"reduce_start_{tag}")
        return (sems, parts, lands, slots, tag), token

    def end(self, state, after):
        sems, parts, lands, slots, tag = state
        parts, lands = reduce_wait(parts, lands, sems, after, f"reduce_wait_{tag}")
        for t, (kind, mi) in enumerate(slots):
            self.groups[kind] = add_chips(parts[t], lands[t], self.chip, self.core, self.groups[kind],
                                          self.sizes[kind], mi, f"reduce_add_chips_{tag}_{t}")

    def finish(self):
        kinds = list(self.groups)
        return dict(zip(kinds, exchange_halves([self.groups[k] for k in kinds], "reduce_swap")))


def _ffn_fwd(x, gain, wg, wu, wd, tag):
    h, rstd = rmsnorm_fwd(x, gain, BF16, f"ffn_norm_{tag}")
    gate, up = ffn_gateup(h, wg, wu, f"ffn_gateup_{tag}")
    out = mm_residual([gate, up], wd, x, 0.5, wd.shape[0] // N_CHIPS, f"ffn_down_{tag}", tm_target=704)
    return out, (x, gain, h, rstd, gate, up)


def _ffn_bwd(dout, saved, wg, wu, wd, index, reducer, dep=None, per_tensor=False):
    x, gain, h, rstd, gate, up = saved
    D = x.shape[1]
    Fs = wg.shape[2]
    td = _tile(D, 512, 128)
    tag = f"ffn{index}"
    begun = []

    def begin(sent, after, suffix):
        state, token = reducer.begin(sent, after, tag + suffix)
        begun.append(state)
        return token

    dgate, dup, act = ffn_bwd_act(dout, wd, gate, up, f"ffn_bwd_act_{index}", dep=dep)
    d_wd = mm_tn(act, dout, Fs, td, f"ffn_bwd_wd_{index}", b_scale=0.5)
    sent_d, tok = reducer.send(d_wd, "down", index, tag + "d")
    d_wg = mm_tn(h, dgate, td, Fs, f"ffn_bwd_wg_{index}", stacked_out=True, dep=tok)
    toks = [begin([sent_d], d_wg, "d")] if per_tensor else []
    sent_g, tok = reducer.send(d_wg, "gate", index, tag + "g")
    d_wu = mm_tn(h, dup, td, Fs, f"ffn_bwd_wu_{index}", stacked_out=True, dep=toks + [tok])
    toks = [begin([sent_g], d_wu, "g")] if per_tensor else []
    sent_u, tok = reducer.send(d_wu, "up", index, tag + "u")
    dh = mm_nt([(dgate, wg), (dup, wu)], _tile(x.shape[0], 704), D, Fs, f"ffn_bwd_dh_{index}", stacked_w=True,
               dep=toks + [tok])
    tok = begin([sent_u] if per_tensor else [sent_d, sent_g, sent_u], dh, "u")
    dx, dgain = rmsnorm_bwd(dh, x, gain, rstd, dout, f"ffn_norm_bwd_{index}")
    return dx, dgain, begun, tok


def kernel(x, meta, ffn_norm, ffn_w_gate, ffn_w_up, ffn_w_down, gla_norm, gla_w_in, gla_w_lr, gla_b_lr, gla_head_norm, gla_w_out, pool_norm, pool_w, pool_b, pool_scale, final_norm, loss_target, m_meta, m_ffn_norm, m_ffn_w_gate, m_ffn_w_up, m_ffn_w_down, m_gla_norm, m_gla_w_in, m_gla_w_lr, m_gla_b_lr, m_gla_head_norm, m_gla_w_out, m_pool_norm, m_pool_w, m_pool_b, m_pool_scale, m_final_norm, v_meta, v_ffn_norm, v_ffn_w_gate, v_ffn_w_up, v_ffn_w_down, v_gla_norm, v_gla_w_in, v_gla_w_lr, v_gla_b_lr, v_gla_head_norm, v_gla_w_out, v_pool_norm, v_pool_w, v_pool_b, v_pool_scale, v_final_norm):
    S, D = x.shape[1], x.shape[2]
    M = OFF + S
    Dq = D // N_CHIPS
    Fs = ffn_w_gate.shape[3]
    F = N_CHIPS * Fs
    dk = D // 2
    n_in = gla_w_in.shape[2]
    W = D // 4
    core = lax.axis_index("c").astype(jnp.int32).reshape(1)
    chip_id = 2 * lax.axis_index("x") + lax.axis_index("y")
    chip = chip_id.astype(jnp.int32).reshape(1)

    small = jnp.concatenate([_pad_rows(t) for t in (
        meta, ffn_norm.reshape(4, Dq), gla_w_lr.reshape(8, Dq), pool_norm, pool_b.reshape(1, Dq), pool_scale)],
        axis=0)
    def stage(w, kind, n, mi, dep=None):
        return stage_shard(w.reshape(n, 2, -1, w.shape[-1]), mi, chip, f"stage_{kind}_{mi}", dep=dep)

    ffn_stage = lambda mi, dep=None: [stage(ffn_w_gate, "gate", 4, mi, dep), stage(ffn_w_up, "up", 4, mi, dep),
                                      stage(ffn_w_down, "down", 4, mi, dep)]
    small_stage = lax.dynamic_update_slice(jnp.zeros((N_CHIPS,) + small.shape, F32), small[None], (chip_id, 0, 0))
    first = ffn_stage(0)
    buckets = [first[:2] + [small_stage], first[2:]]
    sizes = [len(b) for b in buckets]
    gather_sems, in_flight, tok = gather_start([t for b in buckets for t in b], sizes, "gather_start_first")
    buckets = [[stage(gla_w_in, "win", 1, 0, tok), stage(gla_w_out, "wout", 1, 0, tok)],
               ffn_stage(1, tok), ffn_stage(2, tok), [stage(pool_w, "pool", 1, 0, tok)] + ffn_stage(3, tok)]
    more_sems, more_in_flight, gather_token = gather_start([t for b in buckets for t in b],
                                                            [len(b) for b in buckets], "gather_start_rest")
    sizes += [len(b) for b in buckets]
    gather_sems += more_sems
    in_flight += more_in_flight
    starts = [sum(sizes[:b]) for b in range(len(sizes))]

    def arrive(b, after, n_big):
        bufs = gather_wait(in_flight[starts[b]:starts[b] + sizes[b]], gather_sems[b], after, f"gather_wait_{b}")
        return forward_to_sibling(bufs[:n_big], f"gather_forward_{b}") + bufs[n_big:]

    ffn_w = lambda t: (t[0].reshape(N_CHIPS, D, Fs), t[1].reshape(N_CHIPS, D, Fs), t[2].reshape(F, D))
    got = arrive(0, gather_token, 2)
    wg, wu, wd = [None] * 4, [None] * 4, [None] * 4
    wg[0], wu[0] = got[0].reshape(N_CHIPS, D, Fs), got[1].reshape(N_CHIPS, D, Fs)
    sm = got[2]
    unshard = lambda t: t.transpose(1, 0, 2).reshape(t.shape[1], D)
    meta_f = unshard(sm[:, 0:16])
    ffn_norm_f = unshard(sm[:, 16:20])
    w_lr_f = sm[:, 24:32].reshape(N_CHIPS, GATE_RANK, dk // N_CHIPS).transpose(1, 0, 2).reshape(GATE_RANK, dk)
    pool_norm_f = sm[:, 32].reshape(1, D)
    pool_b_f = sm[:, 40].reshape(N_CHIPS, 4, W // N_CHIPS).transpose(1, 0, 2).reshape(1, D)
    pool_scale_f = sm[:, 48].reshape(1, D)
    wlr_pad = jnp.pad(w_lr_f.astype(BF16), ((0, LR_W - GATE_RANK), (0, 0)))
    final_g = final_norm.reshape(1, D)
    qkv = 2 * dk + D

    x0 = jnp.concatenate([jnp.zeros((PAD, D), F32), meta_f, x[0]], axis=0)
    target = jnp.pad(loss_target[0], ((OFF, 0), (0, 0)))
    h0, rstd0 = rmsnorm_fwd(x0, ffn_norm_f[0:1], BF16, "ffn_norm_0")
    gate0, up0 = ffn_gateup(h0, wg[0], wu[0], "ffn_gateup_0")
    wd[0] = arrive(1, gate0, 1)[0].reshape(F, D)
    x1 = mm_residual([gate0, up0], wd[0], x0, 0.5, Fs, "ffn_down_0", tm_target=704)
    ffn0 = (x0, ffn_norm_f[0:1], h0, rstd0, gate0, up0)
    got = arrive(2, x1, 2)
    w_in = got[0].reshape(N_CHIPS, D, n_in).transpose(1, 0, 2).reshape(D, N_CHIPS * n_in)
    w_out = got[1].reshape(D, D)
    w_all = jnp.concatenate([w_in[:, :qkv], w_in[:, qkv + GATE_RANK:], w_in[:, qkv:qkv + GATE_RANK],
                             jnp.zeros((D, LR_W - GATE_RANK), BF16)], axis=1)
    hg, rstd_g = rmsnorm_fwd(x1, gla_norm, BF16, "gla_norm")
    proj = mm_nn(hg, w_all, F32, "gla_proj")
    o, st = gla_fwd(proj, wlr_pad, gla_b_lr, D)
    gated = gla_post_fwd(o, proj, gla_head_norm, D)
    x2 = mm_residual([gated], w_out, x1, 1.0, D, "gla_out")
    wg[1], wu[1], wd[1] = ffn_w(arrive(3, x2, 3))
    x3, ffn1 = _ffn_fwd(x2, ffn_norm_f[1:2], wg[1], wu[1], wd[1], "1")
    wg[2], wu[2], wd[2] = ffn_w(arrive(4, x3, 3))
    x4, ffn2 = _ffn_fwd(x3, ffn_norm_f[2:3], wg[2], wu[2], wd[2], "2")
    got = arrive(5, x4, 4)
    w_pool = got[0].reshape(N_CHIPS, 4, W // N_CHIPS, W).transpose(1, 0, 2, 3).reshape(4, W, W)
    wg[3], wu[3], wd[3] = ffn_w(got[1:])
    hp, rstd_p = rmsnorm_fwd(x4, pool_norm_f, F32, "pool_norm")
    pooled = pool_window(hp)
    x5 = pool_mix(pooled, x4, w_pool, pool_b_f, pool_scale_f)
    x6, ffn3 = _ffn_fwd(x5, ffn_norm_f[3:4], wg[3], wu[3], wd[3], "3")
    loss, dx6, d_final = final_loss(x6, final_g, target)

    reducer = GradReducer(core, chip, [("gate", 4), ("up", 4), ("down", 4), ("win", 1), ("wout", 1), ("pool", 1)])

    def settle(begun, after):
        for state in begun:
            reducer.end(state, after)

    dx5, dn3, red3, tok = _ffn_bwd(dx6, ffn3, wg[3], wu[3], wd[3], 3, reducer)
    dpooled, d_wpool, d_pool_b, d_pool_scale = pool_mix_bwd(dx5, pooled, w_pool, pool_b_f, pool_scale_f, dep=tok)
    dhp = pool_window_bwd(dpooled)
    dx4, d_pool_norm = rmsnorm_bwd(dhp, x4, pool_norm_f, rstd_p, dx5, "pool_norm_bwd")
    d_wpool = d_wpool.reshape(4, N_CHIPS, W // N_CHIPS, W).transpose(1, 0, 2, 3)
    sent_p, tok = reducer.send(d_wpool, "pool", 0, "pool")
    dx3, dn2, red2, tok = _ffn_bwd(dx4, ffn2, wg[2], wu[2], wd[2], 2, reducer, dep=tok)
    redp, tok_p = reducer.begin([sent_p], dx3, "pool")
    settle(red3, dx3)
    dx2, dn1, red1, tok = _ffn_bwd(dx3, ffn1, wg[1], wu[1], wd[1], 1, reducer, dep=[tok, tok_p])
    settle(red2 + [redp], dx2)
    tm = _tile(M, 352)
    td = _tile(D, 512, 128)
    d_wout = mm_tn(gated, dx2, td, td, "gla_out_bwd_w", dep=tok)
    sent_o, tok = reducer.send(d_wout, "wout", 0, "wout")
    dgated = mm_nt([(dx2, w_out)], tm, td, D, "gla_out_bwd_act", dep=tok)
    redo, tok_o = reducer.begin([sent_o], dgated, "wout")
    do, dr, d_head_norm = gla_post_bwd(dgated, o, proj, gla_head_norm, D)
    dq, dkk, dv, dlr, dwlr, dblr = gla_bwd(proj, wlr_pad, gla_b_lr, st, do, D)
    dproj = jnp.concatenate([dq, dkk, dv, dr, dlr.astype(BF16)], axis=1)
    tp = _tile(proj.shape[1], 896, 128)
    d_wall = mm_tn(hg, dproj, td, tp, "gla_proj_bwd_w", dep=tok_o)
    d_win = jnp.concatenate([d_wall[:, :qkv], d_wall[:, qkv + D:qkv + D + GATE_RANK], d_wall[:, qkv:qkv + D]], axis=1)
    d_win = d_win.reshape(D, N_CHIPS, n_in).transpose(1, 0, 2)
    sent_i, tok = reducer.send(d_win, "win", 0, "win")
    dhg = mm_nt([(dproj, w_all)], tm, D, tp, "gla_proj_bwd_act", dep=tok)
    redi, tok = reducer.begin([sent_i], dhg, "win")
    dx1, d_gla_norm = rmsnorm_bwd(dhg, x1, gla_norm, rstd_g, dx2, "gla_norm_bwd")
    settle(red1 + [redo], dx1)
    dx0, dn0, red0, _ = _ffn_bwd(dx1, ffn0, wg[0], wu[0], wd[0], 0, reducer, dep=tok, per_tensor=True)
    settle([redi], dx0)

    d_wlr = dwlr[:, :GATE_RANK].transpose(1, 0, 2).reshape(GATE_RANK, dk)
    pieces = [dx0[PAD:OFF], dn0, dn1, dn2, dn3, d_gla_norm, d_wlr,
              dblr.reshape(1, dk), d_head_norm, d_pool_norm, d_pool_b, d_pool_scale, d_final]
    packed = jnp.concatenate([_pad_rows(p.reshape(-1, Dq)) for p in pieces], axis=0)
    total = sum_devices(gather_devices(packed))

    settle(red0[:-1], total)
    settle(red0[-1:], reducer.groups["gate"])
    reduced = reducer.finish()
    g_gate = reduced["gate"].reshape(ffn_w_gate.shape)
    g_up = reduced["up"].reshape(ffn_w_up.shape)
    g_down = reduced["down"].reshape(ffn_w_down.shape)
    g_win = reduced["win"].reshape(gla_w_in.shape)
    g_wout = reduced["wout"].reshape(gla_w_out.shape)
    g_wpool = reduced["pool"].reshape(pool_w.shape)
    sums, at = [], 0
    for p in pieces:
        r = p.size // Dq
        sums.append(total[at:at + r].reshape(p.shape))
        at += r + (-r % 8)
    (s_meta, s_n0, s_n1, s_n2, s_n3, s_gla_norm, s_wlr, s_blr, s_head_norm, s_pool_norm, s_pool_b, s_pool_scale,
     s_final) = sums
    s_ffn_norm = jnp.stack([s_n0, s_n1, s_n2, s_n3], axis=0)[:, 0]
    mine = lambda t, width: lax.dynamic_slice_in_dim(t, chip_id * width, width, axis=t.ndim - 1)
    g_meta = mine(s_meta, Dq)
    g_ffn_norm = mine(s_ffn_norm, Dq).reshape(ffn_norm.shape)
    g_gla_norm = s_gla_norm
    g_wlr = mine(s_wlr, dk // N_CHIPS).reshape(gla_w_lr.shape)
    g_blr = s_blr
    g_head_norm = s_head_norm
    g_pool_norm = mine(s_pool_norm, Dq)
    g_pool_b = mine(s_pool_b.reshape(4, W), W // N_CHIPS).reshape(pool_b.shape)
    g_pool_scale = mine(s_pool_scale, Dq)
    g_final = s_final.reshape(final_norm.shape)

    weights = [meta, ffn_norm, ffn_w_gate, ffn_w_up, ffn_w_down, gla_norm, gla_w_in, gla_w_lr, gla_b_lr,
               gla_head_norm, gla_w_out, pool_norm, pool_w, pool_b, pool_scale, final_norm]
    moments_m = [m_meta, m_ffn_norm, m_ffn_w_gate, m_ffn_w_up, m_ffn_w_down, m_gla_norm, m_gla_w_in, m_gla_w_lr,
                 m_gla_b_lr, m_gla_head_norm, m_gla_w_out, m_pool_norm, m_pool_w, m_pool_b, m_pool_scale,
                 m_final_norm]
    moments_v = [v_meta, v_ffn_norm, v_ffn_w_gate, v_ffn_w_up, v_ffn_w_down, v_gla_norm, v_gla_w_in, v_gla_w_lr,
                 v_gla_b_lr, v_gla_head_norm, v_gla_w_out, v_pool_norm, v_pool_w, v_pool_b, v_pool_scale,
                 v_final_norm]
    grads_w = [g_meta, g_ffn_norm, g_gate, g_up, g_down, g_gla_norm, g_win, g_wlr, g_blr, g_head_norm, g_wout,
               g_pool_norm, g_wpool, g_pool_b, g_pool_scale, g_final]
    deltas, new_m, new_v = [], [], []
    for i, (w, g, m, v) in enumerate(zip(weights, grads_w, moments_m, moments_v)):
        d, nm, nv = adamw(w, g, m, v, f"adamw_{i}")
        deltas.append(d)
        new_m.append(nm)
        new_v.append(nv)

    loss = lax.psum(loss[0, 0], ("x", "y", "c"))
    grad_x = dx0[OFF:][None]
    return (loss, grad_x, *grads_w, *deltas, *new_m, *new_v)
```

```python
import functools

import jax
import jax.numpy as jnp
from jax import lax
from jax.experimental import pallas as pl
from jax.experimental.pallas import tpu as pltpu

F32 = jnp.float32
BF16 = jnp.bfloat16
MESH = pl.DeviceIdType.MESH
ANY = pl.BlockSpec(memory_space=pl.ANY)

N_META = 16
CHUNK = 64
PAD = CHUNK - N_META
OFF = PAD + N_META
EPS = 1e-6
HEADS = 4
GATE_RANK = 16
GATE_NORM = 16.0
LR_W = 128
N_CHIPS = 4
N_DEV = 8
ADAM_LR, ADAM_B1, ADAM_B2, ADAM_EPS, ADAM_WD, ADAM_STEP = 0.001, 0.9, 0.999, 1e-08, 0.01, 10
VMEM_LIMIT = 56 * 1024 * 1024
ROW_TILE = 176
ONE_BUFFER = pl.Buffered(1)


def _tile(n, target, mult=16):
    best = None
    for d in range(mult, min(n, target) + 1, mult):
        if n % d == 0:
            best = d
    return best if best is not None else n


def _params(*sem):
    return pltpu.CompilerParams(dimension_semantics=sem, vmem_limit_bytes=VMEM_LIMIT)


def _dot(a, b):
    return jnp.dot(a, b, preferred_element_type=F32)


def _dot_nt(a, b):
    return lax.dot_general(a, b, (((1,), (1,)), ((), ())), preferred_element_type=F32)


def _dot_tn(a, b):
    return lax.dot_general(a, b, (((0,), (0,)), ((), ())), preferred_element_type=F32)


MXU_WIDTH = 256


def _chunks(n):
    return [slice(lo, min(lo + MXU_WIDTH, n)) for lo in range(0, n, MXU_WIDTH)]


def _sigmoid(x):
    return 1.0 / (1.0 + jnp.exp(-x))


def _rows(tile, width=1):
    return lax.broadcasted_iota(jnp.int32, (tile, width), 0)


def _dep_args(dep):
    if dep is None:
        return []
    return list(dep) if isinstance(dep, (list, tuple)) else [dep]


def _dep_specs(dep):
    return [ANY] * len(_dep_args(dep))


def _pad_rows(t):
    return jnp.pad(t, ((0, -t.shape[0] % 8), (0, 0)))


def rmsnorm_fwd(x, g, out_dtype, name):
    M, D = x.shape
    tr = _tile(M, ROW_TILE)

    def body(x_ref, g_ref, h_ref, r_ref):
        xv = x_ref[...]
        r = lax.rsqrt(jnp.mean(xv * xv, axis=-1, keepdims=True) + EPS)
        h_ref[...] = (xv * r * g_ref[...]).astype(out_dtype)
        r_ref[...] = r

    return pl.pallas_call(
        body, name=name, grid=(M // tr,),
        in_specs=[pl.BlockSpec((tr, D), lambda i: (i, 0)), pl.BlockSpec((1, D), lambda i: (0, 0))],
        out_specs=[pl.BlockSpec((tr, D), lambda i: (i, 0)), pl.BlockSpec((tr, 1), lambda i: (i, 0))],
        out_shape=[jax.ShapeDtypeStruct((M, D), out_dtype), jax.ShapeDtypeStruct((M, 1), F32)],
        compiler_params=_params("parallel"),
    )(x, g)


def rmsnorm_bwd(dh, x, g, rstd, dres, name):
    M, D = x.shape
    tr = _tile(M, ROW_TILE)

    def body(dh_ref, x_ref, g_ref, r_ref, dres_ref, dx_ref, dg_ref):
        @pl.when(pl.program_id(0) == 0)
        def _():
            dg_ref[...] = jnp.zeros_like(dg_ref)

        r = r_ref[...]
        xhat = x_ref[...] * r
        dhv = dh_ref[...]
        gd = dhv * g_ref[...]
        dx_ref[...] = dres_ref[...] + r * (gd - xhat * jnp.mean(gd * xhat, axis=-1, keepdims=True))
        dg_ref[...] += jnp.sum(dhv * xhat, axis=0, keepdims=True)

    row = pl.BlockSpec((tr, D), lambda i: (i, 0))
    vec = pl.BlockSpec((1, D), lambda i: (0, 0))
    return pl.pallas_call(
        body, name=name, grid=(M // tr,),
        in_specs=[row, row, vec, pl.BlockSpec((tr, 1), lambda i: (i, 0)), row],
        out_specs=[row, vec],
        out_shape=[jax.ShapeDtypeStruct((M, D), F32), jax.ShapeDtypeStruct((1, D), F32)],
        compiler_params=_params("arbitrary"),
    )(dh, x, g, rstd, dres)


def final_loss(x, g, target):
    M, D = x.shape
    tr = _tile(M, ROW_TILE)

    def body(x_ref, g_ref, t_ref, loss_ref, dx_ref, dg_ref):
        i = pl.program_id(0)

        @pl.when(i == 0)
        def _():
            loss_ref[...] = jnp.zeros_like(loss_ref)
            dg_ref[...] = jnp.zeros_like(dg_ref)

        live = (_rows(tr) + i * tr) >= OFF
        xv = x_ref[...]
        gv = g_ref[...]
        r = lax.rsqrt(jnp.mean(xv * xv, axis=-1, keepdims=True) + EPS)
        xhat = xv * r
        err = jnp.where(live, xhat * gv - t_ref[...], 0.0)
        loss_ref[...] += 0.5 * jnp.sum(jnp.mean(err * err, axis=-1, keepdims=True), axis=0, keepdims=True)
        dy = err * (1.0 / D)
        gd = dy * gv
        dx_ref[...] = r * (gd - xhat * jnp.mean(gd * xhat, axis=-1, keepdims=True))
        dg_ref[...] += jnp.sum(dy * xhat, axis=0, keepdims=True)

    row = pl.BlockSpec((tr, D), lambda i: (i, 0))
    vec = pl.BlockSpec((1, D), lambda i: (0, 0))
    return pl.pallas_call(
        body, name="final_loss", grid=(M // tr,),
        in_specs=[row, vec, row],
        out_specs=[pl.BlockSpec((1, 1), lambda i: (0, 0)), row, vec],
        out_shape=[jax.ShapeDtypeStruct((1, 1), F32), jax.ShapeDtypeStruct((M, D), F32),
                   jax.ShapeDtypeStruct((1, D), F32)],
        compiler_params=_params("arbitrary"),
    )(x, g, target)


def mm_nn(a, w, out_dtype, name, tm_target=704, tn_target=896):
    M, K = a.shape
    N = w.shape[1]
    tm, tn = _tile(M, tm_target), _tile(N, tn_target, 128)

    def body(a_ref, w_ref, o_ref):
        o_ref[...] = _dot(a_ref[...], w_ref[...]).astype(out_dtype)

    return pl.pallas_call(
        body, name=name, grid=(N // tn, M // tm),
        in_specs=[pl.BlockSpec((tm, K), lambda n, i: (i, 0)), pl.BlockSpec((K, tn), lambda n, i: (0, n))],
        out_specs=pl.BlockSpec((tm, tn), lambda n, i: (i, n)),
        out_shape=jax.ShapeDtypeStruct((M, N), out_dtype),
        compiler_params=_params("parallel", "parallel"),
    )(a, w)


def ffn_gateup(h, wg, wu, name):
    M, D = h.shape
    Fs = wg.shape[2]
    tm = _tile(M, 352)

    def body(h_ref, wg_ref, wu_ref, g_ref, u_ref):
        hv = h_ref[...]
        g_ref[...] = _dot(hv, wg_ref[...]).astype(BF16)
        u_ref[...] = _dot(hv, wu_ref[...]).astype(BF16)

    wspec = pl.BlockSpec((None, D, Fs), lambda j, i: (j, 0, 0))
    ospec = pl.BlockSpec((tm, Fs), lambda j, i: (i, j))
    return pl.pallas_call(
        body, name=name, grid=(N_CHIPS, M // tm),
        in_specs=[pl.BlockSpec((tm, D), lambda j, i: (i, 0)), wspec, wspec],
        out_specs=[ospec, ospec],
        out_shape=[jax.ShapeDtypeStruct((M, N_CHIPS * Fs), BF16)] * 2,
        compiler_params=_params("parallel", "parallel"),
    )(h, wg, wu)


def mm_residual(acts, w, x, scale, tk, name, tm_target=352):
    M, N = x.shape
    K = w.shape[0]
    tm = _tile(M, tm_target)
    swiglu = len(acts) == 2

    def body(*refs):
        a_refs, (w_ref, x_ref, o_ref, acc) = refs[:len(acts)], refs[len(acts):]
        k = pl.program_id(1)

        @pl.when(k == 0)
        def _():
            acc[...] = jnp.zeros_like(acc)

        if swiglu:
            gv = a_refs[0][...].astype(F32)
            av = (gv * _sigmoid(gv) * a_refs[1][...].astype(F32)).astype(BF16)
        else:
            av = a_refs[0][...]
        acc[...] += _dot(av, w_ref[...])

        @pl.when(k == pl.num_programs(1) - 1)
        def _():
            o_ref[...] = x_ref[...] + scale * acc[...]

    aspec = pl.BlockSpec((tm, tk), lambda i, k: (i, k))
    return pl.pallas_call(
        body, name=name, grid=(M // tm, K // tk),
        in_specs=[aspec] * len(acts) + [pl.BlockSpec((tk, N), lambda i, k: (k, 0)),
                                        pl.BlockSpec((tm, N), lambda i, k: (i, 0), pipeline_mode=ONE_BUFFER)],
        out_specs=pl.BlockSpec((tm, N), lambda i, k: (i, 0), pipeline_mode=ONE_BUFFER),
        out_shape=jax.ShapeDtypeStruct((M, N), F32),
        scratch_shapes=[pltpu.VMEM((tm, N), F32)],
        compiler_params=_params("parallel", "arbitrary"),
    )(*acts, w, x)


def ffn_bwd_act(dout, wd, gate, up, name, dep=None):
    M, D = dout.shape
    F = wd.shape[0]
    Fs = F // N_CHIPS
    tm = _tile(M, 352)

    def body(dy_ref, wd_ref, g_ref, u_ref, *rest):
        dg_ref, du_ref, a_ref = rest[-3:]
        dy = (0.5 * dy_ref[...]).astype(BF16)
        for cols in _chunks(Fs):
            da = _dot_nt(dy, wd_ref[cols, :])
            gv = g_ref[:, cols].astype(F32)
            uv = u_ref[:, cols].astype(F32)
            s = _sigmoid(gv)
            silu = gv * s
            a_ref[:, cols] = (silu * uv).astype(BF16)
            dg_ref[:, cols] = (da * uv * (s * (1.0 + gv * (1.0 - s)))).astype(BF16)
            du_ref[:, cols] = (da * silu).astype(BF16)

    fspec = pl.BlockSpec((tm, Fs), lambda j, i: (i, j))
    return pl.pallas_call(
        body, name=name, grid=(N_CHIPS, M // tm),
        in_specs=[pl.BlockSpec((tm, D), lambda j, i: (i, 0)), pl.BlockSpec((Fs, D), lambda j, i: (j, 0)),
                  fspec, fspec] + _dep_specs(dep),
        out_specs=[fspec, fspec, fspec],
        out_shape=[jax.ShapeDtypeStruct((M, F), BF16)] * 3,
        compiler_params=_params("parallel", "parallel"),
    )(dout, wd, gate, up, *_dep_args(dep))


def mm_tn(a, b, ta, tb, name, b_scale=1.0, stacked_out=False, out_dtype=BF16, dep=None):
    T, Ma = a.shape
    Nb = b.shape[1]

    def body(a_ref, b_ref, *rest):
        o_ref = rest[-1]
        bv = b_ref[...]
        if b_scale != 1.0:
            bv = b_scale * bv
        o_ref[...] = _dot_tn(a_ref[...], bv.astype(BF16)).astype(out_dtype)

    if stacked_out:
        out_spec = pl.BlockSpec((None, ta, tb), lambda jb, ja: (jb, ja, 0))
        out_shape = jax.ShapeDtypeStruct((Nb // tb, Ma, tb), out_dtype)
    else:
        out_spec = pl.BlockSpec((ta, tb), lambda jb, ja: (ja, jb))
        out_shape = jax.ShapeDtypeStruct((Ma, Nb), out_dtype)
    return pl.pallas_call(
        body, name=name, grid=(Nb // tb, Ma // ta),
        in_specs=[pl.BlockSpec((T, ta), lambda jb, ja: (0, ja)), pl.BlockSpec((T, tb), lambda jb, ja: (0, jb))]
        + _dep_specs(dep),
        out_specs=out_spec, out_shape=out_shape,
        compiler_params=_params("parallel", "parallel"),
    )(a, b, *_dep_args(dep))


def mm_nt(pairs, tm, tn, tk, name, a_scale=1.0, stacked_w=False, dep=None):
    M, K = pairs[0][0].shape
    N = pairs[0][1].shape[1] if stacked_w else pairs[0][1].shape[0]
    n_pairs = len(pairs)

    def body(*refs):
        o_ref, acc = refs[-2:]
        k = pl.program_id(2)

        @pl.when(k == 0)
        def _():
            acc[...] = jnp.zeros_like(acc)

        for p in range(n_pairs):
            av = refs[2 * p][...]
            if a_scale != 1.0:
                av = a_scale * av
            acc[...] += _dot_nt(av.astype(BF16), refs[2 * p + 1][...])

        @pl.when(k == pl.num_programs(2) - 1)
        def _():
            o_ref[...] = acc[...]

    aspec = pl.BlockSpec((tm, tk), lambda i, n, k: (i, k))
    if stacked_w:
        wspec = pl.BlockSpec((None, tn, tk), lambda i, n, k: (k, n, 0))
    else:
        wspec = pl.BlockSpec((tn, tk), lambda i, n, k: (n, k))
    return pl.pallas_call(
        body, name=name, grid=(M // tm, N // tn, K // tk),
        in_specs=[aspec, wspec] * n_pairs + _dep_specs(dep),
        out_specs=pl.BlockSpec((tm, tn), lambda i, n, k: (i, n), pipeline_mode=ONE_BUFFER),
        out_shape=jax.ShapeDtypeStruct((M, N), F32),
        scratch_shapes=[pltpu.VMEM((tm, tn), F32)],
        compiler_params=_params("parallel", "parallel", "arbitrary"),
    )(*[t for pair in pairs for t in pair], *_dep_args(dep))


def _tri(lower):
    r = lax.broadcasted_iota(jnp.int32, (CHUNK, CHUNK), 0)
    c = lax.broadcasted_iota(jnp.int32, (CHUNK, CHUNK), 1)
    return (r >= c) if lower else (r <= c)


def _tri_sum(mask, x, pieces):
    ones = mask.astype(BF16)
    acc = jnp.zeros_like(x)
    rest = x
    for _ in range(pieces):
        piece = rest.astype(BF16)
        acc = acc + _dot(ones, piece)
        rest = rest - piece.astype(F32)
    return acc


def _gla_gates(lr_ref, wlr_ref, blr_ref, chunk):
    z = _dot(lr_ref[...].astype(BF16), wlr_ref[...]) + blr_ref[...]
    live = (_rows(CHUNK) + chunk * CHUNK) >= PAD
    lg = jnp.where(live, (jnp.minimum(z, 0.0) - jnp.log(1.0 + jnp.exp(-jnp.abs(z)))) * (1.0 / GATE_NORM), 0.0)
    b = _tri_sum(_tri(True), lg, 3)
    b_last = jnp.sum(lg, axis=0, keepdims=True)
    b_mid = jnp.sum(jnp.where(_rows(CHUNK) < CHUNK // 2, lg, 0.0), axis=0, keepdims=True)
    return z, live, b, b_last, b_mid


def _gla_specs(dkh, dvh, D, chunk_of):
    lr_blk = (3 * D) // LR_W
    return [
        pl.BlockSpec((CHUNK, dkh), lambda c, h: (chunk_of(c), h)),
        pl.BlockSpec((CHUNK, dkh), lambda c, h: (chunk_of(c), HEADS + h)),
        pl.BlockSpec((CHUNK, dvh), lambda c, h: (chunk_of(c), HEADS + h)),
        pl.BlockSpec((CHUNK, LR_W), lambda c, h: (chunk_of(c), lr_blk)),
        pl.BlockSpec((LR_W, dkh), lambda c, h: (0, h)),
        pl.BlockSpec((1, dkh), lambda c, h: (0, h)),
    ]


def gla_fwd(proj, wlr, blr, D):
    M = proj.shape[0]
    n = M // CHUNK
    dkh, dvh = D // 2 // HEADS, D // HEADS
    qscale = float(dkh) ** -0.5

    def body(q_ref, k_ref, v_ref, lr_ref, wlr_ref, blr_ref, o_ref, st_ref, S):
        c, h = pl.program_id(0), pl.program_id(1)

        @pl.when(c == 0)
        def _():
            S[h] = jnp.zeros((dvh, dkh), F32)

        _, _, b, b_last, b_mid = _gla_gates(lr_ref, wlr_ref, blr_ref, c)
        q = q_ref[...] * qscale
        k = k_ref[...]
        v = v_ref[...].astype(BF16)
        s0 = S[h]
        st_ref[...] = s0
        qb = (q * jnp.exp(b)).astype(BF16)
        kb = (k * jnp.exp(b_last - b)).astype(BF16)
        qt = (q * jnp.exp(b - b_mid)).astype(BF16)
        kt = (k * jnp.exp(b_mid - b)).astype(BF16)
        a = jnp.where(_tri(True), _dot_nt(qt, kt), 0.0).astype(BF16)
        o_ref[...] = _dot_nt(qb, s0.astype(BF16)) + _dot(a, v)
        S[h] = jnp.exp(b_last) * s0 + _dot_tn(v, kb)

    return pl.pallas_call(
        body, name="gla_fwd", grid=(n, HEADS),
        in_specs=_gla_specs(dkh, dvh, D, lambda c: c),
        out_specs=[pl.BlockSpec((CHUNK, dvh), lambda c, h: (c, h)),
                   pl.BlockSpec((None, None, dvh, dkh), lambda c, h: (c, h, 0, 0))],
        out_shape=[jax.ShapeDtypeStruct((M, D), F32), jax.ShapeDtypeStruct((n, HEADS, dvh, dkh), F32)],
        scratch_shapes=[pltpu.VMEM((HEADS, dvh, dkh), F32)],
        compiler_params=_params("arbitrary", "arbitrary"),
    )(proj, proj, proj, proj, wlr, blr)


def gla_bwd(proj, wlr, blr, st, do, D):
    M = proj.shape[0]
    n = M // CHUNK
    dkh, dvh = D // 2 // HEADS, D // HEADS
    qscale = float(dkh) ** -0.5
    rev = lambda c: n - 1 - c

    def body(q_ref, k_ref, v_ref, lr_ref, wlr_ref, blr_ref, st_ref, do_ref,
             dq_ref, dk_ref, dv_ref, dlr_ref, dwlr_ref, dblr_ref, dS, acc_w, acc_b):
        step, h = pl.program_id(0), pl.program_id(1)
        c = n - 1 - step

        @pl.when(step == 0)
        def _():
            dS[h] = jnp.zeros((dvh, dkh), F32)
            acc_w[h] = jnp.zeros((LR_W, dkh), F32)
            acc_b[h] = jnp.zeros((1, dkh), F32)

        z, live, b, b_last, b_mid = _gla_gates(lr_ref, wlr_ref, blr_ref, c)
        q = q_ref[...] * qscale
        k = k_ref[...]
        v = v_ref[...].astype(BF16)
        dov = do_ref[...].astype(BF16)
        s0 = st_ref[...]
        ds1 = dS[h]
        ds1b = ds1.astype(BF16)
        e_b, e_lb = jnp.exp(b), jnp.exp(b_last - b)
        e_bm, e_mb = jnp.exp(b - b_mid), jnp.exp(b_mid - b)
        e_last = jnp.exp(b_last)
        qb, kb, qt, kt = q * e_b, k * e_lb, q * e_bm, k * e_mb
        qbb, kbb, qtb, ktb = qb.astype(BF16), kb.astype(BF16), qt.astype(BF16), kt.astype(BF16)
        lower = _tri(True)
        a = jnp.where(lower, _dot_nt(qtb, ktb), 0.0).astype(BF16)
        da = jnp.where(lower, _dot_nt(dov, v), 0.0).astype(BF16)

        dqb = _dot(dov, s0.astype(BF16))
        dqt = _dot(da, ktb)
        dkt = _dot_tn(da, qtb)
        dkb = _dot(v, ds1b)
        keep = live.astype(F32)
        dv_ref[...] = (keep * (_dot_tn(a, dov) + _dot_nt(kbb, ds1b))).astype(BF16)
        dq_ref[...] = (keep * qscale * (dqb * e_b + dqt * e_bm)).astype(BF16)
        dk_ref[...] = (keep * (dkb * e_lb + dkt * e_mb)).astype(BF16)

        db = dqb * qb - dkb * kb + dqt * qt - dkt * kt
        db_last = (jnp.sum(dkb * kb, axis=0, keepdims=True)
                   + jnp.sum(ds1 * s0, axis=0, keepdims=True) * e_last)
        db = db + jnp.where(_rows(CHUNK) == CHUNK - 1, db_last, 0.0)
        dlg = jnp.where(live, _tri_sum(_tri(False), db, 2), 0.0)
        dz = dlg * (1.0 / GATE_NORM) / (1.0 + jnp.exp(z))
        dzb = dz.astype(BF16)

        dlr_h = _dot_nt(dzb, wlr_ref[...])

        @pl.when(h == 0)
        def _():
            dlr_ref[...] = dlr_h

        @pl.when(h > 0)
        def _():
            dlr_ref[...] += dlr_h

        acc_w[h] += _dot_tn(lr_ref[...].astype(BF16), dzb)
        acc_b[h] += jnp.sum(dz, axis=0, keepdims=True)
        dS[h] = e_last * ds1 + _dot_tn(dov, qbb)

        @pl.when(step == n - 1)
        def _():
            dwlr_ref[h] = acc_w[h]
            dblr_ref[h] = acc_b[h]

    return pl.pallas_call(
        body, name="gla_bwd", grid=(n, HEADS),
        in_specs=_gla_specs(dkh, dvh, D, rev) + [
            pl.BlockSpec((None, None, dvh, dkh), lambda c, h: (rev(c), h, 0, 0)),
            pl.BlockSpec((CHUNK, dvh), lambda c, h: (rev(c), h))],
        out_specs=[pl.BlockSpec((CHUNK, dkh), lambda c, h: (rev(c), h)),
                   pl.BlockSpec((CHUNK, dkh), lambda c, h: (rev(c), h)),
                   pl.BlockSpec((CHUNK, dvh), lambda c, h: (rev(c), h)),
                   pl.BlockSpec((CHUNK, LR_W), lambda c, h: (rev(c), 0)),
                   pl.BlockSpec((HEADS, LR_W, dkh), lambda c, h: (0, 0, 0)),
                   pl.BlockSpec((HEADS, 1, dkh), lambda c, h: (0, 0, 0))],
        out_shape=[jax.ShapeDtypeStruct((M, D // 2), BF16), jax.ShapeDtypeStruct((M, D // 2), BF16),
                   jax.ShapeDtypeStruct((M, D), BF16), jax.ShapeDtypeStruct((M, LR_W), F32),
                   jax.ShapeDtypeStruct((HEADS, LR_W, dkh), F32), jax.ShapeDtypeStruct((HEADS, 1, dkh), F32)],
        scratch_shapes=[pltpu.VMEM((HEADS, dvh, dkh), F32), pltpu.VMEM((HEADS, LR_W, dkh), F32),
                        pltpu.VMEM((HEADS, 1, dkh), F32)],
        compiler_params=_params("arbitrary", "arbitrary"),
    )(proj, proj, proj, proj, wlr, blr, st, do)


def gla_post_fwd(o, proj, head_norm, D):
    M = o.shape[0]
    dvh = D // HEADS
    tr = _tile(M, ROW_TILE)

    def body(o_ref, r_ref, hn_ref, out_ref):
        for hd in range(HEADS):
            cols = slice(hd * dvh, (hd + 1) * dvh)
            ov = o_ref[:, cols]
            rs = lax.rsqrt(jnp.mean(ov * ov, axis=-1, keepdims=True) + EPS)
            rv = r_ref[:, cols]
            out_ref[:, cols] = (ov * rs * hn_ref[...] * (rv * _sigmoid(rv))).astype(BF16)

    row = pl.BlockSpec((tr, D), lambda i: (i, 0))
    return pl.pallas_call(
        body, name="gla_post_fwd", grid=(M // tr,),
        in_specs=[row, pl.BlockSpec((tr, D), lambda i: (i, 2)), pl.BlockSpec((1, dvh), lambda i: (0, 0))],
        out_specs=row, out_shape=jax.ShapeDtypeStruct((M, D), BF16),
        compiler_params=_params("parallel"),
    )(o, proj, head_norm)


def gla_post_bwd(dgated, o, proj, head_norm, D):
    M = o.shape[0]
    dvh = D // HEADS
    tr = _tile(M, ROW_TILE)

    def body(dg_ref, o_ref, r_ref, hn_ref, do_ref, dr_ref, dhn_ref):
        @pl.when(pl.program_id(0) == 0)
        def _():
            dhn_ref[...] = jnp.zeros_like(dhn_ref)

        hn = hn_ref[...]
        dhn = jnp.zeros((1, dvh), F32)
        for hd in range(HEADS):
            cols = slice(hd * dvh, (hd + 1) * dvh)
            ov = o_ref[:, cols]
            rs = lax.rsqrt(jnp.mean(ov * ov, axis=-1, keepdims=True) + EPS)
            ohat = ov * rs
            rv = r_ref[:, cols]
            s = _sigmoid(rv)
            dgv = dg_ref[:, cols]
            don = dgv * (rv * s)
            dr_ref[:, cols] = (dgv * ohat * hn * (s * (1.0 + rv * (1.0 - s)))).astype(BF16)
            gd = don * hn
            do_ref[:, cols] = rs * (gd - ohat * jnp.mean(gd * ohat, axis=-1, keepdims=True))
            dhn = dhn + jnp.sum(don * ohat, axis=0, keepdims=True)
        dhn_ref[...] += dhn

    row = pl.BlockSpec((tr, D), lambda i: (i, 0))
    vec = pl.BlockSpec((1, dvh), lambda i: (0, 0))
    return pl.pallas_call(
        body, name="gla_post_bwd", grid=(M // tr,),
        in_specs=[row, row, pl.BlockSpec((tr, D), lambda i: (i, 2)), vec],
        out_specs=[row, row, vec],
        out_shape=[jax.ShapeDtypeStruct((M, D), F32), jax.ShapeDtypeStruct((M, D), BF16),
                   jax.ShapeDtypeStruct((1, dvh), F32)],
        compiler_params=_params("arbitrary"),
    )(dgated, o, proj, head_norm)


def _pool_counts(M, g):
    t = _rows(M) - PAD
    win = jnp.left_shift(2, g)
    return t >= 0, jnp.maximum(jnp.minimum(t + 1, win), 1).astype(F32)


def _window_sum(x, g, M, back):
    sums = []
    s = x
    for lvl in range(4):
        sh = 1 << lvl
        s = s + pltpu.roll(s, (M - sh) if back else sh, 0)
        sums.append(s)
    return jnp.where(g == 0, sums[0], jnp.where(g == 1, sums[1], jnp.where(g == 2, sums[2], sums[3])))


POOL_COLS = 128


def pool_window(hp):
    M, D = hp.shape
    cw = min(POOL_COLS, D // 4)
    per_group = (D // 4) // cw

    def body(h_ref, p_ref):
        g = pl.program_id(0) // per_group
        live, cnt = _pool_counts(M, g)
        hv = h_ref[...]
        p_ref[...] = jnp.where(live, _window_sum(hv, g, M, False) / cnt - hv, 0.0).astype(BF16)

    col = pl.BlockSpec((M, cw), lambda j: (0, j))
    return pl.pallas_call(
        body, name="pool_window", grid=(D // cw,), in_specs=[col], out_specs=col,
        out_shape=jax.ShapeDtypeStruct((M, D), BF16), compiler_params=_params("parallel"),
    )(hp)


def pool_window_bwd(dpooled):
    M, D = dpooled.shape
    cw = min(POOL_COLS, D // 4)
    per_group = (D // 4) // cw

    def body(d_ref, o_ref):
        g = pl.program_id(0) // per_group
        live, cnt = _pool_counts(M, g)
        dv = jnp.where(live, d_ref[...], 0.0)
        o_ref[...] = jnp.where(live, _window_sum(dv / cnt, g, M, True) - dv, 0.0)

    col = pl.BlockSpec((M, cw), lambda j: (0, j))
    return pl.pallas_call(
        body, name="pool_window_bwd", grid=(D // cw,), in_specs=[col], out_specs=col,
        out_shape=jax.ShapeDtypeStruct((M, D), F32), compiler_params=_params("parallel"),
    )(dpooled)


def pool_mix(pooled, x, w, bias, scale):
    M, D = x.shape
    W = D // 4
    tm = _tile(M, 352)

    def body(p_ref, x_ref, w_ref, b_ref, s_ref, out_ref):
        live = (_rows(tm) + pl.program_id(1) * tm) >= PAD
        y = (_dot(p_ref[...], w_ref[...]) + b_ref[...]) * s_ref[...]
        out_ref[...] = x_ref[...] + jnp.where(live, y, 0.0)

    blk = pl.BlockSpec((tm, W), lambda g, i: (i, g))
    vec = pl.BlockSpec((1, W), lambda g, i: (0, g))
    return pl.pallas_call(
        body, name="pool_mix", grid=(4, M // tm),
        in_specs=[blk, blk, pl.BlockSpec((None, W, W), lambda g, i: (g, 0, 0)), vec, vec],
        out_specs=blk, out_shape=jax.ShapeDtypeStruct((M, D), F32),
        compiler_params=_params("parallel", "parallel"),
    )(pooled, x, w, bias, scale)


def pool_mix_bwd(dy, pooled, w, bias, scale, dep=None):
    M, D = dy.shape
    W = D // 4
    tm = _tile(M, 352)

    def body(dy_ref, p_ref, w_ref, b_ref, s_ref, *rest):
        dp_ref, dw_ref, db_ref, ds_ref, acc_w = rest[-5:]
        i = pl.program_id(1)

        @pl.when(i == 0)
        def _():
            acc_w[...] = jnp.zeros_like(acc_w)
            db_ref[...] = jnp.zeros_like(db_ref)
            ds_ref[...] = jnp.zeros_like(ds_ref)

        live = (_rows(tm) + i * tm) >= PAD
        dyv = jnp.where(live, dy_ref[...], 0.0)
        pooled = p_ref[...]
        wv = w_ref[...]
        ds_ref[...] += jnp.sum(dyv * (_dot(pooled, wv) + b_ref[...]), axis=0, keepdims=True)
        dys = dyv * s_ref[...]
        db_ref[...] += jnp.sum(dys, axis=0, keepdims=True)
        dysb = dys.astype(BF16)
        acc_w[...] += _dot_tn(pooled, dysb)
        dp_ref[...] = _dot_nt(dysb, wv)

        @pl.when(i == pl.num_programs(1) - 1)
        def _():
            dw_ref[...] = acc_w[...].astype(BF16)

    blk = pl.BlockSpec((tm, W), lambda g, i: (i, g))
    vec = pl.BlockSpec((1, W), lambda g, i: (0, g))
    wspec = pl.BlockSpec((None, W, W), lambda g, i: (g, 0, 0))
    return pl.pallas_call(
        body, name="pool_mix_bwd", grid=(4, M // tm),
        in_specs=[blk, blk, wspec, vec, vec] + _dep_specs(dep),
        out_specs=[blk, wspec, vec, vec],
        out_shape=[jax.ShapeDtypeStruct((M, D), F32), jax.ShapeDtypeStruct((4, W, W), BF16),
                   jax.ShapeDtypeStruct((1, D), F32), jax.ShapeDtypeStruct((1, D), F32)],
        scratch_shapes=[pltpu.VMEM((W, W), F32)],
        compiler_params=_params("parallel", "arbitrary"),
    )(dy, pooled, w, bias, scale, *_dep_args(dep))


def adamw(w, g, m, v, name, copy_g=False):
    shape = w.shape
    C = shape[-1]
    R = w.size // C
    tr = _tile(R, 256, 8)
    tc = C
    if tr == R and R > 256:
        tc = _tile(C, 256, 128)

    def body(w_ref, g_ref, m_ref, v_ref, d_ref, nm_ref, nv_ref, *g_out):
        gv = g_ref[...]
        for ref in g_out:
            ref[...] = gv
        nm = ADAM_B1 * m_ref[...] + (1.0 - ADAM_B1) * gv
        nv = ADAM_B2 * v_ref[...] + (1.0 - ADAM_B2) * (gv * gv)
        m_hat = nm / (1.0 - ADAM_B1 ** ADAM_STEP)
        v_hat = nv / (1.0 - ADAM_B2 ** ADAM_STEP)
        d_ref[...] = -ADAM_LR * (m_hat / (jnp.sqrt(v_hat) + ADAM_EPS) + ADAM_WD * w_ref[...])
        nm_ref[...] = nm
        nv_ref[...] = nv

    spec = pl.BlockSpec((tr, tc), lambda i, j: (i, j))
    outs = pl.pallas_call(
        body, name=name, grid=(R // tr, C // tc),
        in_specs=[spec] * 4, out_specs=[spec] * (3 + copy_g),
        out_shape=[jax.ShapeDtypeStruct((R, C), F32)] * (3 + copy_g),
        compiler_params=_params("parallel", "parallel"),
    )(*[t.reshape(R, C) for t in (w, g, m, v)])
    return [t.reshape(shape) for t in outs]


def add_sibling(grad, recv, core, name):
    _, _, Rh, C = grad.shape
    tr = _tile(Rh, 512)

    def body(core_ref, g_ref, r_ref, o_ref):
        o_ref[...] = (g_ref[...].astype(F32) + r_ref[...].astype(F32)).astype(BF16)

    return pl.pallas_call(
        body, name=name,
        grid_spec=pltpu.PrefetchScalarGridSpec(
            num_scalar_prefetch=1, grid=(N_CHIPS, Rh // tr),
            in_specs=[pl.BlockSpec((None, None, tr, C), lambda j, i, core_ref: (j, core_ref[0], i, 0)),
                      pl.BlockSpec((None, tr, C), lambda j, i, core_ref: (j, i, 0))],
            out_specs=pl.BlockSpec((None, tr, C), lambda j, i, core_ref: (j, i, 0))),
        out_shape=jax.ShapeDtypeStruct((N_CHIPS, Rh, C), BF16),
        compiler_params=_params("parallel", "parallel"),
    )(core, grad, recv)


def add_chips(part, recv, chip, core, group, n, mi, name):
    _, Rh, C = part.shape
    tr = _tile(Rh, 512)

    def body(chip_ref, core_ref, p_ref, r_ref, *rest):
        o_ref = rest[-1]
        acc = p_ref[...].astype(F32)
        for k in range(N_CHIPS - 1):
            acc = acc + r_ref[k].astype(F32)
        o_ref[...] = acc

    carried = [] if group is None else [group]
    return pl.pallas_call(
        body, name=name,
        grid_spec=pltpu.PrefetchScalarGridSpec(
            num_scalar_prefetch=2, grid=(Rh // tr,),
            in_specs=[pl.BlockSpec((None, tr, C), lambda i, chip_ref, core_ref: (chip_ref[0], i, 0)),
                      pl.BlockSpec((N_CHIPS - 1, tr, C), lambda i, chip_ref, core_ref: (0, i, 0))]
            + [ANY] * len(carried),
            out_specs=pl.BlockSpec((None, None, tr, C), lambda i, chip_ref, core_ref: (mi, core_ref[0], i, 0))),
        out_shape=jax.ShapeDtypeStruct((n, 2, Rh, C), F32),
        input_output_aliases={4: 0} if carried else {},
        compiler_params=_params("parallel"),
    )(chip, core, part, recv, *carried)


def stage_shard(shard, mi, chip, name, dep=None):
    _, _, Rh, C = shard.shape
    tr = _tile(Rh, 512)

    def body(chip_ref, s_ref, *rest):
        rest[-1][...] = s_ref[...].astype(BF16)

    return pl.pallas_call(
        body, name=name,
        grid_spec=pltpu.PrefetchScalarGridSpec(
            num_scalar_prefetch=1, grid=(2, Rh // tr),
            in_specs=[pl.BlockSpec((None, None, tr, C), lambda h, i, chip_ref: (mi, h, i, 0))] + _dep_specs(dep),
            out_specs=pl.BlockSpec((None, None, tr, C), lambda h, i, chip_ref: (chip_ref[0], h, i, 0))),
        out_shape=jax.ShapeDtypeStruct((N_CHIPS, 2, Rh, C), BF16),
        compiler_params=_params("parallel", "parallel"),
    )(chip, shard, *_dep_args(dep))


def sum_devices(gathered):
    _, R, C = gathered.shape

    def body(g_ref, o_ref):
        acc = g_ref[0]
        for d in range(1, N_DEV):
            acc = acc + g_ref[d]
        o_ref[...] = acc

    return pl.pallas_call(
        body, name="sum_devices", grid=(1,),
        in_specs=[pl.BlockSpec((N_DEV, R, C), lambda i: (0, 0, 0))],
        out_specs=pl.BlockSpec((R, C), lambda i: (0, 0)),
        out_shape=jax.ShapeDtypeStruct((R, C), F32),
        compiler_params=_params("arbitrary"),
    )(gathered)


def _place():
    x, y, c = lax.axis_index("x"), lax.axis_index("y"), lax.axis_index("c")
    others = [(1 - x, y), (x, 1 - y), (1 - x, 1 - y)]
    return x, y, c, others


def _remote(src, dst, send_sems, recv_sems, idx, device):
    return pltpu.make_async_remote_copy(src_ref=src, dst_ref=dst, send_sem=send_sems.at[idx],
                                        recv_sem=recv_sems.at[idx], device_id=device, device_id_type=MESH)


HBM = pl.BlockSpec(memory_space=pltpu.HBM)
SEM = pl.BlockSpec(memory_space=pltpu.SEMAPHORE)
EFFECT = pltpu.SideEffectType.DATAFLOW_SIDE_EFFECTING


def _in_hbm(t):
    return pltpu.with_memory_space_constraint(t, pltpu.HBM)


def _own_slice(buf, me, c):
    return buf.at[me, c] if len(buf.shape) == 4 else buf.at[me]


def gather_start(staged, bucket_sizes, name):
    n, nb = len(staged), len(bucket_sizes)

    def body(*refs):
        in_refs, sems, token = refs[:n], refs[n:n + 2 * nb], refs[-1]
        x, y, c, others = _place()
        me = 2 * x + y
        t = 0
        for b, size in enumerate(bucket_sizes):
            for i in range(size):
                mine = _own_slice(in_refs[t], me, c)
                for k, chip in enumerate(others):
                    _remote(mine, mine, sems[2 * b], sems[2 * b + 1], 3 * i + k, (*chip, c)).start()
                t += 1
        token[...] = jnp.zeros_like(token)

    sem_shapes = [pltpu.SemaphoreType.DMA((3 * size,)) for size in bucket_sizes for _ in range(2)]
    outs = pl.pallas_call(
        body, name=name,
        out_shape=sem_shapes + [pltpu.HBM(s.shape, s.dtype) for s in staged] + [jax.ShapeDtypeStruct((8, 128), F32)],
        in_specs=[HBM] * n, out_specs=[SEM] * (2 * nb) + [HBM] * n + [pl.BlockSpec(memory_space=pltpu.VMEM)],
        input_output_aliases={t: 2 * nb + t for t in range(n)},
        compiler_params=pltpu.CompilerParams(has_side_effects=EFFECT),
    )(*[_in_hbm(s) for s in staged])
    sems = [(outs[2 * b], outs[2 * b + 1]) for b in range(nb)]
    return sems, list(outs[2 * nb:2 * nb + n]), outs[-1]


def gather_wait(bufs, sems, after, name):
    n = len(bufs)

    def body(*refs):
        in_refs, send_sems, recv_sems = refs[:n], refs[n], refs[n + 1]
        x, y, c, others = _place()
        me = 2 * x + y
        for i in range(n):
            mine = _own_slice(in_refs[i], me, c)
            for k, (ox, oy) in enumerate(others):
                cp = _remote(mine, _own_slice(in_refs[i], 2 * ox + oy, c), send_sems, recv_sems, 3 * i + k,
                             (ox, oy, c))
                cp.wait_send()
                cp.wait_recv()

    return pl.pallas_call(
        body, name=name, out_shape=[pltpu.HBM(b.shape, b.dtype) for b in bufs],
        in_specs=[HBM] * n + [SEM, SEM, ANY], out_specs=[HBM] * n,
        input_output_aliases={t: t for t in range(n)},
        compiler_params=pltpu.CompilerParams(has_side_effects=EFFECT),
    )(*bufs, *sems, after)


def forward_to_sibling(bufs, name):
    n = len(bufs)

    def body(*refs):
        out_refs, (send_sems, recv_sems) = refs[n:2 * n], refs[2 * n:]
        x, y, c, others = _place()
        sibling = (x, y, 1 - c)
        copies = []
        for t in range(n):
            for k, (ox, oy) in enumerate(others):
                mine = out_refs[t].at[2 * ox + oy, c]
                cp = _remote(mine, mine, send_sems, recv_sems, 3 * t + k, sibling)
                cp.start()
                copies.append(cp)
        for t in range(n):
            for k, (ox, oy) in enumerate(others):
                theirs = out_refs[t].at[2 * ox + oy, 1 - c]
                _remote(theirs, theirs, send_sems, recv_sems, 3 * t + k, sibling).wait_recv()
        for cp in copies:
            cp.wait_send()

    return pl.pallas_call(
        body, name=name, in_specs=[ANY] * n, out_specs=[ANY] * n,
        out_shape=[jax.ShapeDtypeStruct(b.shape, b.dtype) for b in bufs],
        input_output_aliases={t: t for t in range(n)},
        scratch_shapes=[pltpu.SemaphoreType.DMA((3 * n,)), pltpu.SemaphoreType.DMA((3 * n,))],
    )(*bufs)


def sibling_start(grads, name):
    n = len(grads)
    lands = [lax.empty((N_CHIPS,) + g.shape[2:], g.dtype) for g in grads]

    def body(*refs):
        in_refs, land_refs, send_sems, recv_sems, token = refs[:n], refs[n:2 * n], refs[2 * n], refs[2 * n + 1], refs[-1]
        x, y, c, _ = _place()
        for t in range(n):
            for j in range(N_CHIPS):
                _remote(in_refs[t].at[j, 1 - c], land_refs[t].at[j], send_sems, recv_sems, N_CHIPS * t + j,
                        (x, y, 1 - c)).start()
        token[...] = jnp.zeros_like(token)

    outs = pl.pallas_call(
        body, name=name,
        out_shape=[pltpu.SemaphoreType.DMA((N_CHIPS * n,))] * 2 + [pltpu.HBM(t.shape, t.dtype) for t in grads + lands]
        + [jax.ShapeDtypeStruct((8, 128), F32)],
        in_specs=[HBM] * (2 * n), out_specs=[SEM, SEM] + [HBM] * (2 * n) + [pl.BlockSpec(memory_space=pltpu.VMEM)],
        input_output_aliases={t: 2 + t for t in range(2 * n)},
        compiler_params=pltpu.CompilerParams(has_side_effects=EFFECT),
    )(*[_in_hbm(t) for t in grads + lands])
    return (outs[0], outs[1]), list(outs[2:2 + n]), list(outs[2 + n:2 + 2 * n]), outs[-1]


def sibling_wait(grads, lands, sems, after, name):
    n = len(grads)

    def body(*refs):
        in_refs, land_refs, send_sems, recv_sems = refs[:n], refs[n:2 * n], refs[2 * n], refs[2 * n + 1]
        x, y, c, _ = _place()
        for t in range(n):
            for j in range(N_CHIPS):
                cp = _remote(in_refs[t].at[j, 1 - c], land_refs[t].at[j], send_sems, recv_sems, N_CHIPS * t + j,
                             (x, y, 1 - c))
                cp.wait_send()
                cp.wait_recv()

    outs = pl.pallas_call(
        body, name=name, out_shape=[pltpu.HBM(t.shape, t.dtype) for t in grads + lands],
        in_specs=[HBM] * (2 * n) + [SEM, SEM, ANY], out_specs=[HBM] * (2 * n),
        input_output_aliases={t: t for t in range(2 * n)},
        compiler_params=pltpu.CompilerParams(has_side_effects=EFFECT),
    )(*grads, *lands, *sems, after)
    return list(outs[:n]), list(outs[n:])


def reduce_start(parts, name):
    n = len(parts)
    lands = [lax.empty((N_CHIPS - 1,) + p.shape[1:], p.dtype) for p in parts]

    def body(*refs):
        in_refs, land_refs, send_sems, recv_sems, token = refs[:n], refs[n:2 * n], refs[2 * n], refs[2 * n + 1], refs[-1]
        x, y, c, others = _place()
        for t in range(n):
            for k, (ox, oy) in enumerate(others):
                _remote(in_refs[t].at[2 * ox + oy], land_refs[t].at[k], send_sems, recv_sems, 3 * t + k,
                        (ox, oy, c)).start()
        token[...] = jnp.zeros_like(token)

    outs = pl.pallas_call(
        body, name=name,
        out_shape=[pltpu.SemaphoreType.DMA((3 * n,))] * 2 + [pltpu.HBM(t.shape, t.dtype) for t in parts + lands]
        + [jax.ShapeDtypeStruct((8, 128), F32)],
        in_specs=[HBM] * (2 * n), out_specs=[SEM, SEM] + [HBM] * (2 * n) + [pl.BlockSpec(memory_space=pltpu.VMEM)],
        input_output_aliases={t: 2 + t for t in range(2 * n)},
        compiler_params=pltpu.CompilerParams(has_side_effects=EFFECT),
    )(*[_in_hbm(t) for t in parts + lands])
    return (outs[0], outs[1]), list(outs[2:2 + n]), list(outs[2 + n:2 + 2 * n]), outs[-1]


def reduce_wait(parts, lands, sems, after, name):
    n = len(parts)

    def body(*refs):
        in_refs, land_refs, send_sems, recv_sems = refs[:n], refs[n:2 * n], refs[2 * n], refs[2 * n + 1]
        x, y, c, others = _place()
        for t in range(n):
            for k, (ox, oy) in enumerate(others):
                cp = _remote(in_refs[t].at[2 * ox + oy], land_refs[t].at[k], send_sems, recv_sems, 3 * t + k,
                             (ox, oy, c))
                cp.wait_send()
                cp.wait_recv()

    outs = pl.pallas_call(
        body, name=name, out_shape=[pltpu.HBM(t.shape, t.dtype) for t in parts + lands],
        in_specs=[HBM] * (2 * n) + [SEM, SEM] + _dep_specs(after), out_specs=[HBM] * (2 * n),
        input_output_aliases={t: t for t in range(2 * n)},
        compiler_params=pltpu.CompilerParams(has_side_effects=EFFECT),
    )(*parts, *lands, *sems, *_dep_args(after))
    return list(outs[:n]), list(outs[n:])


def exchange_halves(groups, name):
    n_groups = len(groups)
    slots = [(gi, mi) for gi, grp in enumerate(groups) for mi in range(grp.shape[0])]

    def body(*refs):
        out_refs = refs[n_groups:2 * n_groups]
        send_sems, recv_sems = refs[2 * n_groups:]
        x, y, c, _ = _place()
        sibling = (x, y, 1 - c)
        copies = []
        for t, (gi, mi) in enumerate(slots):
            mine = out_refs[gi].at[mi, c]
            cp = _remote(mine, mine, send_sems, recv_sems, t, sibling)
            cp.start()
            copies.append(cp)
        for t, (gi, mi) in enumerate(slots):
            theirs = out_refs[gi].at[mi, 1 - c]
            _remote(theirs, theirs, send_sems, recv_sems, t, sibling).wait_recv()
        for cp in copies:
            cp.wait_send()

    return pl.pallas_call(
        body, name=name, in_specs=[ANY] * n_groups, out_specs=[ANY] * n_groups,
        out_shape=[jax.ShapeDtypeStruct(g.shape, g.dtype) for g in groups],
        input_output_aliases={gi: gi for gi in range(n_groups)},
        scratch_shapes=[pltpu.SemaphoreType.DMA((len(slots),)), pltpu.SemaphoreType.DMA((len(slots),))],
    )(*groups)


def gather_devices(buf):
    def body(in_ref, out_ref, send_sems, recv_sems, local_sem):
        x, y, c, _ = _place()
        me = 4 * x + 2 * y + c
        local = pltpu.make_async_copy(in_ref, out_ref.at[me], local_sem)
        local.start()
        copies = []
        for k in range(1, N_DEV):
            fx, fy, fc = (k >> 2) & 1, (k >> 1) & 1, k & 1
            peer = (x ^ fx, y ^ fy, c ^ fc)
            cp = _remote(in_ref, out_ref.at[me], send_sems, recv_sems, k - 1, peer)
            cp.start()
            copies.append(cp)
        for k in range(1, N_DEV):
            fx, fy, fc = (k >> 2) & 1, (k >> 1) & 1, k & 1
            theirs = out_ref.at[4 * (x ^ fx) + 2 * (y ^ fy) + (c ^ fc)]
            _remote(theirs, theirs, send_sems, recv_sems, k - 1, (x, y, c)).wait_recv()
        for cp in copies:
            cp.wait_send()
        local.wait()

    return pl.pallas_call(
        body, name="gather_devices", in_specs=[ANY], out_specs=ANY,
        out_shape=jax.ShapeDtypeStruct((N_DEV,) + buf.shape, buf.dtype),
        scratch_shapes=[pltpu.SemaphoreType.DMA((N_DEV - 1,)), pltpu.SemaphoreType.DMA((N_DEV - 1,)),
                        pltpu.SemaphoreType.DMA],
    )(buf)


class GradReducer:
    def __init__(self, core, chip, kinds):
        self.core, self.chip = core, chip
        self.sizes = dict(kinds)
        self.groups = {kind: None for kind, _ in kinds}

    def send(self, grad, kind, mi, tag):
        array = grad.reshape(N_CHIPS, 2, -1, grad.shape[-1])
        sems, arrays, lands, token = sibling_start([array], f"reduce_sibling_start_{tag}")
        return (sems, arrays, lands, kind, mi, tag), token

    def begin(self, sent, after, tag):
        parts, slots = [], []
        for sems, arrays, lands, kind, mi, sent_tag in sent:
            arrays, lands = sibling_wait(arrays, lands, sems, after, f"reduce_sibling_wait_{sent_tag}")
            parts.append(add_sibling(arrays[0], lands[0], self.core, f"reduce_add_sibling_{sent_tag}"))
            slots.append((kind, mi))
        sems, parts, lands, token = reduce_start(parts, f"reduce_start_{tag}")
        return (sems, parts, lands, slots, tag), token

    def end(self, state, after):
        sems, parts, lands, slots, tag = state
        parts, lands = reduce_wait(parts, lands, sems, after, f"reduce_wait_{tag}")
        for t, (kind, mi) in enumerate(slots):
            self.groups[kind] = add_chips(parts[t], lands[t], self.chip, self.core, self.groups[kind],
                                          self.sizes[kind], mi, f"reduce_add_chips_{tag}_{t}")

    def finish(self):
        kinds = list(self.groups)
        return dict(zip(kinds, exchange_halves([self.groups[k] for k in kinds], "reduce_swap")))


def _ffn_fwd(x, gain, wg, wu, wd, tag):
    h, rstd = rmsnorm_fwd(x, gain, BF16, f"ffn_norm_{tag}")
    gate, up = ffn_gateup(h, wg, wu, f"ffn_gateup_{tag}")
    out = mm_residual([gate, up], wd, x, 0.5, wd.shape[0] // N_CHIPS, f"ffn_down_{tag}", tm_target=704)
    return out, (x, gain, h, rstd, gate, up)


def _ffn_bwd(dout, saved, wg, wu, wd, index, reducer, dep=None, per_tensor=False):
    x, gain, h, rstd, gate, up = saved
    D = x.shape[1]
    Fs = wg.shape[2]
    td = _tile(D, 512, 128)
    tag = f"ffn{index}"
    begun = []

    def begin(sent, after, suffix):
        state, token = reducer.begin(sent, after, tag + suffix)
        begun.append(state)
        return token

    dgate, dup, act = ffn_bwd_act(dout, wd, gate, up, f"ffn_bwd_act_{index}", dep=dep)
    d_wd = mm_tn(act, dout, Fs, td, f"ffn_bwd_wd_{index}", b_scale=0.5)
    sent_d, tok = reducer.send(d_wd, "down", index, tag + "d")
    d_wg = mm_tn(h, dgate, td, Fs, f"ffn_bwd_wg_{index}", stacked_out=True, dep=tok)
    toks = [begin([sent_d], d_wg, "d")] if per_tensor else []
    sent_g, tok = reducer.send(d_wg, "gate", index, tag + "g")
    d_wu = mm_tn(h, dup, td, Fs, f"ffn_bwd_wu_{index}", stacked_out=True, dep=toks + [tok])
    toks = [begin([sent_g], d_wu, "g")] if per_tensor else []
    sent_u, tok = reducer.send(d_wu, "up", index, tag + "u")
    dh = mm_nt([(dgate, wg), (dup, wu)], _tile(x.shape[0], 704), D, Fs, f"ffn_bwd_dh_{index}", stacked_w=True,
               dep=toks + [tok])
    tok = begin([sent_u] if per_tensor else [sent_d, sent_g, sent_u], dh, "u")
    dx, dgain = rmsnorm_bwd(dh, x, gain, rstd, dout, f"ffn_norm_bwd_{index}")
    return dx, dgain, begun, tok


def kernel(x, meta, ffn_norm, ffn_w_gate, ffn_w_up, ffn_w_down, gla_norm, gla_w_in, gla_w_lr, gla_b_lr, gla_head_norm, gla_w_out, pool_norm, pool_w, pool_b, pool_scale, final_norm, loss_target, m_meta, m_ffn_norm, m_ffn_w_gate, m_ffn_w_up, m_ffn_w_down, m_gla_norm, m_gla_w_in, m_gla_w_lr, m_gla_b_lr, m_gla_head_norm, m_gla_w_out, m_pool_norm, m_pool_w, m_pool_b, m_pool_scale, m_final_norm, v_meta, v_ffn_norm, v_ffn_w_gate, v_ffn_w_up, v_ffn_w_down, v_gla_norm, v_gla_w_in, v_gla_w_lr, v_gla_b_lr, v_gla_head_norm, v_gla_w_out, v_pool_norm, v_pool_w, v_pool_b, v_pool_scale, v_final_norm):
    S, D = x.shape[1], x.shape[2]
    M = OFF + S
    Dq = D // N_CHIPS
    Fs = ffn_w_gate.shape[3]
    F = N_CHIPS * Fs
    dk = D // 2
    n_in = gla_w_in.shape[2]
    W = D // 4
    core = lax.axis_index("c").astype(jnp.int32).reshape(1)
    chip_id = 2 * lax.axis_index("x") + lax.axis_index("y")
    chip = chip_id.astype(jnp.int32).reshape(1)

    small = jnp.concatenate([_pad_rows(t) for t in (
        meta, ffn_norm.reshape(4, Dq), gla_w_lr.reshape(8, Dq), pool_norm, pool_b.reshape(1, Dq), pool_scale)],
        axis=0)
    def stage(w, kind, n, mi, dep=None):
        return stage_shard(w.reshape(n, 2, -1, w.shape[-1]), mi, chip, f"stage_{kind}_{mi}", dep=dep)

    ffn_stage = lambda mi, dep=None: [stage(ffn_w_gate, "gate", 4, mi, dep), stage(ffn_w_up, "up", 4, mi, dep),
                                      stage(ffn_w_down, "down", 4, mi, dep)]
    small_stage = lax.dynamic_update_slice(jnp.zeros((N_CHIPS,) + small.shape, F32), small[None], (chip_id, 0, 0))
    first = ffn_stage(0)
    buckets = [first[:2] + [small_stage], first[2:]]
    sizes = [len(b) for b in buckets]
    gather_sems, in_flight, tok = gather_start([t for b in buckets for t in b], sizes, "gather_start_first")
    buckets = [[stage(gla_w_in, "win", 1, 0, tok), stage(gla_w_out, "wout", 1, 0, tok)],
               ffn_stage(1, tok), ffn_stage(2, tok), [stage(pool_w, "pool", 1, 0, tok)] + ffn_stage(3, tok)]
    more_sems, more_in_flight, gather_token = gather_start([t for b in buckets for t in b],
                                                            [len(b) for b in buckets], "gather_start_rest")
    sizes += [len(b) for b in buckets]
    gather_sems += more_sems
    in_flight += more_in_flight
    starts = [sum(sizes[:b]) for b in range(len(sizes))]

    def arrive(b, after, n_big):
        bufs = gather_wait(in_flight[starts[b]:starts[b] + sizes[b]], gather_sems[b], after, f"gather_wait_{b}")
        return forward_to_sibling(bufs[:n_big], f"gather_forward_{b}") + bufs[n_big:]

    ffn_w = lambda t: (t[0].reshape(N_CHIPS, D, Fs), t[1].reshape(N_CHIPS, D, Fs), t[2].reshape(F, D))
    got = arrive(0, gather_token, 2)
    wg, wu, wd = [None] * 4, [None] * 4, [None] * 4
    wg[0], wu[0] = got[0].reshape(N_CHIPS, D, Fs), got[1].reshape(N_CHIPS, D, Fs)
    sm = got[2]
    unshard = lambda t: t.transpose(1, 0, 2).reshape(t.shape[1], D)
    meta_f = unshard(sm[:, 0:16])
    ffn_norm_f = unshard(sm[:, 16:20])
    w_lr_f = sm[:, 24:32].reshape(N_CHIPS, GATE_RANK, dk // N_CHIPS).transpose(1, 0, 2).reshape(GATE_RANK, dk)
    pool_norm_f = sm[:, 32].reshape(1, D)
    pool_b_f = sm[:, 40].reshape(N_CHIPS, 4, W // N_CHIPS).transpose(1, 0, 2).reshape(1, D)
    pool_scale_f = sm[:, 48].reshape(1, D)
    wlr_pad = jnp.pad(w_lr_f.astype(BF16), ((0, LR_W - GATE_RANK), (0, 0)))
    final_g = final_norm.reshape(1, D)
    qkv = 2 * dk + D

    x0 = jnp.concatenate([jnp.zeros((PAD, D), F32), meta_f, x[0]], axis=0)
    target = jnp.pad(loss_target[0], ((OFF, 0), (0, 0)))
    h0, rstd0 = rmsnorm_fwd(x0, ffn_norm_f[0:1], BF16, "ffn_norm_0")
    gate0, up0 = ffn_gateup(h0, wg[0], wu[0], "ffn_gateup_0")
    wd[0] = arrive(1, gate0, 1)[0].reshape(F, D)
    x1 = mm_residual([gate0, up0], wd[0], x0, 0.5, Fs, "ffn_down_0", tm_target=704)
    ffn0 = (x0, ffn_norm_f[0:1], h0, rstd0, gate0, up0)
    got = arrive(2, x1, 2)
    w_in = got[0].reshape(N_CHIPS, D, n_in).transpose(1, 0, 2).reshape(D, N_CHIPS * n_in)
    w_out = got[1].reshape(D, D)
    w_all = jnp.concatenate([w_in[:, :qkv], w_in[:, qkv + GATE_RANK:], w_in[:, qkv:qkv + GATE_RANK],
                             jnp.zeros((D, LR_W - GATE_RANK), BF16)], axis=1)
    hg, rstd_g = rmsnorm_fwd(x1, gla_norm, BF16, "gla_norm")
    proj = mm_nn(hg, w_all, F32, "gla_proj")
    o, st = gla_fwd(proj, wlr_pad, gla_b_lr, D)
    gated = gla_post_fwd(o, proj, gla_head_norm, D)
    x2 = mm_residual([gated], w_out, x1, 1.0, D, "gla_out")
    wg[1], wu[1], wd[1] = ffn_w(arrive(3, x2, 3))
    x3, ffn1 = _ffn_fwd(x2, ffn_norm_f[1:2], wg[1], wu[1], wd[1], "1")
    wg[2], wu[2], wd[2] = ffn_w(arrive(4, x3, 3))
    x4, ffn2 = _ffn_fwd(x3, ffn_norm_f[2:3], wg[2], wu[2], wd[2], "2")
    got = arrive(5, x4, 4)
    w_pool = got[0].reshape(N_CHIPS, 4, W // N_CHIPS, W).transpose(1, 0, 2, 3).reshape(4, W, W)
    wg[3], wu[3], wd[3] = ffn_w(got[1:])
    hp, rstd_p = rmsnorm_fwd(x4, pool_norm_f, F32, "pool_norm")
    pooled = pool_window(hp)
    x5 = pool_mix(pooled, x4, w_pool, pool_b_f, pool_scale_f)
    x6, ffn3 = _ffn_fwd(x5, ffn_norm_f[3:4], wg[3], wu[3], wd[3], "3")
    loss, dx6, d_final = final_loss(x6, final_g, target)

    reducer = GradReducer(core, chip, [("gate", 4), ("up", 4), ("down", 4), ("win", 1), ("wout", 1), ("pool", 1)])

    def settle(begun, after):
        for state in begun:
            reducer.end(state, after)

    dx5, dn3, red3, tok = _ffn_bwd(dx6, ffn3, wg[3], wu[3], wd[3], 3, reducer)
    dpooled, d_wpool, d_pool_b, d_pool_scale = pool_mix_bwd(dx5, pooled, w_pool, pool_b_f, pool_scale_f, dep=tok)
    dhp = pool_window_bwd(dpooled)
    dx4, d_pool_norm = rmsnorm_bwd(dhp, x4, pool_norm_f, rstd_p, dx5, "pool_norm_bwd")
    d_wpool = d_wpool.reshape(4, N_CHIPS, W // N_CHIPS, W).transpose(1, 0, 2, 3)
    sent_p, tok = reducer.send(d_wpool, "pool", 0, "pool")
    dx3, dn2, red2, tok = _ffn_bwd(dx4, ffn2, wg[2], wu[2], wd[2], 2, reducer, dep=tok)
    redp, tok_p = reducer.begin([sent_p], dx3, "pool")
    dx2, dn1, red1, tok = _ffn_bwd(dx3, ffn1, wg[1], wu[1], wd[1], 1, reducer, dep=[tok, tok_p])
    tm = _tile(M, 352)
    td = _tile(D, 512, 128)
    d_wout = mm_tn(gated, dx2, td, td, "gla_out_bwd_w", dep=tok)
    sent_o, tok = reducer.send(d_wout, "wout", 0, "wout")
    dgated = mm_nt([(dx2, w_out)], tm, td, D, "gla_out_bwd_act", dep=tok)
    redo, tok_o = reducer.begin([sent_o], dgated, "wout")
    do, dr, d_head_norm = gla_post_bwd(dgated, o, proj, gla_head_norm, D)
    dq, dkk, dv, dlr, dwlr, dblr = gla_bwd(proj, wlr_pad, gla_b_lr, st, do, D)
    dproj = jnp.concatenate([dq, dkk, dv, dr, dlr.astype(BF16)], axis=1)
    tp = _tile(proj.shape[1], 896, 128)
    d_wall = mm_tn(hg, dproj, td, tp, "gla_proj_bwd_w", dep=tok_o)
    d_win = jnp.concatenate([d_wall[:, :qkv], d_wall[:, qkv + D:qkv + D + GATE_RANK], d_wall[:, qkv:qkv + D]], axis=1)
    d_win = d_win.reshape(D, N_CHIPS, n_in).transpose(1, 0, 2)
    sent_i, tok = reducer.send(d_win, "win", 0, "win")
    dhg = mm_nt([(dproj, w_all)], tm, D, tp, "gla_proj_bwd_act", dep=tok)
    redi, tok = reducer.begin([sent_i], dhg, "win")
    dx1, d_gla_norm = rmsnorm_bwd(dhg, x1, gla_norm, rstd_g, dx2, "gla_norm_bwd")
    dx0, dn0, red0, tok = _ffn_bwd(dx1, ffn0, wg[0], wu[0], wd[0], 0, reducer, dep=tok, per_tensor=True)
    settle(red3 + [redp] + red2 + red1 + [redo, redi] + red0[:-1], tok)

    d_wlr = dwlr[:, :GATE_RANK].transpose(1, 0, 2).reshape(GATE_RANK, dk)
    pieces = [dx0[PAD:OFF], dn0, dn1, dn2, dn3, d_gla_norm, d_wlr,
              dblr.reshape(1, dk), d_head_norm, d_pool_norm, d_pool_b, d_pool_scale, d_final]
    packed = jnp.concatenate([_pad_rows(p.reshape(-1, Dq)) for p in pieces], axis=0)
    total = sum_devices(gather_devices(packed))

    settle(red0[-1:], [total] + list(reducer.groups.values()))
    reduced = reducer.finish()
    g_gate = reduced["gate"].reshape(ffn_w_gate.shape)
    g_up = reduced["up"].reshape(ffn_w_up.shape)
    g_down = reduced["down"].reshape(ffn_w_down.shape)
    g_win = reduced["win"].reshape(gla_w_in.shape)
    g_wout = reduced["wout"].reshape(gla_w_out.shape)
    g_wpool = reduced["pool"].reshape(pool_w.shape)
    sums, at = [], 0
    for p in pieces:
        r = p.size // Dq
        sums.append(total[at:at + r].reshape(p.shape))
        at += r + (-r % 8)
    (s_meta, s_n0, s_n1, s_n2, s_n3, s_gla_norm, s_wlr, s_blr, s_head_norm, s_pool_norm, s_pool_b, s_pool_scale,
     s_final) = sums
    s_ffn_norm = jnp.stack([s_n0, s_n1, s_n2, s_n3], axis=0)[:, 0]
    mine = lambda t, width: lax.dynamic_slice_in_dim(t, chip_id * width, width, axis=t.ndim - 1)
    g_meta = mine(s_meta, Dq)
    g_ffn_norm = mine(s_ffn_norm, Dq).reshape(ffn_norm.shape)
    g_gla_norm = s_gla_norm
    g_wlr = mine(s_wlr, dk // N_CHIPS).reshape(gla_w_lr.shape)
    g_blr = s_blr
    g_head_norm = s_head_norm
    g_pool_norm = mine(s_pool_norm, Dq)
    g_pool_b = mine(s_pool_b.reshape(4, W), W // N_CHIPS).reshape(pool_b.shape)
    g_pool_scale = mine(s_pool_scale, Dq)
    g_final = s_final.reshape(final_norm.shape)

    weights = [meta, ffn_norm, ffn_w_gate, ffn_w_up, ffn_w_down, gla_norm, gla_w_in, gla_w_lr, gla_b_lr,
               gla_head_norm, gla_w_out, pool_norm, pool_w, pool_b, pool_scale, final_norm]
    moments_m = [m_meta, m_ffn_norm, m_ffn_w_gate, m_ffn_w_up, m_ffn_w_down, m_gla_norm, m_gla_w_in, m_gla_w_lr,
                 m_gla_b_lr, m_gla_head_norm, m_gla_w_out, m_pool_norm, m_pool_w, m_pool_b, m_pool_scale,
                 m_final_norm]
    moments_v = [v_meta, v_ffn_norm, v_ffn_w_gate, v_ffn_w_up, v_ffn_w_down, v_gla_norm, v_gla_w_in, v_gla_w_lr,
                 v_gla_b_lr, v_gla_head_norm, v_gla_w_out, v_pool_norm, v_pool_w, v_pool_b, v_pool_scale,
                 v_final_norm]
    grads_w = [g_meta, g_ffn_norm, g_gate, g_up, g_down, g_gla_norm, g_win, g_wlr, g_blr, g_head_norm, g_wout,
               g_pool_norm, g_wpool, g_pool_b, g_pool_scale, g_final]
    from_swap = {2, 3, 4, 6, 10, 12}
    deltas, new_m, new_v = [], [], []
    for i, (w, g, m, v) in enumerate(zip(weights, grads_w, moments_m, moments_v)):
        outs = adamw(w, g, m, v, f"adamw_{i}", copy_g=i in from_swap)
        deltas.append(outs[0])
        new_m.append(outs[1])
        new_v.append(outs[2])
        if i in from_swap:
            grads_w[i] = outs[3]

    loss = lax.psum(loss[0, 0], ("x", "y", "c"))
    grad_x = dx0[OFF:][None]
    return (loss, grad_x, *grads_w, *deltas, *new_m, *new_v)
```

```python
import functools

import jax
import jax.numpy as jnp
from jax import lax
from jax.experimental import pallas as pl
from jax.experimental.pallas import tpu as pltpu

F32 = jnp.float32
BF16 = jnp.bfloat16
MESH = pl.DeviceIdType.MESH
ANY = pl.BlockSpec(memory_space=pl.ANY)

N_META = 16
CHUNK = 64
PAD = CHUNK - N_META
OFF = PAD + N_META
EPS = 1e-6
HEADS = 4
GATE_RANK = 16
GATE_NORM = 16.0
LR_W = 128
N_CHIPS = 4
N_DEV = 8
ADAM_LR, ADAM_B1, ADAM_B2, ADAM_EPS, ADAM_WD, ADAM_STEP = 0.001, 0.9, 0.999, 1e-08, 0.01, 10
VMEM_LIMIT = 56 * 1024 * 1024
ROW_TILE = 176
ONE_BUFFER = pl.Buffered(1)


def _tile(n, target, mult=16):
    best = None
    for d in range(mult, min(n, target) + 1, mult):
        if n % d == 0:
            best = d
    return best if best is not None else n


def _params(*sem):
    return pltpu.CompilerParams(dimension_semantics=sem, vmem_limit_bytes=VMEM_LIMIT)


def _dot(a, b):
    return jnp.dot(a, b, preferred_element_type=F32)


def _dot_nt(a, b):
    return lax.dot_general(a, b, (((1,), (1,)), ((), ())), preferred_element_type=F32)


def _dot_tn(a, b):
    return lax.dot_general(a, b, (((0,), (0,)), ((), ())), preferred_element_type=F32)


MXU_WIDTH = 256


def _chunks(n):
    return [slice(lo, min(lo + MXU_WIDTH, n)) for lo in range(0, n, MXU_WIDTH)]


def _sigmoid(x):
    return 1.0 / (1.0 + jnp.exp(-x))


def _rows(tile, width=1):
    return lax.broadcasted_iota(jnp.int32, (tile, width), 0)


def _dep_args(dep):
    if dep is None:
        return []
    return list(dep) if isinstance(dep, (list, tuple)) else [dep]


def _dep_specs(dep):
    return [ANY] * len(_dep_args(dep))


def _pad_rows(t):
    return jnp.pad(t, ((0, -t.shape[0] % 8), (0, 0)))


def rmsnorm_fwd(x, g, out_dtype, name):
    M, D = x.shape
    tr = _tile(M, ROW_TILE)

    def body(x_ref, g_ref, h_ref, r_ref):
        xv = x_ref[...]
        r = lax.rsqrt(jnp.mean(xv * xv, axis=-1, keepdims=True) + EPS)
        h_ref[...] = (xv * r * g_ref[...]).astype(out_dtype)
        r_ref[...] = r

    return pl.pallas_call(
        body, name=name, grid=(M // tr,),
        in_specs=[pl.BlockSpec((tr, D), lambda i: (i, 0)), pl.BlockSpec((1, D), lambda i: (0, 0))],
        out_specs=[pl.BlockSpec((tr, D), lambda i: (i, 0)), pl.BlockSpec((tr, 1), lambda i: (i, 0))],
        out_shape=[jax.ShapeDtypeStruct((M, D), out_dtype), jax.ShapeDtypeStruct((M, 1), F32)],
        compiler_params=_params("parallel"),
    )(x, g)


def rmsnorm_bwd(dh, x, g, rstd, dres, name):
    M, D = x.shape
    tr = _tile(M, ROW_TILE)

    def body(dh_ref, x_ref, g_ref, r_ref, dres_ref, dx_ref, dg_ref):
        @pl.when(pl.program_id(0) == 0)
        def _():
            dg_ref[...] = jnp.zeros_like(dg_ref)

        r = r_ref[...]
        xhat = x_ref[...] * r
        dhv = dh_ref[...]
        gd = dhv * g_ref[...]
        dx_ref[...] = dres_ref[...] + r * (gd - xhat * jnp.mean(gd * xhat, axis=-1, keepdims=True))
        dg_ref[...] += jnp.sum(dhv * xhat, axis=0, keepdims=True)

    row = pl.BlockSpec((tr, D), lambda i: (i, 0))
    vec = pl.BlockSpec((1, D), lambda i: (0, 0))
    return pl.pallas_call(
        body, name=name, grid=(M // tr,),
        in_specs=[row, row, vec, pl.BlockSpec((tr, 1), lambda i: (i, 0)), row],
        out_specs=[row, vec],
        out_shape=[jax.ShapeDtypeStruct((M, D), F32), jax.ShapeDtypeStruct((1, D), F32)],
        compiler_params=_params("arbitrary"),
    )(dh, x, g, rstd, dres)


def final_loss(x, g, target):
    M, D = x.shape
    tr = _tile(M, ROW_TILE)

    def body(x_ref, g_ref, t_ref, loss_ref, dx_ref, dg_ref):
        i = pl.program_id(0)

        @pl.when(i == 0)
        def _():
            loss_ref[...] = jnp.zeros_like(loss_ref)
            dg_ref[...] = jnp.zeros_like(dg_ref)

        live = (_rows(tr) + i * tr) >= OFF
        xv = x_ref[...]
        gv = g_ref[...]
        r = lax.rsqrt(jnp.mean(xv * xv, axis=-1, keepdims=True) + EPS)
        xhat = xv * r
        err = jnp.where(live, xhat * gv - t_ref[...], 0.0)
        loss_ref[...] += 0.5 * jnp.sum(jnp.mean(err * err, axis=-1, keepdims=True), axis=0, keepdims=True)
        dy = err * (1.0 / D)
        gd = dy * gv
        dx_ref[...] = r * (gd - xhat * jnp.mean(gd * xhat, axis=-1, keepdims=True))
        dg_ref[...] += jnp.sum(dy * xhat, axis=0, keepdims=True)

    row = pl.BlockSpec((tr, D), lambda i: (i, 0))
    vec = pl.BlockSpec((1, D), lambda i: (0, 0))
    return pl.pallas_call(
        body, name="final_loss", grid=(M // tr,),
        in_specs=[row, vec, row],
        out_specs=[pl.BlockSpec((1, 1), lambda i: (0, 0)), row, vec],
        out_shape=[jax.ShapeDtypeStruct((1, 1), F32), jax.ShapeDtypeStruct((M, D), F32),
                   jax.ShapeDtypeStruct((1, D), F32)],
        compiler_params=_params("arbitrary"),
    )(x, g, target)


def mm_nn(a, w, out_dtype, name, tm_target=704, tn_target=896):
    M, K = a.shape
    N = w.shape[1]
    tm, tn = _tile(M, tm_target), _tile(N, tn_target, 128)

    def body(a_ref, w_ref, o_ref):
        o_ref[...] = _dot(a_ref[...], w_ref[...]).astype(out_dtype)

    return pl.pallas_call(
        body, name=name, grid=(N // tn, M // tm),
        in_specs=[pl.BlockSpec((tm, K), lambda n, i: (i, 0)), pl.BlockSpec((K, tn), lambda n, i: (0, n))],
        out_specs=pl.BlockSpec((tm, tn), lambda n, i: (i, n)),
        out_shape=jax.ShapeDtypeStruct((M, N), out_dtype),
        compiler_params=_params("parallel", "parallel"),
    )(a, w)


def ffn_gateup(h, wg, wu, name):
    M, D = h.shape
    Fs = wg.shape[2]
    tm = _tile(M, 352)

    def body(h_ref, wg_ref, wu_ref, g_ref, u_ref):
        hv = h_ref[...]
        g_ref[...] = _dot(hv, wg_ref[...]).astype(BF16)
        u_ref[...] = _dot(hv, wu_ref[...]).astype(BF16)

    wspec = pl.BlockSpec((None, D, Fs), lambda j, i: (j, 0, 0))
    ospec = pl.BlockSpec((tm, Fs), lambda j, i: (i, j))
    return pl.pallas_call(
        body, name=name, grid=(N_CHIPS, M // tm),
        in_specs=[pl.BlockSpec((tm, D), lambda j, i: (i, 0)), wspec, wspec],
        out_specs=[ospec, ospec],
        out_shape=[jax.ShapeDtypeStruct((M, N_CHIPS * Fs), BF16)] * 2,
        compiler_params=_params("parallel", "parallel"),
    )(h, wg, wu)


def mm_residual(acts, w, x, scale, tk, name, tm_target=352):
    M, N = x.shape
    K = w.shape[0]
    tm = _tile(M, tm_target)
    swiglu = len(acts) == 2

    def body(*refs):
        a_refs, (w_ref, x_ref, o_ref, acc) = refs[:len(acts)], refs[len(acts):]
        k = pl.program_id(1)

        @pl.when(k == 0)
        def _():
            acc[...] = jnp.zeros_like(acc)

        if swiglu:
            gv = a_refs[0][...].astype(F32)
            av = (gv * _sigmoid(gv) * a_refs[1][...].astype(F32)).astype(BF16)
        else:
            av = a_refs[0][...]
        acc[...] += _dot(av, w_ref[...])

        @pl.when(k == pl.num_programs(1) - 1)
        def _():
            o_ref[...] = x_ref[...] + scale * acc[...]

    aspec = pl.BlockSpec((tm, tk), lambda i, k: (i, k))
    return pl.pallas_call(
        body, name=name, grid=(M // tm, K // tk),
        in_specs=[aspec] * len(acts) + [pl.BlockSpec((tk, N), lambda i, k: (k, 0)),
                                        pl.BlockSpec((tm, N), lambda i, k: (i, 0), pipeline_mode=ONE_BUFFER)],
        out_specs=pl.BlockSpec((tm, N), lambda i, k: (i, 0), pipeline_mode=ONE_BUFFER),
        out_shape=jax.ShapeDtypeStruct((M, N), F32),
        scratch_shapes=[pltpu.VMEM((tm, N), F32)],
        compiler_params=_params("parallel", "arbitrary"),
    )(*acts, w, x)


def ffn_bwd_act(dout, wd, gate, up, name, dep=None):
    M, D = dout.shape
    F = wd.shape[0]
    Fs = F // N_CHIPS
    tm = _tile(M, 352)

    def body(dy_ref, wd_ref, g_ref, u_ref, *rest):
        dg_ref, du_ref, a_ref = rest[-3:]
        dy = (0.5 * dy_ref[...]).astype(BF16)
        for cols in _chunks(Fs):
            da = _dot_nt(dy, wd_ref[cols, :])
            gv = g_ref[:, cols].astype(F32)
            uv = u_ref[:, cols].astype(F32)
            s = _sigmoid(gv)
            silu = gv * s
            a_ref[:, cols] = (silu * uv).astype(BF16)
            dg_ref[:, cols] = (da * uv * (s * (1.0 + gv * (1.0 - s)))).astype(BF16)
            du_ref[:, cols] = (da * silu).astype(BF16)

    fspec = pl.BlockSpec((tm, Fs), lambda j, i: (i, j))
    return pl.pallas_call(
        body, name=name, grid=(N_CHIPS, M // tm),
        in_specs=[pl.BlockSpec((tm, D), lambda j, i: (i, 0)), pl.BlockSpec((Fs, D), lambda j, i: (j, 0)),
                  fspec, fspec] + _dep_specs(dep),
        out_specs=[fspec, fspec, fspec],
        out_shape=[jax.ShapeDtypeStruct((M, F), BF16)] * 3,
        compiler_params=_params("parallel", "parallel"),
    )(dout, wd, gate, up, *_dep_args(dep))


def mm_tn(a, b, ta, tb, name, b_scale=1.0, stacked_out=False, out_dtype=BF16, dep=None):
    T, Ma = a.shape
    Nb = b.shape[1]

    def body(a_ref, b_ref, *rest):
        o_ref = rest[-1]
        bv = b_ref[...]
        if b_scale != 1.0:
            bv = b_scale * bv
        o_ref[...] = _dot_tn(a_ref[...], bv.astype(BF16)).astype(out_dtype)

    if stacked_out:
        out_spec = pl.BlockSpec((None, ta, tb), lambda jb, ja: (jb, ja, 0))
        out_shape = jax.ShapeDtypeStruct((Nb // tb, Ma, tb), out_dtype)
    else:
        out_spec = pl.BlockSpec((ta, tb), lambda jb, ja: (ja, jb))
        out_shape = jax.ShapeDtypeStruct((Ma, Nb), out_dtype)
    return pl.pallas_call(
        body, name=name, grid=(Nb // tb, Ma // ta),
        in_specs=[pl.BlockSpec((T, ta), lambda jb, ja: (0, ja)), pl.BlockSpec((T, tb), lambda jb, ja: (0, jb))]
        + _dep_specs(dep),
        out_specs=out_spec, out_shape=out_shape,
        compiler_params=_params("parallel", "parallel"),
    )(a, b, *_dep_args(dep))


def mm_nt(pairs, tm, tn, tk, name, a_scale=1.0, stacked_w=False, dep=None):
    M, K = pairs[0][0].shape
    N = pairs[0][1].shape[1] if stacked_w else pairs[0][1].shape[0]
    n_pairs = len(pairs)

    def body(*refs):
        o_ref, acc = refs[-2:]
        k = pl.program_id(2)

        @pl.when(k == 0)
        def _():
            acc[...] = jnp.zeros_like(acc)

        for p in range(n_pairs):
            av = refs[2 * p][...]
            if a_scale != 1.0:
                av = a_scale * av
            acc[...] += _dot_nt(av.astype(BF16), refs[2 * p + 1][...])

        @pl.when(k == pl.num_programs(2) - 1)
        def _():
            o_ref[...] = acc[...]

    aspec = pl.BlockSpec((tm, tk), lambda i, n, k: (i, k))
    if stacked_w:
        wspec = pl.BlockSpec((None, tn, tk), lambda i, n, k: (k, n, 0))
    else:
        wspec = pl.BlockSpec((tn, tk), lambda i, n, k: (n, k))
    return pl.pallas_call(
        body, name=name, grid=(M // tm, N // tn, K // tk),
        in_specs=[aspec, wspec] * n_pairs + _dep_specs(dep),
        out_specs=pl.BlockSpec((tm, tn), lambda i, n, k: (i, n), pipeline_mode=ONE_BUFFER),
        out_shape=jax.ShapeDtypeStruct((M, N), F32),
        scratch_shapes=[pltpu.VMEM((tm, tn), F32)],
        compiler_params=_params("parallel", "parallel", "arbitrary"),
    )(*[t for pair in pairs for t in pair], *_dep_args(dep))


def _tri(lower):
    r = lax.broadcasted_iota(jnp.int32, (CHUNK, CHUNK), 0)
    c = lax.broadcasted_iota(jnp.int32, (CHUNK, CHUNK), 1)
    return (r >= c) if lower else (r <= c)


def _tri_sum(mask, x, pieces):
    ones = mask.astype(BF16)
    acc = jnp.zeros_like(x)
    rest = x
    for _ in range(pieces):
        piece = rest.astype(BF16)
        acc = acc + _dot(ones, piece)
        rest = rest - piece.astype(F32)
    return acc


def _gla_gates(lr, wlr, blr, chunk):
    z = _dot(lr, wlr) + blr
    live = (_rows(CHUNK) + chunk * CHUNK) >= PAD
    lg = jnp.where(live, (jnp.minimum(z, 0.0) - jnp.log(1.0 + jnp.exp(-jnp.abs(z)))) * (1.0 / GATE_NORM), 0.0)
    b = _tri_sum(_tri(True), lg, 3)
    b_last = jnp.sum(lg, axis=0, keepdims=True)
    b_mid = jnp.sum(jnp.where(_rows(CHUNK) < CHUNK // 2, lg, 0.0), axis=0, keepdims=True)
    return z, live, b, b_last, b_mid


def _gla_specs(D, chunk_of):
    lr_blk = (3 * D) // LR_W
    return [
        pl.BlockSpec((CHUNK, D // 2), lambda c: (chunk_of(c), 0)),
        pl.BlockSpec((CHUNK, D // 2), lambda c: (chunk_of(c), 1)),
        pl.BlockSpec((CHUNK, D), lambda c: (chunk_of(c), 1)),
        pl.BlockSpec((CHUNK, LR_W), lambda c: (chunk_of(c), lr_blk)),
        pl.BlockSpec((LR_W, D // 2), lambda c: (0, 0)),
        pl.BlockSpec((1, D // 2), lambda c: (0, 0)),
    ]


def gla_fwd(proj, wlr, blr, D):
    M = proj.shape[0]
    n = M // CHUNK
    dkh, dvh = D // 2 // HEADS, D // HEADS
    qscale = float(dkh) ** -0.5

    def body(q_ref, k_ref, v_ref, lr_ref, wlr_ref, blr_ref, o_ref, st_ref, S):
        c = pl.program_id(0)

        @pl.when(c == 0)
        def _():
            S[...] = jnp.zeros_like(S)

        lr = lr_ref[...].astype(BF16)
        for h in range(HEADS):
            kc, vc = slice(h * dkh, (h + 1) * dkh), slice(h * dvh, (h + 1) * dvh)
            _, _, b, b_last, b_mid = _gla_gates(lr, wlr_ref[:, kc], blr_ref[:, kc], c)
            q = q_ref[:, kc] * qscale
            k = k_ref[:, kc]
            v = v_ref[:, vc].astype(BF16)
            s0 = S[h]
            st_ref[h] = s0
            qb = (q * jnp.exp(b)).astype(BF16)
            kb = (k * jnp.exp(b_last - b)).astype(BF16)
            qt = (q * jnp.exp(b - b_mid)).astype(BF16)
            kt = (k * jnp.exp(b_mid - b)).astype(BF16)
            a = jnp.where(_tri(True), _dot_nt(qt, kt), 0.0).astype(BF16)
            o_ref[:, vc] = _dot_nt(qb, s0.astype(BF16)) + _dot(a, v)
            S[h] = jnp.exp(b_last) * s0 + _dot_tn(v, kb)

    return pl.pallas_call(
        body, name="gla_fwd", grid=(n,),
        in_specs=_gla_specs(D, lambda c: c),
        out_specs=[pl.BlockSpec((CHUNK, D), lambda c: (c, 0)),
                   pl.BlockSpec((None, HEADS, dvh, dkh), lambda c: (c, 0, 0, 0))],
        out_shape=[jax.ShapeDtypeStruct((M, D), F32), jax.ShapeDtypeStruct((n, HEADS, dvh, dkh), F32)],
        scratch_shapes=[pltpu.VMEM((HEADS, dvh, dkh), F32)],
        compiler_params=_params("arbitrary"),
    )(proj, proj, proj, proj, wlr, blr)


def gla_bwd(proj, wlr, blr, st, do, D):
    M = proj.shape[0]
    n = M // CHUNK
    dkh, dvh = D // 2 // HEADS, D // HEADS
    qscale = float(dkh) ** -0.5
    rev = lambda c: n - 1 - c

    def body(q_ref, k_ref, v_ref, lr_ref, wlr_ref, blr_ref, st_ref, do_ref,
             dq_ref, dk_ref, dv_ref, dlr_ref, dwlr_ref, dblr_ref, dS):
        step = pl.program_id(0)
        c = n - 1 - step

        @pl.when(step == 0)
        def _():
            dS[...] = jnp.zeros_like(dS)
            dwlr_ref[...] = jnp.zeros_like(dwlr_ref)
            dblr_ref[...] = jnp.zeros_like(dblr_ref)

        lr = lr_ref[...].astype(BF16)
        lower = _tri(True)
        dlr = None
        for h in range(HEADS):
            kc, vc = slice(h * dkh, (h + 1) * dkh), slice(h * dvh, (h + 1) * dvh)
            wlr_h = wlr_ref[:, kc]
            z, live, b, b_last, b_mid = _gla_gates(lr, wlr_h, blr_ref[:, kc], c)
            q = q_ref[:, kc] * qscale
            k = k_ref[:, kc]
            v = v_ref[:, vc].astype(BF16)
            dov = do_ref[:, vc].astype(BF16)
            s0 = st_ref[h]
            ds1 = dS[h]
            ds1b = ds1.astype(BF16)
            e_b, e_lb = jnp.exp(b), jnp.exp(b_last - b)
            e_bm, e_mb = jnp.exp(b - b_mid), jnp.exp(b_mid - b)
            e_last = jnp.exp(b_last)
            qb, kb, qt, kt = q * e_b, k * e_lb, q * e_bm, k * e_mb
            qbb, kbb, qtb, ktb = qb.astype(BF16), kb.astype(BF16), qt.astype(BF16), kt.astype(BF16)
            a = jnp.where(lower, _dot_nt(qtb, ktb), 0.0).astype(BF16)
            da = jnp.where(lower, _dot_nt(dov, v), 0.0).astype(BF16)

            dqb = _dot(dov, s0.astype(BF16))
            dqt = _dot(da, ktb)
            dkt = _dot_tn(da, qtb)
            dkb = _dot(v, ds1b)
            keep = live.astype(F32)
            dv_ref[:, vc] = (keep * (_dot_tn(a, dov) + _dot_nt(kbb, ds1b))).astype(BF16)
            dq_ref[:, kc] = (keep * qscale * (dqb * e_b + dqt * e_bm)).astype(BF16)
            dk_ref[:, kc] = (keep * (dkb * e_lb + dkt * e_mb)).astype(BF16)

            db = dqb * qb - dkb * kb + dqt * qt - dkt * kt
            db_last = (jnp.sum(dkb * kb, axis=0, keepdims=True)
                       + jnp.sum(ds1 * s0, axis=0, keepdims=True) * e_last)
            db = db + jnp.where(_rows(CHUNK) == CHUNK - 1, db_last, 0.0)
            dlg = jnp.where(live, _tri_sum(_tri(False), db, 2), 0.0)
            dz = dlg * (1.0 / GATE_NORM) / (1.0 + jnp.exp(z))
            dzb = dz.astype(BF16)

            dlr_h = _dot_nt(dzb, wlr_h)
            dlr = dlr_h if dlr is None else dlr + dlr_h
            dwlr_ref[h] += _dot_tn(lr, dzb)
            dblr_ref[h] += jnp.sum(dz, axis=0, keepdims=True)
            dS[h] = e_last * ds1 + _dot_tn(dov, qbb)
        dlr_ref[...] = dlr

    return pl.pallas_call(
        body, name="gla_bwd", grid=(n,),
        in_specs=_gla_specs(D, rev) + [
            pl.BlockSpec((None, HEADS, dvh, dkh), lambda c: (rev(c), 0, 0, 0)),
            pl.BlockSpec((CHUNK, D), lambda c: (rev(c), 0))],
        out_specs=[pl.BlockSpec((CHUNK, D // 2), lambda c: (rev(c), 0)),
                   pl.BlockSpec((CHUNK, D // 2), lambda c: (rev(c), 0)),
                   pl.BlockSpec((CHUNK, D), lambda c: (rev(c), 0)),
                   pl.BlockSpec((CHUNK, LR_W), lambda c: (rev(c), 0)),
                   pl.BlockSpec((HEADS, LR_W, dkh), lambda c: (0, 0, 0)),
                   pl.BlockSpec((HEADS, 1, dkh), lambda c: (0, 0, 0))],
        out_shape=[jax.ShapeDtypeStruct((M, D // 2), BF16), jax.ShapeDtypeStruct((M, D // 2), BF16),
                   jax.ShapeDtypeStruct((M, D), BF16), jax.ShapeDtypeStruct((M, LR_W), F32),
                   jax.ShapeDtypeStruct((HEADS, LR_W, dkh), F32), jax.ShapeDtypeStruct((HEADS, 1, dkh), F32)],
        scratch_shapes=[pltpu.VMEM((HEADS, dvh, dkh), F32)],
        compiler_params=_params("arbitrary"),
    )(proj, proj, proj, proj, wlr, blr, st, do)


def gla_post_fwd(o, proj, head_norm, D):
    M = o.shape[0]
    dvh = D // HEADS
    tr = _tile(M, ROW_TILE)

    def body(o_ref, r_ref, hn_ref, out_ref):
        for hd in range(HEADS):
            cols = slice(hd * dvh, (hd + 1) * dvh)
            ov = o_ref[:, cols]
            rs = lax.rsqrt(jnp.mean(ov * ov, axis=-1, keepdims=True) + EPS)
            rv = r_ref[:, cols]
            out_ref[:, cols] = (ov * rs * hn_ref[...] * (rv * _sigmoid(rv))).astype(BF16)

    row = pl.BlockSpec((tr, D), lambda i: (i, 0))
    return pl.pallas_call(
        body, name="gla_post_fwd", grid=(M // tr,),
        in_specs=[row, pl.BlockSpec((tr, D), lambda i: (i, 2)), pl.BlockSpec((1, dvh), lambda i: (0, 0))],
        out_specs=row, out_shape=jax.ShapeDtypeStruct((M, D), BF16),
        compiler_params=_params("parallel"),
    )(o, proj, head_norm)


def gla_post_bwd(dgated, o, proj, head_norm, D):
    M = o.shape[0]
    dvh = D // HEADS
    tr = _tile(M, ROW_TILE)

    def body(dg_ref, o_ref, r_ref, hn_ref, do_ref, dr_ref, dhn_ref):
        @pl.when(pl.program_id(0) == 0)
        def _():
            dhn_ref[...] = jnp.zeros_like(dhn_ref)

        hn = hn_ref[...]
        dhn = jnp.zeros((1, dvh), F32)
        for hd in range(HEADS):
            cols = slice(hd * dvh, (hd + 1) * dvh)
            ov = o_ref[:, cols]
            rs = lax.rsqrt(jnp.mean(ov * ov, axis=-1, keepdims=True) + EPS)
            ohat = ov * rs
            rv = r_ref[:, cols]
            s = _sigmoid(rv)
            dgv = dg_ref[:, cols]
            don = dgv * (rv * s)
            dr_ref[:, cols] = (dgv * ohat * hn * (s * (1.0 + rv * (1.0 - s)))).astype(BF16)
            gd = don * hn
            do_ref[:, cols] = rs * (gd - ohat * jnp.mean(gd * ohat, axis=-1, keepdims=True))
            dhn = dhn + jnp.sum(don * ohat, axis=0, keepdims=True)
        dhn_ref[...] += dhn

    row = pl.BlockSpec((tr, D), lambda i: (i, 0))
    vec = pl.BlockSpec((1, dvh), lambda i: (0, 0))
    return pl.pallas_call(
        body, name="gla_post_bwd", grid=(M // tr,),
        in_specs=[row, row, pl.BlockSpec((tr, D), lambda i: (i, 2)), vec],
        out_specs=[row, row, vec],
        out_shape=[jax.ShapeDtypeStruct((M, D), F32), jax.ShapeDtypeStruct((M, D), BF16),
                   jax.ShapeDtypeStruct((1, dvh), F32)],
        compiler_params=_params("arbitrary"),
    )(dgated, o, proj, head_norm)


def _pool_counts(M, g):
    t = _rows(M) - PAD
    win = jnp.left_shift(2, g)
    return t >= 0, jnp.maximum(jnp.minimum(t + 1, win), 1).astype(F32)


def _window_sum(x, g, M, back):
    sums = []
    s = x
    for lvl in range(4):
        sh = 1 << lvl
        s = s + pltpu.roll(s, (M - sh) if back else sh, 0)
        sums.append(s)
    return jnp.where(g == 0, sums[0], jnp.where(g == 1, sums[1], jnp.where(g == 2, sums[2], sums[3])))


POOL_COLS = 128


def pool_window(hp):
    M, D = hp.shape
    cw = min(POOL_COLS, D // 4)
    per_group = (D // 4) // cw

    def body(h_ref, p_ref):
        g = pl.program_id(0) // per_group
        live, cnt = _pool_counts(M, g)
        hv = h_ref[...]
        p_ref[...] = jnp.where(live, _window_sum(hv, g, M, False) / cnt - hv, 0.0).astype(BF16)

    col = pl.BlockSpec((M, cw), lambda j: (0, j))
    return pl.pallas_call(
        body, name="pool_window", grid=(D // cw,), in_specs=[col], out_specs=col,
        out_shape=jax.ShapeDtypeStruct((M, D), BF16), compiler_params=_params("parallel"),
    )(hp)


def pool_window_bwd(dpooled):
    M, D = dpooled.shape
    cw = min(POOL_COLS, D // 4)
    per_group = (D // 4) // cw

    def body(d_ref, o_ref):
        g = pl.program_id(0) // per_group
        live, cnt = _pool_counts(M, g)
        dv = jnp.where(live, d_ref[...], 0.0)
        o_ref[...] = jnp.where(live, _window_sum(dv / cnt, g, M, True) - dv, 0.0)

    col = pl.BlockSpec((M, cw), lambda j: (0, j))
    return pl.pallas_call(
        body, name="pool_window_bwd", grid=(D // cw,), in_specs=[col], out_specs=col,
        out_shape=jax.ShapeDtypeStruct((M, D), F32), compiler_params=_params("parallel"),
    )(dpooled)


def pool_mix(pooled, x, w, bias, scale):
    M, D = x.shape
    W = D // 4
    tm = _tile(M, 352)

    def body(p_ref, x_ref, w_ref, b_ref, s_ref, out_ref):
        live = (_rows(tm) + pl.program_id(1) * tm) >= PAD
        y = (_dot(p_ref[...], w_ref[...]) + b_ref[...]) * s_ref[...]
        out_ref[...] = x_ref[...] + jnp.where(live, y, 0.0)

    blk = pl.BlockSpec((tm, W), lambda g, i: (i, g))
    vec = pl.BlockSpec((1, W), lambda g, i: (0, g))
    return pl.pallas_call(
        body, name="pool_mix", grid=(4, M // tm),
        in_specs=[blk, blk, pl.BlockSpec((None, W, W), lambda g, i: (g, 0, 0)), vec, vec],
        out_specs=blk, out_shape=jax.ShapeDtypeStruct((M, D), F32),
        compiler_params=_params("parallel", "parallel"),
    )(pooled, x, w, bias, scale)


def pool_mix_bwd(dy, pooled, w, bias, scale, dep=None):
    M, D = dy.shape
    W = D // 4
    tm = _tile(M, 352)

    def body(dy_ref, p_ref, w_ref, b_ref, s_ref, *rest):
        dp_ref, dw_ref, db_ref, ds_ref, acc_w = rest[-5:]
        i = pl.program_id(1)

        @pl.when(i == 0)
        def _():
            acc_w[...] = jnp.zeros_like(acc_w)
            db_ref[...] = jnp.zeros_like(db_ref)
            ds_ref[...] = jnp.zeros_like(ds_ref)

        live = (_rows(tm) + i * tm) >= PAD
        dyv = jnp.where(live, dy_ref[...], 0.0)
        pooled = p_ref[...]
        wv = w_ref[...]
        ds_ref[...] += jnp.sum(dyv * (_dot(pooled, wv) + b_ref[...]), axis=0, keepdims=True)
        dys = dyv * s_ref[...]
        db_ref[...] += jnp.sum(dys, axis=0, keepdims=True)
        dysb = dys.astype(BF16)
        acc_w[...] += _dot_tn(pooled, dysb)
        dp_ref[...] = _dot_nt(dysb, wv)

        @pl.when(i == pl.num_programs(1) - 1)
        def _():
            dw_ref[...] = acc_w[...].astype(BF16)

    blk = pl.BlockSpec((tm, W), lambda g, i: (i, g))
    vec = pl.BlockSpec((1, W), lambda g, i: (0, g))
    wspec = pl.BlockSpec((None, W, W), lambda g, i: (g, 0, 0))
    return pl.pallas_call(
        body, name="pool_mix_bwd", grid=(4, M // tm),
        in_specs=[blk, blk, wspec, vec, vec] + _dep_specs(dep),
        out_specs=[blk, wspec, vec, vec],
        out_shape=[jax.ShapeDtypeStruct((M, D), F32), jax.ShapeDtypeStruct((4, W, W), BF16),
                   jax.ShapeDtypeStruct((1, D), F32), jax.ShapeDtypeStruct((1, D), F32)],
        scratch_shapes=[pltpu.VMEM((W, W), F32)],
        compiler_params=_params("parallel", "arbitrary"),
    )(dy, pooled, w, bias, scale, *_dep_args(dep))


def adamw(w, g, m, v, name, copy_g=False):
    shape = w.shape
    C = shape[-1]
    R = w.size // C
    tr = _tile(R, 256, 8)
    tc = C
    if tr == R and R > 256:
        tc = _tile(C, 256, 128)

    def body(w_ref, g_ref, m_ref, v_ref, d_ref, nm_ref, nv_ref, *g_out):
        gv = g_ref[...]
        for ref in g_out:
            ref[...] = gv
        nm = ADAM_B1 * m_ref[...] + (1.0 - ADAM_B1) * gv
        nv = ADAM_B2 * v_ref[...] + (1.0 - ADAM_B2) * (gv * gv)
        m_hat = nm / (1.0 - ADAM_B1 ** ADAM_STEP)
        v_hat = nv / (1.0 - ADAM_B2 ** ADAM_STEP)
        d_ref[...] = -ADAM_LR * (m_hat / (jnp.sqrt(v_hat) + ADAM_EPS) + ADAM_WD * w_ref[...])
        nm_ref[...] = nm
        nv_ref[...] = nv

    spec = pl.BlockSpec((tr, tc), lambda i, j: (i, j))
    outs = pl.pallas_call(
        body, name=name, grid=(R // tr, C // tc),
        in_specs=[spec] * 4, out_specs=[spec] * (3 + copy_g),
        out_shape=[jax.ShapeDtypeStruct((R, C), F32)] * (3 + copy_g),
        compiler_params=_params("parallel", "parallel"),
    )(*[t.reshape(R, C) for t in (w, g, m, v)])
    return [t.reshape(shape) for t in outs]


def add_sibling(grad, recv, core, name):
    _, _, Rh, C = grad.shape
    tr = _tile(Rh, 512)

    def body(core_ref, g_ref, r_ref, o_ref):
        o_ref[...] = (g_ref[...].astype(F32) + r_ref[...].astype(F32)).astype(BF16)

    return pl.pallas_call(
        body, name=name,
        grid_spec=pltpu.PrefetchScalarGridSpec(
            num_scalar_prefetch=1, grid=(N_CHIPS, Rh // tr),
            in_specs=[pl.BlockSpec((None, None, tr, C), lambda j, i, core_ref: (j, core_ref[0], i, 0)),
                      pl.BlockSpec((None, tr, C), lambda j, i, core_ref: (j, i, 0))],
            out_specs=pl.BlockSpec((None, tr, C), lambda j, i, core_ref: (j, i, 0))),
        out_shape=jax.ShapeDtypeStruct((N_CHIPS, Rh, C), BF16),
        compiler_params=_params("parallel", "parallel"),
    )(core, grad, recv)


def add_chips(part, recv, chip, core, group, n, mi, name):
    _, Rh, C = part.shape
    tr = _tile(Rh, 512)

    def body(chip_ref, core_ref, p_ref, r_ref, *rest):
        o_ref = rest[-1]
        acc = p_ref[...].astype(F32)
        for k in range(N_CHIPS - 1):
            acc = acc + r_ref[k].astype(F32)
        o_ref[...] = acc

    carried = [] if group is None else [group]
    return pl.pallas_call(
        body, name=name,
        grid_spec=pltpu.PrefetchScalarGridSpec(
            num_scalar_prefetch=2, grid=(Rh // tr,),
            in_specs=[pl.BlockSpec((None, tr, C), lambda i, chip_ref, core_ref: (chip_ref[0], i, 0)),
                      pl.BlockSpec((N_CHIPS - 1, tr, C), lambda i, chip_ref, core_ref: (0, i, 0))]
            + [ANY] * len(carried),
            out_specs=pl.BlockSpec((None, None, tr, C), lambda i, chip_ref, core_ref: (mi, core_ref[0], i, 0))),
        out_shape=jax.ShapeDtypeStruct((n, 2, Rh, C), F32),
        input_output_aliases={4: 0} if carried else {},
        compiler_params=_params("parallel"),
    )(chip, core, part, recv, *carried)


def stage_shard(shard, mi, chip, name, dep=None):
    _, _, Rh, C = shard.shape
    tr = _tile(Rh, 512)

    def body(chip_ref, s_ref, *rest):
        rest[-1][...] = s_ref[...].astype(BF16)

    return pl.pallas_call(
        body, name=name,
        grid_spec=pltpu.PrefetchScalarGridSpec(
            num_scalar_prefetch=1, grid=(2, Rh // tr),
            in_specs=[pl.BlockSpec((None, None, tr, C), lambda h, i, chip_ref: (mi, h, i, 0))] + _dep_specs(dep),
            out_specs=pl.BlockSpec((None, None, tr, C), lambda h, i, chip_ref: (chip_ref[0], h, i, 0))),
        out_shape=jax.ShapeDtypeStruct((N_CHIPS, 2, Rh, C), BF16),
        compiler_params=_params("parallel", "parallel"),
    )(chip, shard, *_dep_args(dep))


def sum_devices(gathered):
    _, R, C = gathered.shape

    def body(g_ref, o_ref):
        acc = g_ref[0]
        for d in range(1, N_DEV):
            acc = acc + g_ref[d]
        o_ref[...] = acc

    return pl.pallas_call(
        body, name="sum_devices", grid=(1,),
        in_specs=[pl.BlockSpec((N_DEV, R, C), lambda i: (0, 0, 0))],
        out_specs=pl.BlockSpec((R, C), lambda i: (0, 0)),
        out_shape=jax.ShapeDtypeStruct((R, C), F32),
        compiler_params=_params("arbitrary"),
    )(gathered)


def _place():
    x, y, c = lax.axis_index("x"), lax.axis_index("y"), lax.axis_index("c")
    others = [(1 - x, y), (x, 1 - y), (1 - x, 1 - y)]
    return x, y, c, others


def _remote(src, dst, send_sems, recv_sems, idx, device):
    return pltpu.make_async_remote_copy(src_ref=src, dst_ref=dst, send_sem=send_sems.at[idx],
                                        recv_sem=recv_sems.at[idx], device_id=device, device_id_type=MESH)


HBM = pl.BlockSpec(memory_space=pltpu.HBM)
SEM = pl.BlockSpec(memory_space=pltpu.SEMAPHORE)
EFFECT = pltpu.SideEffectType.DATAFLOW_SIDE_EFFECTING


def _in_hbm(t):
    return pltpu.with_memory_space_constraint(t, pltpu.HBM)


def _own_slice(buf, me, c):
    return buf.at[me, c] if len(buf.shape) == 4 else buf.at[me]


def gather_start(staged, bucket_sizes, name):
    n, nb = len(staged), len(bucket_sizes)

    def body(*refs):
        in_refs, sems, token = refs[:n], refs[n:n + 2 * nb], refs[-1]
        x, y, c, others = _place()
        me = 2 * x + y
        t = 0
        for b, size in enumerate(bucket_sizes):
            for i in range(size):
                mine = _own_slice(in_refs[t], me, c)
                for k, chip in enumerate(others):
                    _remote(mine, mine, sems[2 * b], sems[2 * b + 1], 3 * i + k, (*chip, c)).start()
                t += 1
        token[...] = jnp.zeros_like(token)

    sem_shapes = [pltpu.SemaphoreType.DMA((3 * size,)) for size in bucket_sizes for _ in range(2)]
    outs = pl.pallas_call(
        body, name=name,
        out_shape=sem_shapes + [pltpu.HBM(s.shape, s.dtype) for s in staged] + [jax.ShapeDtypeStruct((8, 128), F32)],
        in_specs=[HBM] * n, out_specs=[SEM] * (2 * nb) + [HBM] * n + [pl.BlockSpec(memory_space=pltpu.VMEM)],
        input_output_aliases={t: 2 * nb + t for t in range(n)},
        compiler_params=pltpu.CompilerParams(has_side_effects=EFFECT),
    )(*[_in_hbm(s) for s in staged])
    sems = [(outs[2 * b], outs[2 * b + 1]) for b in range(nb)]
    return sems, list(outs[2 * nb:2 * nb + n]), outs[-1]


def gather_wait(bufs, sems, after, name):
    n = len(bufs)

    def body(*refs):
        in_refs, send_sems, recv_sems = refs[:n], refs[n], refs[n + 1]
        x, y, c, others = _place()
        me = 2 * x + y
        for i in range(n):
            mine = _own_slice(in_refs[i], me, c)
            for k, (ox, oy) in enumerate(others):
                cp = _remote(mine, _own_slice(in_refs[i], 2 * ox + oy, c), send_sems, recv_sems, 3 * i + k,
                             (ox, oy, c))
                cp.wait_send()
                cp.wait_recv()

    return pl.pallas_call(
        body, name=name, out_shape=[pltpu.HBM(b.shape, b.dtype) for b in bufs],
        in_specs=[HBM] * n + [SEM, SEM, ANY], out_specs=[HBM] * n,
        input_output_aliases={t: t for t in range(n)},
        compiler_params=pltpu.CompilerParams(has_side_effects=EFFECT),
    )(*bufs, *sems, after)


def forward_to_sibling(bufs, name):
    n = len(bufs)

    def body(*refs):
        out_refs, (send_sems, recv_sems) = refs[n:2 * n], refs[2 * n:]
        x, y, c, others = _place()
        sibling = (x, y, 1 - c)
        copies = []
        for t in range(n):
            for k, (ox, oy) in enumerate(others):
                mine = out_refs[t].at[2 * ox + oy, c]
                cp = _remote(mine, mine, send_sems, recv_sems, 3 * t + k, sibling)
                cp.start()
                copies.append(cp)
        for t in range(n):
            for k, (ox, oy) in enumerate(others):
                theirs = out_refs[t].at[2 * ox + oy, 1 - c]
                _remote(theirs, theirs, send_sems, recv_sems, 3 * t + k, sibling).wait_recv()
        for cp in copies:
            cp.wait_send()

    return pl.pallas_call(
        body, name=name, in_specs=[ANY] * n, out_specs=[ANY] * n,
        out_shape=[jax.ShapeDtypeStruct(b.shape, b.dtype) for b in bufs],
        input_output_aliases={t: t for t in range(n)},
        scratch_shapes=[pltpu.SemaphoreType.DMA((3 * n,)), pltpu.SemaphoreType.DMA((3 * n,))],
    )(*bufs)


def sibling_start(grads, name):
    n = len(grads)
    lands = [lax.empty((N_CHIPS,) + g.shape[2:], g.dtype) for g in grads]

    def body(*refs):
        in_refs, land_refs, send_sems, recv_sems, token = refs[:n], refs[n:2 * n], refs[2 * n], refs[2 * n + 1], refs[-1]
        x, y, c, _ = _place()
        for t in range(n):
            for j in range(N_CHIPS):
                _remote(in_refs[t].at[j, 1 - c], land_refs[t].at[j], send_sems, recv_sems, N_CHIPS * t + j,
                        (x, y, 1 - c)).start()
        token[...] = jnp.zeros_like(token)

    outs = pl.pallas_call(
        body, name=name,
        out_shape=[pltpu.SemaphoreType.DMA((N_CHIPS * n,))] * 2 + [pltpu.HBM(t.shape, t.dtype) for t in grads + lands]
        + [jax.ShapeDtypeStruct((8, 128), F32)],
        in_specs=[HBM] * (2 * n), out_specs=[SEM, SEM] + [HBM] * (2 * n) + [pl.BlockSpec(memory_space=pltpu.VMEM)],
        input_output_aliases={t: 2 + t for t in range(2 * n)},
        compiler_params=pltpu.CompilerParams(has_side_effects=EFFECT),
    )(*[_in_hbm(t) for t in grads + lands])
    return (outs[0], outs[1]), list(outs[2:2 + n]), list(outs[2 + n:2 + 2 * n]), outs[-1]


def sibling_wait(grads, lands, sems, after, name):
    n = len(grads)

    def body(*refs):
        in_refs, land_refs, send_sems, recv_sems = refs[:n], refs[n:2 * n], refs[2 * n], refs[2 * n + 1]
        x, y, c, _ = _place()
        for t in range(n):
            for j in range(N_CHIPS):
                cp = _remote(in_refs[t].at[j, 1 - c], land_refs[t].at[j], send_sems, recv_sems, N_CHIPS * t + j,
                             (x, y, 1 - c))
                cp.wait_send()
                cp.wait_recv()

    outs = pl.pallas_call(
        body, name=name, out_shape=[pltpu.HBM(t.shape, t.dtype) for t in grads + lands],
        in_specs=[HBM] * (2 * n) + [SEM, SEM, ANY], out_specs=[HBM] * (2 * n),
        input_output_aliases={t: t for t in range(2 * n)},
        compiler_params=pltpu.CompilerParams(has_side_effects=EFFECT),
    )(*grads, *lands, *sems, after)
    return list(outs[:n]), list(outs[n:])


def reduce_start(parts, name):
    n = len(parts)
    lands = [lax.empty((N_CHIPS - 1,) + p.shape[1:], p.dtype) for p in parts]

    def body(*refs):
        in_refs, land_refs, send_sems, recv_sems, token = refs[:n], refs[n:2 * n], refs[2 * n], refs[2 * n + 1], refs[-1]
        x, y, c, others = _place()
        for t in range(n):
            for k, (ox, oy) in enumerate(others):
                _remote(in_refs[t].at[2 * ox + oy], land_refs[t].at[k], send_sems, recv_sems, 3 * t + k,
                        (ox, oy, c)).start()
        token[...] = jnp.zeros_like(token)

    outs = pl.pallas_call(
        body, name=name,
        out_shape=[pltpu.SemaphoreType.DMA((3 * n,))] * 2 + [pltpu.HBM(t.shape, t.dtype) for t in parts + lands]
        + [jax.ShapeDtypeStruct((8, 128), F32)],
        in_specs=[HBM] * (2 * n), out_specs=[SEM, SEM] + [HBM] * (2 * n) + [pl.BlockSpec(memory_space=pltpu.VMEM)],
        input_output_aliases={t: 2 + t for t in range(2 * n)},
        compiler_params=pltpu.CompilerParams(has_side_effects=EFFECT),
    )(*[_in_hbm(t) for t in parts + lands])
    return (outs[0], outs[1]), list(outs[2:2 + n]), list(outs[2 + n:2 + 2 * n]), outs[-1]


def reduce_wait(parts, lands, sems, after, name):
    n = len(parts)

    def body(*refs):
        in_refs, land_refs, send_sems, recv_sems = refs[:n], refs[n:2 * n], refs[2 * n], refs[2 * n + 1]
        x, y, c, others = _place()
        for t in range(n):
            for k, (ox, oy) in enumerate(others):
                cp = _remote(in_refs[t].at[2 * ox + oy], land_refs[t].at[k], send_sems, recv_sems, 3 * t + k,
                             (ox, oy, c))
                cp.wait_send()
                cp.wait_recv()

    outs = pl.pallas_call(
        body, name=name, out_shape=[pltpu.HBM(t.shape, t.dtype) for t in parts + lands],
        in_specs=[HBM] * (2 * n) + [SEM, SEM] + _dep_specs(after), out_specs=[HBM] * (2 * n),
        input_output_aliases={t: t for t in range(2 * n)},
        compiler_params=pltpu.CompilerParams(has_side_effects=EFFECT),
    )(*parts, *lands, *sems, *_dep_args(after))
    return list(outs[:n]), list(outs[n:])


def exchange_halves(groups, name):
    n_groups = len(groups)
    slots = [(gi, mi) for gi, grp in enumerate(groups) for mi in range(grp.shape[0])]

    def body(*refs):
        out_refs = refs[n_groups:2 * n_groups]
        send_sems, recv_sems = refs[2 * n_groups:]
        x, y, c, _ = _place()
        sibling = (x, y, 1 - c)
        copies = []
        for t, (gi, mi) in enumerate(slots):
            mine = out_refs[gi].at[mi, c]
            cp = _remote(mine, mine, send_sems, recv_sems, t, sibling)
            cp.start()
            copies.append(cp)
        for t, (gi, mi) in enumerate(slots):
            theirs = out_refs[gi].at[mi, 1 - c]
            _remote(theirs, theirs, send_sems, recv_sems, t, sibling).wait_recv()
        for cp in copies:
            cp.wait_send()

    return pl.pallas_call(
        body, name=name, in_specs=[ANY] * n_groups, out_specs=[ANY] * n_groups,
        out_shape=[jax.ShapeDtypeStruct(g.shape, g.dtype) for g in groups],
        input_output_aliases={gi: gi for gi in range(n_groups)},
        scratch_shapes=[pltpu.SemaphoreType.DMA((len(slots),)), pltpu.SemaphoreType.DMA((len(slots),))],
    )(*groups)


def gather_devices(buf):
    def body(in_ref, out_ref, send_sems, recv_sems, local_sem):
        x, y, c, _ = _place()
        me = 4 * x + 2 * y + c
        local = pltpu.make_async_copy(in_ref, out_ref.at[me], local_sem)
        local.start()
        copies = []
        for k in range(1, N_DEV):
            fx, fy, fc = (k >> 2) & 1, (k >> 1) & 1, k & 1
            peer = (x ^ fx, y ^ fy, c ^ fc)
            cp = _remote(in_ref, out_ref.at[me], send_sems, recv_sems, k - 1, peer)
            cp.start()
            copies.append(cp)
        for k in range(1, N_DEV):
            fx, fy, fc = (k >> 2) & 1, (k >> 1) & 1, k & 1
            theirs = out_ref.at[4 * (x ^ fx) + 2 * (y ^ fy) + (c ^ fc)]
            _remote(theirs, theirs, send_sems, recv_sems, k - 1, (x, y, c)).wait_recv()
        for cp in copies:
            cp.wait_send()
        local.wait()

    return pl.pallas_call(
        body, name="gather_devices", in_specs=[ANY], out_specs=ANY,
        out_shape=jax.ShapeDtypeStruct((N_DEV,) + buf.shape, buf.dtype),
        scratch_shapes=[pltpu.SemaphoreType.DMA((N_DEV - 1,)), pltpu.SemaphoreType.DMA((N_DEV - 1,)),
                        pltpu.SemaphoreType.DMA],
    )(buf)


class GradReducer:
    def __init__(self, core, chip, kinds):
        self.core, self.chip = core, chip
        self.sizes = dict(kinds)
        self.groups = {kind: None for kind, _ in kinds}

    def send(self, grad, kind, mi, tag):
        array = grad.reshape(N_CHIPS, 2, -1, grad.shape[-1])
        sems, arrays, lands, token = sibling_start([array], f"reduce_sibling_start_{tag}")
        return (sems, arrays, lands, kind, mi, tag), token

    def begin(self, sent, after, tag):
        parts, slots = [], []
        for sems, arrays, lands, kind, mi, sent_tag in sent:
            arrays, lands = sibling_wait(arrays, lands, sems, after, f"reduce_sibling_wait_{sent_tag}")
            parts.append(add_sibling(arrays[0], lands[0], self.core, f"reduce_add_sibling_{sent_tag}"))
            slots.append((kind, mi))
        sems, parts, lands, token = reduce_start(parts, f"reduce_start_{tag}")
        return (sems, parts, lands, slots, tag), token

    def end(self, state, after):
        sems, parts, lands, slots, tag = state
        parts, lands = reduce_wait(parts, lands, sems, after, f"reduce_wait_{tag}")
        for t, (kind, mi) in enumerate(slots):
            self.groups[kind] = add_chips(parts[t], lands[t], self.chip, self.core, self.groups[kind],
                                          self.sizes[kind], mi, f"reduce_add_chips_{tag}_{t}")

    def finish(self):
        kinds = list(self.groups)
        return dict(zip(kinds, exchange_halves([self.groups[k] for k in kinds], "reduce_swap")))


def _ffn_fwd(x, gain, wg, wu, wd, tag):
    h, rstd = rmsnorm_fwd(x, gain, BF16, f"ffn_norm_{tag}")
    gate, up = ffn_gateup(h, wg, wu, f"ffn_gateup_{tag}")
    out = mm_residual([gate, up], wd, x, 0.5, wd.shape[0] // N_CHIPS, f"ffn_down_{tag}", tm_target=704)
    return out, (x, gain, h, rstd, gate, up)


def _ffn_bwd(dout, saved, wg, wu, wd, index, reducer, dep=None, per_tensor=False):
    x, gain, h, rstd, gate, up = saved
    D = x.shape[1]
    Fs = wg.shape[2]
    td = _tile(D, 512, 128)
    tag = f"ffn{index}"
    begun = []

    def begin(sent, after, suffix):
        state, token = reducer.begin(sent, after, tag + suffix)
        begun.append(state)
        return token

    dgate, dup, act = ffn_bwd_act(dout, wd, gate, up, f"ffn_bwd_act_{index}", dep=dep)
    d_wd = mm_tn(act, dout, Fs, td, f"ffn_bwd_wd_{index}", b_scale=0.5)
    sent_d, tok = reducer.send(d_wd, "down", index, tag + "d")
    d_wg = mm_tn(h, dgate, td, Fs, f"ffn_bwd_wg_{index}", stacked_out=True, dep=tok)
    toks = [begin([sent_d], d_wg, "d")] if per_tensor else []
    sent_g, tok = reducer.send(d_wg, "gate", index, tag + "g")
    d_wu = mm_tn(h, dup, td, Fs, f"ffn_bwd_wu_{index}", stacked_out=True, dep=toks + [tok])
    toks = [begin([sent_g], d_wu, "g")] if per_tensor else []
    sent_u, tok = reducer.send(d_wu, "up", index, tag + "u")
    dh = mm_nt([(dgate, wg), (dup, wu)], _tile(x.shape[0], 704), D, Fs, f"ffn_bwd_dh_{index}", stacked_w=True,
               dep=toks + [tok])
    tok = begin([sent_u] if per_tensor else [sent_d, sent_g, sent_u], dh, "u")
    dx, dgain = rmsnorm_bwd(dh, x, gain, rstd, dout, f"ffn_norm_bwd_{index}")
    return dx, dgain, begun, tok


def kernel(x, meta, ffn_norm, ffn_w_gate, ffn_w_up, ffn_w_down, gla_norm, gla_w_in, gla_w_lr, gla_b_lr, gla_head_norm, gla_w_out, pool_norm, pool_w, pool_b, pool_scale, final_norm, loss_target, m_meta, m_ffn_norm, m_ffn_w_gate, m_ffn_w_up, m_ffn_w_down, m_gla_norm, m_gla_w_in, m_gla_w_lr, m_gla_b_lr, m_gla_head_norm, m_gla_w_out, m_pool_norm, m_pool_w, m_pool_b, m_pool_scale, m_final_norm, v_meta, v_ffn_norm, v_ffn_w_gate, v_ffn_w_up, v_ffn_w_down, v_gla_norm, v_gla_w_in, v_gla_w_lr, v_gla_b_lr, v_gla_head_norm, v_gla_w_out, v_pool_norm, v_pool_w, v_pool_b, v_pool_scale, v_final_norm):
    S, D = x.shape[1], x.shape[2]
    M = OFF + S
    Dq = D // N_CHIPS
    Fs = ffn_w_gate.shape[3]
    F = N_CHIPS * Fs
    dk = D // 2
    n_in = gla_w_in.shape[2]
    W = D // 4
    core = lax.axis_index("c").astype(jnp.int32).reshape(1)
    chip_id = 2 * lax.axis_index("x") + lax.axis_index("y")
    chip = chip_id.astype(jnp.int32).reshape(1)

    small = jnp.concatenate([_pad_rows(t) for t in (
        meta, ffn_norm.reshape(4, Dq), gla_w_lr.reshape(8, Dq), pool_norm, pool_b.reshape(1, Dq), pool_scale)],
        axis=0)
    def stage(w, kind, n, mi, dep=None):
        return stage_shard(w.reshape(n, 2, -1, w.shape[-1]), mi, chip, f"stage_{kind}_{mi}", dep=dep)

    ffn_stage = lambda mi, dep=None: [stage(ffn_w_gate, "gate", 4, mi, dep), stage(ffn_w_up, "up", 4, mi, dep),
                                      stage(ffn_w_down, "down", 4, mi, dep)]
    small_stage = lax.dynamic_update_slice(jnp.zeros((N_CHIPS,) + small.shape, F32), small[None], (chip_id, 0, 0))
    first = ffn_stage(0)
    buckets = [first[:2] + [small_stage], first[2:]]
    sizes = [len(b) for b in buckets]
    gather_sems, in_flight, tok = gather_start([t for b in buckets for t in b], sizes, "gather_start_first")
    buckets = [[stage(gla_w_in, "win", 1, 0, tok), stage(gla_w_out, "wout", 1, 0, tok)],
               ffn_stage(1, tok), ffn_stage(2, tok), [stage(pool_w, "pool", 1, 0, tok)] + ffn_stage(3, tok)]
    more_sems, more_in_flight, gather_token = gather_start([t for b in buckets for t in b],
                                                            [len(b) for b in buckets], "gather_start_rest")
    sizes += [len(b) for b in buckets]
    gather_sems += more_sems
    in_flight += more_in_flight
    starts = [sum(sizes[:b]) for b in range(len(sizes))]

    def arrive(b, after, n_big):
        bufs = gather_wait(in_flight[starts[b]:starts[b] + sizes[b]], gather_sems[b], after, f"gather_wait_{b}")
        return forward_to_sibling(bufs[:n_big], f"gather_forward_{b}") + bufs[n_big:]

    ffn_w = lambda t: (t[0].reshape(N_CHIPS, D, Fs), t[1].reshape(N_CHIPS, D, Fs), t[2].reshape(F, D))
    got = arrive(0, gather_token, 2)
    wg, wu, wd = [None] * 4, [None] * 4, [None] * 4
    wg[0], wu[0] = got[0].reshape(N_CHIPS, D, Fs), got[1].reshape(N_CHIPS, D, Fs)
    sm = got[2]
    unshard = lambda t: t.transpose(1, 0, 2).reshape(t.shape[1], D)
    meta_f = unshard(sm[:, 0:16])
    ffn_norm_f = unshard(sm[:, 16:20])
    w_lr_f = sm[:, 24:32].reshape(N_CHIPS, GATE_RANK, dk // N_CHIPS).transpose(1, 0, 2).reshape(GATE_RANK, dk)
    pool_norm_f = sm[:, 32].reshape(1, D)
    pool_b_f = sm[:, 40].reshape(N_CHIPS, 4, W // N_CHIPS).transpose(1, 0, 2).reshape(1, D)
    pool_scale_f = sm[:, 48].reshape(1, D)
    wlr_pad = jnp.pad(w_lr_f.astype(BF16), ((0, LR_W - GATE_RANK), (0, 0)))
    final_g = final_norm.reshape(1, D)
    qkv = 2 * dk + D

    x0 = jnp.concatenate([jnp.zeros((PAD, D), F32), meta_f, x[0]], axis=0)
    target = jnp.pad(loss_target[0], ((OFF, 0), (0, 0)))
    h0, rstd0 = rmsnorm_fwd(x0, ffn_norm_f[0:1], BF16, "ffn_norm_0")
    gate0, up0 = ffn_gateup(h0, wg[0], wu[0], "ffn_gateup_0")
    wd[0] = arrive(1, gate0, 1)[0].reshape(F, D)
    x1 = mm_residual([gate0, up0], wd[0], x0, 0.5, Fs, "ffn_down_0", tm_target=704)
    ffn0 = (x0, ffn_norm_f[0:1], h0, rstd0, gate0, up0)
    got = arrive(2, x1, 2)
    w_in = got[0].reshape(N_CHIPS, D, n_in).transpose(1, 0, 2).reshape(D, N_CHIPS * n_in)
    w_out = got[1].reshape(D, D)
    w_all = jnp.concatenate([w_in[:, :qkv], w_in[:, qkv + GATE_RANK:], w_in[:, qkv:qkv + GATE_RANK],
                             jnp.zeros((D, LR_W - GATE_RANK), BF16)], axis=1)
    hg, rstd_g = rmsnorm_fwd(x1, gla_norm, BF16, "gla_norm")
    proj = mm_nn(hg, w_all, F32, "gla_proj")
    o, st = gla_fwd(proj, wlr_pad, gla_b_lr, D)
    gated = gla_post_fwd(o, proj, gla_head_norm, D)
    x2 = mm_residual([gated], w_out, x1, 1.0, D, "gla_out")
    wg[1], wu[1], wd[1] = ffn_w(arrive(3, x2, 3))
    x3, ffn1 = _ffn_fwd(x2, ffn_norm_f[1:2], wg[1], wu[1], wd[1], "1")
    wg[2], wu[2], wd[2] = ffn_w(arrive(4, x3, 3))
    x4, ffn2 = _ffn_fwd(x3, ffn_norm_f[2:3], wg[2], wu[2], wd[2], "2")
    got = arrive(5, x4, 4)
    w_pool = got[0].reshape(N_CHIPS, 4, W // N_CHIPS, W).transpose(1, 0, 2, 3).reshape(4, W, W)
    wg[3], wu[3], wd[3] = ffn_w(got[1:])
    hp, rstd_p = rmsnorm_fwd(x4, pool_norm_f, F32, "pool_norm")
    pooled = pool_window(hp)
    x5 = pool_mix(pooled, x4, w_pool, pool_b_f, pool_scale_f)
    x6, ffn3 = _ffn_fwd(x5, ffn_norm_f[3:4], wg[3], wu[3], wd[3], "3")
    loss, dx6, d_final = final_loss(x6, final_g, target)

    reducer = GradReducer(core, chip, [("gate", 4), ("up", 4), ("down", 4), ("win", 1), ("wout", 1), ("pool", 1)])

    def settle(begun, after):
        for state in begun:
            reducer.end(state, after)

    dx5, dn3, red3, tok = _ffn_bwd(dx6, ffn3, wg[3], wu[3], wd[3], 3, reducer)
    dpooled, d_wpool, d_pool_b, d_pool_scale = pool_mix_bwd(dx5, pooled, w_pool, pool_b_f, pool_scale_f, dep=tok)
    dhp = pool_window_bwd(dpooled)
    dx4, d_pool_norm = rmsnorm_bwd(dhp, x4, pool_norm_f, rstd_p, dx5, "pool_norm_bwd")
    d_wpool = d_wpool.reshape(4, N_CHIPS, W // N_CHIPS, W).transpose(1, 0, 2, 3)
    sent_p, tok = reducer.send(d_wpool, "pool", 0, "pool")
    dx3, dn2, red2, tok = _ffn_bwd(dx4, ffn2, wg[2], wu[2], wd[2], 2, reducer, dep=tok)
    redp, tok_p = reducer.begin([sent_p], dx3, "pool")
    dx2, dn1, red1, tok = _ffn_bwd(dx3, ffn1, wg[1], wu[1], wd[1], 1, reducer, dep=[tok, tok_p])
    tm = _tile(M, 352)
    td = _tile(D, 512, 128)
    d_wout = mm_tn(gated, dx2, td, td, "gla_out_bwd_w", dep=tok)
    sent_o, tok = reducer.send(d_wout, "wout", 0, "wout")
    dgated = mm_nt([(dx2, w_out)], tm, td, D, "gla_out_bwd_act", dep=tok)
    redo, tok_o = reducer.begin([sent_o], dgated, "wout")
    do, dr, d_head_norm = gla_post_bwd(dgated, o, proj, gla_head_norm, D)
    dq, dkk, dv, dlr, dwlr, dblr = gla_bwd(proj, wlr_pad, gla_b_lr, st, do, D)
    dproj = jnp.concatenate([dq, dkk, dv, dr, dlr.astype(BF16)], axis=1)
    tp = _tile(proj.shape[1], 896, 128)
    d_wall = mm_tn(hg, dproj, td, tp, "gla_proj_bwd_w", dep=tok_o)
    d_win = jnp.concatenate([d_wall[:, :qkv], d_wall[:, qkv + D:qkv + D + GATE_RANK], d_wall[:, qkv:qkv + D]], axis=1)
    d_win = d_win.reshape(D, N_CHIPS, n_in).transpose(1, 0, 2)
    sent_i, tok = reducer.send(d_win, "win", 0, "win")
    dhg = mm_nt([(dproj, w_all)], tm, D, tp, "gla_proj_bwd_act", dep=tok)
    redi, tok = reducer.begin([sent_i], dhg, "win")
    dx1, d_gla_norm = rmsnorm_bwd(dhg, x1, gla_norm, rstd_g, dx2, "gla_norm_bwd")
    dx0, dn0, red0, tok = _ffn_bwd(dx1, ffn0, wg[0], wu[0], wd[0], 0, reducer, dep=tok, per_tensor=True)
    settle(red3 + [redp] + red2 + red1 + [redo, redi] + red0[:-1], tok)

    d_wlr = dwlr[:, :GATE_RANK].transpose(1, 0, 2).reshape(GATE_RANK, dk)
    pieces = [dx0[PAD:OFF], dn0, dn1, dn2, dn3, d_gla_norm, d_wlr,
              dblr.reshape(1, dk), d_head_norm, d_pool_norm, d_pool_b, d_pool_scale, d_final]
    packed = jnp.concatenate([_pad_rows(p.reshape(-1, Dq)) for p in pieces], axis=0)
    total = sum_devices(gather_devices(packed))

    settle(red0[-1:], [total] + list(reducer.groups.values()))
    reduced = reducer.finish()
    g_gate = reduced["gate"].reshape(ffn_w_gate.shape)
    g_up = reduced["up"].reshape(ffn_w_up.shape)
    g_down = reduced["down"].reshape(ffn_w_down.shape)
    g_win = reduced["win"].reshape(gla_w_in.shape)
    g_wout = reduced["wout"].reshape(gla_w_out.shape)
    g_wpool = reduced["pool"].reshape(pool_w.shape)
    sums, at = [], 0
    for p in pieces:
        r = p.size // Dq
        sums.append(total[at:at + r].reshape(p.shape))
        at += r + (-r % 8)
    (s_meta, s_n0, s_n1, s_n2, s_n3, s_gla_norm, s_wlr, s_blr, s_head_norm, s_pool_norm, s_pool_b, s_pool_scale,
     s_final) = sums
    s_ffn_norm = jnp.stack([s_n0, s_n1, s_n2, s_n3], axis=0)[:, 0]
    mine = lambda t, width: lax.dynamic_slice_in_dim(t, chip_id * width, width, axis=t.ndim - 1)
    g_meta = mine(s_meta, Dq)
    g_ffn_norm = mine(s_ffn_norm, Dq).reshape(ffn_norm.shape)
    g_gla_norm = s_gla_norm
    g_wlr = mine(s_wlr, dk // N_CHIPS).reshape(gla_w_lr.shape)
    g_blr = s_blr
    g_head_norm = s_head_norm
    g_pool_norm = mine(s_pool_norm, Dq)
    g_pool_b = mine(s_pool_b.reshape(4, W), W // N_CHIPS).reshape(pool_b.shape)
    g_pool_scale = mine(s_pool_scale, Dq)
    g_final = s_final.reshape(final_norm.shape)

    weights = [meta, ffn_norm, ffn_w_gate, ffn_w_up, ffn_w_down, gla_norm, gla_w_in, gla_w_lr, gla_b_lr,
               gla_head_norm, gla_w_out, pool_norm, pool_w, pool_b, pool_scale, final_norm]
    moments_m = [m_meta, m_ffn_norm, m_ffn_w_gate, m_ffn_w_up, m_ffn_w_down, m_gla_norm, m_gla_w_in, m_gla_w_lr,
                 m_gla_b_lr, m_gla_head_norm, m_gla_w_out, m_pool_norm, m_pool_w, m_pool_b, m_pool_scale,
                 m_final_norm]
    moments_v = [v_meta, v_ffn_norm, v_ffn_w_gate, v_ffn_w_up, v_ffn_w_down, v_gla_norm, v_gla_w_in, v_gla_w_lr,
                 v_gla_b_lr, v_gla_head_norm, v_gla_w_out, v_pool_norm, v_pool_w, v_pool_b, v_pool_scale,
                 v_final_norm]
    grads_w = [g_meta, g_ffn_norm, g_gate, g_up, g_down, g_gla_norm, g_win, g_wlr, g_blr, g_head_norm, g_wout,
               g_pool_norm, g_wpool, g_pool_b, g_pool_scale, g_final]
    from_swap = {2, 3, 4, 6, 10, 12}
    deltas, new_m, new_v = [], [], []
    for i, (w, g, m, v) in enumerate(zip(weights, grads_w, moments_m, moments_v)):
        outs = adamw(w, g, m, v, f"adamw_{i}", copy_g=i in from_swap)
        deltas.append(outs[0])
        new_m.append(outs[1])
        new_v.append(outs[2])
        if i in from_swap:
            grads_w[i] = outs[3]

    loss = lax.psum(loss[0, 0], ("x", "y", "c"))
    grad_x = dx0[OFF:][None]
    return (loss, grad_x, *grads_w, *deltas, *new_m, *new_v)
```

```python
import functools

import jax
import jax.numpy as jnp
from jax import lax
from jax.experimental import pallas as pl
from jax.experimental.pallas import tpu as pltpu

F32 = jnp.float32
BF16 = jnp.bfloat16
MESH = pl.DeviceIdType.MESH
ANY = pl.BlockSpec(memory_space=pl.ANY)

N_META = 16
CHUNK = 64
PAD = CHUNK - N_META
OFF = PAD + N_META
EPS = 1e-6
HEADS = 4
GATE_RANK = 16
GATE_NORM = 16.0
LR_W = 128
N_CHIPS = 4
N_DEV = 8
ADAM_LR, ADAM_B1, ADAM_B2, ADAM_EPS, ADAM_WD, ADAM_STEP = 0.001, 0.9, 0.999, 1e-08, 0.01, 10
VMEM_LIMIT = 56 * 1024 * 1024
ROW_TILE = 176
ONE_BUFFER = pl.Buffered(1)


def _tile(n, target, mult=16):
    best = None
    for d in range(mult, min(n, target) + 1, mult):
        if n % d == 0:
            best = d
    return best if best is not None else n


def _params(*sem):
    return pltpu.CompilerParams(dimension_semantics=sem, vmem_limit_bytes=VMEM_LIMIT)


def _dot(a, b):
    return jnp.dot(a, b, preferred_element_type=F32)


def _dot_nt(a, b):
    return lax.dot_general(a, b, (((1,), (1,)), ((), ())), preferred_element_type=F32)


def _dot_tn(a, b):
    return lax.dot_general(a, b, (((0,), (0,)), ((), ())), preferred_element_type=F32)


MXU_WIDTH = 256


def _chunks(n):
    return [slice(lo, min(lo + MXU_WIDTH, n)) for lo in range(0, n, MXU_WIDTH)]


def _sigmoid(x):
    return 1.0 / (1.0 + jnp.exp(-x))


def _rows(tile, width=1):
    return lax.broadcasted_iota(jnp.int32, (tile, width), 0)


def _dep_args(dep):
    if dep is None:
        return []
    return list(dep) if isinstance(dep, (list, tuple)) else [dep]


def _dep_specs(dep):
    return [ANY] * len(_dep_args(dep))


def _pad_rows(t):
    return jnp.pad(t, ((0, -t.shape[0] % 8), (0, 0)))


def rmsnorm_fwd(x, g, out_dtype, name):
    M, D = x.shape
    tr = _tile(M, ROW_TILE)

    def body(x_ref, g_ref, h_ref, r_ref):
        xv = x_ref[...]
        r = lax.rsqrt(jnp.mean(xv * xv, axis=-1, keepdims=True) + EPS)
        h_ref[...] = (xv * r * g_ref[...]).astype(out_dtype)
        r_ref[...] = r

    return pl.pallas_call(
        body, name=name, grid=(M // tr,),
        in_specs=[pl.BlockSpec((tr, D), lambda i: (i, 0)), pl.BlockSpec((1, D), lambda i: (0, 0))],
        out_specs=[pl.BlockSpec((tr, D), lambda i: (i, 0)), pl.BlockSpec((tr, 1), lambda i: (i, 0))],
        out_shape=[jax.ShapeDtypeStruct((M, D), out_dtype), jax.ShapeDtypeStruct((M, 1), F32)],
        compiler_params=_params("parallel"),
    )(x, g)


def rmsnorm_bwd(dh, x, g, rstd, dres, name):
    M, D = x.shape
    tr = _tile(M, ROW_TILE)

    def body(dh_ref, x_ref, g_ref, r_ref, dres_ref, dx_ref, dg_ref):
        @pl.when(pl.program_id(0) == 0)
        def _():
            dg_ref[...] = jnp.zeros_like(dg_ref)

        r = r_ref[...]
        xhat = x_ref[...] * r
        dhv = dh_ref[...]
        gd = dhv * g_ref[...]
        dx_ref[...] = dres_ref[...] + r * (gd - xhat * jnp.mean(gd * xhat, axis=-1, keepdims=True))
        dg_ref[...] += jnp.sum(dhv * xhat, axis=0, keepdims=True)

    row = pl.BlockSpec((tr, D), lambda i: (i, 0))
    vec = pl.BlockSpec((1, D), lambda i: (0, 0))
    return pl.pallas_call(
        body, name=name, grid=(M // tr,),
        in_specs=[row, row, vec, pl.BlockSpec((tr, 1), lambda i: (i, 0)), row],
        out_specs=[row, vec],
        out_shape=[jax.ShapeDtypeStruct((M, D), F32), jax.ShapeDtypeStruct((1, D), F32)],
        compiler_params=_params("arbitrary"),
    )(dh, x, g, rstd, dres)


def final_loss(x, g, target):
    M, D = x.shape
    tr = _tile(M, ROW_TILE)

    def body(x_ref, g_ref, t_ref, loss_ref, dx_ref, dg_ref):
        i = pl.program_id(0)

        @pl.when(i == 0)
        def _():
            loss_ref[...] = jnp.zeros_like(loss_ref)
            dg_ref[...] = jnp.zeros_like(dg_ref)

        live = (_rows(tr) + i * tr) >= OFF
        xv = x_ref[...]
        gv = g_ref[...]
        r = lax.rsqrt(jnp.mean(xv * xv, axis=-1, keepdims=True) + EPS)
        xhat = xv * r
        err = jnp.where(live, xhat * gv - t_ref[...], 0.0)
        loss_ref[...] += 0.5 * jnp.sum(jnp.mean(err * err, axis=-1, keepdims=True), axis=0, keepdims=True)
        dy = err * (1.0 / D)
        gd = dy * gv
        dx_ref[...] = r * (gd - xhat * jnp.mean(gd * xhat, axis=-1, keepdims=True))
        dg_ref[...] += jnp.sum(dy * xhat, axis=0, keepdims=True)

    row = pl.BlockSpec((tr, D), lambda i: (i, 0))
    vec = pl.BlockSpec((1, D), lambda i: (0, 0))
    return pl.pallas_call(
        body, name="final_loss", grid=(M // tr,),
        in_specs=[row, vec, row],
        out_specs=[pl.BlockSpec((1, 1), lambda i: (0, 0)), row, vec],
        out_shape=[jax.ShapeDtypeStruct((1, 1), F32), jax.ShapeDtypeStruct((M, D), F32),
                   jax.ShapeDtypeStruct((1, D), F32)],
        compiler_params=_params("arbitrary"),
    )(x, g, target)


def mm_nn(a, w, out_dtype, name, tm_target=704, tn_target=896):
    M, K = a.shape
    N = w.shape[1]
    tm, tn = _tile(M, tm_target), _tile(N, tn_target, 128)

    def body(a_ref, w_ref, o_ref):
        o_ref[...] = _dot(a_ref[...], w_ref[...]).astype(out_dtype)

    return pl.pallas_call(
        body, name=name, grid=(N // tn, M // tm),
        in_specs=[pl.BlockSpec((tm, K), lambda n, i: (i, 0)), pl.BlockSpec((K, tn), lambda n, i: (0, n))],
        out_specs=pl.BlockSpec((tm, tn), lambda n, i: (i, n)),
        out_shape=jax.ShapeDtypeStruct((M, N), out_dtype),
        compiler_params=_params("parallel", "parallel"),
    )(a, w)


def ffn_gateup(h, wg, wu, name):
    M, D = h.shape
    Fs = wg.shape[2]
    tm = _tile(M, 352)

    def body(h_ref, wg_ref, wu_ref, u_ref, silu_ref, dsilu_ref, a_ref):
        hv = h_ref[...]
        for cols in _chunks(Fs):
            g = _dot(hv, wg_ref[:, cols])
            u = _dot(hv, wu_ref[:, cols])
            s = _sigmoid(g)
            silu = g * s
            u_ref[:, cols] = u.astype(BF16)
            silu_ref[:, cols] = silu.astype(BF16)
            dsilu_ref[:, cols] = (s * (1.0 + g * (1.0 - s))).astype(BF16)
            a_ref[:, cols] = (silu * u).astype(BF16)

    wspec = pl.BlockSpec((None, D, Fs), lambda j, i: (j, 0, 0))
    ospec = pl.BlockSpec((tm, Fs), lambda j, i: (i, j))
    return pl.pallas_call(
        body, name=name, grid=(N_CHIPS, M // tm),
        in_specs=[pl.BlockSpec((tm, D), lambda j, i: (i, 0)), wspec, wspec],
        out_specs=[ospec] * 4,
        out_shape=[jax.ShapeDtypeStruct((M, N_CHIPS * Fs), BF16)] * 4,
        compiler_params=_params("parallel", "parallel"),
    )(h, wg, wu)


def mm_residual(a, w, x, scale, tk, name, tm_target=352):
    M, N = x.shape
    K = w.shape[0]
    tm = _tile(M, tm_target)

    def body(a_ref, w_ref, x_ref, o_ref, acc):
        k = pl.program_id(1)

        @pl.when(k == 0)
        def _():
            acc[...] = jnp.zeros_like(acc)

        acc[...] += _dot(a_ref[...], w_ref[...])

        @pl.when(k == pl.num_programs(1) - 1)
        def _():
            o_ref[...] = x_ref[...] + scale * acc[...]

    aspec = pl.BlockSpec((tm, tk), lambda i, k: (i, k))
    return pl.pallas_call(
        body, name=name, grid=(M // tm, K // tk),
        in_specs=[aspec, pl.BlockSpec((tk, N), lambda i, k: (k, 0)),
                  pl.BlockSpec((tm, N), lambda i, k: (i, 0), pipeline_mode=ONE_BUFFER)],
        out_specs=pl.BlockSpec((tm, N), lambda i, k: (i, 0), pipeline_mode=ONE_BUFFER),
        out_shape=jax.ShapeDtypeStruct((M, N), F32),
        scratch_shapes=[pltpu.VMEM((tm, N), F32)],
        compiler_params=_params("parallel", "arbitrary"),
    )(a, w, x)


def ffn_bwd_act(dout, wd, up, silu, dsilu, name, dep=None):
    M, D = dout.shape
    F = wd.shape[0]
    Fs = F // N_CHIPS
    tm = _tile(M, 352)

    def body(dy_ref, wd_ref, u_ref, silu_ref, dsilu_ref, *rest):
        dg_ref, du_ref = rest[-2:]
        dy = (0.5 * dy_ref[...]).astype(BF16)
        for cols in _chunks(Fs):
            da = _dot_nt(dy, wd_ref[cols, :])
            dg_ref[:, cols] = (da * u_ref[:, cols].astype(F32) * dsilu_ref[:, cols].astype(F32)).astype(BF16)
            du_ref[:, cols] = (da * silu_ref[:, cols].astype(F32)).astype(BF16)

    fspec = pl.BlockSpec((tm, Fs), lambda j, i: (i, j))
    return pl.pallas_call(
        body, name=name, grid=(N_CHIPS, M // tm),
        in_specs=[pl.BlockSpec((tm, D), lambda j, i: (i, 0)), pl.BlockSpec((Fs, D), lambda j, i: (j, 0)),
                  fspec, fspec, fspec] + _dep_specs(dep),
        out_specs=[fspec, fspec],
        out_shape=[jax.ShapeDtypeStruct((M, F), BF16)] * 2,
        compiler_params=_params("parallel", "parallel"),
    )(dout, wd, up, silu, dsilu, *_dep_args(dep))


def mm_tn(a, b, ta, tb, name, b_scale=1.0, stacked_out=False, out_dtype=BF16, dep=None):
    T, Ma = a.shape
    Nb = b.shape[1]

    def body(a_ref, b_ref, *rest):
        o_ref = rest[-1]
        bv = b_ref[...]
        if b_scale != 1.0:
            bv = b_scale * bv
        o_ref[...] = _dot_tn(a_ref[...], bv.astype(BF16)).astype(out_dtype)

    if stacked_out:
        out_spec = pl.BlockSpec((None, ta, tb), lambda jb, ja: (jb, ja, 0))
        out_shape = jax.ShapeDtypeStruct((Nb // tb, Ma, tb), out_dtype)
    else:
        out_spec = pl.BlockSpec((ta, tb), lambda jb, ja: (ja, jb))
        out_shape = jax.ShapeDtypeStruct((Ma, Nb), out_dtype)
    return pl.pallas_call(
        body, name=name, grid=(Nb // tb, Ma // ta),
        in_specs=[pl.BlockSpec((T, ta), lambda jb, ja: (0, ja)), pl.BlockSpec((T, tb), lambda jb, ja: (0, jb))]
        + _dep_specs(dep),
        out_specs=out_spec, out_shape=out_shape,
        compiler_params=_params("parallel", "parallel"),
    )(a, b, *_dep_args(dep))


def mm_nt(pairs, tm, tn, tk, name, a_scale=1.0, stacked_w=False, dep=None):
    M, K = pairs[0][0].shape
    N = pairs[0][1].shape[1] if stacked_w else pairs[0][1].shape[0]
    n_pairs = len(pairs)

    def body(*refs):
        o_ref, acc = refs[-2:]
        k = pl.program_id(2)

        @pl.when(k == 0)
        def _():
            acc[...] = jnp.zeros_like(acc)

        for p in range(n_pairs):
            av = refs[2 * p][...]
            if a_scale != 1.0:
                av = a_scale * av
            acc[...] += _dot_nt(av.astype(BF16), refs[2 * p + 1][...])

        @pl.when(k == pl.num_programs(2) - 1)
        def _():
            o_ref[...] = acc[...]

    aspec = pl.BlockSpec((tm, tk), lambda i, n, k: (i, k))
    if stacked_w:
        wspec = pl.BlockSpec((None, tn, tk), lambda i, n, k: (k, n, 0))
    else:
        wspec = pl.BlockSpec((tn, tk), lambda i, n, k: (n, k))
    return pl.pallas_call(
        body, name=name, grid=(M // tm, N // tn, K // tk),
        in_specs=[aspec, wspec] * n_pairs + _dep_specs(dep),
        out_specs=pl.BlockSpec((tm, tn), lambda i, n, k: (i, n), pipeline_mode=ONE_BUFFER),
        out_shape=jax.ShapeDtypeStruct((M, N), F32),
        scratch_shapes=[pltpu.VMEM((tm, tn), F32)],
        compiler_params=_params("parallel", "parallel", "arbitrary"),
    )(*[t for pair in pairs for t in pair], *_dep_args(dep))


def _tri(lower):
    r = lax.broadcasted_iota(jnp.int32, (CHUNK, CHUNK), 0)
    c = lax.broadcasted_iota(jnp.int32, (CHUNK, CHUNK), 1)
    return (r >= c) if lower else (r <= c)


def _tri_sum(mask, x, pieces):
    ones = mask.astype(BF16)
    acc = jnp.zeros_like(x)
    rest = x
    for _ in range(pieces):
        piece = rest.astype(BF16)
        acc = acc + _dot(ones, piece)
        rest = rest - piece.astype(F32)
    return acc


def _gla_gates(lr, wlr, blr, chunk):
    z = _dot(lr, wlr) + blr
    live = (_rows(CHUNK) + chunk * CHUNK) >= PAD
    lg = jnp.where(live, (jnp.minimum(z, 0.0) - jnp.log(1.0 + jnp.exp(-jnp.abs(z)))) * (1.0 / GATE_NORM), 0.0)
    b = _tri_sum(_tri(True), lg, 3)
    b_last = jnp.sum(lg, axis=0, keepdims=True)
    b_mid = jnp.sum(jnp.where(_rows(CHUNK) < CHUNK // 2, lg, 0.0), axis=0, keepdims=True)
    return z, live, b, b_last, b_mid


def _gla_specs(D, chunk_of):
    lr_blk = (3 * D) // LR_W
    return [
        pl.BlockSpec((CHUNK, D // 2), lambda c: (chunk_of(c), 0)),
        pl.BlockSpec((CHUNK, D // 2), lambda c: (chunk_of(c), 1)),
        pl.BlockSpec((CHUNK, D), lambda c: (chunk_of(c), 1)),
        pl.BlockSpec((CHUNK, LR_W), lambda c: (chunk_of(c), lr_blk)),
        pl.BlockSpec((LR_W, D // 2), lambda c: (0, 0)),
        pl.BlockSpec((1, D // 2), lambda c: (0, 0)),
    ]


def gla_fwd(proj, wlr, blr, D):
    M = proj.shape[0]
    n = M // CHUNK
    dkh, dvh = D // 2 // HEADS, D // HEADS
    qscale = float(dkh) ** -0.5

    def body(q_ref, k_ref, v_ref, lr_ref, wlr_ref, blr_ref, o_ref, st_ref, S):
        c = pl.program_id(0)

        @pl.when(c == 0)
        def _():
            S[...] = jnp.zeros_like(S)

        lr = lr_ref[...].astype(BF16)
        for h in range(HEADS):
            kc, vc = slice(h * dkh, (h + 1) * dkh), slice(h * dvh, (h + 1) * dvh)
            _, _, b, b_last, b_mid = _gla_gates(lr, wlr_ref[:, kc], blr_ref[:, kc], c)
            q = q_ref[:, kc] * qscale
            k = k_ref[:, kc]
            v = v_ref[:, vc].astype(BF16)
            s0 = S[h]
            st_ref[h] = s0
            qb = (q * jnp.exp(b)).astype(BF16)
            kb = (k * jnp.exp(b_last - b)).astype(BF16)
            qt = (q * jnp.exp(b - b_mid)).astype(BF16)
            kt = (k * jnp.exp(b_mid - b)).astype(BF16)
            a = jnp.where(_tri(True), _dot_nt(qt, kt), 0.0).astype(BF16)
            o_ref[:, vc] = _dot_nt(qb, s0.astype(BF16)) + _dot(a, v)
            S[h] = jnp.exp(b_last) * s0 + _dot_tn(v, kb)

    return pl.pallas_call(
        body, name="gla_fwd", grid=(n,),
        in_specs=_gla_specs(D, lambda c: c),
        out_specs=[pl.BlockSpec((CHUNK, D), lambda c: (c, 0)),
                   pl.BlockSpec((None, HEADS, dvh, dkh), lambda c: (c, 0, 0, 0))],
        out_shape=[jax.ShapeDtypeStruct((M, D), F32), jax.ShapeDtypeStruct((n, HEADS, dvh, dkh), F32)],
        scratch_shapes=[pltpu.VMEM((HEADS, dvh, dkh), F32)],
        compiler_params=_params("arbitrary"),
    )(proj, proj, proj, proj, wlr, blr)


def gla_bwd(proj, wlr, blr, st, do, D):
    M = proj.shape[0]
    n = M // CHUNK
    dkh, dvh = D // 2 // HEADS, D // HEADS
    qscale = float(dkh) ** -0.5
    rev = lambda c: n - 1 - c

    def body(q_ref, k_ref, v_ref, lr_ref, wlr_ref, blr_ref, st_ref, do_ref,
             dq_ref, dk_ref, dv_ref, dlr_ref, dwlr_ref, dblr_ref, dS):
        step = pl.program_id(0)
        c = n - 1 - step

        @pl.when(step == 0)
        def _():
            dS[...] = jnp.zeros_like(dS)
            dwlr_ref[...] = jnp.zeros_like(dwlr_ref)
            dblr_ref[...] = jnp.zeros_like(dblr_ref)

        lr = lr_ref[...].astype(BF16)
        lower = _tri(True)
        dlr = None
        for h in range(HEADS):
            kc, vc = slice(h * dkh, (h + 1) * dkh), slice(h * dvh, (h + 1) * dvh)
            wlr_h = wlr_ref[:, kc]
            z, live, b, b_last, b_mid = _gla_gates(lr, wlr_h, blr_ref[:, kc], c)
            q = q_ref[:, kc] * qscale
            k = k_ref[:, kc]
            v = v_ref[:, vc].astype(BF16)
            dov = do_ref[:, vc].astype(BF16)
            s0 = st_ref[h]
            ds1 = dS[h]
            ds1b = ds1.astype(BF16)
            e_b, e_lb = jnp.exp(b), jnp.exp(b_last - b)
            e_bm, e_mb = jnp.exp(b - b_mid), jnp.exp(b_mid - b)
            e_last = jnp.exp(b_last)
            qb, kb, qt, kt = q * e_b, k * e_lb, q * e_bm, k * e_mb
            qbb, kbb, qtb, ktb = qb.astype(BF16), kb.astype(BF16), qt.astype(BF16), kt.astype(BF16)
            a = jnp.where(lower, _dot_nt(qtb, ktb), 0.0).astype(BF16)
            da = jnp.where(lower, _dot_nt(dov, v), 0.0).astype(BF16)

            dqb = _dot(dov, s0.astype(BF16))
            dqt = _dot(da, ktb)
            dkt = _dot_tn(da, qtb)
            dkb = _dot(v, ds1b)
            keep = live.astype(F32)
            dv_ref[:, vc] = (keep * (_dot_tn(a, dov) + _dot_nt(kbb, ds1b))).astype(BF16)
            dq_ref[:, kc] = (keep * qscale * (dqb * e_b + dqt * e_bm)).astype(BF16)
            dk_ref[:, kc] = (keep * (dkb * e_lb + dkt * e_mb)).astype(BF16)

            db = dqb * qb - dkb * kb + dqt * qt - dkt * kt
            db_last = (jnp.sum(dkb * kb, axis=0, keepdims=True)
                       + jnp.sum(ds1 * s0, axis=0, keepdims=True) * e_last)
            db = db + jnp.where(_rows(CHUNK) == CHUNK - 1, db_last, 0.0)
            dlg = jnp.where(live, _tri_sum(_tri(False), db, 2), 0.0)
            dz = dlg * (1.0 / GATE_NORM) / (1.0 + jnp.exp(z))
            dzb = dz.astype(BF16)

            dlr_h = _dot_nt(dzb, wlr_h)
            dlr = dlr_h if dlr is None else dlr + dlr_h
            dwlr_ref[h] += _dot_tn(lr, dzb)
            dblr_ref[h] += jnp.sum(dz, axis=0, keepdims=True)
            dS[h] = e_last * ds1 + _dot_tn(dov, qbb)
        dlr_ref[...] = dlr

    return pl.pallas_call(
        body, name="gla_bwd", grid=(n,),
        in_specs=_gla_specs(D, rev) + [
            pl.BlockSpec((None, HEADS, dvh, dkh), lambda c: (rev(c), 0, 0, 0)),
            pl.BlockSpec((CHUNK, D), lambda c: (rev(c), 0))],
        out_specs=[pl.BlockSpec((CHUNK, D // 2), lambda c: (rev(c), 0)),
                   pl.BlockSpec((CHUNK, D // 2), lambda c: (rev(c), 0)),
                   pl.BlockSpec((CHUNK, D), lambda c: (rev(c), 0)),
                   pl.BlockSpec((CHUNK, LR_W), lambda c: (rev(c), 0)),
                   pl.BlockSpec((HEADS, LR_W, dkh), lambda c: (0, 0, 0)),
                   pl.BlockSpec((HEADS, 1, dkh), lambda c: (0, 0, 0))],
        out_shape=[jax.ShapeDtypeStruct((M, D // 2), BF16), jax.ShapeDtypeStruct((M, D // 2), BF16),
                   jax.ShapeDtypeStruct((M, D), BF16), jax.ShapeDtypeStruct((M, LR_W), F32),
                   jax.ShapeDtypeStruct((HEADS, LR_W, dkh), F32), jax.ShapeDtypeStruct((HEADS, 1, dkh), F32)],
        scratch_shapes=[pltpu.VMEM((HEADS, dvh, dkh), F32)],
        compiler_params=_params("arbitrary"),
    )(proj, proj, proj, proj, wlr, blr, st, do)


def gla_post_fwd(o, proj, head_norm, D):
    M = o.shape[0]
    dvh = D // HEADS
    tr = _tile(M, ROW_TILE)

    def body(o_ref, r_ref, hn_ref, out_ref):
        for hd in range(HEADS):
            cols = slice(hd * dvh, (hd + 1) * dvh)
            ov = o_ref[:, cols]
            rs = lax.rsqrt(jnp.mean(ov * ov, axis=-1, keepdims=True) + EPS)
            rv = r_ref[:, cols]
            out_ref[:, cols] = (ov * rs * hn_ref[...] * (rv * _sigmoid(rv))).astype(BF16)

    row = pl.BlockSpec((tr, D), lambda i: (i, 0))
    return pl.pallas_call(
        body, name="gla_post_fwd", grid=(M // tr,),
        in_specs=[row, pl.BlockSpec((tr, D), lambda i: (i, 2)), pl.BlockSpec((1, dvh), lambda i: (0, 0))],
        out_specs=row, out_shape=jax.ShapeDtypeStruct((M, D), BF16),
        compiler_params=_params("parallel"),
    )(o, proj, head_norm)


def gla_post_bwd(dgated, o, proj, head_norm, D):
    M = o.shape[0]
    dvh = D // HEADS
    tr = _tile(M, ROW_TILE)

    def body(dg_ref, o_ref, r_ref, hn_ref, do_ref, dr_ref, dhn_ref):
        @pl.when(pl.program_id(0) == 0)
        def _():
            dhn_ref[...] = jnp.zeros_like(dhn_ref)

        hn = hn_ref[...]
        dhn = jnp.zeros((1, dvh), F32)
        for hd in range(HEADS):
            cols = slice(hd * dvh, (hd + 1) * dvh)
            ov = o_ref[:, cols]
            rs = lax.rsqrt(jnp.mean(ov * ov, axis=-1, keepdims=True) + EPS)
            ohat = ov * rs
            rv = r_ref[:, cols]
            s = _sigmoid(rv)
            dgv = dg_ref[:, cols]
            don = dgv * (rv * s)
            dr_ref[:, cols] = (dgv * ohat * hn * (s * (1.0 + rv * (1.0 - s)))).astype(BF16)
            gd = don * hn
            do_ref[:, cols] = rs * (gd - ohat * jnp.mean(gd * ohat, axis=-1, keepdims=True))
            dhn = dhn + jnp.sum(don * ohat, axis=0, keepdims=True)
        dhn_ref[...] += dhn

    row = pl.BlockSpec((tr, D), lambda i: (i, 0))
    vec = pl.BlockSpec((1, dvh), lambda i: (0, 0))
    return pl.pallas_call(
        body, name="gla_post_bwd", grid=(M // tr,),
        in_specs=[row, row, pl.BlockSpec((tr, D), lambda i: (i, 2)), vec],
        out_specs=[row, row, vec],
        out_shape=[jax.ShapeDtypeStruct((M, D), F32), jax.ShapeDtypeStruct((M, D), BF16),
                   jax.ShapeDtypeStruct((1, dvh), F32)],
        compiler_params=_params("arbitrary"),
    )(dgated, o, proj, head_norm)


def _pool_counts(M, g):
    t = _rows(M) - PAD
    win = jnp.left_shift(2, g)
    return t >= 0, jnp.maximum(jnp.minimum(t + 1, win), 1).astype(F32)


def _window_sum(x, g, M, back):
    sums = []
    s = x
    for lvl in range(4):
        sh = 1 << lvl
        s = s + pltpu.roll(s, (M - sh) if back else sh, 0)
        sums.append(s)
    return jnp.where(g == 0, sums[0], jnp.where(g == 1, sums[1], jnp.where(g == 2, sums[2], sums[3])))


POOL_COLS = 128


def pool_window(hp):
    M, D = hp.shape
    cw = min(POOL_COLS, D // 4)
    per_group = (D // 4) // cw

    def body(h_ref, p_ref):
        g = pl.program_id(0) // per_group
        live, cnt = _pool_counts(M, g)
        hv = h_ref[...]
        p_ref[...] = jnp.where(live, _window_sum(hv, g, M, False) / cnt - hv, 0.0).astype(BF16)

    col = pl.BlockSpec((M, cw), lambda j: (0, j))
    return pl.pallas_call(
        body, name="pool_window", grid=(D // cw,), in_specs=[col], out_specs=col,
        out_shape=jax.ShapeDtypeStruct((M, D), BF16), compiler_params=_params("parallel"),
    )(hp)


def pool_window_bwd(dpooled):
    M, D = dpooled.shape
    cw = min(POOL_COLS, D // 4)
    per_group = (D // 4) // cw

    def body(d_ref, o_ref):
        g = pl.program_id(0) // per_group
        live, cnt = _pool_counts(M, g)
        dv = jnp.where(live, d_ref[...], 0.0)
        o_ref[...] = jnp.where(live, _window_sum(dv / cnt, g, M, True) - dv, 0.0)

    col = pl.BlockSpec((M, cw), lambda j: (0, j))
    return pl.pallas_call(
        body, name="pool_window_bwd", grid=(D // cw,), in_specs=[col], out_specs=col,
        out_shape=jax.ShapeDtypeStruct((M, D), F32), compiler_params=_params("parallel"),
    )(dpooled)


def pool_mix(pooled, x, w, bias, scale):
    M, D = x.shape
    W = D // 4
    tm = _tile(M, 352)

    def body(p_ref, x_ref, w_ref, b_ref, s_ref, out_ref):
        live = (_rows(tm) + pl.program_id(1) * tm) >= PAD
        y = (_dot(p_ref[...], w_ref[...]) + b_ref[...]) * s_ref[...]
        out_ref[...] = x_ref[...] + jnp.where(live, y, 0.0)

    blk = pl.BlockSpec((tm, W), lambda g, i: (i, g))
    vec = pl.BlockSpec((1, W), lambda g, i: (0, g))
    return pl.pallas_call(
        body, name="pool_mix", grid=(4, M // tm),
        in_specs=[blk, blk, pl.BlockSpec((None, W, W), lambda g, i: (g, 0, 0)), vec, vec],
        out_specs=blk, out_shape=jax.ShapeDtypeStruct((M, D), F32),
        compiler_params=_params("parallel", "parallel"),
    )(pooled, x, w, bias, scale)


def pool_mix_bwd(dy, pooled, w, bias, scale, dep=None):
    M, D = dy.shape
    W = D // 4
    tm = _tile(M, 352)

    def body(dy_ref, p_ref, w_ref, b_ref, s_ref, *rest):
        dp_ref, dw_ref, db_ref, ds_ref, acc_w = rest[-5:]
        i = pl.program_id(1)

        @pl.when(i == 0)
        def _():
            acc_w[...] = jnp.zeros_like(acc_w)
            db_ref[...] = jnp.zeros_like(db_ref)
            ds_ref[...] = jnp.zeros_like(ds_ref)

        live = (_rows(tm) + i * tm) >= PAD
        dyv = jnp.where(live, dy_ref[...], 0.0)
        pooled = p_ref[...]
        wv = w_ref[...]
        ds_ref[...] += jnp.sum(dyv * (_dot(pooled, wv) + b_ref[...]), axis=0, keepdims=True)
        dys = dyv * s_ref[...]
        db_ref[...] += jnp.sum(dys, axis=0, keepdims=True)
        dysb = dys.astype(BF16)
        acc_w[...] += _dot_tn(pooled, dysb)
        dp_ref[...] = _dot_nt(dysb, wv)

        @pl.when(i == pl.num_programs(1) - 1)
        def _():
            dw_ref[...] = acc_w[...].astype(BF16)

    blk = pl.BlockSpec((tm, W), lambda g, i: (i, g))
    vec = pl.BlockSpec((1, W), lambda g, i: (0, g))
    wspec = pl.BlockSpec((None, W, W), lambda g, i: (g, 0, 0))
    return pl.pallas_call(
        body, name="pool_mix_bwd", grid=(4, M // tm),
        in_specs=[blk, blk, wspec, vec, vec] + _dep_specs(dep),
        out_specs=[blk, wspec, vec, vec],
        out_shape=[jax.ShapeDtypeStruct((M, D), F32), jax.ShapeDtypeStruct((4, W, W), BF16),
                   jax.ShapeDtypeStruct((1, D), F32), jax.ShapeDtypeStruct((1, D), F32)],
        scratch_shapes=[pltpu.VMEM((W, W), F32)],
        compiler_params=_params("parallel", "arbitrary"),
    )(dy, pooled, w, bias, scale, *_dep_args(dep))


def adamw(w, g, m, v, name, copy_g=False):
    shape = w.shape
    C = shape[-1]
    R = w.size // C
    tr = _tile(R, 256, 8)
    tc = C
    if tr == R and R > 256:
        tc = _tile(C, 256, 128)

    def body(w_ref, g_ref, m_ref, v_ref, d_ref, nm_ref, nv_ref, *g_out):
        gv = g_ref[...]
        for ref in g_out:
            ref[...] = gv
        nm = ADAM_B1 * m_ref[...] + (1.0 - ADAM_B1) * gv
        nv = ADAM_B2 * v_ref[...] + (1.0 - ADAM_B2) * (gv * gv)
        m_hat = nm / (1.0 - ADAM_B1 ** ADAM_STEP)
        v_hat = nv / (1.0 - ADAM_B2 ** ADAM_STEP)
        d_ref[...] = -ADAM_LR * (m_hat / (jnp.sqrt(v_hat) + ADAM_EPS) + ADAM_WD * w_ref[...])
        nm_ref[...] = nm
        nv_ref[...] = nv

    spec = pl.BlockSpec((tr, tc), lambda i, j: (i, j))
    outs = pl.pallas_call(
        body, name=name, grid=(R // tr, C // tc),
        in_specs=[spec] * 4, out_specs=[spec] * (3 + copy_g),
        out_shape=[jax.ShapeDtypeStruct((R, C), F32)] * (3 + copy_g),
        compiler_params=_params("parallel", "parallel"),
    )(*[t.reshape(R, C) for t in (w, g, m, v)])
    return [t.reshape(shape) for t in outs]


def add_sibling(grad, recv, core, name):
    _, _, Rh, C = grad.shape
    tr = _tile(Rh, 512)

    def body(core_ref, g_ref, r_ref, o_ref):
        o_ref[...] = (g_ref[...].astype(F32) + r_ref[...].astype(F32)).astype(BF16)

    return pl.pallas_call(
        body, name=name,
        grid_spec=pltpu.PrefetchScalarGridSpec(
            num_scalar_prefetch=1, grid=(N_CHIPS, Rh // tr),
            in_specs=[pl.BlockSpec((None, None, tr, C), lambda j, i, core_ref: (j, core_ref[0], i, 0)),
                      pl.BlockSpec((None, tr, C), lambda j, i, core_ref: (j, i, 0))],
            out_specs=pl.BlockSpec((None, tr, C), lambda j, i, core_ref: (j, i, 0))),
        out_shape=jax.ShapeDtypeStruct((N_CHIPS, Rh, C), BF16),
        compiler_params=_params("parallel", "parallel"),
    )(core, grad, recv)


def add_chips(part, recv, chip, core, group, n, mi, name):
    _, Rh, C = part.shape
    tr = _tile(Rh, 512)

    def body(chip_ref, core_ref, p_ref, r_ref, *rest):
        o_ref = rest[-1]
        acc = p_ref[...].astype(F32)
        for k in range(N_CHIPS - 1):
            acc = acc + r_ref[k].astype(F32)
        o_ref[...] = acc

    carried = [] if group is None else [group]
    return pl.pallas_call(
        body, name=name,
        grid_spec=pltpu.PrefetchScalarGridSpec(
            num_scalar_prefetch=2, grid=(Rh // tr,),
            in_specs=[pl.BlockSpec((None, tr, C), lambda i, chip_ref, core_ref: (chip_ref[0], i, 0)),
                      pl.BlockSpec((N_CHIPS - 1, tr, C), lambda i, chip_ref, core_ref: (0, i, 0))]
            + [ANY] * len(carried),
            out_specs=pl.BlockSpec((None, None, tr, C), lambda i, chip_ref, core_ref: (mi, core_ref[0], i, 0))),
        out_shape=jax.ShapeDtypeStruct((n, 2, Rh, C), F32),
        input_output_aliases={4: 0} if carried else {},
        compiler_params=_params("parallel"),
    )(chip, core, part, recv, *carried)


def stage_shard(shard, mi, chip, name, dep=None):
    _, _, Rh, C = shard.shape
    tr = _tile(Rh, 512)

    def body(chip_ref, s_ref, *rest):
        rest[-1][...] = s_ref[...].astype(BF16)

    return pl.pallas_call(
        body, name=name,
        grid_spec=pltpu.PrefetchScalarGridSpec(
            num_scalar_prefetch=1, grid=(2, Rh // tr),
            in_specs=[pl.BlockSpec((None, None, tr, C), lambda h, i, chip_ref: (mi, h, i, 0))] + _dep_specs(dep),
            out_specs=pl.BlockSpec((None, None, tr, C), lambda h, i, chip_ref: (chip_ref[0], h, i, 0))),
        out_shape=jax.ShapeDtypeStruct((N_CHIPS, 2, Rh, C), BF16),
        compiler_params=_params("parallel", "parallel"),
    )(chip, shard, *_dep_args(dep))


def sum_devices(gathered):
    _, R, C = gathered.shape

    def body(g_ref, o_ref):
        acc = g_ref[0]
        for d in range(1, N_DEV):
            acc = acc + g_ref[d]
        o_ref[...] = acc

    return pl.pallas_call(
        body, name="sum_devices", grid=(1,),
        in_specs=[pl.BlockSpec((N_DEV, R, C), lambda i: (0, 0, 0))],
        out_specs=pl.BlockSpec((R, C), lambda i: (0, 0)),
        out_shape=jax.ShapeDtypeStruct((R, C), F32),
        compiler_params=_params("arbitrary"),
    )(gathered)


def _place():
    x, y, c = lax.axis_index("x"), lax.axis_index("y"), lax.axis_index("c")
    others = [(1 - x, y), (x, 1 - y), (1 - x, 1 - y)]
    return x, y, c, others


def _remote(src, dst, send_sems, recv_sems, idx, device):
    return pltpu.make_async_remote_copy(src_ref=src, dst_ref=dst, send_sem=send_sems.at[idx],
                                        recv_sem=recv_sems.at[idx], device_id=device, device_id_type=MESH)


HBM = pl.BlockSpec(memory_space=pltpu.HBM)
SEM = pl.BlockSpec(memory_space=pltpu.SEMAPHORE)
EFFECT = pltpu.SideEffectType.DATAFLOW_SIDE_EFFECTING


def _in_hbm(t):
    return pltpu.with_memory_space_constraint(t, pltpu.HBM)


def _own_slice(buf, me, c):
    return buf.at[me, c] if len(buf.shape) == 4 else buf.at[me]


def gather_start(staged, bucket_sizes, name):
    n, nb = len(staged), len(bucket_sizes)

    def body(*refs):
        in_refs, sems, token = refs[:n], refs[n:n + 2 * nb], refs[-1]
        x, y, c, others = _place()
        me = 2 * x + y
        t = 0
        for b, size in enumerate(bucket_sizes):
            for i in range(size):
                mine = _own_slice(in_refs[t], me, c)
                for k, chip in enumerate(others):
                    _remote(mine, mine, sems[2 * b], sems[2 * b + 1], 3 * i + k, (*chip, c)).start()
                t += 1
        token[...] = jnp.zeros_like(token)

    sem_shapes = [pltpu.SemaphoreType.DMA((3 * size,)) for size in bucket_sizes for _ in range(2)]
    outs = pl.pallas_call(
        body, name=name,
        out_shape=sem_shapes + [pltpu.HBM(s.shape, s.dtype) for s in staged] + [jax.ShapeDtypeStruct((8, 128), F32)],
        in_specs=[HBM] * n, out_specs=[SEM] * (2 * nb) + [HBM] * n + [pl.BlockSpec(memory_space=pltpu.VMEM)],
        input_output_aliases={t: 2 * nb + t for t in range(n)},
        compiler_params=pltpu.CompilerParams(has_side_effects=EFFECT),
    )(*[_in_hbm(s) for s in staged])
    sems = [(outs[2 * b], outs[2 * b + 1]) for b in range(nb)]
    return sems, list(outs[2 * nb:2 * nb + n]), outs[-1]


def gather_wait(bufs, sems, after, name):
    n = len(bufs)

    def body(*refs):
        in_refs, send_sems, recv_sems = refs[:n], refs[n], refs[n + 1]
        x, y, c, others = _place()
        me = 2 * x + y
        for i in range(n):
            mine = _own_slice(in_refs[i], me, c)
            for k, (ox, oy) in enumerate(others):
                cp = _remote(mine, _own_slice(in_refs[i], 2 * ox + oy, c), send_sems, recv_sems, 3 * i + k,
                             (ox, oy, c))
                cp.wait_send()
                cp.wait_recv()

    return pl.pallas_call(
        body, name=name, out_shape=[pltpu.HBM(b.shape, b.dtype) for b in bufs],
        in_specs=[HBM] * n + [SEM, SEM, ANY], out_specs=[HBM] * n,
        input_output_aliases={t: t for t in range(n)},
        compiler_params=pltpu.CompilerParams(has_side_effects=EFFECT),
    )(*bufs, *sems, after)


def forward_to_sibling(bufs, name):
    n = len(bufs)

    def body(*refs):
        out_refs, (send_sems, recv_sems) = refs[n:2 * n], refs[2 * n:]
        x, y, c, others = _place()
        sibling = (x, y, 1 - c)
        copies = []
        for t in range(n):
            for k, (ox, oy) in enumerate(others):
                mine = out_refs[t].at[2 * ox + oy, c]
                cp = _remote(mine, mine, send_sems, recv_sems, 3 * t + k, sibling)
                cp.start()
                copies.append(cp)
        for t in range(n):
            for k, (ox, oy) in enumerate(others):
                theirs = out_refs[t].at[2 * ox + oy, 1 - c]
                _remote(theirs, theirs, send_sems, recv_sems, 3 * t + k, sibling).wait_recv()
        for cp in copies:
            cp.wait_send()

    return pl.pallas_call(
        body, name=name, in_specs=[ANY] * n, out_specs=[ANY] * n,
        out_shape=[jax.ShapeDtypeStruct(b.shape, b.dtype) for b in bufs],
        input_output_aliases={t: t for t in range(n)},
        scratch_shapes=[pltpu.SemaphoreType.DMA((3 * n,)), pltpu.SemaphoreType.DMA((3 * n,))],
    )(*bufs)


def sibling_start(grads, name):
    n = len(grads)
    lands = [lax.empty((N_CHIPS,) + g.shape[2:], g.dtype) for g in grads]

    def body(*refs):
        in_refs, land_refs, send_sems, recv_sems, token = refs[:n], refs[n:2 * n], refs[2 * n], refs[2 * n + 1], refs[-1]
        x, y, c, _ = _place()
        for t in range(n):
            for j in range(N_CHIPS):
                _remote(in_refs[t].at[j, 1 - c], land_refs[t].at[j], send_sems, recv_sems, N_CHIPS * t + j,
                        (x, y, 1 - c)).start()
        token[...] = jnp.zeros_like(token)

    outs = pl.pallas_call(
        body, name=name,
        out_shape=[pltpu.SemaphoreType.DMA((N_CHIPS * n,))] * 2 + [pltpu.HBM(t.shape, t.dtype) for t in grads + lands]
        + [jax.ShapeDtypeStruct((8, 128), F32)],
        in_specs=[HBM] * (2 * n), out_specs=[SEM, SEM] + [HBM] * (2 * n) + [pl.BlockSpec(memory_space=pltpu.VMEM)],
        input_output_aliases={t: 2 + t for t in range(2 * n)},
        compiler_params=pltpu.CompilerParams(has_side_effects=EFFECT),
    )(*[_in_hbm(t) for t in grads + lands])
    return (outs[0], outs[1]), list(outs[2:2 + n]), list(outs[2 + n:2 + 2 * n]), outs[-1]


def sibling_wait(grads, lands, sems, after, name):
    n = len(grads)

    def body(*refs):
        in_refs, land_refs, send_sems, recv_sems = refs[:n], refs[n:2 * n], refs[2 * n], refs[2 * n + 1]
        x, y, c, _ = _place()
        for t in range(n):
            for j in range(N_CHIPS):
                cp = _remote(in_refs[t].at[j, 1 - c], land_refs[t].at[j], send_sems, recv_sems, N_CHIPS * t + j,
                             (x, y, 1 - c))
                cp.wait_send()
                cp.wait_recv()

    outs = pl.pallas_call(
        body, name=name, out_shape=[pltpu.HBM(t.shape, t.dtype) for t in grads + lands],
        in_specs=[HBM] * (2 * n) + [SEM, SEM, ANY], out_specs=[HBM] * (2 * n),
        input_output_aliases={t: t for t in range(2 * n)},
        compiler_params=pltpu.CompilerParams(has_side_effects=EFFECT),
    )(*grads, *lands, *sems, after)
    return list(outs[:n]), list(outs[n:])


def reduce_start(parts, name):
    n = len(parts)
    lands = [lax.empty((N_CHIPS - 1,) + p.shape[1:], p.dtype) for p in parts]

    def body(*refs):
        in_refs, land_refs, send_sems, recv_sems, token = refs[:n], refs[n:2 * n], refs[2 * n], refs[2 * n + 1], refs[-1]
        x, y, c, others = _place()
        for t in range(n):
            for k, (ox, oy) in enumerate(others):
                _remote(in_refs[t].at[2 * ox + oy], land_refs[t].at[k], send_sems, recv_sems, 3 * t + k,
                        (ox, oy, c)).start()
        token[...] = jnp.zeros_like(token)

    outs = pl.pallas_call(
        body, name=name,
        out_shape=[pltpu.SemaphoreType.DMA((3 * n,))] * 2 + [pltpu.HBM(t.shape, t.dtype) for t in parts + lands]
        + [jax.ShapeDtypeStruct((8, 128), F32)],
        in_specs=[HBM] * (2 * n), out_specs=[SEM, SEM] + [HBM] * (2 * n) + [pl.BlockSpec(memory_space=pltpu.VMEM)],
        input_output_aliases={t: 2 + t for t in range(2 * n)},
        compiler_params=pltpu.CompilerParams(has_side_effects=EFFECT),
    )(*[_in_hbm(t) for t in parts + lands])
    return (outs[0], outs[1]), list(outs[2:2 + n]), list(outs[2 + n:2 + 2 * n]), outs[-1]


def reduce_wait(parts, lands, sems, after, name):
    n = len(parts)

    def body(*refs):
        in_refs, land_refs, send_sems, recv_sems = refs[:n], refs[n:2 * n], refs[2 * n], refs[2 * n + 1]
        x, y, c, others = _place()
        for t in range(n):
            for k, (ox, oy) in enumerate(others):
                cp = _remote(in_refs[t].at[2 * ox + oy], land_refs[t].at[k], send_sems, recv_sems, 3 * t + k,
                             (ox, oy, c))
                cp.wait_send()
                cp.wait_recv()

    outs = pl.pallas_call(
        body, name=name, out_shape=[pltpu.HBM(t.shape, t.dtype) for t in parts + lands],
        in_specs=[HBM] * (2 * n) + [SEM, SEM] + _dep_specs(after), out_specs=[HBM] * (2 * n),
        input_output_aliases={t: t for t in range(2 * n)},
        compiler_params=pltpu.CompilerParams(has_side_effects=EFFECT),
    )(*parts, *lands, *sems, *_dep_args(after))
    return list(outs[:n]), list(outs[n:])


def exchange_halves(groups, name):
    n_groups = len(groups)
    slots = [(gi, mi) for gi, grp in enumerate(groups) for mi in range(grp.shape[0])]

    def body(*refs):
        out_refs = refs[n_groups:2 * n_groups]
        send_sems, recv_sems = refs[2 * n_groups:]
        x, y, c, _ = _place()
        sibling = (x, y, 1 - c)
        copies = []
        for t, (gi, mi) in enumerate(slots):
            mine = out_refs[gi].at[mi, c]
            cp = _remote(mine, mine, send_sems, recv_sems, t, sibling)
            cp.start()
            copies.append(cp)
        for t, (gi, mi) in enumerate(slots):
            theirs = out_refs[gi].at[mi, 1 - c]
            _remote(theirs, theirs, send_sems, recv_sems, t, sibling).wait_recv()
        for cp in copies:
            cp.wait_send()

    return pl.pallas_call(
        body, name=name, in_specs=[ANY] * n_groups, out_specs=[ANY] * n_groups,
        out_shape=[jax.ShapeDtypeStruct(g.shape, g.dtype) for g in groups],
        input_output_aliases={gi: gi for gi in range(n_groups)},
        scratch_shapes=[pltpu.SemaphoreType.DMA((len(slots),)), pltpu.SemaphoreType.DMA((len(slots),))],
    )(*groups)


def gather_devices(buf):
    def body(in_ref, out_ref, send_sems, recv_sems, local_sem):
        x, y, c, _ = _place()
        me = 4 * x + 2 * y + c
        local = pltpu.make_async_copy(in_ref, out_ref.at[me], local_sem)
        local.start()
        copies = []
        for k in range(1, N_DEV):
            fx, fy, fc = (k >> 2) & 1, (k >> 1) & 1, k & 1
            peer = (x ^ fx, y ^ fy, c ^ fc)
            cp = _remote(in_ref, out_ref.at[me], send_sems, recv_sems, k - 1, peer)
            cp.start()
            copies.append(cp)
        for k in range(1, N_DEV):
            fx, fy, fc = (k >> 2) & 1, (k >> 1) & 1, k & 1
            theirs = out_ref.at[4 * (x ^ fx) + 2 * (y ^ fy) + (c ^ fc)]
            _remote(theirs, theirs, send_sems, recv_sems, k - 1, (x, y, c)).wait_recv()
        for cp in copies:
            cp.wait_send()
        local.wait()

    return pl.pallas_call(
        body, name="gather_devices", in_specs=[ANY], out_specs=ANY,
        out_shape=jax.ShapeDtypeStruct((N_DEV,) + buf.shape, buf.dtype),
        scratch_shapes=[pltpu.SemaphoreType.DMA((N_DEV - 1,)), pltpu.SemaphoreType.DMA((N_DEV - 1,)),
                        pltpu.SemaphoreType.DMA],
    )(buf)


class GradReducer:
    def __init__(self, core, chip, kinds):
        self.core, self.chip = core, chip
        self.sizes = dict(kinds)
        self.groups = {kind: None for kind, _ in kinds}

    def send(self, grads, tag):
        arrays = [g.reshape(N_CHIPS, 2, -1, g.shape[-1]) for g, _, _ in grads]
        sems, arrays, lands, token = sibling_start(arrays, f"reduce_sibling_start_{tag}")
        return (sems, arrays, lands, [(kind, mi) for _, kind, mi in grads], tag), token

    def begin(self, sent, after, tag):
        parts, slots = [], []
        for sems, arrays, lands, sent_slots, sent_tag in sent:
            arrays, lands = sibling_wait(arrays, lands, sems, after, f"reduce_sibling_wait_{sent_tag}")
            parts += [add_sibling(g, r, self.core, f"reduce_add_sibling_{sent_tag}_{t}")
                      for t, (g, r) in enumerate(zip(arrays, lands))]
            slots += sent_slots
        sems, parts, lands, token = reduce_start(parts, f"reduce_start_{tag}")
        return (sems, parts, lands, slots, tag), token

    def end(self, state, after):
        sems, parts, lands, slots, tag = state
        parts, lands = reduce_wait(parts, lands, sems, after, f"reduce_wait_{tag}")
        for t, (kind, mi) in enumerate(slots):
            self.groups[kind] = add_chips(parts[t], lands[t], self.chip, self.core, self.groups[kind],
                                          self.sizes[kind], mi, f"reduce_add_chips_{tag}_{t}")

    def finish(self):
        kinds = list(self.groups)
        return dict(zip(kinds, exchange_halves([self.groups[k] for k in kinds], "reduce_swap")))


def _ffn_fwd(x, gain, wg, wu, wd, tag):
    h, rstd = rmsnorm_fwd(x, gain, BF16, f"ffn_norm_{tag}")
    up, silu, dsilu, act = ffn_gateup(h, wg, wu, f"ffn_gateup_{tag}")
    out = mm_residual(act, wd, x, 0.5, wd.shape[0] // N_CHIPS, f"ffn_down_{tag}", tm_target=704)
    return out, (x, gain, h, rstd, up, silu, dsilu, act)


def _ffn_bwd(dout, saved, wg, wu, wd, index, reducer, dep=None, per_tensor=False):
    x, gain, h, rstd, up, silu, dsilu, act = saved
    D = x.shape[1]
    Fs = wg.shape[2]
    td = _tile(D, 512, 128)
    tag = f"ffn{index}"
    begun = []

    def begin(sent, after, suffix):
        state, token = reducer.begin(sent, after, tag + suffix)
        begun.append(state)
        return token

    dgate, dup = ffn_bwd_act(dout, wd, up, silu, dsilu, f"ffn_bwd_act_{index}", dep=dep)
    d_wd = mm_tn(act, dout, Fs, td, f"ffn_bwd_wd_{index}", b_scale=0.5)
    if per_tensor:
        sent_d, tok = reducer.send([(d_wd, "down", index)], tag + "d")
        d_wg = mm_tn(h, dgate, td, Fs, f"ffn_bwd_wg_{index}", stacked_out=True, dep=tok)
        toks = [begin([sent_d], d_wg, "d")]
        sent_g, tok = reducer.send([(d_wg, "gate", index)], tag + "g")
        d_wu = mm_tn(h, dup, td, Fs, f"ffn_bwd_wu_{index}", stacked_out=True, dep=toks + [tok])
        toks = [begin([sent_g], d_wu, "g")]
        sent, tok = reducer.send([(d_wu, "up", index)], tag + "u")
    else:
        d_wg = mm_tn(h, dgate, td, Fs, f"ffn_bwd_wg_{index}", stacked_out=True)
        d_wu = mm_tn(h, dup, td, Fs, f"ffn_bwd_wu_{index}", stacked_out=True)
        toks = []
        sent, tok = reducer.send([(d_wd, "down", index), (d_wg, "gate", index), (d_wu, "up", index)], tag + "u")
    dh = mm_nt([(dgate, wg), (dup, wu)], _tile(x.shape[0], 704), D, Fs, f"ffn_bwd_dh_{index}", stacked_w=True,
               dep=toks + [tok])
    tok = begin([sent], dh, "u")
    dx, dgain = rmsnorm_bwd(dh, x, gain, rstd, dout, f"ffn_norm_bwd_{index}")
    return dx, dgain, begun, tok


def kernel(x, meta, ffn_norm, ffn_w_gate, ffn_w_up, ffn_w_down, gla_norm, gla_w_in, gla_w_lr, gla_b_lr, gla_head_norm, gla_w_out, pool_norm, pool_w, pool_b, pool_scale, final_norm, loss_target, m_meta, m_ffn_norm, m_ffn_w_gate, m_ffn_w_up, m_ffn_w_down, m_gla_norm, m_gla_w_in, m_gla_w_lr, m_gla_b_lr, m_gla_head_norm, m_gla_w_out, m_pool_norm, m_pool_w, m_pool_b, m_pool_scale, m_final_norm, v_meta, v_ffn_norm, v_ffn_w_gate, v_ffn_w_up, v_ffn_w_down, v_gla_norm, v_gla_w_in, v_gla_w_lr, v_gla_b_lr, v_gla_head_norm, v_gla_w_out, v_pool_norm, v_pool_w, v_pool_b, v_pool_scale, v_final_norm):
    S, D = x.shape[1], x.shape[2]
    M = OFF + S
    Dq = D // N_CHIPS
    Fs = ffn_w_gate.shape[3]
    F = N_CHIPS * Fs
    dk = D // 2
    n_in = gla_w_in.shape[2]
    W = D // 4
    core = lax.axis_index("c").astype(jnp.int32).reshape(1)
    chip_id = 2 * lax.axis_index("x") + lax.axis_index("y")
    chip = chip_id.astype(jnp.int32).reshape(1)

    small = jnp.concatenate([_pad_rows(t) for t in (
        meta, ffn_norm.reshape(4, Dq), gla_w_lr.reshape(8, Dq), pool_norm, pool_b.reshape(1, Dq), pool_scale)],
        axis=0)
    def stage(w, kind, n, mi, dep=None):
        return stage_shard(w.reshape(n, 2, -1, w.shape[-1]), mi, chip, f"stage_{kind}_{mi}", dep=dep)

    ffn_stage = lambda mi, dep=None: [stage(ffn_w_gate, "gate", 4, mi, dep), stage(ffn_w_up, "up", 4, mi, dep),
                                      stage(ffn_w_down, "down", 4, mi, dep)]
    small_stage = lax.dynamic_update_slice(jnp.zeros((N_CHIPS,) + small.shape, F32), small[None], (chip_id, 0, 0))
    first = ffn_stage(0)
    buckets = [first[:2] + [small_stage], first[2:]]
    sizes = [len(b) for b in buckets]
    gather_sems, in_flight, tok = gather_start([t for b in buckets for t in b], sizes, "gather_start_first")
    buckets = [[stage(gla_w_in, "win", 1, 0, tok), stage(gla_w_out, "wout", 1, 0, tok)],
               ffn_stage(1, tok), ffn_stage(2, tok), [stage(pool_w, "pool", 1, 0, tok)] + ffn_stage(3, tok)]
    more_sems, more_in_flight, gather_token = gather_start([t for b in buckets for t in b],
                                                            [len(b) for b in buckets], "gather_start_rest")
    sizes += [len(b) for b in buckets]
    gather_sems += more_sems
    in_flight += more_in_flight
    starts = [sum(sizes[:b]) for b in range(len(sizes))]

    def arrive(b, after, n_big):
        bufs = gather_wait(in_flight[starts[b]:starts[b] + sizes[b]], gather_sems[b], after, f"gather_wait_{b}")
        return forward_to_sibling(bufs[:n_big], f"gather_forward_{b}") + bufs[n_big:]

    ffn_w = lambda t: (t[0].reshape(N_CHIPS, D, Fs), t[1].reshape(N_CHIPS, D, Fs), t[2].reshape(F, D))
    got = arrive(0, gather_token, 2)
    wg, wu, wd = [None] * 4, [None] * 4, [None] * 4
    wg[0], wu[0] = got[0].reshape(N_CHIPS, D, Fs), got[1].reshape(N_CHIPS, D, Fs)
    sm = got[2]
    unshard = lambda t: t.transpose(1, 0, 2).reshape(t.shape[1], D)
    meta_f = unshard(sm[:, 0:16])
    ffn_norm_f = unshard(sm[:, 16:20])
    w_lr_f = sm[:, 24:32].reshape(N_CHIPS, GATE_RANK, dk // N_CHIPS).transpose(1, 0, 2).reshape(GATE_RANK, dk)
    pool_norm_f = sm[:, 32].reshape(1, D)
    pool_b_f = sm[:, 40].reshape(N_CHIPS, 4, W // N_CHIPS).transpose(1, 0, 2).reshape(1, D)
    pool_scale_f = sm[:, 48].reshape(1, D)
    wlr_pad = jnp.pad(w_lr_f.astype(BF16), ((0, LR_W - GATE_RANK), (0, 0)))
    final_g = final_norm.reshape(1, D)
    qkv = 2 * dk + D

    x0 = jnp.concatenate([jnp.zeros((PAD, D), F32), meta_f, x[0]], axis=0)
    target = jnp.pad(loss_target[0], ((OFF, 0), (0, 0)))
    h0, rstd0 = rmsnorm_fwd(x0, ffn_norm_f[0:1], BF16, "ffn_norm_0")
    acts0 = ffn_gateup(h0, wg[0], wu[0], "ffn_gateup_0")
    wd[0] = arrive(1, acts0[3], 1)[0].reshape(F, D)
    x1 = mm_residual(acts0[3], wd[0], x0, 0.5, Fs, "ffn_down_0", tm_target=704)
    ffn0 = (x0, ffn_norm_f[0:1], h0, rstd0, *acts0)
    got = arrive(2, x1, 2)
    w_in = got[0].reshape(N_CHIPS, D, n_in).transpose(1, 0, 2).reshape(D, N_CHIPS * n_in)
    w_out = got[1].reshape(D, D)
    w_all = jnp.concatenate([w_in[:, :qkv], w_in[:, qkv + GATE_RANK:], w_in[:, qkv:qkv + GATE_RANK],
                             jnp.zeros((D, LR_W - GATE_RANK), BF16)], axis=1)
    hg, rstd_g = rmsnorm_fwd(x1, gla_norm, BF16, "gla_norm")
    proj = mm_nn(hg, w_all, F32, "gla_proj")
    o, st = gla_fwd(proj, wlr_pad, gla_b_lr, D)
    gated = gla_post_fwd(o, proj, gla_head_norm, D)
    x2 = mm_residual(gated, w_out, x1, 1.0, D, "gla_out")
    wg[1], wu[1], wd[1] = ffn_w(arrive(3, x2, 3))
    x3, ffn1 = _ffn_fwd(x2, ffn_norm_f[1:2], wg[1], wu[1], wd[1], "1")
    wg[2], wu[2], wd[2] = ffn_w(arrive(4, x3, 3))
    x4, ffn2 = _ffn_fwd(x3, ffn_norm_f[2:3], wg[2], wu[2], wd[2], "2")
    got = arrive(5, x4, 4)
    w_pool = got[0].reshape(N_CHIPS, 4, W // N_CHIPS, W).transpose(1, 0, 2, 3).reshape(4, W, W)
    wg[3], wu[3], wd[3] = ffn_w(got[1:])
    hp, rstd_p = rmsnorm_fwd(x4, pool_norm_f, F32, "pool_norm")
    pooled = pool_window(hp)
    x5 = pool_mix(pooled, x4, w_pool, pool_b_f, pool_scale_f)
    x6, ffn3 = _ffn_fwd(x5, ffn_norm_f[3:4], wg[3], wu[3], wd[3], "3")
    loss, dx6, d_final = final_loss(x6, final_g, target)

    reducer = GradReducer(core, chip, [("gate", 4), ("up", 4), ("down", 4), ("win", 1), ("wout", 1), ("pool", 1)])

    def settle(begun, after):
        for state in begun:
            reducer.end(state, after)

    dx5, dn3, red3, tok = _ffn_bwd(dx6, ffn3, wg[3], wu[3], wd[3], 3, reducer)
    dpooled, d_wpool, d_pool_b, d_pool_scale = pool_mix_bwd(dx5, pooled, w_pool, pool_b_f, pool_scale_f, dep=tok)
    dhp = pool_window_bwd(dpooled)
    dx4, d_pool_norm = rmsnorm_bwd(dhp, x4, pool_norm_f, rstd_p, dx5, "pool_norm_bwd")
    d_wpool = d_wpool.reshape(4, N_CHIPS, W // N_CHIPS, W).transpose(1, 0, 2, 3)
    sent_p, tok = reducer.send([(d_wpool, "pool", 0)], "pool")
    dx3, dn2, red2, tok = _ffn_bwd(dx4, ffn2, wg[2], wu[2], wd[2], 2, reducer, dep=tok)
    redp, tok_p = reducer.begin([sent_p], dx3, "pool")
    dx2, dn1, red1, tok = _ffn_bwd(dx3, ffn1, wg[1], wu[1], wd[1], 1, reducer, dep=[tok, tok_p])
    tm = _tile(M, 352)
    td = _tile(D, 512, 128)
    d_wout = mm_tn(gated, dx2, td, td, "gla_out_bwd_w", dep=tok)
    sent_o, tok = reducer.send([(d_wout, "wout", 0)], "wout")
    dgated = mm_nt([(dx2, w_out)], tm, td, D, "gla_out_bwd_act", dep=tok)
    redo, tok_o = reducer.begin([sent_o], dgated, "wout")
    do, dr, d_head_norm = gla_post_bwd(dgated, o, proj, gla_head_norm, D)
    dq, dkk, dv, dlr, dwlr, dblr = gla_bwd(proj, wlr_pad, gla_b_lr, st, do, D)
    dproj = jnp.concatenate([dq, dkk, dv, dr, dlr.astype(BF16)], axis=1)
    tp = _tile(proj.shape[1], 896, 128)
    d_wall = mm_tn(hg, dproj, td, tp, "gla_proj_bwd_w", dep=tok_o)
    d_win = jnp.concatenate([d_wall[:, :qkv], d_wall[:, qkv + D:qkv + D + GATE_RANK], d_wall[:, qkv:qkv + D]], axis=1)
    d_win = d_win.reshape(D, N_CHIPS, n_in).transpose(1, 0, 2)
    sent_i, tok = reducer.send([(d_win, "win", 0)], "win")
    dhg = mm_nt([(dproj, w_all)], tm, D, tp, "gla_proj_bwd_act", dep=tok)
    redi, tok = reducer.begin([sent_i], dhg, "win")
    dx1, d_gla_norm = rmsnorm_bwd(dhg, x1, gla_norm, rstd_g, dx2, "gla_norm_bwd")
    dx0, dn0, red0, tok = _ffn_bwd(dx1, ffn0, wg[0], wu[0], wd[0], 0, reducer, dep=tok, per_tensor=True)
    settle(red3 + [redp] + red2 + red1 + [redo, redi] + red0[:-1], tok)

    d_wlr = dwlr[:, :GATE_RANK].transpose(1, 0, 2).reshape(GATE_RANK, dk)
    pieces = [dx0[PAD:OFF], dn0, dn1, dn2, dn3, d_gla_norm, d_wlr,
              dblr.reshape(1, dk), d_head_norm, d_pool_norm, d_pool_b, d_pool_scale, d_final]
    packed = jnp.concatenate([_pad_rows(p.reshape(-1, Dq)) for p in pieces], axis=0)
    total = sum_devices(gather_devices(packed))

    settle(red0[-1:], [total] + list(reducer.groups.values()))
    reduced = reducer.finish()
    g_gate = reduced["gate"].reshape(ffn_w_gate.shape)
    g_up = reduced["up"].reshape(ffn_w_up.shape)
    g_down = reduced["down"].reshape(ffn_w_down.shape)
    g_win = reduced["win"].reshape(gla_w_in.shape)
    g_wout = reduced["wout"].reshape(gla_w_out.shape)
    g_wpool = reduced["pool"].reshape(pool_w.shape)
    sums, at = [], 0
    for p in pieces:
        r = p.size // Dq
        sums.append(total[at:at + r].reshape(p.shape))
        at += r + (-r % 8)
    (s_meta, s_n0, s_n1, s_n2, s_n3, s_gla_norm, s_wlr, s_blr, s_head_norm, s_pool_norm, s_pool_b, s_pool_scale,
     s_final) = sums
    s_ffn_norm = jnp.stack([s_n0, s_n1, s_n2, s_n3], axis=0)[:, 0]
    mine = lambda t, width: lax.dynamic_slice_in_dim(t, chip_id * width, width, axis=t.ndim - 1)
    g_meta = mine(s_meta, Dq)
    g_ffn_norm = mine(s_ffn_norm, Dq).reshape(ffn_norm.shape)
    g_gla_norm = s_gla_norm
    g_wlr = mine(s_wlr, dk // N_CHIPS).reshape(gla_w_lr.shape)
    g_blr = s_blr
    g_head_norm = s_head_norm
    g_pool_norm = mine(s_pool_norm, Dq)
    g_pool_b = mine(s_pool_b.reshape(4, W), W // N_CHIPS).reshape(pool_b.shape)
    g_pool_scale = mine(s_pool_scale, Dq)
    g_final = s_final.reshape(final_norm.shape)

    weights = [meta, ffn_norm, ffn_w_gate, ffn_w_up, ffn_w_down, gla_norm, gla_w_in, gla_w_lr, gla_b_lr,
               gla_head_norm, gla_w_out, pool_norm, pool_w, pool_b, pool_scale, final_norm]
    moments_m = [m_meta, m_ffn_norm, m_ffn_w_gate, m_ffn_w_up, m_ffn_w_down, m_gla_norm, m_gla_w_in, m_gla_w_lr,
                 m_gla_b_lr, m_gla_head_norm, m_gla_w_out, m_pool_norm, m_pool_w, m_pool_b, m_pool_scale,
                 m_final_norm]
    moments_v = [v_meta, v_ffn_norm, v_ffn_w_gate, v_ffn_w_up, v_ffn_w_down, v_gla_norm, v_gla_w_in, v_gla_w_lr,
                 v_gla_b_lr, v_gla_head_norm, v_gla_w_out, v_pool_norm, v_pool_w, v_pool_b, v_pool_scale,
                 v_final_norm]
    grads_w = [g_meta, g_ffn_norm, g_gate, g_up, g_down, g_gla_norm, g_win, g_wlr, g_blr, g_head_norm, g_wout,
               g_pool_norm, g_wpool, g_pool_b, g_pool_scale, g_final]
    from_swap = {2, 3, 4, 6, 10, 12}
    deltas, new_m, new_v = [], [], []
    for i, (w, g, m, v) in enumerate(zip(weights, grads_w, moments_m, moments_v)):
        outs = adamw(w, g, m, v, f"adamw_{i}", copy_g=i in from_swap)
        deltas.append(outs[0])
        new_m.append(outs[1])
        new_v.append(outs[2])
        if i in from_swap:
            grads_w[i] = outs[3]

    loss = lax.psum(loss[0, 0], ("x", "y", "c"))
    grad_x = dx0[OFF:][None]
    return (loss, grad_x, *grads_w, *deltas, *new_m, *new_v)
```

```python
import functools

import jax
import jax.numpy as jnp
from jax import lax
from jax.experimental import pallas as pl
from jax.experimental.pallas import tpu as pltpu

F32 = jnp.float32
BF16 = jnp.bfloat16
MESH = pl.DeviceIdType.MESH
ANY = pl.BlockSpec(memory_space=pl.ANY)

N_META = 16
CHUNK = 64
PAD = CHUNK - N_META
OFF = PAD + N_META
EPS = 1e-6
HEADS = 4
GATE_RANK = 16
GATE_NORM = 16.0
LR_W = 128
N_CHIPS = 4
N_DEV = 8
ADAM_LR, ADAM_B1, ADAM_B2, ADAM_EPS, ADAM_WD, ADAM_STEP = 0.001, 0.9, 0.999, 1e-08, 0.01, 10
VMEM_LIMIT = 56 * 1024 * 1024
ROW_TILE = 176
ONE_BUFFER = pl.Buffered(1)


def _tile(n, target, mult=16):
    best = None
    for d in range(mult, min(n, target) + 1, mult):
        if n % d == 0:
            best = d
    return best if best is not None else n


def _params(*sem):
    return pltpu.CompilerParams(dimension_semantics=sem, vmem_limit_bytes=VMEM_LIMIT)


def _dot(a, b):
    return jnp.dot(a, b, preferred_element_type=F32)


def _dot_nt(a, b):
    return lax.dot_general(a, b, (((1,), (1,)), ((), ())), preferred_element_type=F32)


def _dot_tn(a, b):
    return lax.dot_general(a, b, (((0,), (0,)), ((), ())), preferred_element_type=F32)


MXU_WIDTH = 256


def _chunks(n):
    return [slice(lo, min(lo + MXU_WIDTH, n)) for lo in range(0, n, MXU_WIDTH)]


def _sigmoid(x):
    return 1.0 / (1.0 + jnp.exp(-x))


def _rows(tile, width=1):
    return lax.broadcasted_iota(jnp.int32, (tile, width), 0)


def _dep_args(dep):
    if dep is None:
        return []
    return list(dep) if isinstance(dep, (list, tuple)) else [dep]


def _dep_specs(dep):
    return [ANY] * len(_dep_args(dep))


def _pad_rows(t):
    return jnp.pad(t, ((0, -t.shape[0] % 8), (0, 0)))


def rmsnorm_fwd(x, g, out_dtype, name):
    M, D = x.shape
    tr = _tile(M, ROW_TILE)

    def body(x_ref, g_ref, h_ref, r_ref):
        xv = x_ref[...]
        r = lax.rsqrt(jnp.mean(xv * xv, axis=-1, keepdims=True) + EPS)
        h_ref[...] = (xv * r * g_ref[...]).astype(out_dtype)
        r_ref[...] = r

    return pl.pallas_call(
        body, name=name, grid=(M // tr,),
        in_specs=[pl.BlockSpec((tr, D), lambda i: (i, 0)), pl.BlockSpec((1, D), lambda i: (0, 0))],
        out_specs=[pl.BlockSpec((tr, D), lambda i: (i, 0)), pl.BlockSpec((tr, 1), lambda i: (i, 0))],
        out_shape=[jax.ShapeDtypeStruct((M, D), out_dtype), jax.ShapeDtypeStruct((M, 1), F32)],
        compiler_params=_params("parallel"),
    )(x, g)


def rmsnorm_bwd(dh, x, g, rstd, dres, name):
    M, D = x.shape
    tr = _tile(M, ROW_TILE)

    def body(dh_ref, x_ref, g_ref, r_ref, dres_ref, dx_ref, dg_ref, half_ref):
        @pl.when(pl.program_id(0) == 0)
        def _():
            dg_ref[...] = jnp.zeros_like(dg_ref)

        r = r_ref[...]
        xhat = x_ref[...] * r
        dhv = dh_ref[...]
        gd = dhv * g_ref[...]
        dx = dres_ref[...] + r * (gd - xhat * jnp.mean(gd * xhat, axis=-1, keepdims=True))
        dx_ref[...] = dx
        half_ref[...] = (0.5 * dx).astype(BF16)
        dg_ref[...] += jnp.sum(dhv * xhat, axis=0, keepdims=True)

    row = pl.BlockSpec((tr, D), lambda i: (i, 0))
    vec = pl.BlockSpec((1, D), lambda i: (0, 0))
    return pl.pallas_call(
        body, name=name, grid=(M // tr,),
        in_specs=[row, row, vec, pl.BlockSpec((tr, 1), lambda i: (i, 0)), row],
        out_specs=[row, vec, row],
        out_shape=[jax.ShapeDtypeStruct((M, D), F32), jax.ShapeDtypeStruct((1, D), F32),
                   jax.ShapeDtypeStruct((M, D), BF16)],
        compiler_params=_params("arbitrary"),
    )(dh, x, g, rstd, dres)


def final_loss(x, g, target):
    M, D = x.shape
    tr = _tile(M, ROW_TILE)

    def body(x_ref, g_ref, t_ref, loss_ref, dx_ref, dg_ref, half_ref):
        i = pl.program_id(0)

        @pl.when(i == 0)
        def _():
            loss_ref[...] = jnp.zeros_like(loss_ref)
            dg_ref[...] = jnp.zeros_like(dg_ref)

        live = (_rows(tr) + i * tr) >= OFF
        xv = x_ref[...]
        gv = g_ref[...]
        r = lax.rsqrt(jnp.mean(xv * xv, axis=-1, keepdims=True) + EPS)
        xhat = xv * r
        err = jnp.where(live, xhat * gv - t_ref[...], 0.0)
        loss_ref[...] += 0.5 * jnp.sum(jnp.mean(err * err, axis=-1, keepdims=True), axis=0, keepdims=True)
        dy = err * (1.0 / D)
        gd = dy * gv
        dx = r * (gd - xhat * jnp.mean(gd * xhat, axis=-1, keepdims=True))
        dx_ref[...] = dx
        half_ref[...] = (0.5 * dx).astype(BF16)
        dg_ref[...] += jnp.sum(dy * xhat, axis=0, keepdims=True)

    row = pl.BlockSpec((tr, D), lambda i: (i, 0))
    vec = pl.BlockSpec((1, D), lambda i: (0, 0))
    return pl.pallas_call(
        body, name="final_loss", grid=(M // tr,),
        in_specs=[row, vec, row],
        out_specs=[pl.BlockSpec((1, 1), lambda i: (0, 0)), row, vec, row],
        out_shape=[jax.ShapeDtypeStruct((1, 1), F32), jax.ShapeDtypeStruct((M, D), F32),
                   jax.ShapeDtypeStruct((1, D), F32), jax.ShapeDtypeStruct((M, D), BF16)],
        compiler_params=_params("arbitrary"),
    )(x, g, target)


def mm_nn(a, w, out_dtype, name, tm_target=704, tn_target=896):
    M, K = a.shape
    N = w.shape[1]
    tm, tn = _tile(M, tm_target), _tile(N, tn_target, 128)

    def body(a_ref, w_ref, o_ref):
        o_ref[...] = _dot(a_ref[...], w_ref[...]).astype(out_dtype)

    return pl.pallas_call(
        body, name=name, grid=(N // tn, M // tm),
        in_specs=[pl.BlockSpec((tm, K), lambda n, i: (i, 0)), pl.BlockSpec((K, tn), lambda n, i: (0, n))],
        out_specs=pl.BlockSpec((tm, tn), lambda n, i: (i, n)),
        out_shape=jax.ShapeDtypeStruct((M, N), out_dtype),
        compiler_params=_params("parallel", "parallel"),
    )(a, w)


def ffn_gateup(h, wg, wu, name):
    M, D = h.shape
    Fs = wg.shape[2]
    tm = _tile(M, 352)

    def body(h_ref, wg_ref, wu_ref, u_ref, silu_ref, dsilu_ref, a_ref):
        hv = h_ref[...]
        for cols in _chunks(Fs):
            g = _dot(hv, wg_ref[:, cols])
            u = _dot(hv, wu_ref[:, cols])
            s = _sigmoid(g)
            silu = g * s
            u_ref[:, cols] = u.astype(BF16)
            silu_ref[:, cols] = silu.astype(BF16)
            dsilu_ref[:, cols] = (s * (1.0 + g * (1.0 - s))).astype(BF16)
            a_ref[:, cols] = (silu * u).astype(BF16)

    wspec = pl.BlockSpec((None, D, Fs), lambda j, i: (j, 0, 0))
    ospec = pl.BlockSpec((tm, Fs), lambda j, i: (i, j))
    return pl.pallas_call(
        body, name=name, grid=(N_CHIPS, M // tm),
        in_specs=[pl.BlockSpec((tm, D), lambda j, i: (i, 0)), wspec, wspec],
        out_specs=[ospec] * 4,
        out_shape=[jax.ShapeDtypeStruct((M, N_CHIPS * Fs), BF16)] * 4,
        compiler_params=_params("parallel", "parallel"),
    )(h, wg, wu)


def mm_residual(a, w, x, scale, tk, name, tm_target=352):
    M, N = x.shape
    K = w.shape[0]
    tm = _tile(M, tm_target)

    def body(a_ref, w_ref, x_ref, o_ref, acc):
        k = pl.program_id(1)

        @pl.when(k == 0)
        def _():
            acc[...] = jnp.zeros_like(acc)

        acc[...] += _dot(a_ref[...], w_ref[...])

        @pl.when(k == pl.num_programs(1) - 1)
        def _():
            o_ref[...] = x_ref[...] + scale * acc[...]

    aspec = pl.BlockSpec((tm, tk), lambda i, k: (i, k))
    return pl.pallas_call(
        body, name=name, grid=(M // tm, K // tk),
        in_specs=[aspec, pl.BlockSpec((tk, N), lambda i, k: (k, 0)),
                  pl.BlockSpec((tm, N), lambda i, k: (i, 0), pipeline_mode=ONE_BUFFER)],
        out_specs=pl.BlockSpec((tm, N), lambda i, k: (i, 0), pipeline_mode=ONE_BUFFER),
        out_shape=jax.ShapeDtypeStruct((M, N), F32),
        scratch_shapes=[pltpu.VMEM((tm, N), F32)],
        compiler_params=_params("parallel", "arbitrary"),
    )(a, w, x)


def ffn_bwd_act(dy, wd, up, silu, dsilu, name, dep=None):
    M, D = dy.shape
    F = wd.shape[0]
    Fs = F // N_CHIPS
    tm = _tile(M, 352)

    def body(dy_ref, wd_ref, u_ref, silu_ref, dsilu_ref, *rest):
        dg_ref, du_ref = rest[-2:]
        dy = dy_ref[...]
        for cols in _chunks(Fs):
            da = _dot_nt(dy, wd_ref[cols, :])
            dg_ref[:, cols] = (da * u_ref[:, cols].astype(F32) * dsilu_ref[:, cols].astype(F32)).astype(BF16)
            du_ref[:, cols] = (da * silu_ref[:, cols].astype(F32)).astype(BF16)

    fspec = pl.BlockSpec((tm, Fs), lambda j, i: (i, j))
    return pl.pallas_call(
        body, name=name, grid=(N_CHIPS, M // tm),
        in_specs=[pl.BlockSpec((tm, D), lambda j, i: (i, 0)), pl.BlockSpec((Fs, D), lambda j, i: (j, 0)),
                  fspec, fspec, fspec] + _dep_specs(dep),
        out_specs=[fspec, fspec],
        out_shape=[jax.ShapeDtypeStruct((M, F), BF16)] * 2,
        compiler_params=_params("parallel", "parallel"),
    )(dy, wd, up, silu, dsilu, *_dep_args(dep))


def mm_tn(a, b, ta, tb, name, stacked_out=False, out_dtype=BF16, dep=None):
    T, Ma = a.shape
    Nb = b.shape[1]

    def body(a_ref, b_ref, *rest):
        o_ref = rest[-1]
        o_ref[...] = _dot_tn(a_ref[...], b_ref[...].astype(BF16)).astype(out_dtype)

    if stacked_out:
        out_spec = pl.BlockSpec((None, ta, tb), lambda jb, ja: (jb, ja, 0))
        out_shape = jax.ShapeDtypeStruct((Nb // tb, Ma, tb), out_dtype)
    else:
        out_spec = pl.BlockSpec((ta, tb), lambda jb, ja: (ja, jb))
        out_shape = jax.ShapeDtypeStruct((Ma, Nb), out_dtype)
    return pl.pallas_call(
        body, name=name, grid=(Nb // tb, Ma // ta),
        in_specs=[pl.BlockSpec((T, ta), lambda jb, ja: (0, ja)), pl.BlockSpec((T, tb), lambda jb, ja: (0, jb))]
        + _dep_specs(dep),
        out_specs=out_spec, out_shape=out_shape,
        compiler_params=_params("parallel", "parallel"),
    )(a, b, *_dep_args(dep))


def mm_nt(pairs, tm, tn, tk, name, stacked_w=False, dep=None):
    M, K = pairs[0][0].shape
    N = pairs[0][1].shape[1] if stacked_w else pairs[0][1].shape[0]
    n_pairs = len(pairs)

    def body(*refs):
        o_ref, acc = refs[-2:]
        k = pl.program_id(2)

        @pl.when(k == 0)
        def _():
            acc[...] = jnp.zeros_like(acc)

        for p in range(n_pairs):
            acc[...] += _dot_nt(refs[2 * p][...].astype(BF16), refs[2 * p + 1][...])

        @pl.when(k == pl.num_programs(2) - 1)
        def _():
            o_ref[...] = acc[...]

    aspec = pl.BlockSpec((tm, tk), lambda i, n, k: (i, k))
    if stacked_w:
        wspec = pl.BlockSpec((None, tn, tk), lambda i, n, k: (k, n, 0))
    else:
        wspec = pl.BlockSpec((tn, tk), lambda i, n, k: (n, k))
    return pl.pallas_call(
        body, name=name, grid=(M // tm, N // tn, K // tk),
        in_specs=[aspec, wspec] * n_pairs + _dep_specs(dep),
        out_specs=pl.BlockSpec((tm, tn), lambda i, n, k: (i, n), pipeline_mode=ONE_BUFFER),
        out_shape=jax.ShapeDtypeStruct((M, N), F32),
        scratch_shapes=[pltpu.VMEM((tm, tn), F32)],
        compiler_params=_params("parallel", "parallel", "arbitrary"),
    )(*[t for pair in pairs for t in pair], *_dep_args(dep))


def _tri(lower):
    r = lax.broadcasted_iota(jnp.int32, (CHUNK, CHUNK), 0)
    c = lax.broadcasted_iota(jnp.int32, (CHUNK, CHUNK), 1)
    return (r >= c) if lower else (r <= c)


def _tri_sum(mask, x, pieces):
    ones = mask.astype(BF16)
    acc = jnp.zeros_like(x)
    rest = x
    for _ in range(pieces):
        piece = rest.astype(BF16)
        acc = acc + _dot(ones, piece)
        rest = rest - piece.astype(F32)
    return acc


def _gla_gates(lr, wlr, blr, chunk):
    z = _dot(lr, wlr) + blr
    live = (_rows(CHUNK) + chunk * CHUNK) >= PAD
    lg = jnp.where(live, (jnp.minimum(z, 0.0) - jnp.log(1.0 + jnp.exp(-jnp.abs(z)))) * (1.0 / GATE_NORM), 0.0)
    b = _tri_sum(_tri(True), lg, 3)
    b_last = jnp.sum(lg, axis=0, keepdims=True)
    b_mid = jnp.sum(jnp.where(_rows(CHUNK) < CHUNK // 2, lg, 0.0), axis=0, keepdims=True)
    return z, live, b, b_last, b_mid


def _gla_specs(D, chunk_of):
    lr_blk = (3 * D) // LR_W
    return [
        pl.BlockSpec((CHUNK, D // 2), lambda c: (chunk_of(c), 0)),
        pl.BlockSpec((CHUNK, D // 2), lambda c: (chunk_of(c), 1)),
        pl.BlockSpec((CHUNK, D), lambda c: (chunk_of(c), 1)),
        pl.BlockSpec((CHUNK, LR_W), lambda c: (chunk_of(c), lr_blk)),
        pl.BlockSpec((LR_W, D // 2), lambda c: (0, 0)),
        pl.BlockSpec((1, D // 2), lambda c: (0, 0)),
    ]


def gla_fwd(proj, wlr, blr, D):
    M = proj.shape[0]
    n = M // CHUNK
    dkh, dvh = D // 2 // HEADS, D // HEADS
    qscale = float(dkh) ** -0.5

    def body(q_ref, k_ref, v_ref, lr_ref, wlr_ref, blr_ref, o_ref, st_ref, S):
        c = pl.program_id(0)

        @pl.when(c == 0)
        def _():
            S[...] = jnp.zeros_like(S)

        lr = lr_ref[...].astype(BF16)
        for h in range(HEADS):
            kc, vc = slice(h * dkh, (h + 1) * dkh), slice(h * dvh, (h + 1) * dvh)
            _, _, b, b_last, b_mid = _gla_gates(lr, wlr_ref[:, kc], blr_ref[:, kc], c)
            q = q_ref[:, kc] * qscale
            k = k_ref[:, kc]
            v = v_ref[:, vc].astype(BF16)
            s0 = S[h]
            st_ref[h] = s0
            qb = (q * jnp.exp(b)).astype(BF16)
            kb = (k * jnp.exp(b_last - b)).astype(BF16)
            qt = (q * jnp.exp(b - b_mid)).astype(BF16)
            kt = (k * jnp.exp(b_mid - b)).astype(BF16)
            a = jnp.where(_tri(True), _dot_nt(qt, kt), 0.0).astype(BF16)
            o_ref[:, vc] = _dot_nt(qb, s0.astype(BF16)) + _dot(a, v)
            S[h] = jnp.exp(b_last) * s0 + _dot_tn(v, kb)

    return pl.pallas_call(
        body, name="gla_fwd", grid=(n,),
        in_specs=_gla_specs(D, lambda c: c),
        out_specs=[pl.BlockSpec((CHUNK, D), lambda c: (c, 0)),
                   pl.BlockSpec((None, HEADS, dvh, dkh), lambda c: (c, 0, 0, 0))],
        out_shape=[jax.ShapeDtypeStruct((M, D), F32), jax.ShapeDtypeStruct((n, HEADS, dvh, dkh), F32)],
        scratch_shapes=[pltpu.VMEM((HEADS, dvh, dkh), F32)],
        compiler_params=_params("arbitrary"),
    )(proj, proj, proj, proj, wlr, blr)


def gla_bwd(proj, wlr, blr, st, do, D):
    M = proj.shape[0]
    n = M // CHUNK
    dkh, dvh = D // 2 // HEADS, D // HEADS
    qscale = float(dkh) ** -0.5
    rev = lambda c: n - 1 - c

    def body(q_ref, k_ref, v_ref, lr_ref, wlr_ref, blr_ref, st_ref, do_ref,
             dq_ref, dk_ref, dv_ref, dlr_ref, dwlr_ref, dblr_ref, dS):
        step = pl.program_id(0)
        c = n - 1 - step

        @pl.when(step == 0)
        def _():
            dS[...] = jnp.zeros_like(dS)
            dwlr_ref[...] = jnp.zeros_like(dwlr_ref)
            dblr_ref[...] = jnp.zeros_like(dblr_ref)

        lr = lr_ref[...].astype(BF16)
        lower = _tri(True)
        dlr = None
        for h in range(HEADS):
            kc, vc = slice(h * dkh, (h + 1) * dkh), slice(h * dvh, (h + 1) * dvh)
            wlr_h = wlr_ref[:, kc]
            z, live, b, b_last, b_mid = _gla_gates(lr, wlr_h, blr_ref[:, kc], c)
            q = q_ref[:, kc] * qscale
            k = k_ref[:, kc]
            v = v_ref[:, vc].astype(BF16)
            dov = do_ref[:, vc].astype(BF16)
            s0 = st_ref[h]
            ds1 = dS[h]
            ds1b = ds1.astype(BF16)
            e_b, e_lb = jnp.exp(b), jnp.exp(b_last - b)
            e_bm, e_mb = jnp.exp(b - b_mid), jnp.exp(b_mid - b)
            e_last = jnp.exp(b_last)
            qb, kb, qt, kt = q * e_b, k * e_lb, q * e_bm, k * e_mb
            qbb, kbb, qtb, ktb = qb.astype(BF16), kb.astype(BF16), qt.astype(BF16), kt.astype(BF16)
            a = jnp.where(lower, _dot_nt(qtb, ktb), 0.0).astype(BF16)
            da = jnp.where(lower, _dot_nt(dov, v), 0.0).astype(BF16)

            dqb = _dot(dov, s0.astype(BF16))
            dqt = _dot(da, ktb)
            dkt = _dot_tn(da, qtb)
            dkb = _dot(v, ds1b)
            keep = live.astype(F32)
            dv_ref[:, vc] = (keep * (_dot_tn(a, dov) + _dot_nt(kbb, ds1b))).astype(BF16)
            dq_ref[:, kc] = (keep * qscale * (dqb * e_b + dqt * e_bm)).astype(BF16)
            dk_ref[:, kc] = (keep * (dkb * e_lb + dkt * e_mb)).astype(BF16)

            db = dqb * qb - dkb * kb + dqt * qt - dkt * kt
            db_last = (jnp.sum(dkb * kb, axis=0, keepdims=True)
                       + jnp.sum(ds1 * s0, axis=0, keepdims=True) * e_last)
            db = db + jnp.where(_rows(CHUNK) == CHUNK - 1, db_last, 0.0)
            dlg = jnp.where(live, _tri_sum(_tri(False), db, 2), 0.0)
            dz = dlg * (1.0 / GATE_NORM) / (1.0 + jnp.exp(z))
            dzb = dz.astype(BF16)

            dlr_h = _dot_nt(dzb, wlr_h)
            dlr = dlr_h if dlr is None else dlr + dlr_h
            dwlr_ref[h] += _dot_tn(lr, dzb)
            dblr_ref[h] += jnp.sum(dz, axis=0, keepdims=True)
            dS[h] = e_last * ds1 + _dot_tn(dov, qbb)
        dlr_ref[...] = dlr

    return pl.pallas_call(
        body, name="gla_bwd", grid=(n,),
        in_specs=_gla_specs(D, rev) + [
            pl.BlockSpec((None, HEADS, dvh, dkh), lambda c: (rev(c), 0, 0, 0)),
            pl.BlockSpec((CHUNK, D), lambda c: (rev(c), 0))],
        out_specs=[pl.BlockSpec((CHUNK, D // 2), lambda c: (rev(c), 0)),
                   pl.BlockSpec((CHUNK, D // 2), lambda c: (rev(c), 0)),
                   pl.BlockSpec((CHUNK, D), lambda c: (rev(c), 0)),
                   pl.BlockSpec((CHUNK, LR_W), lambda c: (rev(c), 0)),
                   pl.BlockSpec((HEADS, LR_W, dkh), lambda c: (0, 0, 0)),
                   pl.BlockSpec((HEADS, 1, dkh), lambda c: (0, 0, 0))],
        out_shape=[jax.ShapeDtypeStruct((M, D // 2), BF16), jax.ShapeDtypeStruct((M, D // 2), BF16),
                   jax.ShapeDtypeStruct((M, D), BF16), jax.ShapeDtypeStruct((M, LR_W), F32),
                   jax.ShapeDtypeStruct((HEADS, LR_W, dkh), F32), jax.ShapeDtypeStruct((HEADS, 1, dkh), F32)],
        scratch_shapes=[pltpu.VMEM((HEADS, dvh, dkh), F32)],
        compiler_params=_params("arbitrary"),
    )(proj, proj, proj, proj, wlr, blr, st, do)


def gla_post_fwd(o, proj, head_norm, D):
    M = o.shape[0]
    dvh = D // HEADS
    tr = _tile(M, ROW_TILE)

    def body(o_ref, r_ref, hn_ref, out_ref):
        for hd in range(HEADS):
            cols = slice(hd * dvh, (hd + 1) * dvh)
            ov = o_ref[:, cols]
            rs = lax.rsqrt(jnp.mean(ov * ov, axis=-1, keepdims=True) + EPS)
            rv = r_ref[:, cols]
            out_ref[:, cols] = (ov * rs * hn_ref[...] * (rv * _sigmoid(rv))).astype(BF16)

    row = pl.BlockSpec((tr, D), lambda i: (i, 0))
    return pl.pallas_call(
        body, name="gla_post_fwd", grid=(M // tr,),
        in_specs=[row, pl.BlockSpec((tr, D), lambda i: (i, 2)), pl.BlockSpec((1, dvh), lambda i: (0, 0))],
        out_specs=row, out_shape=jax.ShapeDtypeStruct((M, D), BF16),
        compiler_params=_params("parallel"),
    )(o, proj, head_norm)


def gla_post_bwd(dgated, o, proj, head_norm, D):
    M = o.shape[0]
    dvh = D // HEADS
    tr = _tile(M, ROW_TILE)

    def body(dg_ref, o_ref, r_ref, hn_ref, do_ref, dr_ref, dhn_ref):
        @pl.when(pl.program_id(0) == 0)
        def _():
            dhn_ref[...] = jnp.zeros_like(dhn_ref)

        hn = hn_ref[...]
        dhn = jnp.zeros((1, dvh), F32)
        for hd in range(HEADS):
            cols = slice(hd * dvh, (hd + 1) * dvh)
            ov = o_ref[:, cols]
            rs = lax.rsqrt(jnp.mean(ov * ov, axis=-1, keepdims=True) + EPS)
            ohat = ov * rs
            rv = r_ref[:, cols]
            s = _sigmoid(rv)
            dgv = dg_ref[:, cols]
            don = dgv * (rv * s)
            dr_ref[:, cols] = (dgv * ohat * hn * (s * (1.0 + rv * (1.0 - s)))).astype(BF16)
            gd = don * hn
            do_ref[:, cols] = rs * (gd - ohat * jnp.mean(gd * ohat, axis=-1, keepdims=True))
            dhn = dhn + jnp.sum(don * ohat, axis=0, keepdims=True)
        dhn_ref[...] += dhn

    row = pl.BlockSpec((tr, D), lambda i: (i, 0))
    vec = pl.BlockSpec((1, dvh), lambda i: (0, 0))
    return pl.pallas_call(
        body, name="gla_post_bwd", grid=(M // tr,),
        in_specs=[row, row, pl.BlockSpec((tr, D), lambda i: (i, 2)), vec],
        out_specs=[row, row, vec],
        out_shape=[jax.ShapeDtypeStruct((M, D), F32), jax.ShapeDtypeStruct((M, D), BF16),
                   jax.ShapeDtypeStruct((1, dvh), F32)],
        compiler_params=_params("arbitrary"),
    )(dgated, o, proj, head_norm)


def _pool_counts(M, g):
    t = _rows(M) - PAD
    win = jnp.left_shift(2, g)
    return t >= 0, jnp.maximum(jnp.minimum(t + 1, win), 1).astype(F32)


def _window_sum(x, g, M, back):
    sums = []
    s = x
    for lvl in range(4):
        sh = 1 << lvl
        s = s + pltpu.roll(s, (M - sh) if back else sh, 0)
        sums.append(s)
    return jnp.where(g == 0, sums[0], jnp.where(g == 1, sums[1], jnp.where(g == 2, sums[2], sums[3])))


POOL_COLS = 128


def pool_window(hp):
    M, D = hp.shape
    cw = min(POOL_COLS, D // 4)
    per_group = (D // 4) // cw

    def body(h_ref, p_ref):
        g = pl.program_id(0) // per_group
        live, cnt = _pool_counts(M, g)
        hv = h_ref[...]
        p_ref[...] = jnp.where(live, _window_sum(hv, g, M, False) / cnt - hv, 0.0).astype(BF16)

    col = pl.BlockSpec((M, cw), lambda j: (0, j))
    return pl.pallas_call(
        body, name="pool_window", grid=(D // cw,), in_specs=[col], out_specs=col,
        out_shape=jax.ShapeDtypeStruct((M, D), BF16), compiler_params=_params("parallel"),
    )(hp)


def pool_window_bwd(dpooled):
    M, D = dpooled.shape
    cw = min(POOL_COLS, D // 4)
    per_group = (D // 4) // cw

    def body(d_ref, o_ref):
        g = pl.program_id(0) // per_group
        live, cnt = _pool_counts(M, g)
        dv = jnp.where(live, d_ref[...], 0.0)
        o_ref[...] = jnp.where(live, _window_sum(dv / cnt, g, M, True) - dv, 0.0)

    col = pl.BlockSpec((M, cw), lambda j: (0, j))
    return pl.pallas_call(
        body, name="pool_window_bwd", grid=(D // cw,), in_specs=[col], out_specs=col,
        out_shape=jax.ShapeDtypeStruct((M, D), F32), compiler_params=_params("parallel"),
    )(dpooled)


def pool_mix(pooled, x, w, bias, scale):
    M, D = x.shape
    W = D // 4
    tm = _tile(M, 352)

    def body(p_ref, x_ref, w_ref, b_ref, s_ref, out_ref):
        live = (_rows(tm) + pl.program_id(1) * tm) >= PAD
        y = (_dot(p_ref[...], w_ref[...]) + b_ref[...]) * s_ref[...]
        out_ref[...] = x_ref[...] + jnp.where(live, y, 0.0)

    blk = pl.BlockSpec((tm, W), lambda g, i: (i, g))
    vec = pl.BlockSpec((1, W), lambda g, i: (0, g))
    return pl.pallas_call(
        body, name="pool_mix", grid=(4, M // tm),
        in_specs=[blk, blk, pl.BlockSpec((None, W, W), lambda g, i: (g, 0, 0)), vec, vec],
        out_specs=blk, out_shape=jax.ShapeDtypeStruct((M, D), F32),
        compiler_params=_params("parallel", "parallel"),
    )(pooled, x, w, bias, scale)


def pool_mix_bwd(dy, pooled, w, bias, scale, dep=None):
    M, D = dy.shape
    W = D // 4
    tm = _tile(M, 352)

    def body(dy_ref, p_ref, w_ref, b_ref, s_ref, *rest):
        dp_ref, dw_ref, db_ref, ds_ref, acc_w = rest[-5:]
        i = pl.program_id(1)

        @pl.when(i == 0)
        def _():
            acc_w[...] = jnp.zeros_like(acc_w)
            db_ref[...] = jnp.zeros_like(db_ref)
            ds_ref[...] = jnp.zeros_like(ds_ref)

        live = (_rows(tm) + i * tm) >= PAD
        dyv = jnp.where(live, dy_ref[...], 0.0)
        pooled = p_ref[...]
        wv = w_ref[...]
        ds_ref[...] += jnp.sum(dyv * (_dot(pooled, wv) + b_ref[...]), axis=0, keepdims=True)
        dys = dyv * s_ref[...]
        db_ref[...] += jnp.sum(dys, axis=0, keepdims=True)
        dysb = dys.astype(BF16)
        acc_w[...] += _dot_tn(pooled, dysb)
        dp_ref[...] = _dot_nt(dysb, wv)

        @pl.when(i == pl.num_programs(1) - 1)
        def _():
            dw_ref[...] = acc_w[...].astype(BF16)

    blk = pl.BlockSpec((tm, W), lambda g, i: (i, g))
    vec = pl.BlockSpec((1, W), lambda g, i: (0, g))
    wspec = pl.BlockSpec((None, W, W), lambda g, i: (g, 0, 0))
    return pl.pallas_call(
        body, name="pool_mix_bwd", grid=(4, M // tm),
        in_specs=[blk, blk, wspec, vec, vec] + _dep_specs(dep),
        out_specs=[blk, wspec, vec, vec],
        out_shape=[jax.ShapeDtypeStruct((M, D), F32), jax.ShapeDtypeStruct((4, W, W), BF16),
                   jax.ShapeDtypeStruct((1, D), F32), jax.ShapeDtypeStruct((1, D), F32)],
        scratch_shapes=[pltpu.VMEM((W, W), F32)],
        compiler_params=_params("parallel", "arbitrary"),
    )(dy, pooled, w, bias, scale, *_dep_args(dep))


def adamw(w, g, m, v, name, copy_g=False):
    shape = w.shape
    C = shape[-1]
    R = w.size // C
    tr = _tile(R, 256, 8)
    tc = C
    if tr == R and R > 256:
        tc = _tile(C, 256, 128)

    def body(w_ref, g_ref, m_ref, v_ref, d_ref, nm_ref, nv_ref, *g_out):
        gv = g_ref[...]
        for ref in g_out:
            ref[...] = gv
        nm = ADAM_B1 * m_ref[...] + (1.0 - ADAM_B1) * gv
        nv = ADAM_B2 * v_ref[...] + (1.0 - ADAM_B2) * (gv * gv)
        m_hat = nm / (1.0 - ADAM_B1 ** ADAM_STEP)
        v_hat = nv / (1.0 - ADAM_B2 ** ADAM_STEP)
        d_ref[...] = -ADAM_LR * (m_hat / (jnp.sqrt(v_hat) + ADAM_EPS) + ADAM_WD * w_ref[...])
        nm_ref[...] = nm
        nv_ref[...] = nv

    spec = pl.BlockSpec((tr, tc), lambda i, j: (i, j))
    outs = pl.pallas_call(
        body, name=name, grid=(R // tr, C // tc),
        in_specs=[spec] * 4, out_specs=[spec] * (3 + copy_g),
        out_shape=[jax.ShapeDtypeStruct((R, C), F32)] * (3 + copy_g),
        compiler_params=_params("parallel", "parallel"),
    )(*[t.reshape(R, C) for t in (w, g, m, v)])
    return [t.reshape(shape) for t in outs]


def add_sibling(grad, recv, core, name):
    _, _, Rh, C = grad.shape
    tr = _tile(Rh, 512)

    def body(core_ref, g_ref, r_ref, o_ref):
        o_ref[...] = (g_ref[...].astype(F32) + r_ref[...].astype(F32)).astype(BF16)

    return pl.pallas_call(
        body, name=name,
        grid_spec=pltpu.PrefetchScalarGridSpec(
            num_scalar_prefetch=1, grid=(N_CHIPS, Rh // tr),
            in_specs=[pl.BlockSpec((None, None, tr, C), lambda j, i, core_ref: (j, core_ref[0], i, 0)),
                      pl.BlockSpec((None, tr, C), lambda j, i, core_ref: (j, i, 0))],
            out_specs=pl.BlockSpec((None, tr, C), lambda j, i, core_ref: (j, i, 0))),
        out_shape=jax.ShapeDtypeStruct((N_CHIPS, Rh, C), BF16),
        compiler_params=_params("parallel", "parallel"),
    )(core, grad, recv)


def add_chips(part, recv, chip, core, group, n, mi, name):
    _, Rh, C = part.shape
    tr = _tile(Rh, 512)

    def body(chip_ref, core_ref, p_ref, r_ref, *rest):
        o_ref = rest[-1]
        acc = p_ref[...].astype(F32)
        for k in range(N_CHIPS - 1):
            acc = acc + r_ref[k].astype(F32)
        o_ref[...] = acc

    carried = [] if group is None else [group]
    return pl.pallas_call(
        body, name=name,
        grid_spec=pltpu.PrefetchScalarGridSpec(
            num_scalar_prefetch=2, grid=(Rh // tr,),
            in_specs=[pl.BlockSpec((None, tr, C), lambda i, chip_ref, core_ref: (chip_ref[0], i, 0)),
                      pl.BlockSpec((N_CHIPS - 1, tr, C), lambda i, chip_ref, core_ref: (0, i, 0))]
            + [ANY] * len(carried),
            out_specs=pl.BlockSpec((None, None, tr, C), lambda i, chip_ref, core_ref: (mi, core_ref[0], i, 0))),
        out_shape=jax.ShapeDtypeStruct((n, 2, Rh, C), F32),
        input_output_aliases={4: 0} if carried else {},
        compiler_params=_params("parallel"),
    )(chip, core, part, recv, *carried)


def stage_shard(shard, mi, chip, name, dep=None):
    _, _, Rh, C = shard.shape
    tr = _tile(Rh, 512)

    def body(chip_ref, s_ref, *rest):
        rest[-1][...] = s_ref[...].astype(BF16)

    return pl.pallas_call(
        body, name=name,
        grid_spec=pltpu.PrefetchScalarGridSpec(
            num_scalar_prefetch=1, grid=(2, Rh // tr),
            in_specs=[pl.BlockSpec((None, None, tr, C), lambda h, i, chip_ref: (mi, h, i, 0))] + _dep_specs(dep),
            out_specs=pl.BlockSpec((None, None, tr, C), lambda h, i, chip_ref: (chip_ref[0], h, i, 0))),
        out_shape=jax.ShapeDtypeStruct((N_CHIPS, 2, Rh, C), BF16),
        compiler_params=_params("parallel", "parallel"),
    )(chip, shard, *_dep_args(dep))


def sum_devices(gathered):
    _, R, C = gathered.shape

    def body(g_ref, o_ref):
        acc = g_ref[0]
        for d in range(1, N_DEV):
            acc = acc + g_ref[d]
        o_ref[...] = acc

    return pl.pallas_call(
        body, name="sum_devices", grid=(1,),
        in_specs=[pl.BlockSpec((N_DEV, R, C), lambda i: (0, 0, 0))],
        out_specs=pl.BlockSpec((R, C), lambda i: (0, 0)),
        out_shape=jax.ShapeDtypeStruct((R, C), F32),
        compiler_params=_params("arbitrary"),
    )(gathered)


def _place():
    x, y, c = lax.axis_index("x"), lax.axis_index("y"), lax.axis_index("c")
    others = [(1 - x, y), (x, 1 - y), (1 - x, 1 - y)]
    return x, y, c, others


def _remote(src, dst, send_sems, recv_sems, idx, device):
    return pltpu.make_async_remote_copy(src_ref=src, dst_ref=dst, send_sem=send_sems.at[idx],
                                        recv_sem=recv_sems.at[idx], device_id=device, device_id_type=MESH)


HBM = pl.BlockSpec(memory_space=pltpu.HBM)
SEM = pl.BlockSpec(memory_space=pltpu.SEMAPHORE)
EFFECT = pltpu.SideEffectType.DATAFLOW_SIDE_EFFECTING


def _in_hbm(t):
    return pltpu.with_memory_space_constraint(t, pltpu.HBM)


def _own_slice(buf, me, c):
    return buf.at[me, c] if len(buf.shape) == 4 else buf.at[me]


def gather_start(staged, bucket_sizes, name):
    n, nb = len(staged), len(bucket_sizes)

    def body(*refs):
        in_refs, sems, token = refs[:n], refs[n:n + 2 * nb], refs[-1]
        x, y, c, others = _place()
        me = 2 * x + y
        t = 0
        for b, size in enumerate(bucket_sizes):
            for i in range(size):
                mine = _own_slice(in_refs[t], me, c)
                for k, chip in enumerate(others):
                    _remote(mine, mine, sems[2 * b], sems[2 * b + 1], 3 * i + k, (*chip, c)).start()
                t += 1
        token[...] = jnp.zeros_like(token)

    sem_shapes = [pltpu.SemaphoreType.DMA((3 * size,)) for size in bucket_sizes for _ in range(2)]
    outs = pl.pallas_call(
        body, name=name,
        out_shape=sem_shapes + [pltpu.HBM(s.shape, s.dtype) for s in staged] + [jax.ShapeDtypeStruct((8, 128), F32)],
        in_specs=[HBM] * n, out_specs=[SEM] * (2 * nb) + [HBM] * n + [pl.BlockSpec(memory_space=pltpu.VMEM)],
        input_output_aliases={t: 2 * nb + t for t in range(n)},
        compiler_params=pltpu.CompilerParams(has_side_effects=EFFECT),
    )(*[_in_hbm(s) for s in staged])
    sems = [(outs[2 * b], outs[2 * b + 1]) for b in range(nb)]
    return sems, list(outs[2 * nb:2 * nb + n]), outs[-1]


def gather_wait(bufs, sems, after, name):
    n = len(bufs)

    def body(*refs):
        in_refs, send_sems, recv_sems = refs[:n], refs[n], refs[n + 1]
        x, y, c, others = _place()
        me = 2 * x + y
        for i in range(n):
            mine = _own_slice(in_refs[i], me, c)
            for k, (ox, oy) in enumerate(others):
                cp = _remote(mine, _own_slice(in_refs[i], 2 * ox + oy, c), send_sems, recv_sems, 3 * i + k,
                             (ox, oy, c))
                cp.wait_send()
                cp.wait_recv()

    return pl.pallas_call(
        body, name=name, out_shape=[pltpu.HBM(b.shape, b.dtype) for b in bufs],
        in_specs=[HBM] * n + [SEM, SEM, ANY], out_specs=[HBM] * n,
        input_output_aliases={t: t for t in range(n)},
        compiler_params=pltpu.CompilerParams(has_side_effects=EFFECT),
    )(*bufs, *sems, after)


def forward_to_sibling(bufs, name):
    n = len(bufs)

    def body(*refs):
        out_refs, (send_sems, recv_sems) = refs[n:2 * n], refs[2 * n:]
        x, y, c, others = _place()
        sibling = (x, y, 1 - c)
        copies = []
        for t in range(n):
            for k, (ox, oy) in enumerate(others):
                mine = out_refs[t].at[2 * ox + oy, c]
                cp = _remote(mine, mine, send_sems, recv_sems, 3 * t + k, sibling)
                cp.start()
                copies.append(cp)
        for t in range(n):
            for k, (ox, oy) in enumerate(others):
                theirs = out_refs[t].at[2 * ox + oy, 1 - c]
                _remote(theirs, theirs, send_sems, recv_sems, 3 * t + k, sibling).wait_recv()
        for cp in copies:
            cp.wait_send()

    return pl.pallas_call(
        body, name=name, in_specs=[ANY] * n, out_specs=[ANY] * n,
        out_shape=[jax.ShapeDtypeStruct(b.shape, b.dtype) for b in bufs],
        input_output_aliases={t: t for t in range(n)},
        scratch_shapes=[pltpu.SemaphoreType.DMA((3 * n,)), pltpu.SemaphoreType.DMA((3 * n,))],
    )(*bufs)


def sibling_start(grads, name):
    n = len(grads)
    lands = [lax.empty((N_CHIPS,) + g.shape[2:], g.dtype) for g in grads]

    def body(*refs):
        in_refs, land_refs, send_sems, recv_sems, token = refs[:n], refs[n:2 * n], refs[2 * n], refs[2 * n + 1], refs[-1]
        x, y, c, _ = _place()
        for t in range(n):
            for j in range(N_CHIPS):
                _remote(in_refs[t].at[j, 1 - c], land_refs[t].at[j], send_sems, recv_sems, N_CHIPS * t + j,
                        (x, y, 1 - c)).start()
        token[...] = jnp.zeros_like(token)

    outs = pl.pallas_call(
        body, name=name,
        out_shape=[pltpu.SemaphoreType.DMA((N_CHIPS * n,))] * 2 + [pltpu.HBM(t.shape, t.dtype) for t in grads + lands]
        + [jax.ShapeDtypeStruct((8, 128), F32)],
        in_specs=[HBM] * (2 * n), out_specs=[SEM, SEM] + [HBM] * (2 * n) + [pl.BlockSpec(memory_space=pltpu.VMEM)],
        input_output_aliases={t: 2 + t for t in range(2 * n)},
        compiler_params=pltpu.CompilerParams(has_side_effects=EFFECT),
    )(*[_in_hbm(t) for t in grads + lands])
    return (outs[0], outs[1]), list(outs[2:2 + n]), list(outs[2 + n:2 + 2 * n]), outs[-1]


def sibling_wait(grads, lands, sems, after, name):
    n = len(grads)

    def body(*refs):
        in_refs, land_refs, send_sems, recv_sems = refs[:n], refs[n:2 * n], refs[2 * n], refs[2 * n + 1]
        x, y, c, _ = _place()
        for t in range(n):
            for j in range(N_CHIPS):
                cp = _remote(in_refs[t].at[j, 1 - c], land_refs[t].at[j], send_sems, recv_sems, N_CHIPS * t + j,
                             (x, y, 1 - c))
                cp.wait_send()
                cp.wait_recv()

    outs = pl.pallas_call(
        body, name=name, out_shape=[pltpu.HBM(t.shape, t.dtype) for t in grads + lands],
        in_specs=[HBM] * (2 * n) + [SEM, SEM, ANY], out_specs=[HBM] * (2 * n),
        input_output_aliases={t: t for t in range(2 * n)},
        compiler_params=pltpu.CompilerParams(has_side_effects=EFFECT),
    )(*grads, *lands, *sems, after)
    return list(outs[:n]), list(outs[n:])


def reduce_start(parts, name):
    n = len(parts)
    lands = [lax.empty((N_CHIPS - 1,) + p.shape[1:], p.dtype) for p in parts]

    def body(*refs):
        in_refs, land_refs, send_sems, recv_sems, token = refs[:n], refs[n:2 * n], refs[2 * n], refs[2 * n + 1], refs[-1]
        x, y, c, others = _place()
        for t in range(n):
            for k, (ox, oy) in enumerate(others):
                _remote(in_refs[t].at[2 * ox + oy], land_refs[t].at[k], send_sems, recv_sems, 3 * t + k,
                        (ox, oy, c)).start()
        token[...] = jnp.zeros_like(token)

    outs = pl.pallas_call(
        body, name=name,
        out_shape=[pltpu.SemaphoreType.DMA((3 * n,))] * 2 + [pltpu.HBM(t.shape, t.dtype) for t in parts + lands]
        + [jax.ShapeDtypeStruct((8, 128), F32)],
        in_specs=[HBM] * (2 * n), out_specs=[SEM, SEM] + [HBM] * (2 * n) + [pl.BlockSpec(memory_space=pltpu.VMEM)],
        input_output_aliases={t: 2 + t for t in range(2 * n)},
        compiler_params=pltpu.CompilerParams(has_side_effects=EFFECT),
    )(*[_in_hbm(t) for t in parts + lands])
    return (outs[0], outs[1]), list(outs[2:2 + n]), list(outs[2 + n:2 + 2 * n]), outs[-1]


def reduce_wait(parts, lands, sems, after, name):
    n = len(parts)

    def body(*refs):
        in_refs, land_refs, send_sems, recv_sems = refs[:n], refs[n:2 * n], refs[2 * n], refs[2 * n + 1]
        x, y, c, others = _place()
        for t in range(n):
            for k, (ox, oy) in enumerate(others):
                cp = _remote(in_refs[t].at[2 * ox + oy], land_refs[t].at[k], send_sems, recv_sems, 3 * t + k,
                             (ox, oy, c))
                cp.wait_send()
                cp.wait_recv()

    outs = pl.pallas_call(
        body, name=name, out_shape=[pltpu.HBM(t.shape, t.dtype) for t in parts + lands],
        in_specs=[HBM] * (2 * n) + [SEM, SEM] + _dep_specs(after), out_specs=[HBM] * (2 * n),
        input_output_aliases={t: t for t in range(2 * n)},
        compiler_params=pltpu.CompilerParams(has_side_effects=EFFECT),
    )(*parts, *lands, *sems, *_dep_args(after))
    return list(outs[:n]), list(outs[n:])


def exchange_halves(groups, name):
    n_groups = len(groups)
    slots = [(gi, mi) for gi, grp in enumerate(groups) for mi in range(grp.shape[0])]

    def body(*refs):
        out_refs = refs[n_groups:2 * n_groups]
        send_sems, recv_sems = refs[2 * n_groups:]
        x, y, c, _ = _place()
        sibling = (x, y, 1 - c)
        copies = []
        for t, (gi, mi) in enumerate(slots):
            mine = out_refs[gi].at[mi, c]
            cp = _remote(mine, mine, send_sems, recv_sems, t, sibling)
            cp.start()
            copies.append(cp)
        for t, (gi, mi) in enumerate(slots):
            theirs = out_refs[gi].at[mi, 1 - c]
            _remote(theirs, theirs, send_sems, recv_sems, t, sibling).wait_recv()
        for cp in copies:
            cp.wait_send()

    return pl.pallas_call(
        body, name=name, in_specs=[ANY] * n_groups, out_specs=[ANY] * n_groups,
        out_shape=[jax.ShapeDtypeStruct(g.shape, g.dtype) for g in groups],
        input_output_aliases={gi: gi for gi in range(n_groups)},
        scratch_shapes=[pltpu.SemaphoreType.DMA((len(slots),)), pltpu.SemaphoreType.DMA((len(slots),))],
    )(*groups)


def gather_devices(buf):
    def body(in_ref, out_ref, send_sems, recv_sems, local_sem):
        x, y, c, _ = _place()
        me = 4 * x + 2 * y + c
        local = pltpu.make_async_copy(in_ref, out_ref.at[me], local_sem)
        local.start()
        copies = []
        for k in range(1, N_DEV):
            fx, fy, fc = (k >> 2) & 1, (k >> 1) & 1, k & 1
            peer = (x ^ fx, y ^ fy, c ^ fc)
            cp = _remote(in_ref, out_ref.at[me], send_sems, recv_sems, k - 1, peer)
            cp.start()
            copies.append(cp)
        for k in range(1, N_DEV):
            fx, fy, fc = (k >> 2) & 1, (k >> 1) & 1, k & 1
            theirs = out_ref.at[4 * (x ^ fx) + 2 * (y ^ fy) + (c ^ fc)]
            _remote(theirs, theirs, send_sems, recv_sems, k - 1, (x, y, c)).wait_recv()
        for cp in copies:
            cp.wait_send()
        local.wait()

    return pl.pallas_call(
        body, name="gather_devices", in_specs=[ANY], out_specs=ANY,
        out_shape=jax.ShapeDtypeStruct((N_DEV,) + buf.shape, buf.dtype),
        scratch_shapes=[pltpu.SemaphoreType.DMA((N_DEV - 1,)), pltpu.SemaphoreType.DMA((N_DEV - 1,)),
                        pltpu.SemaphoreType.DMA],
    )(buf)


class GradReducer:
    def __init__(self, core, chip, kinds):
        self.core, self.chip = core, chip
        self.sizes = dict(kinds)
        self.groups = {kind: None for kind, _ in kinds}

    def send(self, grads, tag):
        arrays = [g.reshape(N_CHIPS, 2, -1, g.shape[-1]) for g, _, _ in grads]
        sems, arrays, lands, token = sibling_start(arrays, f"reduce_sibling_start_{tag}")
        return (sems, arrays, lands, [(kind, mi) for _, kind, mi in grads], tag), token

    def begin(self, sent, after, tag):
        parts, slots = [], []
        for sems, arrays, lands, sent_slots, sent_tag in sent:
            arrays, lands = sibling_wait(arrays, lands, sems, after, f"reduce_sibling_wait_{sent_tag}")
            parts += [add_sibling(g, r, self.core, f"reduce_add_sibling_{sent_tag}_{t}")
                      for t, (g, r) in enumerate(zip(arrays, lands))]
            slots += sent_slots
        sems, parts, lands, token = reduce_start(parts, f"reduce_start_{tag}")
        return (sems, parts, lands, slots, tag), token

    def end(self, state, after):
        sems, parts, lands, slots, tag = state
        parts, lands = reduce_wait(parts, lands, sems, after, f"reduce_wait_{tag}")
        for t, (kind, mi) in enumerate(slots):
            self.groups[kind] = add_chips(parts[t], lands[t], self.chip, self.core, self.groups[kind],
                                          self.sizes[kind], mi, f"reduce_add_chips_{tag}_{t}")

    def finish(self):
        kinds = list(self.groups)
        return dict(zip(kinds, exchange_halves([self.groups[k] for k in kinds], "reduce_swap")))


def _ffn_fwd(x, gain, wg, wu, wd, tag):
    h, rstd = rmsnorm_fwd(x, gain, BF16, f"ffn_norm_{tag}")
    up, silu, dsilu, act = ffn_gateup(h, wg, wu, f"ffn_gateup_{tag}")
    out = mm_residual(act, wd, x, 0.5, wd.shape[0] // N_CHIPS, f"ffn_down_{tag}", tm_target=704)
    return out, (x, gain, h, rstd, up, silu, dsilu, act)


def _ffn_bwd(dout, dy, saved, wg, wu, wd, index, reducer, dep=None, per_tensor=False):
    x, gain, h, rstd, up, silu, dsilu, act = saved
    D = x.shape[1]
    Fs = wg.shape[2]
    td = _tile(D, 512, 128)
    tag = f"ffn{index}"
    begun = []

    def begin(sent, after, suffix):
        state, token = reducer.begin(sent, after, tag + suffix)
        begun.append(state)
        return token

    dgate, dup = ffn_bwd_act(dy, wd, up, silu, dsilu, f"ffn_bwd_act_{index}", dep=dep)
    d_wd = mm_tn(act, dy, Fs, td, f"ffn_bwd_wd_{index}")
    if per_tensor:
        sent_d, tok = reducer.send([(d_wd, "down", index)], tag + "d")
        d_wg = mm_tn(h, dgate, td, Fs, f"ffn_bwd_wg_{index}", stacked_out=True, dep=tok)
        toks = [begin([sent_d], d_wg, "d")]
        sent_g, tok = reducer.send([(d_wg, "gate", index)], tag + "g")
        d_wu = mm_tn(h, dup, td, Fs, f"ffn_bwd_wu_{index}", stacked_out=True, dep=toks + [tok])
        toks = [begin([sent_g], d_wu, "g")]
        sent, tok = reducer.send([(d_wu, "up", index)], tag + "u")
    else:
        d_wg = mm_tn(h, dgate, td, Fs, f"ffn_bwd_wg_{index}", stacked_out=True)
        d_wu = mm_tn(h, dup, td, Fs, f"ffn_bwd_wu_{index}", stacked_out=True)
        toks = []
        sent, tok = reducer.send([(d_wd, "down", index), (d_wg, "gate", index), (d_wu, "up", index)], tag + "u")
    dh = mm_nt([(dgate, wg), (dup, wu)], _tile(x.shape[0], 704), D, Fs, f"ffn_bwd_dh_{index}", stacked_w=True,
               dep=toks + [tok])
    tok = begin([sent], dh, "u")
    dx, dgain, dx_half = rmsnorm_bwd(dh, x, gain, rstd, dout, f"ffn_norm_bwd_{index}")
    return dx, dx_half, dgain, begun, tok


def kernel(x, meta, ffn_norm, ffn_w_gate, ffn_w_up, ffn_w_down, gla_norm, gla_w_in, gla_w_lr, gla_b_lr, gla_head_norm, gla_w_out, pool_norm, pool_w, pool_b, pool_scale, final_norm, loss_target, m_meta, m_ffn_norm, m_ffn_w_gate, m_ffn_w_up, m_ffn_w_down, m_gla_norm, m_gla_w_in, m_gla_w_lr, m_gla_b_lr, m_gla_head_norm, m_gla_w_out, m_pool_norm, m_pool_w, m_pool_b, m_pool_scale, m_final_norm, v_meta, v_ffn_norm, v_ffn_w_gate, v_ffn_w_up, v_ffn_w_down, v_gla_norm, v_gla_w_in, v_gla_w_lr, v_gla_b_lr, v_gla_head_norm, v_gla_w_out, v_pool_norm, v_pool_w, v_pool_b, v_pool_scale, v_final_norm):
    S, D = x.shape[1], x.shape[2]
    M = OFF + S
    Dq = D // N_CHIPS
    Fs = ffn_w_gate.shape[3]
    F = N_CHIPS * Fs
    dk = D // 2
    n_in = gla_w_in.shape[2]
    W = D // 4
    core = lax.axis_index("c").astype(jnp.int32).reshape(1)
    chip_id = 2 * lax.axis_index("x") + lax.axis_index("y")
    chip = chip_id.astype(jnp.int32).reshape(1)

    small = jnp.concatenate([_pad_rows(t) for t in (
        meta, ffn_norm.reshape(4, Dq), gla_w_lr.reshape(8, Dq), pool_norm, pool_b.reshape(1, Dq), pool_scale)],
        axis=0)
    def stage(w, kind, n, mi, dep=None):
        return stage_shard(w.reshape(n, 2, -1, w.shape[-1]), mi, chip, f"stage_{kind}_{mi}", dep=dep)

    ffn_stage = lambda mi, dep=None: [stage(ffn_w_gate, "gate", 4, mi, dep), stage(ffn_w_up, "up", 4, mi, dep),
                                      stage(ffn_w_down, "down", 4, mi, dep)]
    small_stage = lax.dynamic_update_slice(jnp.zeros((N_CHIPS,) + small.shape, F32), small[None], (chip_id, 0, 0))
    first = ffn_stage(0)
    buckets = [first[:2] + [small_stage], first[2:]]
    sizes = [len(b) for b in buckets]
    gather_sems, in_flight, tok = gather_start([t for b in buckets for t in b], sizes, "gather_start_first")
    buckets = [[stage(gla_w_in, "win", 1, 0, tok), stage(gla_w_out, "wout", 1, 0, tok)],
               ffn_stage(1, tok), ffn_stage(2, tok), [stage(pool_w, "pool", 1, 0, tok)] + ffn_stage(3, tok)]
    more_sems, more_in_flight, gather_token = gather_start([t for b in buckets for t in b],
                                                            [len(b) for b in buckets], "gather_start_rest")
    sizes += [len(b) for b in buckets]
    gather_sems += more_sems
    in_flight += more_in_flight
    starts = [sum(sizes[:b]) for b in range(len(sizes))]

    def arrive(b, after, n_big):
        bufs = gather_wait(in_flight[starts[b]:starts[b] + sizes[b]], gather_sems[b], after, f"gather_wait_{b}")
        return forward_to_sibling(bufs[:n_big], f"gather_forward_{b}") + bufs[n_big:]

    ffn_w = lambda t: (t[0].reshape(N_CHIPS, D, Fs), t[1].reshape(N_CHIPS, D, Fs), t[2].reshape(F, D))
    got = arrive(0, gather_token, 2)
    wg, wu, wd = [None] * 4, [None] * 4, [None] * 4
    wg[0], wu[0] = got[0].reshape(N_CHIPS, D, Fs), got[1].reshape(N_CHIPS, D, Fs)
    sm = got[2]
    unshard = lambda t: t.transpose(1, 0, 2).reshape(t.shape[1], D)
    meta_f = unshard(sm[:, 0:16])
    ffn_norm_f = unshard(sm[:, 16:20])
    w_lr_f = sm[:, 24:32].reshape(N_CHIPS, GATE_RANK, dk // N_CHIPS).transpose(1, 0, 2).reshape(GATE_RANK, dk)
    pool_norm_f = sm[:, 32].reshape(1, D)
    pool_b_f = sm[:, 40].reshape(N_CHIPS, 4, W // N_CHIPS).transpose(1, 0, 2).reshape(1, D)
    pool_scale_f = sm[:, 48].reshape(1, D)
    wlr_pad = jnp.pad(w_lr_f.astype(BF16), ((0, LR_W - GATE_RANK), (0, 0)))
    final_g = final_norm.reshape(1, D)
    qkv = 2 * dk + D

    x0 = jnp.concatenate([jnp.zeros((PAD, D), F32), meta_f, x[0]], axis=0)
    target = jnp.pad(loss_target[0], ((OFF, 0), (0, 0)))
    h0, rstd0 = rmsnorm_fwd(x0, ffn_norm_f[0:1], BF16, "ffn_norm_0")
    acts0 = ffn_gateup(h0, wg[0], wu[0], "ffn_gateup_0")
    wd[0] = arrive(1, acts0[3], 1)[0].reshape(F, D)
    x1 = mm_residual(acts0[3], wd[0], x0, 0.5, Fs, "ffn_down_0", tm_target=704)
    ffn0 = (x0, ffn_norm_f[0:1], h0, rstd0, *acts0)
    got = arrive(2, x1, 2)
    w_in = got[0].reshape(N_CHIPS, D, n_in).transpose(1, 0, 2).reshape(D, N_CHIPS * n_in)
    w_out = got[1].reshape(D, D)
    w_all = jnp.concatenate([w_in[:, :qkv], w_in[:, qkv + GATE_RANK:], w_in[:, qkv:qkv + GATE_RANK],
                             jnp.zeros((D, LR_W - GATE_RANK), BF16)], axis=1)
    hg, rstd_g = rmsnorm_fwd(x1, gla_norm, BF16, "gla_norm")
    proj = mm_nn(hg, w_all, F32, "gla_proj")
    o, st = gla_fwd(proj, wlr_pad, gla_b_lr, D)
    gated = gla_post_fwd(o, proj, gla_head_norm, D)
    x2 = mm_residual(gated, w_out, x1, 1.0, D, "gla_out")
    wg[1], wu[1], wd[1] = ffn_w(arrive(3, x2, 3))
    x3, ffn1 = _ffn_fwd(x2, ffn_norm_f[1:2], wg[1], wu[1], wd[1], "1")
    wg[2], wu[2], wd[2] = ffn_w(arrive(4, x3, 3))
    x4, ffn2 = _ffn_fwd(x3, ffn_norm_f[2:3], wg[2], wu[2], wd[2], "2")
    got = arrive(5, x4, 4)
    w_pool = got[0].reshape(N_CHIPS, 4, W // N_CHIPS, W).transpose(1, 0, 2, 3).reshape(4, W, W)
    wg[3], wu[3], wd[3] = ffn_w(got[1:])
    hp, rstd_p = rmsnorm_fwd(x4, pool_norm_f, F32, "pool_norm")
    pooled = pool_window(hp)
    x5 = pool_mix(pooled, x4, w_pool, pool_b_f, pool_scale_f)
    x6, ffn3 = _ffn_fwd(x5, ffn_norm_f[3:4], wg[3], wu[3], wd[3], "3")
    loss, dx6, d_final, dy6 = final_loss(x6, final_g, target)

    reducer = GradReducer(core, chip, [("gate", 4), ("up", 4), ("down", 4), ("win", 1), ("wout", 1), ("pool", 1)])

    def settle(begun, after):
        for state in begun:
            reducer.end(state, after)

    dx5, _, dn3, red3, tok = _ffn_bwd(dx6, dy6, ffn3, wg[3], wu[3], wd[3], 3, reducer)
    dpooled, d_wpool, d_pool_b, d_pool_scale = pool_mix_bwd(dx5, pooled, w_pool, pool_b_f, pool_scale_f, dep=tok)
    dhp = pool_window_bwd(dpooled)
    dx4, d_pool_norm, dy4 = rmsnorm_bwd(dhp, x4, pool_norm_f, rstd_p, dx5, "pool_norm_bwd")
    d_wpool = d_wpool.reshape(4, N_CHIPS, W // N_CHIPS, W).transpose(1, 0, 2, 3)
    sent_p, tok = reducer.send([(d_wpool, "pool", 0)], "pool")
    dx3, dy3, dn2, red2, tok = _ffn_bwd(dx4, dy4, ffn2, wg[2], wu[2], wd[2], 2, reducer, dep=tok)
    redp, tok_p = reducer.begin([sent_p], dx3, "pool")
    dx2, _, dn1, red1, tok = _ffn_bwd(dx3, dy3, ffn1, wg[1], wu[1], wd[1], 1, reducer, dep=[tok, tok_p])
    tm = _tile(M, 352)
    td = _tile(D, 512, 128)
    d_wout = mm_tn(gated, dx2, td, td, "gla_out_bwd_w", dep=tok)
    sent_o, tok = reducer.send([(d_wout, "wout", 0)], "wout")
    dgated = mm_nt([(dx2, w_out)], tm, td, D, "gla_out_bwd_act", dep=tok)
    redo, tok_o = reducer.begin([sent_o], dgated, "wout")
    do, dr, d_head_norm = gla_post_bwd(dgated, o, proj, gla_head_norm, D)
    dq, dkk, dv, dlr, dwlr, dblr = gla_bwd(proj, wlr_pad, gla_b_lr, st, do, D)
    dproj = jnp.concatenate([dq, dkk, dv, dr, dlr.astype(BF16)], axis=1)
    tp = _tile(proj.shape[1], 896, 128)
    d_wall = mm_tn(hg, dproj, td, tp, "gla_proj_bwd_w", dep=tok_o)
    d_win = jnp.concatenate([d_wall[:, :qkv], d_wall[:, qkv + D:qkv + D + GATE_RANK], d_wall[:, qkv:qkv + D]], axis=1)
    d_win = d_win.reshape(D, N_CHIPS, n_in).transpose(1, 0, 2)
    sent_i, tok = reducer.send([(d_win, "win", 0)], "win")
    dhg = mm_nt([(dproj, w_all)], tm, D, tp, "gla_proj_bwd_act", dep=tok)
    redi, tok = reducer.begin([sent_i], dhg, "win")
    dx1, d_gla_norm, dy1 = rmsnorm_bwd(dhg, x1, gla_norm, rstd_g, dx2, "gla_norm_bwd")
    dx0, _, dn0, red0, tok = _ffn_bwd(dx1, dy1, ffn0, wg[0], wu[0], wd[0], 0, reducer, dep=tok, per_tensor=True)
    settle(red3 + [redp] + red2 + red1 + [redo, redi] + red0[:-1], tok)

    d_wlr = dwlr[:, :GATE_RANK].transpose(1, 0, 2).reshape(GATE_RANK, dk)
    pieces = [dx0[PAD:OFF], dn0, dn1, dn2, dn3, d_gla_norm, d_wlr,
              dblr.reshape(1, dk), d_head_norm, d_pool_norm, d_pool_b, d_pool_scale, d_final]
    packed = jnp.concatenate([_pad_rows(p.reshape(-1, Dq)) for p in pieces], axis=0)
    total = sum_devices(gather_devices(packed))

    settle(red0[-1:], [total] + list(reducer.groups.values()))
    reduced = reducer.finish()
    g_gate = reduced["gate"].reshape(ffn_w_gate.shape)
    g_up = reduced["up"].reshape(ffn_w_up.shape)
    g_down = reduced["down"].reshape(ffn_w_down.shape)
    g_win = reduced["win"].reshape(gla_w_in.shape)
    g_wout = reduced["wout"].reshape(gla_w_out.shape)
    g_wpool = reduced["pool"].reshape(pool_w.shape)
    sums, at = [], 0
    for p in pieces:
        r = p.size // Dq
        sums.append(total[at:at + r].reshape(p.shape))
        at += r + (-r % 8)
    (s_meta, s_n0, s_n1, s_n2, s_n3, s_gla_norm, s_wlr, s_blr, s_head_norm, s_pool_norm, s_pool_b, s_pool_scale,
     s_final) = sums
    s_ffn_norm = jnp.stack([s_n0, s_n1, s_n2, s_n3], axis=0)[:, 0]
    mine = lambda t, width: lax.dynamic_slice_in_dim(t, chip_id * width, width, axis=t.ndim - 1)
    g_meta = mine(s_meta, Dq)
    g_ffn_norm = mine(s_ffn_norm, Dq).reshape(ffn_norm.shape)
    g_gla_norm = s_gla_norm
    g_wlr = mine(s_wlr, dk // N_CHIPS).reshape(gla_w_lr.shape)
    g_blr = s_blr
    g_head_norm = s_head_norm
    g_pool_norm = mine(s_pool_norm, Dq)
    g_pool_b = mine(s_pool_b.reshape(4, W), W // N_CHIPS).reshape(pool_b.shape)
    g_pool_scale = mine(s_pool_scale, Dq)
    g_final = s_final.reshape(final_norm.shape)

    weights = [meta, ffn_norm, ffn_w_gate, ffn_w_up, ffn_w_down, gla_norm, gla_w_in, gla_w_lr, gla_b_lr,
               gla_head_norm, gla_w_out, pool_norm, pool_w, pool_b, pool_scale, final_norm]
    moments_m = [m_meta, m_ffn_norm, m_ffn_w_gate, m_ffn_w_up, m_ffn_w_down, m_gla_norm, m_gla_w_in, m_gla_w_lr,
                 m_gla_b_lr, m_gla_head_norm, m_gla_w_out, m_pool_norm, m_pool_w, m_pool_b, m_pool_scale,
                 m_final_norm]
    moments_v = [v_meta, v_ffn_norm, v_ffn_w_gate, v_ffn_w_up, v_ffn_w_down, v_gla_norm, v_gla_w_in, v_gla_w_lr,
                 v_gla_b_lr, v_gla_head_norm, v_gla_w_out, v_pool_norm, v_pool_w, v_pool_b, v_pool_scale,
                 v_final_norm]
    grads_w = [g_meta, g_ffn_norm, g_gate, g_up, g_down, g_gla_norm, g_win, g_wlr, g_blr, g_head_norm, g_wout,
               g_pool_norm, g_wpool, g_pool_b, g_pool_scale, g_final]
    from_swap = {2, 3, 4, 6, 10, 12}
    deltas, new_m, new_v = [], [], []
    for i, (w, g, m, v) in enumerate(zip(weights, grads_w, moments_m, moments_v)):
        outs = adamw(w, g, m, v, f"adamw_{i}", copy_g=i in from_swap)
        deltas.append(outs[0])
        new_m.append(outs[1])
        new_v.append(outs[2])
        if i in from_swap:
            grads_w[i] = outs[3]

    loss = lax.psum(loss[0, 0], ("x", "y", "c"))
    grad_x = dx0[OFF:][None]
    return (loss, grad_x, *grads_w, *deltas, *new_m, *new_v)
```

```python
import functools

import jax
import jax.numpy as jnp
from jax import lax
from jax.experimental import pallas as pl
from jax.experimental.pallas import tpu as pltpu

F32 = jnp.float32
BF16 = jnp.bfloat16
MESH = pl.DeviceIdType.MESH
ANY = pl.BlockSpec(memory_space=pl.ANY)

N_META = 16
CHUNK = 64
PAD = CHUNK - N_META
OFF = PAD + N_META
EPS = 1e-6
HEADS = 4
GATE_RANK = 16
GATE_NORM = 16.0
LR_W = 128
N_CHIPS = 4
N_DEV = 8
ADAM_LR, ADAM_B1, ADAM_B2, ADAM_EPS, ADAM_WD, ADAM_STEP = 0.001, 0.9, 0.999, 1e-08, 0.01, 10
VMEM_LIMIT = 56 * 1024 * 1024
ROW_TILE = 176
ONE_BUFFER = pl.Buffered(1)


def _tile(n, target, mult=16):
    best = None
    for d in range(mult, min(n, target) + 1, mult):
        if n % d == 0:
            best = d
    return best if best is not None else n


def _params(*sem):
    return pltpu.CompilerParams(dimension_semantics=sem, vmem_limit_bytes=VMEM_LIMIT)


def _dot(a, b):
    return jnp.dot(a, b, preferred_element_type=F32)


def _dot_nt(a, b):
    return lax.dot_general(a, b, (((1,), (1,)), ((), ())), preferred_element_type=F32)


def _dot_tn(a, b):
    return lax.dot_general(a, b, (((0,), (0,)), ((), ())), preferred_element_type=F32)


MXU_WIDTH = 256


def _chunks(n):
    return [slice(lo, min(lo + MXU_WIDTH, n)) for lo in range(0, n, MXU_WIDTH)]


def _sigmoid(x):
    return 1.0 / (1.0 + jnp.exp(-x))


def _rows(tile, width=1):
    return lax.broadcasted_iota(jnp.int32, (tile, width), 0)


def _dep_args(dep):
    if dep is None:
        return []
    return list(dep) if isinstance(dep, (list, tuple)) else [dep]


def _dep_specs(dep):
    return [ANY] * len(_dep_args(dep))


def _pad_rows(t):
    return jnp.pad(t, ((0, -t.shape[0] % 8), (0, 0)))


def rmsnorm_fwd(x, g, out_dtype, name):
    M, D = x.shape
    tr = _tile(M, ROW_TILE)

    def body(x_ref, g_ref, h_ref, r_ref):
        xv = x_ref[...]
        r = lax.rsqrt(jnp.mean(xv * xv, axis=-1, keepdims=True) + EPS)
        h_ref[...] = (xv * r * g_ref[...]).astype(out_dtype)
        r_ref[...] = r

    return pl.pallas_call(
        body, name=name, grid=(M // tr,),
        in_specs=[pl.BlockSpec((tr, D), lambda i: (i, 0)), pl.BlockSpec((1, D), lambda i: (0, 0))],
        out_specs=[pl.BlockSpec((tr, D), lambda i: (i, 0)), pl.BlockSpec((tr, 1), lambda i: (i, 0))],
        out_shape=[jax.ShapeDtypeStruct((M, D), out_dtype), jax.ShapeDtypeStruct((M, 1), F32)],
        compiler_params=_params("parallel"),
    )(x, g)


def rmsnorm_bwd(dh, x, g, rstd, dres, name):
    M, D = x.shape
    tr = _tile(M, ROW_TILE)

    def body(dh_ref, x_ref, g_ref, r_ref, dres_ref, dx_ref, dg_ref, half_ref):
        @pl.when(pl.program_id(0) == 0)
        def _():
            dg_ref[...] = jnp.zeros_like(dg_ref)

        r = r_ref[...]
        xhat = x_ref[...] * r
        dhv = dh_ref[...]
        gd = dhv * g_ref[...]
        dx = dres_ref[...] + r * (gd - xhat * jnp.mean(gd * xhat, axis=-1, keepdims=True))
        dx_ref[...] = dx
        half_ref[...] = (0.5 * dx).astype(BF16)
        dg_ref[...] += jnp.sum(dhv * xhat, axis=0, keepdims=True)

    row = pl.BlockSpec((tr, D), lambda i: (i, 0))
    vec = pl.BlockSpec((1, D), lambda i: (0, 0))
    return pl.pallas_call(
        body, name=name, grid=(M // tr,),
        in_specs=[row, row, vec, pl.BlockSpec((tr, 1), lambda i: (i, 0)), row],
        out_specs=[row, vec, row],
        out_shape=[jax.ShapeDtypeStruct((M, D), F32), jax.ShapeDtypeStruct((1, D), F32),
                   jax.ShapeDtypeStruct((M, D), BF16)],
        compiler_params=_params("arbitrary"),
    )(dh, x, g, rstd, dres)


def final_loss(x, g, target):
    M, D = x.shape
    tr = _tile(M, ROW_TILE)

    def body(x_ref, g_ref, t_ref, loss_ref, dx_ref, dg_ref, half_ref):
        i = pl.program_id(0)

        @pl.when(i == 0)
        def _():
            loss_ref[...] = jnp.zeros_like(loss_ref)
            dg_ref[...] = jnp.zeros_like(dg_ref)

        live = (_rows(tr) + i * tr) >= OFF
        xv = x_ref[...]
        gv = g_ref[...]
        r = lax.rsqrt(jnp.mean(xv * xv, axis=-1, keepdims=True) + EPS)
        xhat = xv * r
        err = jnp.where(live, xhat * gv - t_ref[...], 0.0)
        loss_ref[...] += 0.5 * jnp.sum(jnp.mean(err * err, axis=-1, keepdims=True), axis=0, keepdims=True)
        dy = err * (1.0 / D)
        gd = dy * gv
        dx = r * (gd - xhat * jnp.mean(gd * xhat, axis=-1, keepdims=True))
        dx_ref[...] = dx
        half_ref[...] = (0.5 * dx).astype(BF16)
        dg_ref[...] += jnp.sum(dy * xhat, axis=0, keepdims=True)

    row = pl.BlockSpec((tr, D), lambda i: (i, 0))
    vec = pl.BlockSpec((1, D), lambda i: (0, 0))
    return pl.pallas_call(
        body, name="final_loss", grid=(M // tr,),
        in_specs=[row, vec, row],
        out_specs=[pl.BlockSpec((1, 1), lambda i: (0, 0)), row, vec, row],
        out_shape=[jax.ShapeDtypeStruct((1, 1), F32), jax.ShapeDtypeStruct((M, D), F32),
                   jax.ShapeDtypeStruct((1, D), F32), jax.ShapeDtypeStruct((M, D), BF16)],
        compiler_params=_params("arbitrary"),
    )(x, g, target)


def mm_nn(a, w, out_dtype, name, tm_target=704, tn_target=896):
    M, K = a.shape
    N = w.shape[1]
    tm, tn = _tile(M, tm_target), _tile(N, tn_target, 128)

    def body(a_ref, w_ref, o_ref):
        o_ref[...] = _dot(a_ref[...], w_ref[...]).astype(out_dtype)

    return pl.pallas_call(
        body, name=name, grid=(N // tn, M // tm),
        in_specs=[pl.BlockSpec((tm, K), lambda n, i: (i, 0)), pl.BlockSpec((K, tn), lambda n, i: (0, n))],
        out_specs=pl.BlockSpec((tm, tn), lambda n, i: (i, n)),
        out_shape=jax.ShapeDtypeStruct((M, N), out_dtype),
        compiler_params=_params("parallel", "parallel"),
    )(a, w)


def ffn_gateup(h, wg, wu, name):
    M, D = h.shape
    Fs = wg.shape[2]
    tm = _tile(M, 352)

    def body(h_ref, wg_ref, wu_ref, u_ref, silu_ref, dsilu_ref, a_ref):
        hv = h_ref[...]
        for cols in _chunks(Fs):
            g = _dot(hv, wg_ref[:, cols])
            u = _dot(hv, wu_ref[:, cols])
            s = _sigmoid(g)
            silu = g * s
            u_ref[:, cols] = u.astype(BF16)
            silu_ref[:, cols] = silu.astype(BF16)
            dsilu_ref[:, cols] = (s * (1.0 + g * (1.0 - s))).astype(BF16)
            a_ref[:, cols] = (silu * u).astype(BF16)

    wspec = pl.BlockSpec((None, D, Fs), lambda j, i: (j, 0, 0))
    ospec = pl.BlockSpec((tm, Fs), lambda j, i: (i, j))
    return pl.pallas_call(
        body, name=name, grid=(N_CHIPS, M // tm),
        in_specs=[pl.BlockSpec((tm, D), lambda j, i: (i, 0)), wspec, wspec],
        out_specs=[ospec] * 4,
        out_shape=[jax.ShapeDtypeStruct((M, N_CHIPS * Fs), BF16)] * 4,
        compiler_params=_params("parallel", "parallel"),
    )(h, wg, wu)


def mm_residual(a, w, x, scale, tk, name, tm_target=352):
    M, N = x.shape
    K = w.shape[0]
    tm = _tile(M, tm_target)

    def body(a_ref, w_ref, x_ref, o_ref, acc):
        k = pl.program_id(1)

        @pl.when(k == 0)
        def _():
            acc[...] = jnp.zeros_like(acc)

        acc[...] += _dot(a_ref[...], w_ref[...])

        @pl.when(k == pl.num_programs(1) - 1)
        def _():
            o_ref[...] = x_ref[...] + scale * acc[...]

    aspec = pl.BlockSpec((tm, tk), lambda i, k: (i, k))
    return pl.pallas_call(
        body, name=name, grid=(M // tm, K // tk),
        in_specs=[aspec, pl.BlockSpec((tk, N), lambda i, k: (k, 0)),
                  pl.BlockSpec((tm, N), lambda i, k: (i, 0), pipeline_mode=ONE_BUFFER)],
        out_specs=pl.BlockSpec((tm, N), lambda i, k: (i, 0), pipeline_mode=ONE_BUFFER),
        out_shape=jax.ShapeDtypeStruct((M, N), F32),
        scratch_shapes=[pltpu.VMEM((tm, N), F32)],
        compiler_params=_params("parallel", "arbitrary"),
    )(a, w, x)


def ffn_bwd_act(dy, wd, up, silu, dsilu, name, dep=None):
    M, D = dy.shape
    F = wd.shape[0]
    Fs = F // N_CHIPS
    tm = _tile(M, 704)

    def body(dy_ref, wd_ref, u_ref, silu_ref, dsilu_ref, *rest):
        dg_ref, du_ref = rest[-2:]
        dy = dy_ref[...]
        for cols in _chunks(Fs):
            da = _dot_nt(dy, wd_ref[cols, :])
            dg_ref[:, cols] = (da * u_ref[:, cols].astype(F32) * dsilu_ref[:, cols].astype(F32)).astype(BF16)
            du_ref[:, cols] = (da * silu_ref[:, cols].astype(F32)).astype(BF16)

    fspec = pl.BlockSpec((tm, Fs), lambda j, i: (i, j))
    return pl.pallas_call(
        body, name=name, grid=(N_CHIPS, M // tm),
        in_specs=[pl.BlockSpec((tm, D), lambda j, i: (i, 0)), pl.BlockSpec((Fs, D), lambda j, i: (j, 0)),
                  fspec, fspec, fspec] + _dep_specs(dep),
        out_specs=[fspec, fspec],
        out_shape=[jax.ShapeDtypeStruct((M, F), BF16)] * 2,
        compiler_params=_params("parallel", "parallel"),
    )(dy, wd, up, silu, dsilu, *_dep_args(dep))


def mm_tn(a, b, ta, tb, name, stacked_out=False, out_dtype=BF16, dep=None):
    T, Ma = a.shape
    Nb = b.shape[1]

    def body(a_ref, b_ref, *rest):
        o_ref = rest[-1]
        o_ref[...] = _dot_tn(a_ref[...], b_ref[...].astype(BF16)).astype(out_dtype)

    if stacked_out:
        out_spec = pl.BlockSpec((None, ta, tb), lambda jb, ja: (jb, ja, 0))
        out_shape = jax.ShapeDtypeStruct((Nb // tb, Ma, tb), out_dtype)
    else:
        out_spec = pl.BlockSpec((ta, tb), lambda jb, ja: (ja, jb))
        out_shape = jax.ShapeDtypeStruct((Ma, Nb), out_dtype)
    return pl.pallas_call(
        body, name=name, grid=(Nb // tb, Ma // ta),
        in_specs=[pl.BlockSpec((T, ta), lambda jb, ja: (0, ja)), pl.BlockSpec((T, tb), lambda jb, ja: (0, jb))]
        + _dep_specs(dep),
        out_specs=out_spec, out_shape=out_shape,
        compiler_params=_params("parallel", "parallel"),
    )(a, b, *_dep_args(dep))


def mm_nt(pairs, tm, tn, tk, name, stacked_w=False, dep=None):
    M, K = pairs[0][0].shape
    N = pairs[0][1].shape[1] if stacked_w else pairs[0][1].shape[0]
    n_pairs = len(pairs)

    def body(*refs):
        o_ref, acc = refs[-2:]
        k = pl.program_id(2)

        @pl.when(k == 0)
        def _():
            acc[...] = jnp.zeros_like(acc)

        for p in range(n_pairs):
            acc[...] += _dot_nt(refs[2 * p][...].astype(BF16), refs[2 * p + 1][...])

        @pl.when(k == pl.num_programs(2) - 1)
        def _():
            o_ref[...] = acc[...]

    aspec = pl.BlockSpec((tm, tk), lambda i, n, k: (i, k))
    if stacked_w:
        wspec = pl.BlockSpec((None, tn, tk), lambda i, n, k: (k, n, 0))
    else:
        wspec = pl.BlockSpec((tn, tk), lambda i, n, k: (n, k))
    return pl.pallas_call(
        body, name=name, grid=(M // tm, N // tn, K // tk),
        in_specs=[aspec, wspec] * n_pairs + _dep_specs(dep),
        out_specs=pl.BlockSpec((tm, tn), lambda i, n, k: (i, n), pipeline_mode=ONE_BUFFER),
        out_shape=jax.ShapeDtypeStruct((M, N), F32),
        scratch_shapes=[pltpu.VMEM((tm, tn), F32)],
        compiler_params=_params("parallel", "parallel", "arbitrary"),
    )(*[t for pair in pairs for t in pair], *_dep_args(dep))


def _tri(lower):
    r = lax.broadcasted_iota(jnp.int32, (CHUNK, CHUNK), 0)
    c = lax.broadcasted_iota(jnp.int32, (CHUNK, CHUNK), 1)
    return (r >= c) if lower else (r <= c)


def _tri_sum(mask, x, pieces):
    ones = mask.astype(BF16)
    acc = jnp.zeros_like(x)
    rest = x
    for _ in range(pieces):
        piece = rest.astype(BF16)
        acc = acc + _dot(ones, piece)
        rest = rest - piece.astype(F32)
    return acc


def _gla_gates(lr, wlr, blr, chunk):
    z = _dot(lr, wlr) + blr
    live = (_rows(CHUNK) + chunk * CHUNK) >= PAD
    lg = jnp.where(live, (jnp.minimum(z, 0.0) - jnp.log(1.0 + jnp.exp(-jnp.abs(z)))) * (1.0 / GATE_NORM), 0.0)
    b = _tri_sum(_tri(True), lg, 3)
    b_last = jnp.sum(lg, axis=0, keepdims=True)
    b_mid = jnp.sum(jnp.where(_rows(CHUNK) < CHUNK // 2, lg, 0.0), axis=0, keepdims=True)
    return z, live, b, b_last, b_mid


def _gla_specs(D, chunk_of):
    lr_blk = (3 * D) // LR_W
    return [
        pl.BlockSpec((CHUNK, D // 2), lambda c: (chunk_of(c), 0)),
        pl.BlockSpec((CHUNK, D // 2), lambda c: (chunk_of(c), 1)),
        pl.BlockSpec((CHUNK, D), lambda c: (chunk_of(c), 1)),
        pl.BlockSpec((CHUNK, LR_W), lambda c: (chunk_of(c), lr_blk)),
        pl.BlockSpec((LR_W, D // 2), lambda c: (0, 0)),
        pl.BlockSpec((1, D // 2), lambda c: (0, 0)),
    ]


def gla_fwd(proj, wlr, blr, D):
    M = proj.shape[0]
    n = M // CHUNK
    dkh, dvh = D // 2 // HEADS, D // HEADS
    qscale = float(dkh) ** -0.5

    def body(q_ref, k_ref, v_ref, lr_ref, wlr_ref, blr_ref, o_ref, st_ref, S):
        c = pl.program_id(0)

        @pl.when(c == 0)
        def _():
            S[...] = jnp.zeros_like(S)

        lr = lr_ref[...].astype(BF16)
        for h in range(HEADS):
            kc, vc = slice(h * dkh, (h + 1) * dkh), slice(h * dvh, (h + 1) * dvh)
            _, _, b, b_last, b_mid = _gla_gates(lr, wlr_ref[:, kc], blr_ref[:, kc], c)
            q = q_ref[:, kc] * qscale
            k = k_ref[:, kc]
            v = v_ref[:, vc].astype(BF16)
            s0 = S[h]
            st_ref[h] = s0
            qb = (q * jnp.exp(b)).astype(BF16)
            kb = (k * jnp.exp(b_last - b)).astype(BF16)
            qt = (q * jnp.exp(b - b_mid)).astype(BF16)
            kt = (k * jnp.exp(b_mid - b)).astype(BF16)
            a = jnp.where(_tri(True), _dot_nt(qt, kt), 0.0).astype(BF16)
            o_ref[:, vc] = _dot_nt(qb, s0.astype(BF16)) + _dot(a, v)
            S[h] = jnp.exp(b_last) * s0 + _dot_tn(v, kb)

    return pl.pallas_call(
        body, name="gla_fwd", grid=(n,),
        in_specs=_gla_specs(D, lambda c: c),
        out_specs=[pl.BlockSpec((CHUNK, D), lambda c: (c, 0)),
                   pl.BlockSpec((None, HEADS, dvh, dkh), lambda c: (c, 0, 0, 0))],
        out_shape=[jax.ShapeDtypeStruct((M, D), F32), jax.ShapeDtypeStruct((n, HEADS, dvh, dkh), F32)],
        scratch_shapes=[pltpu.VMEM((HEADS, dvh, dkh), F32)],
        compiler_params=_params("arbitrary"),
    )(proj, proj, proj, proj, wlr, blr)


def gla_bwd(proj, wlr, blr, st, do, dr, D):
    M = proj.shape[0]
    n = M // CHUNK
    dkh, dvh = D // 2 // HEADS, D // HEADS
    qscale = float(dkh) ** -0.5
    rev = lambda c: n - 1 - c

    def body(q_ref, k_ref, v_ref, lr_ref, wlr_ref, blr_ref, st_ref, do_ref, dr_ref,
             dp_ref, dwlr_ref, dblr_ref, dS):
        step = pl.program_id(0)
        c = n - 1 - step

        @pl.when(step == 0)
        def _():
            dS[...] = jnp.zeros_like(dS)
            dwlr_ref[...] = jnp.zeros_like(dwlr_ref)
            dblr_ref[...] = jnp.zeros_like(dblr_ref)

        lr = lr_ref[...].astype(BF16)
        lower = _tri(True)
        dlr = None
        for h in range(HEADS):
            kc, vc = slice(h * dkh, (h + 1) * dkh), slice(h * dvh, (h + 1) * dvh)
            wlr_h = wlr_ref[:, kc]
            z, live, b, b_last, b_mid = _gla_gates(lr, wlr_h, blr_ref[:, kc], c)
            q = q_ref[:, kc] * qscale
            k = k_ref[:, kc]
            v = v_ref[:, vc].astype(BF16)
            dov = do_ref[:, vc].astype(BF16)
            s0 = st_ref[h]
            ds1 = dS[h]
            ds1b = ds1.astype(BF16)
            e_b, e_lb = jnp.exp(b), jnp.exp(b_last - b)
            e_bm, e_mb = jnp.exp(b - b_mid), jnp.exp(b_mid - b)
            e_last = jnp.exp(b_last)
            qb, kb, qt, kt = q * e_b, k * e_lb, q * e_bm, k * e_mb
            qbb, kbb, qtb, ktb = qb.astype(BF16), kb.astype(BF16), qt.astype(BF16), kt.astype(BF16)
            a = jnp.where(lower, _dot_nt(qtb, ktb), 0.0).astype(BF16)
            da = jnp.where(lower, _dot_nt(dov, v), 0.0).astype(BF16)

            dqb = _dot(dov, s0.astype(BF16))
            dqt = _dot(da, ktb)
            dkt = _dot_tn(da, qtb)
            dkb = _dot(v, ds1b)
            keep = live.astype(F32)
            dp_ref[:, D + h * dvh:D + (h + 1) * dvh] = (keep * (_dot_tn(a, dov) + _dot_nt(kbb, ds1b))).astype(BF16)
            dp_ref[:, kc] = (keep * qscale * (dqb * e_b + dqt * e_bm)).astype(BF16)
            dp_ref[:, D // 2 + h * dkh:D // 2 + (h + 1) * dkh] = (keep * (dkb * e_lb + dkt * e_mb)).astype(BF16)

            db = dqb * qb - dkb * kb + dqt * qt - dkt * kt
            db_last = (jnp.sum(dkb * kb, axis=0, keepdims=True)
                       + jnp.sum(ds1 * s0, axis=0, keepdims=True) * e_last)
            db = db + jnp.where(_rows(CHUNK) == CHUNK - 1, db_last, 0.0)
            dlg = jnp.where(live, _tri_sum(_tri(False), db, 2), 0.0)
            dz = dlg * (1.0 / GATE_NORM) / (1.0 + jnp.exp(z))
            dzb = dz.astype(BF16)

            dlr_h = _dot_nt(dzb, wlr_h)
            dlr = dlr_h if dlr is None else dlr + dlr_h
            dwlr_ref[h] += _dot_tn(lr, dzb)
            dblr_ref[h] += jnp.sum(dz, axis=0, keepdims=True)
            dS[h] = e_last * ds1 + _dot_tn(dov, qbb)
        dp_ref[:, 2 * D:3 * D] = dr_ref[...]
        dp_ref[:, 3 * D:] = dlr.astype(BF16)

    row = pl.BlockSpec((CHUNK, D), lambda c: (rev(c), 0))
    return pl.pallas_call(
        body, name="gla_bwd", grid=(n,),
        in_specs=_gla_specs(D, rev) + [pl.BlockSpec((None, HEADS, dvh, dkh), lambda c: (rev(c), 0, 0, 0)), row, row],
        out_specs=[pl.BlockSpec((CHUNK, 3 * D + LR_W), lambda c: (rev(c), 0)),
                   pl.BlockSpec((HEADS, LR_W, dkh), lambda c: (0, 0, 0)),
                   pl.BlockSpec((HEADS, 1, dkh), lambda c: (0, 0, 0))],
        out_shape=[jax.ShapeDtypeStruct((M, 3 * D + LR_W), BF16),
                   jax.ShapeDtypeStruct((HEADS, LR_W, dkh), F32), jax.ShapeDtypeStruct((HEADS, 1, dkh), F32)],
        scratch_shapes=[pltpu.VMEM((HEADS, dvh, dkh), F32)],
        compiler_params=_params("arbitrary"),
    )(proj, proj, proj, proj, wlr, blr, st, do, dr)


def gla_post_fwd(o, proj, head_norm, D):
    M = o.shape[0]
    dvh = D // HEADS
    tr = _tile(M, ROW_TILE)

    def body(o_ref, r_ref, hn_ref, out_ref):
        for hd in range(HEADS):
            cols = slice(hd * dvh, (hd + 1) * dvh)
            ov = o_ref[:, cols]
            rs = lax.rsqrt(jnp.mean(ov * ov, axis=-1, keepdims=True) + EPS)
            rv = r_ref[:, cols]
            out_ref[:, cols] = (ov * rs * hn_ref[...] * (rv * _sigmoid(rv))).astype(BF16)

    row = pl.BlockSpec((tr, D), lambda i: (i, 0))
    return pl.pallas_call(
        body, name="gla_post_fwd", grid=(M // tr,),
        in_specs=[row, pl.BlockSpec((tr, D), lambda i: (i, 2)), pl.BlockSpec((1, dvh), lambda i: (0, 0))],
        out_specs=row, out_shape=jax.ShapeDtypeStruct((M, D), BF16),
        compiler_params=_params("parallel"),
    )(o, proj, head_norm)


def gla_post_bwd(dgated, o, proj, head_norm, D):
    M = o.shape[0]
    dvh = D // HEADS
    tr = _tile(M, ROW_TILE)

    def body(dg_ref, o_ref, r_ref, hn_ref, do_ref, dr_ref, dhn_ref):
        @pl.when(pl.program_id(0) == 0)
        def _():
            dhn_ref[...] = jnp.zeros_like(dhn_ref)

        hn = hn_ref[...]
        dhn = jnp.zeros((1, dvh), F32)
        for hd in range(HEADS):
            cols = slice(hd * dvh, (hd + 1) * dvh)
            ov = o_ref[:, cols]
            rs = lax.rsqrt(jnp.mean(ov * ov, axis=-1, keepdims=True) + EPS)
            ohat = ov * rs
            rv = r_ref[:, cols]
            s = _sigmoid(rv)
            dgv = dg_ref[:, cols]
            don = dgv * (rv * s)
            dr_ref[:, cols] = (dgv * ohat * hn * (s * (1.0 + rv * (1.0 - s)))).astype(BF16)
            gd = don * hn
            do_ref[:, cols] = rs * (gd - ohat * jnp.mean(gd * ohat, axis=-1, keepdims=True))
            dhn = dhn + jnp.sum(don * ohat, axis=0, keepdims=True)
        dhn_ref[...] += dhn

    row = pl.BlockSpec((tr, D), lambda i: (i, 0))
    vec = pl.BlockSpec((1, dvh), lambda i: (0, 0))
    return pl.pallas_call(
        body, name="gla_post_bwd", grid=(M // tr,),
        in_specs=[row, row, pl.BlockSpec((tr, D), lambda i: (i, 2)), vec],
        out_specs=[row, row, vec],
        out_shape=[jax.ShapeDtypeStruct((M, D), F32), jax.ShapeDtypeStruct((M, D), BF16),
                   jax.ShapeDtypeStruct((1, dvh), F32)],
        compiler_params=_params("arbitrary"),
    )(dgated, o, proj, head_norm)


def _pool_counts(M, g):
    t = _rows(M) - PAD
    win = jnp.left_shift(2, g)
    return t >= 0, jnp.maximum(jnp.minimum(t + 1, win), 1).astype(F32)


def _window_sum(x, g, M, back):
    sums = []
    s = x
    for lvl in range(4):
        sh = 1 << lvl
        s = s + pltpu.roll(s, (M - sh) if back else sh, 0)
        sums.append(s)
    return jnp.where(g == 0, sums[0], jnp.where(g == 1, sums[1], jnp.where(g == 2, sums[2], sums[3])))


POOL_COLS = 128


def pool_window(hp):
    M, D = hp.shape
    cw = min(POOL_COLS, D // 4)
    per_group = (D // 4) // cw

    def body(h_ref, p_ref):
        g = pl.program_id(0) // per_group
        live, cnt = _pool_counts(M, g)
        hv = h_ref[...]
        p_ref[...] = jnp.where(live, _window_sum(hv, g, M, False) / cnt - hv, 0.0).astype(BF16)

    col = pl.BlockSpec((M, cw), lambda j: (0, j))
    return pl.pallas_call(
        body, name="pool_window", grid=(D // cw,), in_specs=[col], out_specs=col,
        out_shape=jax.ShapeDtypeStruct((M, D), BF16), compiler_params=_params("parallel"),
    )(hp)


def pool_window_bwd(dpooled):
    M, D = dpooled.shape
    cw = min(POOL_COLS, D // 4)
    per_group = (D // 4) // cw

    def body(d_ref, o_ref):
        g = pl.program_id(0) // per_group
        live, cnt = _pool_counts(M, g)
        dv = jnp.where(live, d_ref[...], 0.0)
        o_ref[...] = jnp.where(live, _window_sum(dv / cnt, g, M, True) - dv, 0.0)

    col = pl.BlockSpec((M, cw), lambda j: (0, j))
    return pl.pallas_call(
        body, name="pool_window_bwd", grid=(D // cw,), in_specs=[col], out_specs=col,
        out_shape=jax.ShapeDtypeStruct((M, D), F32), compiler_params=_params("parallel"),
    )(dpooled)


def pool_mix(pooled, x, w, bias, scale):
    M, D = x.shape
    W = D // 4
    tm = _tile(M, 352)

    def body(p_ref, x_ref, w_ref, b_ref, s_ref, out_ref):
        live = (_rows(tm) + pl.program_id(1) * tm) >= PAD
        y = (_dot(p_ref[...], w_ref[...]) + b_ref[...]) * s_ref[...]
        out_ref[...] = x_ref[...] + jnp.where(live, y, 0.0)

    blk = pl.BlockSpec((tm, W), lambda g, i: (i, g))
    vec = pl.BlockSpec((1, W), lambda g, i: (0, g))
    return pl.pallas_call(
        body, name="pool_mix", grid=(4, M // tm),
        in_specs=[blk, blk, pl.BlockSpec((None, W, W), lambda g, i: (g, 0, 0)), vec, vec],
        out_specs=blk, out_shape=jax.ShapeDtypeStruct((M, D), F32),
        compiler_params=_params("parallel", "parallel"),
    )(pooled, x, w, bias, scale)


def pool_mix_bwd(dy, pooled, w, bias, scale, dep=None):
    M, D = dy.shape
    W = D // 4
    tm = _tile(M, 352)

    def body(dy_ref, p_ref, w_ref, b_ref, s_ref, *rest):
        dp_ref, dw_ref, db_ref, ds_ref, acc_w = rest[-5:]
        i = pl.program_id(1)

        @pl.when(i == 0)
        def _():
            acc_w[...] = jnp.zeros_like(acc_w)
            db_ref[...] = jnp.zeros_like(db_ref)
            ds_ref[...] = jnp.zeros_like(ds_ref)

        live = (_rows(tm) + i * tm) >= PAD
        dyv = jnp.where(live, dy_ref[...], 0.0)
        pooled = p_ref[...]
        wv = w_ref[...]
        ds_ref[...] += jnp.sum(dyv * (_dot(pooled, wv) + b_ref[...]), axis=0, keepdims=True)
        dys = dyv * s_ref[...]
        db_ref[...] += jnp.sum(dys, axis=0, keepdims=True)
        dysb = dys.astype(BF16)
        acc_w[...] += _dot_tn(pooled, dysb)
        dp_ref[...] = _dot_nt(dysb, wv)

        @pl.when(i == pl.num_programs(1) - 1)
        def _():
            dw_ref[...] = acc_w[...].astype(BF16)

    blk = pl.BlockSpec((tm, W), lambda g, i: (i, g))
    vec = pl.BlockSpec((1, W), lambda g, i: (0, g))
    wspec = pl.BlockSpec((None, W, W), lambda g, i: (g, 0, 0))
    return pl.pallas_call(
        body, name="pool_mix_bwd", grid=(4, M // tm),
        in_specs=[blk, blk, wspec, vec, vec] + _dep_specs(dep),
        out_specs=[blk, wspec, vec, vec],
        out_shape=[jax.ShapeDtypeStruct((M, D), F32), jax.ShapeDtypeStruct((4, W, W), BF16),
                   jax.ShapeDtypeStruct((1, D), F32), jax.ShapeDtypeStruct((1, D), F32)],
        scratch_shapes=[pltpu.VMEM((W, W), F32)],
        compiler_params=_params("parallel", "arbitrary"),
    )(dy, pooled, w, bias, scale, *_dep_args(dep))


def adamw(w, g, m, v, name, copy_g=False):
    shape = w.shape
    C = shape[-1]
    R = w.size // C
    tr = _tile(R, 256, 8)
    tc = C
    if tr == R and R > 256:
        tc = _tile(C, 256, 128)

    def body(w_ref, g_ref, m_ref, v_ref, d_ref, nm_ref, nv_ref, *g_out):
        gv = g_ref[...]
        for ref in g_out:
            ref[...] = gv
        nm = ADAM_B1 * m_ref[...] + (1.0 - ADAM_B1) * gv
        nv = ADAM_B2 * v_ref[...] + (1.0 - ADAM_B2) * (gv * gv)
        m_hat = nm / (1.0 - ADAM_B1 ** ADAM_STEP)
        v_hat = nv / (1.0 - ADAM_B2 ** ADAM_STEP)
        d_ref[...] = -ADAM_LR * (m_hat / (jnp.sqrt(v_hat) + ADAM_EPS) + ADAM_WD * w_ref[...])
        nm_ref[...] = nm
        nv_ref[...] = nv

    spec = pl.BlockSpec((tr, tc), lambda i, j: (i, j))
    outs = pl.pallas_call(
        body, name=name, grid=(R // tr, C // tc),
        in_specs=[spec] * 4, out_specs=[spec] * (3 + copy_g),
        out_shape=[jax.ShapeDtypeStruct((R, C), F32)] * (3 + copy_g),
        compiler_params=_params("parallel", "parallel"),
    )(*[t.reshape(R, C) for t in (w, g, m, v)])
    return [t.reshape(shape) for t in outs]


def add_sibling(grad, recv, core, name):
    _, _, Rh, C = grad.shape
    tr = _tile(Rh, 512)

    def body(core_ref, g_ref, r_ref, o_ref):
        o_ref[...] = (g_ref[...].astype(F32) + r_ref[...].astype(F32)).astype(BF16)

    return pl.pallas_call(
        body, name=name,
        grid_spec=pltpu.PrefetchScalarGridSpec(
            num_scalar_prefetch=1, grid=(N_CHIPS, Rh // tr),
            in_specs=[pl.BlockSpec((None, None, tr, C), lambda j, i, core_ref: (j, core_ref[0], i, 0)),
                      pl.BlockSpec((None, tr, C), lambda j, i, core_ref: (j, i, 0))],
            out_specs=pl.BlockSpec((None, tr, C), lambda j, i, core_ref: (j, i, 0))),
        out_shape=jax.ShapeDtypeStruct((N_CHIPS, Rh, C), BF16),
        compiler_params=_params("parallel", "parallel"),
    )(core, grad, recv)


def add_chips(part, recv, chip, core, group, n, mi, name):
    _, Rh, C = part.shape
    tr = _tile(Rh, 512)

    def body(chip_ref, core_ref, p_ref, r_ref, *rest):
        o_ref = rest[-1]
        acc = p_ref[...].astype(F32)
        for k in range(N_CHIPS - 1):
            acc = acc + r_ref[k].astype(F32)
        o_ref[...] = acc

    carried = [] if group is None else [group]
    return pl.pallas_call(
        body, name=name,
        grid_spec=pltpu.PrefetchScalarGridSpec(
            num_scalar_prefetch=2, grid=(Rh // tr,),
            in_specs=[pl.BlockSpec((None, tr, C), lambda i, chip_ref, core_ref: (chip_ref[0], i, 0)),
                      pl.BlockSpec((N_CHIPS - 1, tr, C), lambda i, chip_ref, core_ref: (0, i, 0))]
            + [ANY] * len(carried),
            out_specs=pl.BlockSpec((None, None, tr, C), lambda i, chip_ref, core_ref: (mi, core_ref[0], i, 0))),
        out_shape=jax.ShapeDtypeStruct((n, 2, Rh, C), F32),
        input_output_aliases={4: 0} if carried else {},
        compiler_params=_params("parallel"),
    )(chip, core, part, recv, *carried)


def stage_shard(shard, mi, chip, name, dep=None):
    _, _, Rh, C = shard.shape
    tr = _tile(Rh, 512)

    def body(chip_ref, s_ref, *rest):
        rest[-1][...] = s_ref[...].astype(BF16)

    return pl.pallas_call(
        body, name=name,
        grid_spec=pltpu.PrefetchScalarGridSpec(
            num_scalar_prefetch=1, grid=(2, Rh // tr),
            in_specs=[pl.BlockSpec((None, None, tr, C), lambda h, i, chip_ref: (mi, h, i, 0))] + _dep_specs(dep),
            out_specs=pl.BlockSpec((None, None, tr, C), lambda h, i, chip_ref: (chip_ref[0], h, i, 0))),
        out_shape=jax.ShapeDtypeStruct((N_CHIPS, 2, Rh, C), BF16),
        compiler_params=_params("parallel", "parallel"),
    )(chip, shard, *_dep_args(dep))


def sum_devices(gathered):
    _, R, C = gathered.shape

    def body(g_ref, o_ref):
        acc = g_ref[0]
        for d in range(1, N_DEV):
            acc = acc + g_ref[d]
        o_ref[...] = acc

    return pl.pallas_call(
        body, name="sum_devices", grid=(1,),
        in_specs=[pl.BlockSpec((N_DEV, R, C), lambda i: (0, 0, 0))],
        out_specs=pl.BlockSpec((R, C), lambda i: (0, 0)),
        out_shape=jax.ShapeDtypeStruct((R, C), F32),
        compiler_params=_params("arbitrary"),
    )(gathered)


def _place():
    x, y, c = lax.axis_index("x"), lax.axis_index("y"), lax.axis_index("c")
    others = [(1 - x, y), (x, 1 - y), (1 - x, 1 - y)]
    return x, y, c, others


def _remote(src, dst, send_sems, recv_sems, idx, device):
    return pltpu.make_async_remote_copy(src_ref=src, dst_ref=dst, send_sem=send_sems.at[idx],
                                        recv_sem=recv_sems.at[idx], device_id=device, device_id_type=MESH)


HBM = pl.BlockSpec(memory_space=pltpu.HBM)
SEM = pl.BlockSpec(memory_space=pltpu.SEMAPHORE)
EFFECT = pltpu.SideEffectType.DATAFLOW_SIDE_EFFECTING


def _in_hbm(t):
    return pltpu.with_memory_space_constraint(t, pltpu.HBM)


def _own_slice(buf, me, c):
    return buf.at[me, c] if len(buf.shape) == 4 else buf.at[me]


def gather_start(staged, bucket_sizes, name):
    n, nb = len(staged), len(bucket_sizes)

    def body(*refs):
        in_refs, sems, token = refs[:n], refs[n:n + 2 * nb], refs[-1]
        x, y, c, others = _place()
        me = 2 * x + y
        t = 0
        for b, size in enumerate(bucket_sizes):
            for i in range(size):
                mine = _own_slice(in_refs[t], me, c)
                for k, chip in enumerate(others):
                    _remote(mine, mine, sems[2 * b], sems[2 * b + 1], 3 * i + k, (*chip, c)).start()
                t += 1
        token[...] = jnp.zeros_like(token)

    sem_shapes = [pltpu.SemaphoreType.DMA((3 * size,)) for size in bucket_sizes for _ in range(2)]
    outs = pl.pallas_call(
        body, name=name,
        out_shape=sem_shapes + [pltpu.HBM(s.shape, s.dtype) for s in staged] + [jax.ShapeDtypeStruct((8, 128), F32)],
        in_specs=[HBM] * n, out_specs=[SEM] * (2 * nb) + [HBM] * n + [pl.BlockSpec(memory_space=pltpu.VMEM)],
        input_output_aliases={t: 2 * nb + t for t in range(n)},
        compiler_params=pltpu.CompilerParams(has_side_effects=EFFECT),
    )(*[_in_hbm(s) for s in staged])
    sems = [(outs[2 * b], outs[2 * b + 1]) for b in range(nb)]
    return sems, list(outs[2 * nb:2 * nb + n]), outs[-1]


def gather_wait(bufs, sems, after, name):
    n = len(bufs)

    def body(*refs):
        in_refs, send_sems, recv_sems = refs[:n], refs[n], refs[n + 1]
        x, y, c, others = _place()
        me = 2 * x + y
        for i in range(n):
            mine = _own_slice(in_refs[i], me, c)
            for k, (ox, oy) in enumerate(others):
                cp = _remote(mine, _own_slice(in_refs[i], 2 * ox + oy, c), send_sems, recv_sems, 3 * i + k,
                             (ox, oy, c))
                cp.wait_send()
                cp.wait_recv()

    return pl.pallas_call(
        body, name=name, out_shape=[pltpu.HBM(b.shape, b.dtype) for b in bufs],
        in_specs=[HBM] * n + [SEM, SEM, ANY], out_specs=[HBM] * n,
        input_output_aliases={t: t for t in range(n)},
        compiler_params=pltpu.CompilerParams(has_side_effects=EFFECT),
    )(*bufs, *sems, after)


def forward_to_sibling(bufs, name):
    n = len(bufs)

    def body(*refs):
        out_refs, (send_sems, recv_sems) = refs[n:2 * n], refs[2 * n:]
        x, y, c, others = _place()
        sibling = (x, y, 1 - c)
        copies = []
        for t in range(n):
            for k, (ox, oy) in enumerate(others):
                mine = out_refs[t].at[2 * ox + oy, c]
                cp = _remote(mine, mine, send_sems, recv_sems, 3 * t + k, sibling)
                cp.start()
                copies.append(cp)
        for t in range(n):
            for k, (ox, oy) in enumerate(others):
                theirs = out_refs[t].at[2 * ox + oy, 1 - c]
                _remote(theirs, theirs, send_sems, recv_sems, 3 * t + k, sibling).wait_recv()
        for cp in copies:
            cp.wait_send()

    return pl.pallas_call(
        body, name=name, in_specs=[ANY] * n, out_specs=[ANY] * n,
        out_shape=[jax.ShapeDtypeStruct(b.shape, b.dtype) for b in bufs],
        input_output_aliases={t: t for t in range(n)},
        scratch_shapes=[pltpu.SemaphoreType.DMA((3 * n,)), pltpu.SemaphoreType.DMA((3 * n,))],
    )(*bufs)


def sibling_start(grads, name):
    n = len(grads)
    lands = [lax.empty((N_CHIPS,) + g.shape[2:], g.dtype) for g in grads]

    def body(*refs):
        in_refs, land_refs, send_sems, recv_sems, token = refs[:n], refs[n:2 * n], refs[2 * n], refs[2 * n + 1], refs[-1]
        x, y, c, _ = _place()
        for t in range(n):
            for j in range(N_CHIPS):
                _remote(in_refs[t].at[j, 1 - c], land_refs[t].at[j], send_sems, recv_sems, N_CHIPS * t + j,
                        (x, y, 1 - c)).start()
        token[...] = jnp.zeros_like(token)

    outs = pl.pallas_call(
        body, name=name,
        out_shape=[pltpu.SemaphoreType.DMA((N_CHIPS * n,))] * 2 + [pltpu.HBM(t.shape, t.dtype) for t in grads + lands]
        + [jax.ShapeDtypeStruct((8, 128), F32)],
        in_specs=[HBM] * (2 * n), out_specs=[SEM, SEM] + [HBM] * (2 * n) + [pl.BlockSpec(memory_space=pltpu.VMEM)],
        input_output_aliases={t: 2 + t for t in range(2 * n)},
        compiler_params=pltpu.CompilerParams(has_side_effects=EFFECT),
    )(*[_in_hbm(t) for t in grads + lands])
    return (outs[0], outs[1]), list(outs[2:2 + n]), list(outs[2 + n:2 + 2 * n]), outs[-1]


def sibling_wait(grads, lands, sems, after, name):
    n = len(grads)

    def body(*refs):
        in_refs, land_refs, send_sems, recv_sems = refs[:n], refs[n:2 * n], refs[2 * n], refs[2 * n + 1]
        x, y, c, _ = _place()
        for t in range(n):
            for j in range(N_CHIPS):
                cp = _remote(in_refs[t].at[j, 1 - c], land_refs[t].at[j], send_sems, recv_sems, N_CHIPS * t + j,
                             (x, y, 1 - c))
                cp.wait_send()
                cp.wait_recv()

    outs = pl.pallas_call(
        body, name=name, out_shape=[pltpu.HBM(t.shape, t.dtype) for t in grads + lands],
        in_specs=[HBM] * (2 * n) + [SEM, SEM, ANY], out_specs=[HBM] * (2 * n),
        input_output_aliases={t: t for t in range(2 * n)},
        compiler_params=pltpu.CompilerParams(has_side_effects=EFFECT),
    )(*grads, *lands, *sems, after)
    return list(outs[:n]), list(outs[n:])


def reduce_start(parts, name):
    n = len(parts)
    lands = [lax.empty((N_CHIPS - 1,) + p.shape[1:], p.dtype) for p in parts]

    def body(*refs):
        in_refs, land_refs, send_sems, recv_sems, token = refs[:n], refs[n:2 * n], refs[2 * n], refs[2 * n + 1], refs[-1]
        x, y, c, others = _place()
        for t in range(n):
            for k, (ox, oy) in enumerate(others):
                _remote(in_refs[t].at[2 * ox + oy], land_refs[t].at[k], send_sems, recv_sems, 3 * t + k,
                        (ox, oy, c)).start()
        token[...] = jnp.zeros_like(token)

    outs = pl.pallas_call(
        body, name=name,
        out_shape=[pltpu.SemaphoreType.DMA((3 * n,))] * 2 + [pltpu.HBM(t.shape, t.dtype) for t in parts + lands]
        + [jax.ShapeDtypeStruct((8, 128), F32)],
        in_specs=[HBM] * (2 * n), out_specs=[SEM, SEM] + [HBM] * (2 * n) + [pl.BlockSpec(memory_space=pltpu.VMEM)],
        input_output_aliases={t: 2 + t for t in range(2 * n)},
        compiler_params=pltpu.CompilerParams(has_side_effects=EFFECT),
    )(*[_in_hbm(t) for t in parts + lands])
    return (outs[0], outs[1]), list(outs[2:2 + n]), list(outs[2 + n:2 + 2 * n]), outs[-1]


def reduce_wait(parts, lands, sems, after, name):
    n = len(parts)

    def body(*refs):
        in_refs, land_refs, send_sems, recv_sems = refs[:n], refs[n:2 * n], refs[2 * n], refs[2 * n + 1]
        x, y, c, others = _place()
        for t in range(n):
            for k, (ox, oy) in enumerate(others):
                cp = _remote(in_refs[t].at[2 * ox + oy], land_refs[t].at[k], send_sems, recv_sems, 3 * t + k,
                             (ox, oy, c))
                cp.wait_send()
                cp.wait_recv()

    outs = pl.pallas_call(
        body, name=name, out_shape=[pltpu.HBM(t.shape, t.dtype) for t in parts + lands],
        in_specs=[HBM] * (2 * n) + [SEM, SEM] + _dep_specs(after), out_specs=[HBM] * (2 * n),
        input_output_aliases={t: t for t in range(2 * n)},
        compiler_params=pltpu.CompilerParams(has_side_effects=EFFECT),
    )(*parts, *lands, *sems, *_dep_args(after))
    return list(outs[:n]), list(outs[n:])


def exchange_halves(groups, name):
    n_groups = len(groups)
    slots = [(gi, mi) for gi, grp in enumerate(groups) for mi in range(grp.shape[0])]

    def body(*refs):
        out_refs = refs[n_groups:2 * n_groups]
        send_sems, recv_sems = refs[2 * n_groups:]
        x, y, c, _ = _place()
        sibling = (x, y, 1 - c)
        copies = []
        for t, (gi, mi) in enumerate(slots):
            mine = out_refs[gi].at[mi, c]
            cp = _remote(mine, mine, send_sems, recv_sems, t, sibling)
            cp.start()
            copies.append(cp)
        for t, (gi, mi) in enumerate(slots):
            theirs = out_refs[gi].at[mi, 1 - c]
            _remote(theirs, theirs, send_sems, recv_sems, t, sibling).wait_recv()
        for cp in copies:
            cp.wait_send()

    return pl.pallas_call(
        body, name=name, in_specs=[ANY] * n_groups, out_specs=[ANY] * n_groups,
        out_shape=[jax.ShapeDtypeStruct(g.shape, g.dtype) for g in groups],
        input_output_aliases={gi: gi for gi in range(n_groups)},
        scratch_shapes=[pltpu.SemaphoreType.DMA((len(slots),)), pltpu.SemaphoreType.DMA((len(slots),))],
    )(*groups)


def gather_devices(buf):
    def body(in_ref, out_ref, send_sems, recv_sems, local_sem):
        x, y, c, _ = _place()
        me = 4 * x + 2 * y + c
        local = pltpu.make_async_copy(in_ref, out_ref.at[me], local_sem)
        local.start()
        copies = []
        for k in range(1, N_DEV):
            fx, fy, fc = (k >> 2) & 1, (k >> 1) & 1, k & 1
            peer = (x ^ fx, y ^ fy, c ^ fc)
            cp = _remote(in_ref, out_ref.at[me], send_sems, recv_sems, k - 1, peer)
            cp.start()
            copies.append(cp)
        for k in range(1, N_DEV):
            fx, fy, fc = (k >> 2) & 1, (k >> 1) & 1, k & 1
            theirs = out_ref.at[4 * (x ^ fx) + 2 * (y ^ fy) + (c ^ fc)]
            _remote(theirs, theirs, send_sems, recv_sems, k - 1, (x, y, c)).wait_recv()
        for cp in copies:
            cp.wait_send()
        local.wait()

    return pl.pallas_call(
        body, name="gather_devices", in_specs=[ANY], out_specs=ANY,
        out_shape=jax.ShapeDtypeStruct((N_DEV,) + buf.shape, buf.dtype),
        scratch_shapes=[pltpu.SemaphoreType.DMA((N_DEV - 1,)), pltpu.SemaphoreType.DMA((N_DEV - 1,)),
                        pltpu.SemaphoreType.DMA],
    )(buf)


class GradReducer:
    def __init__(self, core, chip, kinds):
        self.core, self.chip = core, chip
        self.sizes = dict(kinds)
        self.groups = {kind: None for kind, _ in kinds}

    def send(self, grads, tag):
        arrays = [g.reshape(N_CHIPS, 2, -1, g.shape[-1]) for g, _, _ in grads]
        sems, arrays, lands, token = sibling_start(arrays, f"reduce_sibling_start_{tag}")
        return (sems, arrays, lands, [(kind, mi) for _, kind, mi in grads], tag), token

    def begin(self, sent, after, tag):
        parts, slots = [], []
        for sems, arrays, lands, sent_slots, sent_tag in sent:
            arrays, lands = sibling_wait(arrays, lands, sems, after, f"reduce_sibling_wait_{sent_tag}")
            parts += [add_sibling(g, r, self.core, f"reduce_add_sibling_{sent_tag}_{t}")
                      for t, (g, r) in enumerate(zip(arrays, lands))]
            slots += sent_slots
        sems, parts, lands, token = reduce_start(parts, f"reduce_start_{tag}")
        return (sems, parts, lands, slots, tag), token

    def end(self, state, after):
        sems, parts, lands, slots, tag = state
        parts, lands = reduce_wait(parts, lands, sems, after, f"reduce_wait_{tag}")
        for t, (kind, mi) in enumerate(slots):
            self.groups[kind] = add_chips(parts[t], lands[t], self.chip, self.core, self.groups[kind],
                                          self.sizes[kind], mi, f"reduce_add_chips_{tag}_{t}")

    def finish(self):
        kinds = list(self.groups)
        return dict(zip(kinds, exchange_halves([self.groups[k] for k in kinds], "reduce_swap")))


def _ffn_fwd(x, gain, wg, wu, wd, tag):
    h, rstd = rmsnorm_fwd(x, gain, BF16, f"ffn_norm_{tag}")
    up, silu, dsilu, act = ffn_gateup(h, wg, wu, f"ffn_gateup_{tag}")
    out = mm_residual(act, wd, x, 0.5, wd.shape[0] // N_CHIPS, f"ffn_down_{tag}", tm_target=704)
    return out, (x, gain, h, rstd, up, silu, dsilu, act)


def _ffn_bwd(dout, dy, saved, wg, wu, wd, index, reducer, dep=None, per_tensor=False):
    x, gain, h, rstd, up, silu, dsilu, act = saved
    D = x.shape[1]
    Fs = wg.shape[2]
    td = _tile(D, 512, 128)
    tag = f"ffn{index}"
    begun = []

    def begin(sent, after, suffix):
        state, token = reducer.begin(sent, after, tag + suffix)
        begun.append(state)
        return token

    dgate, dup = ffn_bwd_act(dy, wd, up, silu, dsilu, f"ffn_bwd_act_{index}", dep=dep)
    d_wd = mm_tn(act, dy, Fs, td, f"ffn_bwd_wd_{index}")
    if per_tensor:
        sent_d, tok = reducer.send([(d_wd, "down", index)], tag + "d")
        d_wg = mm_tn(h, dgate, td, Fs, f"ffn_bwd_wg_{index}", stacked_out=True, dep=tok)
        toks = [begin([sent_d], d_wg, "d")]
        sent_g, tok = reducer.send([(d_wg, "gate", index)], tag + "g")
        d_wu = mm_tn(h, dup, td, Fs, f"ffn_bwd_wu_{index}", stacked_out=True, dep=toks + [tok])
        toks = [begin([sent_g], d_wu, "g")]
        sent, tok = reducer.send([(d_wu, "up", index)], tag + "u")
    else:
        d_wg = mm_tn(h, dgate, td, Fs, f"ffn_bwd_wg_{index}", stacked_out=True)
        d_wu = mm_tn(h, dup, td, Fs, f"ffn_bwd_wu_{index}", stacked_out=True)
        toks = []
        sent, tok = reducer.send([(d_wd, "down", index), (d_wg, "gate", index), (d_wu, "up", index)], tag + "u")
    dh = mm_nt([(dgate, wg), (dup, wu)], _tile(x.shape[0], 704), D, Fs, f"ffn_bwd_dh_{index}", stacked_w=True,
               dep=toks + [tok])
    tok = begin([sent], dh, "u")
    dx, dgain, dx_half = rmsnorm_bwd(dh, x, gain, rstd, dout, f"ffn_norm_bwd_{index}")
    return dx, dx_half, dgain, begun, tok


def kernel(x, meta, ffn_norm, ffn_w_gate, ffn_w_up, ffn_w_down, gla_norm, gla_w_in, gla_w_lr, gla_b_lr, gla_head_norm, gla_w_out, pool_norm, pool_w, pool_b, pool_scale, final_norm, loss_target, m_meta, m_ffn_norm, m_ffn_w_gate, m_ffn_w_up, m_ffn_w_down, m_gla_norm, m_gla_w_in, m_gla_w_lr, m_gla_b_lr, m_gla_head_norm, m_gla_w_out, m_pool_norm, m_pool_w, m_pool_b, m_pool_scale, m_final_norm, v_meta, v_ffn_norm, v_ffn_w_gate, v_ffn_w_up, v_ffn_w_down, v_gla_norm, v_gla_w_in, v_gla_w_lr, v_gla_b_lr, v_gla_head_norm, v_gla_w_out, v_pool_norm, v_pool_w, v_pool_b, v_pool_scale, v_final_norm):
    S, D = x.shape[1], x.shape[2]
    M = OFF + S
    Dq = D // N_CHIPS
    Fs = ffn_w_gate.shape[3]
    F = N_CHIPS * Fs
    dk = D // 2
    n_in = gla_w_in.shape[2]
    W = D // 4
    core = lax.axis_index("c").astype(jnp.int32).reshape(1)
    chip_id = 2 * lax.axis_index("x") + lax.axis_index("y")
    chip = chip_id.astype(jnp.int32).reshape(1)

    small = jnp.concatenate([_pad_rows(t) for t in (
        meta, ffn_norm.reshape(4, Dq), gla_w_lr.reshape(8, Dq), pool_norm, pool_b.reshape(1, Dq), pool_scale)],
        axis=0)
    def stage(w, kind, n, mi, dep=None):
        return stage_shard(w.reshape(n, 2, -1, w.shape[-1]), mi, chip, f"stage_{kind}_{mi}", dep=dep)

    ffn_stage = lambda mi, dep=None: [stage(ffn_w_gate, "gate", 4, mi, dep), stage(ffn_w_up, "up", 4, mi, dep),
                                      stage(ffn_w_down, "down", 4, mi, dep)]
    small_stage = lax.dynamic_update_slice(jnp.zeros((N_CHIPS,) + small.shape, F32), small[None], (chip_id, 0, 0))
    first = ffn_stage(0)
    buckets = [first[:2] + [small_stage], first[2:]]
    sizes = [len(b) for b in buckets]
    gather_sems, in_flight, tok = gather_start([t for b in buckets for t in b], sizes, "gather_start_first")
    buckets = [[stage(gla_w_in, "win", 1, 0, tok), stage(gla_w_out, "wout", 1, 0, tok)],
               ffn_stage(1, tok), ffn_stage(2, tok), [stage(pool_w, "pool", 1, 0, tok)] + ffn_stage(3, tok)]
    more_sems, more_in_flight, gather_token = gather_start([t for b in buckets for t in b],
                                                            [len(b) for b in buckets], "gather_start_rest")
    sizes += [len(b) for b in buckets]
    gather_sems += more_sems
    in_flight += more_in_flight
    starts = [sum(sizes[:b]) for b in range(len(sizes))]

    def arrive(b, after, n_big):
        bufs = gather_wait(in_flight[starts[b]:starts[b] + sizes[b]], gather_sems[b], after, f"gather_wait_{b}")
        return forward_to_sibling(bufs[:n_big], f"gather_forward_{b}") + bufs[n_big:]

    ffn_w = lambda t: (t[0].reshape(N_CHIPS, D, Fs), t[1].reshape(N_CHIPS, D, Fs), t[2].reshape(F, D))
    got = arrive(0, gather_token, 2)
    wg, wu, wd = [None] * 4, [None] * 4, [None] * 4
    wg[0], wu[0] = got[0].reshape(N_CHIPS, D, Fs), got[1].reshape(N_CHIPS, D, Fs)
    sm = got[2]
    unshard = lambda t: t.transpose(1, 0, 2).reshape(t.shape[1], D)
    meta_f = unshard(sm[:, 0:16])
    ffn_norm_f = unshard(sm[:, 16:20])
    w_lr_f = sm[:, 24:32].reshape(N_CHIPS, GATE_RANK, dk // N_CHIPS).transpose(1, 0, 2).reshape(GATE_RANK, dk)
    pool_norm_f = sm[:, 32].reshape(1, D)
    pool_b_f = sm[:, 40].reshape(N_CHIPS, 4, W // N_CHIPS).transpose(1, 0, 2).reshape(1, D)
    pool_scale_f = sm[:, 48].reshape(1, D)
    wlr_pad = jnp.pad(w_lr_f.astype(BF16), ((0, LR_W - GATE_RANK), (0, 0)))
    final_g = final_norm.reshape(1, D)
    qkv = 2 * dk + D

    x0 = jnp.concatenate([jnp.zeros((PAD, D), F32), meta_f, x[0]], axis=0)
    target = jnp.pad(loss_target[0], ((OFF, 0), (0, 0)))
    h0, rstd0 = rmsnorm_fwd(x0, ffn_norm_f[0:1], BF16, "ffn_norm_0")
    acts0 = ffn_gateup(h0, wg[0], wu[0], "ffn_gateup_0")
    wd[0] = arrive(1, acts0[3], 1)[0].reshape(F, D)
    x1 = mm_residual(acts0[3], wd[0], x0, 0.5, Fs, "ffn_down_0", tm_target=704)
    ffn0 = (x0, ffn_norm_f[0:1], h0, rstd0, *acts0)
    got = arrive(2, x1, 2)
    w_in = got[0].reshape(N_CHIPS, D, n_in).transpose(1, 0, 2).reshape(D, N_CHIPS * n_in)
    w_out = got[1].reshape(D, D)
    w_all = jnp.concatenate([w_in[:, :qkv], w_in[:, qkv + GATE_RANK:], w_in[:, qkv:qkv + GATE_RANK],
                             jnp.zeros((D, LR_W - GATE_RANK), BF16)], axis=1)
    hg, rstd_g = rmsnorm_fwd(x1, gla_norm, BF16, "gla_norm")
    proj = mm_nn(hg, w_all, F32, "gla_proj")
    o, st = gla_fwd(proj, wlr_pad, gla_b_lr, D)
    gated = gla_post_fwd(o, proj, gla_head_norm, D)
    x2 = mm_residual(gated, w_out, x1, 1.0, D, "gla_out")
    wg[1], wu[1], wd[1] = ffn_w(arrive(3, x2, 3))
    x3, ffn1 = _ffn_fwd(x2, ffn_norm_f[1:2], wg[1], wu[1], wd[1], "1")
    wg[2], wu[2], wd[2] = ffn_w(arrive(4, x3, 3))
    x4, ffn2 = _ffn_fwd(x3, ffn_norm_f[2:3], wg[2], wu[2], wd[2], "2")
    got = arrive(5, x4, 4)
    w_pool = got[0].reshape(N_CHIPS, 4, W // N_CHIPS, W).transpose(1, 0, 2, 3).reshape(4, W, W)
    wg[3], wu[3], wd[3] = ffn_w(got[1:])
    hp, rstd_p = rmsnorm_fwd(x4, pool_norm_f, F32, "pool_norm")
    pooled = pool_window(hp)
    x5 = pool_mix(pooled, x4, w_pool, pool_b_f, pool_scale_f)
    x6, ffn3 = _ffn_fwd(x5, ffn_norm_f[3:4], wg[3], wu[3], wd[3], "3")
    loss, dx6, d_final, dy6 = final_loss(x6, final_g, target)

    reducer = GradReducer(core, chip, [("gate", 4), ("up", 4), ("down", 4), ("win", 1), ("wout", 1), ("pool", 1)])

    def settle(begun, after):
        for state in begun:
            reducer.end(state, after)

    dx5, _, dn3, red3, tok = _ffn_bwd(dx6, dy6, ffn3, wg[3], wu[3], wd[3], 3, reducer)
    dpooled, d_wpool, d_pool_b, d_pool_scale = pool_mix_bwd(dx5, pooled, w_pool, pool_b_f, pool_scale_f, dep=tok)
    dhp = pool_window_bwd(dpooled)
    dx4, d_pool_norm, dy4 = rmsnorm_bwd(dhp, x4, pool_norm_f, rstd_p, dx5, "pool_norm_bwd")
    d_wpool = d_wpool.reshape(4, N_CHIPS, W // N_CHIPS, W).transpose(1, 0, 2, 3)
    sent_p, tok = reducer.send([(d_wpool, "pool", 0)], "pool")
    dx3, dy3, dn2, red2, tok = _ffn_bwd(dx4, dy4, ffn2, wg[2], wu[2], wd[2], 2, reducer, dep=tok)
    redp, tok_p = reducer.begin([sent_p], dx3, "pool")
    dx2, _, dn1, red1, tok = _ffn_bwd(dx3, dy3, ffn1, wg[1], wu[1], wd[1], 1, reducer, dep=[tok, tok_p])
    tm = _tile(M, 352)
    td = _tile(D, 512, 128)
    d_wout = mm_tn(gated, dx2, td, td, "gla_out_bwd_w", dep=tok)
    sent_o, tok = reducer.send([(d_wout, "wout", 0)], "wout")
    dgated = mm_nt([(dx2, w_out)], tm, td, D, "gla_out_bwd_act", dep=tok)
    redo, tok_o = reducer.begin([sent_o], dgated, "wout")
    do, dr, d_head_norm = gla_post_bwd(dgated, o, proj, gla_head_norm, D)
    dproj, dwlr, dblr = gla_bwd(proj, wlr_pad, gla_b_lr, st, do, dr, D)
    tp = _tile(proj.shape[1], 896, 128)
    d_wall = mm_tn(hg, dproj, td, tp, "gla_proj_bwd_w", dep=tok_o)
    d_win = jnp.concatenate([d_wall[:, :qkv], d_wall[:, qkv + D:qkv + D + GATE_RANK], d_wall[:, qkv:qkv + D]], axis=1)
    d_win = d_win.reshape(D, N_CHIPS, n_in).transpose(1, 0, 2)
    sent_i, tok = reducer.send([(d_win, "win", 0)], "win")
    dhg = mm_nt([(dproj, w_all)], tm, D, tp, "gla_proj_bwd_act", dep=tok)
    redi, tok = reducer.begin([sent_i], dhg, "win")
    dx1, d_gla_norm, dy1 = rmsnorm_bwd(dhg, x1, gla_norm, rstd_g, dx2, "gla_norm_bwd")
    dx0, _, dn0, red0, tok = _ffn_bwd(dx1, dy1, ffn0, wg[0], wu[0], wd[0], 0, reducer, dep=tok, per_tensor=True)
    settle(red3 + [redp] + red2 + red1 + [redo, redi] + red0[:-1], tok)

    d_wlr = dwlr[:, :GATE_RANK].transpose(1, 0, 2).reshape(GATE_RANK, dk)
    pieces = [dx0[PAD:OFF], dn0, dn1, dn2, dn3, d_gla_norm, d_wlr,
              dblr.reshape(1, dk), d_head_norm, d_pool_norm, d_pool_b, d_pool_scale, d_final]
    packed = jnp.concatenate([_pad_rows(p.reshape(-1, Dq)) for p in pieces], axis=0)
    total = sum_devices(gather_devices(packed))

    settle(red0[-1:], [total] + list(reducer.groups.values()))
    reduced = reducer.finish()
    g_gate = reduced["gate"].reshape(ffn_w_gate.shape)
    g_up = reduced["up"].reshape(ffn_w_up.shape)
    g_down = reduced["down"].reshape(ffn_w_down.shape)
    g_win = reduced["win"].reshape(gla_w_in.shape)
    g_wout = reduced["wout"].reshape(gla_w_out.shape)
    g_wpool = reduced["pool"].reshape(pool_w.shape)
    sums, at = [], 0
    for p in pieces:
        r = p.size // Dq
        sums.append(total[at:at + r].reshape(p.shape))
        at += r + (-r % 8)
    (s_meta, s_n0, s_n1, s_n2, s_n3, s_gla_norm, s_wlr, s_blr, s_head_norm, s_pool_norm, s_pool_b, s_pool_scale,
     s_final) = sums
    s_ffn_norm = jnp.stack([s_n0, s_n1, s_n2, s_n3], axis=0)[:, 0]
    mine = lambda t, width: lax.dynamic_slice_in_dim(t, chip_id * width, width, axis=t.ndim - 1)
    g_meta = mine(s_meta, Dq)
    g_ffn_norm = mine(s_ffn_norm, Dq).reshape(ffn_norm.shape)
    g_gla_norm = s_gla_norm
    g_wlr = mine(s_wlr, dk // N_CHIPS).reshape(gla_w_lr.shape)
    g_blr = s_blr
    g_head_norm = s_head_norm
    g_pool_norm = mine(s_pool_norm, Dq)
    g_pool_b = mine(s_pool_b.reshape(4, W), W // N_CHIPS).reshape(pool_b.shape)
    g_pool_scale = mine(s_pool_scale, Dq)
    g_final = s_final.reshape(final_norm.shape)

    weights = [meta, ffn_norm, ffn_w_gate, ffn_w_up, ffn_w_down, gla_norm, gla_w_in, gla_w_lr, gla_b_lr,
               gla_head_norm, gla_w_out, pool_norm, pool_w, pool_b, pool_scale, final_norm]
    moments_m = [m_meta, m_ffn_norm, m_ffn_w_gate, m_ffn_w_up, m_ffn_w_down, m_gla_norm, m_gla_w_in, m_gla_w_lr,
                 m_gla_b_lr, m_gla_head_norm, m_gla_w_out, m_pool_norm, m_pool_w, m_pool_b, m_pool_scale,
                 m_final_norm]
    moments_v = [v_meta, v_ffn_norm, v_ffn_w_gate, v_ffn_w_up, v_ffn_w_down, v_gla_norm, v_gla_w_in, v_gla_w_lr,
                 v_gla_b_lr, v_gla_head_norm, v_gla_w_out, v_pool_norm, v_pool_w, v_pool_b, v_pool_scale,
                 v_final_norm]
    grads_w = [g_meta, g_ffn_norm, g_gate, g_up, g_down, g_gla_norm, g_win, g_wlr, g_blr, g_head_norm, g_wout,
               g_pool_norm, g_wpool, g_pool_b, g_pool_scale, g_final]
    from_swap = {2, 3, 4, 6, 10, 12}
    deltas, new_m, new_v = [], [], []
    for i, (w, g, m, v) in enumerate(zip(weights, grads_w, moments_m, moments_v)):
        outs = adamw(w, g, m, v, f"adamw_{i}", copy_g=i in from_swap)
        deltas.append(outs[0])
        new_m.append(outs[1])
        new_v.append(outs[2])
        if i in from_swap:
            grads_w[i] = outs[3]

    loss = lax.psum(loss[0, 0], ("x", "y", "c"))
    grad_x = dx0[OFF:][None]
    return (loss, grad_x, *grads_w, *deltas, *new_m, *new_v)
```

```python
import functools

import jax
import jax.numpy as jnp
from jax import lax
from jax.experimental import pallas as pl
from jax.experimental.pallas import tpu as pltpu

F32 = jnp.float32
BF16 = jnp.bfloat16
MESH = pl.DeviceIdType.MESH
ANY = pl.BlockSpec(memory_space=pl.ANY)

N_META = 16
CHUNK = 64
PAD = CHUNK - N_META
OFF = PAD + N_META
EPS = 1e-6
HEADS = 4
GATE_RANK = 16
GATE_NORM = 16.0
LR_W = 128
N_CHIPS = 4
N_DEV = 8
ADAM_LR, ADAM_B1, ADAM_B2, ADAM_EPS, ADAM_WD, ADAM_STEP = 0.001, 0.9, 0.999, 1e-08, 0.01, 10
VMEM_LIMIT = 56 * 1024 * 1024
ROW_TILE = 176
ONE_BUFFER = pl.Buffered(1)


def _tile(n, target, mult=16):
    best = None
    for d in range(mult, min(n, target) + 1, mult):
        if n % d == 0:
            best = d
    return best if best is not None else n


def _params(*sem):
    return pltpu.CompilerParams(dimension_semantics=sem, vmem_limit_bytes=VMEM_LIMIT)


def _dot(a, b):
    return jnp.dot(a, b, preferred_element_type=F32)


def _dot_nt(a, b):
    return lax.dot_general(a, b, (((1,), (1,)), ((), ())), preferred_element_type=F32)


def _dot_tn(a, b):
    return lax.dot_general(a, b, (((0,), (0,)), ((), ())), preferred_element_type=F32)


MXU_WIDTH = 256


def _chunks(n):
    return [slice(lo, min(lo + MXU_WIDTH, n)) for lo in range(0, n, MXU_WIDTH)]


def _sigmoid(x):
    return 1.0 / (1.0 + jnp.exp(-x))


def _rows(tile, width=1):
    return lax.broadcasted_iota(jnp.int32, (tile, width), 0)


def _dep_args(dep):
    if dep is None:
        return []
    return list(dep) if isinstance(dep, (list, tuple)) else [dep]


def _dep_specs(dep):
    return [ANY] * len(_dep_args(dep))


def _pad_rows(t):
    return jnp.pad(t, ((0, -t.shape[0] % 8), (0, 0)))


def rmsnorm_fwd(x, g, out_dtype, name):
    M, D = x.shape
    tr = _tile(M, ROW_TILE)

    def body(x_ref, g_ref, h_ref, r_ref):
        xv = x_ref[...]
        r = lax.rsqrt(jnp.mean(xv * xv, axis=-1, keepdims=True) + EPS)
        h_ref[...] = (xv * r * g_ref[...]).astype(out_dtype)
        r_ref[...] = r

    return pl.pallas_call(
        body, name=name, grid=(M // tr,),
        in_specs=[pl.BlockSpec((tr, D), lambda i: (i, 0)), pl.BlockSpec((1, D), lambda i: (0, 0))],
        out_specs=[pl.BlockSpec((tr, D), lambda i: (i, 0)), pl.BlockSpec((tr, 1), lambda i: (i, 0))],
        out_shape=[jax.ShapeDtypeStruct((M, D), out_dtype), jax.ShapeDtypeStruct((M, 1), F32)],
        compiler_params=_params("parallel"),
    )(x, g)


def rmsnorm_bwd(dh, x, g, rstd, dres, name):
    M, D = x.shape
    tr = _tile(M, ROW_TILE)

    def body(dh_ref, x_ref, g_ref, r_ref, dres_ref, dx_ref, dg_ref, half_ref):
        @pl.when(pl.program_id(0) == 0)
        def _():
            dg_ref[...] = jnp.zeros_like(dg_ref)

        r = r_ref[...]
        xhat = x_ref[...] * r
        dhv = dh_ref[...]
        gd = dhv * g_ref[...]
        dx = dres_ref[...] + r * (gd - xhat * jnp.mean(gd * xhat, axis=-1, keepdims=True))
        dx_ref[...] = dx
        half_ref[...] = (0.5 * dx).astype(BF16)
        dg_ref[...] += jnp.sum(dhv * xhat, axis=0, keepdims=True)

    row = pl.BlockSpec((tr, D), lambda i: (i, 0))
    vec = pl.BlockSpec((1, D), lambda i: (0, 0))
    return pl.pallas_call(
        body, name=name, grid=(M // tr,),
        in_specs=[row, row, vec, pl.BlockSpec((tr, 1), lambda i: (i, 0)), row],
        out_specs=[row, vec, row],
        out_shape=[jax.ShapeDtypeStruct((M, D), F32), jax.ShapeDtypeStruct((1, D), F32),
                   jax.ShapeDtypeStruct((M, D), BF16)],
        compiler_params=_params("arbitrary"),
    )(dh, x, g, rstd, dres)


def final_loss(x, g, target):
    M, D = x.shape
    tr = _tile(M, ROW_TILE)

    def body(x_ref, g_ref, t_ref, loss_ref, dx_ref, dg_ref, half_ref):
        i = pl.program_id(0)

        @pl.when(i == 0)
        def _():
            loss_ref[...] = jnp.zeros_like(loss_ref)
            dg_ref[...] = jnp.zeros_like(dg_ref)

        live = (_rows(tr) + i * tr) >= OFF
        xv = x_ref[...]
        gv = g_ref[...]
        r = lax.rsqrt(jnp.mean(xv * xv, axis=-1, keepdims=True) + EPS)
        xhat = xv * r
        err = jnp.where(live, xhat * gv - t_ref[...], 0.0)
        loss_ref[...] += 0.5 * jnp.sum(jnp.mean(err * err, axis=-1, keepdims=True), axis=0, keepdims=True)
        dy = err * (1.0 / D)
        gd = dy * gv
        dx = r * (gd - xhat * jnp.mean(gd * xhat, axis=-1, keepdims=True))
        dx_ref[...] = dx
        half_ref[...] = (0.5 * dx).astype(BF16)
        dg_ref[...] += jnp.sum(dy * xhat, axis=0, keepdims=True)

    row = pl.BlockSpec((tr, D), lambda i: (i, 0))
    vec = pl.BlockSpec((1, D), lambda i: (0, 0))
    return pl.pallas_call(
        body, name="final_loss", grid=(M // tr,),
        in_specs=[row, vec, row],
        out_specs=[pl.BlockSpec((1, 1), lambda i: (0, 0)), row, vec, row],
        out_shape=[jax.ShapeDtypeStruct((1, 1), F32), jax.ShapeDtypeStruct((M, D), F32),
                   jax.ShapeDtypeStruct((1, D), F32), jax.ShapeDtypeStruct((M, D), BF16)],
        compiler_params=_params("arbitrary"),
    )(x, g, target)


def mm_nn(a, w, out_dtype, name, tm_target=704, tn_target=896):
    M, K = a.shape
    N = w.shape[1]
    tm, tn = _tile(M, tm_target), _tile(N, tn_target, 128)

    def body(a_ref, w_ref, o_ref):
        o_ref[...] = _dot(a_ref[...], w_ref[...]).astype(out_dtype)

    return pl.pallas_call(
        body, name=name, grid=(N // tn, M // tm),
        in_specs=[pl.BlockSpec((tm, K), lambda n, i: (i, 0)), pl.BlockSpec((K, tn), lambda n, i: (0, n))],
        out_specs=pl.BlockSpec((tm, tn), lambda n, i: (i, n)),
        out_shape=jax.ShapeDtypeStruct((M, N), out_dtype),
        compiler_params=_params("parallel", "parallel"),
    )(a, w)


def ffn_gateup(h, wg, wu, name):
    M, D = h.shape
    Fs = wg.shape[2]
    tm = _tile(M, 352)

    def body(h_ref, wg_ref, wu_ref, u_ref, silu_ref, dsilu_ref, a_ref):
        hv = h_ref[...]
        for cols in _chunks(Fs):
            g = _dot(hv, wg_ref[:, cols])
            u = _dot(hv, wu_ref[:, cols])
            s = _sigmoid(g)
            silu = g * s
            u_ref[:, cols] = u.astype(BF16)
            silu_ref[:, cols] = silu.astype(BF16)
            dsilu_ref[:, cols] = (s * (1.0 + g * (1.0 - s))).astype(BF16)
            a_ref[:, cols] = (silu * u).astype(BF16)

    wspec = pl.BlockSpec((None, D, Fs), lambda j, i: (j, 0, 0))
    ospec = pl.BlockSpec((tm, Fs), lambda j, i: (i, j))
    return pl.pallas_call(
        body, name=name, grid=(N_CHIPS, M // tm),
        in_specs=[pl.BlockSpec((tm, D), lambda j, i: (i, 0)), wspec, wspec],
        out_specs=[ospec] * 4,
        out_shape=[jax.ShapeDtypeStruct((M, N_CHIPS * Fs), BF16)] * 4,
        compiler_params=_params("parallel", "parallel"),
    )(h, wg, wu)


def mm_residual(a, w, x, scale, tk, name, tm_target=352):
    M, N = x.shape
    K = w.shape[0]
    tm = _tile(M, tm_target)

    def body(a_ref, w_ref, x_ref, o_ref, acc):
        k = pl.program_id(1)

        @pl.when(k == 0)
        def _():
            acc[...] = jnp.zeros_like(acc)

        acc[...] += _dot(a_ref[...], w_ref[...])

        @pl.when(k == pl.num_programs(1) - 1)
        def _():
            o_ref[...] = x_ref[...] + scale * acc[...]

    aspec = pl.BlockSpec((tm, tk), lambda i, k: (i, k))
    return pl.pallas_call(
        body, name=name, grid=(M // tm, K // tk),
        in_specs=[aspec, pl.BlockSpec((tk, N), lambda i, k: (k, 0)),
                  pl.BlockSpec((tm, N), lambda i, k: (i, 0), pipeline_mode=ONE_BUFFER)],
        out_specs=pl.BlockSpec((tm, N), lambda i, k: (i, 0), pipeline_mode=ONE_BUFFER),
        out_shape=jax.ShapeDtypeStruct((M, N), F32),
        scratch_shapes=[pltpu.VMEM((tm, N), F32)],
        compiler_params=_params("parallel", "arbitrary"),
    )(a, w, x)


def ffn_bwd_act(dy, wd, up, silu, dsilu, name, dep=None):
    M, D = dy.shape
    F = wd.shape[0]
    Fs = F // N_CHIPS
    tm = _tile(M, 704)

    def body(dy_ref, wd_ref, u_ref, silu_ref, dsilu_ref, *rest):
        dg_ref, du_ref = rest[-2:]
        dy = dy_ref[...]
        for cols in _chunks(Fs):
            da = _dot_nt(dy, wd_ref[cols, :])
            dg_ref[:, cols] = (da * u_ref[:, cols].astype(F32) * dsilu_ref[:, cols].astype(F32)).astype(BF16)
            du_ref[:, cols] = (da * silu_ref[:, cols].astype(F32)).astype(BF16)

    fspec = pl.BlockSpec((tm, Fs), lambda j, i: (i, j))
    return pl.pallas_call(
        body, name=name, grid=(N_CHIPS, M // tm),
        in_specs=[pl.BlockSpec((tm, D), lambda j, i: (i, 0)), pl.BlockSpec((Fs, D), lambda j, i: (j, 0)),
                  fspec, fspec, fspec] + _dep_specs(dep),
        out_specs=[fspec, fspec],
        out_shape=[jax.ShapeDtypeStruct((M, F), BF16)] * 2,
        compiler_params=_params("parallel", "parallel"),
    )(dy, wd, up, silu, dsilu, *_dep_args(dep))


def mm_tn(a, b, ta, tb, name, stacked_out=False, out_dtype=BF16, dep=None):
    T, Ma = a.shape
    Nb = b.shape[1]

    def body(a_ref, b_ref, *rest):
        o_ref = rest[-1]
        o_ref[...] = _dot_tn(a_ref[...], b_ref[...].astype(BF16)).astype(out_dtype)

    if stacked_out:
        out_spec = pl.BlockSpec((None, ta, tb), lambda jb, ja: (jb, ja, 0))
        out_shape = jax.ShapeDtypeStruct((Nb // tb, Ma, tb), out_dtype)
    else:
        out_spec = pl.BlockSpec((ta, tb), lambda jb, ja: (ja, jb))
        out_shape = jax.ShapeDtypeStruct((Ma, Nb), out_dtype)
    return pl.pallas_call(
        body, name=name, grid=(Nb // tb, Ma // ta),
        in_specs=[pl.BlockSpec((T, ta), lambda jb, ja: (0, ja)), pl.BlockSpec((T, tb), lambda jb, ja: (0, jb))]
        + _dep_specs(dep),
        out_specs=out_spec, out_shape=out_shape,
        compiler_params=_params("parallel", "parallel"),
    )(a, b, *_dep_args(dep))


def mm_nt(pairs, tm, tn, tk, name, stacked_w=False, dep=None):
    M, K = pairs[0][0].shape
    N = pairs[0][1].shape[1] if stacked_w else pairs[0][1].shape[0]
    n_pairs = len(pairs)

    def body(*refs):
        o_ref, acc = refs[-2:]
        k = pl.program_id(2)

        @pl.when(k == 0)
        def _():
            acc[...] = jnp.zeros_like(acc)

        for p in range(n_pairs):
            acc[...] += _dot_nt(refs[2 * p][...].astype(BF16), refs[2 * p + 1][...])

        @pl.when(k == pl.num_programs(2) - 1)
        def _():
            o_ref[...] = acc[...]

    aspec = pl.BlockSpec((tm, tk), lambda i, n, k: (i, k))
    if stacked_w:
        wspec = pl.BlockSpec((None, tn, tk), lambda i, n, k: (k, n, 0))
    else:
        wspec = pl.BlockSpec((tn, tk), lambda i, n, k: (n, k))
    return pl.pallas_call(
        body, name=name, grid=(M // tm, N // tn, K // tk),
        in_specs=[aspec, wspec] * n_pairs + _dep_specs(dep),
        out_specs=pl.BlockSpec((tm, tn), lambda i, n, k: (i, n), pipeline_mode=ONE_BUFFER),
        out_shape=jax.ShapeDtypeStruct((M, N), F32),
        scratch_shapes=[pltpu.VMEM((tm, tn), F32)],
        compiler_params=_params("parallel", "parallel", "arbitrary"),
    )(*[t for pair in pairs for t in pair], *_dep_args(dep))


def _tri(lower):
    r = lax.broadcasted_iota(jnp.int32, (CHUNK, CHUNK), 0)
    c = lax.broadcasted_iota(jnp.int32, (CHUNK, CHUNK), 1)
    return (r >= c) if lower else (r <= c)


def _tri_sum(mask, x, pieces):
    ones = mask.astype(BF16)
    acc = jnp.zeros_like(x)
    rest = x
    for _ in range(pieces):
        piece = rest.astype(BF16)
        acc = acc + _dot(ones, piece)
        rest = rest - piece.astype(F32)
    return acc


def _gla_gates(lr, wlr, blr, chunk):
    z = _dot(lr, wlr) + blr
    live = (_rows(CHUNK) + chunk * CHUNK) >= PAD
    lg = jnp.where(live, (jnp.minimum(z, 0.0) - jnp.log(1.0 + jnp.exp(-jnp.abs(z)))) * (1.0 / GATE_NORM), 0.0)
    b = _tri_sum(_tri(True), lg, 3)
    b_last = jnp.sum(lg, axis=0, keepdims=True)
    b_mid = jnp.sum(jnp.where(_rows(CHUNK) < CHUNK // 2, lg, 0.0), axis=0, keepdims=True)
    return z, live, b, b_last, b_mid


def _gla_specs(D, chunk_of):
    lr_blk = (3 * D) // LR_W
    return [
        pl.BlockSpec((CHUNK, D // 2), lambda c: (chunk_of(c), 0)),
        pl.BlockSpec((CHUNK, D // 2), lambda c: (chunk_of(c), 1)),
        pl.BlockSpec((CHUNK, D), lambda c: (chunk_of(c), 1)),
        pl.BlockSpec((CHUNK, LR_W), lambda c: (chunk_of(c), lr_blk)),
        pl.BlockSpec((LR_W, D // 2), lambda c: (0, 0)),
        pl.BlockSpec((1, D // 2), lambda c: (0, 0)),
    ]


def gla_fwd(proj, wlr, blr, D):
    M = proj.shape[0]
    n = M // CHUNK
    dkh, dvh = D // 2 // HEADS, D // HEADS
    qscale = float(dkh) ** -0.5

    def body(q_ref, k_ref, v_ref, lr_ref, wlr_ref, blr_ref, o_ref, st_ref, S):
        c = pl.program_id(0)

        @pl.when(c == 0)
        def _():
            S[...] = jnp.zeros_like(S)

        lr = lr_ref[...].astype(BF16)
        for h in range(HEADS):
            kc, vc = slice(h * dkh, (h + 1) * dkh), slice(h * dvh, (h + 1) * dvh)
            _, _, b, b_last, b_mid = _gla_gates(lr, wlr_ref[:, kc], blr_ref[:, kc], c)
            q = q_ref[:, kc] * qscale
            k = k_ref[:, kc]
            v = v_ref[:, vc].astype(BF16)
            s0 = S[h]
            st_ref[h] = s0
            qb = (q * jnp.exp(b)).astype(BF16)
            kb = (k * jnp.exp(b_last - b)).astype(BF16)
            qt = (q * jnp.exp(b - b_mid)).astype(BF16)
            kt = (k * jnp.exp(b_mid - b)).astype(BF16)
            a = jnp.where(_tri(True), _dot_nt(qt, kt), 0.0).astype(BF16)
            o_ref[:, vc] = _dot_nt(qb, s0.astype(BF16)) + _dot(a, v)
            S[h] = jnp.exp(b_last) * s0 + _dot_tn(v, kb)

    return pl.pallas_call(
        body, name="gla_fwd", grid=(n,),
        in_specs=_gla_specs(D, lambda c: c),
        out_specs=[pl.BlockSpec((CHUNK, D), lambda c: (c, 0)),
                   pl.BlockSpec((None, HEADS, dvh, dkh), lambda c: (c, 0, 0, 0))],
        out_shape=[jax.ShapeDtypeStruct((M, D), F32), jax.ShapeDtypeStruct((n, HEADS, dvh, dkh), F32)],
        scratch_shapes=[pltpu.VMEM((HEADS, dvh, dkh), F32)],
        compiler_params=_params("arbitrary"),
    )(proj, proj, proj, proj, wlr, blr)


def gla_bwd(proj, wlr, blr, st, do, dr, D):
    M = proj.shape[0]
    n = M // CHUNK
    dkh, dvh = D // 2 // HEADS, D // HEADS
    qscale = float(dkh) ** -0.5
    rev = lambda c: n - 1 - c

    def body(q_ref, k_ref, v_ref, lr_ref, wlr_ref, blr_ref, st_ref, do_ref, dr_ref,
             dp_ref, dwlr_ref, dblr_ref, dS):
        step = pl.program_id(0)
        c = n - 1 - step

        @pl.when(step == 0)
        def _():
            dS[...] = jnp.zeros_like(dS)
            dwlr_ref[...] = jnp.zeros_like(dwlr_ref)
            dblr_ref[...] = jnp.zeros_like(dblr_ref)

        lr = lr_ref[...].astype(BF16)
        lower = _tri(True)
        dlr = None
        for h in range(HEADS):
            kc, vc = slice(h * dkh, (h + 1) * dkh), slice(h * dvh, (h + 1) * dvh)
            wlr_h = wlr_ref[:, kc]
            z, live, b, b_last, b_mid = _gla_gates(lr, wlr_h, blr_ref[:, kc], c)
            q = q_ref[:, kc] * qscale
            k = k_ref[:, kc]
            v = v_ref[:, vc].astype(BF16)
            dov = do_ref[:, vc].astype(BF16)
            s0 = st_ref[h]
            ds1 = dS[h]
            ds1b = ds1.astype(BF16)
            e_b, e_lb = jnp.exp(b), jnp.exp(b_last - b)
            e_bm, e_mb = jnp.exp(b - b_mid), jnp.exp(b_mid - b)
            e_last = jnp.exp(b_last)
            qb, kb, qt, kt = q * e_b, k * e_lb, q * e_bm, k * e_mb
            qbb, kbb, qtb, ktb = qb.astype(BF16), kb.astype(BF16), qt.astype(BF16), kt.astype(BF16)
            a = jnp.where(lower, _dot_nt(qtb, ktb), 0.0).astype(BF16)
            da = jnp.where(lower, _dot_nt(dov, v), 0.0).astype(BF16)

            dqb = _dot(dov, s0.astype(BF16))
            dqt = _dot(da, ktb)
            dkt = _dot_tn(da, qtb)
            dkb = _dot(v, ds1b)
            keep = live.astype(F32)
            dp_ref[:, D + h * dvh:D + (h + 1) * dvh] = (keep * (_dot_tn(a, dov) + _dot_nt(kbb, ds1b))).astype(BF16)
            dp_ref[:, kc] = (keep * qscale * (dqb * e_b + dqt * e_bm)).astype(BF16)
            dp_ref[:, D // 2 + h * dkh:D // 2 + (h + 1) * dkh] = (keep * (dkb * e_lb + dkt * e_mb)).astype(BF16)

            db = dqb * qb - dkb * kb + dqt * qt - dkt * kt
            db_last = (jnp.sum(dkb * kb, axis=0, keepdims=True)
                       + jnp.sum(ds1 * s0, axis=0, keepdims=True) * e_last)
            db = db + jnp.where(_rows(CHUNK) == CHUNK - 1, db_last, 0.0)
            dlg = jnp.where(live, _tri_sum(_tri(False), db, 2), 0.0)
            dz = dlg * (1.0 / GATE_NORM) / (1.0 + jnp.exp(z))
            dzb = dz.astype(BF16)

            dlr_h = _dot_nt(dzb, wlr_h)
            dlr = dlr_h if dlr is None else dlr + dlr_h
            dwlr_ref[h] += _dot_tn(lr, dzb)
            dblr_ref[h] += jnp.sum(dz, axis=0, keepdims=True)
            dS[h] = e_last * ds1 + _dot_tn(dov, qbb)
        dp_ref[:, 2 * D:3 * D] = dr_ref[...]
        dp_ref[:, 3 * D:] = dlr.astype(BF16)

    row = pl.BlockSpec((CHUNK, D), lambda c: (rev(c), 0))
    return pl.pallas_call(
        body, name="gla_bwd", grid=(n,),
        in_specs=_gla_specs(D, rev) + [pl.BlockSpec((None, HEADS, dvh, dkh), lambda c: (rev(c), 0, 0, 0)), row, row],
        out_specs=[pl.BlockSpec((CHUNK, 3 * D + LR_W), lambda c: (rev(c), 0)),
                   pl.BlockSpec((HEADS, LR_W, dkh), lambda c: (0, 0, 0)),
                   pl.BlockSpec((HEADS, 1, dkh), lambda c: (0, 0, 0))],
        out_shape=[jax.ShapeDtypeStruct((M, 3 * D + LR_W), BF16),
                   jax.ShapeDtypeStruct((HEADS, LR_W, dkh), F32), jax.ShapeDtypeStruct((HEADS, 1, dkh), F32)],
        scratch_shapes=[pltpu.VMEM((HEADS, dvh, dkh), F32)],
        compiler_params=_params("arbitrary"),
    )(proj, proj, proj, proj, wlr, blr, st, do, dr)


def gla_post_fwd(o, proj, head_norm, D):
    M = o.shape[0]
    dvh = D // HEADS
    tr = _tile(M, ROW_TILE)

    def body(o_ref, r_ref, hn_ref, out_ref):
        for hd in range(HEADS):
            cols = slice(hd * dvh, (hd + 1) * dvh)
            ov = o_ref[:, cols]
            rs = lax.rsqrt(jnp.mean(ov * ov, axis=-1, keepdims=True) + EPS)
            rv = r_ref[:, cols]
            out_ref[:, cols] = (ov * rs * hn_ref[...] * (rv * _sigmoid(rv))).astype(BF16)

    row = pl.BlockSpec((tr, D), lambda i: (i, 0))
    return pl.pallas_call(
        body, name="gla_post_fwd", grid=(M // tr,),
        in_specs=[row, pl.BlockSpec((tr, D), lambda i: (i, 2)), pl.BlockSpec((1, dvh), lambda i: (0, 0))],
        out_specs=row, out_shape=jax.ShapeDtypeStruct((M, D), BF16),
        compiler_params=_params("parallel"),
    )(o, proj, head_norm)


def gla_post_bwd(dgated, o, proj, head_norm, D):
    M = o.shape[0]
    dvh = D // HEADS
    tr = _tile(M, ROW_TILE)

    def body(dg_ref, o_ref, r_ref, hn_ref, do_ref, dr_ref, dhn_ref):
        @pl.when(pl.program_id(0) == 0)
        def _():
            dhn_ref[...] = jnp.zeros_like(dhn_ref)

        hn = hn_ref[...]
        dhn = jnp.zeros((1, dvh), F32)
        for hd in range(HEADS):
            cols = slice(hd * dvh, (hd + 1) * dvh)
            ov = o_ref[:, cols]
            rs = lax.rsqrt(jnp.mean(ov * ov, axis=-1, keepdims=True) + EPS)
            ohat = ov * rs
            rv = r_ref[:, cols]
            s = _sigmoid(rv)
            dgv = dg_ref[:, cols]
            don = dgv * (rv * s)
            dr_ref[:, cols] = (dgv * ohat * hn * (s * (1.0 + rv * (1.0 - s)))).astype(BF16)
            gd = don * hn
            do_ref[:, cols] = rs * (gd - ohat * jnp.mean(gd * ohat, axis=-1, keepdims=True))
            dhn = dhn + jnp.sum(don * ohat, axis=0, keepdims=True)
        dhn_ref[...] += dhn

    row = pl.BlockSpec((tr, D), lambda i: (i, 0))
    vec = pl.BlockSpec((1, dvh), lambda i: (0, 0))
    return pl.pallas_call(
        body, name="gla_post_bwd", grid=(M // tr,),
        in_specs=[row, row, pl.BlockSpec((tr, D), lambda i: (i, 2)), vec],
        out_specs=[row, row, vec],
        out_shape=[jax.ShapeDtypeStruct((M, D), F32), jax.ShapeDtypeStruct((M, D), BF16),
                   jax.ShapeDtypeStruct((1, dvh), F32)],
        compiler_params=_params("arbitrary"),
    )(dgated, o, proj, head_norm)


def _pool_counts(M, g):
    t = _rows(M) - PAD
    win = jnp.left_shift(2, g)
    return t >= 0, jnp.maximum(jnp.minimum(t + 1, win), 1).astype(F32)


def _window_sum(x, g, M, back):
    sums = []
    s = x
    for lvl in range(4):
        sh = 1 << lvl
        s = s + pltpu.roll(s, (M - sh) if back else sh, 0)
        sums.append(s)
    return jnp.where(g == 0, sums[0], jnp.where(g == 1, sums[1], jnp.where(g == 2, sums[2], sums[3])))


POOL_COLS = 128


def pool_window(hp):
    M, D = hp.shape
    cw = min(POOL_COLS, D // 4)
    per_group = (D // 4) // cw

    def body(h_ref, p_ref):
        g = pl.program_id(0) // per_group
        live, cnt = _pool_counts(M, g)
        hv = h_ref[...]
        p_ref[...] = jnp.where(live, _window_sum(hv, g, M, False) / cnt - hv, 0.0).astype(BF16)

    col = pl.BlockSpec((M, cw), lambda j: (0, j))
    return pl.pallas_call(
        body, name="pool_window", grid=(D // cw,), in_specs=[col], out_specs=col,
        out_shape=jax.ShapeDtypeStruct((M, D), BF16), compiler_params=_params("parallel"),
    )(hp)


def pool_window_bwd(dpooled):
    M, D = dpooled.shape
    cw = min(POOL_COLS, D // 4)
    per_group = (D // 4) // cw

    def body(d_ref, o_ref):
        g = pl.program_id(0) // per_group
        live, cnt = _pool_counts(M, g)
        dv = jnp.where(live, d_ref[...], 0.0)
        o_ref[...] = jnp.where(live, _window_sum(dv / cnt, g, M, True) - dv, 0.0)

    col = pl.BlockSpec((M, cw), lambda j: (0, j))
    return pl.pallas_call(
        body, name="pool_window_bwd", grid=(D // cw,), in_specs=[col], out_specs=col,
        out_shape=jax.ShapeDtypeStruct((M, D), F32), compiler_params=_params("parallel"),
    )(dpooled)


def pool_mix(pooled, x, w, bias, scale):
    M, D = x.shape
    W = D // 4
    tm = _tile(M, 352)

    def body(p_ref, x_ref, w_ref, b_ref, s_ref, out_ref):
        live = (_rows(tm) + pl.program_id(1) * tm) >= PAD
        y = (_dot(p_ref[...], w_ref[...]) + b_ref[...]) * s_ref[...]
        out_ref[...] = x_ref[...] + jnp.where(live, y, 0.0)

    blk = pl.BlockSpec((tm, W), lambda g, i: (i, g))
    vec = pl.BlockSpec((1, W), lambda g, i: (0, g))
    return pl.pallas_call(
        body, name="pool_mix", grid=(4, M // tm),
        in_specs=[blk, blk, pl.BlockSpec((None, W, W), lambda g, i: (g, 0, 0)), vec, vec],
        out_specs=blk, out_shape=jax.ShapeDtypeStruct((M, D), F32),
        compiler_params=_params("parallel", "parallel"),
    )(pooled, x, w, bias, scale)


def pool_mix_bwd(dy, pooled, w, bias, scale, dep=None):
    M, D = dy.shape
    W = D // 4
    tm = _tile(M, 352)

    def body(dy_ref, p_ref, w_ref, b_ref, s_ref, *rest):
        dp_ref, dw_ref, db_ref, ds_ref, acc_w = rest[-5:]
        i = pl.program_id(1)

        @pl.when(i == 0)
        def _():
            acc_w[...] = jnp.zeros_like(acc_w)
            db_ref[...] = jnp.zeros_like(db_ref)
            ds_ref[...] = jnp.zeros_like(ds_ref)

        live = (_rows(tm) + i * tm) >= PAD
        dyv = jnp.where(live, dy_ref[...], 0.0)
        pooled = p_ref[...]
        wv = w_ref[...]
        ds_ref[...] += jnp.sum(dyv * (_dot(pooled, wv) + b_ref[...]), axis=0, keepdims=True)
        dys = dyv * s_ref[...]
        db_ref[...] += jnp.sum(dys, axis=0, keepdims=True)
        dysb = dys.astype(BF16)
        acc_w[...] += _dot_tn(pooled, dysb)
        dp_ref[...] = _dot_nt(dysb, wv)

        @pl.when(i == pl.num_programs(1) - 1)
        def _():
            dw_ref[...] = acc_w[...].astype(BF16)

    blk = pl.BlockSpec((tm, W), lambda g, i: (i, g))
    vec = pl.BlockSpec((1, W), lambda g, i: (0, g))
    wspec = pl.BlockSpec((None, W, W), lambda g, i: (g, 0, 0))
    return pl.pallas_call(
        body, name="pool_mix_bwd", grid=(4, M // tm),
        in_specs=[blk, blk, wspec, vec, vec] + _dep_specs(dep),
        out_specs=[blk, wspec, vec, vec],
        out_shape=[jax.ShapeDtypeStruct((M, D), F32), jax.ShapeDtypeStruct((4, W, W), BF16),
                   jax.ShapeDtypeStruct((1, D), F32), jax.ShapeDtypeStruct((1, D), F32)],
        scratch_shapes=[pltpu.VMEM((W, W), F32)],
        compiler_params=_params("parallel", "arbitrary"),
    )(dy, pooled, w, bias, scale, *_dep_args(dep))


def adamw(w, g, m, v, name, copy_g=False):
    shape = w.shape
    C = shape[-1]
    R = w.size // C
    tr = _tile(R, 256, 8)
    tc = C
    if tr == R and R > 256:
        tc = _tile(C, 256, 128)

    def body(w_ref, g_ref, m_ref, v_ref, d_ref, nm_ref, nv_ref, *g_out):
        gv = g_ref[...]
        for ref in g_out:
            ref[...] = gv
        nm = ADAM_B1 * m_ref[...] + (1.0 - ADAM_B1) * gv
        nv = ADAM_B2 * v_ref[...] + (1.0 - ADAM_B2) * (gv * gv)
        m_hat = nm / (1.0 - ADAM_B1 ** ADAM_STEP)
        v_hat = nv / (1.0 - ADAM_B2 ** ADAM_STEP)
        d_ref[...] = -ADAM_LR * (m_hat / (jnp.sqrt(v_hat) + ADAM_EPS) + ADAM_WD * w_ref[...])
        nm_ref[...] = nm
        nv_ref[...] = nv

    spec = pl.BlockSpec((tr, tc), lambda i, j: (i, j))
    outs = pl.pallas_call(
        body, name=name, grid=(R // tr, C // tc),
        in_specs=[spec] * 4, out_specs=[spec] * (3 + copy_g),
        out_shape=[jax.ShapeDtypeStruct((R, C), F32)] * (3 + copy_g),
        compiler_params=_params("parallel", "parallel"),
    )(*[t.reshape(R, C) for t in (w, g, m, v)])
    return [t.reshape(shape) for t in outs]


def add_sibling(grad, recv, core, name):
    _, _, Rh, C = grad.shape
    tr = _tile(Rh, 512)

    def body(core_ref, g_ref, r_ref, o_ref):
        o_ref[...] = (g_ref[...].astype(F32) + r_ref[...].astype(F32)).astype(BF16)

    return pl.pallas_call(
        body, name=name,
        grid_spec=pltpu.PrefetchScalarGridSpec(
            num_scalar_prefetch=1, grid=(N_CHIPS, Rh // tr),
            in_specs=[pl.BlockSpec((None, None, tr, C), lambda j, i, core_ref: (j, core_ref[0], i, 0)),
                      pl.BlockSpec((None, tr, C), lambda j, i, core_ref: (j, i, 0))],
            out_specs=pl.BlockSpec((None, tr, C), lambda j, i, core_ref: (j, i, 0))),
        out_shape=jax.ShapeDtypeStruct((N_CHIPS, Rh, C), BF16),
        compiler_params=_params("parallel", "parallel"),
    )(core, grad, recv)


def add_chips(part, recv, chip, core, group, n, mi, name):
    _, Rh, C = part.shape
    tr = _tile(Rh, 512)

    def body(chip_ref, core_ref, p_ref, r_ref, *rest):
        o_ref = rest[-1]
        acc = p_ref[...].astype(F32)
        for k in range(N_CHIPS - 1):
            acc = acc + r_ref[k].astype(F32)
        o_ref[...] = acc

    carried = [] if group is None else [group]
    return pl.pallas_call(
        body, name=name,
        grid_spec=pltpu.PrefetchScalarGridSpec(
            num_scalar_prefetch=2, grid=(Rh // tr,),
            in_specs=[pl.BlockSpec((None, tr, C), lambda i, chip_ref, core_ref: (chip_ref[0], i, 0)),
                      pl.BlockSpec((N_CHIPS - 1, tr, C), lambda i, chip_ref, core_ref: (0, i, 0))]
            + [ANY] * len(carried),
            out_specs=pl.BlockSpec((None, None, tr, C), lambda i, chip_ref, core_ref: (mi, core_ref[0], i, 0))),
        out_shape=jax.ShapeDtypeStruct((n, 2, Rh, C), F32),
        input_output_aliases={4: 0} if carried else {},
        compiler_params=_params("parallel"),
    )(chip, core, part, recv, *carried)


def stage_shard(shard, mi, chip, name, dep=None):
    _, _, Rh, C = shard.shape
    tr = _tile(Rh, 512)

    def body(chip_ref, s_ref, *rest):
        rest[-1][...] = s_ref[...].astype(BF16)

    return pl.pallas_call(
        body, name=name,
        grid_spec=pltpu.PrefetchScalarGridSpec(
            num_scalar_prefetch=1, grid=(2, Rh // tr),
            in_specs=[pl.BlockSpec((None, None, tr, C), lambda h, i, chip_ref: (mi, h, i, 0))] + _dep_specs(dep),
            out_specs=pl.BlockSpec((None, None, tr, C), lambda h, i, chip_ref: (chip_ref[0], h, i, 0))),
        out_shape=jax.ShapeDtypeStruct((N_CHIPS, 2, Rh, C), BF16),
        compiler_params=_params("parallel", "parallel"),
    )(chip, shard, *_dep_args(dep))


def sum_devices(gathered):
    _, R, C = gathered.shape

    def body(g_ref, o_ref):
        acc = g_ref[0]
        for d in range(1, N_DEV):
            acc = acc + g_ref[d]
        o_ref[...] = acc

    return pl.pallas_call(
        body, name="sum_devices", grid=(1,),
        in_specs=[pl.BlockSpec((N_DEV, R, C), lambda i: (0, 0, 0))],
        out_specs=pl.BlockSpec((R, C), lambda i: (0, 0)),
        out_shape=jax.ShapeDtypeStruct((R, C), F32),
        compiler_params=_params("arbitrary"),
    )(gathered)


def _place():
    x, y, c = lax.axis_index("x"), lax.axis_index("y"), lax.axis_index("c")
    others = [(1 - x, y), (x, 1 - y), (1 - x, 1 - y)]
    return x, y, c, others


def _remote(src, dst, send_sems, recv_sems, idx, device):
    return pltpu.make_async_remote_copy(src_ref=src, dst_ref=dst, send_sem=send_sems.at[idx],
                                        recv_sem=recv_sems.at[idx], device_id=device, device_id_type=MESH)


HBM = pl.BlockSpec(memory_space=pltpu.HBM)
SEM = pl.BlockSpec(memory_space=pltpu.SEMAPHORE)
EFFECT = pltpu.SideEffectType.DATAFLOW_SIDE_EFFECTING


def _in_hbm(t):
    return pltpu.with_memory_space_constraint(t, pltpu.HBM)


def _own_slice(buf, me, c):
    return buf.at[me, c] if len(buf.shape) == 4 else buf.at[me]


def gather_start(staged, bucket_sizes, name):
    n, nb = len(staged), len(bucket_sizes)

    def body(*refs):
        in_refs, sems, token = refs[:n], refs[n:n + 2 * nb], refs[-1]
        x, y, c, others = _place()
        me = 2 * x + y
        t = 0
        for b, size in enumerate(bucket_sizes):
            for i in range(size):
                mine = _own_slice(in_refs[t], me, c)
                for k, chip in enumerate(others):
                    _remote(mine, mine, sems[2 * b], sems[2 * b + 1], 3 * i + k, (*chip, c)).start()
                t += 1
        token[...] = jnp.zeros_like(token)

    sem_shapes = [pltpu.SemaphoreType.DMA((3 * size,)) for size in bucket_sizes for _ in range(2)]
    outs = pl.pallas_call(
        body, name=name,
        out_shape=sem_shapes + [pltpu.HBM(s.shape, s.dtype) for s in staged] + [jax.ShapeDtypeStruct((8, 128), F32)],
        in_specs=[HBM] * n, out_specs=[SEM] * (2 * nb) + [HBM] * n + [pl.BlockSpec(memory_space=pltpu.VMEM)],
        input_output_aliases={t: 2 * nb + t for t in range(n)},
        compiler_params=pltpu.CompilerParams(has_side_effects=EFFECT),
    )(*[_in_hbm(s) for s in staged])
    sems = [(outs[2 * b], outs[2 * b + 1]) for b in range(nb)]
    return sems, list(outs[2 * nb:2 * nb + n]), outs[-1]


def gather_wait(bufs, sems, after, name):
    n = len(bufs)

    def body(*refs):
        in_refs, send_sems, recv_sems = refs[:n], refs[n], refs[n + 1]
        x, y, c, others = _place()
        me = 2 * x + y
        for i in range(n):
            mine = _own_slice(in_refs[i], me, c)
            for k, (ox, oy) in enumerate(others):
                cp = _remote(mine, _own_slice(in_refs[i], 2 * ox + oy, c), send_sems, recv_sems, 3 * i + k,
                             (ox, oy, c))
                cp.wait_send()
                cp.wait_recv()

    return pl.pallas_call(
        body, name=name, out_shape=[pltpu.HBM(b.shape, b.dtype) for b in bufs],
        in_specs=[HBM] * n + [SEM, SEM, ANY], out_specs=[HBM] * n,
        input_output_aliases={t: t for t in range(n)},
        compiler_params=pltpu.CompilerParams(has_side_effects=EFFECT),
    )(*bufs, *sems, after)


def forward_to_sibling(bufs, name):
    n = len(bufs)

    def body(*refs):
        out_refs, (send_sems, recv_sems) = refs[n:2 * n], refs[2 * n:]
        x, y, c, others = _place()
        sibling = (x, y, 1 - c)
        copies = []
        for t in range(n):
            for k, (ox, oy) in enumerate(others):
                mine = out_refs[t].at[2 * ox + oy, c]
                cp = _remote(mine, mine, send_sems, recv_sems, 3 * t + k, sibling)
                cp.start()
                copies.append(cp)
        for t in range(n):
            for k, (ox, oy) in enumerate(others):
                theirs = out_refs[t].at[2 * ox + oy, 1 - c]
                _remote(theirs, theirs, send_sems, recv_sems, 3 * t + k, sibling).wait_recv()
        for cp in copies:
            cp.wait_send()

    return pl.pallas_call(
        body, name=name, in_specs=[ANY] * n, out_specs=[ANY] * n,
        out_shape=[jax.ShapeDtypeStruct(b.shape, b.dtype) for b in bufs],
        input_output_aliases={t: t for t in range(n)},
        scratch_shapes=[pltpu.SemaphoreType.DMA((3 * n,)), pltpu.SemaphoreType.DMA((3 * n,))],
    )(*bufs)


def sibling_start(grads, name):
    n = len(grads)
    lands = [lax.empty((N_CHIPS,) + g.shape[2:], g.dtype) for g in grads]

    def body(*refs):
        in_refs, land_refs, send_sems, recv_sems, token = refs[:n], refs[n:2 * n], refs[2 * n], refs[2 * n + 1], refs[-1]
        x, y, c, _ = _place()
        for t in range(n):
            for j in range(N_CHIPS):
                _remote(in_refs[t].at[j, 1 - c], land_refs[t].at[j], send_sems, recv_sems, N_CHIPS * t + j,
                        (x, y, 1 - c)).start()
        token[...] = jnp.zeros_like(token)

    outs = pl.pallas_call(
        body, name=name,
        out_shape=[pltpu.SemaphoreType.DMA((N_CHIPS * n,))] * 2 + [pltpu.HBM(t.shape, t.dtype) for t in grads + lands]
        + [jax.ShapeDtypeStruct((8, 128), F32)],
        in_specs=[HBM] * (2 * n), out_specs=[SEM, SEM] + [HBM] * (2 * n) + [pl.BlockSpec(memory_space=pltpu.VMEM)],
        input_output_aliases={t: 2 + t for t in range(2 * n)},
        compiler_params=pltpu.CompilerParams(has_side_effects=EFFECT),
    )(*[_in_hbm(t) for t in grads + lands])
    return (outs[0], outs[1]), list(outs[2:2 + n]), list(outs[2 + n:2 + 2 * n]), outs[-1]


def sibling_wait(grads, lands, sems, after, name):
    n = len(grads)

    def body(*refs):
        in_refs, land_refs, send_sems, recv_sems = refs[:n], refs[n:2 * n], refs[2 * n], refs[2 * n + 1]
        x, y, c, _ = _place()
        for t in range(n):
            for j in range(N_CHIPS):
                cp = _remote(in_refs[t].at[j, 1 - c], land_refs[t].at[j], send_sems, recv_sems, N_CHIPS * t + j,
                             (x, y, 1 - c))
                cp.wait_send()
                cp.wait_recv()

    outs = pl.pallas_call(
        body, name=name, out_shape=[pltpu.HBM(t.shape, t.dtype) for t in grads + lands],
        in_specs=[HBM] * (2 * n) + [SEM, SEM, ANY], out_specs=[HBM] * (2 * n),
        input_output_aliases={t: t for t in range(2 * n)},
        compiler_params=pltpu.CompilerParams(has_side_effects=EFFECT),
    )(*grads, *lands, *sems, after)
    return list(outs[:n]), list(outs[n:])


def reduce_start(parts, name):
    n = len(parts)
    lands = [lax.empty((N_CHIPS - 1,) + p.shape[1:], p.dtype) for p in parts]

    def body(*refs):
        in_refs, land_refs, send_sems, recv_sems, token = refs[:n], refs[n:2 * n], refs[2 * n], refs[2 * n + 1], refs[-1]
        x, y, c, others = _place()
        for t in range(n):
            for k, (ox, oy) in enumerate(others):
                _remote(in_refs[t].at[2 * ox + oy], land_refs[t].at[k], send_sems, recv_sems, 3 * t + k,
                        (ox, oy, c)).start()
        token[...] = jnp.zeros_like(token)

    outs = pl.pallas_call(
        body, name=name,
        out_shape=[pltpu.SemaphoreType.DMA((3 * n,))] * 2 + [pltpu.HBM(t.shape, t.dtype) for t in parts + lands]
        + [jax.ShapeDtypeStruct((8, 128), F32)],
        in_specs=[HBM] * (2 * n), out_specs=[SEM, SEM] + [HBM] * (2 * n) + [pl.BlockSpec(memory_space=pltpu.VMEM)],
        input_output_aliases={t: 2 + t for t in range(2 * n)},
        compiler_params=pltpu.CompilerParams(has_side_effects=EFFECT),
    )(*[_in_hbm(t) for t in parts + lands])
    return (outs[0], outs[1]), list(outs[2:2 + n]), list(outs[2 + n:2 + 2 * n]), outs[-1]


def reduce_wait(parts, lands, sems, after, name):
    n = len(parts)

    def body(*refs):
        in_refs, land_refs, send_sems, recv_sems = refs[:n], refs[n:2 * n], refs[2 * n], refs[2 * n + 1]
        x, y, c, others = _place()
        for t in range(n):
            for k, (ox, oy) in enumerate(others):
                cp = _remote(in_refs[t].at[2 * ox + oy], land_refs[t].at[k], send_sems, recv_sems, 3 * t + k,
                             (ox, oy, c))
                cp.wait_send()
                cp.wait_recv()

    outs = pl.pallas_call(
        body, name=name, out_shape=[pltpu.HBM(t.shape, t.dtype) for t in parts + lands],
        in_specs=[HBM] * (2 * n) + [SEM, SEM] + _dep_specs(after), out_specs=[HBM] * (2 * n),
        input_output_aliases={t: t for t in range(2 * n)},
        compiler_params=pltpu.CompilerParams(has_side_effects=EFFECT),
    )(*parts, *lands, *sems, *_dep_args(after))
    return list(outs[:n]), list(outs[n:])


def exchange_halves(groups, name):
    n_groups = len(groups)
    slots = [(gi, mi) for gi, grp in enumerate(groups) for mi in range(grp.shape[0])]

    def body(*refs):
        out_refs = refs[n_groups:2 * n_groups]
        send_sems, recv_sems = refs[2 * n_groups:]
        x, y, c, _ = _place()
        sibling = (x, y, 1 - c)
        copies = []
        for t, (gi, mi) in enumerate(slots):
            mine = out_refs[gi].at[mi, c]
            cp = _remote(mine, mine, send_sems, recv_sems, t, sibling)
            cp.start()
            copies.append(cp)
        for t, (gi, mi) in enumerate(slots):
            theirs = out_refs[gi].at[mi, 1 - c]
            _remote(theirs, theirs, send_sems, recv_sems, t, sibling).wait_recv()
        for cp in copies:
            cp.wait_send()

    return pl.pallas_call(
        body, name=name, in_specs=[ANY] * n_groups, out_specs=[ANY] * n_groups,
        out_shape=[jax.ShapeDtypeStruct(g.shape, g.dtype) for g in groups],
        input_output_aliases={gi: gi for gi in range(n_groups)},
        scratch_shapes=[pltpu.SemaphoreType.DMA((len(slots),)), pltpu.SemaphoreType.DMA((len(slots),))],
    )(*groups)


def gather_devices(buf):
    def body(in_ref, out_ref, send_sems, recv_sems, local_sem):
        x, y, c, _ = _place()
        me = 4 * x + 2 * y + c
        local = pltpu.make_async_copy(in_ref, out_ref.at[me], local_sem)
        local.start()
        copies = []
        for k in range(1, N_DEV):
            fx, fy, fc = (k >> 2) & 1, (k >> 1) & 1, k & 1
            peer = (x ^ fx, y ^ fy, c ^ fc)
            cp = _remote(in_ref, out_ref.at[me], send_sems, recv_sems, k - 1, peer)
            cp.start()
            copies.append(cp)
        for k in range(1, N_DEV):
            fx, fy, fc = (k >> 2) & 1, (k >> 1) & 1, k & 1
            theirs = out_ref.at[4 * (x ^ fx) + 2 * (y ^ fy) + (c ^ fc)]
            _remote(theirs, theirs, send_sems, recv_sems, k - 1, (x, y, c)).wait_recv()
        for cp in copies:
            cp.wait_send()
        local.wait()

    return pl.pallas_call(
        body, name="gather_devices", in_specs=[ANY], out_specs=ANY,
        out_shape=jax.ShapeDtypeStruct((N_DEV,) + buf.shape, buf.dtype),
        scratch_shapes=[pltpu.SemaphoreType.DMA((N_DEV - 1,)), pltpu.SemaphoreType.DMA((N_DEV - 1,)),
                        pltpu.SemaphoreType.DMA],
    )(buf)


class GradReducer:
    def __init__(self, core, chip, kinds):
        self.core, self.chip = core, chip
        self.sizes = dict(kinds)
        self.groups = {kind: None for kind, _ in kinds}

    def send(self, grads, tag):
        arrays = [g.reshape(N_CHIPS, 2, -1, g.shape[-1]) for g, _, _ in grads]
        sems, arrays, lands, token = sibling_start(arrays, f"reduce_sibling_start_{tag}")
        return (sems, arrays, lands, [(kind, mi) for _, kind, mi in grads], tag), token

    def begin(self, sent, after, tag):
        parts, slots = [], []
        for sems, arrays, lands, sent_slots, sent_tag in sent:
            arrays, lands = sibling_wait(arrays, lands, sems, after, f"reduce_sibling_wait_{sent_tag}")
            parts += [add_sibling(g, r, self.core, f"reduce_add_sibling_{sent_tag}_{t}")
                      for t, (g, r) in enumerate(zip(arrays, lands))]
            slots += sent_slots
        sems, parts, lands, token = reduce_start(parts, f"reduce_start_{tag}")
        return (sems, parts, lands, slots, tag), token

    def end(self, state, after):
        sems, parts, lands, slots, tag = state
        parts, lands = reduce_wait(parts, lands, sems, after, f"reduce_wait_{tag}")
        for t, (kind, mi) in enumerate(slots):
            self.groups[kind] = add_chips(parts[t], lands[t], self.chip, self.core, self.groups[kind],
                                          self.sizes[kind], mi, f"reduce_add_chips_{tag}_{t}")

    def finish(self):
        kinds = list(self.groups)
        return dict(zip(kinds, exchange_halves([self.groups[k] for k in kinds], "reduce_swap")))


def _ffn_fwd(x, gain, wg, wu, wd, tag):
    h, rstd = rmsnorm_fwd(x, gain, BF16, f"ffn_norm_{tag}")
    up, silu, dsilu, act = ffn_gateup(h, wg, wu, f"ffn_gateup_{tag}")
    out = mm_residual(act, wd, x, 0.5, wd.shape[0] // N_CHIPS, f"ffn_down_{tag}", tm_target=704)
    return out, (x, gain, h, rstd, up, silu, dsilu, act)


def _ffn_bwd(dout, dy, saved, wg, wu, wd, index, reducer, dep=None, per_tensor=False):
    x, gain, h, rstd, up, silu, dsilu, act = saved
    D = x.shape[1]
    Fs = wg.shape[2]
    td = _tile(D, 1024, 128)
    tag = f"ffn{index}"
    begun = []

    def begin(sent, after, suffix):
        state, token = reducer.begin(sent, after, tag + suffix)
        begun.append(state)
        return token

    dgate, dup = ffn_bwd_act(dy, wd, up, silu, dsilu, f"ffn_bwd_act_{index}", dep=dep)
    d_wd = mm_tn(act, dy, Fs, td, f"ffn_bwd_wd_{index}")
    if per_tensor:
        sent_d, tok = reducer.send([(d_wd, "down", index)], tag + "d")
        d_wg = mm_tn(h, dgate, td, Fs, f"ffn_bwd_wg_{index}", stacked_out=True, dep=tok)
        toks = [begin([sent_d], d_wg, "d")]
        sent_g, tok = reducer.send([(d_wg, "gate", index)], tag + "g")
        d_wu = mm_tn(h, dup, td, Fs, f"ffn_bwd_wu_{index}", stacked_out=True, dep=toks + [tok])
        toks = [begin([sent_g], d_wu, "g")]
        sent, tok = reducer.send([(d_wu, "up", index)], tag + "u")
    else:
        d_wg = mm_tn(h, dgate, td, Fs, f"ffn_bwd_wg_{index}", stacked_out=True)
        d_wu = mm_tn(h, dup, td, Fs, f"ffn_bwd_wu_{index}", stacked_out=True)
        toks = []
        sent, tok = reducer.send([(d_wd, "down", index), (d_wg, "gate", index), (d_wu, "up", index)], tag + "u")
    dh = mm_nt([(dgate, wg), (dup, wu)], _tile(x.shape[0], 704), D, Fs, f"ffn_bwd_dh_{index}", stacked_w=True,
               dep=toks + [tok])
    tok = begin([sent], dh, "u")
    dx, dgain, dx_half = rmsnorm_bwd(dh, x, gain, rstd, dout, f"ffn_norm_bwd_{index}")
    return dx, dx_half, dgain, begun, tok


def kernel(x, meta, ffn_norm, ffn_w_gate, ffn_w_up, ffn_w_down, gla_norm, gla_w_in, gla_w_lr, gla_b_lr, gla_head_norm, gla_w_out, pool_norm, pool_w, pool_b, pool_scale, final_norm, loss_target, m_meta, m_ffn_norm, m_ffn_w_gate, m_ffn_w_up, m_ffn_w_down, m_gla_norm, m_gla_w_in, m_gla_w_lr, m_gla_b_lr, m_gla_head_norm, m_gla_w_out, m_pool_norm, m_pool_w, m_pool_b, m_pool_scale, m_final_norm, v_meta, v_ffn_norm, v_ffn_w_gate, v_ffn_w_up, v_ffn_w_down, v_gla_norm, v_gla_w_in, v_gla_w_lr, v_gla_b_lr, v_gla_head_norm, v_gla_w_out, v_pool_norm, v_pool_w, v_pool_b, v_pool_scale, v_final_norm):
    S, D = x.shape[1], x.shape[2]
    M = OFF + S
    Dq = D // N_CHIPS
    Fs = ffn_w_gate.shape[3]
    F = N_CHIPS * Fs
    dk = D // 2
    n_in = gla_w_in.shape[2]
    W = D // 4
    core = lax.axis_index("c").astype(jnp.int32).reshape(1)
    chip_id = 2 * lax.axis_index("x") + lax.axis_index("y")
    chip = chip_id.astype(jnp.int32).reshape(1)

    small = jnp.concatenate([_pad_rows(t) for t in (
        meta, ffn_norm.reshape(4, Dq), gla_w_lr.reshape(8, Dq), pool_norm, pool_b.reshape(1, Dq), pool_scale)],
        axis=0)
    def stage(w, kind, n, mi, dep=None):
        return stage_shard(w.reshape(n, 2, -1, w.shape[-1]), mi, chip, f"stage_{kind}_{mi}", dep=dep)

    ffn_stage = lambda mi, dep=None: [stage(ffn_w_gate, "gate", 4, mi, dep), stage(ffn_w_up, "up", 4, mi, dep),
                                      stage(ffn_w_down, "down", 4, mi, dep)]
    small_stage = lax.dynamic_update_slice(jnp.zeros((N_CHIPS,) + small.shape, F32), small[None], (chip_id, 0, 0))
    first = ffn_stage(0)
    buckets = [first[:2] + [small_stage], first[2:]]
    sizes = [len(b) for b in buckets]
    gather_sems, in_flight, tok = gather_start([t for b in buckets for t in b], sizes, "gather_start_first")
    buckets = [[stage(gla_w_in, "win", 1, 0, tok), stage(gla_w_out, "wout", 1, 0, tok)],
               ffn_stage(1, tok), ffn_stage(2, tok), [stage(pool_w, "pool", 1, 0, tok)] + ffn_stage(3, tok)]
    more_sems, more_in_flight, gather_token = gather_start([t for b in buckets for t in b],
                                                            [len(b) for b in buckets], "gather_start_rest")
    sizes += [len(b) for b in buckets]
    gather_sems += more_sems
    in_flight += more_in_flight
    starts = [sum(sizes[:b]) for b in range(len(sizes))]

    def arrive(b, after, n_big):
        bufs = gather_wait(in_flight[starts[b]:starts[b] + sizes[b]], gather_sems[b], after, f"gather_wait_{b}")
        return forward_to_sibling(bufs[:n_big], f"gather_forward_{b}") + bufs[n_big:]

    ffn_w = lambda t: (t[0].reshape(N_CHIPS, D, Fs), t[1].reshape(N_CHIPS, D, Fs), t[2].reshape(F, D))
    got = arrive(0, gather_token, 2)
    wg, wu, wd = [None] * 4, [None] * 4, [None] * 4
    wg[0], wu[0] = got[0].reshape(N_CHIPS, D, Fs), got[1].reshape(N_CHIPS, D, Fs)
    sm = got[2]
    unshard = lambda t: t.transpose(1, 0, 2).reshape(t.shape[1], D)
    meta_f = unshard(sm[:, 0:16])
    ffn_norm_f = unshard(sm[:, 16:20])
    w_lr_f = sm[:, 24:32].reshape(N_CHIPS, GATE_RANK, dk // N_CHIPS).transpose(1, 0, 2).reshape(GATE_RANK, dk)
    pool_norm_f = sm[:, 32].reshape(1, D)
    pool_b_f = sm[:, 40].reshape(N_CHIPS, 4, W // N_CHIPS).transpose(1, 0, 2).reshape(1, D)
    pool_scale_f = sm[:, 48].reshape(1, D)
    wlr_pad = jnp.pad(w_lr_f.astype(BF16), ((0, LR_W - GATE_RANK), (0, 0)))
    final_g = final_norm.reshape(1, D)
    qkv = 2 * dk + D

    x0 = jnp.concatenate([jnp.zeros((PAD, D), F32), meta_f, x[0]], axis=0)
    target = jnp.pad(loss_target[0], ((OFF, 0), (0, 0)))
    h0, rstd0 = rmsnorm_fwd(x0, ffn_norm_f[0:1], BF16, "ffn_norm_0")
    acts0 = ffn_gateup(h0, wg[0], wu[0], "ffn_gateup_0")
    wd[0] = arrive(1, acts0[3], 1)[0].reshape(F, D)
    x1 = mm_residual(acts0[3], wd[0], x0, 0.5, Fs, "ffn_down_0", tm_target=704)
    ffn0 = (x0, ffn_norm_f[0:1], h0, rstd0, *acts0)
    got = arrive(2, x1, 2)
    w_in = got[0].reshape(N_CHIPS, D, n_in).transpose(1, 0, 2).reshape(D, N_CHIPS * n_in)
    w_out = got[1].reshape(D, D)
    w_all = jnp.concatenate([w_in[:, :qkv], w_in[:, qkv + GATE_RANK:], w_in[:, qkv:qkv + GATE_RANK],
                             jnp.zeros((D, LR_W - GATE_RANK), BF16)], axis=1)
    hg, rstd_g = rmsnorm_fwd(x1, gla_norm, BF16, "gla_norm")
    proj = mm_nn(hg, w_all, F32, "gla_proj")
    o, st = gla_fwd(proj, wlr_pad, gla_b_lr, D)
    gated = gla_post_fwd(o, proj, gla_head_norm, D)
    x2 = mm_residual(gated, w_out, x1, 1.0, D, "gla_out")
    wg[1], wu[1], wd[1] = ffn_w(arrive(3, x2, 3))
    x3, ffn1 = _ffn_fwd(x2, ffn_norm_f[1:2], wg[1], wu[1], wd[1], "1")
    wg[2], wu[2], wd[2] = ffn_w(arrive(4, x3, 3))
    x4, ffn2 = _ffn_fwd(x3, ffn_norm_f[2:3], wg[2], wu[2], wd[2], "2")
    got = arrive(5, x4, 4)
    w_pool = got[0].reshape(N_CHIPS, 4, W // N_CHIPS, W).transpose(1, 0, 2, 3).reshape(4, W, W)
    wg[3], wu[3], wd[3] = ffn_w(got[1:])
    hp, rstd_p = rmsnorm_fwd(x4, pool_norm_f, F32, "pool_norm")
    pooled = pool_window(hp)
    x5 = pool_mix(pooled, x4, w_pool, pool_b_f, pool_scale_f)
    x6, ffn3 = _ffn_fwd(x5, ffn_norm_f[3:4], wg[3], wu[3], wd[3], "3")
    loss, dx6, d_final, dy6 = final_loss(x6, final_g, target)

    reducer = GradReducer(core, chip, [("gate", 4), ("up", 4), ("down", 4), ("win", 1), ("wout", 1), ("pool", 1)])

    def settle(begun, after):
        for state in begun:
            reducer.end(state, after)

    dx5, _, dn3, red3, tok = _ffn_bwd(dx6, dy6, ffn3, wg[3], wu[3], wd[3], 3, reducer)
    dpooled, d_wpool, d_pool_b, d_pool_scale = pool_mix_bwd(dx5, pooled, w_pool, pool_b_f, pool_scale_f, dep=tok)
    dhp = pool_window_bwd(dpooled)
    dx4, d_pool_norm, dy4 = rmsnorm_bwd(dhp, x4, pool_norm_f, rstd_p, dx5, "pool_norm_bwd")
    d_wpool = d_wpool.reshape(4, N_CHIPS, W // N_CHIPS, W).transpose(1, 0, 2, 3)
    sent_p, tok = reducer.send([(d_wpool, "pool", 0)], "pool")
    dx3, dy3, dn2, red2, tok = _ffn_bwd(dx4, dy4, ffn2, wg[2], wu[2], wd[2], 2, reducer, dep=tok)
    redp, tok_p = reducer.begin([sent_p], dx3, "pool")
    dx2, _, dn1, red1, tok = _ffn_bwd(dx3, dy3, ffn1, wg[1], wu[1], wd[1], 1, reducer, dep=[tok, tok_p])
    tm = _tile(M, 352)
    td = _tile(D, 512, 128)
    d_wout = mm_tn(gated, dx2, td, td, "gla_out_bwd_w", dep=tok)
    sent_o, tok = reducer.send([(d_wout, "wout", 0)], "wout")
    dgated = mm_nt([(dx2, w_out)], tm, td, D, "gla_out_bwd_act", dep=tok)
    redo, tok_o = reducer.begin([sent_o], dgated, "wout")
    do, dr, d_head_norm = gla_post_bwd(dgated, o, proj, gla_head_norm, D)
    dproj, dwlr, dblr = gla_bwd(proj, wlr_pad, gla_b_lr, st, do, dr, D)
    tp = _tile(proj.shape[1], 896, 128)
    d_wall = mm_tn(hg, dproj, td, tp, "gla_proj_bwd_w", dep=tok_o)
    d_win = jnp.concatenate([d_wall[:, :qkv], d_wall[:, qkv + D:qkv + D + GATE_RANK], d_wall[:, qkv:qkv + D]], axis=1)
    d_win = d_win.reshape(D, N_CHIPS, n_in).transpose(1, 0, 2)
    sent_i, tok = reducer.send([(d_win, "win", 0)], "win")
    dhg = mm_nt([(dproj, w_all)], tm, D, tp, "gla_proj_bwd_act", dep=tok)
    redi, tok = reducer.begin([sent_i], dhg, "win")
    dx1, d_gla_norm, dy1 = rmsnorm_bwd(dhg, x1, gla_norm, rstd_g, dx2, "gla_norm_bwd")
    dx0, _, dn0, red0, tok = _ffn_bwd(dx1, dy1, ffn0, wg[0], wu[0], wd[0], 0, reducer, dep=tok, per_tensor=True)
    settle(red3 + [redp] + red2 + red1 + [redo, redi] + red0[:-1], tok)

    d_wlr = dwlr[:, :GATE_RANK].transpose(1, 0, 2).reshape(GATE_RANK, dk)
    pieces = [dx0[PAD:OFF], dn0, dn1, dn2, dn3, d_gla_norm, d_wlr,
              dblr.reshape(1, dk), d_head_norm, d_pool_norm, d_pool_b, d_pool_scale, d_final]
    packed = jnp.concatenate([_pad_rows(p.reshape(-1, Dq)) for p in pieces], axis=0)
    total = sum_devices(gather_devices(packed))

    settle(red0[-1:], [total] + list(reducer.groups.values()))
    reduced = reducer.finish()
    g_gate = reduced["gate"].reshape(ffn_w_gate.shape)
    g_up = reduced["up"].reshape(ffn_w_up.shape)
    g_down = reduced["down"].reshape(ffn_w_down.shape)
    g_win = reduced["win"].reshape(gla_w_in.shape)
    g_wout = reduced["wout"].reshape(gla_w_out.shape)
    g_wpool = reduced["pool"].reshape(pool_w.shape)
    sums, at = [], 0
    for p in pieces:
        r = p.size // Dq
        sums.append(total[at:at + r].reshape(p.shape))
        at += r + (-r % 8)
    (s_meta, s_n0, s_n1, s_n2, s_n3, s_gla_norm, s_wlr, s_blr, s_head_norm, s_pool_norm, s_pool_b, s_pool_scale,
     s_final) = sums
    s_ffn_norm = jnp.stack([s_n0, s_n1, s_n2, s_n3], axis=0)[:, 0]
    mine = lambda t, width: lax.dynamic_slice_in_dim(t, chip_id * width, width, axis=t.ndim - 1)
    g_meta = mine(s_meta, Dq)
    g_ffn_norm = mine(s_ffn_norm, Dq).reshape(ffn_norm.shape)
    g_gla_norm = s_gla_norm
    g_wlr = mine(s_wlr, dk // N_CHIPS).reshape(gla_w_lr.shape)
    g_blr = s_blr
    g_head_norm = s_head_norm
    g_pool_norm = mine(s_pool_norm, Dq)
    g_pool_b = mine(s_pool_b.reshape(4, W), W // N_CHIPS).reshape(pool_b.shape)
    g_pool_scale = mine(s_pool_scale, Dq)
    g_final = s_final.reshape(final_norm.shape)

    weights = [meta, ffn_norm, ffn_w_gate, ffn_w_up, ffn_w_down, gla_norm, gla_w_in, gla_w_lr, gla_b_lr,
               gla_head_norm, gla_w_out, pool_norm, pool_w, pool_b, pool_scale, final_norm]
    moments_m = [m_meta, m_ffn_norm, m_ffn_w_gate, m_ffn_w_up, m_ffn_w_down, m_gla_norm, m_gla_w_in, m_gla_w_lr,
                 m_gla_b_lr, m_gla_head_norm, m_gla_w_out, m_pool_norm, m_pool_w, m_pool_b, m_pool_scale,
                 m_final_norm]
    moments_v = [v_meta, v_ffn_norm, v_ffn_w_gate, v_ffn_w_up, v_ffn_w_down, v_gla_norm, v_gla_w_in, v_gla_w_lr,
                 v_gla_b_lr, v_gla_head_norm, v_gla_w_out, v_pool_norm, v_pool_w, v_pool_b, v_pool_scale,
                 v_final_norm]
    grads_w = [g_meta, g_ffn_norm, g_gate, g_up, g_down, g_gla_norm, g_win, g_wlr, g_blr, g_head_norm, g_wout,
               g_pool_norm, g_wpool, g_pool_b, g_pool_scale, g_final]
    from_swap = {2, 3, 4, 6, 10, 12}
    deltas, new_m, new_v = [], [], []
    for i, (w, g, m, v) in enumerate(zip(weights, grads_w, moments_m, moments_v)):
        outs = adamw(w, g, m, v, f"adamw_{i}", copy_g=i in from_swap)
        deltas.append(outs[0])
        new_m.append(outs[1])
        new_v.append(outs[2])
        if i in from_swap:
            grads_w[i] = outs[3]

    loss = lax.psum(loss[0, 0], ("x", "y", "c"))
    grad_x = dx0[OFF:][None]
    return (loss, grad_x, *grads_w, *deltas, *new_m, *new_v)
```

```python
import functools

import jax
import jax.numpy as jnp
from jax import lax
from jax.experimental import pallas as pl
from jax.experimental.pallas import tpu as pltpu

F32 = jnp.float32
BF16 = jnp.bfloat16
MESH = pl.DeviceIdType.MESH
ANY = pl.BlockSpec(memory_space=pl.ANY)

N_META = 16
CHUNK = 64
PAD = CHUNK - N_META
OFF = PAD + N_META
EPS = 1e-6
HEADS = 4
GATE_RANK = 16
GATE_NORM = 16.0
LR_W = 128
N_CHIPS = 4
N_DEV = 8
ADAM_LR, ADAM_B1, ADAM_B2, ADAM_EPS, ADAM_WD, ADAM_STEP = 0.001, 0.9, 0.999, 1e-08, 0.01, 10
VMEM_LIMIT = 56 * 1024 * 1024
ROW_TILE = 352
ONE_BUFFER = pl.Buffered(1)


def _tile(n, target, mult=16):
    best = None
    for d in range(mult, min(n, target) + 1, mult):
        if n % d == 0:
            best = d
    return best if best is not None else n


def _params(*sem):
    return pltpu.CompilerParams(dimension_semantics=sem, vmem_limit_bytes=VMEM_LIMIT)


def _dot(a, b):
    return jnp.dot(a, b, preferred_element_type=F32)


def _dot_nt(a, b):
    return lax.dot_general(a, b, (((1,), (1,)), ((), ())), preferred_element_type=F32)


def _dot_tn(a, b):
    return lax.dot_general(a, b, (((0,), (0,)), ((), ())), preferred_element_type=F32)


MXU_WIDTH = 256


def _chunks(n):
    return [slice(lo, min(lo + MXU_WIDTH, n)) for lo in range(0, n, MXU_WIDTH)]


def _sigmoid(x):
    return 1.0 / (1.0 + jnp.exp(-x))


def _rows(tile, width=1):
    return lax.broadcasted_iota(jnp.int32, (tile, width), 0)


def _dep_args(dep):
    if dep is None:
        return []
    return list(dep) if isinstance(dep, (list, tuple)) else [dep]


def _dep_specs(dep):
    return [ANY] * len(_dep_args(dep))


def _pad_rows(t):
    return jnp.pad(t, ((0, -t.shape[0] % 8), (0, 0)))


def rmsnorm_fwd(x, g, out_dtype, name):
    M, D = x.shape
    tr = _tile(M, ROW_TILE)

    def body(x_ref, g_ref, h_ref, r_ref):
        xv = x_ref[...]
        r = lax.rsqrt(jnp.mean(xv * xv, axis=-1, keepdims=True) + EPS)
        h_ref[...] = (xv * r * g_ref[...]).astype(out_dtype)
        r_ref[...] = r

    return pl.pallas_call(
        body, name=name, grid=(M // tr,),
        in_specs=[pl.BlockSpec((tr, D), lambda i: (i, 0)), pl.BlockSpec((1, D), lambda i: (0, 0))],
        out_specs=[pl.BlockSpec((tr, D), lambda i: (i, 0)), pl.BlockSpec((tr, 1), lambda i: (i, 0))],
        out_shape=[jax.ShapeDtypeStruct((M, D), out_dtype), jax.ShapeDtypeStruct((M, 1), F32)],
        compiler_params=_params("parallel"),
    )(x, g)


def rmsnorm_bwd(dh, x, g, rstd, dres, name):
    M, D = x.shape
    tr = _tile(M, ROW_TILE)

    def body(dh_ref, x_ref, g_ref, r_ref, dres_ref, dx_ref, dg_ref, half_ref):
        @pl.when(pl.program_id(0) == 0)
        def _():
            dg_ref[...] = jnp.zeros_like(dg_ref)

        r = r_ref[...]
        xhat = x_ref[...] * r
        dhv = dh_ref[...]
        gd = dhv * g_ref[...]
        dx = dres_ref[...] + r * (gd - xhat * jnp.mean(gd * xhat, axis=-1, keepdims=True))
        dx_ref[...] = dx
        half_ref[...] = (0.5 * dx).astype(BF16)
        dg_ref[...] += jnp.sum(dhv * xhat, axis=0, keepdims=True)

    row = pl.BlockSpec((tr, D), lambda i: (i, 0))
    vec = pl.BlockSpec((1, D), lambda i: (0, 0))
    return pl.pallas_call(
        body, name=name, grid=(M // tr,),
        in_specs=[row, row, vec, pl.BlockSpec((tr, 1), lambda i: (i, 0)), row],
        out_specs=[row, vec, row],
        out_shape=[jax.ShapeDtypeStruct((M, D), F32), jax.ShapeDtypeStruct((1, D), F32),
                   jax.ShapeDtypeStruct((M, D), BF16)],
        compiler_params=_params("arbitrary"),
    )(dh, x, g, rstd, dres)


def final_loss(x, g, target):
    M, D = x.shape
    tr = _tile(M, ROW_TILE)

    def body(x_ref, g_ref, t_ref, loss_ref, dx_ref, dg_ref, half_ref):
        i = pl.program_id(0)

        @pl.when(i == 0)
        def _():
            loss_ref[...] = jnp.zeros_like(loss_ref)
            dg_ref[...] = jnp.zeros_like(dg_ref)

        live = (_rows(tr) + i * tr) >= OFF
        xv = x_ref[...]
        gv = g_ref[...]
        r = lax.rsqrt(jnp.mean(xv * xv, axis=-1, keepdims=True) + EPS)
        xhat = xv * r
        err = jnp.where(live, xhat * gv - t_ref[...], 0.0)
        loss_ref[...] += 0.5 * jnp.sum(jnp.mean(err * err, axis=-1, keepdims=True), axis=0, keepdims=True)
        dy = err * (1.0 / D)
        gd = dy * gv
        dx = r * (gd - xhat * jnp.mean(gd * xhat, axis=-1, keepdims=True))
        dx_ref[...] = dx
        half_ref[...] = (0.5 * dx).astype(BF16)
        dg_ref[...] += jnp.sum(dy * xhat, axis=0, keepdims=True)

    row = pl.BlockSpec((tr, D), lambda i: (i, 0))
    vec = pl.BlockSpec((1, D), lambda i: (0, 0))
    return pl.pallas_call(
        body, name="final_loss", grid=(M // tr,),
        in_specs=[row, vec, row],
        out_specs=[pl.BlockSpec((1, 1), lambda i: (0, 0)), row, vec, row],
        out_shape=[jax.ShapeDtypeStruct((1, 1), F32), jax.ShapeDtypeStruct((M, D), F32),
                   jax.ShapeDtypeStruct((1, D), F32), jax.ShapeDtypeStruct((M, D), BF16)],
        compiler_params=_params("arbitrary"),
    )(x, g, target)


def mm_nn(a, w, out_dtype, name, tm_target=704, tn_target=896):
    M, K = a.shape
    N = w.shape[1]
    tm, tn = _tile(M, tm_target), _tile(N, tn_target, 128)

    def body(a_ref, w_ref, o_ref):
        o_ref[...] = _dot(a_ref[...], w_ref[...]).astype(out_dtype)

    return pl.pallas_call(
        body, name=name, grid=(N // tn, M // tm),
        in_specs=[pl.BlockSpec((tm, K), lambda n, i: (i, 0)), pl.BlockSpec((K, tn), lambda n, i: (0, n))],
        out_specs=pl.BlockSpec((tm, tn), lambda n, i: (i, n)),
        out_shape=jax.ShapeDtypeStruct((M, N), out_dtype),
        compiler_params=_params("parallel", "parallel"),
    )(a, w)


def ffn_gateup(h, wg, wu, name):
    M, D = h.shape
    Fs = wg.shape[2]
    tm = _tile(M, 352)

    def body(h_ref, wg_ref, wu_ref, u_ref, silu_ref, dsilu_ref, a_ref):
        hv = h_ref[...]
        for cols in _chunks(Fs):
            g = _dot(hv, wg_ref[:, cols])
            u = _dot(hv, wu_ref[:, cols])
            s = _sigmoid(g)
            silu = g * s
            u_ref[:, cols] = u.astype(BF16)
            silu_ref[:, cols] = silu.astype(BF16)
            dsilu_ref[:, cols] = (s * (1.0 + g * (1.0 - s))).astype(BF16)
            a_ref[:, cols] = (silu * u).astype(BF16)

    wspec = pl.BlockSpec((None, D, Fs), lambda j, i: (j, 0, 0))
    ospec = pl.BlockSpec((tm, Fs), lambda j, i: (i, j))
    return pl.pallas_call(
        body, name=name, grid=(N_CHIPS, M // tm),
        in_specs=[pl.BlockSpec((tm, D), lambda j, i: (i, 0)), wspec, wspec],
        out_specs=[ospec] * 4,
        out_shape=[jax.ShapeDtypeStruct((M, N_CHIPS * Fs), BF16)] * 4,
        compiler_params=_params("parallel", "parallel"),
    )(h, wg, wu)


def mm_residual(a, w, x, scale, tk, name, tm_target=352):
    M, N = x.shape
    K = w.shape[0]
    tm = _tile(M, tm_target)

    def body(a_ref, w_ref, x_ref, o_ref, acc):
        k = pl.program_id(1)

        @pl.when(k == 0)
        def _():
            acc[...] = jnp.zeros_like(acc)

        acc[...] += _dot(a_ref[...], w_ref[...])

        @pl.when(k == pl.num_programs(1) - 1)
        def _():
            o_ref[...] = x_ref[...] + scale * acc[...]

    aspec = pl.BlockSpec((tm, tk), lambda i, k: (i, k))
    return pl.pallas_call(
        body, name=name, grid=(M // tm, K // tk),
        in_specs=[aspec, pl.BlockSpec((tk, N), lambda i, k: (k, 0)),
                  pl.BlockSpec((tm, N), lambda i, k: (i, 0), pipeline_mode=ONE_BUFFER)],
        out_specs=pl.BlockSpec((tm, N), lambda i, k: (i, 0), pipeline_mode=ONE_BUFFER),
        out_shape=jax.ShapeDtypeStruct((M, N), F32),
        scratch_shapes=[pltpu.VMEM((tm, N), F32)],
        compiler_params=_params("parallel", "arbitrary"),
    )(a, w, x)


def ffn_bwd_act(dy, wd, up, silu, dsilu, name, dep=None):
    M, D = dy.shape
    F = wd.shape[0]
    Fs = F // N_CHIPS
    tm = _tile(M, 704)

    def body(dy_ref, wd_ref, u_ref, silu_ref, dsilu_ref, *rest):
        dg_ref, du_ref = rest[-2:]
        dy = dy_ref[...]
        for cols in _chunks(Fs):
            da = _dot_nt(dy, wd_ref[cols, :])
            dg_ref[:, cols] = (da * u_ref[:, cols].astype(F32) * dsilu_ref[:, cols].astype(F32)).astype(BF16)
            du_ref[:, cols] = (da * silu_ref[:, cols].astype(F32)).astype(BF16)

    fspec = pl.BlockSpec((tm, Fs), lambda j, i: (i, j))
    return pl.pallas_call(
        body, name=name, grid=(N_CHIPS, M // tm),
        in_specs=[pl.BlockSpec((tm, D), lambda j, i: (i, 0)), pl.BlockSpec((Fs, D), lambda j, i: (j, 0)),
                  fspec, fspec, fspec] + _dep_specs(dep),
        out_specs=[fspec, fspec],
        out_shape=[jax.ShapeDtypeStruct((M, F), BF16)] * 2,
        compiler_params=_params("parallel", "parallel"),
    )(dy, wd, up, silu, dsilu, *_dep_args(dep))


def mm_tn(a, b, ta, tb, name, stacked_out=False, out_dtype=BF16, dep=None):
    T, Ma = a.shape
    Nb = b.shape[1]

    def body(a_ref, b_ref, *rest):
        o_ref = rest[-1]
        o_ref[...] = _dot_tn(a_ref[...], b_ref[...].astype(BF16)).astype(out_dtype)

    if stacked_out:
        out_spec = pl.BlockSpec((None, ta, tb), lambda jb, ja: (jb, ja, 0))
        out_shape = jax.ShapeDtypeStruct((Nb // tb, Ma, tb), out_dtype)
    else:
        out_spec = pl.BlockSpec((ta, tb), lambda jb, ja: (ja, jb))
        out_shape = jax.ShapeDtypeStruct((Ma, Nb), out_dtype)
    return pl.pallas_call(
        body, name=name, grid=(Nb // tb, Ma // ta),
        in_specs=[pl.BlockSpec((T, ta), lambda jb, ja: (0, ja)), pl.BlockSpec((T, tb), lambda jb, ja: (0, jb))]
        + _dep_specs(dep),
        out_specs=out_spec, out_shape=out_shape,
        compiler_params=_params("parallel", "parallel"),
    )(a, b, *_dep_args(dep))


def mm_nt(pairs, tm, tn, tk, name, stacked_w=False, dep=None):
    M, K = pairs[0][0].shape
    N = pairs[0][1].shape[1] if stacked_w else pairs[0][1].shape[0]
    n_pairs = len(pairs)

    def body(*refs):
        o_ref, acc = refs[-2:]
        k = pl.program_id(2)

        @pl.when(k == 0)
        def _():
            acc[...] = jnp.zeros_like(acc)

        for p in range(n_pairs):
            acc[...] += _dot_nt(refs[2 * p][...].astype(BF16), refs[2 * p + 1][...])

        @pl.when(k == pl.num_programs(2) - 1)
        def _():
            o_ref[...] = acc[...]

    aspec = pl.BlockSpec((tm, tk), lambda i, n, k: (i, k))
    if stacked_w:
        wspec = pl.BlockSpec((None, tn, tk), lambda i, n, k: (k, n, 0))
    else:
        wspec = pl.BlockSpec((tn, tk), lambda i, n, k: (n, k))
    return pl.pallas_call(
        body, name=name, grid=(M // tm, N // tn, K // tk),
        in_specs=[aspec, wspec] * n_pairs + _dep_specs(dep),
        out_specs=pl.BlockSpec((tm, tn), lambda i, n, k: (i, n), pipeline_mode=ONE_BUFFER),
        out_shape=jax.ShapeDtypeStruct((M, N), F32),
        scratch_shapes=[pltpu.VMEM((tm, tn), F32)],
        compiler_params=_params("parallel", "parallel", "arbitrary"),
    )(*[t for pair in pairs for t in pair], *_dep_args(dep))


def _tri(lower):
    r = lax.broadcasted_iota(jnp.int32, (CHUNK, CHUNK), 0)
    c = lax.broadcasted_iota(jnp.int32, (CHUNK, CHUNK), 1)
    return (r >= c) if lower else (r <= c)


def _tri_sum(mask, x, pieces):
    ones = mask.astype(BF16)
    acc = jnp.zeros_like(x)
    rest = x
    for _ in range(pieces):
        piece = rest.astype(BF16)
        acc = acc + _dot(ones, piece)
        rest = rest - piece.astype(F32)
    return acc


def _gla_gates(lr, wlr, blr, chunk):
    z = _dot(lr, wlr) + blr
    live = (_rows(CHUNK) + chunk * CHUNK) >= PAD
    lg = jnp.where(live, (jnp.minimum(z, 0.0) - jnp.log(1.0 + jnp.exp(-jnp.abs(z)))) * (1.0 / GATE_NORM), 0.0)
    b = _tri_sum(_tri(True), lg, 3)
    b_last = jnp.sum(lg, axis=0, keepdims=True)
    b_mid = jnp.sum(jnp.where(_rows(CHUNK) < CHUNK // 2, lg, 0.0), axis=0, keepdims=True)
    return z, live, b, b_last, b_mid


def _gla_specs(D, chunk_of):
    lr_blk = (3 * D) // LR_W
    return [
        pl.BlockSpec((CHUNK, D // 2), lambda c: (chunk_of(c), 0)),
        pl.BlockSpec((CHUNK, D // 2), lambda c: (chunk_of(c), 1)),
        pl.BlockSpec((CHUNK, D), lambda c: (chunk_of(c), 1)),
        pl.BlockSpec((CHUNK, LR_W), lambda c: (chunk_of(c), lr_blk)),
        pl.BlockSpec((LR_W, D // 2), lambda c: (0, 0)),
        pl.BlockSpec((1, D // 2), lambda c: (0, 0)),
    ]


def gla_fwd(proj, wlr, blr, D):
    M = proj.shape[0]
    n = M // CHUNK
    dkh, dvh = D // 2 // HEADS, D // HEADS
    qscale = float(dkh) ** -0.5

    def body(q_ref, k_ref, v_ref, lr_ref, wlr_ref, blr_ref, o_ref, st_ref, S):
        c = pl.program_id(0)

        @pl.when(c == 0)
        def _():
            S[...] = jnp.zeros_like(S)

        lr = lr_ref[...].astype(BF16)
        for h in range(HEADS):
            kc, vc = slice(h * dkh, (h + 1) * dkh), slice(h * dvh, (h + 1) * dvh)
            _, _, b, b_last, b_mid = _gla_gates(lr, wlr_ref[:, kc], blr_ref[:, kc], c)
            q = q_ref[:, kc] * qscale
            k = k_ref[:, kc]
            v = v_ref[:, vc].astype(BF16)
            s0 = S[h]
            st_ref[h] = s0
            qb = (q * jnp.exp(b)).astype(BF16)
            kb = (k * jnp.exp(b_last - b)).astype(BF16)
            qt = (q * jnp.exp(b - b_mid)).astype(BF16)
            kt = (k * jnp.exp(b_mid - b)).astype(BF16)
            a = jnp.where(_tri(True), _dot_nt(qt, kt), 0.0).astype(BF16)
            o_ref[:, vc] = _dot_nt(qb, s0.astype(BF16)) + _dot(a, v)
            S[h] = jnp.exp(b_last) * s0 + _dot_tn(v, kb)

    return pl.pallas_call(
        body, name="gla_fwd", grid=(n,),
        in_specs=_gla_specs(D, lambda c: c),
        out_specs=[pl.BlockSpec((CHUNK, D), lambda c: (c, 0)),
                   pl.BlockSpec((None, HEADS, dvh, dkh), lambda c: (c, 0, 0, 0))],
        out_shape=[jax.ShapeDtypeStruct((M, D), F32), jax.ShapeDtypeStruct((n, HEADS, dvh, dkh), F32)],
        scratch_shapes=[pltpu.VMEM((HEADS, dvh, dkh), F32)],
        compiler_params=_params("arbitrary"),
    )(proj, proj, proj, proj, wlr, blr)


def gla_bwd(proj, wlr, blr, st, do, dr, D):
    M = proj.shape[0]
    n = M // CHUNK
    dkh, dvh = D // 2 // HEADS, D // HEADS
    qscale = float(dkh) ** -0.5
    rev = lambda c: n - 1 - c

    def body(q_ref, k_ref, v_ref, lr_ref, wlr_ref, blr_ref, st_ref, do_ref, dr_ref,
             dp_ref, dwlr_ref, dblr_ref, dS):
        step = pl.program_id(0)
        c = n - 1 - step

        @pl.when(step == 0)
        def _():
            dS[...] = jnp.zeros_like(dS)
            dwlr_ref[...] = jnp.zeros_like(dwlr_ref)
            dblr_ref[...] = jnp.zeros_like(dblr_ref)

        lr = lr_ref[...].astype(BF16)
        lower = _tri(True)
        dlr = None
        for h in range(HEADS):
            kc, vc = slice(h * dkh, (h + 1) * dkh), slice(h * dvh, (h + 1) * dvh)
            wlr_h = wlr_ref[:, kc]
            z, live, b, b_last, b_mid = _gla_gates(lr, wlr_h, blr_ref[:, kc], c)
            q = q_ref[:, kc] * qscale
            k = k_ref[:, kc]
            v = v_ref[:, vc].astype(BF16)
            dov = do_ref[:, vc].astype(BF16)
            s0 = st_ref[h]
            ds1 = dS[h]
            ds1b = ds1.astype(BF16)
            e_b, e_lb = jnp.exp(b), jnp.exp(b_last - b)
            e_bm, e_mb = jnp.exp(b - b_mid), jnp.exp(b_mid - b)
            e_last = jnp.exp(b_last)
            qb, kb, qt, kt = q * e_b, k * e_lb, q * e_bm, k * e_mb
            qbb, kbb, qtb, ktb = qb.astype(BF16), kb.astype(BF16), qt.astype(BF16), kt.astype(BF16)
            a = jnp.where(lower, _dot_nt(qtb, ktb), 0.0).astype(BF16)
            da = jnp.where(lower, _dot_nt(dov, v), 0.0).astype(BF16)

            dqb = _dot(dov, s0.astype(BF16))
            dqt = _dot(da, ktb)
            dkt = _dot_tn(da, qtb)
            dkb = _dot(v, ds1b)
            keep = live.astype(F32)
            dp_ref[:, D + h * dvh:D + (h + 1) * dvh] = (keep * (_dot_tn(a, dov) + _dot_nt(kbb, ds1b))).astype(BF16)
            dp_ref[:, kc] = (keep * qscale * (dqb * e_b + dqt * e_bm)).astype(BF16)
            dp_ref[:, D // 2 + h * dkh:D // 2 + (h + 1) * dkh] = (keep * (dkb * e_lb + dkt * e_mb)).astype(BF16)

            db = dqb * qb - dkb * kb + dqt * qt - dkt * kt
            db_last = (jnp.sum(dkb * kb, axis=0, keepdims=True)
                       + jnp.sum(ds1 * s0, axis=0, keepdims=True) * e_last)
            db = db + jnp.where(_rows(CHUNK) == CHUNK - 1, db_last, 0.0)
            dlg = jnp.where(live, _tri_sum(_tri(False), db, 2), 0.0)
            dz = dlg * (1.0 / GATE_NORM) / (1.0 + jnp.exp(z))
            dzb = dz.astype(BF16)

            dlr_h = _dot_nt(dzb, wlr_h)
            dlr = dlr_h if dlr is None else dlr + dlr_h
            dwlr_ref[h] += _dot_tn(lr, dzb)
            dblr_ref[h] += jnp.sum(dz, axis=0, keepdims=True)
            dS[h] = e_last * ds1 + _dot_tn(dov, qbb)
        dp_ref[:, 2 * D:3 * D] = dr_ref[...]
        dp_ref[:, 3 * D:] = dlr.astype(BF16)

    row = pl.BlockSpec((CHUNK, D), lambda c: (rev(c), 0))
    return pl.pallas_call(
        body, name="gla_bwd", grid=(n,),
        in_specs=_gla_specs(D, rev) + [pl.BlockSpec((None, HEADS, dvh, dkh), lambda c: (rev(c), 0, 0, 0)), row, row],
        out_specs=[pl.BlockSpec((CHUNK, 3 * D + LR_W), lambda c: (rev(c), 0)),
                   pl.BlockSpec((HEADS, LR_W, dkh), lambda c: (0, 0, 0)),
                   pl.BlockSpec((HEADS, 1, dkh), lambda c: (0, 0, 0))],
        out_shape=[jax.ShapeDtypeStruct((M, 3 * D + LR_W), BF16),
                   jax.ShapeDtypeStruct((HEADS, LR_W, dkh), F32), jax.ShapeDtypeStruct((HEADS, 1, dkh), F32)],
        scratch_shapes=[pltpu.VMEM((HEADS, dvh, dkh), F32)],
        compiler_params=_params("arbitrary"),
    )(proj, proj, proj, proj, wlr, blr, st, do, dr)


def gla_post_fwd(o, proj, head_norm, D):
    M = o.shape[0]
    dvh = D // HEADS
    tr = _tile(M, ROW_TILE)

    def body(o_ref, r_ref, hn_ref, out_ref):
        for hd in range(HEADS):
            cols = slice(hd * dvh, (hd + 1) * dvh)
            ov = o_ref[:, cols]
            rs = lax.rsqrt(jnp.mean(ov * ov, axis=-1, keepdims=True) + EPS)
            rv = r_ref[:, cols]
            out_ref[:, cols] = (ov * rs * hn_ref[...] * (rv * _sigmoid(rv))).astype(BF16)

    row = pl.BlockSpec((tr, D), lambda i: (i, 0))
    return pl.pallas_call(
        body, name="gla_post_fwd", grid=(M // tr,),
        in_specs=[row, pl.BlockSpec((tr, D), lambda i: (i, 2)), pl.BlockSpec((1, dvh), lambda i: (0, 0))],
        out_specs=row, out_shape=jax.ShapeDtypeStruct((M, D), BF16),
        compiler_params=_params("parallel"),
    )(o, proj, head_norm)


def gla_post_bwd(dgated, o, proj, head_norm, D):
    M = o.shape[0]
    dvh = D // HEADS
    tr = _tile(M, ROW_TILE)

    def body(dg_ref, o_ref, r_ref, hn_ref, do_ref, dr_ref, dhn_ref):
        @pl.when(pl.program_id(0) == 0)
        def _():
            dhn_ref[...] = jnp.zeros_like(dhn_ref)

        hn = hn_ref[...]
        dhn = jnp.zeros((1, dvh), F32)
        for hd in range(HEADS):
            cols = slice(hd * dvh, (hd + 1) * dvh)
            ov = o_ref[:, cols]
            rs = lax.rsqrt(jnp.mean(ov * ov, axis=-1, keepdims=True) + EPS)
            ohat = ov * rs
            rv = r_ref[:, cols]
            s = _sigmoid(rv)
            dgv = dg_ref[:, cols]
            don = dgv * (rv * s)
            dr_ref[:, cols] = (dgv * ohat * hn * (s * (1.0 + rv * (1.0 - s)))).astype(BF16)
            gd = don * hn
            do_ref[:, cols] = rs * (gd - ohat * jnp.mean(gd * ohat, axis=-1, keepdims=True))
            dhn = dhn + jnp.sum(don * ohat, axis=0, keepdims=True)
        dhn_ref[...] += dhn

    row = pl.BlockSpec((tr, D), lambda i: (i, 0))
    vec = pl.BlockSpec((1, dvh), lambda i: (0, 0))
    return pl.pallas_call(
        body, name="gla_post_bwd", grid=(M // tr,),
        in_specs=[row, row, pl.BlockSpec((tr, D), lambda i: (i, 2)), vec],
        out_specs=[row, row, vec],
        out_shape=[jax.ShapeDtypeStruct((M, D), F32), jax.ShapeDtypeStruct((M, D), BF16),
                   jax.ShapeDtypeStruct((1, dvh), F32)],
        compiler_params=_params("arbitrary"),
    )(dgated, o, proj, head_norm)


def _pool_counts(M, g):
    t = _rows(M) - PAD
    win = jnp.left_shift(2, g)
    return t >= 0, jnp.maximum(jnp.minimum(t + 1, win), 1).astype(F32)


def _window_sum(x, g, M, back):
    sums = []
    s = x
    for lvl in range(4):
        sh = 1 << lvl
        s = s + pltpu.roll(s, (M - sh) if back else sh, 0)
        sums.append(s)
    return jnp.where(g == 0, sums[0], jnp.where(g == 1, sums[1], jnp.where(g == 2, sums[2], sums[3])))


POOL_COLS = 128


def pool_window(hp):
    M, D = hp.shape
    cw = min(POOL_COLS, D // 4)
    per_group = (D // 4) // cw

    def body(h_ref, p_ref):
        g = pl.program_id(0) // per_group
        live, cnt = _pool_counts(M, g)
        hv = h_ref[...]
        p_ref[...] = jnp.where(live, _window_sum(hv, g, M, False) / cnt - hv, 0.0).astype(BF16)

    col = pl.BlockSpec((M, cw), lambda j: (0, j))
    return pl.pallas_call(
        body, name="pool_window", grid=(D // cw,), in_specs=[col], out_specs=col,
        out_shape=jax.ShapeDtypeStruct((M, D), BF16), compiler_params=_params("parallel"),
    )(hp)


def pool_window_bwd(dpooled):
    M, D = dpooled.shape
    cw = min(POOL_COLS, D // 4)
    per_group = (D // 4) // cw

    def body(d_ref, o_ref):
        g = pl.program_id(0) // per_group
        live, cnt = _pool_counts(M, g)
        dv = jnp.where(live, d_ref[...], 0.0)
        o_ref[...] = jnp.where(live, _window_sum(dv / cnt, g, M, True) - dv, 0.0)

    col = pl.BlockSpec((M, cw), lambda j: (0, j))
    return pl.pallas_call(
        body, name="pool_window_bwd", grid=(D // cw,), in_specs=[col], out_specs=col,
        out_shape=jax.ShapeDtypeStruct((M, D), F32), compiler_params=_params("parallel"),
    )(dpooled)


def pool_mix(pooled, x, w, bias, scale):
    M, D = x.shape
    W = D // 4
    tm = _tile(M, 352)

    def body(p_ref, x_ref, w_ref, b_ref, s_ref, out_ref):
        live = (_rows(tm) + pl.program_id(1) * tm) >= PAD
        y = (_dot(p_ref[...], w_ref[...]) + b_ref[...]) * s_ref[...]
        out_ref[...] = x_ref[...] + jnp.where(live, y, 0.0)

    blk = pl.BlockSpec((tm, W), lambda g, i: (i, g))
    vec = pl.BlockSpec((1, W), lambda g, i: (0, g))
    return pl.pallas_call(
        body, name="pool_mix", grid=(4, M // tm),
        in_specs=[blk, blk, pl.BlockSpec((None, W, W), lambda g, i: (g, 0, 0)), vec, vec],
        out_specs=blk, out_shape=jax.ShapeDtypeStruct((M, D), F32),
        compiler_params=_params("parallel", "parallel"),
    )(pooled, x, w, bias, scale)


def pool_mix_bwd(dy, pooled, w, bias, scale, dep=None):
    M, D = dy.shape
    W = D // 4
    tm = _tile(M, 352)

    def body(dy_ref, p_ref, w_ref, b_ref, s_ref, *rest):
        dp_ref, dw_ref, db_ref, ds_ref, acc_w = rest[-5:]
        i = pl.program_id(1)

        @pl.when(i == 0)
        def _():
            acc_w[...] = jnp.zeros_like(acc_w)
            db_ref[...] = jnp.zeros_like(db_ref)
            ds_ref[...] = jnp.zeros_like(ds_ref)

        live = (_rows(tm) + i * tm) >= PAD
        dyv = jnp.where(live, dy_ref[...], 0.0)
        pooled = p_ref[...]
        wv = w_ref[...]
        ds_ref[...] += jnp.sum(dyv * (_dot(pooled, wv) + b_ref[...]), axis=0, keepdims=True)
        dys = dyv * s_ref[...]
        db_ref[...] += jnp.sum(dys, axis=0, keepdims=True)
        dysb = dys.astype(BF16)
        acc_w[...] += _dot_tn(pooled, dysb)
        dp_ref[...] = _dot_nt(dysb, wv)

        @pl.when(i == pl.num_programs(1) - 1)
        def _():
            dw_ref[...] = acc_w[...].astype(BF16)

    blk = pl.BlockSpec((tm, W), lambda g, i: (i, g))
    vec = pl.BlockSpec((1, W), lambda g, i: (0, g))
    wspec = pl.BlockSpec((None, W, W), lambda g, i: (g, 0, 0))
    return pl.pallas_call(
        body, name="pool_mix_bwd", grid=(4, M // tm),
        in_specs=[blk, blk, wspec, vec, vec] + _dep_specs(dep),
        out_specs=[blk, wspec, vec, vec],
        out_shape=[jax.ShapeDtypeStruct((M, D), F32), jax.ShapeDtypeStruct((4, W, W), BF16),
                   jax.ShapeDtypeStruct((1, D), F32), jax.ShapeDtypeStruct((1, D), F32)],
        scratch_shapes=[pltpu.VMEM((W, W), F32)],
        compiler_params=_params("parallel", "arbitrary"),
    )(dy, pooled, w, bias, scale, *_dep_args(dep))


def adamw(w, g, m, v, name, copy_g=False):
    shape = w.shape
    C = shape[-1]
    R = w.size // C
    tr = _tile(R, 256, 8)
    tc = C
    if tr == R and R > 256:
        tc = _tile(C, 256, 128)

    def body(w_ref, g_ref, m_ref, v_ref, d_ref, nm_ref, nv_ref, *g_out):
        gv = g_ref[...]
        for ref in g_out:
            ref[...] = gv
        nm = ADAM_B1 * m_ref[...] + (1.0 - ADAM_B1) * gv
        nv = ADAM_B2 * v_ref[...] + (1.0 - ADAM_B2) * (gv * gv)
        m_hat = nm / (1.0 - ADAM_B1 ** ADAM_STEP)
        v_hat = nv / (1.0 - ADAM_B2 ** ADAM_STEP)
        d_ref[...] = -ADAM_LR * (m_hat / (jnp.sqrt(v_hat) + ADAM_EPS) + ADAM_WD * w_ref[...])
        nm_ref[...] = nm
        nv_ref[...] = nv

    spec = pl.BlockSpec((tr, tc), lambda i, j: (i, j))
    outs = pl.pallas_call(
        body, name=name, grid=(R // tr, C // tc),
        in_specs=[spec] * 4, out_specs=[spec] * (3 + copy_g),
        out_shape=[jax.ShapeDtypeStruct((R, C), F32)] * (3 + copy_g),
        compiler_params=_params("parallel", "parallel"),
    )(*[t.reshape(R, C) for t in (w, g, m, v)])
    return [t.reshape(shape) for t in outs]


def add_sibling(grad, recv, core, name):
    _, _, Rh, C = grad.shape
    tr = _tile(Rh, 512)

    def body(core_ref, g_ref, r_ref, o_ref):
        o_ref[...] = (g_ref[...].astype(F32) + r_ref[...].astype(F32)).astype(BF16)

    return pl.pallas_call(
        body, name=name,
        grid_spec=pltpu.PrefetchScalarGridSpec(
            num_scalar_prefetch=1, grid=(N_CHIPS, Rh // tr),
            in_specs=[pl.BlockSpec((None, None, tr, C), lambda j, i, core_ref: (j, core_ref[0], i, 0)),
                      pl.BlockSpec((None, tr, C), lambda j, i, core_ref: (j, i, 0))],
            out_specs=pl.BlockSpec((None, tr, C), lambda j, i, core_ref: (j, i, 0))),
        out_shape=jax.ShapeDtypeStruct((N_CHIPS, Rh, C), BF16),
        compiler_params=_params("parallel", "parallel"),
    )(core, grad, recv)


def add_chips(part, recv, chip, core, group, n, mi, name):
    _, Rh, C = part.shape
    tr = _tile(Rh, 512)

    def body(chip_ref, core_ref, p_ref, r_ref, *rest):
        o_ref = rest[-1]
        acc = p_ref[...].astype(F32)
        for k in range(N_CHIPS - 1):
            acc = acc + r_ref[k].astype(F32)
        o_ref[...] = acc

    carried = [] if group is None else [group]
    return pl.pallas_call(
        body, name=name,
        grid_spec=pltpu.PrefetchScalarGridSpec(
            num_scalar_prefetch=2, grid=(Rh // tr,),
            in_specs=[pl.BlockSpec((None, tr, C), lambda i, chip_ref, core_ref: (chip_ref[0], i, 0)),
                      pl.BlockSpec((N_CHIPS - 1, tr, C), lambda i, chip_ref, core_ref: (0, i, 0))]
            + [ANY] * len(carried),
            out_specs=pl.BlockSpec((None, None, tr, C), lambda i, chip_ref, core_ref: (mi, core_ref[0], i, 0))),
        out_shape=jax.ShapeDtypeStruct((n, 2, Rh, C), F32),
        input_output_aliases={4: 0} if carried else {},
        compiler_params=_params("parallel"),
    )(chip, core, part, recv, *carried)


def stage_shard(shard, mi, chip, name, dep=None):
    _, _, Rh, C = shard.shape
    tr = _tile(Rh, 512)

    def body(chip_ref, s_ref, *rest):
        rest[-1][...] = s_ref[...].astype(BF16)

    return pl.pallas_call(
        body, name=name,
        grid_spec=pltpu.PrefetchScalarGridSpec(
            num_scalar_prefetch=1, grid=(2, Rh // tr),
            in_specs=[pl.BlockSpec((None, None, tr, C), lambda h, i, chip_ref: (mi, h, i, 0))] + _dep_specs(dep),
            out_specs=pl.BlockSpec((None, None, tr, C), lambda h, i, chip_ref: (chip_ref[0], h, i, 0))),
        out_shape=jax.ShapeDtypeStruct((N_CHIPS, 2, Rh, C), BF16),
        compiler_params=_params("parallel", "parallel"),
    )(chip, shard, *_dep_args(dep))


def sum_devices(gathered):
    _, R, C = gathered.shape

    def body(g_ref, o_ref):
        acc = g_ref[0]
        for d in range(1, N_DEV):
            acc = acc + g_ref[d]
        o_ref[...] = acc

    return pl.pallas_call(
        body, name="sum_devices", grid=(1,),
        in_specs=[pl.BlockSpec((N_DEV, R, C), lambda i: (0, 0, 0))],
        out_specs=pl.BlockSpec((R, C), lambda i: (0, 0)),
        out_shape=jax.ShapeDtypeStruct((R, C), F32),
        compiler_params=_params("arbitrary"),
    )(gathered)


def _place():
    x, y, c = lax.axis_index("x"), lax.axis_index("y"), lax.axis_index("c")
    others = [(1 - x, y), (x, 1 - y), (1 - x, 1 - y)]
    return x, y, c, others


def _remote(src, dst, send_sems, recv_sems, idx, device):
    return pltpu.make_async_remote_copy(src_ref=src, dst_ref=dst, send_sem=send_sems.at[idx],
                                        recv_sem=recv_sems.at[idx], device_id=device, device_id_type=MESH)


HBM = pl.BlockSpec(memory_space=pltpu.HBM)
SEM = pl.BlockSpec(memory_space=pltpu.SEMAPHORE)
EFFECT = pltpu.SideEffectType.DATAFLOW_SIDE_EFFECTING


def _in_hbm(t):
    return pltpu.with_memory_space_constraint(t, pltpu.HBM)


def _own_slice(buf, me, c):
    return buf.at[me, c] if len(buf.shape) == 4 else buf.at[me]


def gather_start(staged, bucket_sizes, name):
    n, nb = len(staged), len(bucket_sizes)

    def body(*refs):
        in_refs, sems, token = refs[:n], refs[n:n + 2 * nb], refs[-1]
        x, y, c, others = _place()
        me = 2 * x + y
        t = 0
        for b, size in enumerate(bucket_sizes):
            for i in range(size):
                mine = _own_slice(in_refs[t], me, c)
                for k, chip in enumerate(others):
                    _remote(mine, mine, sems[2 * b], sems[2 * b + 1], 3 * i + k, (*chip, c)).start()
                t += 1
        token[...] = jnp.zeros_like(token)

    sem_shapes = [pltpu.SemaphoreType.DMA((3 * size,)) for size in bucket_sizes for _ in range(2)]
    outs = pl.pallas_call(
        body, name=name,
        out_shape=sem_shapes + [pltpu.HBM(s.shape, s.dtype) for s in staged] + [jax.ShapeDtypeStruct((8, 128), F32)],
        in_specs=[HBM] * n, out_specs=[SEM] * (2 * nb) + [HBM] * n + [pl.BlockSpec(memory_space=pltpu.VMEM)],
        input_output_aliases={t: 2 * nb + t for t in range(n)},
        compiler_params=pltpu.CompilerParams(has_side_effects=EFFECT),
    )(*[_in_hbm(s) for s in staged])
    sems = [(outs[2 * b], outs[2 * b + 1]) for b in range(nb)]
    return sems, list(outs[2 * nb:2 * nb + n]), outs[-1]


def gather_wait(bufs, sems, after, name):
    n = len(bufs)

    def body(*refs):
        in_refs, send_sems, recv_sems = refs[:n], refs[n], refs[n + 1]
        x, y, c, others = _place()
        me = 2 * x + y
        for i in range(n):
            mine = _own_slice(in_refs[i], me, c)
            for k, (ox, oy) in enumerate(others):
                cp = _remote(mine, _own_slice(in_refs[i], 2 * ox + oy, c), send_sems, recv_sems, 3 * i + k,
                             (ox, oy, c))
                cp.wait_send()
                cp.wait_recv()

    return pl.pallas_call(
        body, name=name, out_shape=[pltpu.HBM(b.shape, b.dtype) for b in bufs],
        in_specs=[HBM] * n + [SEM, SEM, ANY], out_specs=[HBM] * n,
        input_output_aliases={t: t for t in range(n)},
        compiler_params=pltpu.CompilerParams(has_side_effects=EFFECT),
    )(*bufs, *sems, after)


def forward_to_sibling(bufs, name):
    n = len(bufs)

    def body(*refs):
        out_refs, (send_sems, recv_sems) = refs[n:2 * n], refs[2 * n:]
        x, y, c, others = _place()
        sibling = (x, y, 1 - c)
        copies = []
        for t in range(n):
            for k, (ox, oy) in enumerate(others):
                mine = out_refs[t].at[2 * ox + oy, c]
                cp = _remote(mine, mine, send_sems, recv_sems, 3 * t + k, sibling)
                cp.start()
                copies.append(cp)
        for t in range(n):
            for k, (ox, oy) in enumerate(others):
                theirs = out_refs[t].at[2 * ox + oy, 1 - c]
                _remote(theirs, theirs, send_sems, recv_sems, 3 * t + k, sibling).wait_recv()
        for cp in copies:
            cp.wait_send()

    return pl.pallas_call(
        body, name=name, in_specs=[ANY] * n, out_specs=[ANY] * n,
        out_shape=[jax.ShapeDtypeStruct(b.shape, b.dtype) for b in bufs],
        input_output_aliases={t: t for t in range(n)},
        scratch_shapes=[pltpu.SemaphoreType.DMA((3 * n,)), pltpu.SemaphoreType.DMA((3 * n,))],
    )(*bufs)


def sibling_start(grads, name):
    n = len(grads)
    lands = [lax.empty((N_CHIPS,) + g.shape[2:], g.dtype) for g in grads]

    def body(*refs):
        in_refs, land_refs, send_sems, recv_sems, token = refs[:n], refs[n:2 * n], refs[2 * n], refs[2 * n + 1], refs[-1]
        x, y, c, _ = _place()
        for t in range(n):
            for j in range(N_CHIPS):
                _remote(in_refs[t].at[j, 1 - c], land_refs[t].at[j], send_sems, recv_sems, N_CHIPS * t + j,
                        (x, y, 1 - c)).start()
        token[...] = jnp.zeros_like(token)

    outs = pl.pallas_call(
        body, name=name,
        out_shape=[pltpu.SemaphoreType.DMA((N_CHIPS * n,))] * 2 + [pltpu.HBM(t.shape, t.dtype) for t in grads + lands]
        + [jax.ShapeDtypeStruct((8, 128), F32)],
        in_specs=[HBM] * (2 * n), out_specs=[SEM, SEM] + [HBM] * (2 * n) + [pl.BlockSpec(memory_space=pltpu.VMEM)],
        input_output_aliases={t: 2 + t for t in range(2 * n)},
        compiler_params=pltpu.CompilerParams(has_side_effects=EFFECT),
    )(*[_in_hbm(t) for t in grads + lands])
    return (outs[0], outs[1]), list(outs[2:2 + n]), list(outs[2 + n:2 + 2 * n]), outs[-1]


def sibling_wait(grads, lands, sems, after, name):
    n = len(grads)

    def body(*refs):
        in_refs, land_refs, send_sems, recv_sems = refs[:n], refs[n:2 * n], refs[2 * n], refs[2 * n + 1]
        x, y, c, _ = _place()
        for t in range(n):
            for j in range(N_CHIPS):
                cp = _remote(in_refs[t].at[j, 1 - c], land_refs[t].at[j], send_sems, recv_sems, N_CHIPS * t + j,
                             (x, y, 1 - c))
                cp.wait_send()
                cp.wait_recv()

    outs = pl.pallas_call(
        body, name=name, out_shape=[pltpu.HBM(t.shape, t.dtype) for t in grads + lands],
        in_specs=[HBM] * (2 * n) + [SEM, SEM, ANY], out_specs=[HBM] * (2 * n),
        input_output_aliases={t: t for t in range(2 * n)},
        compiler_params=pltpu.CompilerParams(has_side_effects=EFFECT),
    )(*grads, *lands, *sems, after)
    return list(outs[:n]), list(outs[n:])


def reduce_start(parts, name):
    n = len(parts)
    lands = [lax.empty((N_CHIPS - 1,) + p.shape[1:], p.dtype) for p in parts]

    def body(*refs):
        in_refs, land_refs, send_sems, recv_sems, token = refs[:n], refs[n:2 * n], refs[2 * n], refs[2 * n + 1], refs[-1]
        x, y, c, others = _place()
        for t in range(n):
            for k, (ox, oy) in enumerate(others):
                _remote(in_refs[t].at[2 * ox + oy], land_refs[t].at[k], send_sems, recv_sems, 3 * t + k,
                        (ox, oy, c)).start()
        token[...] = jnp.zeros_like(token)

    outs = pl.pallas_call(
        body, name=name,
        out_shape=[pltpu.SemaphoreType.DMA((3 * n,))] * 2 + [pltpu.HBM(t.shape, t.dtype) for t in parts + lands]
        + [jax.ShapeDtypeStruct((8, 128), F32)],
        in_specs=[HBM] * (2 * n), out_specs=[SEM, SEM] + [HBM] * (2 * n) + [pl.BlockSpec(memory_space=pltpu.VMEM)],
        input_output_aliases={t: 2 + t for t in range(2 * n)},
        compiler_params=pltpu.CompilerParams(has_side_effects=EFFECT),
    )(*[_in_hbm(t) for t in parts + lands])
    return (outs[0], outs[1]), list(outs[2:2 + n]), list(outs[2 + n:2 + 2 * n]), outs[-1]


def reduce_wait(parts, lands, sems, after, name):
    n = len(parts)

    def body(*refs):
        in_refs, land_refs, send_sems, recv_sems = refs[:n], refs[n:2 * n], refs[2 * n], refs[2 * n + 1]
        x, y, c, others = _place()
        for t in range(n):
            for k, (ox, oy) in enumerate(others):
                cp = _remote(in_refs[t].at[2 * ox + oy], land_refs[t].at[k], send_sems, recv_sems, 3 * t + k,
                             (ox, oy, c))
                cp.wait_send()
                cp.wait_recv()

    outs = pl.pallas_call(
        body, name=name, out_shape=[pltpu.HBM(t.shape, t.dtype) for t in parts + lands],
        in_specs=[HBM] * (2 * n) + [SEM, SEM] + _dep_specs(after), out_specs=[HBM] * (2 * n),
        input_output_aliases={t: t for t in range(2 * n)},
        compiler_params=pltpu.CompilerParams(has_side_effects=EFFECT),
    )(*parts, *lands, *sems, *_dep_args(after))
    return list(outs[:n]), list(outs[n:])


def exchange_halves(groups, name):
    n_groups = len(groups)
    slots = [(gi, mi) for gi, grp in enumerate(groups) for mi in range(grp.shape[0])]

    def body(*refs):
        out_refs = refs[n_groups:2 * n_groups]
        send_sems, recv_sems = refs[2 * n_groups:]
        x, y, c, _ = _place()
        sibling = (x, y, 1 - c)
        copies = []
        for t, (gi, mi) in enumerate(slots):
            mine = out_refs[gi].at[mi, c]
            cp = _remote(mine, mine, send_sems, recv_sems, t, sibling)
            cp.start()
            copies.append(cp)
        for t, (gi, mi) in enumerate(slots):
            theirs = out_refs[gi].at[mi, 1 - c]
            _remote(theirs, theirs, send_sems, recv_sems, t, sibling).wait_recv()
        for cp in copies:
            cp.wait_send()

    return pl.pallas_call(
        body, name=name, in_specs=[ANY] * n_groups, out_specs=[ANY] * n_groups,
        out_shape=[jax.ShapeDtypeStruct(g.shape, g.dtype) for g in groups],
        input_output_aliases={gi: gi for gi in range(n_groups)},
        scratch_shapes=[pltpu.SemaphoreType.DMA((len(slots),)), pltpu.SemaphoreType.DMA((len(slots),))],
    )(*groups)


def gather_devices(buf):
    def body(in_ref, out_ref, send_sems, recv_sems, local_sem):
        x, y, c, _ = _place()
        me = 4 * x + 2 * y + c
        local = pltpu.make_async_copy(in_ref, out_ref.at[me], local_sem)
        local.start()
        copies = []
        for k in range(1, N_DEV):
            fx, fy, fc = (k >> 2) & 1, (k >> 1) & 1, k & 1
            peer = (x ^ fx, y ^ fy, c ^ fc)
            cp = _remote(in_ref, out_ref.at[me], send_sems, recv_sems, k - 1, peer)
            cp.start()
            copies.append(cp)
        for k in range(1, N_DEV):
            fx, fy, fc = (k >> 2) & 1, (k >> 1) & 1, k & 1
            theirs = out_ref.at[4 * (x ^ fx) + 2 * (y ^ fy) + (c ^ fc)]
            _remote(theirs, theirs, send_sems, recv_sems, k - 1, (x, y, c)).wait_recv()
        for cp in copies:
            cp.wait_send()
        local.wait()

    return pl.pallas_call(
        body, name="gather_devices", in_specs=[ANY], out_specs=ANY,
        out_shape=jax.ShapeDtypeStruct((N_DEV,) + buf.shape, buf.dtype),
        scratch_shapes=[pltpu.SemaphoreType.DMA((N_DEV - 1,)), pltpu.SemaphoreType.DMA((N_DEV - 1,)),
                        pltpu.SemaphoreType.DMA],
    )(buf)


class GradReducer:
    def __init__(self, core, chip, kinds):
        self.core, self.chip = core, chip
        self.sizes = dict(kinds)
        self.groups = {kind: None for kind, _ in kinds}

    def send(self, grads, tag):
        arrays = [g.reshape(N_CHIPS, 2, -1, g.shape[-1]) for g, _, _ in grads]
        sems, arrays, lands, token = sibling_start(arrays, f"reduce_sibling_start_{tag}")
        return (sems, arrays, lands, [(kind, mi) for _, kind, mi in grads], tag), token

    def begin(self, sent, after, tag):
        parts, slots = [], []
        for sems, arrays, lands, sent_slots, sent_tag in sent:
            arrays, lands = sibling_wait(arrays, lands, sems, after, f"reduce_sibling_wait_{sent_tag}")
            parts += [add_sibling(g, r, self.core, f"reduce_add_sibling_{sent_tag}_{t}")
                      for t, (g, r) in enumerate(zip(arrays, lands))]
            slots += sent_slots
        sems, parts, lands, token = reduce_start(parts, f"reduce_start_{tag}")
        return (sems, parts, lands, slots, tag), token

    def end(self, state, after):
        sems, parts, lands, slots, tag = state
        parts, lands = reduce_wait(parts, lands, sems, after, f"reduce_wait_{tag}")
        for t, (kind, mi) in enumerate(slots):
            self.groups[kind] = add_chips(parts[t], lands[t], self.chip, self.core, self.groups[kind],
                                          self.sizes[kind], mi, f"reduce_add_chips_{tag}_{t}")

    def finish(self):
        kinds = list(self.groups)
        return dict(zip(kinds, exchange_halves([self.groups[k] for k in kinds], "reduce_swap")))


def _ffn_fwd(x, gain, wg, wu, wd, tag):
    h, rstd = rmsnorm_fwd(x, gain, BF16, f"ffn_norm_{tag}")
    up, silu, dsilu, act = ffn_gateup(h, wg, wu, f"ffn_gateup_{tag}")
    out = mm_residual(act, wd, x, 0.5, wd.shape[0] // N_CHIPS, f"ffn_down_{tag}", tm_target=704)
    return out, (x, gain, h, rstd, up, silu, dsilu, act)


def _ffn_bwd(dout, dy, saved, wg, wu, wd, index, reducer, dep=None, per_tensor=False):
    x, gain, h, rstd, up, silu, dsilu, act = saved
    D = x.shape[1]
    Fs = wg.shape[2]
    td = _tile(D, 1024, 128)
    tag = f"ffn{index}"
    begun = []

    def begin(sent, after, suffix):
        state, token = reducer.begin(sent, after, tag + suffix)
        begun.append(state)
        return token

    dgate, dup = ffn_bwd_act(dy, wd, up, silu, dsilu, f"ffn_bwd_act_{index}", dep=dep)
    d_wd = mm_tn(act, dy, Fs, td, f"ffn_bwd_wd_{index}")
    if per_tensor:
        sent_d, tok = reducer.send([(d_wd, "down", index)], tag + "d")
        d_wg = mm_tn(h, dgate, td, Fs, f"ffn_bwd_wg_{index}", stacked_out=True, dep=tok)
        toks = [begin([sent_d], d_wg, "d")]
        sent_g, tok = reducer.send([(d_wg, "gate", index)], tag + "g")
        d_wu = mm_tn(h, dup, td, Fs, f"ffn_bwd_wu_{index}", stacked_out=True, dep=toks + [tok])
        toks = [begin([sent_g], d_wu, "g")]
        sent, tok = reducer.send([(d_wu, "up", index)], tag + "u")
    else:
        d_wg = mm_tn(h, dgate, td, Fs, f"ffn_bwd_wg_{index}", stacked_out=True)
        d_wu = mm_tn(h, dup, td, Fs, f"ffn_bwd_wu_{index}", stacked_out=True)
        toks = []
        sent, tok = reducer.send([(d_wd, "down", index), (d_wg, "gate", index), (d_wu, "up", index)], tag + "u")
    dh = mm_nt([(dgate, wg), (dup, wu)], _tile(x.shape[0], 704), D, Fs, f"ffn_bwd_dh_{index}", stacked_w=True,
               dep=toks + [tok])
    tok = begin([sent], dh, "u")
    dx, dgain, dx_half = rmsnorm_bwd(dh, x, gain, rstd, dout, f"ffn_norm_bwd_{index}")
    return dx, dx_half, dgain, begun, tok


def kernel(x, meta, ffn_norm, ffn_w_gate, ffn_w_up, ffn_w_down, gla_norm, gla_w_in, gla_w_lr, gla_b_lr, gla_head_norm, gla_w_out, pool_norm, pool_w, pool_b, pool_scale, final_norm, loss_target, m_meta, m_ffn_norm, m_ffn_w_gate, m_ffn_w_up, m_ffn_w_down, m_gla_norm, m_gla_w_in, m_gla_w_lr, m_gla_b_lr, m_gla_head_norm, m_gla_w_out, m_pool_norm, m_pool_w, m_pool_b, m_pool_scale, m_final_norm, v_meta, v_ffn_norm, v_ffn_w_gate, v_ffn_w_up, v_ffn_w_down, v_gla_norm, v_gla_w_in, v_gla_w_lr, v_gla_b_lr, v_gla_head_norm, v_gla_w_out, v_pool_norm, v_pool_w, v_pool_b, v_pool_scale, v_final_norm):
    S, D = x.shape[1], x.shape[2]
    M = OFF + S
    Dq = D // N_CHIPS
    Fs = ffn_w_gate.shape[3]
    F = N_CHIPS * Fs
    dk = D // 2
    n_in = gla_w_in.shape[2]
    W = D // 4
    core = lax.axis_index("c").astype(jnp.int32).reshape(1)
    chip_id = 2 * lax.axis_index("x") + lax.axis_index("y")
    chip = chip_id.astype(jnp.int32).reshape(1)

    small = jnp.concatenate([_pad_rows(t) for t in (
        meta, ffn_norm.reshape(4, Dq), gla_w_lr.reshape(8, Dq), pool_norm, pool_b.reshape(1, Dq), pool_scale)],
        axis=0)
    def stage(w, kind, n, mi, dep=None):
        return stage_shard(w.reshape(n, 2, -1, w.shape[-1]), mi, chip, f"stage_{kind}_{mi}", dep=dep)

    ffn_stage = lambda mi, dep=None: [stage(ffn_w_gate, "gate", 4, mi, dep), stage(ffn_w_up, "up", 4, mi, dep),
                                      stage(ffn_w_down, "down", 4, mi, dep)]
    small_stage = lax.dynamic_update_slice(jnp.zeros((N_CHIPS,) + small.shape, F32), small[None], (chip_id, 0, 0))
    first = ffn_stage(0)
    buckets = [first[:2] + [small_stage], first[2:]]
    sizes = [len(b) for b in buckets]
    gather_sems, in_flight, tok = gather_start([t for b in buckets for t in b], sizes, "gather_start_first")
    buckets = [[stage(gla_w_in, "win", 1, 0, tok), stage(gla_w_out, "wout", 1, 0, tok)],
               ffn_stage(1, tok), ffn_stage(2, tok), [stage(pool_w, "pool", 1, 0, tok)] + ffn_stage(3, tok)]
    more_sems, more_in_flight, gather_token = gather_start([t for b in buckets for t in b],
                                                            [len(b) for b in buckets], "gather_start_rest")
    sizes += [len(b) for b in buckets]
    gather_sems += more_sems
    in_flight += more_in_flight
    starts = [sum(sizes[:b]) for b in range(len(sizes))]

    def arrive(b, after, n_big):
        bufs = gather_wait(in_flight[starts[b]:starts[b] + sizes[b]], gather_sems[b], after, f"gather_wait_{b}")
        return forward_to_sibling(bufs[:n_big], f"gather_forward_{b}") + bufs[n_big:]

    ffn_w = lambda t: (t[0].reshape(N_CHIPS, D, Fs), t[1].reshape(N_CHIPS, D, Fs), t[2].reshape(F, D))
    got = arrive(0, gather_token, 2)
    wg, wu, wd = [None] * 4, [None] * 4, [None] * 4
    wg[0], wu[0] = got[0].reshape(N_CHIPS, D, Fs), got[1].reshape(N_CHIPS, D, Fs)
    sm = got[2]
    unshard = lambda t: t.transpose(1, 0, 2).reshape(t.shape[1], D)
    meta_f = unshard(sm[:, 0:16])
    ffn_norm_f = unshard(sm[:, 16:20])
    w_lr_f = sm[:, 24:32].reshape(N_CHIPS, GATE_RANK, dk // N_CHIPS).transpose(1, 0, 2).reshape(GATE_RANK, dk)
    pool_norm_f = sm[:, 32].reshape(1, D)
    pool_b_f = sm[:, 40].reshape(N_CHIPS, 4, W // N_CHIPS).transpose(1, 0, 2).reshape(1, D)
    pool_scale_f = sm[:, 48].reshape(1, D)
    wlr_pad = jnp.pad(w_lr_f.astype(BF16), ((0, LR_W - GATE_RANK), (0, 0)))
    final_g = final_norm.reshape(1, D)
    qkv = 2 * dk + D

    x0 = jnp.concatenate([jnp.zeros((PAD, D), F32), meta_f, x[0]], axis=0)
    target = jnp.pad(loss_target[0], ((OFF, 0), (0, 0)))
    h0, rstd0 = rmsnorm_fwd(x0, ffn_norm_f[0:1], BF16, "ffn_norm_0")
    acts0 = ffn_gateup(h0, wg[0], wu[0], "ffn_gateup_0")
    wd[0] = arrive(1, acts0[3], 1)[0].reshape(F, D)
    x1 = mm_residual(acts0[3], wd[0], x0, 0.5, Fs, "ffn_down_0", tm_target=704)
    ffn0 = (x0, ffn_norm_f[0:1], h0, rstd0, *acts0)
    got = arrive(2, x1, 2)
    w_in = got[0].reshape(N_CHIPS, D, n_in).transpose(1, 0, 2).reshape(D, N_CHIPS * n_in)
    w_out = got[1].reshape(D, D)
    w_all = jnp.concatenate([w_in[:, :qkv], w_in[:, qkv + GATE_RANK:], w_in[:, qkv:qkv + GATE_RANK],
                             jnp.zeros((D, LR_W - GATE_RANK), BF16)], axis=1)
    hg, rstd_g = rmsnorm_fwd(x1, gla_norm, BF16, "gla_norm")
    proj = mm_nn(hg, w_all, F32, "gla_proj")
    o, st = gla_fwd(proj, wlr_pad, gla_b_lr, D)
    gated = gla_post_fwd(o, proj, gla_head_norm, D)
    x2 = mm_residual(gated, w_out, x1, 1.0, D, "gla_out")
    wg[1], wu[1], wd[1] = ffn_w(arrive(3, x2, 3))
    x3, ffn1 = _ffn_fwd(x2, ffn_norm_f[1:2], wg[1], wu[1], wd[1], "1")
    wg[2], wu[2], wd[2] = ffn_w(arrive(4, x3, 3))
    x4, ffn2 = _ffn_fwd(x3, ffn_norm_f[2:3], wg[2], wu[2], wd[2], "2")
    got = arrive(5, x4, 4)
    w_pool = got[0].reshape(N_CHIPS, 4, W // N_CHIPS, W).transpose(1, 0, 2, 3).reshape(4, W, W)
    wg[3], wu[3], wd[3] = ffn_w(got[1:])
    hp, rstd_p = rmsnorm_fwd(x4, pool_norm_f, F32, "pool_norm")
    pooled = pool_window(hp)
    x5 = pool_mix(pooled, x4, w_pool, pool_b_f, pool_scale_f)
    x6, ffn3 = _ffn_fwd(x5, ffn_norm_f[3:4], wg[3], wu[3], wd[3], "3")
    loss, dx6, d_final, dy6 = final_loss(x6, final_g, target)

    reducer = GradReducer(core, chip, [("gate", 4), ("up", 4), ("down", 4), ("win", 1), ("wout", 1), ("pool", 1)])

    def settle(begun, after):
        for state in begun:
            reducer.end(state, after)

    dx5, _, dn3, red3, tok = _ffn_bwd(dx6, dy6, ffn3, wg[3], wu[3], wd[3], 3, reducer)
    dpooled, d_wpool, d_pool_b, d_pool_scale = pool_mix_bwd(dx5, pooled, w_pool, pool_b_f, pool_scale_f, dep=tok)
    dhp = pool_window_bwd(dpooled)
    dx4, d_pool_norm, dy4 = rmsnorm_bwd(dhp, x4, pool_norm_f, rstd_p, dx5, "pool_norm_bwd")
    d_wpool = d_wpool.reshape(4, N_CHIPS, W // N_CHIPS, W).transpose(1, 0, 2, 3)
    sent_p, tok = reducer.send([(d_wpool, "pool", 0)], "pool")
    dx3, dy3, dn2, red2, tok = _ffn_bwd(dx4, dy4, ffn2, wg[2], wu[2], wd[2], 2, reducer, dep=tok)
    redp, tok_p = reducer.begin([sent_p], dx3, "pool")
    dx2, _, dn1, red1, tok = _ffn_bwd(dx3, dy3, ffn1, wg[1], wu[1], wd[1], 1, reducer, dep=[tok, tok_p])
    tm = _tile(M, 352)
    td = _tile(D, 512, 128)
    d_wout = mm_tn(gated, dx2, td, td, "gla_out_bwd_w", dep=tok)
    sent_o, tok = reducer.send([(d_wout, "wout", 0)], "wout")
    dgated = mm_nt([(dx2, w_out)], tm, td, D, "gla_out_bwd_act", dep=tok)
    redo, tok_o = reducer.begin([sent_o], dgated, "wout")
    do, dr, d_head_norm = gla_post_bwd(dgated, o, proj, gla_head_norm, D)
    dproj, dwlr, dblr = gla_bwd(proj, wlr_pad, gla_b_lr, st, do, dr, D)
    tp = _tile(proj.shape[1], 896, 128)
    d_wall = mm_tn(hg, dproj, td, tp, "gla_proj_bwd_w", dep=tok_o)
    d_win = jnp.concatenate([d_wall[:, :qkv], d_wall[:, qkv + D:qkv + D + GATE_RANK], d_wall[:, qkv:qkv + D]], axis=1)
    d_win = d_win.reshape(D, N_CHIPS, n_in).transpose(1, 0, 2)
    sent_i, tok = reducer.send([(d_win, "win", 0)], "win")
    dhg = mm_nt([(dproj, w_all)], tm, D, tp, "gla_proj_bwd_act", dep=tok)
    redi, tok = reducer.begin([sent_i], dhg, "win")
    dx1, d_gla_norm, dy1 = rmsnorm_bwd(dhg, x1, gla_norm, rstd_g, dx2, "gla_norm_bwd")
    dx0, _, dn0, red0, tok = _ffn_bwd(dx1, dy1, ffn0, wg[0], wu[0], wd[0], 0, reducer, dep=tok, per_tensor=True)
    settle(red3 + [redp] + red2 + red1 + [redo, redi] + red0[:-1], tok)

    d_wlr = dwlr[:, :GATE_RANK].transpose(1, 0, 2).reshape(GATE_RANK, dk)
    pieces = [dx0[PAD:OFF], dn0, dn1, dn2, dn3, d_gla_norm, d_wlr,
              dblr.reshape(1, dk), d_head_norm, d_pool_norm, d_pool_b, d_pool_scale, d_final]
    packed = jnp.concatenate([_pad_rows(p.reshape(-1, Dq)) for p in pieces], axis=0)
    total = sum_devices(gather_devices(packed))

    settle(red0[-1:], [total] + list(reducer.groups.values()))
    reduced = reducer.finish()
    g_gate = reduced["gate"].reshape(ffn_w_gate.shape)
    g_up = reduced["up"].reshape(ffn_w_up.shape)
    g_down = reduced["down"].reshape(ffn_w_down.shape)
    g_win = reduced["win"].reshape(gla_w_in.shape)
    g_wout = reduced["wout"].reshape(gla_w_out.shape)
    g_wpool = reduced["pool"].reshape(pool_w.shape)
    sums, at = [], 0
    for p in pieces:
        r = p.size // Dq
        sums.append(total[at:at + r].reshape(p.shape))
        at += r + (-r % 8)
    (s_meta, s_n0, s_n1, s_n2, s_n3, s_gla_norm, s_wlr, s_blr, s_head_norm, s_pool_norm, s_pool_b, s_pool_scale,
     s_final) = sums
    s_ffn_norm = jnp.stack([s_n0, s_n1, s_n2, s_n3], axis=0)[:, 0]
    mine = lambda t, width: lax.dynamic_slice_in_dim(t, chip_id * width, width, axis=t.ndim - 1)
    g_meta = mine(s_meta, Dq)
    g_ffn_norm = mine(s_ffn_norm, Dq).reshape(ffn_norm.shape)
    g_gla_norm = s_gla_norm
    g_wlr = mine(s_wlr, dk // N_CHIPS).reshape(gla_w_lr.shape)
    g_blr = s_blr
    g_head_norm = s_head_norm
    g_pool_norm = mine(s_pool_norm, Dq)
    g_pool_b = mine(s_pool_b.reshape(4, W), W // N_CHIPS).reshape(pool_b.shape)
    g_pool_scale = mine(s_pool_scale, Dq)
    g_final = s_final.reshape(final_norm.shape)

    weights = [meta, ffn_norm, ffn_w_gate, ffn_w_up, ffn_w_down, gla_norm, gla_w_in, gla_w_lr, gla_b_lr,
               gla_head_norm, gla_w_out, pool_norm, pool_w, pool_b, pool_scale, final_norm]
    moments_m = [m_meta, m_ffn_norm, m_ffn_w_gate, m_ffn_w_up, m_ffn_w_down, m_gla_norm, m_gla_w_in, m_gla_w_lr,
                 m_gla_b_lr, m_gla_head_norm, m_gla_w_out, m_pool_norm, m_pool_w, m_pool_b, m_pool_scale,
                 m_final_norm]
    moments_v = [v_meta, v_ffn_norm, v_ffn_w_gate, v_ffn_w_up, v_ffn_w_down, v_gla_norm, v_gla_w_in, v_gla_w_lr,
                 v_gla_b_lr, v_gla_head_norm, v_gla_w_out, v_pool_norm, v_pool_w, v_pool_b, v_pool_scale,
                 v_final_norm]
    grads_w = [g_meta, g_ffn_norm, g_gate, g_up, g_down, g_gla_norm, g_win, g_wlr, g_blr, g_head_norm, g_wout,
               g_pool_norm, g_wpool, g_pool_b, g_pool_scale, g_final]
    from_swap = {2, 3, 4, 6, 10, 12}
    deltas, new_m, new_v = [], [], []
    for i, (w, g, m, v) in enumerate(zip(weights, grads_w, moments_m, moments_v)):
        outs = adamw(w, g, m, v, f"adamw_{i}", copy_g=i in from_swap)
        deltas.append(outs[0])
        new_m.append(outs[1])
        new_v.append(outs[2])
        if i in from_swap:
            grads_w[i] = outs[3]

    loss = lax.psum(loss[0, 0], ("x", "y", "c"))
    grad_x = dx0[OFF:][None]
    return (loss, grad_x, *grads_w, *deltas, *new_m, *new_v)
```

```python
import functools

import jax
import jax.numpy as jnp
from jax import lax
from jax.experimental import pallas as pl
from jax.experimental.pallas import tpu as pltpu

F32 = jnp.float32
BF16 = jnp.bfloat16
MESH = pl.DeviceIdType.MESH
ANY = pl.BlockSpec(memory_space=pl.ANY)

N_META = 16
CHUNK = 64
PAD = CHUNK - N_META
OFF = PAD + N_META
EPS = 1e-6
HEADS = 4
GATE_RANK = 16
GATE_NORM = 16.0
LR_W = 128
N_CHIPS = 4
N_DEV = 8
ADAM_LR, ADAM_B1, ADAM_B2, ADAM_EPS, ADAM_WD, ADAM_STEP = 0.001, 0.9, 0.999, 1e-08, 0.01, 10
VMEM_LIMIT = 56 * 1024 * 1024
ROW_TILE = 352
ONE_BUFFER = pl.Buffered(1)


def _tile(n, target, mult=16):
    best = None
    for d in range(mult, min(n, target) + 1, mult):
        if n % d == 0:
            best = d
    return best if best is not None else n


def _params(*sem):
    return pltpu.CompilerParams(dimension_semantics=sem, vmem_limit_bytes=VMEM_LIMIT)


def _dot(a, b):
    return jnp.dot(a, b, preferred_element_type=F32)


def _dot_nt(a, b):
    return lax.dot_general(a, b, (((1,), (1,)), ((), ())), preferred_element_type=F32)


def _dot_tn(a, b):
    return lax.dot_general(a, b, (((0,), (0,)), ((), ())), preferred_element_type=F32)


MXU_WIDTH = 256


def _chunks(n):
    return [slice(lo, min(lo + MXU_WIDTH, n)) for lo in range(0, n, MXU_WIDTH)]


def _sigmoid(x):
    return 1.0 / (1.0 + jnp.exp(-x))


def _rows(tile, width=1):
    return lax.broadcasted_iota(jnp.int32, (tile, width), 0)


def _dep_args(dep):
    if dep is None:
        return []
    return list(dep) if isinstance(dep, (list, tuple)) else [dep]


def _dep_specs(dep):
    return [ANY] * len(_dep_args(dep))


def _pad_rows(t):
    return jnp.pad(t, ((0, -t.shape[0] % 8), (0, 0)))


def rmsnorm_fwd(x, g, out_dtype, name):
    M, D = x.shape
    tr = _tile(M, ROW_TILE)

    def body(x_ref, g_ref, h_ref, r_ref):
        xv = x_ref[...]
        r = lax.rsqrt(jnp.mean(xv * xv, axis=-1, keepdims=True) + EPS)
        h_ref[...] = (xv * r * g_ref[...]).astype(out_dtype)
        r_ref[...] = r

    return pl.pallas_call(
        body, name=name, grid=(M // tr,),
        in_specs=[pl.BlockSpec((tr, D), lambda i: (i, 0)), pl.BlockSpec((1, D), lambda i: (0, 0))],
        out_specs=[pl.BlockSpec((tr, D), lambda i: (i, 0)), pl.BlockSpec((tr, 1), lambda i: (i, 0))],
        out_shape=[jax.ShapeDtypeStruct((M, D), out_dtype), jax.ShapeDtypeStruct((M, 1), F32)],
        compiler_params=_params("parallel"),
    )(x, g)


def rmsnorm_bwd(dh, x, g, rstd, dres, name):
    M, D = x.shape
    tr = _tile(M, ROW_TILE)

    def body(dh_ref, x_ref, g_ref, r_ref, dres_ref, dx_ref, dg_ref, half_ref):
        @pl.when(pl.program_id(0) == 0)
        def _():
            dg_ref[...] = jnp.zeros_like(dg_ref)

        r = r_ref[...]
        xhat = x_ref[...] * r
        dhv = dh_ref[...]
        gd = dhv * g_ref[...]
        dx = dres_ref[...] + r * (gd - xhat * jnp.mean(gd * xhat, axis=-1, keepdims=True))
        dx_ref[...] = dx
        half_ref[...] = (0.5 * dx).astype(BF16)
        dg_ref[...] += jnp.sum(dhv * xhat, axis=0, keepdims=True)

    row = pl.BlockSpec((tr, D), lambda i: (i, 0))
    vec = pl.BlockSpec((1, D), lambda i: (0, 0))
    return pl.pallas_call(
        body, name=name, grid=(M // tr,),
        in_specs=[row, row, vec, pl.BlockSpec((tr, 1), lambda i: (i, 0)), row],
        out_specs=[row, vec, row],
        out_shape=[jax.ShapeDtypeStruct((M, D), F32), jax.ShapeDtypeStruct((1, D), F32),
                   jax.ShapeDtypeStruct((M, D), BF16)],
        compiler_params=_params("arbitrary"),
    )(dh, x, g, rstd, dres)


def final_loss(x, g, target):
    M, D = x.shape
    tr = _tile(M, ROW_TILE)

    def body(x_ref, g_ref, t_ref, loss_ref, dx_ref, dg_ref, half_ref):
        i = pl.program_id(0)

        @pl.when(i == 0)
        def _():
            loss_ref[...] = jnp.zeros_like(loss_ref)
            dg_ref[...] = jnp.zeros_like(dg_ref)

        live = (_rows(tr) + i * tr) >= OFF
        xv = x_ref[...]
        gv = g_ref[...]
        r = lax.rsqrt(jnp.mean(xv * xv, axis=-1, keepdims=True) + EPS)
        xhat = xv * r
        err = jnp.where(live, xhat * gv - t_ref[...], 0.0)
        loss_ref[...] += 0.5 * jnp.sum(jnp.mean(err * err, axis=-1, keepdims=True), axis=0, keepdims=True)
        dy = err * (1.0 / D)
        gd = dy * gv
        dx = r * (gd - xhat * jnp.mean(gd * xhat, axis=-1, keepdims=True))
        dx_ref[...] = dx
        half_ref[...] = (0.5 * dx).astype(BF16)
        dg_ref[...] += jnp.sum(dy * xhat, axis=0, keepdims=True)

    row = pl.BlockSpec((tr, D), lambda i: (i, 0))
    vec = pl.BlockSpec((1, D), lambda i: (0, 0))
    return pl.pallas_call(
        body, name="final_loss", grid=(M // tr,),
        in_specs=[row, vec, row],
        out_specs=[pl.BlockSpec((1, 1), lambda i: (0, 0)), row, vec, row],
        out_shape=[jax.ShapeDtypeStruct((1, 1), F32), jax.ShapeDtypeStruct((M, D), F32),
                   jax.ShapeDtypeStruct((1, D), F32), jax.ShapeDtypeStruct((M, D), BF16)],
        compiler_params=_params("arbitrary"),
    )(x, g, target)


def mm_nn(a, w, out_dtype, name, tm_target=704, tn_target=896):
    M, K = a.shape
    N = w.shape[1]
    tm, tn = _tile(M, tm_target), _tile(N, tn_target, 128)

    def body(a_ref, w_ref, o_ref):
        o_ref[...] = _dot(a_ref[...], w_ref[...]).astype(out_dtype)

    return pl.pallas_call(
        body, name=name, grid=(N // tn, M // tm),
        in_specs=[pl.BlockSpec((tm, K), lambda n, i: (i, 0)), pl.BlockSpec((K, tn), lambda n, i: (0, n))],
        out_specs=pl.BlockSpec((tm, tn), lambda n, i: (i, n)),
        out_shape=jax.ShapeDtypeStruct((M, N), out_dtype),
        compiler_params=_params("parallel", "parallel"),
    )(a, w)


def ffn_gateup(h, wg, wu, name):
    M, D = h.shape
    Fs = wg.shape[2]
    tm = _tile(M, 352)

    def body(h_ref, wg_ref, wu_ref, u_ref, silu_ref, dsilu_ref, a_ref):
        hv = h_ref[...]
        for cols in _chunks(Fs):
            g = _dot(hv, wg_ref[:, cols])
            u = _dot(hv, wu_ref[:, cols])
            s = _sigmoid(g)
            silu = g * s
            u_ref[:, cols] = u.astype(BF16)
            silu_ref[:, cols] = silu.astype(BF16)
            dsilu_ref[:, cols] = (s * (1.0 + g * (1.0 - s))).astype(BF16)
            a_ref[:, cols] = (silu * u).astype(BF16)

    wspec = pl.BlockSpec((None, D, Fs), lambda j, i: (j, 0, 0))
    ospec = pl.BlockSpec((tm, Fs), lambda j, i: (i, j))
    return pl.pallas_call(
        body, name=name, grid=(N_CHIPS, M // tm),
        in_specs=[pl.BlockSpec((tm, D), lambda j, i: (i, 0)), wspec, wspec],
        out_specs=[ospec] * 4,
        out_shape=[jax.ShapeDtypeStruct((M, N_CHIPS * Fs), BF16)] * 4,
        compiler_params=_params("parallel", "parallel"),
    )(h, wg, wu)


def mm_residual(a, w, x, scale, tk, name, tm_target=352):
    M, N = x.shape
    K = w.shape[0]
    tm = _tile(M, tm_target)

    def body(a_ref, w_ref, x_ref, o_ref, acc):
        k = pl.program_id(1)

        @pl.when(k == 0)
        def _():
            acc[...] = jnp.zeros_like(acc)

        acc[...] += _dot(a_ref[...], w_ref[...])

        @pl.when(k == pl.num_programs(1) - 1)
        def _():
            o_ref[...] = x_ref[...] + scale * acc[...]

    aspec = pl.BlockSpec((tm, tk), lambda i, k: (i, k))
    return pl.pallas_call(
        body, name=name, grid=(M // tm, K // tk),
        in_specs=[aspec, pl.BlockSpec((tk, N), lambda i, k: (k, 0)),
                  pl.BlockSpec((tm, N), lambda i, k: (i, 0), pipeline_mode=ONE_BUFFER)],
        out_specs=pl.BlockSpec((tm, N), lambda i, k: (i, 0), pipeline_mode=ONE_BUFFER),
        out_shape=jax.ShapeDtypeStruct((M, N), F32),
        scratch_shapes=[pltpu.VMEM((tm, N), F32)],
        compiler_params=_params("parallel", "arbitrary"),
    )(a, w, x)


def ffn_bwd_act(dy, wd, up, silu, dsilu, name, dep=None):
    M, D = dy.shape
    F = wd.shape[0]
    Fs = F // N_CHIPS
    tm = _tile(M, 704)

    def body(dy_ref, wd_ref, u_ref, silu_ref, dsilu_ref, *rest):
        dg_ref, du_ref = rest[-2:]
        dy = dy_ref[...]
        for cols in _chunks(Fs):
            da = _dot_nt(dy, wd_ref[cols, :])
            dg_ref[:, cols] = (da * u_ref[:, cols].astype(F32) * dsilu_ref[:, cols].astype(F32)).astype(BF16)
            du_ref[:, cols] = (da * silu_ref[:, cols].astype(F32)).astype(BF16)

    fspec = pl.BlockSpec((tm, Fs), lambda j, i: (i, j))
    return pl.pallas_call(
        body, name=name, grid=(N_CHIPS, M // tm),
        in_specs=[pl.BlockSpec((tm, D), lambda j, i: (i, 0)), pl.BlockSpec((Fs, D), lambda j, i: (j, 0)),
                  fspec, fspec, fspec] + _dep_specs(dep),
        out_specs=[fspec, fspec],
        out_shape=[jax.ShapeDtypeStruct((M, F), BF16)] * 2,
        compiler_params=_params("parallel", "parallel"),
    )(dy, wd, up, silu, dsilu, *_dep_args(dep))


def mm_tn(a, b, ta, tb, name, stacked_out=False, out_dtype=BF16, dep=None):
    T, Ma = a.shape
    Nb = b.shape[1]

    def body(a_ref, b_ref, *rest):
        o_ref = rest[-1]
        o_ref[...] = _dot_tn(a_ref[...], b_ref[...].astype(BF16)).astype(out_dtype)

    if stacked_out:
        out_spec = pl.BlockSpec((None, ta, tb), lambda jb, ja: (jb, ja, 0))
        out_shape = jax.ShapeDtypeStruct((Nb // tb, Ma, tb), out_dtype)
    else:
        out_spec = pl.BlockSpec((ta, tb), lambda jb, ja: (ja, jb))
        out_shape = jax.ShapeDtypeStruct((Ma, Nb), out_dtype)
    return pl.pallas_call(
        body, name=name, grid=(Nb // tb, Ma // ta),
        in_specs=[pl.BlockSpec((T, ta), lambda jb, ja: (0, ja)), pl.BlockSpec((T, tb), lambda jb, ja: (0, jb))]
        + _dep_specs(dep),
        out_specs=out_spec, out_shape=out_shape,
        compiler_params=_params("parallel", "parallel"),
    )(a, b, *_dep_args(dep))


def mm_nt(pairs, tm, tn, tk, name, stacked_w=False, dep=None):
    M, K = pairs[0][0].shape
    N = pairs[0][1].shape[1] if stacked_w else pairs[0][1].shape[0]
    n_pairs = len(pairs)

    def body(*refs):
        o_ref, acc = refs[-2:]
        k = pl.program_id(2)

        @pl.when(k == 0)
        def _():
            acc[...] = jnp.zeros_like(acc)

        for p in range(n_pairs):
            acc[...] += _dot_nt(refs[2 * p][...].astype(BF16), refs[2 * p + 1][...])

        @pl.when(k == pl.num_programs(2) - 1)
        def _():
            o_ref[...] = acc[...]

    aspec = pl.BlockSpec((tm, tk), lambda i, n, k: (i, k))
    if stacked_w:
        wspec = pl.BlockSpec((None, tn, tk), lambda i, n, k: (k, n, 0))
    else:
        wspec = pl.BlockSpec((tn, tk), lambda i, n, k: (n, k))
    return pl.pallas_call(
        body, name=name, grid=(M // tm, N // tn, K // tk),
        in_specs=[aspec, wspec] * n_pairs + _dep_specs(dep),
        out_specs=pl.BlockSpec((tm, tn), lambda i, n, k: (i, n), pipeline_mode=ONE_BUFFER),
        out_shape=jax.ShapeDtypeStruct((M, N), F32),
        scratch_shapes=[pltpu.VMEM((tm, tn), F32)],
        compiler_params=_params("parallel", "parallel", "arbitrary"),
    )(*[t for pair in pairs for t in pair], *_dep_args(dep))


def _tri(lower):
    r = lax.broadcasted_iota(jnp.int32, (CHUNK, CHUNK), 0)
    c = lax.broadcasted_iota(jnp.int32, (CHUNK, CHUNK), 1)
    return (r >= c) if lower else (r <= c)


def _tri_sum(mask, x, pieces):
    ones = mask.astype(BF16)
    acc = jnp.zeros_like(x)
    rest = x
    for _ in range(pieces):
        piece = rest.astype(BF16)
        acc = acc + _dot(ones, piece)
        rest = rest - piece.astype(F32)
    return acc


def _gla_gates(lr, wlr, blr, chunk):
    z = _dot(lr, wlr) + blr
    live = (_rows(CHUNK) + chunk * CHUNK) >= PAD
    lg = jnp.where(live, (jnp.minimum(z, 0.0) - jnp.log(1.0 + jnp.exp(-jnp.abs(z)))) * (1.0 / GATE_NORM), 0.0)
    b = _tri_sum(_tri(True), lg, 3)
    b_last = jnp.sum(lg, axis=0, keepdims=True)
    b_mid = jnp.sum(jnp.where(_rows(CHUNK) < CHUNK // 2, lg, 0.0), axis=0, keepdims=True)
    return z, live, b, b_last, b_mid


def _gla_specs(D, chunk_of):
    lr_blk = (3 * D) // LR_W
    return [
        pl.BlockSpec((CHUNK, D // 2), lambda c: (chunk_of(c), 0)),
        pl.BlockSpec((CHUNK, D // 2), lambda c: (chunk_of(c), 1)),
        pl.BlockSpec((CHUNK, D), lambda c: (chunk_of(c), 1)),
        pl.BlockSpec((CHUNK, LR_W), lambda c: (chunk_of(c), lr_blk)),
        pl.BlockSpec((LR_W, D // 2), lambda c: (0, 0)),
        pl.BlockSpec((1, D // 2), lambda c: (0, 0)),
    ]


def gla_fwd(proj, wlr, blr, D):
    M = proj.shape[0]
    n = M // CHUNK
    dkh, dvh = D // 2 // HEADS, D // HEADS
    qscale = float(dkh) ** -0.5

    def body(q_ref, k_ref, v_ref, lr_ref, wlr_ref, blr_ref, o_ref, st_ref, S):
        c = pl.program_id(0)

        @pl.when(c == 0)
        def _():
            S[...] = jnp.zeros_like(S)

        lr = lr_ref[...].astype(BF16)
        for h in range(HEADS):
            kc, vc = slice(h * dkh, (h + 1) * dkh), slice(h * dvh, (h + 1) * dvh)
            _, _, b, b_last, b_mid = _gla_gates(lr, wlr_ref[:, kc], blr_ref[:, kc], c)
            q = q_ref[:, kc] * qscale
            k = k_ref[:, kc]
            v = v_ref[:, vc].astype(BF16)
            s0 = S[h]
            st_ref[h] = s0
            qb = (q * jnp.exp(b)).astype(BF16)
            kb = (k * jnp.exp(b_last - b)).astype(BF16)
            qt = (q * jnp.exp(b - b_mid)).astype(BF16)
            kt = (k * jnp.exp(b_mid - b)).astype(BF16)
            a = jnp.where(_tri(True), _dot_nt(qt, kt), 0.0).astype(BF16)
            o_ref[:, vc] = _dot_nt(qb, s0.astype(BF16)) + _dot(a, v)
            S[h] = jnp.exp(b_last) * s0 + _dot_tn(v, kb)

    return pl.pallas_call(
        body, name="gla_fwd", grid=(n,),
        in_specs=_gla_specs(D, lambda c: c),
        out_specs=[pl.BlockSpec((CHUNK, D), lambda c: (c, 0)),
                   pl.BlockSpec((None, HEADS, dvh, dkh), lambda c: (c, 0, 0, 0))],
        out_shape=[jax.ShapeDtypeStruct((M, D), F32), jax.ShapeDtypeStruct((n, HEADS, dvh, dkh), F32)],
        scratch_shapes=[pltpu.VMEM((HEADS, dvh, dkh), F32)],
        compiler_params=_params("arbitrary"),
    )(proj, proj, proj, proj, wlr, blr)


def gla_bwd(proj, wlr, blr, st, do, dr, D):
    M = proj.shape[0]
    n = M // CHUNK
    dkh, dvh = D // 2 // HEADS, D // HEADS
    qscale = float(dkh) ** -0.5
    rev = lambda c: n - 1 - c

    def body(q_ref, k_ref, v_ref, lr_ref, wlr_ref, blr_ref, st_ref, do_ref, dr_ref,
             dp_ref, dwlr_ref, dblr_ref, dS):
        step = pl.program_id(0)
        c = n - 1 - step

        @pl.when(step == 0)
        def _():
            dS[...] = jnp.zeros_like(dS)
            dwlr_ref[...] = jnp.zeros_like(dwlr_ref)
            dblr_ref[...] = jnp.zeros_like(dblr_ref)

        lr = lr_ref[...].astype(BF16)
        lower = _tri(True)
        dlr = None
        for h in range(HEADS):
            kc, vc = slice(h * dkh, (h + 1) * dkh), slice(h * dvh, (h + 1) * dvh)
            wlr_h = wlr_ref[:, kc]
            z, live, b, b_last, b_mid = _gla_gates(lr, wlr_h, blr_ref[:, kc], c)
            q = q_ref[:, kc] * qscale
            k = k_ref[:, kc]
            v = v_ref[:, vc].astype(BF16)
            dov = do_ref[:, vc].astype(BF16)
            s0 = st_ref[h]
            ds1 = dS[h]
            ds1b = ds1.astype(BF16)
            e_b, e_lb = jnp.exp(b), jnp.exp(b_last - b)
            e_bm, e_mb = jnp.exp(b - b_mid), jnp.exp(b_mid - b)
            e_last = jnp.exp(b_last)
            qb, kb, qt, kt = q * e_b, k * e_lb, q * e_bm, k * e_mb
            qbb, kbb, qtb, ktb = qb.astype(BF16), kb.astype(BF16), qt.astype(BF16), kt.astype(BF16)
            a = jnp.where(lower, _dot_nt(qtb, ktb), 0.0).astype(BF16)
            da = jnp.where(lower, _dot_nt(dov, v), 0.0).astype(BF16)

            dqb = _dot(dov, s0.astype(BF16))
            dqt = _dot(da, ktb)
            dkt = _dot_tn(da, qtb)
            dkb = _dot(v, ds1b)
            keep = live.astype(F32)
            dp_ref[:, D + h * dvh:D + (h + 1) * dvh] = (keep * (_dot_tn(a, dov) + _dot_nt(kbb, ds1b))).astype(BF16)
            dp_ref[:, kc] = (keep * qscale * (dqb * e_b + dqt * e_bm)).astype(BF16)
            dp_ref[:, D // 2 + h * dkh:D // 2 + (h + 1) * dkh] = (keep * (dkb * e_lb + dkt * e_mb)).astype(BF16)

            db = dqb * qb - dkb * kb + dqt * qt - dkt * kt
            db_last = (jnp.sum(dkb * kb, axis=0, keepdims=True)
                       + jnp.sum(ds1 * s0, axis=0, keepdims=True) * e_last)
            db = db + jnp.where(_rows(CHUNK) == CHUNK - 1, db_last, 0.0)
            dlg = jnp.where(live, _tri_sum(_tri(False), db, 2), 0.0)
            dz = dlg * (1.0 / GATE_NORM) / (1.0 + jnp.exp(z))
            dzb = dz.astype(BF16)

            dlr_h = _dot_nt(dzb, wlr_h)
            dlr = dlr_h if dlr is None else dlr + dlr_h
            dwlr_ref[h] += _dot_tn(lr, dzb)
            dblr_ref[h] += jnp.sum(dz, axis=0, keepdims=True)
            dS[h] = e_last * ds1 + _dot_tn(dov, qbb)
        dp_ref[:, 2 * D:3 * D] = dr_ref[...]
        dp_ref[:, 3 * D:] = dlr.astype(BF16)

    row = pl.BlockSpec((CHUNK, D), lambda c: (rev(c), 0))
    return pl.pallas_call(
        body, name="gla_bwd", grid=(n,),
        in_specs=_gla_specs(D, rev) + [pl.BlockSpec((None, HEADS, dvh, dkh), lambda c: (rev(c), 0, 0, 0)), row, row],
        out_specs=[pl.BlockSpec((CHUNK, 3 * D + LR_W), lambda c: (rev(c), 0)),
                   pl.BlockSpec((HEADS, LR_W, dkh), lambda c: (0, 0, 0)),
                   pl.BlockSpec((HEADS, 1, dkh), lambda c: (0, 0, 0))],
        out_shape=[jax.ShapeDtypeStruct((M, 3 * D + LR_W), BF16),
                   jax.ShapeDtypeStruct((HEADS, LR_W, dkh), F32), jax.ShapeDtypeStruct((HEADS, 1, dkh), F32)],
        scratch_shapes=[pltpu.VMEM((HEADS, dvh, dkh), F32)],
        compiler_params=_params("arbitrary"),
    )(proj, proj, proj, proj, wlr, blr, st, do, dr)


def gla_post_fwd(o, proj, head_norm, D):
    M = o.shape[0]
    dvh = D // HEADS
    tr = _tile(M, ROW_TILE)

    def body(o_ref, r_ref, hn_ref, out_ref):
        for hd in range(HEADS):
            cols = slice(hd * dvh, (hd + 1) * dvh)
            ov = o_ref[:, cols]
            rs = lax.rsqrt(jnp.mean(ov * ov, axis=-1, keepdims=True) + EPS)
            rv = r_ref[:, cols]
            out_ref[:, cols] = (ov * rs * hn_ref[...] * (rv * _sigmoid(rv))).astype(BF16)

    row = pl.BlockSpec((tr, D), lambda i: (i, 0))
    return pl.pallas_call(
        body, name="gla_post_fwd", grid=(M // tr,),
        in_specs=[row, pl.BlockSpec((tr, D), lambda i: (i, 2)), pl.BlockSpec((1, dvh), lambda i: (0, 0))],
        out_specs=row, out_shape=jax.ShapeDtypeStruct((M, D), BF16),
        compiler_params=_params("parallel"),
    )(o, proj, head_norm)


def gla_post_bwd(dgated, o, proj, head_norm, D):
    M = o.shape[0]
    dvh = D // HEADS
    tr = _tile(M, ROW_TILE)

    def body(dg_ref, o_ref, r_ref, hn_ref, do_ref, dr_ref, dhn_ref):
        @pl.when(pl.program_id(0) == 0)
        def _():
            dhn_ref[...] = jnp.zeros_like(dhn_ref)

        hn = hn_ref[...]
        dhn = jnp.zeros((1, dvh), F32)
        for hd in range(HEADS):
            cols = slice(hd * dvh, (hd + 1) * dvh)
            ov = o_ref[:, cols]
            rs = lax.rsqrt(jnp.mean(ov * ov, axis=-1, keepdims=True) + EPS)
            ohat = ov * rs
            rv = r_ref[:, cols]
            s = _sigmoid(rv)
            dgv = dg_ref[:, cols]
            don = dgv * (rv * s)
            dr_ref[:, cols] = (dgv * ohat * hn * (s * (1.0 + rv * (1.0 - s)))).astype(BF16)
            gd = don * hn
            do_ref[:, cols] = rs * (gd - ohat * jnp.mean(gd * ohat, axis=-1, keepdims=True))
            dhn = dhn + jnp.sum(don * ohat, axis=0, keepdims=True)
        dhn_ref[...] += dhn

    row = pl.BlockSpec((tr, D), lambda i: (i, 0))
    vec = pl.BlockSpec((1, dvh), lambda i: (0, 0))
    return pl.pallas_call(
        body, name="gla_post_bwd", grid=(M // tr,),
        in_specs=[row, row, pl.BlockSpec((tr, D), lambda i: (i, 2)), vec],
        out_specs=[row, row, vec],
        out_shape=[jax.ShapeDtypeStruct((M, D), F32), jax.ShapeDtypeStruct((M, D), BF16),
                   jax.ShapeDtypeStruct((1, dvh), F32)],
        compiler_params=_params("arbitrary"),
    )(dgated, o, proj, head_norm)


def _pool_counts(M, g):
    t = _rows(M) - PAD
    win = jnp.left_shift(2, g)
    return t >= 0, jnp.maximum(jnp.minimum(t + 1, win), 1).astype(F32)


def _window_sum(x, g, M, back):
    sums = []
    s = x
    for lvl in range(4):
        sh = 1 << lvl
        s = s + pltpu.roll(s, (M - sh) if back else sh, 0)
        sums.append(s)
    return jnp.where(g == 0, sums[0], jnp.where(g == 1, sums[1], jnp.where(g == 2, sums[2], sums[3])))


POOL_COLS = 128


def pool_window(hp):
    M, D = hp.shape
    cw = min(POOL_COLS, D // 4)
    per_group = (D // 4) // cw

    def body(h_ref, p_ref):
        g = pl.program_id(0) // per_group
        live, cnt = _pool_counts(M, g)
        hv = h_ref[...]
        p_ref[...] = jnp.where(live, _window_sum(hv, g, M, False) / cnt - hv, 0.0).astype(BF16)

    col = pl.BlockSpec((M, cw), lambda j: (0, j))
    return pl.pallas_call(
        body, name="pool_window", grid=(D // cw,), in_specs=[col], out_specs=col,
        out_shape=jax.ShapeDtypeStruct((M, D), BF16), compiler_params=_params("parallel"),
    )(hp)


def pool_window_bwd(dpooled):
    M, D = dpooled.shape
    cw = min(POOL_COLS, D // 4)
    per_group = (D // 4) // cw

    def body(d_ref, o_ref):
        g = pl.program_id(0) // per_group
        live, cnt = _pool_counts(M, g)
        dv = jnp.where(live, d_ref[...], 0.0)
        o_ref[...] = jnp.where(live, _window_sum(dv / cnt, g, M, True) - dv, 0.0)

    col = pl.BlockSpec((M, cw), lambda j: (0, j))
    return pl.pallas_call(
        body, name="pool_window_bwd", grid=(D // cw,), in_specs=[col], out_specs=col,
        out_shape=jax.ShapeDtypeStruct((M, D), F32), compiler_params=_params("parallel"),
    )(dpooled)


def pool_mix(pooled, x, w, bias, scale):
    M, D = x.shape
    W = D // 4
    tm = _tile(M, 352)

    def body(p_ref, x_ref, w_ref, b_ref, s_ref, out_ref):
        live = (_rows(tm) + pl.program_id(1) * tm) >= PAD
        y = (_dot(p_ref[...], w_ref[...]) + b_ref[...]) * s_ref[...]
        out_ref[...] = x_ref[...] + jnp.where(live, y, 0.0)

    blk = pl.BlockSpec((tm, W), lambda g, i: (i, g))
    vec = pl.BlockSpec((1, W), lambda g, i: (0, g))
    return pl.pallas_call(
        body, name="pool_mix", grid=(4, M // tm),
        in_specs=[blk, blk, pl.BlockSpec((None, W, W), lambda g, i: (g, 0, 0)), vec, vec],
        out_specs=blk, out_shape=jax.ShapeDtypeStruct((M, D), F32),
        compiler_params=_params("parallel", "parallel"),
    )(pooled, x, w, bias, scale)


def pool_mix_bwd(dy, pooled, w, bias, scale, dep=None):
    M, D = dy.shape
    W = D // 4
    tm = _tile(M, 352)

    def body(dy_ref, p_ref, w_ref, b_ref, s_ref, *rest):
        dp_ref, dw_ref, db_ref, ds_ref, acc_w = rest[-5:]
        i = pl.program_id(1)

        @pl.when(i == 0)
        def _():
            acc_w[...] = jnp.zeros_like(acc_w)
            db_ref[...] = jnp.zeros_like(db_ref)
            ds_ref[...] = jnp.zeros_like(ds_ref)

        live = (_rows(tm) + i * tm) >= PAD
        dyv = jnp.where(live, dy_ref[...], 0.0)
        pooled = p_ref[...]
        wv = w_ref[...]
        ds_ref[...] += jnp.sum(dyv * (_dot(pooled, wv) + b_ref[...]), axis=0, keepdims=True)
        dys = dyv * s_ref[...]
        db_ref[...] += jnp.sum(dys, axis=0, keepdims=True)
        dysb = dys.astype(BF16)
        acc_w[...] += _dot_tn(pooled, dysb)
        dp_ref[...] = _dot_nt(dysb, wv)

        @pl.when(i == pl.num_programs(1) - 1)
        def _():
            dw_ref[...] = acc_w[...].astype(BF16)

    blk = pl.BlockSpec((tm, W), lambda g, i: (i, g))
    vec = pl.BlockSpec((1, W), lambda g, i: (0, g))
    wspec = pl.BlockSpec((None, W, W), lambda g, i: (g, 0, 0))
    return pl.pallas_call(
        body, name="pool_mix_bwd", grid=(4, M // tm),
        in_specs=[blk, blk, wspec, vec, vec] + _dep_specs(dep),
        out_specs=[blk, wspec, vec, vec],
        out_shape=[jax.ShapeDtypeStruct((M, D), F32), jax.ShapeDtypeStruct((4, W, W), BF16),
                   jax.ShapeDtypeStruct((1, D), F32), jax.ShapeDtypeStruct((1, D), F32)],
        scratch_shapes=[pltpu.VMEM((W, W), F32)],
        compiler_params=_params("parallel", "arbitrary"),
    )(dy, pooled, w, bias, scale, *_dep_args(dep))


def adamw(w, g, m, v, name, copy_g=False):
    shape = w.shape
    C = shape[-1]
    R = w.size // C
    tr = _tile(R, 256, 8)
    tc = C
    if tr == R and R > 256:
        tc = _tile(C, 256, 128)

    def body(w_ref, g_ref, m_ref, v_ref, d_ref, nm_ref, nv_ref, *g_out):
        gv = g_ref[...]
        for ref in g_out:
            ref[...] = gv
        nm = ADAM_B1 * m_ref[...] + (1.0 - ADAM_B1) * gv
        nv = ADAM_B2 * v_ref[...] + (1.0 - ADAM_B2) * (gv * gv)
        m_hat = nm / (1.0 - ADAM_B1 ** ADAM_STEP)
        v_hat = nv / (1.0 - ADAM_B2 ** ADAM_STEP)
        d_ref[...] = -ADAM_LR * (m_hat / (jnp.sqrt(v_hat) + ADAM_EPS) + ADAM_WD * w_ref[...])
        nm_ref[...] = nm
        nv_ref[...] = nv

    spec = pl.BlockSpec((tr, tc), lambda i, j: (i, j))
    outs = pl.pallas_call(
        body, name=name, grid=(R // tr, C // tc),
        in_specs=[spec] * 4, out_specs=[spec] * (3 + copy_g),
        out_shape=[jax.ShapeDtypeStruct((R, C), F32)] * (3 + copy_g),
        compiler_params=_params("parallel", "parallel"),
    )(*[t.reshape(R, C) for t in (w, g, m, v)])
    return [t.reshape(shape) for t in outs]


def add_sibling(grad, recv, core, name):
    _, _, Rh, C = grad.shape
    tr = _tile(Rh, 512)

    def body(core_ref, g_ref, r_ref, o_ref):
        o_ref[...] = (g_ref[...].astype(F32) + r_ref[...].astype(F32)).astype(BF16)

    return pl.pallas_call(
        body, name=name,
        grid_spec=pltpu.PrefetchScalarGridSpec(
            num_scalar_prefetch=1, grid=(N_CHIPS, Rh // tr),
            in_specs=[pl.BlockSpec((None, None, tr, C), lambda j, i, core_ref: (j, core_ref[0], i, 0)),
                      pl.BlockSpec((None, tr, C), lambda j, i, core_ref: (j, i, 0))],
            out_specs=pl.BlockSpec((None, tr, C), lambda j, i, core_ref: (j, i, 0))),
        out_shape=jax.ShapeDtypeStruct((N_CHIPS, Rh, C), BF16),
        compiler_params=_params("parallel", "parallel"),
    )(core, grad, recv)


def add_chips(part, recv, chip, core, group, n, mi, name):
    _, Rh, C = part.shape
    tr = _tile(Rh, 512)

    def body(chip_ref, core_ref, p_ref, r_ref, *rest):
        o_ref = rest[-1]
        acc = p_ref[...].astype(F32)
        for k in range(N_CHIPS - 1):
            acc = acc + r_ref[k].astype(F32)
        o_ref[...] = acc

    carried = [] if group is None else [group]
    return pl.pallas_call(
        body, name=name,
        grid_spec=pltpu.PrefetchScalarGridSpec(
            num_scalar_prefetch=2, grid=(Rh // tr,),
            in_specs=[pl.BlockSpec((None, tr, C), lambda i, chip_ref, core_ref: (chip_ref[0], i, 0)),
                      pl.BlockSpec((N_CHIPS - 1, tr, C), lambda i, chip_ref, core_ref: (0, i, 0))]
            + [ANY] * len(carried),
            out_specs=pl.BlockSpec((None, None, tr, C), lambda i, chip_ref, core_ref: (mi, core_ref[0], i, 0))),
        out_shape=jax.ShapeDtypeStruct((n, 2, Rh, C), F32),
        input_output_aliases={4: 0} if carried else {},
        compiler_params=_params("parallel"),
    )(chip, core, part, recv, *carried)


def stage_shard(shard, mi, chip, name, dep=None):
    _, _, Rh, C = shard.shape
    tr = _tile(Rh, 512)

    def body(chip_ref, s_ref, *rest):
        rest[-1][...] = s_ref[...].astype(BF16)

    return pl.pallas_call(
        body, name=name,
        grid_spec=pltpu.PrefetchScalarGridSpec(
            num_scalar_prefetch=1, grid=(2, Rh // tr),
            in_specs=[pl.BlockSpec((None, None, tr, C), lambda h, i, chip_ref: (mi, h, i, 0))] + _dep_specs(dep),
            out_specs=pl.BlockSpec((None, None, tr, C), lambda h, i, chip_ref: (chip_ref[0], h, i, 0))),
        out_shape=jax.ShapeDtypeStruct((N_CHIPS, 2, Rh, C), BF16),
        compiler_params=_params("parallel", "parallel"),
    )(chip, shard, *_dep_args(dep))


def sum_devices(gathered):
    _, R, C = gathered.shape

    def body(g_ref, o_ref):
        acc = g_ref[0]
        for d in range(1, N_DEV):
            acc = acc + g_ref[d]
        o_ref[...] = acc

    return pl.pallas_call(
        body, name="sum_devices", grid=(1,),
        in_specs=[pl.BlockSpec((N_DEV, R, C), lambda i: (0, 0, 0))],
        out_specs=pl.BlockSpec((R, C), lambda i: (0, 0)),
        out_shape=jax.ShapeDtypeStruct((R, C), F32),
        compiler_params=_params("arbitrary"),
    )(gathered)


def _place():
    x, y, c = lax.axis_index("x"), lax.axis_index("y"), lax.axis_index("c")
    others = [(1 - x, y), (x, 1 - y), (1 - x, 1 - y)]
    return x, y, c, others


def _remote(src, dst, send_sems, recv_sems, idx, device):
    return pltpu.make_async_remote_copy(src_ref=src, dst_ref=dst, send_sem=send_sems.at[idx],
                                        recv_sem=recv_sems.at[idx], device_id=device, device_id_type=MESH)


HBM = pl.BlockSpec(memory_space=pltpu.HBM)
SEM = pl.BlockSpec(memory_space=pltpu.SEMAPHORE)
EFFECT = pltpu.SideEffectType.DATAFLOW_SIDE_EFFECTING


def _in_hbm(t):
    return pltpu.with_memory_space_constraint(t, pltpu.HBM)


def _own_slice(buf, me, c):
    return buf.at[me, c] if len(buf.shape) == 4 else buf.at[me]


def gather_start(staged, bucket_sizes, name):
    n, nb = len(staged), len(bucket_sizes)

    def body(*refs):
        in_refs, sems, token = refs[:n], refs[n:n + 2 * nb], refs[-1]
        x, y, c, others = _place()
        me = 2 * x + y
        t = 0
        for b, size in enumerate(bucket_sizes):
            for i in range(size):
                mine = _own_slice(in_refs[t], me, c)
                for k, chip in enumerate(others):
                    _remote(mine, mine, sems[2 * b], sems[2 * b + 1], 3 * i + k, (*chip, c)).start()
                t += 1
        token[...] = jnp.zeros_like(token)

    sem_shapes = [pltpu.SemaphoreType.DMA((3 * size,)) for size in bucket_sizes for _ in range(2)]
    outs = pl.pallas_call(
        body, name=name,
        out_shape=sem_shapes + [pltpu.HBM(s.shape, s.dtype) for s in staged] + [jax.ShapeDtypeStruct((8, 128), F32)],
        in_specs=[HBM] * n, out_specs=[SEM] * (2 * nb) + [HBM] * n + [pl.BlockSpec(memory_space=pltpu.VMEM)],
        input_output_aliases={t: 2 * nb + t for t in range(n)},
        compiler_params=pltpu.CompilerParams(has_side_effects=EFFECT),
    )(*[_in_hbm(s) for s in staged])
    sems = [(outs[2 * b], outs[2 * b + 1]) for b in range(nb)]
    return sems, list(outs[2 * nb:2 * nb + n]), outs[-1]


def gather_wait(bufs, sems, after, name):
    n = len(bufs)

    def body(*refs):
        in_refs, send_sems, recv_sems = refs[:n], refs[n], refs[n + 1]
        x, y, c, others = _place()
        me = 2 * x + y
        for i in range(n):
            mine = _own_slice(in_refs[i], me, c)
            for k, (ox, oy) in enumerate(others):
                cp = _remote(mine, _own_slice(in_refs[i], 2 * ox + oy, c), send_sems, recv_sems, 3 * i + k,
                             (ox, oy, c))
                cp.wait_send()
                cp.wait_recv()

    return pl.pallas_call(
        body, name=name, out_shape=[pltpu.HBM(b.shape, b.dtype) for b in bufs],
        in_specs=[HBM] * n + [SEM, SEM, ANY], out_specs=[HBM] * n,
        input_output_aliases={t: t for t in range(n)},
        compiler_params=pltpu.CompilerParams(has_side_effects=EFFECT),
    )(*bufs, *sems, after)


def forward_to_sibling(bufs, name):
    n = len(bufs)

    def body(*refs):
        out_refs, (send_sems, recv_sems) = refs[n:2 * n], refs[2 * n:]
        x, y, c, others = _place()
        sibling = (x, y, 1 - c)
        copies = []
        for t in range(n):
            for k, (ox, oy) in enumerate(others):
                mine = out_refs[t].at[2 * ox + oy, c]
                cp = _remote(mine, mine, send_sems, recv_sems, 3 * t + k, sibling)
                cp.start()
                copies.append(cp)
        for t in range(n):
            for k, (ox, oy) in enumerate(others):
                theirs = out_refs[t].at[2 * ox + oy, 1 - c]
                _remote(theirs, theirs, send_sems, recv_sems, 3 * t + k, sibling).wait_recv()
        for cp in copies:
            cp.wait_send()

    return pl.pallas_call(
        body, name=name, in_specs=[ANY] * n, out_specs=[ANY] * n,
        out_shape=[jax.ShapeDtypeStruct(b.shape, b.dtype) for b in bufs],
        input_output_aliases={t: t for t in range(n)},
        scratch_shapes=[pltpu.SemaphoreType.DMA((3 * n,)), pltpu.SemaphoreType.DMA((3 * n,))],
    )(*bufs)


def sibling_start(grads, name):
    n = len(grads)
    lands = [lax.empty((N_CHIPS,) + g.shape[2:], g.dtype) for g in grads]

    def body(*refs):
        in_refs, land_refs, send_sems, recv_sems, token = refs[:n], refs[n:2 * n], refs[2 * n], refs[2 * n + 1], refs[-1]
        x, y, c, _ = _place()
        for t in range(n):
            for j in range(N_CHIPS):
                _remote(in_refs[t].at[j, 1 - c], land_refs[t].at[j], send_sems, recv_sems, N_CHIPS * t + j,
                        (x, y, 1 - c)).start()
        token[...] = jnp.zeros_like(token)

    outs = pl.pallas_call(
        body, name=name,
        out_shape=[pltpu.SemaphoreType.DMA((N_CHIPS * n,))] * 2 + [pltpu.HBM(t.shape, t.dtype) for t in grads + lands]
        + [jax.ShapeDtypeStruct((8, 128), F32)],
        in_specs=[HBM] * (2 * n), out_specs=[SEM, SEM] + [HBM] * (2 * n) + [pl.BlockSpec(memory_space=pltpu.VMEM)],
        input_output_aliases={t: 2 + t for t in range(2 * n)},
        compiler_params=pltpu.CompilerParams(has_side_effects=EFFECT),
    )(*[_in_hbm(t) for t in grads + lands])
    return (outs[0], outs[1]), list(outs[2:2 + n]), list(outs[2 + n:2 + 2 * n]), outs[-1]


def sibling_wait(grads, lands, sems, after, name):
    n = len(grads)

    def body(*refs):
        in_refs, land_refs, send_sems, recv_sems = refs[:n], refs[n:2 * n], refs[2 * n], refs[2 * n + 1]
        x, y, c, _ = _place()
        for t in range(n):
            for j in range(N_CHIPS):
                cp = _remote(in_refs[t].at[j, 1 - c], land_refs[t].at[j], send_sems, recv_sems, N_CHIPS * t + j,
                             (x, y, 1 - c))
                cp.wait_send()
                cp.wait_recv()

    outs = pl.pallas_call(
        body, name=name, out_shape=[pltpu.HBM(t.shape, t.dtype) for t in grads + lands],
        in_specs=[HBM] * (2 * n) + [SEM, SEM, ANY], out_specs=[HBM] * (2 * n),
        input_output_aliases={t: t for t in range(2 * n)},
        compiler_params=pltpu.CompilerParams(has_side_effects=EFFECT),
    )(*grads, *lands, *sems, after)
    return list(outs[:n]), list(outs[n:])


def reduce_start(parts, name):
    n = len(parts)
    lands = [lax.empty((N_CHIPS - 1,) + p.shape[1:], p.dtype) for p in parts]

    def body(*refs):
        in_refs, land_refs, send_sems, recv_sems, token = refs[:n], refs[n:2 * n], refs[2 * n], refs[2 * n + 1], refs[-1]
        x, y, c, others = _place()
        for t in range(n):
            for k, (ox, oy) in enumerate(others):
                _remote(in_refs[t].at[2 * ox + oy], land_refs[t].at[k], send_sems, recv_sems, 3 * t + k,
                        (ox, oy, c)).start()
        token[...] = jnp.zeros_like(token)

    outs = pl.pallas_call(
        body, name=name,
        out_shape=[pltpu.SemaphoreType.DMA((3 * n,))] * 2 + [pltpu.HBM(t.shape, t.dtype) for t in parts + lands]
        + [jax.ShapeDtypeStruct((8, 128), F32)],
        in_specs=[HBM] * (2 * n), out_specs=[SEM, SEM] + [HBM] * (2 * n) + [pl.BlockSpec(memory_space=pltpu.VMEM)],
        input_output_aliases={t: 2 + t for t in range(2 * n)},
        compiler_params=pltpu.CompilerParams(has_side_effects=EFFECT),
    )(*[_in_hbm(t) for t in parts + lands])
    return (outs[0], outs[1]), list(outs[2:2 + n]), list(outs[2 + n:2 + 2 * n]), outs[-1]


def reduce_wait(parts, lands, sems, after, name):
    n = len(parts)

    def body(*refs):
        in_refs, land_refs, send_sems, recv_sems = refs[:n], refs[n:2 * n], refs[2 * n], refs[2 * n + 1]
        x, y, c, others = _place()
        for t in range(n):
            for k, (ox, oy) in enumerate(others):
                cp = _remote(in_refs[t].at[2 * ox + oy], land_refs[t].at[k], send_sems, recv_sems, 3 * t + k,
                             (ox, oy, c))
                cp.wait_send()
                cp.wait_recv()

    outs = pl.pallas_call(
        body, name=name, out_shape=[pltpu.HBM(t.shape, t.dtype) for t in parts + lands],
        in_specs=[HBM] * (2 * n) + [SEM, SEM] + _dep_specs(after), out_specs=[HBM] * (2 * n),
        input_output_aliases={t: t for t in range(2 * n)},
        compiler_params=pltpu.CompilerParams(has_side_effects=EFFECT),
    )(*parts, *lands, *sems, *_dep_args(after))
    return list(outs[:n]), list(outs[n:])


def exchange_halves(groups, name):
    n_groups = len(groups)
    slots = [(gi, mi) for gi, grp in enumerate(groups) for mi in range(grp.shape[0])]

    def body(*refs):
        out_refs = refs[n_groups:2 * n_groups]
        send_sems, recv_sems = refs[2 * n_groups:]
        x, y, c, _ = _place()
        sibling = (x, y, 1 - c)
        copies = []
        for t, (gi, mi) in enumerate(slots):
            mine = out_refs[gi].at[mi, c]
            cp = _remote(mine, mine, send_sems, recv_sems, t, sibling)
            cp.start()
            copies.append(cp)
        for t, (gi, mi) in enumerate(slots):
            theirs = out_refs[gi].at[mi, 1 - c]
            _remote(theirs, theirs, send_sems, recv_sems, t, sibling).wait_recv()
        for cp in copies:
            cp.wait_send()

    return pl.pallas_call(
        body, name=name, in_specs=[ANY] * n_groups, out_specs=[ANY] * n_groups,
        out_shape=[jax.ShapeDtypeStruct(g.shape, g.dtype) for g in groups],
        input_output_aliases={gi: gi for gi in range(n_groups)},
        scratch_shapes=[pltpu.SemaphoreType.DMA((len(slots),)), pltpu.SemaphoreType.DMA((len(slots),))],
    )(*groups)


def gather_devices(buf):
    def body(in_ref, out_ref, send_sems, recv_sems, local_sem):
        x, y, c, _ = _place()
        me = 4 * x + 2 * y + c
        local = pltpu.make_async_copy(in_ref, out_ref.at[me], local_sem)
        local.start()
        copies = []
        for k in range(1, N_DEV):
            fx, fy, fc = (k >> 2) & 1, (k >> 1) & 1, k & 1
            peer = (x ^ fx, y ^ fy, c ^ fc)
            cp = _remote(in_ref, out_ref.at[me], send_sems, recv_sems, k - 1, peer)
            cp.start()
            copies.append(cp)
        for k in range(1, N_DEV):
            fx, fy, fc = (k >> 2) & 1, (k >> 1) & 1, k & 1
            theirs = out_ref.at[4 * (x ^ fx) + 2 * (y ^ fy) + (c ^ fc)]
            _remote(theirs, theirs, send_sems, recv_sems, k - 1, (x, y, c)).wait_recv()
        for cp in copies:
            cp.wait_send()
        local.wait()

    return pl.pallas_call(
        body, name="gather_devices", in_specs=[ANY], out_specs=ANY,
        out_shape=jax.ShapeDtypeStruct((N_DEV,) + buf.shape, buf.dtype),
        scratch_shapes=[pltpu.SemaphoreType.DMA((N_DEV - 1,)), pltpu.SemaphoreType.DMA((N_DEV - 1,)),
                        pltpu.SemaphoreType.DMA],
    )(buf)


class GradReducer:
    def __init__(self, core, chip, kinds):
        self.core, self.chip = core, chip
        self.sizes = dict(kinds)
        self.groups = {kind: None for kind, _ in kinds}

    def send(self, grads, tag):
        arrays = [g.reshape(N_CHIPS, 2, -1, g.shape[-1]) for g, _, _ in grads]
        sems, arrays, lands, token = sibling_start(arrays, f"reduce_sibling_start_{tag}")
        return (sems, arrays, lands, [(kind, mi) for _, kind, mi in grads], tag), token

    def begin(self, sent, after, tag):
        parts, slots = [], []
        for sems, arrays, lands, sent_slots, sent_tag in sent:
            arrays, lands = sibling_wait(arrays, lands, sems, after, f"reduce_sibling_wait_{sent_tag}")
            parts += [add_sibling(g, r, self.core, f"reduce_add_sibling_{sent_tag}_{t}")
                      for t, (g, r) in enumerate(zip(arrays, lands))]
            slots += sent_slots
        sems, parts, lands, token = reduce_start(parts, f"reduce_start_{tag}")
        return (sems, parts, lands, slots, tag), token

    def end(self, state, after):
        sems, parts, lands, slots, tag = state
        parts, lands = reduce_wait(parts, lands, sems, after, f"reduce_wait_{tag}")
        for t, (kind, mi) in enumerate(slots):
            self.groups[kind] = add_chips(parts[t], lands[t], self.chip, self.core, self.groups[kind],
                                          self.sizes[kind], mi, f"reduce_add_chips_{tag}_{t}")

    def finish(self):
        kinds = list(self.groups)
        return dict(zip(kinds, exchange_halves([self.groups[k] for k in kinds], "reduce_swap")))


def _ffn_fwd(x, gain, wg, wu, wd, tag):
    h, rstd = rmsnorm_fwd(x, gain, BF16, f"ffn_norm_{tag}")
    up, silu, dsilu, act = ffn_gateup(h, wg, wu, f"ffn_gateup_{tag}")
    out = mm_residual(act, wd, x, 0.5, wd.shape[0] // N_CHIPS, f"ffn_down_{tag}", tm_target=704)
    return out, (x, gain, h, rstd, up, silu, dsilu, act)


def _ffn_bwd(dout, dy, saved, wg, wu, wd, index, reducer, dep=None, per_tensor=False):
    x, gain, h, rstd, up, silu, dsilu, act = saved
    D = x.shape[1]
    Fs = wg.shape[2]
    td = _tile(D, 1024, 128)
    tag = f"ffn{index}"
    begun = []

    def begin(sent, after, suffix):
        state, token = reducer.begin(sent, after, tag + suffix)
        begun.append(state)
        return token

    dgate, dup = ffn_bwd_act(dy, wd, up, silu, dsilu, f"ffn_bwd_act_{index}", dep=dep)
    d_wd = mm_tn(act, dy, Fs, td, f"ffn_bwd_wd_{index}")
    if per_tensor:
        sent_d, tok = reducer.send([(d_wd, "down", index)], tag + "d")
        d_wg = mm_tn(h, dgate, td, Fs, f"ffn_bwd_wg_{index}", stacked_out=True, dep=tok)
        toks = [begin([sent_d], d_wg, "d")]
        sent_g, tok = reducer.send([(d_wg, "gate", index)], tag + "g")
        d_wu = mm_tn(h, dup, td, Fs, f"ffn_bwd_wu_{index}", stacked_out=True, dep=toks + [tok])
        toks = [begin([sent_g], d_wu, "g")]
        sent, tok = reducer.send([(d_wu, "up", index)], tag + "u")
    else:
        d_wg = mm_tn(h, dgate, td, Fs, f"ffn_bwd_wg_{index}", stacked_out=True)
        d_wu = mm_tn(h, dup, td, Fs, f"ffn_bwd_wu_{index}", stacked_out=True)
        toks = []
        sent, tok = reducer.send([(d_wd, "down", index), (d_wg, "gate", index), (d_wu, "up", index)], tag + "u")
    dh = mm_nt([(dgate, wg), (dup, wu)], _tile(x.shape[0], 704), D, Fs, f"ffn_bwd_dh_{index}", stacked_w=True,
               dep=toks + [tok])
    tok = begin([sent], dh, "u")
    dx, dgain, dx_half = rmsnorm_bwd(dh, x, gain, rstd, dout, f"ffn_norm_bwd_{index}")
    return dx, dx_half, dgain, begun, tok


def kernel(x, meta, ffn_norm, ffn_w_gate, ffn_w_up, ffn_w_down, gla_norm, gla_w_in, gla_w_lr, gla_b_lr, gla_head_norm, gla_w_out, pool_norm, pool_w, pool_b, pool_scale, final_norm, loss_target, m_meta, m_ffn_norm, m_ffn_w_gate, m_ffn_w_up, m_ffn_w_down, m_gla_norm, m_gla_w_in, m_gla_w_lr, m_gla_b_lr, m_gla_head_norm, m_gla_w_out, m_pool_norm, m_pool_w, m_pool_b, m_pool_scale, m_final_norm, v_meta, v_ffn_norm, v_ffn_w_gate, v_ffn_w_up, v_ffn_w_down, v_gla_norm, v_gla_w_in, v_gla_w_lr, v_gla_b_lr, v_gla_head_norm, v_gla_w_out, v_pool_norm, v_pool_w, v_pool_b, v_pool_scale, v_final_norm):
    S, D = x.shape[1], x.shape[2]
    M = OFF + S
    Dq = D // N_CHIPS
    Fs = ffn_w_gate.shape[3]
    F = N_CHIPS * Fs
    dk = D // 2
    n_in = gla_w_in.shape[2]
    W = D // 4
    core = lax.axis_index("c").astype(jnp.int32).reshape(1)
    chip_id = 2 * lax.axis_index("x") + lax.axis_index("y")
    chip = chip_id.astype(jnp.int32).reshape(1)

    small = jnp.concatenate([_pad_rows(t) for t in (
        meta, ffn_norm.reshape(4, Dq), gla_w_lr.reshape(8, Dq), pool_norm, pool_b.reshape(1, Dq), pool_scale)],
        axis=0)
    def stage(w, kind, n, mi, dep=None):
        return stage_shard(w.reshape(n, 2, -1, w.shape[-1]), mi, chip, f"stage_{kind}_{mi}", dep=dep)

    ffn_stage = lambda mi, dep=None: [stage(ffn_w_gate, "gate", 4, mi, dep), stage(ffn_w_up, "up", 4, mi, dep),
                                      stage(ffn_w_down, "down", 4, mi, dep)]
    small_stage = lax.dynamic_update_slice(jnp.zeros((N_CHIPS,) + small.shape, F32), small[None], (chip_id, 0, 0))
    first = ffn_stage(0)
    buckets = [first[:2] + [small_stage], first[2:]]
    sizes = [len(b) for b in buckets]
    gather_sems, in_flight, tok = gather_start([t for b in buckets for t in b], sizes, "gather_start_first")
    buckets = [[stage(gla_w_in, "win", 1, 0, tok), stage(gla_w_out, "wout", 1, 0, tok)],
               ffn_stage(1, tok), ffn_stage(2, tok), [stage(pool_w, "pool", 1, 0, tok)] + ffn_stage(3, tok)]
    more_sems, more_in_flight, gather_token = gather_start([t for b in buckets for t in b],
                                                            [len(b) for b in buckets], "gather_start_rest")
    sizes += [len(b) for b in buckets]
    gather_sems += more_sems
    in_flight += more_in_flight
    starts = [sum(sizes[:b]) for b in range(len(sizes))]

    def arrive(b, after, n_big):
        bufs = gather_wait(in_flight[starts[b]:starts[b] + sizes[b]], gather_sems[b], after, f"gather_wait_{b}")
        return forward_to_sibling(bufs[:n_big], f"gather_forward_{b}") + bufs[n_big:]

    ffn_w = lambda t: (t[0].reshape(N_CHIPS, D, Fs), t[1].reshape(N_CHIPS, D, Fs), t[2].reshape(F, D))
    got = arrive(0, gather_token, 2)
    wg, wu, wd = [None] * 4, [None] * 4, [None] * 4
    wg[0], wu[0] = got[0].reshape(N_CHIPS, D, Fs), got[1].reshape(N_CHIPS, D, Fs)
    sm = got[2]
    unshard = lambda t: t.transpose(1, 0, 2).reshape(t.shape[1], D)
    meta_f = unshard(sm[:, 0:16])
    ffn_norm_f = unshard(sm[:, 16:20])
    w_lr_f = sm[:, 24:32].reshape(N_CHIPS, GATE_RANK, dk // N_CHIPS).transpose(1, 0, 2).reshape(GATE_RANK, dk)
    pool_norm_f = sm[:, 32].reshape(1, D)
    pool_b_f = sm[:, 40].reshape(N_CHIPS, 4, W // N_CHIPS).transpose(1, 0, 2).reshape(1, D)
    pool_scale_f = sm[:, 48].reshape(1, D)
    wlr_pad = jnp.pad(w_lr_f.astype(BF16), ((0, LR_W - GATE_RANK), (0, 0)))
    final_g = final_norm.reshape(1, D)
    qkv = 2 * dk + D

    x0 = jnp.concatenate([jnp.zeros((PAD, D), F32), meta_f, x[0]], axis=0)
    target = jnp.pad(loss_target[0], ((OFF, 0), (0, 0)))
    h0, rstd0 = rmsnorm_fwd(x0, ffn_norm_f[0:1], BF16, "ffn_norm_0")
    acts0 = ffn_gateup(h0, wg[0], wu[0], "ffn_gateup_0")
    wd[0] = arrive(1, acts0[3], 1)[0].reshape(F, D)
    x1 = mm_residual(acts0[3], wd[0], x0, 0.5, Fs, "ffn_down_0", tm_target=704)
    ffn0 = (x0, ffn_norm_f[0:1], h0, rstd0, *acts0)
    got = arrive(2, x1, 2)
    w_in = got[0].reshape(N_CHIPS, D, n_in).transpose(1, 0, 2).reshape(D, N_CHIPS * n_in)
    w_out = got[1].reshape(D, D)
    w_all = jnp.concatenate([w_in[:, :qkv], w_in[:, qkv + GATE_RANK:], w_in[:, qkv:qkv + GATE_RANK],
                             jnp.zeros((D, LR_W - GATE_RANK), BF16)], axis=1)
    hg, rstd_g = rmsnorm_fwd(x1, gla_norm, BF16, "gla_norm")
    proj = mm_nn(hg, w_all, F32, "gla_proj")
    o, st = gla_fwd(proj, wlr_pad, gla_b_lr, D)
    gated = gla_post_fwd(o, proj, gla_head_norm, D)
    x2 = mm_residual(gated, w_out, x1, 1.0, D, "gla_out")
    wg[1], wu[1], wd[1] = ffn_w(arrive(3, x2, 3))
    x3, ffn1 = _ffn_fwd(x2, ffn_norm_f[1:2], wg[1], wu[1], wd[1], "1")
    wg[2], wu[2], wd[2] = ffn_w(arrive(4, x3, 3))
    x4, ffn2 = _ffn_fwd(x3, ffn_norm_f[2:3], wg[2], wu[2], wd[2], "2")
    got = arrive(5, x4, 4)
    w_pool = got[0].reshape(N_CHIPS, 4, W // N_CHIPS, W).transpose(1, 0, 2, 3).reshape(4, W, W)
    wg[3], wu[3], wd[3] = ffn_w(got[1:])
    hp, rstd_p = rmsnorm_fwd(x4, pool_norm_f, F32, "pool_norm")
    pooled = pool_window(hp)
    x5 = pool_mix(pooled, x4, w_pool, pool_b_f, pool_scale_f)
    x6, ffn3 = _ffn_fwd(x5, ffn_norm_f[3:4], wg[3], wu[3], wd[3], "3")
    loss, dx6, d_final, dy6 = final_loss(x6, final_g, target)

    reducer = GradReducer(core, chip, [("gate", 4), ("up", 4), ("down", 4), ("win", 1), ("wout", 1), ("pool", 1)])

    def settle(begun, after):
        for state in begun:
            reducer.end(state, after)

    dx5, _, dn3, red3, tok = _ffn_bwd(dx6, dy6, ffn3, wg[3], wu[3], wd[3], 3, reducer)
    dpooled, d_wpool, d_pool_b, d_pool_scale = pool_mix_bwd(dx5, pooled, w_pool, pool_b_f, pool_scale_f, dep=tok)
    dhp = pool_window_bwd(dpooled)
    dx4, d_pool_norm, dy4 = rmsnorm_bwd(dhp, x4, pool_norm_f, rstd_p, dx5, "pool_norm_bwd")
    d_wpool = d_wpool.reshape(4, N_CHIPS, W // N_CHIPS, W).transpose(1, 0, 2, 3)
    sent_p, tok = reducer.send([(d_wpool, "pool", 0)], "pool")
    dx3, dy3, dn2, red2, tok = _ffn_bwd(dx4, dy4, ffn2, wg[2], wu[2], wd[2], 2, reducer, dep=tok)
    redp, tok_p = reducer.begin([sent_p], dx3, "pool")
    dx2, _, dn1, red1, tok = _ffn_bwd(dx3, dy3, ffn1, wg[1], wu[1], wd[1], 1, reducer, dep=[tok, tok_p])
    tm = _tile(M, 352)
    td = _tile(D, 512, 128)
    d_wout = mm_tn(gated, dx2, td, td, "gla_out_bwd_w", dep=tok)
    sent_o, tok = reducer.send([(d_wout, "wout", 0)], "wout")
    dgated = mm_nt([(dx2, w_out)], tm, td, D, "gla_out_bwd_act", dep=tok)
    redo, tok_o = reducer.begin([sent_o], dgated, "wout")
    do, dr, d_head_norm = gla_post_bwd(dgated, o, proj, gla_head_norm, D)
    dproj, dwlr, dblr = gla_bwd(proj, wlr_pad, gla_b_lr, st, do, dr, D)
    tp = _tile(proj.shape[1], 896, 128)
    d_wall = mm_tn(hg, dproj, td, tp, "gla_proj_bwd_w", dep=tok_o)
    d_win = jnp.concatenate([d_wall[:, :qkv], d_wall[:, qkv + D:qkv + D + GATE_RANK], d_wall[:, qkv:qkv + D]], axis=1)
    d_win = d_win.reshape(D, N_CHIPS, n_in).transpose(1, 0, 2)
    sent_i, tok = reducer.send([(d_win, "win", 0)], "win")
    dhg = mm_nt([(dproj, w_all)], _tile(M, 704), D, tp, "gla_proj_bwd_act", dep=tok)
    redi, tok = reducer.begin([sent_i], dhg, "win")
    dx1, d_gla_norm, dy1 = rmsnorm_bwd(dhg, x1, gla_norm, rstd_g, dx2, "gla_norm_bwd")
    dx0, _, dn0, red0, tok = _ffn_bwd(dx1, dy1, ffn0, wg[0], wu[0], wd[0], 0, reducer, dep=tok, per_tensor=True)
    settle(red3 + [redp] + red2 + red1 + [redo, redi] + red0[:-1], tok)

    d_wlr = dwlr[:, :GATE_RANK].transpose(1, 0, 2).reshape(GATE_RANK, dk)
    pieces = [dx0[PAD:OFF], dn0, dn1, dn2, dn3, d_gla_norm, d_wlr,
              dblr.reshape(1, dk), d_head_norm, d_pool_norm, d_pool_b, d_pool_scale, d_final]
    packed = jnp.concatenate([_pad_rows(p.reshape(-1, Dq)) for p in pieces], axis=0)
    total = sum_devices(gather_devices(packed))

    settle(red0[-1:], [total] + list(reducer.groups.values()))
    reduced = reducer.finish()
    g_gate = reduced["gate"].reshape(ffn_w_gate.shape)
    g_up = reduced["up"].reshape(ffn_w_up.shape)
    g_down = reduced["down"].reshape(ffn_w_down.shape)
    g_win = reduced["win"].reshape(gla_w_in.shape)
    g_wout = reduced["wout"].reshape(gla_w_out.shape)
    g_wpool = reduced["pool"].reshape(pool_w.shape)
    sums, at = [], 0
    for p in pieces:
        r = p.size // Dq
        sums.append(total[at:at + r].reshape(p.shape))
        at += r + (-r % 8)
    (s_meta, s_n0, s_n1, s_n2, s_n3, s_gla_norm, s_wlr, s_blr, s_head_norm, s_pool_norm, s_pool_b, s_pool_scale,
     s_final) = sums
    s_ffn_norm = jnp.stack([s_n0, s_n1, s_n2, s_n3], axis=0)[:, 0]
    mine = lambda t, width: lax.dynamic_slice_in_dim(t, chip_id * width, width, axis=t.ndim - 1)
    g_meta = mine(s_meta, Dq)
    g_ffn_norm = mine(s_ffn_norm, Dq).reshape(ffn_norm.shape)
    g_gla_norm = s_gla_norm
    g_wlr = mine(s_wlr, dk // N_CHIPS).reshape(gla_w_lr.shape)
    g_blr = s_blr
    g_head_norm = s_head_norm
    g_pool_norm = mine(s_pool_norm, Dq)
    g_pool_b = mine(s_pool_b.reshape(4, W), W // N_CHIPS).reshape(pool_b.shape)
    g_pool_scale = mine(s_pool_scale, Dq)
    g_final = s_final.reshape(final_norm.shape)

    weights = [meta, ffn_norm, ffn_w_gate, ffn_w_up, ffn_w_down, gla_norm, gla_w_in, gla_w_lr, gla_b_lr,
               gla_head_norm, gla_w_out, pool_norm, pool_w, pool_b, pool_scale, final_norm]
    moments_m = [m_meta, m_ffn_norm, m_ffn_w_gate, m_ffn_w_up, m_ffn_w_down, m_gla_norm, m_gla_w_in, m_gla_w_lr,
                 m_gla_b_lr, m_gla_head_norm, m_gla_w_out, m_pool_norm, m_pool_w, m_pool_b, m_pool_scale,
                 m_final_norm]
    moments_v = [v_meta, v_ffn_norm, v_ffn_w_gate, v_ffn_w_up, v_ffn_w_down, v_gla_norm, v_gla_w_in, v_gla_w_lr,
                 v_gla_b_lr, v_gla_head_norm, v_gla_w_out, v_pool_norm, v_pool_w, v_pool_b, v_pool_scale,
                 v_final_norm]
    grads_w = [g_meta, g_ffn_norm, g_gate, g_up, g_down, g_gla_norm, g_win, g_wlr, g_blr, g_head_norm, g_wout,
               g_pool_norm, g_wpool, g_pool_b, g_pool_scale, g_final]
    from_swap = {2, 3, 4, 6, 10, 12}
    deltas, new_m, new_v = [], [], []
    for i, (w, g, m, v) in enumerate(zip(weights, grads_w, moments_m, moments_v)):
        outs = adamw(w, g, m, v, f"adamw_{i}", copy_g=i in from_swap)
        deltas.append(outs[0])
        new_m.append(outs[1])
        new_v.append(outs[2])
        if i in from_swap:
            grads_w[i] = outs[3]

    loss = lax.psum(loss[0, 0], ("x", "y", "c"))
    grad_x = dx0[OFF:][None]
    return (loss, grad_x, *grads_w, *deltas, *new_m, *new_v)
```
